```python
import math
import jax, jax.numpy as jnp
from jax import lax
import numpy as np

D_MODEL = 2048
BATCH = 8
SEQ = 8192
DEPTH = 1

MIX_WIDTH = D_MODEL
CONV_WIDTH = MIX_WIDTH // 2
CONV_GROUPS = 8
MLA_HEADS = 8
QK_NOPE_DIM = 128
QK_ROPE_DIM = 64
V_HEAD_DIM = 128
Q_LORA_RANK = 768
KV_LORA_RANK = 512
ROPE_THETA = 10000.0
Q_BLOCK = 128
D_FF = 5632
CONV_K = 3
RMS_EPS = 1e-6
N_MOD = 6

IN_SPLITS = (Q_LORA_RANK, KV_LORA_RANK, QK_ROPE_DIM, CONV_WIDTH, CONV_WIDTH, CONV_WIDTH)
IN_COLS = sum(IN_SPLITS)

kernel_name = "hybrid_mla_shortconv_convffn_adaln"


def rms_norm(x, g):
    xf = x.astype(jnp.float32)
    y = xf * lax.rsqrt(jnp.mean(xf * xf, axis=-1, keepdims=True) + RMS_EPS)
    return (y * g.astype(jnp.float32)).astype(x.dtype)


def rope(x, cos, sin):
    x1, x2 = jnp.split(x, 2, axis=-1)
    return jnp.concatenate([x1 * cos - x2 * sin, x2 * cos + x1 * sin], axis=-1)


def causal_dwconv3(u, w, b):
    s = u.shape[1]
    up = jnp.pad(u, ((0, 0), (CONV_K - 1, 0), (0, 0)))
    return up[:, :s] * w[0] + up[:, 1:s + 1] * w[1] + u * w[2] + b


def mla_attention(q_nope, q_rope, k_nope, k_rope, v):
    b, s, h, _ = q_nope.shape
    nb = s // Q_BLOCK
    scale = 1.0 / math.sqrt(QK_NOPE_DIM + QK_ROPE_DIM)
    k_idx = jnp.arange(s)
    neg = jnp.finfo(jnp.float32).min

    def blockify(t):
        return t.reshape(b, nb, Q_BLOCK, *t.shape[2:]).swapaxes(0, 1)

    def one_block(args):
        qn, qr, i = args
        sc = (jnp.einsum('bqhd,bkhd->bhqk', qn, k_nope)
              + jnp.einsum('bqhd,bkd->bhqk', qr, k_rope)).astype(jnp.float32) * scale
        q_idx = i * Q_BLOCK + jnp.arange(Q_BLOCK)
        mask = k_idx[None, :] <= q_idx[:, None]
        sc = jnp.where(mask, sc, neg)
        p = jax.nn.softmax(sc, axis=-1).astype(v.dtype)
        return jnp.einsum('bhqk,bkhd->bqhd', p, v)

    out = lax.map(one_block, (blockify(q_nope), blockify(q_rope), jnp.arange(nb)))
    return out.swapaxes(0, 1).reshape(b, s, h * V_HEAD_DIM)


def _fwd_setup_inputs(seed: int = 0) -> dict:
    key = jax.random.key(seed)
    ks = jax.random.split(key, 24)
    f32 = jnp.float32

    def nrm(k, shape, fan_in):
        return jax.random.normal(k, shape, f32) * (fan_in ** -0.5)

    def gain(k, n):
        return 1.0 + 0.02 * jax.random.normal(k, (DEPTH, n), f32)

    x = jax.random.normal(ks[0], (BATCH, SEQ, D_MODEL), f32)
    c = jax.random.normal(ks[1], (BATCH, D_MODEL), f32)
    offset = jax.random.randint(ks[2], (BATCH, 1), 0, 1024, dtype=jnp.int32)
    positions = offset + jnp.arange(SEQ, dtype=jnp.int32)[None, :]
    return {
        "x": x,
        "c": c,
        "positions": positions,
        "w_ada": nrm(ks[3], (DEPTH, D_MODEL, N_MOD * D_MODEL), D_MODEL),
        "b_ada": 0.02 * jax.random.normal(ks[4], (DEPTH, N_MOD * D_MODEL), f32),
        "g_pre_mix": gain(ks[5], D_MODEL),
        "g_post_mix": gain(ks[6], D_MODEL),
        "w_in": nrm(ks[7], (DEPTH, D_MODEL, IN_COLS), D_MODEL),
        "g_q": gain(ks[8], Q_LORA_RANK),
        "w_uq": nrm(ks[9], (DEPTH, Q_LORA_RANK, MLA_HEADS * (QK_NOPE_DIM + QK_ROPE_DIM)), Q_LORA_RANK),
        "g_kv": gain(ks[10], KV_LORA_RANK),
        "w_ukv": nrm(ks[11], (DEPTH, KV_LORA_RANK, MLA_HEADS * (QK_NOPE_DIM + V_HEAD_DIM)), KV_LORA_RANK),
        "conv_w_mix": nrm(ks[12], (DEPTH, CONV_K, CONV_WIDTH), CONV_K),
        "conv_b_mix": 0.02 * jax.random.normal(ks[13], (DEPTH, CONV_WIDTH), f32),
        "w_o": nrm(ks[14], (DEPTH, MIX_WIDTH, D_MODEL), MIX_WIDTH),
        "g_pre_ffn": gain(ks[15], D_MODEL),
        "g_post_ffn": gain(ks[16], D_MODEL),
        "w_up": nrm(ks[17], (DEPTH, D_MODEL, 2 * D_FF), D_MODEL),
        "conv_w_ffn": nrm(ks[18], (DEPTH, CONV_K, 2 * D_FF), CONV_K),
        "conv_b_ffn": 0.02 * jax.random.normal(ks[19], (DEPTH, 2 * D_FF), f32),
        "w_down": nrm(ks[20], (DEPTH, D_FF, D_MODEL), D_FF),
    }


def _fwd_reference(x, c, positions, w_ada, b_ada, g_pre_mix, g_post_mix, w_in, g_q, w_uq,
              g_kv, w_ukv, conv_w_mix, conv_b_mix, w_o, g_pre_ffn, g_post_ffn,
              w_up, conv_w_ffn, conv_b_ffn, w_down):
    b, s, _ = x.shape
    inv_freq = 1.0 / (ROPE_THETA ** (jnp.arange(0, QK_ROPE_DIM, 2, dtype=jnp.float32) / QK_ROPE_DIM))
    ang = positions.astype(jnp.float32)[..., None] * inv_freq
    cos = jnp.cos(ang).astype(x.dtype)
    sin = jnp.sin(ang).astype(x.dtype)
    c_act = jax.nn.silu(c)
    cut = np.cumsum(IN_SPLITS)[:-1].tolist()

    for l in range(DEPTH):
        mod = c_act @ w_ada[l] + b_ada[l]
        sh_m, sc_m, gt_m, sh_f, sc_f, gt_f = [m[:, None, :] for m in jnp.split(mod, N_MOD, axis=-1)]

        h = rms_norm(x, g_pre_mix[l]) * (1.0 + sc_m) + sh_m
        proj = h @ w_in[l]
        q_lat, kv_lat, k_rope, gate_b, gate_c, conv_in = jnp.split(proj, cut, axis=-1)

        q = (rms_norm(q_lat, g_q[l]) @ w_uq[l]).reshape(b, s, MLA_HEADS, QK_NOPE_DIM + QK_ROPE_DIM)
        q_nope, q_rope = q[..., :QK_NOPE_DIM], q[..., QK_NOPE_DIM:]
        q_rope = rope(q_rope, cos[:, :, None, :], sin[:, :, None, :])
        k_rope = rope(k_rope, cos, sin)
        kv = (rms_norm(kv_lat, g_kv[l]) @ w_ukv[l]).reshape(b, s, MLA_HEADS, QK_NOPE_DIM + V_HEAD_DIM)
        k_nope, v = kv[..., :QK_NOPE_DIM], kv[..., QK_NOPE_DIM:]
        attn_out = mla_attention(q_nope, q_rope, k_nope, k_rope, v)

        conv_out = gate_b * causal_dwconv3(gate_c * conv_in, conv_w_mix[l], conv_b_mix[l])

        mix = jnp.concatenate([attn_out, conv_out], axis=-1) @ w_o[l]
        x = x + gt_m * rms_norm(mix, g_post_mix[l])

        h = rms_norm(x, g_pre_ffn[l]) * (1.0 + sc_f) + sh_f
        u = causal_dwconv3(h @ w_up[l], conv_w_ffn[l], conv_b_ffn[l])
        a, g = jnp.split(u, 2, axis=-1)
        y = (jax.nn.silu(g) * a) @ w_down[l]
        x = x + gt_f * rms_norm(y, g_post_ffn[l])
    return x


import jax as _jax
import jax.numpy as _jnp

TWIN_FORMAT = 'train_step'
FWD_PARAMS = ['x', 'c', 'positions', 'w_ada', 'b_ada', 'g_pre_mix', 'g_post_mix', 'w_in', 'g_q', 'w_uq', 'g_kv', 'w_ukv', 'conv_w_mix', 'conv_b_mix', 'w_o', 'g_pre_ffn', 'g_post_ffn', 'w_up', 'conv_w_ffn', 'conv_b_ffn', 'w_down']
TWIN_WEIGHTS = ['w_ada', 'b_ada', 'g_pre_mix', 'g_post_mix', 'w_in', 'g_q', 'w_uq', 'g_kv', 'w_ukv', 'conv_w_mix', 'conv_b_mix', 'w_o', 'g_pre_ffn', 'g_post_ffn', 'w_up', 'conv_w_ffn', 'conv_b_ffn', 'w_down']
TWIN_DIFF_INPUT = 'x'
TWIN_INPUTS = ['x', 'c', 'positions', 'w_ada', 'b_ada', 'g_pre_mix', 'g_post_mix', 'w_in', 'g_q', 'w_uq', 'g_kv', 'w_ukv', 'conv_w_mix', 'conv_b_mix', 'w_o', 'g_pre_ffn', 'g_post_ffn', 'w_up', 'conv_w_ffn', 'conv_b_ffn', 'w_down', 'loss_target', 'm_w_ada', 'm_b_ada', 'm_g_pre_mix', 'm_g_post_mix', 'm_w_in', 'm_g_q', 'm_w_uq', 'm_g_kv', 'm_w_ukv', 'm_conv_w_mix', 'm_conv_b_mix', 'm_w_o', 'm_g_pre_ffn', 'm_g_post_ffn', 'm_w_up', 'm_conv_w_ffn', 'm_conv_b_ffn', 'm_w_down', 'v_w_ada', 'v_b_ada', 'v_g_pre_mix', 'v_g_post_mix', 'v_w_in', 'v_g_q', 'v_w_uq', 'v_g_kv', 'v_w_ukv', 'v_conv_w_mix', 'v_conv_b_mix', 'v_w_o', 'v_g_pre_ffn', 'v_g_post_ffn', 'v_w_up', 'v_conv_w_ffn', 'v_conv_b_ffn', 'v_w_down']
TWIN_OUTPUTS = ['loss', 'grad_x', 'grad_w_ada', 'grad_b_ada', 'grad_g_pre_mix', 'grad_g_post_mix', 'grad_w_in', 'grad_g_q', 'grad_w_uq', 'grad_g_kv', 'grad_w_ukv', 'grad_conv_w_mix', 'grad_conv_b_mix', 'grad_w_o', 'grad_g_pre_ffn', 'grad_g_post_ffn', 'grad_w_up', 'grad_conv_w_ffn', 'grad_conv_b_ffn', 'grad_w_down', 'delta_w_ada', 'delta_b_ada', 'delta_g_pre_mix', 'delta_g_post_mix', 'delta_w_in', 'delta_g_q', 'delta_w_uq', 'delta_g_kv', 'delta_w_ukv', 'delta_conv_w_mix', 'delta_conv_b_mix', 'delta_w_o', 'delta_g_pre_ffn', 'delta_g_post_ffn', 'delta_w_up', 'delta_conv_w_ffn', 'delta_conv_b_ffn', 'delta_w_down', 'new_m_w_ada', 'new_m_b_ada', 'new_m_g_pre_mix', 'new_m_g_post_mix', 'new_m_w_in', 'new_m_g_q', 'new_m_w_uq', 'new_m_g_kv', 'new_m_w_ukv', 'new_m_conv_w_mix', 'new_m_conv_b_mix', 'new_m_w_o', 'new_m_g_pre_ffn', 'new_m_g_post_ffn', 'new_m_w_up', 'new_m_conv_w_ffn', 'new_m_conv_b_ffn', 'new_m_w_down', 'new_v_w_ada', 'new_v_b_ada', 'new_v_g_pre_mix', 'new_v_g_post_mix', 'new_v_w_in', 'new_v_g_q', 'new_v_w_uq', 'new_v_g_kv', 'new_v_w_ukv', 'new_v_conv_w_mix', 'new_v_conv_b_mix', 'new_v_w_o', 'new_v_g_pre_ffn', 'new_v_g_post_ffn', 'new_v_w_up', 'new_v_conv_w_ffn', 'new_v_conv_b_ffn', 'new_v_w_down']
TWIN_LEAF_KINDS = {'loss': 'loss', 'grad_x': 'grad_x', 'grad_w_ada': 'grad_w', 'grad_b_ada': 'grad_w', 'grad_g_pre_mix': 'grad_w', 'grad_g_post_mix': 'grad_w', 'grad_w_in': 'grad_w', 'grad_g_q': 'grad_w', 'grad_w_uq': 'grad_w', 'grad_g_kv': 'grad_w', 'grad_w_ukv': 'grad_w', 'grad_conv_w_mix': 'grad_w', 'grad_conv_b_mix': 'grad_w', 'grad_w_o': 'grad_w', 'grad_g_pre_ffn': 'grad_w', 'grad_g_post_ffn': 'grad_w', 'grad_w_up': 'grad_w', 'grad_conv_w_ffn': 'grad_w', 'grad_conv_b_ffn': 'grad_w', 'grad_w_down': 'grad_w', 'delta_w_ada': 'delta_w', 'delta_b_ada': 'delta_w', 'delta_g_pre_mix': 'delta_w', 'delta_g_post_mix': 'delta_w', 'delta_w_in': 'delta_w', 'delta_g_q': 'delta_w', 'delta_w_uq': 'delta_w', 'delta_g_kv': 'delta_w', 'delta_w_ukv': 'delta_w', 'delta_conv_w_mix': 'delta_w', 'delta_conv_b_mix': 'delta_w', 'delta_w_o': 'delta_w', 'delta_g_pre_ffn': 'delta_w', 'delta_g_post_ffn': 'delta_w', 'delta_w_up': 'delta_w', 'delta_conv_w_ffn': 'delta_w', 'delta_conv_b_ffn': 'delta_w', 'delta_w_down': 'delta_w', 'new_m_w_ada': 'new_m', 'new_m_b_ada': 'new_m', 'new_m_g_pre_mix': 'new_m', 'new_m_g_post_mix': 'new_m', 'new_m_w_in': 'new_m', 'new_m_g_q': 'new_m', 'new_m_w_uq': 'new_m', 'new_m_g_kv': 'new_m', 'new_m_w_ukv': 'new_m', 'new_m_conv_w_mix': 'new_m', 'new_m_conv_b_mix': 'new_m', 'new_m_w_o': 'new_m', 'new_m_g_pre_ffn': 'new_m', 'new_m_g_post_ffn': 'new_m', 'new_m_w_up': 'new_m', 'new_m_conv_w_ffn': 'new_m', 'new_m_conv_b_ffn': 'new_m', 'new_m_w_down': 'new_m', 'new_v_w_ada': 'new_v', 'new_v_b_ada': 'new_v', 'new_v_g_pre_mix': 'new_v', 'new_v_g_post_mix': 'new_v', 'new_v_w_in': 'new_v', 'new_v_g_q': 'new_v', 'new_v_w_uq': 'new_v', 'new_v_g_kv': 'new_v', 'new_v_w_ukv': 'new_v', 'new_v_conv_w_mix': 'new_v', 'new_v_conv_b_mix': 'new_v', 'new_v_w_o': 'new_v', 'new_v_g_pre_ffn': 'new_v', 'new_v_g_post_ffn': 'new_v', 'new_v_w_up': 'new_v', 'new_v_conv_w_ffn': 'new_v', 'new_v_conv_b_ffn': 'new_v', 'new_v_w_down': 'new_v'}


def _forward(args):
    return _fwd_reference(*[args[k] for k in FWD_PARAMS])


def _output_shape():
    def fwd():
        inp = _fwd_setup_inputs(0)
        return _fwd_reference(*[inp[k] for k in FWD_PARAMS])
    out = _jax.eval_shape(fwd)
    return out.shape, out.dtype

N_MICROBATCH = 1
ADAM_LR = 0.001
ADAM_B1 = 0.9
ADAM_B2 = 0.999
ADAM_EPS = 1e-08
ADAM_WD = 0.01
ADAM_STEP = 10
PER_EXAMPLE_BATCH_AXIS = {'x': 0, 'c': 0, 'positions': 0, 'loss_target': 0}
SHARED_INPUTS = []
_WEIGHT_DTYPES = {'w_ada': _jnp.float32, 'b_ada': _jnp.float32, 'g_pre_mix': _jnp.float32, 'g_post_mix': _jnp.float32, 'w_in': _jnp.float32, 'g_q': _jnp.float32, 'w_uq': _jnp.float32, 'g_kv': _jnp.float32, 'w_ukv': _jnp.float32, 'conv_w_mix': _jnp.float32, 'conv_b_mix': _jnp.float32, 'w_o': _jnp.float32, 'g_pre_ffn': _jnp.float32, 'g_post_ffn': _jnp.float32, 'w_up': _jnp.float32, 'conv_w_ffn': _jnp.float32, 'conv_b_ffn': _jnp.float32, 'w_down': _jnp.float32}
MOMENT_SCALE = {'w_ada': 2.438075e+00, 'b_ada': 5.332402e+00, 'g_pre_mix': 2.467974e-01, 'g_post_mix': 1.325378e+01, 'w_in': 2.692026e-01, 'g_q': 2.040401e-02, 'w_uq': 1.466523e-02, 'g_kv': 7.532436e-01, 'w_ukv': 3.280334e-01, 'conv_w_mix': 3.011946e-01, 'conv_b_mix': 5.649441e-01, 'w_o': 3.442749e-01, 'g_pre_ffn': 1.811925e-01, 'g_post_ffn': 1.307251e+01, 'w_up': 1.708803e-01, 'conv_w_ffn': 2.069817e-01, 'conv_b_ffn': 3.345371e-01, 'w_down': 3.620489e-01}


def _to_microbatches(a, axis):
    t = _jnp.moveaxis(a, axis, 0)
    t = t.reshape((N_MICROBATCH, t.shape[0] // N_MICROBATCH) + t.shape[1:])
    return _jnp.moveaxis(t, 1, axis + 1)


def setup_inputs(seed: int = 0) -> dict:
    inp = _fwd_setup_inputs(seed)
    key = _jax.random.fold_in(_jax.random.key(seed), 7919)
    shape, _ = _output_shape()
    out = dict(inp)
    out["loss_target"] = _jax.random.normal(_jax.random.fold_in(key, 0), shape, _jnp.float32)
    for i, name in enumerate(TWIN_WEIGHTS):
        w = inp[name].astype(_jnp.float32)
        if MOMENT_SCALE is None:
            s = _jnp.sqrt(_jnp.mean(_jnp.square(w)) + 1e-30)
        else:
            s = MOMENT_SCALE[name]
        km, kv = _jax.random.split(_jax.random.fold_in(key, i + 1))
        out[name] = w
        out["m_" + name] = s * _jax.random.normal(km, w.shape, _jnp.float32)
        out["v_" + name] = (s * s) * _jax.random.uniform(kv, w.shape, _jnp.float32, 0.5, 1.5)
    if N_MICROBATCH > 1:
        for name, axis in PER_EXAMPLE_BATCH_AXIS.items():
            out[name] = _to_microbatches(out[name], axis)
    return {'x': out['x'], 'c': out['c'], 'positions': out['positions'], 'w_ada': out['w_ada'], 'b_ada': out['b_ada'], 'g_pre_mix': out['g_pre_mix'], 'g_post_mix': out['g_post_mix'], 'w_in': out['w_in'], 'g_q': out['g_q'], 'w_uq': out['w_uq'], 'g_kv': out['g_kv'], 'w_ukv': out['w_ukv'], 'conv_w_mix': out['conv_w_mix'], 'conv_b_mix': out['conv_b_mix'], 'w_o': out['w_o'], 'g_pre_ffn': out['g_pre_ffn'], 'g_post_ffn': out['g_post_ffn'], 'w_up': out['w_up'], 'conv_w_ffn': out['conv_w_ffn'], 'conv_b_ffn': out['conv_b_ffn'], 'w_down': out['w_down'], 'loss_target': out['loss_target'], 'm_w_ada': out['m_w_ada'], 'm_b_ada': out['m_b_ada'], 'm_g_pre_mix': out['m_g_pre_mix'], 'm_g_post_mix': out['m_g_post_mix'], 'm_w_in': out['m_w_in'], 'm_g_q': out['m_g_q'], 'm_w_uq': out['m_w_uq'], 'm_g_kv': out['m_g_kv'], 'm_w_ukv': out['m_w_ukv'], 'm_conv_w_mix': out['m_conv_w_mix'], 'm_conv_b_mix': out['m_conv_b_mix'], 'm_w_o': out['m_w_o'], 'm_g_pre_ffn': out['m_g_pre_ffn'], 'm_g_post_ffn': out['m_g_post_ffn'], 'm_w_up': out['m_w_up'], 'm_conv_w_ffn': out['m_conv_w_ffn'], 'm_conv_b_ffn': out['m_conv_b_ffn'], 'm_w_down': out['m_w_down'], 'v_w_ada': out['v_w_ada'], 'v_b_ada': out['v_b_ada'], 'v_g_pre_mix': out['v_g_pre_mix'], 'v_g_post_mix': out['v_g_post_mix'], 'v_w_in': out['v_w_in'], 'v_g_q': out['v_g_q'], 'v_w_uq': out['v_w_uq'], 'v_g_kv': out['v_g_kv'], 'v_w_ukv': out['v_w_ukv'], 'v_conv_w_mix': out['v_conv_w_mix'], 'v_conv_b_mix': out['v_conv_b_mix'], 'v_w_o': out['v_w_o'], 'v_g_pre_ffn': out['v_g_pre_ffn'], 'v_g_post_ffn': out['v_g_post_ffn'], 'v_w_up': out['v_w_up'], 'v_conv_w_ffn': out['v_conv_w_ffn'], 'v_conv_b_ffn': out['v_conv_b_ffn'], 'v_w_down': out['v_w_down']}


def _loss(weights, diff, rest, loss_target):
    with _jax.named_scope("forward"):
        args = {**rest, TWIN_DIFF_INPUT: diff, **{k: w.astype(_WEIGHT_DTYPES[k]) for k, w in weights.items()}}
        y = _forward(args)
    with _jax.named_scope("loss_head"):
        err = _jnp.square(y.astype(_jnp.float32) - loss_target)
        return 0.5 * _jnp.sum(_jnp.mean(err, axis=-1)) if err.ndim else 0.5 * err


def _adamw(w, g, m, v):
    m = ADAM_B1 * m + (1.0 - ADAM_B1) * g
    v = ADAM_B2 * v + (1.0 - ADAM_B2) * _jnp.square(g)
    m_hat = m / (1.0 - ADAM_B1 ** ADAM_STEP)
    v_hat = v / (1.0 - ADAM_B2 ** ADAM_STEP)
    delta = -ADAM_LR * (m_hat / (_jnp.sqrt(v_hat) + ADAM_EPS) + ADAM_WD * w)
    return delta, m, v


def reference(x, c, positions, w_ada, b_ada, g_pre_mix, g_post_mix, w_in, g_q, w_uq, g_kv, w_ukv, conv_w_mix, conv_b_mix, w_o, g_pre_ffn, g_post_ffn, w_up, conv_w_ffn, conv_b_ffn, w_down, loss_target, m_w_ada, m_b_ada, m_g_pre_mix, m_g_post_mix, m_w_in, m_g_q, m_w_uq, m_g_kv, m_w_ukv, m_conv_w_mix, m_conv_b_mix, m_w_o, m_g_pre_ffn, m_g_post_ffn, m_w_up, m_conv_w_ffn, m_conv_b_ffn, m_w_down, v_w_ada, v_b_ada, v_g_pre_mix, v_g_post_mix, v_w_in, v_g_q, v_w_uq, v_g_kv, v_w_ukv, v_conv_w_mix, v_conv_b_mix, v_w_o, v_g_pre_ffn, v_g_post_ffn, v_w_up, v_conv_w_ffn, v_conv_b_ffn, v_w_down):
    given = dict(x=x, c=c, positions=positions, w_ada=w_ada, b_ada=b_ada, g_pre_mix=g_pre_mix, g_post_mix=g_post_mix, w_in=w_in, g_q=g_q, w_uq=w_uq, g_kv=g_kv, w_ukv=w_ukv, conv_w_mix=conv_w_mix, conv_b_mix=conv_b_mix, w_o=w_o, g_pre_ffn=g_pre_ffn, g_post_ffn=g_post_ffn, w_up=w_up, conv_w_ffn=conv_w_ffn, conv_b_ffn=conv_b_ffn, w_down=w_down, loss_target=loss_target, m_w_ada=m_w_ada, m_b_ada=m_b_ada, m_g_pre_mix=m_g_pre_mix, m_g_post_mix=m_g_post_mix, m_w_in=m_w_in, m_g_q=m_g_q, m_w_uq=m_w_uq, m_g_kv=m_g_kv, m_w_ukv=m_w_ukv, m_conv_w_mix=m_conv_w_mix, m_conv_b_mix=m_conv_b_mix, m_w_o=m_w_o, m_g_pre_ffn=m_g_pre_ffn, m_g_post_ffn=m_g_post_ffn, m_w_up=m_w_up, m_conv_w_ffn=m_conv_w_ffn, m_conv_b_ffn=m_conv_b_ffn, m_w_down=m_w_down, v_w_ada=v_w_ada, v_b_ada=v_b_ada, v_g_pre_mix=v_g_pre_mix, v_g_post_mix=v_g_post_mix, v_w_in=v_w_in, v_g_q=v_g_q, v_w_uq=v_w_uq, v_g_kv=v_g_kv, v_w_ukv=v_w_ukv, v_conv_w_mix=v_conv_w_mix, v_conv_b_mix=v_conv_b_mix, v_w_o=v_w_o, v_g_pre_ffn=v_g_pre_ffn, v_g_post_ffn=v_g_post_ffn, v_w_up=v_w_up, v_conv_w_ffn=v_conv_w_ffn, v_conv_b_ffn=v_conv_b_ffn, v_w_down=v_w_down)
    weights = {n: given[n] for n in TWIN_WEIGHTS}
    shared = {n: given[n] for n in SHARED_INPUTS}
    per_example = {n: given[n] for n in ['x', 'c', 'positions']}
    grad_fn = _jax.value_and_grad(_loss, argnums=(0, 1))

    def one_microbatch(ex, loss_target):
        ex = dict(ex)
        diff = ex.pop(TWIN_DIFF_INPUT)
        return grad_fn(weights, diff, {**shared, **ex}, loss_target)

    if N_MICROBATCH == 1:
        loss, (grad_w, grad_x) = one_microbatch(per_example, given["loss_target"])
    else:
        def body(carry, xs):
            loss_sum, grad_sum = carry
            l_k, (gw_k, gx_k) = one_microbatch(xs[0], xs[1])
            with _jax.named_scope("update"):
                return (loss_sum + l_k, _jax.tree.map(_jnp.add, grad_sum, gw_k)), gx_k

        init = (_jnp.zeros((), _jnp.float32), _jax.tree.map(_jnp.zeros_like, weights))
        (loss, grad_w), grad_x = _jax.lax.scan(body, init, (per_example, given["loss_target"]))
    with _jax.named_scope("update"):
        delta_w, new_m, new_v = {}, {}, {}
        for n in TWIN_WEIGHTS:
            delta_w[n], new_m[n], new_v[n] = _adamw(weights[n], grad_w[n], given["m_" + n], given["v_" + n])
    return (loss, grad_x, *[grad_w[n] for n in TWIN_WEIGHTS], *[delta_w[n] for n in TWIN_WEIGHTS],
            *[new_m[n] for n in TWIN_WEIGHTS], *[new_v[n] for n in TWIN_WEIGHTS])
```

```python
import math

import numpy as np
import jax
import jax.numpy as jnp
from jax import lax
from jax.experimental import pallas as pl
from jax.experimental.pallas import tpu as pltpu

F32 = jnp.float32
BF16 = jnp.bfloat16
N_DEV = 8
MESH = pl.DeviceIdType.MESH

QK_NOPE = 128
QK_ROPE = 64
V_DIM = 128
HEAD_W = 256
LANE = 128
SUB = 8
RMS_EPS = 1e-6
ROPE_THETA = 10000.0
ADAM_LR = 0.001
ADAM_B1 = 0.9
ADAM_B2 = 0.999
ADAM_EPS = 1e-08
ADAM_WD = 0.01
ADAM_STEP = 10
NEG = -1e30
VMEM_LIMIT = 56 * 1024 * 1024

PREF = {"row": 256, "conv_rows": 512, "conv_cols": 512, "attn": 1024}

NT_DIMS = (((1,), (1,)), ((), ()))
TN_DIMS = (((0,), (0,)), ((), ()))


def _cparams(*sem):
    return pltpu.CompilerParams(dimension_semantics=sem if sem else None, vmem_limit_bytes=VMEM_LIMIT)


def _tile(n, pref, unit=LANE):
    if n <= pref:
        return n
    t = (pref // unit) * unit
    while t >= unit:
        if n % t == 0:
            return t
        t -= unit
    return n


def _roundup(n, m):
    return (n + m - 1) // m * m


def _rsq(x):
    return lax.rsqrt(jnp.mean(x * x, axis=-1, keepdims=True) + RMS_EPS)


def _colsum(x):
    return jnp.sum(x, axis=0, keepdims=True)


def _sigmoid(x):
    return 1.0 / (1.0 + jnp.exp(-x))


def _matmul(a, b, *, ta=False, tb=False, out_dtype, tm, tn, tk, name):
    m_dim, k_dim = (a.shape[1], a.shape[0]) if ta else a.shape
    n_dim = b.shape[0] if tb else b.shape[1]
    tm, tn, tk = _tile(m_dim, tm), _tile(n_dim, tn), _tile(k_dim, tk)
    nk = k_dim // tk
    dims = (((0 if ta else 1,), (1 if tb else 0,)), ((), ()))

    def body(a_ref, b_ref, o_ref, *acc):
        part = lax.dot_general(a_ref[...], b_ref[...], dims, preferred_element_type=F32)
        if nk == 1:
            o_ref[...] = part.astype(o_ref.dtype)
            return
        acc_ref = acc[0]
        k = pl.program_id(2)

        @pl.when(k == 0)
        def _():
            acc_ref[...] = part

        @pl.when(k > 0)
        def _():
            acc_ref[...] += part

        @pl.when(k == nk - 1)
        def _():
            o_ref[...] = acc_ref[...].astype(o_ref.dtype)

    a_spec = pl.BlockSpec((tk, tm), lambda i, j, k: (k, i)) if ta else pl.BlockSpec((tm, tk), lambda i, j, k: (i, k))
    b_spec = pl.BlockSpec((tn, tk), lambda i, j, k: (j, k)) if tb else pl.BlockSpec((tk, tn), lambda i, j, k: (k, j))
    return pl.pallas_call(
        body,
        name=name,
        grid=(m_dim // tm, n_dim // tn, nk),
        in_specs=[a_spec, b_spec],
        out_specs=pl.BlockSpec((tm, tn), lambda i, j, k: (i, j)),
        out_shape=jax.ShapeDtypeStruct((m_dim, n_dim), out_dtype),
        scratch_shapes=[pltpu.VMEM((tm, tn), F32)] if nk > 1 else [],
        compiler_params=_cparams("parallel", "parallel", "arbitrary"),
    )(a, b)


def _shift_down(x, halo, n):
    r = pltpu.roll(x, n, 0)
    hr = pltpu.roll(halo, n, 0)
    row = lax.broadcasted_iota(jnp.int32, halo.shape, 0)
    top = jnp.where(row < n, hr, r[:SUB])
    return jnp.concatenate([top, r[SUB:]], axis=0)


def _shift_up(x, halo, n):
    ts = x.shape[0]
    r = pltpu.roll(x, ts - n, 0)
    hr = pltpu.roll(halo, SUB - n, 0)
    row = lax.broadcasted_iota(jnp.int32, halo.shape, 0)
    bot = jnp.where(row >= SUB - n, hr, r[ts - SUB:])
    return jnp.concatenate([r[:ts - SUB], bot], axis=0)


def _conv3(x, halo, w_ref, b_ref):
    return _shift_down(x, halo, 2) * w_ref[0:1, :] + _shift_down(x, halo, 1) * w_ref[1:2, :] + x * w_ref[2:3, :] + b_ref[...]


def _prev_halo(ts, col):
    return lambda j, i: (jnp.maximum(i * (ts // SUB) - 1, 0), col(j))


def _next_halo(ts, n_rows, col):
    return lambda j, i: (jnp.minimum((i + 1) * (ts // SUB), n_rows // SUB - 1), col(j))


def _modnorm_fwd(x, g, sc, sh, ts):
    s_len, d = x.shape

    def body(x_ref, g_ref, sc_ref, sh_ref, h_ref):
        xv = x_ref[...]
        h_ref[...] = ((xv * _rsq(xv) * g_ref[...]) * (1.0 + sc_ref[...]) + sh_ref[...]).astype(BF16)

    vec = pl.BlockSpec((1, d), lambda i: (0, 0))
    return pl.pallas_call(
        body, name="modnorm_fwd", grid=(s_len // ts,),
        in_specs=[pl.BlockSpec((ts, d), lambda i: (i, 0)), vec, vec, vec],
        out_specs=pl.BlockSpec((ts, d), lambda i: (i, 0)),
        out_shape=jax.ShapeDtypeStruct((s_len, d), BF16),
        compiler_params=_cparams("parallel"),
    )(x, g, sc, sh)


def _rope_tables(pos_col, inv_freq_row, ts):
    s_len = pos_col.shape[0]
    half = QK_ROPE // 2

    def body(p_ref, f_ref, c_ref, a_ref, b_ref):
        ang = p_ref[...] * f_ref[...]
        lane = lax.broadcasted_iota(jnp.int32, ang.shape, 1)
        cos, sin = jnp.cos(ang), jnp.sin(ang)
        c_ref[...] = jnp.where(lane < 2 * half, cos, 0.0)
        a_ref[...] = jnp.where(lane < half, -sin, 0.0)
        b_ref[...] = jnp.where((lane >= half) & (lane < 2 * half), sin, 0.0)

    out = jax.ShapeDtypeStruct((s_len, LANE), F32)
    blk = pl.BlockSpec((ts, LANE), lambda i: (i, 0))
    return pl.pallas_call(
        body, name="rope_tables", grid=(s_len // ts,),
        in_specs=[pl.BlockSpec((ts, 1), lambda i: (i, 0)), pl.BlockSpec((1, LANE), lambda i: (0, 0))],
        out_specs=[blk, blk, blk], out_shape=[out, out, out],
        compiler_params=_cparams("parallel"),
    )(pos_col, inv_freq_row)


def _rope(seg, c, a, b):
    return seg * c + pltpu.roll(seg, LANE - QK_ROPE // 2, 1) * a + pltpu.roll(seg, QK_ROPE // 2, 1) * b


def _rope_t(seg, c, a, b):
    return seg * c - pltpu.roll(seg, LANE - QK_ROPE // 2, 1) * a - pltpu.roll(seg, QK_ROPE // 2, 1) * b


def _qkv_fwd(proj, lay, wuq, wk, wv, g_q, g_kv, ctab, atab, btab, n_heads, ts, scale):
    s_len = proj.shape[0]
    ql_w, kl_w = wuq.shape[0], wk.shape[0]

    def body(ql_ref, kl_ref, kr_ref, wuq_ref, wk_ref, wv_ref, gq_ref, gkv_ref, c_ref, a_ref, b_ref,
             q_out, k_out, v_out, qn_out, kvn_out):
        c, a, b = c_ref[...], a_ref[...], b_ref[...]
        ql = ql_ref[...].astype(F32)
        qn = (ql * _rsq(ql) * gq_ref[...]).astype(BF16)
        qn_out[...] = qn
        q = jnp.dot(qn, wuq_ref[...], preferred_element_type=F32)
        kl = kl_ref[...].astype(F32)
        kvn = (kl * _rsq(kl) * gkv_ref[...]).astype(BF16)
        kvn_out[...] = kvn
        kn = jnp.dot(kvn, wk_ref[...], preferred_element_type=F32)
        v_out[...] = jnp.dot(kvn, wv_ref[...], preferred_element_type=F32).astype(BF16)
        kr = _rope(kr_ref[...].astype(F32), c, a, b).astype(BF16)
        for h in range(n_heads):
            o = h * HEAD_W
            q_out[:, o:o + QK_NOPE] = (q[:, o:o + QK_NOPE] * scale).astype(BF16)
            q_out[:, o + QK_NOPE:o + HEAD_W] = (_rope(q[:, o + QK_NOPE:o + HEAD_W], c, a, b) * scale).astype(BF16)
            k_out[:, o:o + QK_NOPE] = kn[:, h * QK_NOPE:(h + 1) * QK_NOPE].astype(BF16)
            k_out[:, o + QK_NOPE:o + HEAD_W] = kr

    def full(arr):
        return pl.BlockSpec(arr.shape, lambda i: (0, 0))

    tab = pl.BlockSpec((ts, LANE), lambda i: (i, 0))
    hw, hv = n_heads * HEAD_W, n_heads * V_DIM
    return pl.pallas_call(
        body, name="qkv_fwd", grid=(s_len // ts,),
        in_specs=[pl.BlockSpec((ts, ql_w), lambda i: (i, lay["ql"] // ql_w)),
                  pl.BlockSpec((ts, kl_w), lambda i: (i, lay["kv"] // kl_w)),
                  pl.BlockSpec((ts, LANE), lambda i: (i, lay["kr"] // LANE)),
                  full(wuq), full(wk), full(wv), full(g_q), full(g_kv), tab, tab, tab],
        out_specs=[pl.BlockSpec((ts, hw), lambda i: (i, 0)), pl.BlockSpec((ts, hw), lambda i: (i, 0)),
                   pl.BlockSpec((ts, hv), lambda i: (i, 0)), pl.BlockSpec((ts, ql_w), lambda i: (i, 0)),
                   pl.BlockSpec((ts, kl_w), lambda i: (i, 0))],
        out_shape=[jax.ShapeDtypeStruct((s_len, hw), BF16), jax.ShapeDtypeStruct((s_len, hw), BF16),
                   jax.ShapeDtypeStruct((s_len, hv), BF16), jax.ShapeDtypeStruct((s_len, ql_w), BF16),
                   jax.ShapeDtypeStruct((s_len, kl_w), BF16)],
        compiler_params=_cparams("parallel"),
    )(proj, proj, proj, wuq, wk, wv, g_q, g_kv, ctab, atab, btab)


def _flash_fwd(q, k, v, n_heads, t):
    s_len = q.shape[0]
    nb = s_len // t
    pairs = [(i, j) for i in range(nb) for j in range(i + 1)]
    itab = jnp.asarray(np.array([p[0] for p in pairs], np.int32))
    jtab = jnp.asarray(np.array([p[1] for p in pairs], np.int32))

    def body(it_ref, jt_ref, q_ref, k_ref, v_ref, o_ref, lse_ref, m_sc, l_sc, acc_sc):
        step_id = pl.program_id(1)
        i, j = it_ref[step_id], jt_ref[step_id]

        @pl.when(j == 0)
        def _():
            m_sc[...] = jnp.full(m_sc.shape, NEG, F32)
            l_sc[...] = jnp.zeros(l_sc.shape, F32)
            acc_sc[...] = jnp.zeros(acc_sc.shape, F32)

        def step(diag):
            s = lax.dot_general(q_ref[...], k_ref[...], NT_DIMS, preferred_element_type=F32)
            if diag:
                row = lax.broadcasted_iota(jnp.int32, s.shape, 0)
                col = lax.broadcasted_iota(jnp.int32, s.shape, 1)
                s = jnp.where(col <= row, s, NEG)
            m_prev = m_sc[...]
            m_new = jnp.maximum(m_prev, jnp.max(s, axis=1, keepdims=True))
            alpha = jnp.exp(m_prev - m_new)
            p = jnp.exp(s - m_new)
            l_sc[...] = alpha * l_sc[...] + jnp.sum(p, axis=1, keepdims=True)
            acc_sc[...] = acc_sc[...] * alpha + jnp.dot(p.astype(BF16), v_ref[...], preferred_element_type=F32)
            m_sc[...] = m_new

        @pl.when(j < i)
        def _():
            step(False)

        @pl.when(j == i)
        def _():
            step(True)
            l = l_sc[...]
            o_ref[...] = (acc_sc[...] / l).astype(BF16)
            lse_ref[0] = jnp.broadcast_to(m_sc[...] + jnp.log(l), (t, LANE))

    grid_spec = pltpu.PrefetchScalarGridSpec(
        num_scalar_prefetch=2, grid=(n_heads, len(pairs)),
        in_specs=[pl.BlockSpec((t, HEAD_W), lambda h, s, it, jt: (it[s], h)),
                  pl.BlockSpec((t, HEAD_W), lambda h, s, it, jt: (jt[s], h)),
                  pl.BlockSpec((t, V_DIM), lambda h, s, it, jt: (jt[s], h))],
        out_specs=[pl.BlockSpec((t, V_DIM), lambda h, s, it, jt: (it[s], h)),
                   pl.BlockSpec((1, t, LANE), lambda h, s, it, jt: (h, it[s], 0))],
        scratch_shapes=[pltpu.VMEM((t, 1), F32), pltpu.VMEM((t, 1), F32), pltpu.VMEM((t, V_DIM), F32)],
    )
    return pl.pallas_call(
        body, name="flash_fwd", grid_spec=grid_spec,
        out_shape=[jax.ShapeDtypeStruct((s_len, n_heads * V_DIM), BF16),
                   jax.ShapeDtypeStruct((n_heads, s_len, LANE), F32)],
        compiler_params=_cparams("parallel", "arbitrary"),
    )(itab, jtab, q, k, v)


def _gconv_fwd(proj, lay, w, b, ts, tc):
    s_len = proj.shape[0]
    cw = w.shape[1]
    nj = cw // tc

    def body(gb_ref, gc_ref, ci_ref, gch_ref, cih_ref, w_ref, b_ref, o_ref):
        i = pl.program_id(1)
        p = gc_ref[...].astype(F32) * ci_ref[...].astype(F32)
        ph = jnp.where(i > 0, gch_ref[...].astype(F32) * cih_ref[...].astype(F32), 0.0)
        o_ref[...] = (gb_ref[...].astype(F32) * _conv3(p, ph, w_ref, b_ref)).astype(BF16)

    def blk(off):
        return pl.BlockSpec((ts, tc), lambda j, i: (i, off // tc + j))

    def halo(off):
        return pl.BlockSpec((SUB, tc), _prev_halo(ts, lambda j: off // tc + j))

    return pl.pallas_call(
        body, name="gconv_fwd", grid=(nj, s_len // ts),
        in_specs=[blk(lay["gb"]), blk(lay["gc"]), blk(lay["ci"]), halo(lay["gc"]), halo(lay["ci"]),
                  pl.BlockSpec((3, tc), lambda j, i: (0, j)), pl.BlockSpec((1, tc), lambda j, i: (0, j))],
        out_specs=pl.BlockSpec((ts, tc), lambda j, i: (i, j)),
        out_shape=jax.ShapeDtypeStruct((s_len, cw), BF16),
        compiler_params=_cparams("parallel", "parallel"),
    )(proj, proj, proj, proj, proj, w, b)


def _post_mix_fwd(mix, x, gt, g_post, g_pre, sc, sh, ts):
    s_len, d = x.shape

    def body(mix_ref, x_ref, gt_ref, gp_ref, g2_ref, sc_ref, sh_ref, x1_ref, h2_ref):
        mv = mix_ref[...]
        x1 = x_ref[...] + gt_ref[...] * (mv * _rsq(mv) * gp_ref[...])
        x1_ref[...] = x1
        h2_ref[...] = ((x1 * _rsq(x1) * g2_ref[...]) * (1.0 + sc_ref[...]) + sh_ref[...]).astype(BF16)

    vec = pl.BlockSpec((1, d), lambda i: (0, 0))
    row = pl.BlockSpec((ts, d), lambda i: (i, 0))
    return pl.pallas_call(
        body, name="post_mix_fwd", grid=(s_len // ts,),
        in_specs=[row, row, vec, vec, vec, vec, vec], out_specs=[row, row],
        out_shape=[jax.ShapeDtypeStruct((s_len, d), F32), jax.ShapeDtypeStruct((s_len, d), BF16)],
        compiler_params=_cparams("parallel"),
    )(mix, x, gt, g_post, g_pre, sc, sh)


def _act_fwd(up, w, b, ts, tc):
    s_len, f2 = up.shape
    nh = f2 // 2 // tc

    def body(a_ref, g_ref, ah_ref, gh_ref, wa_ref, wg_ref, ba_ref, bg_ref, o_ref):
        i = pl.program_id(1)
        ah = jnp.where(i > 0, ah_ref[...].astype(F32), 0.0)
        gh = jnp.where(i > 0, gh_ref[...].astype(F32), 0.0)
        ua = _conv3(a_ref[...].astype(F32), ah, wa_ref, ba_ref)
        ug = _conv3(g_ref[...].astype(F32), gh, wg_ref, bg_ref)
        o_ref[...] = (ug * _sigmoid(ug) * ua).astype(BF16)

    return pl.pallas_call(
        body, name="act_fwd", grid=(nh, s_len // ts),
        in_specs=[pl.BlockSpec((ts, tc), lambda j, i: (i, j)), pl.BlockSpec((ts, tc), lambda j, i: (i, nh + j)),
                  pl.BlockSpec((SUB, tc), _prev_halo(ts, lambda j: j)),
                  pl.BlockSpec((SUB, tc), _prev_halo(ts, lambda j: nh + j)),
                  pl.BlockSpec((3, tc), lambda j, i: (0, j)), pl.BlockSpec((3, tc), lambda j, i: (0, nh + j)),
                  pl.BlockSpec((1, tc), lambda j, i: (0, j)), pl.BlockSpec((1, tc), lambda j, i: (0, nh + j))],
        out_specs=pl.BlockSpec((ts, tc), lambda j, i: (i, j)),
        out_shape=jax.ShapeDtypeStruct((s_len, f2 // 2), BF16),
        compiler_params=_cparams("parallel", "parallel"),
    )(up, up, up, up, w, w, b, b)


def _final_bwd(y, x1, tgt, gt, g_post, ts):
    s_len, d = y.shape

    def body(y_ref, x1_ref, t_ref, gt_ref, g_ref, dy_ref, dx2_ref, loss_ref, dgt_ref, dg_ref):
        i = pl.program_id(0)
        yv = y_ref[...]
        r = _rsq(yv)
        yh = yv * r
        n = yh * g_ref[...]
        e = x1_ref[...] + gt_ref[...] * n - t_ref[...]
        loss = 0.5 * jnp.sum(jnp.mean(e * e, axis=-1, keepdims=True), axis=0, keepdims=True)
        dx2 = e * (1.0 / d)
        dx2_ref[...] = dx2
        dn = dx2 * gt_ref[...]
        dyh = dn * g_ref[...]
        dy_ref[...] = (r * (dyh - yh * jnp.mean(dyh * yh, axis=-1, keepdims=True))).astype(BF16)

        @pl.when(i == 0)
        def _():
            loss_ref[...] = jnp.zeros(loss_ref.shape, F32)
            dgt_ref[...] = jnp.zeros(dgt_ref.shape, F32)
            dg_ref[...] = jnp.zeros(dg_ref.shape, F32)

        loss_ref[...] += jnp.broadcast_to(loss, loss_ref.shape)
        dgt_ref[...] += _colsum(dx2 * n)
        dg_ref[...] += _colsum(dn * yh)

    vec = pl.BlockSpec((1, d), lambda i: (0, 0))
    row = pl.BlockSpec((ts, d), lambda i: (i, 0))
    vshape = jax.ShapeDtypeStruct((1, d), F32)
    return pl.pallas_call(
        body, name="final_bwd", grid=(s_len // ts,),
        in_specs=[row, row, row, vec, vec],
        out_specs=[row, row, pl.BlockSpec((1, LANE), lambda i: (0, 0)), vec, vec],
        out_shape=[jax.ShapeDtypeStruct((s_len, d), BF16), jax.ShapeDtypeStruct((s_len, d), F32),
                   jax.ShapeDtypeStruct((1, LANE), F32), vshape, vshape],
        compiler_params=_cparams("arbitrary"),
    )(y, x1, tgt, gt, g_post)


def _gate_bwd(d_act, up, w, b, ts, tc):
    s_len, f2 = up.shape
    nh = f2 // 2 // tc
    nj = 2 * nh

    def body(d_ref, s_ref, p_ref, sh_ref, ph_ref, ws_ref, wp_ref, bs_ref, bp_ref, o_ref):
        j, i = pl.program_id(0), pl.program_id(1)
        sh = jnp.where(i > 0, sh_ref[...].astype(F32), 0.0)
        ph = jnp.where(i > 0, ph_ref[...].astype(F32), 0.0)
        us = _conv3(s_ref[...].astype(F32), sh, ws_ref, bs_ref)
        upart = _conv3(p_ref[...].astype(F32), ph, wp_ref, bp_ref)
        dv = d_ref[...].astype(F32)

        @pl.when(j < nh)
        def _():
            o_ref[...] = (dv * upart * _sigmoid(upart)).astype(BF16)

        @pl.when(j >= nh)
        def _():
            sg = _sigmoid(us)
            o_ref[...] = (dv * upart * (sg * (1.0 + us * (1.0 - sg)))).astype(BF16)

    def part(j):
        return (j + nh) % nj

    return pl.pallas_call(
        body, name="gate_bwd", grid=(nj, s_len // ts),
        in_specs=[pl.BlockSpec((ts, tc), lambda j, i: (i, j % nh)),
                  pl.BlockSpec((ts, tc), lambda j, i: (i, j)), pl.BlockSpec((ts, tc), lambda j, i: (i, part(j))),
                  pl.BlockSpec((SUB, tc), _prev_halo(ts, lambda j: j)),
                  pl.BlockSpec((SUB, tc), _prev_halo(ts, part)),
                  pl.BlockSpec((3, tc), lambda j, i: (0, j)), pl.BlockSpec((3, tc), lambda j, i: (0, part(j))),
                  pl.BlockSpec((1, tc), lambda j, i: (0, j)), pl.BlockSpec((1, tc), lambda j, i: (0, part(j)))],
        out_specs=pl.BlockSpec((ts, tc), lambda j, i: (i, j)),
        out_shape=jax.ShapeDtypeStruct((s_len, f2), BF16),
        compiler_params=_cparams("parallel", "parallel"),
    )(d_act, up, up, up, up, w, w, b, b)


def _conv_bwd(du, x, w, ts, tc):
    s_len, c = x.shape
    n_i = s_len // ts

    def body(du_ref, duh_ref, x_ref, xh_ref, w_ref, dx_ref, dw_ref, db_ref):
        i = pl.program_id(1)
        duv = du_ref[...].astype(F32)
        duh = jnp.where(i < n_i - 1, duh_ref[...].astype(F32), 0.0)
        xv = x_ref[...].astype(F32)
        xh = jnp.where(i > 0, xh_ref[...].astype(F32), 0.0)
        dx_ref[...] = (duv * w_ref[2:3, :] + _shift_up(duv, duh, 1) * w_ref[1:2, :]
                       + _shift_up(duv, duh, 2) * w_ref[0:1, :]).astype(BF16)

        @pl.when(i == 0)
        def _():
            dw_ref[...] = jnp.zeros(dw_ref.shape, F32)
            db_ref[...] = jnp.zeros(db_ref.shape, F32)

        dw_ref[0:1, :] += _colsum(duv * _shift_down(xv, xh, 2))
        dw_ref[1:2, :] += _colsum(duv * _shift_down(xv, xh, 1))
        dw_ref[2:3, :] += _colsum(duv * xv)
        db_ref[...] += _colsum(duv)

    blk = pl.BlockSpec((ts, tc), lambda j, i: (i, j))
    return pl.pallas_call(
        body, name="conv_bwd", grid=(c // tc, n_i),
        in_specs=[blk, pl.BlockSpec((SUB, tc), _next_halo(ts, s_len, lambda j: j)),
                  blk, pl.BlockSpec((SUB, tc), _prev_halo(ts, lambda j: j)),
                  pl.BlockSpec((3, tc), lambda j, i: (0, j))],
        out_specs=[blk, pl.BlockSpec((3, tc), lambda j, i: (0, j)), pl.BlockSpec((1, tc), lambda j, i: (0, j))],
        out_shape=[jax.ShapeDtypeStruct((s_len, c), BF16), jax.ShapeDtypeStruct((3, c), F32),
                   jax.ShapeDtypeStruct((1, c), F32)],
        compiler_params=_cparams("parallel", "arbitrary"),
    )(du, du, x, x, w)


def _mid_bwd(dh2, x1, dx2, mix, g_pre, sc, gt_m, g_post, ts):
    s_len, d = x1.shape

    def body(dh_ref, x1_ref, dx2_ref, mix_ref, g_ref, sc_ref, gt_ref, gp_ref,
             dx1_ref, dmix_ref, dsh_ref, dsc_ref, dg_ref, dgt_ref, dgp_ref):
        i = pl.program_id(0)
        dh = dh_ref[...]
        x1 = x1_ref[...]
        r1 = _rsq(x1)
        xh = x1 * r1
        dxh = dh * (1.0 + sc_ref[...]) * g_ref[...]
        dx1 = dx2_ref[...] + r1 * (dxh - xh * jnp.mean(dxh * xh, axis=-1, keepdims=True))
        dx1_ref[...] = dx1
        mv = mix_ref[...]
        rm = _rsq(mv)
        mh = mv * rm
        dn = dx1 * gt_ref[...]
        dmh = dn * gp_ref[...]
        dmix_ref[...] = (rm * (dmh - mh * jnp.mean(dmh * mh, axis=-1, keepdims=True))).astype(BF16)

        @pl.when(i == 0)
        def _():
            for ref in (dsh_ref, dsc_ref, dg_ref, dgt_ref, dgp_ref):
                ref[...] = jnp.zeros(ref.shape, F32)

        dsh_ref[...] += _colsum(dh)
        dsc_ref[...] += _colsum(dh * (xh * g_ref[...]))
        dg_ref[...] += _colsum(dh * (1.0 + sc_ref[...]) * xh)
        dgt_ref[...] += _colsum(dx1 * (mh * gp_ref[...]))
        dgp_ref[...] += _colsum(dn * mh)

    vec = pl.BlockSpec((1, d), lambda i: (0, 0))
    row = pl.BlockSpec((ts, d), lambda i: (i, 0))
    vshape = jax.ShapeDtypeStruct((1, d), F32)
    return pl.pallas_call(
        body, name="mid_bwd", grid=(s_len // ts,),
        in_specs=[row, row, row, row, vec, vec, vec, vec],
        out_specs=[row, row, vec, vec, vec, vec, vec],
        out_shape=[jax.ShapeDtypeStruct((s_len, d), F32), jax.ShapeDtypeStruct((s_len, d), BF16)] + [vshape] * 5,
        compiler_params=_cparams("arbitrary"),
    )(dh2, x1, dx2, mix, g_pre, sc, gt_m, g_post)


def _first_bwd(dh1, x, dx1, g_pre, sc, ts):
    s_len, d = x.shape

    def body(dh_ref, x_ref, dx1_ref, g_ref, sc_ref, dx_ref, dsh_ref, dsc_ref, dg_ref):
        i = pl.program_id(0)
        dh = dh_ref[...]
        xv = x_ref[...]
        r = _rsq(xv)
        xh = xv * r
        dxh = dh * (1.0 + sc_ref[...]) * g_ref[...]
        dx_ref[...] = dx1_ref[...] + r * (dxh - xh * jnp.mean(dxh * xh, axis=-1, keepdims=True))

        @pl.when(i == 0)
        def _():
            for ref in (dsh_ref, dsc_ref, dg_ref):
                ref[...] = jnp.zeros(ref.shape, F32)

        dsh_ref[...] += _colsum(dh)
        dsc_ref[...] += _colsum(dh * (xh * g_ref[...]))
        dg_ref[...] += _colsum(dh * (1.0 + sc_ref[...]) * xh)

    vec = pl.BlockSpec((1, d), lambda i: (0, 0))
    row = pl.BlockSpec((ts, d), lambda i: (i, 0))
    vshape = jax.ShapeDtypeStruct((1, d), F32)
    return pl.pallas_call(
        body, name="first_bwd", grid=(s_len // ts,),
        in_specs=[row, row, row, vec, vec], out_specs=[row, vec, vec, vec],
        out_shape=[jax.ShapeDtypeStruct((s_len, d), F32), vshape, vshape, vshape],
        compiler_params=_cparams("arbitrary"),
    )(dh1, x, dx1, g_pre, sc)


def _gconv_bwd(d_mixcat, proj, lay, w, b, ts, tc):
    s_len = proj.shape[0]
    cw = w.shape[1]
    n_i = s_len // ts
    dc_off = d_mixcat.shape[1] - cw

    def body(dc_ref, dch_ref, gb_ref, gbh_ref, gc_ref, gch_ref, ci_ref, cih_ref, w_ref, b_ref,
             dgb_ref, dgc_ref, dci_ref, dw_ref, db_ref):
        i = pl.program_id(1)
        gc, ci = gc_ref[...].astype(F32), ci_ref[...].astype(F32)
        p = gc * ci
        ph = jnp.where(i > 0, gch_ref[...].astype(F32) * cih_ref[...].astype(F32), 0.0)
        pm1, pm2 = _shift_down(p, ph, 1), _shift_down(p, ph, 2)
        z = pm2 * w_ref[0:1, :] + pm1 * w_ref[1:2, :] + p * w_ref[2:3, :] + b_ref[...]
        dc = dc_ref[...].astype(F32)
        dgb_ref[...] = (dc * z).astype(BF16)
        dz = dc * gb_ref[...].astype(F32)
        dzh = jnp.where(i < n_i - 1, dch_ref[...].astype(F32) * gbh_ref[...].astype(F32), 0.0)
        dp = dz * w_ref[2:3, :] + _shift_up(dz, dzh, 1) * w_ref[1:2, :] + _shift_up(dz, dzh, 2) * w_ref[0:1, :]
        dgc_ref[...] = (dp * ci).astype(BF16)
        dci_ref[...] = (dp * gc).astype(BF16)

        @pl.when(i == 0)
        def _():
            dw_ref[...] = jnp.zeros(dw_ref.shape, F32)
            db_ref[...] = jnp.zeros(db_ref.shape, F32)

        dw_ref[0:1, :] += _colsum(dz * pm2)
        dw_ref[1:2, :] += _colsum(dz * pm1)
        dw_ref[2:3, :] += _colsum(dz * p)
        db_ref[...] += _colsum(dz)

    def blk(off):
        return pl.BlockSpec((ts, tc), lambda j, i: (i, off // tc + j))

    def prev(off):
        return pl.BlockSpec((SUB, tc), _prev_halo(ts, lambda j: off // tc + j))

    def nxt(off):
        return pl.BlockSpec((SUB, tc), _next_halo(ts, s_len, lambda j: off // tc + j))

    out_blk = pl.BlockSpec((ts, tc), lambda j, i: (i, j))
    act = jax.ShapeDtypeStruct((s_len, cw), BF16)
    return pl.pallas_call(
        body, name="gconv_bwd", grid=(cw // tc, n_i),
        in_specs=[blk(dc_off), nxt(dc_off), blk(lay["gb"]), nxt(lay["gb"]), blk(lay["gc"]), prev(lay["gc"]),
                  blk(lay["ci"]), prev(lay["ci"]),
                  pl.BlockSpec((3, tc), lambda j, i: (0, j)), pl.BlockSpec((1, tc), lambda j, i: (0, j))],
        out_specs=[out_blk, out_blk, out_blk,
                   pl.BlockSpec((3, tc), lambda j, i: (0, j)), pl.BlockSpec((1, tc), lambda j, i: (0, j))],
        out_shape=[act, act, act, jax.ShapeDtypeStruct((3, cw), F32), jax.ShapeDtypeStruct((1, cw), F32)],
        compiler_params=_cparams("parallel", "arbitrary"),
    )(d_mixcat, d_mixcat, proj, proj, proj, proj, proj, proj, w, b)


def _delta(o, d_mixcat, n_heads, ts):
    s_len = o.shape[0]

    def body(o_ref, do_ref, out_ref):
        for h in range(n_heads):
            sl = slice(h * V_DIM, (h + 1) * V_DIM)
            prod = o_ref[:, sl].astype(F32) * do_ref[:, sl].astype(F32)
            out_ref[h] = jnp.broadcast_to(jnp.sum(prod, axis=1, keepdims=True), (ts, LANE))

    hv = n_heads * V_DIM
    return pl.pallas_call(
        body, name="attn_delta", grid=(s_len // ts,),
        in_specs=[pl.BlockSpec((ts, hv), lambda i: (i, 0)), pl.BlockSpec((ts, hv), lambda i: (i, 0))],
        out_specs=pl.BlockSpec((n_heads, ts, LANE), lambda i: (0, i, 0)),
        out_shape=jax.ShapeDtypeStruct((n_heads, s_len, LANE), F32),
        compiler_params=_cparams("parallel"),
    )(o, d_mixcat)


def _flash_bwd(q, k, v, d_mixcat, lse_row, delta_row, n_heads, t, scale):
    s_len = q.shape[0]
    nb = s_len // t
    pairs = [(j, i) for j in range(nb) for i in range(j, nb)]
    jtab = jnp.asarray(np.array([p[0] for p in pairs], np.int32))
    itab = jnp.asarray(np.array([p[1] for p in pairs], np.int32))
    n_steps = len(pairs)

    def body(jt_ref, it_ref, q_ref, k_ref, v_ref, do_ref, lse_ref, dl_ref, dq_ref, dk_ref, dv_ref,
             dq_acc, dk_acc, dv_acc):
        step_id = pl.program_id(1)
        j, i = jt_ref[step_id], it_ref[step_id]

        @pl.when(step_id == 0)
        def _():
            dq_acc[...] = jnp.zeros(dq_acc.shape, F32)

        @pl.when(i == j)
        def _():
            dk_acc[...] = jnp.zeros(dk_acc.shape, F32)
            dv_acc[...] = jnp.zeros(dv_acc.shape, F32)

        def step(diag):
            qv, kv, vv, dov = q_ref[...], k_ref[...], v_ref[...], do_ref[...]
            s_t = lax.dot_general(kv, qv, NT_DIMS, preferred_element_type=F32)
            if diag:
                krow = lax.broadcasted_iota(jnp.int32, s_t.shape, 0)
                qcol = lax.broadcasted_iota(jnp.int32, s_t.shape, 1)
                s_t = jnp.where(krow <= qcol, s_t, NEG)
            p_t = jnp.exp(s_t - lse_ref[0])
            dv_acc[...] += jnp.dot(p_t.astype(BF16), dov, preferred_element_type=F32)
            dp_t = lax.dot_general(vv, dov, NT_DIMS, preferred_element_type=F32)
            ds_t = (p_t * (dp_t - dl_ref[0])).astype(BF16)
            dk_acc[...] += jnp.dot(ds_t, qv, preferred_element_type=F32)
            rows = pl.ds(pl.multiple_of(i * t, t), t)
            dq_acc[rows, :] += lax.dot_general(ds_t, kv, TN_DIMS, preferred_element_type=F32)

        @pl.when(i > j)
        def _():
            step(False)

        @pl.when(i == j)
        def _():
            step(True)

        @pl.when(i == nb - 1)
        def _():
            dk_ref[...] = dk_acc[...].astype(BF16)
            dv_ref[...] = dv_acc[...].astype(BF16)

        @pl.when(step_id == n_steps - 1)
        def _():
            dq_ref[...] = (dq_acc[...] * scale).astype(BF16)

    hv = n_heads * V_DIM
    do_off = 0
    grid_spec = pltpu.PrefetchScalarGridSpec(
        num_scalar_prefetch=2, grid=(n_heads, n_steps),
        in_specs=[pl.BlockSpec((t, HEAD_W), lambda h, s, jt, it: (it[s], h)),
                  pl.BlockSpec((t, HEAD_W), lambda h, s, jt, it: (jt[s], h)),
                  pl.BlockSpec((t, V_DIM), lambda h, s, jt, it: (jt[s], h)),
                  pl.BlockSpec((t, V_DIM), lambda h, s, jt, it: (it[s], do_off + h)),
                  pl.BlockSpec((1, 1, t), lambda h, s, jt, it: (h, 0, it[s])),
                  pl.BlockSpec((1, 1, t), lambda h, s, jt, it: (h, 0, it[s]))],
        out_specs=[pl.BlockSpec((s_len, HEAD_W), lambda h, s, jt, it: (0, h)),
                   pl.BlockSpec((t, HEAD_W), lambda h, s, jt, it: (jt[s], h)),
                   pl.BlockSpec((t, V_DIM), lambda h, s, jt, it: (jt[s], h))],
        scratch_shapes=[pltpu.VMEM((s_len, HEAD_W), F32), pltpu.VMEM((t, HEAD_W), F32), pltpu.VMEM((t, V_DIM), F32)],
    )
    return pl.pallas_call(
        body, name="flash_bwd", grid_spec=grid_spec,
        out_shape=[jax.ShapeDtypeStruct((s_len, n_heads * HEAD_W), BF16),
                   jax.ShapeDtypeStruct((s_len, n_heads * HEAD_W), BF16),
                   jax.ShapeDtypeStruct((s_len, hv), BF16)],
        compiler_params=_cparams("parallel", "arbitrary"),
    )(jtab, itab, q, k, v, d_mixcat, lse_row, delta_row)


def _qkv_bwd(dq, dk, dv, proj, lay, wuq, wk, wv, g_q, g_kv, ctab, atab, btab, n_heads, ts):
    s_len = proj.shape[0]
    ql_w, kl_w = wuq.shape[0], wk.shape[0]
    tail_w = lay["np"] - lay["ql"]
    kv_o, kr_o = lay["kv"] - lay["ql"], lay["kr"] - lay["ql"]

    def body(dq_ref, dk_ref, dv_ref, ql_ref, kl_ref, wuq_ref, wk_ref, wv_ref, gq_ref, gkv_ref, c_ref, a_ref, b_ref,
             dqr_ref, dkn_ref, tail_ref, dgq_ref, dgkv_ref):
        i = pl.program_id(0)
        c, a, b = c_ref[...], a_ref[...], b_ref[...]
        dkr = jnp.zeros((ts, LANE), F32)
        for h in range(n_heads):
            o = h * HEAD_W
            dqr_ref[:, o:o + QK_NOPE] = dq_ref[:, o:o + QK_NOPE]
            dqr_ref[:, o + QK_NOPE:o + HEAD_W] = _rope_t(dq_ref[:, o + QK_NOPE:o + HEAD_W].astype(F32), c, a, b).astype(BF16)
            dkn_ref[:, h * QK_NOPE:(h + 1) * QK_NOPE] = dk_ref[:, o:o + QK_NOPE]
            dkr = dkr + dk_ref[:, o + QK_NOPE:o + HEAD_W].astype(F32)
        tail_ref[...] = jnp.zeros(tail_ref.shape, BF16)
        tail_ref[:, kr_o:kr_o + LANE] = _rope_t(dkr, c, a, b).astype(BF16)

        def rms_bwd(lat_ref, dn, g_ref):
            lat = lat_ref[...].astype(F32)
            r = _rsq(lat)
            xh = lat * r
            dxh = dn * g_ref[...]
            return r * (dxh - xh * jnp.mean(dxh * xh, axis=-1, keepdims=True)), _colsum(dn * xh)

        dqn = lax.dot_general(dqr_ref[...], wuq_ref[...], NT_DIMS, preferred_element_type=F32)
        d_ql, dgq = rms_bwd(ql_ref, dqn, gq_ref)
        tail_ref[:, 0:ql_w] = d_ql.astype(BF16)
        dkvn = (lax.dot_general(dkn_ref[...], wk_ref[...], NT_DIMS, preferred_element_type=F32)
                + lax.dot_general(dv_ref[...], wv_ref[...], NT_DIMS, preferred_element_type=F32))
        d_kl, dgkv = rms_bwd(kl_ref, dkvn, gkv_ref)
        tail_ref[:, kv_o:kv_o + kl_w] = d_kl.astype(BF16)

        @pl.when(i == 0)
        def _():
            dgq_ref[...] = jnp.zeros(dgq_ref.shape, F32)
            dgkv_ref[...] = jnp.zeros(dgkv_ref.shape, F32)

        dgq_ref[...] += dgq
        dgkv_ref[...] += dgkv

    def full(arr):
        return pl.BlockSpec(arr.shape, lambda i: (0, 0))

    def rows(w):
        return pl.BlockSpec((ts, w), lambda i: (i, 0))

    tab = pl.BlockSpec((ts, LANE), lambda i: (i, 0))
    hw, hv, hn = n_heads * HEAD_W, n_heads * V_DIM, n_heads * QK_NOPE
    return pl.pallas_call(
        body, name="qkv_bwd", grid=(s_len // ts,),
        in_specs=[rows(hw), rows(hw), rows(hv),
                  pl.BlockSpec((ts, ql_w), lambda i: (i, lay["ql"] // ql_w)),
                  pl.BlockSpec((ts, kl_w), lambda i: (i, lay["kv"] // kl_w)),
                  full(wuq), full(wk), full(wv), full(g_q), full(g_kv), tab, tab, tab],
        out_specs=[rows(hw), rows(hn), rows(tail_w), full(g_q), full(g_kv)],
        out_shape=[jax.ShapeDtypeStruct((s_len, hw), BF16), jax.ShapeDtypeStruct((s_len, hn), BF16),
                   jax.ShapeDtypeStruct((s_len, tail_w), BF16),
                   jax.ShapeDtypeStruct(g_q.shape, F32), jax.ShapeDtypeStruct(g_kv.shape, F32)],
        compiler_params=_cparams("arbitrary"),
    )(dq, dk, dv, proj, proj, wuq, wk, wv, g_q, g_kv, ctab, atab, btab)


def _adamw(w, g, m, v):
    m = ADAM_B1 * m + (1.0 - ADAM_B1) * g
    v = ADAM_B2 * v + (1.0 - ADAM_B2) * (g * g)
    m_hat = m / (1.0 - ADAM_B1 ** ADAM_STEP)
    v_hat = v / (1.0 - ADAM_B2 ** ADAM_STEP)
    delta = -ADAM_LR * (m_hat / (jnp.sqrt(v_hat) + ADAM_EPS) + ADAM_WD * w)
    return delta, m, v


def _adam_parts(parts, w, m, v, tr, name):
    r, c = w.shape
    tr = _tile(r, tr, SUB)

    def body(p_ref, w_ref, m_ref, v_ref, g_out, d_out, m_out, v_out):
        g = p_ref[0].astype(F32)
        for dev in range(1, N_DEV):
            g = g + p_ref[dev].astype(F32)
        g_out[...] = g
        d_out[...], m_out[...], v_out[...] = _adamw(w_ref[...], g, m_ref[...], v_ref[...])

    blk = pl.BlockSpec((tr, c), lambda i: (i, 0))
    shp = jax.ShapeDtypeStruct((r, c), F32)
    return pl.pallas_call(
        body, name=name, grid=(r // tr,),
        in_specs=[pl.BlockSpec((N_DEV, tr, c), lambda i: (0, i, 0)), blk, blk, blk],
        out_specs=[blk, blk, blk, blk], out_shape=[shp, shp, shp, shp],
        compiler_params=_cparams("parallel"),
    )(parts, w, m, v)


def _adam_ada(cact_t, dmod_sh, w, m, v, tr):
    r, c = w.shape

    def body(ct_ref, dm_ref, w_ref, m_ref, v_ref, g_out, d_out, m_out, v_out):
        g = jnp.dot(ct_ref[...], dm_ref[...], preferred_element_type=F32, precision=lax.Precision.HIGHEST)
        g_out[...] = g
        d_out[...], m_out[...], v_out[...] = _adamw(w_ref[...], g, m_ref[...], v_ref[...])

    blk = pl.BlockSpec((tr, c), lambda i: (i, 0))
    shp = jax.ShapeDtypeStruct((r, c), F32)
    return pl.pallas_call(
        body, name="adam_ada", grid=(r // tr,),
        in_specs=[pl.BlockSpec((tr, N_DEV), lambda i: (i, 0)), pl.BlockSpec((N_DEV, c), lambda i: (0, 0)), blk, blk, blk],
        out_specs=[blk, blk, blk, blk], out_shape=[shp, shp, shp, shp],
        compiler_params=_cparams("parallel"),
    )(cact_t, dmod_sh, w, m, v)


def _adam_small(v_all, w, m, v):
    n = w.shape[1]

    def body(p_ref, w_ref, m_ref, v_ref, g_out, d_out, m_out, v_out):
        g = p_ref[0:1, :]
        for dev in range(1, N_DEV):
            g = g + p_ref[dev:dev + 1, :]
        g_out[...] = g
        d_out[...], m_out[...], v_out[...] = _adamw(w_ref[...], g, m_ref[...], v_ref[...])

    shp = jax.ShapeDtypeStruct((1, n), F32)
    vm = pl.BlockSpec(memory_space=pltpu.VMEM)
    return pl.pallas_call(
        body, name="adam_small", in_specs=[vm, vm, vm, vm], out_specs=[vm, vm, vm, vm],
        out_shape=[shp, shp, shp, shp], compiler_params=_cparams(),
    )(v_all, w, m, v)


def _my_place():
    return lax.axis_index("x"), lax.axis_index("y"), lax.axis_index("c")


def _peer(place, k):
    x, y, c = place
    return (x ^ (k >> 2), y ^ ((k >> 1) & 1), c ^ (k & 1))


def _index(place):
    return 4 * place[0] + 2 * place[1] + place[2]


def _ada_fwd(vec, w_ada, b_ada_rows):
    lv = vec.shape[1]
    d, c = w_ada.shape

    def body(vec_ref, w_ref, b_ref, gath_ref, cact_ref, mod_ref, modsh, send_a, recv_a, send_b, recv_b, local_s):
        me = _my_place()
        my_i = _index(me)

        def gather_copy(k, to, src_row):
            row = gath_ref.at[pl.ds(src_row, 1), :]
            return pltpu.make_async_remote_copy(src_ref=row, dst_ref=row, send_sem=send_a.at[k], recv_sem=recv_a.at[k],
                                                device_id=to, device_id_type=MESH)

        own = pltpu.make_async_copy(vec_ref, gath_ref.at[pl.ds(my_i, 1), :], local_s.at[0])
        own.start()
        own.wait()
        sends = [gather_copy(k, _peer(me, k), my_i) for k in range(1, N_DEV)]
        for cp in sends:
            cp.start()
        for k in range(1, N_DEV):
            gather_copy(k, me, _index(_peer(me, k))).wait_recv()
        for cp in sends:
            cp.wait_send()

        c_all = gath_ref[:, 0:d]
        cact = c_all * _sigmoid(c_all)
        cact_ref[...] = cact
        modsh[...] = jnp.dot(cact, w_ref[...], preferred_element_type=F32, precision=lax.Precision.HIGHEST)

        def mod_copy(k, to, src_row, dst_row):
            return pltpu.make_async_remote_copy(src_ref=modsh.at[pl.ds(src_row, 1), :], dst_ref=mod_ref.at[pl.ds(dst_row, 1), :],
                                                send_sem=send_b.at[k], recv_sem=recv_b.at[k],
                                                device_id=to, device_id_type=MESH)

        own = pltpu.make_async_copy(modsh.at[pl.ds(my_i, 1), :], mod_ref.at[pl.ds(my_i, 1), :], local_s.at[1])
        own.start()
        sends = [mod_copy(k, _peer(me, k), _index(_peer(me, k)), my_i) for k in range(1, N_DEV)]
        for cp in sends:
            cp.start()
        for k in range(1, N_DEV):
            mod_copy(k, me, my_i, _index(_peer(me, k))).wait_recv()
        for cp in sends:
            cp.wait_send()
        own.wait()
        mod_ref[...] = mod_ref[...] + b_ref[...]

    vm = pl.BlockSpec(memory_space=pltpu.VMEM)
    return pl.pallas_call(
        body, name="ada_fwd", in_specs=[vm, vm, vm], out_specs=[vm, vm, vm],
        out_shape=[jax.ShapeDtypeStruct((N_DEV, lv), F32), jax.ShapeDtypeStruct((N_DEV, d), F32),
                   jax.ShapeDtypeStruct((N_DEV, c), F32)],
        scratch_shapes=[pltpu.VMEM((N_DEV, c), F32)] + [pltpu.SemaphoreType.DMA((N_DEV,))] * 4
        + [pltpu.SemaphoreType.DMA((2,))],
        compiler_params=pltpu.CompilerParams(vmem_limit_bytes=VMEM_LIMIT),
    )(vec, w_ada, b_ada_rows)


def _gather_small(vec):
    lv = vec.shape[1]

    def body(vec_ref, gath_ref, send_s, recv_s, local_s):
        me = _my_place()
        my_i = _index(me)

        def copy(k, to, src_row):
            row = gath_ref.at[pl.ds(src_row, 1), :]
            return pltpu.make_async_remote_copy(src_ref=row, dst_ref=row, send_sem=send_s.at[k], recv_sem=recv_s.at[k],
                                                device_id=to, device_id_type=MESH)

        own = pltpu.make_async_copy(vec_ref, gath_ref.at[pl.ds(my_i, 1), :], local_s)
        own.start()
        own.wait()
        sends = [copy(k, _peer(me, k), my_i) for k in range(1, N_DEV)]
        for cp in sends:
            cp.start()
        for k in range(1, N_DEV):
            copy(k, me, _index(_peer(me, k))).wait_recv()
        for cp in sends:
            cp.wait_send()

    vm = pl.BlockSpec(memory_space=pltpu.VMEM)
    return pl.pallas_call(
        body, name="gather_small", in_specs=[vm], out_specs=vm,
        out_shape=jax.ShapeDtypeStruct((N_DEV, lv), F32),
        scratch_shapes=[pltpu.SemaphoreType.DMA((N_DEV,))] * 2 + [pltpu.SemaphoreType.DMA],
        compiler_params=pltpu.CompilerParams(vmem_limit_bytes=VMEM_LIMIT),
    )(vec)


def _gather_weights(shards):
    n = len(shards)
    per = N_DEV - 1

    def body(*refs):
        ins, outs = refs[:n], refs[n:2 * n]
        send_s, recv_s, local_s = refs[2 * n:]
        me = _my_place()
        x, y, c = me
        sibling = (x, y, 1 - c)
        chips = [(1 - x, y), (x, 1 - y), (1 - x, 1 - y)]

        def copy(a, k, block, to, src=None):
            slot = outs[a].at[_index(block)]
            return pltpu.make_async_remote_copy(src_ref=slot if src is None else src, dst_ref=slot,
                                                send_sem=send_s.at[a * per + k], recv_sem=recv_s.at[a * per + k],
                                                device_id=to, device_id_type=MESH)

        mine = [pltpu.make_async_copy(ins[a], outs[a].at[_index(me)], local_s.at[a]) for a in range(n)]
        for cp in mine:
            cp.start()
        first = []
        for a in range(n):
            first.append(copy(a, 0, me, sibling, src=ins[a]))
            first += [copy(a, 1 + j, me, (*chip, c), src=ins[a]) for j, chip in enumerate(chips)]
        for cp in first:
            cp.start()
        passed = []
        for j, chip in enumerate(chips):
            for a in range(n):
                copy(a, 1 + j, (*chip, c), me).wait_recv()
                fwd = copy(a, 4 + j, (*chip, c), sibling)
                fwd.start()
                passed.append(fwd)
        for a in range(n):
            copy(a, 0, sibling, me).wait_recv()
            for j, chip in enumerate(chips):
                copy(a, 4 + j, (*chip, 1 - c), me).wait_recv()
        for cp in first + passed:
            cp.wait_send()
        for cp in mine:
            cp.wait()

    hbm = pl.BlockSpec(memory_space=pl.ANY)
    return pl.pallas_call(
        body, name="gather_weights", in_specs=[hbm] * n, out_specs=[hbm] * n,
        out_shape=[jax.ShapeDtypeStruct((N_DEV,) + s.shape, s.dtype) for s in shards],
        scratch_shapes=[pltpu.SemaphoreType.DMA((n * per,)), pltpu.SemaphoreType.DMA((n * per,)),
                        pltpu.SemaphoreType.DMA((n,))],
    )(*shards)


def _exchange_grads(chunks):
    n = len(chunks)
    per = N_DEV - 1

    def body(*refs):
        ins, outs = refs[:n], refs[n:2 * n]
        send_s, recv_s, local_s = refs[2 * n:]
        me = _my_place()
        my_i = _index(me)

        def copy(a, k, to, src_slot, dst_slot):
            return pltpu.make_async_remote_copy(src_ref=ins[a].at[src_slot], dst_ref=outs[a].at[dst_slot],
                                                send_sem=send_s.at[a * per + k - 1], recv_sem=recv_s.at[a * per + k - 1],
                                                device_id=to, device_id_type=MESH)

        mine = [pltpu.make_async_copy(ins[a].at[my_i], outs[a].at[my_i], local_s.at[a]) for a in range(n)]
        for cp in mine:
            cp.start()
        sends = [copy(a, k, _peer(me, k), _index(_peer(me, k)), my_i) for k in range(1, N_DEV) for a in range(n)]
        for cp in sends:
            cp.start()
        for k in range(1, N_DEV):
            for a in range(n):
                copy(a, k, me, my_i, _index(_peer(me, k))).wait_recv()
        for cp in sends:
            cp.wait_send()
        for cp in mine:
            cp.wait()

    hbm = pl.BlockSpec(memory_space=pl.ANY)
    return pl.pallas_call(
        body, name="exchange_grads", in_specs=[hbm] * n, out_specs=[hbm] * n,
        out_shape=[jax.ShapeDtypeStruct(s.shape, s.dtype) for s in chunks],
        scratch_shapes=[pltpu.SemaphoreType.DMA((n * per,)), pltpu.SemaphoreType.DMA((n * per,)),
                        pltpu.SemaphoreType.DMA((n,))],
    )(*chunks)


def _proj_layout(cw, ql, kl):
    lay = {"gb": 0, "gc": cw, "ci": 2 * cw, "ql": 3 * cw}
    assert lay["ql"] % ql == 0
    lay["kv"] = _roundup(lay["ql"] + ql, kl)
    lay["kr"] = lay["kv"] + kl
    lay["np"] = _roundup(lay["kr"] + LANE, 4 * LANE)
    return lay


def _chunks_cols(g):
    r, c8 = g.shape
    return jnp.transpose(g.reshape(r, N_DEV, c8 // N_DEV), (1, 0, 2))


def _from_col_shards(a):
    n, r, c = a.shape
    return jnp.transpose(a, (1, 0, 2)).reshape(r, n * c)


def kernel(x, c, positions, w_ada, b_ada, g_pre_mix, g_post_mix, w_in, g_q, w_uq, g_kv, w_ukv, conv_w_mix, conv_b_mix, w_o, g_pre_ffn, g_post_ffn, w_up, conv_w_ffn, conv_b_ffn, w_down, loss_target, m_w_ada, m_b_ada, m_g_pre_mix, m_g_post_mix, m_w_in, m_g_q, m_w_uq, m_g_kv, m_w_ukv, m_conv_w_mix, m_conv_b_mix, m_w_o, m_g_pre_ffn, m_g_post_ffn, m_w_up, m_conv_w_ffn, m_conv_b_ffn, m_w_down, v_w_ada, v_b_ada, v_g_pre_mix, v_g_post_mix, v_w_in, v_g_q, v_w_uq, v_g_kv, v_w_ukv, v_conv_w_mix, v_conv_b_mix, v_w_o, v_g_pre_ffn, v_g_post_ffn, v_w_up, v_conv_w_ffn, v_conv_b_ffn, v_w_down):
    s_len, d = x.shape[1], x.shape[2]
    ql, kl = w_uq.shape[1], w_ukv.shape[1]
    n_heads = w_ukv.shape[2] * N_DEV // (QK_NOPE + V_DIM)
    cw = conv_w_mix.shape[2] * N_DEV
    f2 = w_up.shape[2] * N_DEV
    ff = f2 // 2
    in_cols = w_in.shape[2] * N_DEV
    ada_c = w_ada.shape[2]
    cwm_c, cwf_c = conv_w_mix.shape[2], conv_w_ffn.shape[2]
    scale = 1.0 / math.sqrt(QK_NOPE + QK_ROPE)
    lay = _proj_layout(cw, ql, kl)
    n_pad = lay["np"]
    my_i = _index(_my_place())

    ts_row = _tile(s_len, PREF["row"], SUB)
    ts_conv = _tile(s_len, PREF["conv_rows"], SUB)
    tc_conv = _tile(cw, PREF["conv_cols"])
    tc_ffn = _tile(ff, PREF["conv_cols"])
    ts_qkv = _tile(s_len, PREF["row"], SUB)
    t_attn = _tile(s_len, PREF["attn"])

    x2d, tgt = x[0], loss_target[0]

    vec = jnp.concatenate([c, conv_w_mix[0].reshape(1, -1), conv_w_ffn[0].reshape(1, -1)], axis=1)
    gath, cact, mod_rows = _ada_fwd(vec, w_ada[0], b_ada.reshape(N_DEV, ada_c))
    cwm_full = _from_col_shards(gath[:, d:d + 3 * cwm_c].reshape(N_DEV, 3, cwm_c))
    cwf_full = _from_col_shards(gath[:, d + 3 * cwm_c:].reshape(N_DEV, 3, cwf_c))
    mod = mod_rows.reshape(1, N_DEV * ada_c)
    sh_m, sc_m, gt_m, sh_f, sc_f, gt_f = [mod[:, k * d:(k + 1) * d] for k in range(6)]

    g_in, g_uq, g_ukv, g_o, g_up, g_down = _gather_weights(
        [w_in[0].astype(BF16), w_uq[0].astype(BF16), w_ukv[0].astype(BF16), w_o[0].astype(BF16),
         w_up[0].astype(BF16), w_down[0].astype(BF16)])
    win = _from_col_shards(g_in)
    cut = np.cumsum([0, ql, kl, QK_ROPE, cw, cw, cw])
    part = [win[:, cut[k]:cut[k + 1]] for k in range(6)]

    def zcols(n):
        return jnp.zeros((d, n), BF16)

    win_p = jnp.concatenate([part[3], part[4], part[5], part[0], zcols(lay["kv"] - lay["ql"] - ql), part[1],
                             part[2], zcols(n_pad - lay["kr"] - QK_ROPE)], axis=1)
    wuq_p = jnp.pad(_from_col_shards(g_uq).reshape(ql, n_heads, QK_NOPE + QK_ROPE),
                    ((0, 0), (0, 0), (0, HEAD_W - QK_NOPE - QK_ROPE))).reshape(ql, n_heads * HEAD_W)
    wukv = _from_col_shards(g_ukv).reshape(kl, n_heads, QK_NOPE + V_DIM)
    wk = wukv[:, :, :QK_NOPE].reshape(kl, n_heads * QK_NOPE)
    wv = wukv[:, :, QK_NOPE:].reshape(kl, n_heads * V_DIM)
    wo = g_o.reshape(d, d)
    wup = _from_col_shards(g_up)
    wdown = g_down.reshape(ff, d)

    inv_freq = 1.0 / (ROPE_THETA ** (jnp.arange(0, QK_ROPE, 2, dtype=F32) / QK_ROPE))
    inv_row = jnp.tile(inv_freq, LANE // (QK_ROPE // 2)).reshape(1, LANE)
    ctab, atab, btab = _rope_tables(positions.astype(F32).reshape(s_len, 1), inv_row, _tile(s_len, 1024, SUB))

    h1 = _modnorm_fwd(x2d, g_pre_mix, sc_m, sh_m, ts_row)
    proj = _matmul(h1, win_p, out_dtype=BF16, tm=1024, tn=1280, tk=2048, name="mm_proj")
    q, k, v, qn, kvn = _qkv_fwd(proj, lay, wuq_p, wk, wv, g_q, g_kv, ctab, atab, btab, n_heads, ts_qkv, scale)
    attn, lse = _flash_fwd(q, k, v, n_heads, t_attn)
    conv_out = _gconv_fwd(proj, lay, cwm_full, conv_b_mix, ts_conv, tc_conv)
    mixcat = jnp.concatenate([attn, conv_out], axis=1)
    mix = _matmul(mixcat, wo, out_dtype=F32, tm=512, tn=2048, tk=2048, name="mm_mix")
    x1, h2 = _post_mix_fwd(mix, x2d, gt_m, g_post_mix, g_pre_ffn, sc_f, sh_f, ts_row)
    up = _matmul(h2, wup, out_dtype=BF16, tm=1024, tn=1408, tk=2048, name="mm_up")
    act = _act_fwd(up, cwf_full, conv_b_ffn, ts_conv, tc_ffn)
    y = _matmul(act, wdown, out_dtype=F32, tm=512, tn=2048, tk=1408, name="mm_down")

    dy, dx2, loss_row, d_gt_f, dg_post_ffn = _final_bwd(y, x1, tgt, gt_f, g_post_ffn, ts_row)
    gw_down = _matmul(act, dy, ta=True, out_dtype=BF16, tm=1408, tn=1024, tk=512, name="mm_gw_down")
    d_act = _matmul(dy, wdown, tb=True, out_dtype=BF16, tm=1024, tn=1408, tk=2048, name="mm_d_act")
    du = _gate_bwd(d_act, up, cwf_full, conv_b_ffn, ts_conv, tc_ffn)
    d_up, dcw_ffn, dcb_ffn = _conv_bwd(du, up, cwf_full, ts_conv, tc_ffn)
    gw_up = _matmul(h2, d_up, ta=True, out_dtype=BF16, tm=1024, tn=1408, tk=512, name="mm_gw_up")
    dh2 = _matmul(d_up, wup, tb=True, out_dtype=F32, tm=512, tn=2048, tk=1408, name="mm_dh2")
    dx1, dmix, d_sh_f, d_sc_f, dg_pre_ffn, d_gt_m, dg_post_mix = _mid_bwd(
        dh2, x1, dx2, mix, g_pre_ffn, sc_f, gt_m, g_post_mix, ts_row)
    gw_o = _matmul(mixcat, dmix, ta=True, out_dtype=BF16, tm=1024, tn=1024, tk=512, name="mm_gw_o")
    d_mixcat = _matmul(dmix, wo, tb=True, out_dtype=BF16, tm=1024, tn=1024, tk=2048, name="mm_d_mixcat")
    d_gb, d_gc, d_ci, dcw_mix, dcb_mix = _gconv_bwd(d_mixcat, proj, lay, cwm_full, conv_b_mix, ts_conv, tc_conv)
    delta = _delta(attn, d_mixcat, n_heads, _tile(s_len, 512, SUB))
    lse_row = lse[:, :, 0].reshape(n_heads, 1, s_len)
    delta_row = delta[:, :, 0].reshape(n_heads, 1, s_len)
    dq, dk, dv = _flash_bwd(q, k, v, d_mixcat, lse_row, delta_row, n_heads, t_attn, scale)
    dq_r, dkn, d_tail, dg_q, dg_kv = _qkv_bwd(dq, dk, dv, proj, lay, wuq_p, wk, wv, g_q, g_kv, ctab, atab, btab,
                                              n_heads, ts_qkv)
    gw_uq_p = _matmul(qn, dq_r, ta=True, out_dtype=BF16, tm=768, tn=1024, tk=512, name="mm_gw_uq")
    gw_k = _matmul(kvn, dkn, ta=True, out_dtype=BF16, tm=512, tn=1024, tk=512, name="mm_gw_k")
    gw_v = _matmul(kvn, dv, ta=True, out_dtype=BF16, tm=512, tn=1024, tk=512, name="mm_gw_v")
    d_proj = jnp.concatenate([d_gb, d_gc, d_ci, d_tail], axis=1)
    gw_in_p = _matmul(h1, d_proj, ta=True, out_dtype=BF16, tm=1024, tn=1280, tk=512, name="mm_gw_in")
    dh1 = _matmul(d_proj, win_p, tb=True, out_dtype=F32, tm=512, tn=2048, tk=1280, name="mm_dh1")
    grad_x, d_sh_m, d_sc_m, dg_pre_mix = _first_bwd(dh1, x2d, dx1, g_pre_mix, sc_m, ts_row)

    gw_in = jnp.concatenate([gw_in_p[:, lay["ql"]:lay["ql"] + ql], gw_in_p[:, lay["kv"]:lay["kv"] + kl],
                             gw_in_p[:, lay["kr"]:lay["kr"] + QK_ROPE], gw_in_p[:, :3 * cw]], axis=1)
    gw_uq = gw_uq_p.reshape(ql, n_heads, HEAD_W)[:, :, :QK_NOPE + QK_ROPE].reshape(ql, n_heads * (QK_NOPE + QK_ROPE))
    gw_ukv = jnp.concatenate([gw_k.reshape(kl, n_heads, QK_NOPE), gw_v.reshape(kl, n_heads, V_DIM)],
                             axis=2).reshape(kl, n_heads * (QK_NOPE + V_DIM))
    p_in, p_uq, p_ukv, p_o, p_up, p_down = _exchange_grads(
        [_chunks_cols(gw_in), _chunks_cols(gw_uq), _chunks_cols(gw_ukv), gw_o.reshape(N_DEV, d // N_DEV, d),
         _chunks_cols(gw_up), gw_down.reshape(N_DEV, ff // N_DEV, d)])

    dmod = jnp.concatenate([d_sh_m, d_sc_m, d_gt_m, d_sh_f, d_sc_f, d_gt_f], axis=1)
    small_g = [loss_row, dmod, dg_pre_mix, dg_post_mix, dg_q, dg_kv, dcb_mix, dg_pre_ffn, dg_post_ffn, dcb_ffn,
               dcw_mix.reshape(1, -1), dcw_ffn.reshape(1, -1)]
    v_all = _gather_small(jnp.concatenate(small_g, axis=1))

    def place_shard(shard):
        full = jnp.zeros((3, N_DEV * shard.shape[2]), F32)
        return lax.dynamic_update_slice(full, shard[0], (0, my_i * shard.shape[2])).reshape(1, -1)

    def pack(b_ada_, g_pre_mix_, g_post_mix_, g_q_, g_kv_, cb_mix_, g_pre_ffn_, g_post_ffn_, cb_ffn_, cw_mix_, cw_ffn_):
        return jnp.concatenate([jnp.zeros((1, LANE), F32), b_ada_, g_pre_mix_, g_post_mix_, g_q_, g_kv_, cb_mix_,
                                g_pre_ffn_, g_post_ffn_, cb_ffn_, cw_mix_, cw_ffn_], axis=1)

    w_small = pack(b_ada, g_pre_mix, g_post_mix, g_q, g_kv, conv_b_mix, g_pre_ffn, g_post_ffn, conv_b_ffn,
                   cwm_full.reshape(1, -1), cwf_full.reshape(1, -1))
    m_small = pack(m_b_ada, m_g_pre_mix, m_g_post_mix, m_g_q, m_g_kv, m_conv_b_mix, m_g_pre_ffn, m_g_post_ffn,
                   m_conv_b_ffn, place_shard(m_conv_w_mix), place_shard(m_conv_w_ffn))
    v_small = pack(v_b_ada, v_g_pre_mix, v_g_post_mix, v_g_q, v_g_kv, v_conv_b_mix, v_g_pre_ffn, v_g_post_ffn,
                   v_conv_b_ffn, place_shard(v_conv_w_mix), place_shard(v_conv_w_ffn))
    small_out = _adam_small(v_all, w_small, m_small, v_small)

    sizes = [LANE, 6 * d, d, d, ql, kl, cw, d, d, f2, 3 * cw, 3 * f2]
    offs = np.cumsum([0] + sizes)

    def unpack(row):
        parts = [row[:, offs[k]:offs[k + 1]] for k in range(len(sizes))]

        def shard_of(flat, width):
            return lax.dynamic_slice(flat.reshape(3, N_DEV * width), (0, my_i * width), (3, width))[None]

        named = dict(loss=parts[0][0, 0], b_ada=parts[1], g_pre_mix=parts[2], g_post_mix=parts[3], g_q=parts[4],
                     g_kv=parts[5], conv_b_mix=parts[6], g_pre_ffn=parts[7], g_post_ffn=parts[8], conv_b_ffn=parts[9],
                     conv_w_mix=shard_of(parts[10], cwm_c), conv_w_ffn=shard_of(parts[11], cwf_c))
        return named

    small = [unpack(r) for r in small_out]

    dmod_all = v_all[:, offs[1]:offs[2]]
    dmod_sh = lax.dynamic_slice(dmod_all, (0, my_i * ada_c), (N_DEV, ada_c))
    cact_t = jnp.transpose(cact)
    big = dict(
        w_ada=_adam_ada(cact_t, dmod_sh, w_ada[0], m_w_ada[0], v_w_ada[0], _tile(d, 256, SUB)),
        w_in=_adam_parts(p_in, w_in[0], m_w_in[0], v_w_in[0], 256, "adam_w_in"),
        w_uq=_adam_parts(p_uq, w_uq[0], m_w_uq[0], v_w_uq[0], 256, "adam_w_uq"),
        w_ukv=_adam_parts(p_ukv, w_ukv[0], m_w_ukv[0], v_w_ukv[0], 256, "adam_w_ukv"),
        w_o=_adam_parts(p_o, w_o[0], m_w_o[0], v_w_o[0], 128, "adam_w_o"),
        w_up=_adam_parts(p_up, w_up[0], m_w_up[0], v_w_up[0], 256, "adam_w_up"),
        w_down=_adam_parts(p_down, w_down[0], m_w_down[0], v_w_down[0], 176, "adam_w_down"),
    )

    names = ["w_ada", "b_ada", "g_pre_mix", "g_post_mix", "w_in", "g_q", "w_uq", "g_kv", "w_ukv", "conv_w_mix",
             "conv_b_mix", "w_o", "g_pre_ffn", "g_post_ffn", "w_up", "conv_w_ffn", "conv_b_ffn", "w_down"]
    outs = [small[0]["loss"], grad_x[None]]
    for kind in range(4):
        for nm in names:
            outs.append(big[nm][kind][None] if nm in big else small[kind][nm])
    return tuple(outs)
```

```python
import math

import numpy as np
import jax
import jax.numpy as jnp
from jax import lax
from jax.experimental import pallas as pl
from jax.experimental.pallas import tpu as pltpu

F32 = jnp.float32
BF16 = jnp.bfloat16
N_DEV = 8
MESH = pl.DeviceIdType.MESH

QK_NOPE = 128
QK_ROPE = 64
V_DIM = 128
HEAD_W = 256
LANE = 128
SUB = 8
RMS_EPS = 1e-6
ROPE_THETA = 10000.0
ADAM_LR = 0.001
ADAM_B1 = 0.9
ADAM_B2 = 0.999
ADAM_EPS = 1e-08
ADAM_WD = 0.01
ADAM_STEP = 10
NEG = -1e30
VMEM_LIMIT = 56 * 1024 * 1024

PREF = {"row": 256, "conv_rows": 512, "conv_cols": 512, "attn": 1024}

NT_DIMS = (((1,), (1,)), ((), ()))
TN_DIMS = (((0,), (0,)), ((), ()))


def _cparams(*sem):
    return pltpu.CompilerParams(dimension_semantics=sem if sem else None, vmem_limit_bytes=VMEM_LIMIT)


def _tile(n, pref, unit=LANE):
    if n <= pref:
        return n
    t = (pref // unit) * unit
    while t >= unit:
        if n % t == 0:
            return t
        t -= unit
    return n


def _roundup(n, m):
    return (n + m - 1) // m * m


def _rsq(x):
    return lax.rsqrt(jnp.mean(x * x, axis=-1, keepdims=True) + RMS_EPS)


def _colsum(x):
    return jnp.sum(x, axis=0, keepdims=True)


def _sigmoid(x):
    return 1.0 / (1.0 + jnp.exp(-x))


def _matmul(a, b, *, ta=False, tb=False, out_dtype, tm, tn, tk, name):
    m_dim, k_dim = (a.shape[1], a.shape[0]) if ta else a.shape
    n_dim = b.shape[0] if tb else b.shape[1]
    tm, tn, tk = _tile(m_dim, tm), _tile(n_dim, tn), _tile(k_dim, tk)
    nk = k_dim // tk
    dims = (((0 if ta else 1,), (1 if tb else 0,)), ((), ()))

    def body(a_ref, b_ref, o_ref, *acc):
        part = lax.dot_general(a_ref[...], b_ref[...], dims, preferred_element_type=F32)
        if nk == 1:
            o_ref[...] = part.astype(o_ref.dtype)
            return
        acc_ref = acc[0]
        k = pl.program_id(2)

        @pl.when(k == 0)
        def _():
            acc_ref[...] = part

        @pl.when(k > 0)
        def _():
            acc_ref[...] += part

        @pl.when(k == nk - 1)
        def _():
            o_ref[...] = acc_ref[...].astype(o_ref.dtype)

    a_spec = pl.BlockSpec((tk, tm), lambda i, j, k: (k, i)) if ta else pl.BlockSpec((tm, tk), lambda i, j, k: (i, k))
    b_spec = pl.BlockSpec((tn, tk), lambda i, j, k: (j, k)) if tb else pl.BlockSpec((tk, tn), lambda i, j, k: (k, j))
    return pl.pallas_call(
        body,
        name=name,
        grid=(m_dim // tm, n_dim // tn, nk),
        in_specs=[a_spec, b_spec],
        out_specs=pl.BlockSpec((tm, tn), lambda i, j, k: (i, j)),
        out_shape=jax.ShapeDtypeStruct((m_dim, n_dim), out_dtype),
        scratch_shapes=[pltpu.VMEM((tm, tn), F32)] if nk > 1 else [],
        compiler_params=_cparams("parallel", "parallel", "arbitrary"),
    )(a, b)


def _shift_down(x, halo, n):
    r = pltpu.roll(x, n, 0)
    hr = pltpu.roll(halo, n, 0)
    row = lax.broadcasted_iota(jnp.int32, halo.shape, 0)
    top = jnp.where(row < n, hr, r[:SUB])
    return jnp.concatenate([top, r[SUB:]], axis=0)


def _shift_up(x, halo, n):
    ts = x.shape[0]
    r = pltpu.roll(x, ts - n, 0)
    hr = pltpu.roll(halo, SUB - n, 0)
    row = lax.broadcasted_iota(jnp.int32, halo.shape, 0)
    bot = jnp.where(row >= SUB - n, hr, r[ts - SUB:])
    return jnp.concatenate([r[:ts - SUB], bot], axis=0)


def _conv3(x, halo, w_ref, b_ref):
    return _shift_down(x, halo, 2) * w_ref[0:1, :] + _shift_down(x, halo, 1) * w_ref[1:2, :] + x * w_ref[2:3, :] + b_ref[...]


def _prev_halo(ts, col):
    return lambda j, i: (jnp.maximum(i * (ts // SUB) - 1, 0), col(j))


def _next_halo(ts, n_rows, col):
    return lambda j, i: (jnp.minimum((i + 1) * (ts // SUB), n_rows // SUB - 1), col(j))


def _modnorm_fwd(x, g, sc, sh, ts):
    s_len, d = x.shape

    def body(x_ref, g_ref, sc_ref, sh_ref, h_ref):
        xv = x_ref[...]
        h_ref[...] = ((xv * _rsq(xv) * g_ref[...]) * (1.0 + sc_ref[...]) + sh_ref[...]).astype(BF16)

    vec = pl.BlockSpec((1, d), lambda i: (0, 0))
    return pl.pallas_call(
        body, name="modnorm_fwd", grid=(s_len // ts,),
        in_specs=[pl.BlockSpec((ts, d), lambda i: (i, 0)), vec, vec, vec],
        out_specs=pl.BlockSpec((ts, d), lambda i: (i, 0)),
        out_shape=jax.ShapeDtypeStruct((s_len, d), BF16),
        compiler_params=_cparams("parallel"),
    )(x, g, sc, sh)


def _rope_tables(pos_col, inv_freq_row, ts):
    s_len = pos_col.shape[0]
    half = QK_ROPE // 2

    def body(p_ref, f_ref, c_ref, a_ref, b_ref):
        ang = p_ref[...] * f_ref[...]
        lane = lax.broadcasted_iota(jnp.int32, ang.shape, 1)
        cos, sin = jnp.cos(ang), jnp.sin(ang)
        c_ref[...] = jnp.where(lane < 2 * half, cos, 0.0)
        a_ref[...] = jnp.where(lane < half, -sin, 0.0)
        b_ref[...] = jnp.where((lane >= half) & (lane < 2 * half), sin, 0.0)

    out = jax.ShapeDtypeStruct((s_len, LANE), F32)
    blk = pl.BlockSpec((ts, LANE), lambda i: (i, 0))
    return pl.pallas_call(
        body, name="rope_tables", grid=(s_len // ts,),
        in_specs=[pl.BlockSpec((ts, 1), lambda i: (i, 0)), pl.BlockSpec((1, LANE), lambda i: (0, 0))],
        out_specs=[blk, blk, blk], out_shape=[out, out, out],
        compiler_params=_cparams("parallel"),
    )(pos_col, inv_freq_row)


def _rope(seg, c, a, b):
    return seg * c + pltpu.roll(seg, LANE - QK_ROPE // 2, 1) * a + pltpu.roll(seg, QK_ROPE // 2, 1) * b


def _rope_t(seg, c, a, b):
    return seg * c - pltpu.roll(seg, LANE - QK_ROPE // 2, 1) * a - pltpu.roll(seg, QK_ROPE // 2, 1) * b


def _qkv_fwd(proj, lay, wuq, wk, wv, g_q, g_kv, ctab, atab, btab, n_heads, ts, scale):
    s_len = proj.shape[0]
    ql_w, kl_w = wuq.shape[0], wk.shape[0]

    def body(ql_ref, kl_ref, kr_ref, wuq_ref, wk_ref, wv_ref, gq_ref, gkv_ref, c_ref, a_ref, b_ref,
             q_out, k_out, v_out, qn_out, kvn_out):
        c, a, b = c_ref[...], a_ref[...], b_ref[...]
        ql = ql_ref[...].astype(F32)
        qn = (ql * _rsq(ql) * gq_ref[...]).astype(BF16)
        qn_out[...] = qn
        q = jnp.dot(qn, wuq_ref[...], preferred_element_type=F32)
        kl = kl_ref[...].astype(F32)
        kvn = (kl * _rsq(kl) * gkv_ref[...]).astype(BF16)
        kvn_out[...] = kvn
        kn = jnp.dot(kvn, wk_ref[...], preferred_element_type=F32)
        v_out[...] = jnp.dot(kvn, wv_ref[...], preferred_element_type=F32).astype(BF16)
        kr = _rope(kr_ref[...].astype(F32), c, a, b).astype(BF16)
        for h in range(n_heads):
            o = h * HEAD_W
            q_out[:, o:o + QK_NOPE] = (q[:, o:o + QK_NOPE] * scale).astype(BF16)
            q_out[:, o + QK_NOPE:o + HEAD_W] = (_rope(q[:, o + QK_NOPE:o + HEAD_W], c, a, b) * scale).astype(BF16)
            k_out[:, o:o + QK_NOPE] = kn[:, h * QK_NOPE:(h + 1) * QK_NOPE].astype(BF16)
            k_out[:, o + QK_NOPE:o + HEAD_W] = kr

    def full(arr):
        return pl.BlockSpec(arr.shape, lambda i: (0, 0))

    tab = pl.BlockSpec((ts, LANE), lambda i: (i, 0))
    hw, hv = n_heads * HEAD_W, n_heads * V_DIM
    return pl.pallas_call(
        body, name="qkv_fwd", grid=(s_len // ts,),
        in_specs=[pl.BlockSpec((ts, ql_w), lambda i: (i, lay["ql"] // ql_w)),
                  pl.BlockSpec((ts, kl_w), lambda i: (i, lay["kv"] // kl_w)),
                  pl.BlockSpec((ts, LANE), lambda i: (i, lay["kr"] // LANE)),
                  full(wuq), full(wk), full(wv), full(g_q), full(g_kv), tab, tab, tab],
        out_specs=[pl.BlockSpec((ts, hw), lambda i: (i, 0)), pl.BlockSpec((ts, hw), lambda i: (i, 0)),
                   pl.BlockSpec((ts, hv), lambda i: (i, 0)), pl.BlockSpec((ts, ql_w), lambda i: (i, 0)),
                   pl.BlockSpec((ts, kl_w), lambda i: (i, 0))],
        out_shape=[jax.ShapeDtypeStruct((s_len, hw), BF16), jax.ShapeDtypeStruct((s_len, hw), BF16),
                   jax.ShapeDtypeStruct((s_len, hv), BF16), jax.ShapeDtypeStruct((s_len, ql_w), BF16),
                   jax.ShapeDtypeStruct((s_len, kl_w), BF16)],
        compiler_params=_cparams("parallel"),
    )(proj, proj, proj, wuq, wk, wv, g_q, g_kv, ctab, atab, btab)


def _flash_fwd(q, k, v, n_heads, t):
    s_len = q.shape[0]
    nb = s_len // t
    pairs = [(i, j) for i in range(nb) for j in range(i + 1)]
    itab = jnp.asarray(np.array([p[0] for p in pairs], np.int32))
    jtab = jnp.asarray(np.array([p[1] for p in pairs], np.int32))

    def body(it_ref, jt_ref, q_ref, k_ref, v_ref, o_ref, lse_ref, m_sc, l_sc, acc_sc):
        step_id = pl.program_id(1)
        i, j = it_ref[step_id], jt_ref[step_id]

        @pl.when(j == 0)
        def _():
            m_sc[...] = jnp.full(m_sc.shape, NEG, F32)
            l_sc[...] = jnp.zeros(l_sc.shape, F32)
            acc_sc[...] = jnp.zeros(acc_sc.shape, F32)

        def step(diag):
            s = lax.dot_general(q_ref[...], k_ref[...], NT_DIMS, preferred_element_type=F32)
            if diag:
                row = lax.broadcasted_iota(jnp.int32, s.shape, 0)
                col = lax.broadcasted_iota(jnp.int32, s.shape, 1)
                s = jnp.where(col <= row, s, NEG)
            m_prev = m_sc[...]
            m_new = jnp.maximum(m_prev, jnp.max(s, axis=1, keepdims=True))
            alpha = jnp.exp(m_prev - m_new)
            p = jnp.exp(s - m_new)
            l_sc[...] = alpha * l_sc[...] + jnp.sum(p, axis=1, keepdims=True)
            acc_sc[...] = acc_sc[...] * alpha + jnp.dot(p.astype(BF16), v_ref[...], preferred_element_type=F32)
            m_sc[...] = m_new

        @pl.when(j < i)
        def _():
            step(False)

        @pl.when(j == i)
        def _():
            step(True)
            l = l_sc[...]
            o_ref[...] = (acc_sc[...] / l).astype(BF16)
            lse_ref[0] = jnp.broadcast_to(m_sc[...] + jnp.log(l), (t, LANE))

    grid_spec = pltpu.PrefetchScalarGridSpec(
        num_scalar_prefetch=2, grid=(n_heads, len(pairs)),
        in_specs=[pl.BlockSpec((t, HEAD_W), lambda h, s, it, jt: (it[s], h)),
                  pl.BlockSpec((t, HEAD_W), lambda h, s, it, jt: (jt[s], h)),
                  pl.BlockSpec((t, V_DIM), lambda h, s, it, jt: (jt[s], h))],
        out_specs=[pl.BlockSpec((t, V_DIM), lambda h, s, it, jt: (it[s], h)),
                   pl.BlockSpec((1, t, LANE), lambda h, s, it, jt: (h, it[s], 0))],
        scratch_shapes=[pltpu.VMEM((t, 1), F32), pltpu.VMEM((t, 1), F32), pltpu.VMEM((t, V_DIM), F32)],
    )
    return pl.pallas_call(
        body, name="flash_fwd", grid_spec=grid_spec,
        out_shape=[jax.ShapeDtypeStruct((s_len, n_heads * V_DIM), BF16),
                   jax.ShapeDtypeStruct((n_heads, s_len, LANE), F32)],
        compiler_params=_cparams("parallel", "arbitrary"),
    )(itab, jtab, q, k, v)


def _gconv_fwd(proj, lay, w, b, ts, tc):
    s_len = proj.shape[0]
    cw = w.shape[1]
    nj = cw // tc

    def body(gb_ref, gc_ref, ci_ref, gch_ref, cih_ref, w_ref, b_ref, o_ref):
        i = pl.program_id(1)
        p = gc_ref[...].astype(F32) * ci_ref[...].astype(F32)
        ph = jnp.where(i > 0, gch_ref[...].astype(F32) * cih_ref[...].astype(F32), 0.0)
        o_ref[...] = (gb_ref[...].astype(F32) * _conv3(p, ph, w_ref, b_ref)).astype(BF16)

    def blk(off):
        return pl.BlockSpec((ts, tc), lambda j, i: (i, off // tc + j))

    def halo(off):
        return pl.BlockSpec((SUB, tc), _prev_halo(ts, lambda j: off // tc + j))

    return pl.pallas_call(
        body, name="gconv_fwd", grid=(nj, s_len // ts),
        in_specs=[blk(lay["gb"]), blk(lay["gc"]), blk(lay["ci"]), halo(lay["gc"]), halo(lay["ci"]),
                  pl.BlockSpec((3, tc), lambda j, i: (0, j)), pl.BlockSpec((1, tc), lambda j, i: (0, j))],
        out_specs=pl.BlockSpec((ts, tc), lambda j, i: (i, j)),
        out_shape=jax.ShapeDtypeStruct((s_len, cw), BF16),
        compiler_params=_cparams("parallel", "parallel"),
    )(proj, proj, proj, proj, proj, w, b)


def _post_mix_fwd(mix, x, gt, g_post, g_pre, sc, sh, ts):
    s_len, d = x.shape

    def body(mix_ref, x_ref, gt_ref, gp_ref, g2_ref, sc_ref, sh_ref, x1_ref, h2_ref):
        mv = mix_ref[...]
        x1 = x_ref[...] + gt_ref[...] * (mv * _rsq(mv) * gp_ref[...])
        x1_ref[...] = x1
        h2_ref[...] = ((x1 * _rsq(x1) * g2_ref[...]) * (1.0 + sc_ref[...]) + sh_ref[...]).astype(BF16)

    vec = pl.BlockSpec((1, d), lambda i: (0, 0))
    row = pl.BlockSpec((ts, d), lambda i: (i, 0))
    return pl.pallas_call(
        body, name="post_mix_fwd", grid=(s_len // ts,),
        in_specs=[row, row, vec, vec, vec, vec, vec], out_specs=[row, row],
        out_shape=[jax.ShapeDtypeStruct((s_len, d), F32), jax.ShapeDtypeStruct((s_len, d), BF16)],
        compiler_params=_cparams("parallel"),
    )(mix, x, gt, g_post, g_pre, sc, sh)


def _act_fwd(up, w, b, ts, tc):
    s_len, f2 = up.shape
    nh = f2 // 2 // tc

    def body(up_ref, uph_ref, w_ref, b_ref, u_ref, o_ref):
        i = pl.program_id(1)
        halo = jnp.where(i > 0, uph_ref[...].astype(F32), 0.0)
        u = _conv3(up_ref[...].astype(F32), halo, w_ref, b_ref)
        u_ref[...] = u.astype(BF16)
        ua, ug = u[:, :tc], u[:, tc:]
        o_ref[...] = (ug * _sigmoid(ug) * ua).astype(BF16)

    pair = pl.BlockSpec((ts, 2 * tc), lambda j, i: (i, j))
    return pl.pallas_call(
        body, name="act_fwd", grid=(nh, s_len // ts),
        in_specs=[pair, pl.BlockSpec((SUB, 2 * tc), _prev_halo(ts, lambda j: j)),
                  pl.BlockSpec((3, 2 * tc), lambda j, i: (0, j)), pl.BlockSpec((1, 2 * tc), lambda j, i: (0, j))],
        out_specs=[pair, pl.BlockSpec((ts, tc), lambda j, i: (i, j))],
        out_shape=[jax.ShapeDtypeStruct((s_len, f2), BF16), jax.ShapeDtypeStruct((s_len, f2 // 2), BF16)],
        compiler_params=_cparams("parallel", "parallel"),
    )(up, up, w, b)


def _final_bwd(y, x1, tgt, gt, g_post, ts):
    s_len, d = y.shape

    def body(y_ref, x1_ref, t_ref, gt_ref, g_ref, dy_ref, dx2_ref, loss_ref, dgt_ref, dg_ref):
        i = pl.program_id(0)
        yv = y_ref[...]
        r = _rsq(yv)
        yh = yv * r
        n = yh * g_ref[...]
        e = x1_ref[...] + gt_ref[...] * n - t_ref[...]
        loss = 0.5 * jnp.sum(jnp.mean(e * e, axis=-1, keepdims=True), axis=0, keepdims=True)
        dx2 = e * (1.0 / d)
        dx2_ref[...] = dx2
        dn = dx2 * gt_ref[...]
        dyh = dn * g_ref[...]
        dy_ref[...] = (r * (dyh - yh * jnp.mean(dyh * yh, axis=-1, keepdims=True))).astype(BF16)

        @pl.when(i == 0)
        def _():
            loss_ref[...] = jnp.zeros(loss_ref.shape, F32)
            dgt_ref[...] = jnp.zeros(dgt_ref.shape, F32)
            dg_ref[...] = jnp.zeros(dg_ref.shape, F32)

        loss_ref[...] += jnp.broadcast_to(loss, loss_ref.shape)
        dgt_ref[...] += _colsum(dx2 * n)
        dg_ref[...] += _colsum(dn * yh)

    vec = pl.BlockSpec((1, d), lambda i: (0, 0))
    row = pl.BlockSpec((ts, d), lambda i: (i, 0))
    vshape = jax.ShapeDtypeStruct((1, d), F32)
    return pl.pallas_call(
        body, name="final_bwd", grid=(s_len // ts,),
        in_specs=[row, row, row, vec, vec],
        out_specs=[row, row, pl.BlockSpec((1, LANE), lambda i: (0, 0)), vec, vec],
        out_shape=[jax.ShapeDtypeStruct((s_len, d), BF16), jax.ShapeDtypeStruct((s_len, d), F32),
                   jax.ShapeDtypeStruct((1, LANE), F32), vshape, vshape],
        compiler_params=_cparams("arbitrary"),
    )(y, x1, tgt, gt, g_post)


def _ffn_act_bwd(d_act, u, up, w, ts, tc):
    s_len, f2 = up.shape
    nh = f2 // 2 // tc
    n_i = s_len // ts

    def gate(dv, uv):
        ua, ug = uv[:, :tc], uv[:, tc:]
        sg = _sigmoid(ug)
        return jnp.concatenate([dv * (ug * sg), dv * ua * (sg * (1.0 + ug * (1.0 - sg)))], axis=1)

    def body(d_ref, dh_ref, u_ref, uh_ref, x_ref, w_ref, dx_ref, dw_ref, db_ref):
        i = pl.program_id(1)
        du = gate(d_ref[...].astype(F32), u_ref[...].astype(F32))
        duh = jnp.where(i < n_i - 1, gate(dh_ref[...].astype(F32), uh_ref[...].astype(F32)), 0.0)
        du1, du2 = _shift_up(du, duh, 1), _shift_up(du, duh, 2)
        dx_ref[...] = (du * w_ref[2:3, :] + du1 * w_ref[1:2, :] + du2 * w_ref[0:1, :]).astype(BF16)

        @pl.when(i == 0)
        def _():
            dw_ref[...] = jnp.zeros(dw_ref.shape, F32)
            db_ref[...] = jnp.zeros(db_ref.shape, F32)

        xv = x_ref[...].astype(F32)
        dw_ref[0:1, :] += _colsum(du2 * xv)
        dw_ref[1:2, :] += _colsum(du1 * xv)
        dw_ref[2:3, :] += _colsum(du * xv)
        db_ref[...] += _colsum(du)

    pair = pl.BlockSpec((ts, 2 * tc), lambda j, i: (i, j))
    pair_halo = pl.BlockSpec((SUB, 2 * tc), _next_halo(ts, s_len, lambda j: j))
    return pl.pallas_call(
        body, name="ffn_act_bwd", grid=(nh, n_i),
        in_specs=[pl.BlockSpec((ts, tc), lambda j, i: (i, j)), pl.BlockSpec((SUB, tc), _next_halo(ts, s_len, lambda j: j)),
                  pair, pair_halo, pair, pl.BlockSpec((3, 2 * tc), lambda j, i: (0, j))],
        out_specs=[pair, pl.BlockSpec((3, 2 * tc), lambda j, i: (0, j)), pl.BlockSpec((1, 2 * tc), lambda j, i: (0, j))],
        out_shape=[jax.ShapeDtypeStruct((s_len, f2), BF16), jax.ShapeDtypeStruct((3, f2), F32),
                   jax.ShapeDtypeStruct((1, f2), F32)],
        compiler_params=_cparams("parallel", "arbitrary"),
    )(d_act, d_act, u, u, up, w)


def _mid_bwd(dh2, x1, dx2, mix, g_pre, sc, gt_m, g_post, ts):
    s_len, d = x1.shape

    def body(dh_ref, x1_ref, dx2_ref, mix_ref, g_ref, sc_ref, gt_ref, gp_ref,
             dx1_ref, dmix_ref, dsh_ref, dsc_ref, dg_ref, dgt_ref, dgp_ref):
        i = pl.program_id(0)
        dh = dh_ref[...]
        x1 = x1_ref[...]
        r1 = _rsq(x1)
        xh = x1 * r1
        dxh = dh * (1.0 + sc_ref[...]) * g_ref[...]
        dx1 = dx2_ref[...] + r1 * (dxh - xh * jnp.mean(dxh * xh, axis=-1, keepdims=True))
        dx1_ref[...] = dx1
        mv = mix_ref[...]
        rm = _rsq(mv)
        mh = mv * rm
        dn = dx1 * gt_ref[...]
        dmh = dn * gp_ref[...]
        dmix_ref[...] = (rm * (dmh - mh * jnp.mean(dmh * mh, axis=-1, keepdims=True))).astype(BF16)

        @pl.when(i == 0)
        def _():
            for ref in (dsh_ref, dsc_ref, dg_ref, dgt_ref, dgp_ref):
                ref[...] = jnp.zeros(ref.shape, F32)

        dsh_ref[...] += _colsum(dh)
        dsc_ref[...] += _colsum(dh * (xh * g_ref[...]))
        dg_ref[...] += _colsum(dh * (1.0 + sc_ref[...]) * xh)
        dgt_ref[...] += _colsum(dx1 * (mh * gp_ref[...]))
        dgp_ref[...] += _colsum(dn * mh)

    vec = pl.BlockSpec((1, d), lambda i: (0, 0))
    row = pl.BlockSpec((ts, d), lambda i: (i, 0))
    vshape = jax.ShapeDtypeStruct((1, d), F32)
    return pl.pallas_call(
        body, name="mid_bwd", grid=(s_len // ts,),
        in_specs=[row, row, row, row, vec, vec, vec, vec],
        out_specs=[row, row, vec, vec, vec, vec, vec],
        out_shape=[jax.ShapeDtypeStruct((s_len, d), F32), jax.ShapeDtypeStruct((s_len, d), BF16)] + [vshape] * 5,
        compiler_params=_cparams("arbitrary"),
    )(dh2, x1, dx2, mix, g_pre, sc, gt_m, g_post)


def _first_bwd(dh1, x, dx1, g_pre, sc, ts):
    s_len, d = x.shape

    def body(dh_ref, x_ref, dx1_ref, g_ref, sc_ref, dx_ref, dsh_ref, dsc_ref, dg_ref):
        i = pl.program_id(0)
        dh = dh_ref[...]
        xv = x_ref[...]
        r = _rsq(xv)
        xh = xv * r
        dxh = dh * (1.0 + sc_ref[...]) * g_ref[...]
        dx_ref[...] = dx1_ref[...] + r * (dxh - xh * jnp.mean(dxh * xh, axis=-1, keepdims=True))

        @pl.when(i == 0)
        def _():
            for ref in (dsh_ref, dsc_ref, dg_ref):
                ref[...] = jnp.zeros(ref.shape, F32)

        dsh_ref[...] += _colsum(dh)
        dsc_ref[...] += _colsum(dh * (xh * g_ref[...]))
        dg_ref[...] += _colsum(dh * (1.0 + sc_ref[...]) * xh)

    vec = pl.BlockSpec((1, d), lambda i: (0, 0))
    row = pl.BlockSpec((ts, d), lambda i: (i, 0))
    vshape = jax.ShapeDtypeStruct((1, d), F32)
    return pl.pallas_call(
        body, name="first_bwd", grid=(s_len // ts,),
        in_specs=[row, row, row, vec, vec], out_specs=[row, vec, vec, vec],
        out_shape=[jax.ShapeDtypeStruct((s_len, d), F32), vshape, vshape, vshape],
        compiler_params=_cparams("arbitrary"),
    )(dh1, x, dx1, g_pre, sc)


def _gconv_bwd(d_mixcat, proj, lay, w, b, ts, tc):
    s_len = proj.shape[0]
    cw = w.shape[1]
    n_i = s_len // ts
    dc_off = d_mixcat.shape[1] - cw

    def body(dc_ref, dch_ref, gb_ref, gbh_ref, gc_ref, gch_ref, ci_ref, cih_ref, w_ref, b_ref,
             dgb_ref, dgc_ref, dci_ref, dw_ref, db_ref):
        i = pl.program_id(1)
        gc, ci = gc_ref[...].astype(F32), ci_ref[...].astype(F32)
        p = gc * ci
        ph = jnp.where(i > 0, gch_ref[...].astype(F32) * cih_ref[...].astype(F32), 0.0)
        pm1, pm2 = _shift_down(p, ph, 1), _shift_down(p, ph, 2)
        z = pm2 * w_ref[0:1, :] + pm1 * w_ref[1:2, :] + p * w_ref[2:3, :] + b_ref[...]
        dc = dc_ref[...].astype(F32)
        dgb_ref[...] = (dc * z).astype(BF16)
        dz = dc * gb_ref[...].astype(F32)
        dzh = jnp.where(i < n_i - 1, dch_ref[...].astype(F32) * gbh_ref[...].astype(F32), 0.0)
        dz1, dz2 = _shift_up(dz, dzh, 1), _shift_up(dz, dzh, 2)
        dp = dz * w_ref[2:3, :] + dz1 * w_ref[1:2, :] + dz2 * w_ref[0:1, :]
        dgc_ref[...] = (dp * ci).astype(BF16)
        dci_ref[...] = (dp * gc).astype(BF16)

        @pl.when(i == 0)
        def _():
            dw_ref[...] = jnp.zeros(dw_ref.shape, F32)
            db_ref[...] = jnp.zeros(db_ref.shape, F32)

        dw_ref[0:1, :] += _colsum(dz2 * p)
        dw_ref[1:2, :] += _colsum(dz1 * p)
        dw_ref[2:3, :] += _colsum(dz * p)
        db_ref[...] += _colsum(dz)

    def blk(off):
        return pl.BlockSpec((ts, tc), lambda j, i: (i, off // tc + j))

    def prev(off):
        return pl.BlockSpec((SUB, tc), _prev_halo(ts, lambda j: off // tc + j))

    def nxt(off):
        return pl.BlockSpec((SUB, tc), _next_halo(ts, s_len, lambda j: off // tc + j))

    out_blk = pl.BlockSpec((ts, tc), lambda j, i: (i, j))
    act = jax.ShapeDtypeStruct((s_len, cw), BF16)
    return pl.pallas_call(
        body, name="gconv_bwd", grid=(cw // tc, n_i),
        in_specs=[blk(dc_off), nxt(dc_off), blk(lay["gb"]), nxt(lay["gb"]), blk(lay["gc"]), prev(lay["gc"]),
                  blk(lay["ci"]), prev(lay["ci"]),
                  pl.BlockSpec((3, tc), lambda j, i: (0, j)), pl.BlockSpec((1, tc), lambda j, i: (0, j))],
        out_specs=[out_blk, out_blk, out_blk,
                   pl.BlockSpec((3, tc), lambda j, i: (0, j)), pl.BlockSpec((1, tc), lambda j, i: (0, j))],
        out_shape=[act, act, act, jax.ShapeDtypeStruct((3, cw), F32), jax.ShapeDtypeStruct((1, cw), F32)],
        compiler_params=_cparams("parallel", "arbitrary"),
    )(d_mixcat, d_mixcat, proj, proj, proj, proj, proj, proj, w, b)


def _delta(o, d_mixcat, n_heads, ts):
    s_len = o.shape[0]

    def body(o_ref, do_ref, out_ref):
        for h in range(n_heads):
            sl = slice(h * V_DIM, (h + 1) * V_DIM)
            prod = o_ref[:, sl].astype(F32) * do_ref[:, sl].astype(F32)
            out_ref[h] = jnp.broadcast_to(jnp.sum(prod, axis=1, keepdims=True), (ts, LANE))

    hv = n_heads * V_DIM
    return pl.pallas_call(
        body, name="attn_delta", grid=(s_len // ts,),
        in_specs=[pl.BlockSpec((ts, hv), lambda i: (i, 0)), pl.BlockSpec((ts, hv), lambda i: (i, 0))],
        out_specs=pl.BlockSpec((n_heads, ts, LANE), lambda i: (0, i, 0)),
        out_shape=jax.ShapeDtypeStruct((n_heads, s_len, LANE), F32),
        compiler_params=_cparams("parallel"),
    )(o, d_mixcat)


def _flash_bwd(q, k, v, d_mixcat, lse_row, delta_row, n_heads, t, scale):
    s_len = q.shape[0]
    nb = s_len // t
    pairs = [(j, i) for j in range(nb) for i in range(j, nb)]
    jtab = jnp.asarray(np.array([p[0] for p in pairs], np.int32))
    itab = jnp.asarray(np.array([p[1] for p in pairs], np.int32))
    n_steps = len(pairs)

    def body(jt_ref, it_ref, q_ref, k_ref, v_ref, do_ref, lse_ref, dl_ref, dq_ref, dk_ref, dv_ref,
             dq_acc, dk_acc, dv_acc):
        step_id = pl.program_id(1)
        j, i = jt_ref[step_id], it_ref[step_id]

        @pl.when(step_id == 0)
        def _():
            dq_acc[...] = jnp.zeros(dq_acc.shape, F32)

        @pl.when(i == j)
        def _():
            dk_acc[...] = jnp.zeros(dk_acc.shape, F32)
            dv_acc[...] = jnp.zeros(dv_acc.shape, F32)

        def step(diag):
            qv, kv, vv, dov = q_ref[...], k_ref[...], v_ref[...], do_ref[...]
            s_t = lax.dot_general(kv, qv, NT_DIMS, preferred_element_type=F32)
            if diag:
                krow = lax.broadcasted_iota(jnp.int32, s_t.shape, 0)
                qcol = lax.broadcasted_iota(jnp.int32, s_t.shape, 1)
                s_t = jnp.where(krow <= qcol, s_t, NEG)
            p_t = jnp.exp(s_t - lse_ref[0])
            dv_acc[...] += jnp.dot(p_t.astype(BF16), dov, preferred_element_type=F32)
            dp_t = lax.dot_general(vv, dov, NT_DIMS, preferred_element_type=F32)
            ds_t = (p_t * (dp_t - dl_ref[0])).astype(BF16)
            dk_acc[...] += jnp.dot(ds_t, qv, preferred_element_type=F32)
            rows = pl.ds(pl.multiple_of(i * t, t), t)
            dq_acc[rows, :] += lax.dot_general(ds_t, kv, TN_DIMS, preferred_element_type=F32)

        @pl.when(i > j)
        def _():
            step(False)

        @pl.when(i == j)
        def _():
            step(True)

        @pl.when(i == nb - 1)
        def _():
            dk_ref[...] = dk_acc[...].astype(BF16)
            dv_ref[...] = dv_acc[...].astype(BF16)

        @pl.when(step_id == n_steps - 1)
        def _():
            dq_ref[...] = (dq_acc[...] * scale).astype(BF16)

    hv = n_heads * V_DIM
    do_off = 0
    grid_spec = pltpu.PrefetchScalarGridSpec(
        num_scalar_prefetch=2, grid=(n_heads, n_steps),
        in_specs=[pl.BlockSpec((t, HEAD_W), lambda h, s, jt, it: (it[s], h)),
                  pl.BlockSpec((t, HEAD_W), lambda h, s, jt, it: (jt[s], h)),
                  pl.BlockSpec((t, V_DIM), lambda h, s, jt, it: (jt[s], h)),
                  pl.BlockSpec((t, V_DIM), lambda h, s, jt, it: (it[s], do_off + h)),
                  pl.BlockSpec((1, 1, t), lambda h, s, jt, it: (h, 0, it[s])),
                  pl.BlockSpec((1, 1, t), lambda h, s, jt, it: (h, 0, it[s]))],
        out_specs=[pl.BlockSpec((s_len, HEAD_W), lambda h, s, jt, it: (0, h)),
                   pl.BlockSpec((t, HEAD_W), lambda h, s, jt, it: (jt[s], h)),
                   pl.BlockSpec((t, V_DIM), lambda h, s, jt, it: (jt[s], h))],
        scratch_shapes=[pltpu.VMEM((s_len, HEAD_W), F32), pltpu.VMEM((t, HEAD_W), F32), pltpu.VMEM((t, V_DIM), F32)],
    )
    return pl.pallas_call(
        body, name="flash_bwd", grid_spec=grid_spec,
        out_shape=[jax.ShapeDtypeStruct((s_len, n_heads * HEAD_W), BF16),
                   jax.ShapeDtypeStruct((s_len, n_heads * HEAD_W), BF16),
                   jax.ShapeDtypeStruct((s_len, hv), BF16)],
        compiler_params=_cparams("parallel", "arbitrary"),
    )(jtab, itab, q, k, v, d_mixcat, lse_row, delta_row)


def _qkv_bwd(dq, dk, dv, proj, lay, wuq, wk, wv, g_q, g_kv, ctab, atab, btab, n_heads, ts):
    s_len = proj.shape[0]
    ql_w, kl_w = wuq.shape[0], wk.shape[0]
    tail_w = lay["np"] - lay["ql"]
    kv_o, kr_o = lay["kv"] - lay["ql"], lay["kr"] - lay["ql"]

    def body(dq_ref, dk_ref, dv_ref, ql_ref, kl_ref, wuq_ref, wk_ref, wv_ref, gq_ref, gkv_ref, c_ref, a_ref, b_ref,
             dqr_ref, dkn_ref, tail_ref, dgq_ref, dgkv_ref):
        i = pl.program_id(0)
        c, a, b = c_ref[...], a_ref[...], b_ref[...]
        dkr = jnp.zeros((ts, LANE), F32)
        for h in range(n_heads):
            o = h * HEAD_W
            dqr_ref[:, o:o + QK_NOPE] = dq_ref[:, o:o + QK_NOPE]
            dqr_ref[:, o + QK_NOPE:o + HEAD_W] = _rope_t(dq_ref[:, o + QK_NOPE:o + HEAD_W].astype(F32), c, a, b).astype(BF16)
            dkn_ref[:, h * QK_NOPE:(h + 1) * QK_NOPE] = dk_ref[:, o:o + QK_NOPE]
            dkr = dkr + dk_ref[:, o + QK_NOPE:o + HEAD_W].astype(F32)
        tail_ref[...] = jnp.zeros(tail_ref.shape, BF16)
        tail_ref[:, kr_o:kr_o + LANE] = _rope_t(dkr, c, a, b).astype(BF16)

        def rms_bwd(lat_ref, dn, g_ref):
            lat = lat_ref[...].astype(F32)
            r = _rsq(lat)
            xh = lat * r
            dxh = dn * g_ref[...]
            return r * (dxh - xh * jnp.mean(dxh * xh, axis=-1, keepdims=True)), _colsum(dn * xh)

        dqn = lax.dot_general(dqr_ref[...], wuq_ref[...], NT_DIMS, preferred_element_type=F32)
        d_ql, dgq = rms_bwd(ql_ref, dqn, gq_ref)
        tail_ref[:, 0:ql_w] = d_ql.astype(BF16)
        dkvn = (lax.dot_general(dkn_ref[...], wk_ref[...], NT_DIMS, preferred_element_type=F32)
                + lax.dot_general(dv_ref[...], wv_ref[...], NT_DIMS, preferred_element_type=F32))
        d_kl, dgkv = rms_bwd(kl_ref, dkvn, gkv_ref)
        tail_ref[:, kv_o:kv_o + kl_w] = d_kl.astype(BF16)

        @pl.when(i == 0)
        def _():
            dgq_ref[...] = jnp.zeros(dgq_ref.shape, F32)
            dgkv_ref[...] = jnp.zeros(dgkv_ref.shape, F32)

        dgq_ref[...] += dgq
        dgkv_ref[...] += dgkv

    def full(arr):
        return pl.BlockSpec(arr.shape, lambda i: (0, 0))

    def rows(w):
        return pl.BlockSpec((ts, w), lambda i: (i, 0))

    tab = pl.BlockSpec((ts, LANE), lambda i: (i, 0))
    hw, hv, hn = n_heads * HEAD_W, n_heads * V_DIM, n_heads * QK_NOPE
    return pl.pallas_call(
        body, name="qkv_bwd", grid=(s_len // ts,),
        in_specs=[rows(hw), rows(hw), rows(hv),
                  pl.BlockSpec((ts, ql_w), lambda i: (i, lay["ql"] // ql_w)),
                  pl.BlockSpec((ts, kl_w), lambda i: (i, lay["kv"] // kl_w)),
                  full(wuq), full(wk), full(wv), full(g_q), full(g_kv), tab, tab, tab],
        out_specs=[rows(hw), rows(hn), rows(tail_w), full(g_q), full(g_kv)],
        out_shape=[jax.ShapeDtypeStruct((s_len, hw), BF16), jax.ShapeDtypeStruct((s_len, hn), BF16),
                   jax.ShapeDtypeStruct((s_len, tail_w), BF16),
                   jax.ShapeDtypeStruct(g_q.shape, F32), jax.ShapeDtypeStruct(g_kv.shape, F32)],
        compiler_params=_cparams("arbitrary"),
    )(dq, dk, dv, proj, proj, wuq, wk, wv, g_q, g_kv, ctab, atab, btab)


def _adamw(w, g, m, v):
    m = ADAM_B1 * m + (1.0 - ADAM_B1) * g
    v = ADAM_B2 * v + (1.0 - ADAM_B2) * (g * g)
    m_hat = m / (1.0 - ADAM_B1 ** ADAM_STEP)
    v_hat = v / (1.0 - ADAM_B2 ** ADAM_STEP)
    delta = -ADAM_LR * (m_hat / (jnp.sqrt(v_hat) + ADAM_EPS) + ADAM_WD * w)
    return delta, m, v


def _adam_parts(parts, w, m, v, tr, name):
    r, c = w.shape
    tr = _tile(r, tr, SUB)

    def body(p_ref, w_ref, m_ref, v_ref, g_out, d_out, m_out, v_out):
        g = p_ref[0].astype(F32)
        for dev in range(1, N_DEV):
            g = g + p_ref[dev].astype(F32)
        g_out[...] = g
        d_out[...], m_out[...], v_out[...] = _adamw(w_ref[...], g, m_ref[...], v_ref[...])

    blk = pl.BlockSpec((tr, c), lambda i: (i, 0))
    shp = jax.ShapeDtypeStruct((r, c), F32)
    return pl.pallas_call(
        body, name=name, grid=(r // tr,),
        in_specs=[pl.BlockSpec((N_DEV, tr, c), lambda i: (0, i, 0)), blk, blk, blk],
        out_specs=[blk, blk, blk, blk], out_shape=[shp, shp, shp, shp],
        compiler_params=_cparams("parallel"),
    )(parts, w, m, v)


def _adam_ada(cact_t, dmod_sh, w, m, v, tr):
    r, c = w.shape

    def body(ct_ref, dm_ref, w_ref, m_ref, v_ref, g_out, d_out, m_out, v_out):
        g = jnp.dot(ct_ref[...], dm_ref[...], preferred_element_type=F32, precision=lax.Precision.HIGHEST)
        g_out[...] = g
        d_out[...], m_out[...], v_out[...] = _adamw(w_ref[...], g, m_ref[...], v_ref[...])

    blk = pl.BlockSpec((tr, c), lambda i: (i, 0))
    shp = jax.ShapeDtypeStruct((r, c), F32)
    return pl.pallas_call(
        body, name="adam_ada", grid=(r // tr,),
        in_specs=[pl.BlockSpec((tr, N_DEV), lambda i: (i, 0)), pl.BlockSpec((N_DEV, c), lambda i: (0, 0)), blk, blk, blk],
        out_specs=[blk, blk, blk, blk], out_shape=[shp, shp, shp, shp],
        compiler_params=_cparams("parallel"),
    )(cact_t, dmod_sh, w, m, v)


def _adam_small(v_all, w, m, v):
    n = w.shape[1]

    def body(p_ref, w_ref, m_ref, v_ref, g_out, d_out, m_out, v_out):
        g = p_ref[0:1, :]
        for dev in range(1, N_DEV):
            g = g + p_ref[dev:dev + 1, :]
        g_out[...] = g
        d_out[...], m_out[...], v_out[...] = _adamw(w_ref[...], g, m_ref[...], v_ref[...])

    shp = jax.ShapeDtypeStruct((1, n), F32)
    vm = pl.BlockSpec(memory_space=pltpu.VMEM)
    return pl.pallas_call(
        body, name="adam_small", in_specs=[vm, vm, vm, vm], out_specs=[vm, vm, vm, vm],
        out_shape=[shp, shp, shp, shp], compiler_params=_cparams(),
    )(v_all, w, m, v)


def _my_place():
    return lax.axis_index("x"), lax.axis_index("y"), lax.axis_index("c")


def _peer(place, k):
    x, y, c = place
    return (x ^ (k >> 2), y ^ ((k >> 1) & 1), c ^ (k & 1))


def _index(place):
    return 4 * place[0] + 2 * place[1] + place[2]


def _ada_fwd(vec, w_ada, b_ada_rows):
    lv = vec.shape[1]
    d, c = w_ada.shape

    def body(vec_ref, w_ref, b_ref, gath_ref, cact_ref, mod_ref, modsh, send_a, recv_a, send_b, recv_b, local_s):
        me = _my_place()
        my_i = _index(me)

        def gather_copy(k, to, src_row):
            row = gath_ref.at[pl.ds(src_row, 1), :]
            return pltpu.make_async_remote_copy(src_ref=row, dst_ref=row, send_sem=send_a.at[k], recv_sem=recv_a.at[k],
                                                device_id=to, device_id_type=MESH)

        own = pltpu.make_async_copy(vec_ref, gath_ref.at[pl.ds(my_i, 1), :], local_s.at[0])
        own.start()
        own.wait()
        sends = [gather_copy(k, _peer(me, k), my_i) for k in range(1, N_DEV)]
        for cp in sends:
            cp.start()
        for k in range(1, N_DEV):
            gather_copy(k, me, _index(_peer(me, k))).wait_recv()
        for cp in sends:
            cp.wait_send()

        c_all = gath_ref[:, 0:d]
        cact = c_all * _sigmoid(c_all)
        cact_ref[...] = cact
        modsh[...] = jnp.dot(cact, w_ref[...], preferred_element_type=F32, precision=lax.Precision.HIGHEST)

        def mod_copy(k, to, src_row, dst_row):
            return pltpu.make_async_remote_copy(src_ref=modsh.at[pl.ds(src_row, 1), :], dst_ref=mod_ref.at[pl.ds(dst_row, 1), :],
                                                send_sem=send_b.at[k], recv_sem=recv_b.at[k],
                                                device_id=to, device_id_type=MESH)

        own = pltpu.make_async_copy(modsh.at[pl.ds(my_i, 1), :], mod_ref.at[pl.ds(my_i, 1), :], local_s.at[1])
        own.start()
        sends = [mod_copy(k, _peer(me, k), _index(_peer(me, k)), my_i) for k in range(1, N_DEV)]
        for cp in sends:
            cp.start()
        for k in range(1, N_DEV):
            mod_copy(k, me, my_i, _index(_peer(me, k))).wait_recv()
        for cp in sends:
            cp.wait_send()
        own.wait()
        mod_ref[...] = mod_ref[...] + b_ref[...]

    vm = pl.BlockSpec(memory_space=pltpu.VMEM)
    return pl.pallas_call(
        body, name="ada_fwd", in_specs=[vm, vm, vm], out_specs=[vm, vm, vm],
        out_shape=[jax.ShapeDtypeStruct((N_DEV, lv), F32), jax.ShapeDtypeStruct((N_DEV, d), F32),
                   jax.ShapeDtypeStruct((N_DEV, c), F32)],
        scratch_shapes=[pltpu.VMEM((N_DEV, c), F32)] + [pltpu.SemaphoreType.DMA((N_DEV,))] * 4
        + [pltpu.SemaphoreType.DMA((2,))],
        compiler_params=pltpu.CompilerParams(vmem_limit_bytes=VMEM_LIMIT),
    )(vec, w_ada, b_ada_rows)


def _gather_small(vec):
    lv = vec.shape[1]

    def body(vec_ref, gath_ref, send_s, recv_s, local_s):
        me = _my_place()
        my_i = _index(me)

        def copy(k, to, src_row):
            row = gath_ref.at[pl.ds(src_row, 1), :]
            return pltpu.make_async_remote_copy(src_ref=row, dst_ref=row, send_sem=send_s.at[k], recv_sem=recv_s.at[k],
                                                device_id=to, device_id_type=MESH)

        own = pltpu.make_async_copy(vec_ref, gath_ref.at[pl.ds(my_i, 1), :], local_s)
        own.start()
        own.wait()
        sends = [copy(k, _peer(me, k), my_i) for k in range(1, N_DEV)]
        for cp in sends:
            cp.start()
        for k in range(1, N_DEV):
            copy(k, me, _index(_peer(me, k))).wait_recv()
        for cp in sends:
            cp.wait_send()

    vm = pl.BlockSpec(memory_space=pltpu.VMEM)
    return pl.pallas_call(
        body, name="gather_small", in_specs=[vm], out_specs=vm,
        out_shape=jax.ShapeDtypeStruct((N_DEV, lv), F32),
        scratch_shapes=[pltpu.SemaphoreType.DMA((N_DEV,))] * 2 + [pltpu.SemaphoreType.DMA],
        compiler_params=pltpu.CompilerParams(vmem_limit_bytes=VMEM_LIMIT),
    )(vec)


def _gather_weights(shards):
    n = len(shards)
    per = N_DEV - 1

    def body(*refs):
        ins, outs = refs[:n], refs[n:2 * n]
        send_s, recv_s, local_s = refs[2 * n:]
        me = _my_place()
        x, y, c = me
        sibling = (x, y, 1 - c)
        chips = [(1 - x, y), (x, 1 - y), (1 - x, 1 - y)]

        def copy(a, k, block, to, src=None):
            slot = outs[a].at[_index(block)]
            return pltpu.make_async_remote_copy(src_ref=slot if src is None else src, dst_ref=slot,
                                                send_sem=send_s.at[a * per + k], recv_sem=recv_s.at[a * per + k],
                                                device_id=to, device_id_type=MESH)

        mine = [pltpu.make_async_copy(ins[a], outs[a].at[_index(me)], local_s.at[a]) for a in range(n)]
        for cp in mine:
            cp.start()
        first = []
        for a in range(n):
            first.append(copy(a, 0, me, sibling, src=ins[a]))
            first += [copy(a, 1 + j, me, (*chip, c), src=ins[a]) for j, chip in enumerate(chips)]
        for cp in first:
            cp.start()
        passed = []
        for j, chip in enumerate(chips):
            for a in range(n):
                copy(a, 1 + j, (*chip, c), me).wait_recv()
                fwd = copy(a, 4 + j, (*chip, c), sibling)
                fwd.start()
                passed.append(fwd)
        for a in range(n):
            copy(a, 0, sibling, me).wait_recv()
            for j, chip in enumerate(chips):
                copy(a, 4 + j, (*chip, 1 - c), me).wait_recv()
        for cp in first + passed:
            cp.wait_send()
        for cp in mine:
            cp.wait()

    hbm = pl.BlockSpec(memory_space=pl.ANY)
    return pl.pallas_call(
        body, name="gather_weights", in_specs=[hbm] * n, out_specs=[hbm] * n,
        out_shape=[jax.ShapeDtypeStruct((N_DEV,) + s.shape, s.dtype) for s in shards],
        scratch_shapes=[pltpu.SemaphoreType.DMA((n * per,)), pltpu.SemaphoreType.DMA((n * per,)),
                        pltpu.SemaphoreType.DMA((n,))],
    )(*shards)


def _exchange_grads(chunks):
    n = len(chunks)
    per = N_DEV - 1

    def body(*refs):
        ins, outs = refs[:n], refs[n:2 * n]
        send_s, recv_s, local_s = refs[2 * n:]
        me = _my_place()
        my_i = _index(me)

        def copy(a, k, to, src_slot, dst_slot):
            return pltpu.make_async_remote_copy(src_ref=ins[a].at[src_slot], dst_ref=outs[a].at[dst_slot],
                                                send_sem=send_s.at[a * per + k - 1], recv_sem=recv_s.at[a * per + k - 1],
                                                device_id=to, device_id_type=MESH)

        mine = [pltpu.make_async_copy(ins[a].at[my_i], outs[a].at[my_i], local_s.at[a]) for a in range(n)]
        for cp in mine:
            cp.start()
        sends = [copy(a, k, _peer(me, k), _index(_peer(me, k)), my_i) for k in range(1, N_DEV) for a in range(n)]
        for cp in sends:
            cp.start()
        for k in range(1, N_DEV):
            for a in range(n):
                copy(a, k, me, my_i, _index(_peer(me, k))).wait_recv()
        for cp in sends:
            cp.wait_send()
        for cp in mine:
            cp.wait()

    hbm = pl.BlockSpec(memory_space=pl.ANY)
    return pl.pallas_call(
        body, name="exchange_grads", in_specs=[hbm] * n, out_specs=[hbm] * n,
        out_shape=[jax.ShapeDtypeStruct(s.shape, s.dtype) for s in chunks],
        scratch_shapes=[pltpu.SemaphoreType.DMA((n * per,)), pltpu.SemaphoreType.DMA((n * per,)),
                        pltpu.SemaphoreType.DMA((n,))],
    )(*chunks)


def _proj_layout(cw, ql, kl):
    lay = {"gb": 0, "gc": cw, "ci": 2 * cw, "ql": 3 * cw}
    assert lay["ql"] % ql == 0
    lay["kv"] = _roundup(lay["ql"] + ql, kl)
    lay["kr"] = lay["kv"] + kl
    lay["np"] = _roundup(lay["kr"] + LANE, 4 * LANE)
    return lay


def _chunks_cols(g):
    r, c8 = g.shape
    return jnp.transpose(g.reshape(r, N_DEV, c8 // N_DEV), (1, 0, 2))


def _from_col_shards(a):
    n, r, c = a.shape
    return jnp.transpose(a, (1, 0, 2)).reshape(r, n * c)


def kernel(x, c, positions, w_ada, b_ada, g_pre_mix, g_post_mix, w_in, g_q, w_uq, g_kv, w_ukv, conv_w_mix, conv_b_mix, w_o, g_pre_ffn, g_post_ffn, w_up, conv_w_ffn, conv_b_ffn, w_down, loss_target, m_w_ada, m_b_ada, m_g_pre_mix, m_g_post_mix, m_w_in, m_g_q, m_w_uq, m_g_kv, m_w_ukv, m_conv_w_mix, m_conv_b_mix, m_w_o, m_g_pre_ffn, m_g_post_ffn, m_w_up, m_conv_w_ffn, m_conv_b_ffn, m_w_down, v_w_ada, v_b_ada, v_g_pre_mix, v_g_post_mix, v_w_in, v_g_q, v_w_uq, v_g_kv, v_w_ukv, v_conv_w_mix, v_conv_b_mix, v_w_o, v_g_pre_ffn, v_g_post_ffn, v_w_up, v_conv_w_ffn, v_conv_b_ffn, v_w_down):
    s_len, d = x.shape[1], x.shape[2]
    ql, kl = w_uq.shape[1], w_ukv.shape[1]
    n_heads = w_ukv.shape[2] * N_DEV // (QK_NOPE + V_DIM)
    cw = conv_w_mix.shape[2] * N_DEV
    f2 = w_up.shape[2] * N_DEV
    ff = f2 // 2
    in_cols = w_in.shape[2] * N_DEV
    ada_c = w_ada.shape[2]
    cwm_c, cwf_c = conv_w_mix.shape[2], conv_w_ffn.shape[2]
    scale = 1.0 / math.sqrt(QK_NOPE + QK_ROPE)
    lay = _proj_layout(cw, ql, kl)
    n_pad = lay["np"]
    my_i = _index(_my_place())

    ts_row = _tile(s_len, PREF["row"], SUB)
    ts_conv = _tile(s_len, PREF["conv_rows"], SUB)
    tc_conv = _tile(cw, PREF["conv_cols"])
    tc_ffn = _tile(ff, PREF["conv_cols"])
    ts_qkv = _tile(s_len, PREF["row"], SUB)
    t_attn = _tile(s_len, PREF["attn"])

    x2d, tgt = x[0], loss_target[0]

    vec = jnp.concatenate([c, conv_w_mix[0].reshape(1, -1), conv_w_ffn[0].reshape(1, -1)], axis=1)
    gath, cact, mod_rows = _ada_fwd(vec, w_ada[0], b_ada.reshape(N_DEV, ada_c))
    cwm_full = _from_col_shards(gath[:, d:d + 3 * cwm_c].reshape(N_DEV, 3, cwm_c))
    cwf_full = _from_col_shards(gath[:, d + 3 * cwm_c:].reshape(N_DEV, 3, cwf_c))
    mod = mod_rows.reshape(1, N_DEV * ada_c)
    sh_m, sc_m, gt_m, sh_f, sc_f, gt_f = [mod[:, k * d:(k + 1) * d] for k in range(6)]

    g_in, g_uq, g_ukv, g_o, g_up, g_down = _gather_weights(
        [w_in[0].astype(BF16), w_uq[0].astype(BF16), w_ukv[0].astype(BF16), w_o[0].astype(BF16),
         w_up[0].astype(BF16), w_down[0].astype(BF16)])
    win = _from_col_shards(g_in)
    cut = np.cumsum([0, ql, kl, QK_ROPE, cw, cw, cw])
    part = [win[:, cut[k]:cut[k + 1]] for k in range(6)]

    def zcols(n):
        return jnp.zeros((d, n), BF16)

    win_p = jnp.concatenate([part[3], part[4], part[5], part[0], zcols(lay["kv"] - lay["ql"] - ql), part[1],
                             part[2], zcols(n_pad - lay["kr"] - QK_ROPE)], axis=1)
    wuq_p = jnp.pad(_from_col_shards(g_uq).reshape(ql, n_heads, QK_NOPE + QK_ROPE),
                    ((0, 0), (0, 0), (0, HEAD_W - QK_NOPE - QK_ROPE))).reshape(ql, n_heads * HEAD_W)
    wukv = _from_col_shards(g_ukv).reshape(kl, n_heads, QK_NOPE + V_DIM)
    wk = wukv[:, :, :QK_NOPE].reshape(kl, n_heads * QK_NOPE)
    wv = wukv[:, :, QK_NOPE:].reshape(kl, n_heads * V_DIM)
    wo = g_o.reshape(d, d)
    nh_ffn = ff // tc_ffn

    def pair_cols(a):
        return a.reshape(a.shape[0], 2, nh_ffn, tc_ffn).transpose(0, 2, 1, 3).reshape(a.shape[0], f2)

    def unpair_cols(a):
        return a.reshape(a.shape[0], nh_ffn, 2, tc_ffn).transpose(0, 2, 1, 3).reshape(a.shape[0], f2)

    wup = pair_cols(_from_col_shards(g_up))
    cwf_pair, cbf_pair = pair_cols(cwf_full), pair_cols(conv_b_ffn)
    wdown = g_down.reshape(ff, d)

    inv_freq = 1.0 / (ROPE_THETA ** (jnp.arange(0, QK_ROPE, 2, dtype=F32) / QK_ROPE))
    inv_row = jnp.tile(inv_freq, LANE // (QK_ROPE // 2)).reshape(1, LANE)
    ctab, atab, btab = _rope_tables(positions.astype(F32).reshape(s_len, 1), inv_row, _tile(s_len, 1024, SUB))

    h1 = _modnorm_fwd(x2d, g_pre_mix, sc_m, sh_m, ts_row)
    proj = _matmul(h1, win_p, out_dtype=BF16, tm=1024, tn=1280, tk=2048, name="mm_proj")
    q, k, v, qn, kvn = _qkv_fwd(proj, lay, wuq_p, wk, wv, g_q, g_kv, ctab, atab, btab, n_heads, ts_qkv, scale)
    attn, lse = _flash_fwd(q, k, v, n_heads, t_attn)
    conv_out = _gconv_fwd(proj, lay, cwm_full, conv_b_mix, ts_conv, tc_conv)
    mixcat = jnp.concatenate([attn, conv_out], axis=1)
    mix = _matmul(mixcat, wo, out_dtype=F32, tm=512, tn=2048, tk=2048, name="mm_mix")
    x1, h2 = _post_mix_fwd(mix, x2d, gt_m, g_post_mix, g_pre_ffn, sc_f, sh_f, ts_row)
    up = _matmul(h2, wup, out_dtype=BF16, tm=1024, tn=1408, tk=2048, name="mm_up")
    u, act = _act_fwd(up, cwf_pair, cbf_pair, ts_conv, tc_ffn)
    y = _matmul(act, wdown, out_dtype=F32, tm=1024, tn=512, tk=ff, name="mm_down")

    dy, dx2, loss_row, d_gt_f, dg_post_ffn = _final_bwd(y, x1, tgt, gt_f, g_post_ffn, ts_row)
    gw_down = _matmul(act, dy, ta=True, out_dtype=BF16, tm=1408, tn=1024, tk=2048, name="mm_gw_down")
    d_act = _matmul(dy, wdown, tb=True, out_dtype=BF16, tm=1024, tn=1408, tk=2048, name="mm_d_act")
    d_up, dcw_pair, dcb_pair = _ffn_act_bwd(d_act, u, up, cwf_pair, ts_conv, tc_ffn)
    dcw_ffn, dcb_ffn = unpair_cols(dcw_pair), unpair_cols(dcb_pair)
    gw_up = unpair_cols(_matmul(h2, d_up, ta=True, out_dtype=BF16, tm=1024, tn=1408, tk=2048, name="mm_gw_up"))
    dh2 = _matmul(d_up, wup, tb=True, out_dtype=F32, tm=512, tn=1024, tk=ff, name="mm_dh2")
    dx1, dmix, d_sh_f, d_sc_f, dg_pre_ffn, d_gt_m, dg_post_mix = _mid_bwd(
        dh2, x1, dx2, mix, g_pre_ffn, sc_f, gt_m, g_post_mix, ts_row)
    gw_o = _matmul(mixcat, dmix, ta=True, out_dtype=BF16, tm=1024, tn=1024, tk=2048, name="mm_gw_o")
    d_mixcat = _matmul(dmix, wo, tb=True, out_dtype=BF16, tm=1024, tn=1024, tk=2048, name="mm_d_mixcat")
    d_gb, d_gc, d_ci, dcw_mix, dcb_mix = _gconv_bwd(d_mixcat, proj, lay, cwm_full, conv_b_mix, ts_conv, tc_conv)
    delta = _delta(attn, d_mixcat, n_heads, _tile(s_len, 512, SUB))
    lse_row = lse[:, :, 0].reshape(n_heads, 1, s_len)
    delta_row = delta[:, :, 0].reshape(n_heads, 1, s_len)
    dq, dk, dv = _flash_bwd(q, k, v, d_mixcat, lse_row, delta_row, n_heads, t_attn, scale)
    dq_r, dkn, d_tail, dg_q, dg_kv = _qkv_bwd(dq, dk, dv, proj, lay, wuq_p, wk, wv, g_q, g_kv, ctab, atab, btab,
                                              n_heads, ts_qkv)
    gw_uq_p = _matmul(qn, dq_r, ta=True, out_dtype=BF16, tm=768, tn=1024, tk=2048, name="mm_gw_uq")
    gw_k = _matmul(kvn, dkn, ta=True, out_dtype=BF16, tm=512, tn=1024, tk=2048, name="mm_gw_k")
    gw_v = _matmul(kvn, dv, ta=True, out_dtype=BF16, tm=512, tn=1024, tk=2048, name="mm_gw_v")
    d_proj = jnp.concatenate([d_gb, d_gc, d_ci, d_tail], axis=1)
    gw_in_p = _matmul(h1, d_proj, ta=True, out_dtype=BF16, tm=1024, tn=1280, tk=2048, name="mm_gw_in")
    dh1 = _matmul(d_proj, win_p, tb=True, out_dtype=F32, tm=512, tn=1024, tk=n_pad, name="mm_dh1")
    grad_x, d_sh_m, d_sc_m, dg_pre_mix = _first_bwd(dh1, x2d, dx1, g_pre_mix, sc_m, ts_row)

    gw_in = jnp.concatenate([gw_in_p[:, lay["ql"]:lay["ql"] + ql], gw_in_p[:, lay["kv"]:lay["kv"] + kl],
                             gw_in_p[:, lay["kr"]:lay["kr"] + QK_ROPE], gw_in_p[:, :3 * cw]], axis=1)
    gw_uq = gw_uq_p.reshape(ql, n_heads, HEAD_W)[:, :, :QK_NOPE + QK_ROPE].reshape(ql, n_heads * (QK_NOPE + QK_ROPE))
    gw_ukv = jnp.concatenate([gw_k.reshape(kl, n_heads, QK_NOPE), gw_v.reshape(kl, n_heads, V_DIM)],
                             axis=2).reshape(kl, n_heads * (QK_NOPE + V_DIM))
    p_in, p_uq, p_ukv, p_o, p_up, p_down = _exchange_grads(
        [_chunks_cols(gw_in), _chunks_cols(gw_uq), _chunks_cols(gw_ukv), gw_o.reshape(N_DEV, d // N_DEV, d),
         _chunks_cols(gw_up), gw_down.reshape(N_DEV, ff // N_DEV, d)])

    dmod = jnp.concatenate([d_sh_m, d_sc_m, d_gt_m, d_sh_f, d_sc_f, d_gt_f], axis=1)
    small_g = [loss_row, dmod, dg_pre_mix, dg_post_mix, dg_q, dg_kv, dcb_mix, dg_pre_ffn, dg_post_ffn, dcb_ffn,
               dcw_mix.reshape(1, -1), dcw_ffn.reshape(1, -1)]
    v_all = _gather_small(jnp.concatenate(small_g, axis=1))

    def place_shard(shard):
        full = jnp.zeros((3, N_DEV * shard.shape[2]), F32)
        return lax.dynamic_update_slice(full, shard[0], (0, my_i * shard.shape[2])).reshape(1, -1)

    def pack(b_ada_, g_pre_mix_, g_post_mix_, g_q_, g_kv_, cb_mix_, g_pre_ffn_, g_post_ffn_, cb_ffn_, cw_mix_, cw_ffn_):
        return jnp.concatenate([jnp.zeros((1, LANE), F32), b_ada_, g_pre_mix_, g_post_mix_, g_q_, g_kv_, cb_mix_,
                                g_pre_ffn_, g_post_ffn_, cb_ffn_, cw_mix_, cw_ffn_], axis=1)

    w_small = pack(b_ada, g_pre_mix, g_post_mix, g_q, g_kv, conv_b_mix, g_pre_ffn, g_post_ffn, conv_b_ffn,
                   cwm_full.reshape(1, -1), cwf_full.reshape(1, -1))
    m_small = pack(m_b_ada, m_g_pre_mix, m_g_post_mix, m_g_q, m_g_kv, m_conv_b_mix, m_g_pre_ffn, m_g_post_ffn,
                   m_conv_b_ffn, place_shard(m_conv_w_mix), place_shard(m_conv_w_ffn))
    v_small = pack(v_b_ada, v_g_pre_mix, v_g_post_mix, v_g_q, v_g_kv, v_conv_b_mix, v_g_pre_ffn, v_g_post_ffn,
                   v_conv_b_ffn, place_shard(v_conv_w_mix), place_shard(v_conv_w_ffn))
    small_out = _adam_small(v_all, w_small, m_small, v_small)

    sizes = [LANE, 6 * d, d, d, ql, kl, cw, d, d, f2, 3 * cw, 3 * f2]
    offs = np.cumsum([0] + sizes)

    def unpack(row):
        parts = [row[:, offs[k]:offs[k + 1]] for k in range(len(sizes))]

        def shard_of(flat, width):
            return lax.dynamic_slice(flat.reshape(3, N_DEV * width), (0, my_i * width), (3, width))[None]

        named = dict(loss=parts[0][0, 0], b_ada=parts[1], g_pre_mix=parts[2], g_post_mix=parts[3], g_q=parts[4],
                     g_kv=parts[5], conv_b_mix=parts[6], g_pre_ffn=parts[7], g_post_ffn=parts[8], conv_b_ffn=parts[9],
                     conv_w_mix=shard_of(parts[10], cwm_c), conv_w_ffn=shard_of(parts[11], cwf_c))
        return named

    small = [unpack(r) for r in small_out]

    dmod_all = v_all[:, offs[1]:offs[2]]
    dmod_sh = lax.dynamic_slice(dmod_all, (0, my_i * ada_c), (N_DEV, ada_c))
    cact_t = jnp.transpose(cact)
    big = dict(
        w_ada=_adam_ada(cact_t, dmod_sh, w_ada[0], m_w_ada[0], v_w_ada[0], _tile(d, 256, SUB)),
        w_in=_adam_parts(p_in, w_in[0], m_w_in[0], v_w_in[0], 256, "adam_w_in"),
        w_uq=_adam_parts(p_uq, w_uq[0], m_w_uq[0], v_w_uq[0], 256, "adam_w_uq"),
        w_ukv=_adam_parts(p_ukv, w_ukv[0], m_w_ukv[0], v_w_ukv[0], 256, "adam_w_ukv"),
        w_o=_adam_parts(p_o, w_o[0], m_w_o[0], v_w_o[0], 128, "adam_w_o"),
        w_up=_adam_parts(p_up, w_up[0], m_w_up[0], v_w_up[0], 256, "adam_w_up"),
        w_down=_adam_parts(p_down, w_down[0], m_w_down[0], v_w_down[0], 176, "adam_w_down"),
    )

    names = ["w_ada", "b_ada", "g_pre_mix", "g_post_mix", "w_in", "g_q", "w_uq", "g_kv", "w_ukv", "conv_w_mix",
             "conv_b_mix", "w_o", "g_pre_ffn", "g_post_ffn", "w_up", "conv_w_ffn", "conv_b_ffn", "w_down"]
    outs = [small[0]["loss"], grad_x[None]]
    for kind in range(4):
        for nm in names:
            outs.append(big[nm][kind][None] if nm in big else small[kind][nm])
    return tuple(outs)
```

```python
import math

import numpy as np
import jax
import jax.numpy as jnp
from jax import lax
from jax.experimental import pallas as pl
from jax.experimental.pallas import tpu as pltpu

F32 = jnp.float32
BF16 = jnp.bfloat16
N_DEV = 8
MESH = pl.DeviceIdType.MESH

QK_NOPE = 128
QK_ROPE = 64
V_DIM = 128
HEAD_W = 256
LANE = 128
SUB = 8
RMS_EPS = 1e-6
ROPE_THETA = 10000.0
ADAM_LR = 0.001
ADAM_B1 = 0.9
ADAM_B2 = 0.999
ADAM_EPS = 1e-08
ADAM_WD = 0.01
ADAM_STEP = 10
NEG = -1e30
VMEM_LIMIT = 56 * 1024 * 1024

PREF = {"row": 256, "conv_rows": 512, "conv_cols": 512, "ffn_rows": 256, "attn": 1024}

NT_DIMS = (((1,), (1,)), ((), ()))
TN_DIMS = (((0,), (0,)), ((), ()))


def _cparams(*sem):
    return pltpu.CompilerParams(dimension_semantics=sem if sem else None, vmem_limit_bytes=VMEM_LIMIT)


def _tile(n, pref, unit=LANE):
    if n <= pref:
        return n
    t = (pref // unit) * unit
    while t >= unit:
        if n % t == 0:
            return t
        t -= unit
    return n


def _roundup(n, m):
    return (n + m - 1) // m * m


def _rsq(x):
    return lax.rsqrt(jnp.mean(x * x, axis=-1, keepdims=True) + RMS_EPS)


def _colsum(x):
    return jnp.sum(x, axis=0, keepdims=True)


def _sigmoid(x):
    return 1.0 / (1.0 + jnp.exp(-x))


def _matmul(a, b, *, ta=False, tb=False, out_dtype, tm, tn, tk, name, exchange=None):
    m_dim, k_dim = (a.shape[1], a.shape[0]) if ta else a.shape
    n_dim = b.shape[0] if tb else b.shape[1]
    tm, tn, tk = _tile(m_dim, tm), _tile(n_dim, tn), _tile(k_dim, tk)
    gi, gj, nk = m_dim // tm, n_dim // tn, k_dim // tk
    dims = (((0 if ta else 1,), (1 if tb else 0,)), ((), ()))
    chunks = list(exchange or [])
    nx = len(chunks)

    def body(*refs):
        a_ref, b_ref = refs[:2]
        xin, o_ref, xout = refs[2:2 + nx], refs[2 + nx], refs[3 + nx:3 + 2 * nx]
        scratch = refs[3 + 2 * nx:]
        sems = scratch[1:] if nk > 1 else scratch
        i, j, k = pl.program_id(0), pl.program_id(1), pl.program_id(2)
        if nx:
            @pl.when((i == 0) & (j == 0) & (k == 0))
            def _():
                _exchange_start(xin, xout, *sems)

        part = lax.dot_general(a_ref[...], b_ref[...], dims, preferred_element_type=F32)
        if nk == 1:
            o_ref[...] = part.astype(o_ref.dtype)
        else:
            acc_ref = scratch[0]

            @pl.when(k == 0)
            def _():
                acc_ref[...] = part

            @pl.when(k > 0)
            def _():
                acc_ref[...] += part

            @pl.when(k == nk - 1)
            def _():
                o_ref[...] = acc_ref[...].astype(o_ref.dtype)

        if nx:
            @pl.when((i == gi - 1) & (j == gj - 1) & (k == nk - 1))
            def _():
                _exchange_finish(xin, xout, *sems)

    a_spec = pl.BlockSpec((tk, tm), lambda i, j, k: (k, i)) if ta else pl.BlockSpec((tm, tk), lambda i, j, k: (i, k))
    b_spec = pl.BlockSpec((tn, tk), lambda i, j, k: (j, k)) if tb else pl.BlockSpec((tk, tn), lambda i, j, k: (k, j))
    hbm = pl.BlockSpec(memory_space=pl.ANY)
    out = pl.pallas_call(
        body,
        name=name,
        grid=(gi, gj, nk),
        in_specs=[a_spec, b_spec] + [hbm] * nx,
        out_specs=[pl.BlockSpec((tm, tn), lambda i, j, k: (i, j))] + [hbm] * nx,
        out_shape=[jax.ShapeDtypeStruct((m_dim, n_dim), out_dtype)] + [jax.ShapeDtypeStruct(c.shape, c.dtype) for c in chunks],
        scratch_shapes=([pltpu.VMEM((tm, tn), F32)] if nk > 1 else []) + (_comm_scratch(nx) if nx else []),
        compiler_params=_cparams(*(("arbitrary",) * 3 if nx else ("parallel", "parallel", "arbitrary"))),
    )(a, b, *chunks)
    return out if nx else out[0]


def _shift_down(x, halo, n):
    r = pltpu.roll(x, n, 0)
    hr = pltpu.roll(halo, n, 0)
    row = lax.broadcasted_iota(jnp.int32, halo.shape, 0)
    top = jnp.where(row < n, hr, r[:SUB])
    return jnp.concatenate([top, r[SUB:]], axis=0)


def _shift_up(x, halo, n):
    ts = x.shape[0]
    r = pltpu.roll(x, ts - n, 0)
    hr = pltpu.roll(halo, SUB - n, 0)
    row = lax.broadcasted_iota(jnp.int32, halo.shape, 0)
    bot = jnp.where(row >= SUB - n, hr, r[ts - SUB:])
    return jnp.concatenate([r[:ts - SUB], bot], axis=0)


def _conv3(x, halo, w_ref, b_ref):
    return _shift_down(x, halo, 2) * w_ref[0:1, :] + _shift_down(x, halo, 1) * w_ref[1:2, :] + x * w_ref[2:3, :] + b_ref[...]


def _prev_halo(ts, col):
    return lambda j, i: (jnp.maximum(i * (ts // SUB) - 1, 0), col(j))


def _next_halo(ts, n_rows, col):
    return lambda j, i: (jnp.minimum((i + 1) * (ts // SUB), n_rows // SUB - 1), col(j))


def _modnorm_fwd(x, g, sc, sh, ts):
    s_len, d = x.shape

    def body(x_ref, g_ref, sc_ref, sh_ref, h_ref):
        xv = x_ref[...]
        h_ref[...] = ((xv * _rsq(xv) * g_ref[...]) * (1.0 + sc_ref[...]) + sh_ref[...]).astype(BF16)

    vec = pl.BlockSpec((1, d), lambda i: (0, 0))
    return pl.pallas_call(
        body, name="modnorm_fwd", grid=(s_len // ts,),
        in_specs=[pl.BlockSpec((ts, d), lambda i: (i, 0)), vec, vec, vec],
        out_specs=pl.BlockSpec((ts, d), lambda i: (i, 0)),
        out_shape=jax.ShapeDtypeStruct((s_len, d), BF16),
        compiler_params=_cparams("parallel"),
    )(x, g, sc, sh)


def _rope_tables(pos_col, inv_freq_row, ts):
    s_len = pos_col.shape[0]
    half = QK_ROPE // 2

    def body(p_ref, f_ref, c_ref, a_ref, b_ref):
        ang = p_ref[...] * f_ref[...]
        lane = lax.broadcasted_iota(jnp.int32, ang.shape, 1)
        cos, sin = jnp.cos(ang), jnp.sin(ang)
        c_ref[...] = jnp.where(lane < 2 * half, cos, 0.0)
        a_ref[...] = jnp.where(lane < half, -sin, 0.0)
        b_ref[...] = jnp.where((lane >= half) & (lane < 2 * half), sin, 0.0)

    out = jax.ShapeDtypeStruct((s_len, LANE), F32)
    blk = pl.BlockSpec((ts, LANE), lambda i: (i, 0))
    return pl.pallas_call(
        body, name="rope_tables", grid=(s_len // ts,),
        in_specs=[pl.BlockSpec((ts, 1), lambda i: (i, 0)), pl.BlockSpec((1, LANE), lambda i: (0, 0))],
        out_specs=[blk, blk, blk], out_shape=[out, out, out],
        compiler_params=_cparams("parallel"),
    )(pos_col, inv_freq_row)


def _rope(seg, c, a, b):
    return seg * c + pltpu.roll(seg, LANE - QK_ROPE // 2, 1) * a + pltpu.roll(seg, QK_ROPE // 2, 1) * b


def _rope_t(seg, c, a, b):
    return seg * c - pltpu.roll(seg, LANE - QK_ROPE // 2, 1) * a - pltpu.roll(seg, QK_ROPE // 2, 1) * b


def _qkv_fwd(proj, lay, wuq, wk, wv, g_q, g_kv, ctab, atab, btab, n_heads, ts, scale):
    s_len = proj.shape[0]
    ql_w, kl_w = wuq.shape[0], wk.shape[0]

    def body(ql_ref, kl_ref, kr_ref, wuq_ref, wk_ref, wv_ref, gq_ref, gkv_ref, c_ref, a_ref, b_ref,
             q_out, k_out, v_out, qn_out, kvn_out):
        c, a, b = c_ref[...], a_ref[...], b_ref[...]
        ql = ql_ref[...].astype(F32)
        qn = (ql * _rsq(ql) * gq_ref[...]).astype(BF16)
        qn_out[...] = qn
        q = jnp.dot(qn, wuq_ref[...], preferred_element_type=F32)
        kl = kl_ref[...].astype(F32)
        kvn = (kl * _rsq(kl) * gkv_ref[...]).astype(BF16)
        kvn_out[...] = kvn
        kn = jnp.dot(kvn, wk_ref[...], preferred_element_type=F32)
        v_out[...] = jnp.dot(kvn, wv_ref[...], preferred_element_type=F32).astype(BF16)
        kr = _rope(kr_ref[...].astype(F32), c, a, b).astype(BF16)
        for h in range(n_heads):
            o = h * HEAD_W
            q_out[:, o:o + QK_NOPE] = (q[:, o:o + QK_NOPE] * scale).astype(BF16)
            q_out[:, o + QK_NOPE:o + HEAD_W] = (_rope(q[:, o + QK_NOPE:o + HEAD_W], c, a, b) * scale).astype(BF16)
            k_out[:, o:o + QK_NOPE] = kn[:, h * QK_NOPE:(h + 1) * QK_NOPE].astype(BF16)
            k_out[:, o + QK_NOPE:o + HEAD_W] = kr

    def full(arr):
        return pl.BlockSpec(arr.shape, lambda i: (0, 0))

    tab = pl.BlockSpec((ts, LANE), lambda i: (i, 0))
    hw, hv = n_heads * HEAD_W, n_heads * V_DIM
    return pl.pallas_call(
        body, name="qkv_fwd", grid=(s_len // ts,),
        in_specs=[pl.BlockSpec((ts, ql_w), lambda i: (i, lay["ql"] // ql_w)),
                  pl.BlockSpec((ts, kl_w), lambda i: (i, lay["kv"] // kl_w)),
                  pl.BlockSpec((ts, LANE), lambda i: (i, lay["kr"] // LANE)),
                  full(wuq), full(wk), full(wv), full(g_q), full(g_kv), tab, tab, tab],
        out_specs=[pl.BlockSpec((ts, hw), lambda i: (i, 0)), pl.BlockSpec((ts, hw), lambda i: (i, 0)),
                   pl.BlockSpec((ts, hv), lambda i: (i, 0)), pl.BlockSpec((ts, ql_w), lambda i: (i, 0)),
                   pl.BlockSpec((ts, kl_w), lambda i: (i, 0))],
        out_shape=[jax.ShapeDtypeStruct((s_len, hw), BF16), jax.ShapeDtypeStruct((s_len, hw), BF16),
                   jax.ShapeDtypeStruct((s_len, hv), BF16), jax.ShapeDtypeStruct((s_len, ql_w), BF16),
                   jax.ShapeDtypeStruct((s_len, kl_w), BF16)],
        compiler_params=_cparams("parallel"),
    )(proj, proj, proj, wuq, wk, wv, g_q, g_kv, ctab, atab, btab)


def _flash_fwd(q, k, v, n_heads, t, gather):
    ng = len(gather)
    s_len = q.shape[0]
    nb = s_len // t
    pairs = [(i, j) for i in range(nb) for j in range(i + 1)]
    itab = jnp.asarray(np.array([p[0] for p in pairs], np.int32))
    jtab = jnp.asarray(np.array([p[1] for p in pairs], np.int32))

    n_steps = len(pairs)

    def body(it_ref, jt_ref, q_ref, k_ref, v_ref, *rest):
        gin, (o_ref, lse_ref), gout = rest[:ng], rest[ng:ng + 2], rest[ng + 2:2 * ng + 2]
        m_sc, l_sc, acc_sc = rest[2 * ng + 2:2 * ng + 5]
        sems = rest[2 * ng + 5:]
        head, step_id = pl.program_id(0), pl.program_id(1)
        i, j = it_ref[step_id], jt_ref[step_id]

        @pl.when((head == 0) & (step_id == 0))
        def _():
            _gather_start(gin, gout, *sems)

        @pl.when((head == n_heads // 2) & (step_id == 0))
        def _():
            _gather_forward(gin, gout, *sems)

        @pl.when(j == 0)
        def _():
            m_sc[...] = jnp.full(m_sc.shape, NEG, F32)
            l_sc[...] = jnp.zeros(l_sc.shape, F32)
            acc_sc[...] = jnp.zeros(acc_sc.shape, F32)

        def step(diag):
            s = lax.dot_general(q_ref[...], k_ref[...], NT_DIMS, preferred_element_type=F32)
            if diag:
                row = lax.broadcasted_iota(jnp.int32, s.shape, 0)
                col = lax.broadcasted_iota(jnp.int32, s.shape, 1)
                s = jnp.where(col <= row, s, NEG)
            m_prev = m_sc[...]
            m_new = jnp.maximum(m_prev, jnp.max(s, axis=1, keepdims=True))
            alpha = jnp.exp(m_prev - m_new)
            p = jnp.exp(s - m_new)
            l_sc[...] = alpha * l_sc[...] + jnp.sum(p, axis=1, keepdims=True)
            acc_sc[...] = acc_sc[...] * alpha + jnp.dot(p.astype(BF16), v_ref[...], preferred_element_type=F32)
            m_sc[...] = m_new

        @pl.when(j < i)
        def _():
            step(False)

        @pl.when(j == i)
        def _():
            step(True)
            l = l_sc[...]
            o_ref[...] = (acc_sc[...] / l).astype(BF16)
            lse_ref[0] = jnp.broadcast_to(m_sc[...] + jnp.log(l), (t, LANE))

        @pl.when((head == n_heads - 1) & (step_id == n_steps - 1))
        def _():
            _gather_finish(gin, gout, *sems)

    hbm = pl.BlockSpec(memory_space=pl.ANY)
    grid_spec = pltpu.PrefetchScalarGridSpec(
        num_scalar_prefetch=2, grid=(n_heads, n_steps),
        in_specs=[pl.BlockSpec((t, HEAD_W), lambda h, s, it, jt: (it[s], h)),
                  pl.BlockSpec((t, HEAD_W), lambda h, s, it, jt: (jt[s], h)),
                  pl.BlockSpec((t, V_DIM), lambda h, s, it, jt: (jt[s], h))] + [hbm] * ng,
        out_specs=[pl.BlockSpec((t, V_DIM), lambda h, s, it, jt: (it[s], h)),
                   pl.BlockSpec((1, t, LANE), lambda h, s, it, jt: (h, it[s], 0))] + [hbm] * ng,
        scratch_shapes=[pltpu.VMEM((t, 1), F32), pltpu.VMEM((t, 1), F32), pltpu.VMEM((t, V_DIM), F32)]
        + _comm_scratch(ng),
    )
    return pl.pallas_call(
        body, name="flash_fwd", grid_spec=grid_spec,
        out_shape=[jax.ShapeDtypeStruct((s_len, n_heads * V_DIM), BF16),
                   jax.ShapeDtypeStruct((n_heads, s_len, LANE), F32)] + _gathered_shapes(gather),
        compiler_params=_cparams("arbitrary", "arbitrary"),
    )(itab, jtab, q, k, v, *gather)


def _gconv_fwd(proj, lay, w, b, ts, tc):
    s_len = proj.shape[0]
    cw = w.shape[1]
    nj = cw // tc

    def body(gb_ref, gc_ref, ci_ref, gch_ref, cih_ref, w_ref, b_ref, o_ref):
        i = pl.program_id(1)
        p = gc_ref[...].astype(F32) * ci_ref[...].astype(F32)
        ph = jnp.where(i > 0, gch_ref[...].astype(F32) * cih_ref[...].astype(F32), 0.0)
        o_ref[...] = (gb_ref[...].astype(F32) * _conv3(p, ph, w_ref, b_ref)).astype(BF16)

    def blk(off):
        return pl.BlockSpec((ts, tc), lambda j, i: (i, off // tc + j))

    def halo(off):
        return pl.BlockSpec((SUB, tc), _prev_halo(ts, lambda j: off // tc + j))

    return pl.pallas_call(
        body, name="gconv_fwd", grid=(nj, s_len // ts),
        in_specs=[blk(lay["gb"]), blk(lay["gc"]), blk(lay["ci"]), halo(lay["gc"]), halo(lay["ci"]),
                  pl.BlockSpec((3, tc), lambda j, i: (0, j)), pl.BlockSpec((1, tc), lambda j, i: (0, j))],
        out_specs=pl.BlockSpec((ts, tc), lambda j, i: (i, j)),
        out_shape=jax.ShapeDtypeStruct((s_len, cw), BF16),
        compiler_params=_cparams("parallel", "parallel"),
    )(proj, proj, proj, proj, proj, w, b)


def _post_mix_fwd(mix, x, gt, g_post, g_pre, sc, sh, ts):
    s_len, d = x.shape

    def body(mix_ref, x_ref, gt_ref, gp_ref, g2_ref, sc_ref, sh_ref, x1_ref, h2_ref):
        mv = mix_ref[...]
        x1 = x_ref[...] + gt_ref[...] * (mv * _rsq(mv) * gp_ref[...])
        x1_ref[...] = x1
        h2_ref[...] = ((x1 * _rsq(x1) * g2_ref[...]) * (1.0 + sc_ref[...]) + sh_ref[...]).astype(BF16)

    vec = pl.BlockSpec((1, d), lambda i: (0, 0))
    row = pl.BlockSpec((ts, d), lambda i: (i, 0))
    return pl.pallas_call(
        body, name="post_mix_fwd", grid=(s_len // ts,),
        in_specs=[row, row, vec, vec, vec, vec, vec], out_specs=[row, row],
        out_shape=[jax.ShapeDtypeStruct((s_len, d), F32), jax.ShapeDtypeStruct((s_len, d), BF16)],
        compiler_params=_cparams("parallel"),
    )(mix, x, gt, g_post, g_pre, sc, sh)


def _act_fwd(up, w, b, ts, tc):
    s_len, f2 = up.shape
    nh = f2 // 2 // tc

    def body(up_ref, uph_ref, w_ref, b_ref, u_ref, o_ref):
        i = pl.program_id(1)
        halo = jnp.where(i > 0, uph_ref[...].astype(F32), 0.0)
        u = _conv3(up_ref[...].astype(F32), halo, w_ref, b_ref)
        u_ref[...] = u.astype(BF16)
        ua, ug = u[:, :tc], u[:, tc:]
        o_ref[...] = (ug * _sigmoid(ug) * ua).astype(BF16)

    pair = pl.BlockSpec((ts, 2 * tc), lambda j, i: (i, j))
    return pl.pallas_call(
        body, name="act_fwd", grid=(nh, s_len // ts),
        in_specs=[pair, pl.BlockSpec((SUB, 2 * tc), _prev_halo(ts, lambda j: j)),
                  pl.BlockSpec((3, 2 * tc), lambda j, i: (0, j)), pl.BlockSpec((1, 2 * tc), lambda j, i: (0, j))],
        out_specs=[pair, pl.BlockSpec((ts, tc), lambda j, i: (i, j))],
        out_shape=[jax.ShapeDtypeStruct((s_len, f2), BF16), jax.ShapeDtypeStruct((s_len, f2 // 2), BF16)],
        compiler_params=_cparams("parallel", "parallel"),
    )(up, up, w, b)


def _final_bwd(y, x1, tgt, gt, g_post, ts):
    s_len, d = y.shape

    def body(y_ref, x1_ref, t_ref, gt_ref, g_ref, dy_ref, dx2_ref, loss_ref, dgt_ref, dg_ref):
        i = pl.program_id(0)
        yv = y_ref[...]
        r = _rsq(yv)
        yh = yv * r
        n = yh * g_ref[...]
        e = x1_ref[...] + gt_ref[...] * n - t_ref[...]
        loss = 0.5 * jnp.sum(jnp.mean(e * e, axis=-1, keepdims=True), axis=0, keepdims=True)
        dx2 = e * (1.0 / d)
        dx2_ref[...] = dx2
        dn = dx2 * gt_ref[...]
        dyh = dn * g_ref[...]
        dy_ref[...] = (r * (dyh - yh * jnp.mean(dyh * yh, axis=-1, keepdims=True))).astype(BF16)

        @pl.when(i == 0)
        def _():
            loss_ref[...] = jnp.zeros(loss_ref.shape, F32)
            dgt_ref[...] = jnp.zeros(dgt_ref.shape, F32)
            dg_ref[...] = jnp.zeros(dg_ref.shape, F32)

        loss_ref[...] += jnp.broadcast_to(loss, loss_ref.shape)
        dgt_ref[...] += _colsum(dx2 * n)
        dg_ref[...] += _colsum(dn * yh)

    vec = pl.BlockSpec((1, d), lambda i: (0, 0))
    row = pl.BlockSpec((ts, d), lambda i: (i, 0))
    vshape = jax.ShapeDtypeStruct((1, d), F32)
    return pl.pallas_call(
        body, name="final_bwd", grid=(s_len // ts,),
        in_specs=[row, row, row, vec, vec],
        out_specs=[row, row, pl.BlockSpec((1, LANE), lambda i: (0, 0)), vec, vec],
        out_shape=[jax.ShapeDtypeStruct((s_len, d), BF16), jax.ShapeDtypeStruct((s_len, d), F32),
                   jax.ShapeDtypeStruct((1, LANE), F32), vshape, vshape],
        compiler_params=_cparams("arbitrary"),
    )(y, x1, tgt, gt, g_post)


def _ffn_act_bwd(d_act, u, up, w, ts, tc):
    s_len, f2 = up.shape
    nh = f2 // 2 // tc
    n_i = s_len // ts

    def gate(dv, uv):
        ua, ug = uv[:, :tc], uv[:, tc:]
        sg = _sigmoid(ug)
        return jnp.concatenate([dv * (ug * sg), dv * ua * (sg * (1.0 + ug * (1.0 - sg)))], axis=1)

    def body(d_ref, dh_ref, u_ref, uh_ref, x_ref, w_ref, dx_ref, dw_ref, db_ref):
        i = pl.program_id(1)
        du = gate(d_ref[...].astype(F32), u_ref[...].astype(F32))
        duh = jnp.where(i < n_i - 1, gate(dh_ref[...].astype(F32), uh_ref[...].astype(F32)), 0.0)
        du1, du2 = _shift_up(du, duh, 1), _shift_up(du, duh, 2)
        dx_ref[...] = (du * w_ref[2:3, :] + du1 * w_ref[1:2, :] + du2 * w_ref[0:1, :]).astype(BF16)

        @pl.when(i == 0)
        def _():
            dw_ref[...] = jnp.zeros(dw_ref.shape, F32)
            db_ref[...] = jnp.zeros(db_ref.shape, F32)

        xv = x_ref[...].astype(F32)
        dw_ref[0:1, :] += _colsum(du2 * xv)
        dw_ref[1:2, :] += _colsum(du1 * xv)
        dw_ref[2:3, :] += _colsum(du * xv)
        db_ref[...] += _colsum(du)

    pair = pl.BlockSpec((ts, 2 * tc), lambda j, i: (i, j))
    pair_halo = pl.BlockSpec((SUB, 2 * tc), _next_halo(ts, s_len, lambda j: j))
    return pl.pallas_call(
        body, name="ffn_act_bwd", grid=(nh, n_i),
        in_specs=[pl.BlockSpec((ts, tc), lambda j, i: (i, j)), pl.BlockSpec((SUB, tc), _next_halo(ts, s_len, lambda j: j)),
                  pair, pair_halo, pair, pl.BlockSpec((3, 2 * tc), lambda j, i: (0, j))],
        out_specs=[pair, pl.BlockSpec((3, 2 * tc), lambda j, i: (0, j)), pl.BlockSpec((1, 2 * tc), lambda j, i: (0, j))],
        out_shape=[jax.ShapeDtypeStruct((s_len, f2), BF16), jax.ShapeDtypeStruct((3, f2), F32),
                   jax.ShapeDtypeStruct((1, f2), F32)],
        compiler_params=_cparams("parallel", "arbitrary"),
    )(d_act, d_act, u, u, up, w)


def _mid_bwd(dh2, x1, dx2, mix, g_pre, sc, gt_m, g_post, ts):
    s_len, d = x1.shape

    def body(dh_ref, x1_ref, dx2_ref, mix_ref, g_ref, sc_ref, gt_ref, gp_ref,
             dx1_ref, dmix_ref, dsh_ref, dsc_ref, dg_ref, dgt_ref, dgp_ref):
        i = pl.program_id(0)
        dh = dh_ref[...]
        x1 = x1_ref[...]
        r1 = _rsq(x1)
        xh = x1 * r1
        dxh = dh * (1.0 + sc_ref[...]) * g_ref[...]
        dx1 = dx2_ref[...] + r1 * (dxh - xh * jnp.mean(dxh * xh, axis=-1, keepdims=True))
        dx1_ref[...] = dx1
        mv = mix_ref[...]
        rm = _rsq(mv)
        mh = mv * rm
        dn = dx1 * gt_ref[...]
        dmh = dn * gp_ref[...]
        dmix_ref[...] = (rm * (dmh - mh * jnp.mean(dmh * mh, axis=-1, keepdims=True))).astype(BF16)

        @pl.when(i == 0)
        def _():
            for ref in (dsh_ref, dsc_ref, dg_ref, dgt_ref, dgp_ref):
                ref[...] = jnp.zeros(ref.shape, F32)

        dsh_ref[...] += _colsum(dh)
        dsc_ref[...] += _colsum(dh * (xh * g_ref[...]))
        dg_ref[...] += _colsum(dh * (1.0 + sc_ref[...]) * xh)
        dgt_ref[...] += _colsum(dx1 * (mh * gp_ref[...]))
        dgp_ref[...] += _colsum(dn * mh)

    vec = pl.BlockSpec((1, d), lambda i: (0, 0))
    row = pl.BlockSpec((ts, d), lambda i: (i, 0))
    vshape = jax.ShapeDtypeStruct((1, d), F32)
    return pl.pallas_call(
        body, name="mid_bwd", grid=(s_len // ts,),
        in_specs=[row, row, row, row, vec, vec, vec, vec],
        out_specs=[row, row, vec, vec, vec, vec, vec],
        out_shape=[jax.ShapeDtypeStruct((s_len, d), F32), jax.ShapeDtypeStruct((s_len, d), BF16)] + [vshape] * 5,
        compiler_params=_cparams("arbitrary"),
    )(dh2, x1, dx2, mix, g_pre, sc, gt_m, g_post)


def _first_bwd(dh1, x, dx1, g_pre, sc, ts):
    s_len, d = x.shape

    def body(dh_ref, x_ref, dx1_ref, g_ref, sc_ref, dx_ref, dsh_ref, dsc_ref, dg_ref):
        i = pl.program_id(0)
        dh = dh_ref[...]
        xv = x_ref[...]
        r = _rsq(xv)
        xh = xv * r
        dxh = dh * (1.0 + sc_ref[...]) * g_ref[...]
        dx_ref[...] = dx1_ref[...] + r * (dxh - xh * jnp.mean(dxh * xh, axis=-1, keepdims=True))

        @pl.when(i == 0)
        def _():
            for ref in (dsh_ref, dsc_ref, dg_ref):
                ref[...] = jnp.zeros(ref.shape, F32)

        dsh_ref[...] += _colsum(dh)
        dsc_ref[...] += _colsum(dh * (xh * g_ref[...]))
        dg_ref[...] += _colsum(dh * (1.0 + sc_ref[...]) * xh)

    vec = pl.BlockSpec((1, d), lambda i: (0, 0))
    row = pl.BlockSpec((ts, d), lambda i: (i, 0))
    vshape = jax.ShapeDtypeStruct((1, d), F32)
    return pl.pallas_call(
        body, name="first_bwd", grid=(s_len // ts,),
        in_specs=[row, row, row, vec, vec], out_specs=[row, vec, vec, vec],
        out_shape=[jax.ShapeDtypeStruct((s_len, d), F32), vshape, vshape, vshape],
        compiler_params=_cparams("arbitrary"),
    )(dh1, x, dx1, g_pre, sc)


def _gconv_bwd(d_mixcat, proj, lay, w, b, ts, tc):
    s_len = proj.shape[0]
    cw = w.shape[1]
    n_i = s_len // ts
    dc_off = d_mixcat.shape[1] - cw

    def body(dc_ref, dch_ref, gb_ref, gbh_ref, gc_ref, gch_ref, ci_ref, cih_ref, w_ref, b_ref,
             dgb_ref, dgc_ref, dci_ref, dw_ref, db_ref):
        i = pl.program_id(1)
        gc, ci = gc_ref[...].astype(F32), ci_ref[...].astype(F32)
        p = gc * ci
        ph = jnp.where(i > 0, gch_ref[...].astype(F32) * cih_ref[...].astype(F32), 0.0)
        pm1, pm2 = _shift_down(p, ph, 1), _shift_down(p, ph, 2)
        z = pm2 * w_ref[0:1, :] + pm1 * w_ref[1:2, :] + p * w_ref[2:3, :] + b_ref[...]
        dc = dc_ref[...].astype(F32)
        dgb_ref[...] = (dc * z).astype(BF16)
        dz = dc * gb_ref[...].astype(F32)
        dzh = jnp.where(i < n_i - 1, dch_ref[...].astype(F32) * gbh_ref[...].astype(F32), 0.0)
        dz1, dz2 = _shift_up(dz, dzh, 1), _shift_up(dz, dzh, 2)
        dp = dz * w_ref[2:3, :] + dz1 * w_ref[1:2, :] + dz2 * w_ref[0:1, :]
        dgc_ref[...] = (dp * ci).astype(BF16)
        dci_ref[...] = (dp * gc).astype(BF16)

        @pl.when(i == 0)
        def _():
            dw_ref[...] = jnp.zeros(dw_ref.shape, F32)
            db_ref[...] = jnp.zeros(db_ref.shape, F32)

        dw_ref[0:1, :] += _colsum(dz2 * p)
        dw_ref[1:2, :] += _colsum(dz1 * p)
        dw_ref[2:3, :] += _colsum(dz * p)
        db_ref[...] += _colsum(dz)

    def blk(off):
        return pl.BlockSpec((ts, tc), lambda j, i: (i, off // tc + j))

    def prev(off):
        return pl.BlockSpec((SUB, tc), _prev_halo(ts, lambda j: off // tc + j))

    def nxt(off):
        return pl.BlockSpec((SUB, tc), _next_halo(ts, s_len, lambda j: off // tc + j))

    out_blk = pl.BlockSpec((ts, tc), lambda j, i: (i, j))
    act = jax.ShapeDtypeStruct((s_len, cw), BF16)
    return pl.pallas_call(
        body, name="gconv_bwd", grid=(cw // tc, n_i),
        in_specs=[blk(dc_off), nxt(dc_off), blk(lay["gb"]), nxt(lay["gb"]), blk(lay["gc"]), prev(lay["gc"]),
                  blk(lay["ci"]), prev(lay["ci"]),
                  pl.BlockSpec((3, tc), lambda j, i: (0, j)), pl.BlockSpec((1, tc), lambda j, i: (0, j))],
        out_specs=[out_blk, out_blk, out_blk,
                   pl.BlockSpec((3, tc), lambda j, i: (0, j)), pl.BlockSpec((1, tc), lambda j, i: (0, j))],
        out_shape=[act, act, act, jax.ShapeDtypeStruct((3, cw), F32), jax.ShapeDtypeStruct((1, cw), F32)],
        compiler_params=_cparams("parallel", "arbitrary"),
    )(d_mixcat, d_mixcat, proj, proj, proj, proj, proj, proj, w, b)


def _delta(o, d_mixcat, n_heads, ts):
    s_len = o.shape[0]

    def body(o_ref, do_ref, out_ref):
        for h in range(n_heads):
            sl = slice(h * V_DIM, (h + 1) * V_DIM)
            prod = o_ref[:, sl].astype(F32) * do_ref[:, sl].astype(F32)
            out_ref[h] = jnp.broadcast_to(jnp.sum(prod, axis=1, keepdims=True), (ts, LANE))

    hv = n_heads * V_DIM
    return pl.pallas_call(
        body, name="attn_delta", grid=(s_len // ts,),
        in_specs=[pl.BlockSpec((ts, hv), lambda i: (i, 0)), pl.BlockSpec((ts, hv), lambda i: (i, 0))],
        out_specs=pl.BlockSpec((n_heads, ts, LANE), lambda i: (0, i, 0)),
        out_shape=jax.ShapeDtypeStruct((n_heads, s_len, LANE), F32),
        compiler_params=_cparams("parallel"),
    )(o, d_mixcat)


def _flash_bwd(q, k, v, d_mixcat, lse_row, delta_row, n_heads, t, scale, exchange):
    nx = len(exchange)
    s_len = q.shape[0]
    nb = s_len // t
    pairs = [(j, i) for j in range(nb) for i in range(j, nb)]
    jtab = jnp.asarray(np.array([p[0] for p in pairs], np.int32))
    itab = jnp.asarray(np.array([p[1] for p in pairs], np.int32))
    n_steps = len(pairs)

    def body(jt_ref, it_ref, q_ref, k_ref, v_ref, do_ref, lse_ref, dl_ref, *rest):
        xin, (dq_ref, dk_ref, dv_ref), xout = rest[:nx], rest[nx:nx + 3], rest[nx + 3:2 * nx + 3]
        dq_acc, dk_acc, dv_acc = rest[2 * nx + 3:2 * nx + 6]
        sems = rest[2 * nx + 6:]
        head, step_id = pl.program_id(0), pl.program_id(1)
        j, i = jt_ref[step_id], it_ref[step_id]

        @pl.when((head == 0) & (step_id == 0))
        def _():
            _exchange_start(xin, xout, *sems)

        @pl.when(step_id == 0)
        def _():
            dq_acc[...] = jnp.zeros(dq_acc.shape, F32)

        @pl.when(i == j)
        def _():
            dk_acc[...] = jnp.zeros(dk_acc.shape, F32)
            dv_acc[...] = jnp.zeros(dv_acc.shape, F32)

        def step(diag):
            qv, kv, vv, dov = q_ref[...], k_ref[...], v_ref[...], do_ref[...]
            s_t = lax.dot_general(kv, qv, NT_DIMS, preferred_element_type=F32)
            if diag:
                krow = lax.broadcasted_iota(jnp.int32, s_t.shape, 0)
                qcol = lax.broadcasted_iota(jnp.int32, s_t.shape, 1)
                s_t = jnp.where(krow <= qcol, s_t, NEG)
            p_t = jnp.exp(s_t - lse_ref[0])
            dv_acc[...] += jnp.dot(p_t.astype(BF16), dov, preferred_element_type=F32)
            dp_t = lax.dot_general(vv, dov, NT_DIMS, preferred_element_type=F32)
            ds_t = (p_t * (dp_t - dl_ref[0])).astype(BF16)
            dk_acc[...] += jnp.dot(ds_t, qv, preferred_element_type=F32)
            rows = pl.ds(pl.multiple_of(i * t, t), t)
            dq_acc[rows, :] += lax.dot_general(ds_t, kv, TN_DIMS, preferred_element_type=F32)

        @pl.when(i > j)
        def _():
            step(False)

        @pl.when(i == j)
        def _():
            step(True)

        @pl.when(i == nb - 1)
        def _():
            dk_ref[...] = dk_acc[...].astype(BF16)
            dv_ref[...] = dv_acc[...].astype(BF16)

        @pl.when(step_id == n_steps - 1)
        def _():
            dq_ref[...] = (dq_acc[...] * scale).astype(BF16)

        @pl.when((head == n_heads - 1) & (step_id == n_steps - 1))
        def _():
            _exchange_finish(xin, xout, *sems)

    hbm = pl.BlockSpec(memory_space=pl.ANY)
    hv = n_heads * V_DIM
    do_off = 0
    grid_spec = pltpu.PrefetchScalarGridSpec(
        num_scalar_prefetch=2, grid=(n_heads, n_steps),
        in_specs=[pl.BlockSpec((t, HEAD_W), lambda h, s, jt, it: (it[s], h)),
                  pl.BlockSpec((t, HEAD_W), lambda h, s, jt, it: (jt[s], h)),
                  pl.BlockSpec((t, V_DIM), lambda h, s, jt, it: (jt[s], h)),
                  pl.BlockSpec((t, V_DIM), lambda h, s, jt, it: (it[s], do_off + h)),
                  pl.BlockSpec((1, 1, t), lambda h, s, jt, it: (h, 0, it[s])),
                  pl.BlockSpec((1, 1, t), lambda h, s, jt, it: (h, 0, it[s]))] + [hbm] * nx,
        out_specs=[pl.BlockSpec((s_len, HEAD_W), lambda h, s, jt, it: (0, h)),
                   pl.BlockSpec((t, HEAD_W), lambda h, s, jt, it: (jt[s], h)),
                   pl.BlockSpec((t, V_DIM), lambda h, s, jt, it: (jt[s], h))] + [hbm] * nx,
        scratch_shapes=[pltpu.VMEM((s_len, HEAD_W), F32), pltpu.VMEM((t, HEAD_W), F32), pltpu.VMEM((t, V_DIM), F32)]
        + _comm_scratch(nx),
    )
    return pl.pallas_call(
        body, name="flash_bwd", grid_spec=grid_spec,
        out_shape=[jax.ShapeDtypeStruct((s_len, n_heads * HEAD_W), BF16),
                   jax.ShapeDtypeStruct((s_len, n_heads * HEAD_W), BF16),
                   jax.ShapeDtypeStruct((s_len, hv), BF16)] + [jax.ShapeDtypeStruct(c.shape, c.dtype) for c in exchange],
        compiler_params=_cparams("arbitrary", "arbitrary"),
    )(jtab, itab, q, k, v, d_mixcat, lse_row, delta_row, *exchange)


def _qkv_bwd(dq, dk, dv, proj, lay, wuq, wk, wv, g_q, g_kv, ctab, atab, btab, n_heads, ts):
    s_len = proj.shape[0]
    ql_w, kl_w = wuq.shape[0], wk.shape[0]
    tail_w = lay["np"] - lay["ql"]
    kv_o, kr_o = lay["kv"] - lay["ql"], lay["kr"] - lay["ql"]

    def body(dq_ref, dk_ref, dv_ref, ql_ref, kl_ref, wuq_ref, wk_ref, wv_ref, gq_ref, gkv_ref, c_ref, a_ref, b_ref,
             dqr_ref, dkn_ref, tail_ref, dgq_ref, dgkv_ref):
        i = pl.program_id(0)
        c, a, b = c_ref[...], a_ref[...], b_ref[...]
        dkr = jnp.zeros((ts, LANE), F32)
        for h in range(n_heads):
            o = h * HEAD_W
            dqr_ref[:, o:o + QK_NOPE] = dq_ref[:, o:o + QK_NOPE]
            dqr_ref[:, o + QK_NOPE:o + HEAD_W] = _rope_t(dq_ref[:, o + QK_NOPE:o + HEAD_W].astype(F32), c, a, b).astype(BF16)
            dkn_ref[:, h * QK_NOPE:(h + 1) * QK_NOPE] = dk_ref[:, o:o + QK_NOPE]
            dkr = dkr + dk_ref[:, o + QK_NOPE:o + HEAD_W].astype(F32)
        tail_ref[...] = jnp.zeros(tail_ref.shape, BF16)
        tail_ref[:, kr_o:kr_o + LANE] = _rope_t(dkr, c, a, b).astype(BF16)

        def rms_bwd(lat_ref, dn, g_ref):
            lat = lat_ref[...].astype(F32)
            r = _rsq(lat)
            xh = lat * r
            dxh = dn * g_ref[...]
            return r * (dxh - xh * jnp.mean(dxh * xh, axis=-1, keepdims=True)), _colsum(dn * xh)

        dqn = lax.dot_general(dqr_ref[...], wuq_ref[...], NT_DIMS, preferred_element_type=F32)
        d_ql, dgq = rms_bwd(ql_ref, dqn, gq_ref)
        tail_ref[:, 0:ql_w] = d_ql.astype(BF16)
        dkvn = (lax.dot_general(dkn_ref[...], wk_ref[...], NT_DIMS, preferred_element_type=F32)
                + lax.dot_general(dv_ref[...], wv_ref[...], NT_DIMS, preferred_element_type=F32))
        d_kl, dgkv = rms_bwd(kl_ref, dkvn, gkv_ref)
        tail_ref[:, kv_o:kv_o + kl_w] = d_kl.astype(BF16)

        @pl.when(i == 0)
        def _():
            dgq_ref[...] = jnp.zeros(dgq_ref.shape, F32)
            dgkv_ref[...] = jnp.zeros(dgkv_ref.shape, F32)

        dgq_ref[...] += dgq
        dgkv_ref[...] += dgkv

    def full(arr):
        return pl.BlockSpec(arr.shape, lambda i: (0, 0))

    def rows(w):
        return pl.BlockSpec((ts, w), lambda i: (i, 0))

    tab = pl.BlockSpec((ts, LANE), lambda i: (i, 0))
    hw, hv, hn = n_heads * HEAD_W, n_heads * V_DIM, n_heads * QK_NOPE
    return pl.pallas_call(
        body, name="qkv_bwd", grid=(s_len // ts,),
        in_specs=[rows(hw), rows(hw), rows(hv),
                  pl.BlockSpec((ts, ql_w), lambda i: (i, lay["ql"] // ql_w)),
                  pl.BlockSpec((ts, kl_w), lambda i: (i, lay["kv"] // kl_w)),
                  full(wuq), full(wk), full(wv), full(g_q), full(g_kv), tab, tab, tab],
        out_specs=[rows(hw), rows(hn), rows(tail_w), full(g_q), full(g_kv)],
        out_shape=[jax.ShapeDtypeStruct((s_len, hw), BF16), jax.ShapeDtypeStruct((s_len, hn), BF16),
                   jax.ShapeDtypeStruct((s_len, tail_w), BF16),
                   jax.ShapeDtypeStruct(g_q.shape, F32), jax.ShapeDtypeStruct(g_kv.shape, F32)],
        compiler_params=_cparams("arbitrary"),
    )(dq, dk, dv, proj, proj, wuq, wk, wv, g_q, g_kv, ctab, atab, btab)


def _adamw(w, g, m, v):
    m = ADAM_B1 * m + (1.0 - ADAM_B1) * g
    v = ADAM_B2 * v + (1.0 - ADAM_B2) * (g * g)
    m_hat = m / (1.0 - ADAM_B1 ** ADAM_STEP)
    v_hat = v / (1.0 - ADAM_B2 ** ADAM_STEP)
    delta = -ADAM_LR * (m_hat / (jnp.sqrt(v_hat) + ADAM_EPS) + ADAM_WD * w)
    return delta, m, v


def _adam_parts(parts, w, m, v, tr, name):
    r, c = w.shape
    tr = _tile(r, tr, SUB)

    def body(p_ref, w_ref, m_ref, v_ref, g_out, d_out, m_out, v_out):
        g = p_ref[0].astype(F32)
        for dev in range(1, N_DEV):
            g = g + p_ref[dev].astype(F32)
        g_out[...] = g
        d_out[...], m_out[...], v_out[...] = _adamw(w_ref[...], g, m_ref[...], v_ref[...])

    blk = pl.BlockSpec((tr, c), lambda i: (i, 0))
    shp = jax.ShapeDtypeStruct((r, c), F32)
    return pl.pallas_call(
        body, name=name, grid=(r // tr,),
        in_specs=[pl.BlockSpec((N_DEV, tr, c), lambda i: (0, i, 0)), blk, blk, blk],
        out_specs=[blk, blk, blk, blk], out_shape=[shp, shp, shp, shp],
        compiler_params=_cparams("parallel"),
    )(parts, w, m, v)


def _adam_ada(cact_t, dmod_sh, w, m, v, tr):
    r, c = w.shape

    def body(ct_ref, dm_ref, w_ref, m_ref, v_ref, g_out, d_out, m_out, v_out):
        g = jnp.dot(ct_ref[...], dm_ref[...], preferred_element_type=F32, precision=lax.Precision.HIGHEST)
        g_out[...] = g
        d_out[...], m_out[...], v_out[...] = _adamw(w_ref[...], g, m_ref[...], v_ref[...])

    blk = pl.BlockSpec((tr, c), lambda i: (i, 0))
    shp = jax.ShapeDtypeStruct((r, c), F32)
    return pl.pallas_call(
        body, name="adam_ada", grid=(r // tr,),
        in_specs=[pl.BlockSpec((tr, N_DEV), lambda i: (i, 0)), pl.BlockSpec((N_DEV, c), lambda i: (0, 0)), blk, blk, blk],
        out_specs=[blk, blk, blk, blk], out_shape=[shp, shp, shp, shp],
        compiler_params=_cparams("parallel"),
    )(cact_t, dmod_sh, w, m, v)


def _adam_small(v_all, w, m, v):
    n = w.shape[1]

    def body(p_ref, w_ref, m_ref, v_ref, g_out, d_out, m_out, v_out):
        g = p_ref[0:1, :]
        for dev in range(1, N_DEV):
            g = g + p_ref[dev:dev + 1, :]
        g_out[...] = g
        d_out[...], m_out[...], v_out[...] = _adamw(w_ref[...], g, m_ref[...], v_ref[...])

    shp = jax.ShapeDtypeStruct((1, n), F32)
    vm = pl.BlockSpec(memory_space=pltpu.VMEM)
    return pl.pallas_call(
        body, name="adam_small", in_specs=[vm, vm, vm, vm], out_specs=[vm, vm, vm, vm],
        out_shape=[shp, shp, shp, shp], compiler_params=_cparams(),
    )(v_all, w, m, v)


def _my_place():
    return lax.axis_index("x"), lax.axis_index("y"), lax.axis_index("c")


def _peer(place, k):
    x, y, c = place
    return (x ^ (k >> 2), y ^ ((k >> 1) & 1), c ^ (k & 1))


def _index(place):
    return 4 * place[0] + 2 * place[1] + place[2]


def _ada_fwd(vec, w_ada, b_ada_rows):
    lv = vec.shape[1]
    d, c = w_ada.shape

    def body(vec_ref, w_ref, b_ref, gath_ref, cact_ref, mod_ref, modsh, send_a, recv_a, send_b, recv_b, local_s):
        me = _my_place()
        my_i = _index(me)

        def gather_copy(k, to, src_row):
            row = gath_ref.at[pl.ds(src_row, 1), :]
            return pltpu.make_async_remote_copy(src_ref=row, dst_ref=row, send_sem=send_a.at[k], recv_sem=recv_a.at[k],
                                                device_id=to, device_id_type=MESH)

        own = pltpu.make_async_copy(vec_ref, gath_ref.at[pl.ds(my_i, 1), :], local_s.at[0])
        own.start()
        own.wait()
        sends = [gather_copy(k, _peer(me, k), my_i) for k in range(1, N_DEV)]
        for cp in sends:
            cp.start()
        for k in range(1, N_DEV):
            gather_copy(k, me, _index(_peer(me, k))).wait_recv()
        for cp in sends:
            cp.wait_send()

        c_all = gath_ref[:, 0:d]
        cact = c_all * _sigmoid(c_all)
        cact_ref[...] = cact
        modsh[...] = jnp.dot(cact, w_ref[...], preferred_element_type=F32, precision=lax.Precision.HIGHEST)

        def mod_copy(k, to, src_row, dst_row):
            return pltpu.make_async_remote_copy(src_ref=modsh.at[pl.ds(src_row, 1), :], dst_ref=mod_ref.at[pl.ds(dst_row, 1), :],
                                                send_sem=send_b.at[k], recv_sem=recv_b.at[k],
                                                device_id=to, device_id_type=MESH)

        own = pltpu.make_async_copy(modsh.at[pl.ds(my_i, 1), :], mod_ref.at[pl.ds(my_i, 1), :], local_s.at[1])
        own.start()
        sends = [mod_copy(k, _peer(me, k), _index(_peer(me, k)), my_i) for k in range(1, N_DEV)]
        for cp in sends:
            cp.start()
        for k in range(1, N_DEV):
            mod_copy(k, me, my_i, _index(_peer(me, k))).wait_recv()
        for cp in sends:
            cp.wait_send()
        own.wait()
        mod_ref[...] = mod_ref[...] + b_ref[...]

    vm = pl.BlockSpec(memory_space=pltpu.VMEM)
    return pl.pallas_call(
        body, name="ada_fwd", in_specs=[vm, vm, vm], out_specs=[vm, vm, vm],
        out_shape=[jax.ShapeDtypeStruct((N_DEV, lv), F32), jax.ShapeDtypeStruct((N_DEV, d), F32),
                   jax.ShapeDtypeStruct((N_DEV, c), F32)],
        scratch_shapes=[pltpu.VMEM((N_DEV, c), F32)] + [pltpu.SemaphoreType.DMA((N_DEV,))] * 4
        + [pltpu.SemaphoreType.DMA((2,))],
        compiler_params=pltpu.CompilerParams(vmem_limit_bytes=VMEM_LIMIT),
    )(vec, w_ada, b_ada_rows)


def _gather_small(vec):
    lv = vec.shape[1]

    def body(vec_ref, gath_ref, send_s, recv_s, local_s):
        me = _my_place()
        my_i = _index(me)

        def copy(k, to, src_row):
            row = gath_ref.at[pl.ds(src_row, 1), :]
            return pltpu.make_async_remote_copy(src_ref=row, dst_ref=row, send_sem=send_s.at[k], recv_sem=recv_s.at[k],
                                                device_id=to, device_id_type=MESH)

        own = pltpu.make_async_copy(vec_ref, gath_ref.at[pl.ds(my_i, 1), :], local_s)
        own.start()
        own.wait()
        sends = [copy(k, _peer(me, k), my_i) for k in range(1, N_DEV)]
        for cp in sends:
            cp.start()
        for k in range(1, N_DEV):
            copy(k, me, _index(_peer(me, k))).wait_recv()
        for cp in sends:
            cp.wait_send()

    vm = pl.BlockSpec(memory_space=pltpu.VMEM)
    return pl.pallas_call(
        body, name="gather_small", in_specs=[vm], out_specs=vm,
        out_shape=jax.ShapeDtypeStruct((N_DEV, lv), F32),
        scratch_shapes=[pltpu.SemaphoreType.DMA((N_DEV,))] * 2 + [pltpu.SemaphoreType.DMA],
        compiler_params=pltpu.CompilerParams(vmem_limit_bytes=VMEM_LIMIT),
    )(vec)


PER = N_DEV - 1


def _comm_scratch(n):
    return [pltpu.SemaphoreType.DMA((n * PER,)), pltpu.SemaphoreType.DMA((n * PER,)), pltpu.SemaphoreType.DMA((n,))]


def _gather_copies(ins, outs, send_s, recv_s, local_s):
    n = len(ins)
    me = _my_place()
    x, y, c = me
    sibling = (x, y, 1 - c)
    chips = [(1 - x, y), (x, 1 - y), (1 - x, 1 - y)]

    def copy(a, k, block, to, src=None):
        slot = outs[a].at[_index(block)]
        return pltpu.make_async_remote_copy(src_ref=slot if src is None else src, dst_ref=slot,
                                            send_sem=send_s.at[a * PER + k], recv_sem=recv_s.at[a * PER + k],
                                            device_id=to, device_id_type=MESH)

    mine = [pltpu.make_async_copy(ins[a], outs[a].at[_index(me)], local_s.at[a]) for a in range(n)]
    first = []
    for a in range(n):
        first.append(copy(a, 0, me, sibling, src=ins[a]))
        first += [copy(a, 1 + j, me, (*chip, c), src=ins[a]) for j, chip in enumerate(chips)]
    landed = [copy(a, 1 + j, (*chip, c), me) for j, chip in enumerate(chips) for a in range(n)]
    passed = [copy(a, 4 + j, (*chip, c), sibling) for j, chip in enumerate(chips) for a in range(n)]
    from_sibling = [copy(a, 0, sibling, me) for a in range(n)]
    from_sibling += [copy(a, 4 + j, (*chip, 1 - c), me) for a in range(n) for j, chip in enumerate(chips)]
    return mine, first, landed, passed, from_sibling


def _gather_start(*refs):
    mine, first, _, _, _ = _gather_copies(*refs)
    for cp in mine + first:
        cp.start()


def _gather_forward(*refs):
    _, _, landed, passed, _ = _gather_copies(*refs)
    for got, fwd in zip(landed, passed):
        got.wait_recv()
        fwd.start()


def _gather_finish(*refs):
    mine, first, _, passed, from_sibling = _gather_copies(*refs)
    for cp in from_sibling:
        cp.wait_recv()
    for cp in first + passed:
        cp.wait_send()
    for cp in mine:
        cp.wait()


def _exchange_copies(ins, outs, send_s, recv_s, local_s):
    n = len(ins)
    me = _my_place()
    my_i = _index(me)

    def copy(a, k, to, src_slot, dst_slot):
        return pltpu.make_async_remote_copy(src_ref=ins[a].at[src_slot], dst_ref=outs[a].at[dst_slot],
                                            send_sem=send_s.at[a * PER + k - 1], recv_sem=recv_s.at[a * PER + k - 1],
                                            device_id=to, device_id_type=MESH)

    mine = [pltpu.make_async_copy(ins[a].at[my_i], outs[a].at[my_i], local_s.at[a]) for a in range(n)]
    sends = [copy(a, k, _peer(me, k), _index(_peer(me, k)), my_i) for k in range(1, N_DEV) for a in range(n)]
    recvs = [copy(a, k, me, my_i, _index(_peer(me, k))) for k in range(1, N_DEV) for a in range(n)]
    return mine, sends, recvs


def _exchange_start(*refs):
    mine, sends, _ = _exchange_copies(*refs)
    for cp in mine + sends:
        cp.start()


def _exchange_finish(*refs):
    mine, sends, recvs = _exchange_copies(*refs)
    for cp in recvs:
        cp.wait_recv()
    for cp in sends:
        cp.wait_send()
    for cp in mine:
        cp.wait()


def _gathered_shapes(shards):
    return [jax.ShapeDtypeStruct((N_DEV,) + s.shape, s.dtype) for s in shards]


def _gather_weights(shards):
    n = len(shards)

    def body(*refs):
        parts = (refs[:n], refs[n:2 * n]) + tuple(refs[2 * n:])
        _gather_start(*parts)
        _gather_forward(*parts)
        _gather_finish(*parts)

    hbm = pl.BlockSpec(memory_space=pl.ANY)
    return pl.pallas_call(
        body, name="gather_weights", in_specs=[hbm] * n, out_specs=[hbm] * n,
        out_shape=_gathered_shapes(shards), scratch_shapes=_comm_scratch(n),
    )(*shards)


def _proj_layout(cw, ql, kl):
    lay = {"gb": 0, "gc": cw, "ci": 2 * cw, "ql": 3 * cw}
    assert lay["ql"] % ql == 0
    lay["kv"] = _roundup(lay["ql"] + ql, kl)
    lay["kr"] = lay["kv"] + kl
    lay["np"] = _roundup(lay["kr"] + LANE, 4 * LANE)
    return lay


def _chunks_cols(g):
    r, c8 = g.shape
    return jnp.transpose(g.reshape(r, N_DEV, c8 // N_DEV), (1, 0, 2))


def _from_col_shards(a):
    n, r, c = a.shape
    return jnp.transpose(a, (1, 0, 2)).reshape(r, n * c)


def kernel(x, c, positions, w_ada, b_ada, g_pre_mix, g_post_mix, w_in, g_q, w_uq, g_kv, w_ukv, conv_w_mix, conv_b_mix, w_o, g_pre_ffn, g_post_ffn, w_up, conv_w_ffn, conv_b_ffn, w_down, loss_target, m_w_ada, m_b_ada, m_g_pre_mix, m_g_post_mix, m_w_in, m_g_q, m_w_uq, m_g_kv, m_w_ukv, m_conv_w_mix, m_conv_b_mix, m_w_o, m_g_pre_ffn, m_g_post_ffn, m_w_up, m_conv_w_ffn, m_conv_b_ffn, m_w_down, v_w_ada, v_b_ada, v_g_pre_mix, v_g_post_mix, v_w_in, v_g_q, v_w_uq, v_g_kv, v_w_ukv, v_conv_w_mix, v_conv_b_mix, v_w_o, v_g_pre_ffn, v_g_post_ffn, v_w_up, v_conv_w_ffn, v_conv_b_ffn, v_w_down):
    s_len, d = x.shape[1], x.shape[2]
    ql, kl = w_uq.shape[1], w_ukv.shape[1]
    n_heads = w_ukv.shape[2] * N_DEV // (QK_NOPE + V_DIM)
    cw = conv_w_mix.shape[2] * N_DEV
    f2 = w_up.shape[2] * N_DEV
    ff = f2 // 2
    in_cols = w_in.shape[2] * N_DEV
    ada_c = w_ada.shape[2]
    cwm_c, cwf_c = conv_w_mix.shape[2], conv_w_ffn.shape[2]
    scale = 1.0 / math.sqrt(QK_NOPE + QK_ROPE)
    lay = _proj_layout(cw, ql, kl)
    n_pad = lay["np"]
    my_i = _index(_my_place())

    ts_row = _tile(s_len, PREF["row"], SUB)
    ts_conv = _tile(s_len, PREF["conv_rows"], SUB)
    tc_conv = _tile(cw, PREF["conv_cols"])
    tc_ffn = cwf_c
    ts_ffn = _tile(s_len, PREF["ffn_rows"], SUB)
    pair_order = [k // 2 + (k % 2) * (N_DEV // 2) for k in range(N_DEV)]
    pair_place = [pair_order.index(k) for k in range(N_DEV)]

    def paired(shards):
        return _from_col_shards(jnp.stack([shards[p] for p in pair_order]))

    def unpaired_chunks(g):
        ch = _chunks_cols(g)
        return jnp.stack([ch[p] for p in pair_place])
    ts_qkv = _tile(s_len, PREF["row"], SUB)
    t_attn = _tile(s_len, PREF["attn"])

    x2d, tgt = x[0], loss_target[0]

    vec = jnp.concatenate([c, conv_w_mix[0].reshape(1, -1), conv_w_ffn[0].reshape(1, -1)], axis=1)
    gath, cact, mod_rows = _ada_fwd(vec, w_ada[0], b_ada.reshape(N_DEV, ada_c))
    cwm_full = _from_col_shards(gath[:, d:d + 3 * cwm_c].reshape(N_DEV, 3, cwm_c))
    cwf_shards = gath[:, d + 3 * cwm_c:].reshape(N_DEV, 3, cwf_c)
    cwf_full, cwf_pair = _from_col_shards(cwf_shards), paired(cwf_shards)
    cbf_pair = paired(jnp.transpose(conv_b_ffn.reshape(1, N_DEV, cwf_c), (1, 0, 2)))
    mod = mod_rows.reshape(1, N_DEV * ada_c)
    sh_m, sc_m, gt_m, sh_f, sc_f, gt_f = [mod[:, k * d:(k + 1) * d] for k in range(6)]

    g_in, g_uq, g_ukv = _gather_weights([w_in[0].astype(BF16), w_uq[0].astype(BF16), w_ukv[0].astype(BF16)])
    win = _from_col_shards(g_in)
    cut = np.cumsum([0, ql, kl, QK_ROPE, cw, cw, cw])
    part = [win[:, cut[k]:cut[k + 1]] for k in range(6)]

    def zcols(n):
        return jnp.zeros((d, n), BF16)

    win_p = jnp.concatenate([part[3], part[4], part[5], part[0], zcols(lay["kv"] - lay["ql"] - ql), part[1],
                             part[2], zcols(n_pad - lay["kr"] - QK_ROPE)], axis=1)
    wuq_p = jnp.pad(_from_col_shards(g_uq).reshape(ql, n_heads, QK_NOPE + QK_ROPE),
                    ((0, 0), (0, 0), (0, HEAD_W - QK_NOPE - QK_ROPE))).reshape(ql, n_heads * HEAD_W)
    wukv = _from_col_shards(g_ukv).reshape(kl, n_heads, QK_NOPE + V_DIM)
    wk = wukv[:, :, :QK_NOPE].reshape(kl, n_heads * QK_NOPE)
    wv = wukv[:, :, QK_NOPE:].reshape(kl, n_heads * V_DIM)

    inv_freq = 1.0 / (ROPE_THETA ** (jnp.arange(0, QK_ROPE, 2, dtype=F32) / QK_ROPE))
    inv_row = jnp.tile(inv_freq, LANE // (QK_ROPE // 2)).reshape(1, LANE)
    ctab, atab, btab = _rope_tables(positions.astype(F32).reshape(s_len, 1), inv_row, _tile(s_len, 1024, SUB))

    h1 = _modnorm_fwd(x2d, g_pre_mix, sc_m, sh_m, ts_row)
    proj = _matmul(h1, win_p, out_dtype=BF16, tm=1024, tn=1280, tk=2048, name="mm_proj")
    q, k, v, qn, kvn = _qkv_fwd(proj, lay, wuq_p, wk, wv, g_q, g_kv, ctab, atab, btab, n_heads, ts_qkv, scale)
    attn, lse, g_o, g_up, g_down = _flash_fwd(q, k, v, n_heads, t_attn,
                                              [w_o[0].astype(BF16), w_up[0].astype(BF16), w_down[0].astype(BF16)])
    wo, wup, wdown = g_o.reshape(d, d), paired(g_up), g_down.reshape(ff, d)
    conv_out = _gconv_fwd(proj, lay, cwm_full, conv_b_mix, ts_conv, tc_conv)
    mixcat = jnp.concatenate([attn, conv_out], axis=1)
    mix = _matmul(mixcat, wo, out_dtype=F32, tm=512, tn=2048, tk=2048, name="mm_mix")
    x1, h2 = _post_mix_fwd(mix, x2d, gt_m, g_post_mix, g_pre_ffn, sc_f, sh_f, ts_row)
    up = _matmul(h2, wup, out_dtype=BF16, tm=1024, tn=1408, tk=2048, name="mm_up")
    u, act = _act_fwd(up, cwf_pair, cbf_pair, ts_ffn, tc_ffn)
    y = _matmul(act, wdown, out_dtype=F32, tm=1024, tn=512, tk=ff, name="mm_down")

    dy, dx2, loss_row, d_gt_f, dg_post_ffn = _final_bwd(y, x1, tgt, gt_f, g_post_ffn, ts_row)
    gw_down = _matmul(act, dy, ta=True, out_dtype=BF16, tm=1408, tn=1024, tk=2048, name="mm_gw_down")
    d_act = _matmul(dy, wdown, tb=True, out_dtype=BF16, tm=1024, tn=1408, tk=2048, name="mm_d_act")
    d_up, dcw_pair, dcb_pair = _ffn_act_bwd(d_act, u, up, cwf_pair, ts_ffn, tc_ffn)
    dcw_ffn, dcb_ffn = _from_col_shards(unpaired_chunks(dcw_pair)), _from_col_shards(unpaired_chunks(dcb_pair))
    gw_up_pair, p_down = _matmul(h2, d_up, ta=True, out_dtype=BF16, tm=1024, tn=1408, tk=2048, name="mm_gw_up",
                                 exchange=[gw_down.reshape(N_DEV, ff // N_DEV, d)])
    dh2 = _matmul(d_up, wup, tb=True, out_dtype=F32, tm=512, tn=1024, tk=ff, name="mm_dh2")
    dx1, dmix, d_sh_f, d_sc_f, dg_pre_ffn, d_gt_m, dg_post_mix = _mid_bwd(
        dh2, x1, dx2, mix, g_pre_ffn, sc_f, gt_m, g_post_mix, ts_row)
    gw_o = _matmul(mixcat, dmix, ta=True, out_dtype=BF16, tm=1024, tn=1024, tk=2048, name="mm_gw_o")
    d_mixcat = _matmul(dmix, wo, tb=True, out_dtype=BF16, tm=1024, tn=1024, tk=2048, name="mm_d_mixcat")
    d_gb, d_gc, d_ci, dcw_mix, dcb_mix = _gconv_bwd(d_mixcat, proj, lay, cwm_full, conv_b_mix, ts_conv, tc_conv)
    delta = _delta(attn, d_mixcat, n_heads, _tile(s_len, 512, SUB))
    lse_row = lse[:, :, 0].reshape(n_heads, 1, s_len)
    delta_row = delta[:, :, 0].reshape(n_heads, 1, s_len)
    dq, dk, dv, p_up, p_o = _flash_bwd(q, k, v, d_mixcat, lse_row, delta_row, n_heads, t_attn, scale,
                                       [unpaired_chunks(gw_up_pair), gw_o.reshape(N_DEV, d // N_DEV, d)])
    dq_r, dkn, d_tail, dg_q, dg_kv = _qkv_bwd(dq, dk, dv, proj, lay, wuq_p, wk, wv, g_q, g_kv, ctab, atab, btab,
                                              n_heads, ts_qkv)
    gw_uq_p = _matmul(qn, dq_r, ta=True, out_dtype=BF16, tm=768, tn=1024, tk=2048, name="mm_gw_uq")
    gw_k = _matmul(kvn, dkn, ta=True, out_dtype=BF16, tm=512, tn=1024, tk=2048, name="mm_gw_k")
    gw_v = _matmul(kvn, dv, ta=True, out_dtype=BF16, tm=512, tn=1024, tk=2048, name="mm_gw_v")
    d_proj = jnp.concatenate([d_gb, d_gc, d_ci, d_tail], axis=1)
    gw_in_p = _matmul(h1, d_proj, ta=True, out_dtype=BF16, tm=1024, tn=1280, tk=2048, name="mm_gw_in")

    gw_in = jnp.concatenate([gw_in_p[:, lay["ql"]:lay["ql"] + ql], gw_in_p[:, lay["kv"]:lay["kv"] + kl],
                             gw_in_p[:, lay["kr"]:lay["kr"] + QK_ROPE], gw_in_p[:, :3 * cw]], axis=1)
    gw_uq = gw_uq_p.reshape(ql, n_heads, HEAD_W)[:, :, :QK_NOPE + QK_ROPE].reshape(ql, n_heads * (QK_NOPE + QK_ROPE))
    gw_ukv = jnp.concatenate([gw_k.reshape(kl, n_heads, QK_NOPE), gw_v.reshape(kl, n_heads, V_DIM)],
                             axis=2).reshape(kl, n_heads * (QK_NOPE + V_DIM))
    dh1, p_in, p_uq, p_ukv = _matmul(d_proj, win_p, tb=True, out_dtype=F32, tm=512, tn=1024, tk=n_pad, name="mm_dh1",
                                     exchange=[_chunks_cols(gw_in), _chunks_cols(gw_uq), _chunks_cols(gw_ukv)])
    grad_x, d_sh_m, d_sc_m, dg_pre_mix = _first_bwd(dh1, x2d, dx1, g_pre_mix, sc_m, ts_row)

    dmod = jnp.concatenate([d_sh_m, d_sc_m, d_gt_m, d_sh_f, d_sc_f, d_gt_f], axis=1)
    small_g = [loss_row, dmod, dg_pre_mix, dg_post_mix, dg_q, dg_kv, dcb_mix, dg_pre_ffn, dg_post_ffn, dcb_ffn,
               dcw_mix.reshape(1, -1), dcw_ffn.reshape(1, -1)]
    v_all = _gather_small(jnp.concatenate(small_g, axis=1))

    def place_shard(shard):
        full = jnp.zeros((3, N_DEV * shard.shape[2]), F32)
        return lax.dynamic_update_slice(full, shard[0], (0, my_i * shard.shape[2])).reshape(1, -1)

    def pack(b_ada_, g_pre_mix_, g_post_mix_, g_q_, g_kv_, cb_mix_, g_pre_ffn_, g_post_ffn_, cb_ffn_, cw_mix_, cw_ffn_):
        return jnp.concatenate([jnp.zeros((1, LANE), F32), b_ada_, g_pre_mix_, g_post_mix_, g_q_, g_kv_, cb_mix_,
                                g_pre_ffn_, g_post_ffn_, cb_ffn_, cw_mix_, cw_ffn_], axis=1)

    w_small = pack(b_ada, g_pre_mix, g_post_mix, g_q, g_kv, conv_b_mix, g_pre_ffn, g_post_ffn, conv_b_ffn,
                   cwm_full.reshape(1, -1), cwf_full.reshape(1, -1))
    m_small = pack(m_b_ada, m_g_pre_mix, m_g_post_mix, m_g_q, m_g_kv, m_conv_b_mix, m_g_pre_ffn, m_g_post_ffn,
                   m_conv_b_ffn, place_shard(m_conv_w_mix), place_shard(m_conv_w_ffn))
    v_small = pack(v_b_ada, v_g_pre_mix, v_g_post_mix, v_g_q, v_g_kv, v_conv_b_mix, v_g_pre_ffn, v_g_post_ffn,
                   v_conv_b_ffn, place_shard(v_conv_w_mix), place_shard(v_conv_w_ffn))
    small_out = _adam_small(v_all, w_small, m_small, v_small)

    sizes = [LANE, 6 * d, d, d, ql, kl, cw, d, d, f2, 3 * cw, 3 * f2]
    offs = np.cumsum([0] + sizes)

    def unpack(row):
        parts = [row[:, offs[k]:offs[k + 1]] for k in range(len(sizes))]

        def shard_of(flat, width):
            return lax.dynamic_slice(flat.reshape(3, N_DEV * width), (0, my_i * width), (3, width))[None]

        named = dict(loss=parts[0][0, 0], b_ada=parts[1], g_pre_mix=parts[2], g_post_mix=parts[3], g_q=parts[4],
                     g_kv=parts[5], conv_b_mix=parts[6], g_pre_ffn=parts[7], g_post_ffn=parts[8], conv_b_ffn=parts[9],
                     conv_w_mix=shard_of(parts[10], cwm_c), conv_w_ffn=shard_of(parts[11], cwf_c))
        return named

    small = [unpack(r) for r in small_out]

    dmod_all = v_all[:, offs[1]:offs[2]]
    dmod_sh = lax.dynamic_slice(dmod_all, (0, my_i * ada_c), (N_DEV, ada_c))
    cact_t = jnp.transpose(cact)
    big = dict(
        w_ada=_adam_ada(cact_t, dmod_sh, w_ada[0], m_w_ada[0], v_w_ada[0], _tile(d, 256, SUB)),
        w_in=_adam_parts(p_in, w_in[0], m_w_in[0], v_w_in[0], 256, "adam_w_in"),
        w_uq=_adam_parts(p_uq, w_uq[0], m_w_uq[0], v_w_uq[0], 256, "adam_w_uq"),
        w_ukv=_adam_parts(p_ukv, w_ukv[0], m_w_ukv[0], v_w_ukv[0], 256, "adam_w_ukv"),
        w_o=_adam_parts(p_o, w_o[0], m_w_o[0], v_w_o[0], 128, "adam_w_o"),
        w_up=_adam_parts(p_up, w_up[0], m_w_up[0], v_w_up[0], 256, "adam_w_up"),
        w_down=_adam_parts(p_down, w_down[0], m_w_down[0], v_w_down[0], 176, "adam_w_down"),
    )

    names = ["w_ada", "b_ada", "g_pre_mix", "g_post_mix", "w_in", "g_q", "w_uq", "g_kv", "w_ukv", "conv_w_mix",
             "conv_b_mix", "w_o", "g_pre_ffn", "g_post_ffn", "w_up", "conv_w_ffn", "conv_b_ffn", "w_down"]
    outs = [small[0]["loss"], grad_x[None]]
    for kind in range(4):
        for nm in names:
            outs.append(big[nm][kind][None] if nm in big else small[kind][nm])
    return tuple(outs)
```

```python
import math

import numpy as np
import jax
import jax.numpy as jnp
from jax import lax
from jax.experimental import pallas as pl
from jax.experimental.pallas import tpu as pltpu

F32 = jnp.float32
BF16 = jnp.bfloat16
N_DEV = 8
MESH = pl.DeviceIdType.MESH

QK_NOPE = 128
QK_ROPE = 64
V_DIM = 128
HEAD_W = 256
LANE = 128
SUB = 8
RMS_EPS = 1e-6
ROPE_THETA = 10000.0
ADAM_LR = 0.001
ADAM_B1 = 0.9
ADAM_B2 = 0.999
ADAM_EPS = 1e-08
ADAM_WD = 0.01
ADAM_STEP = 10
NEG = -1e30
LOG2E = 1.4426950408889634
LN2 = 0.6931471805599453
VMEM_LIMIT = 56 * 1024 * 1024

PREF = {"row": 256, "conv_rows": 512, "conv_cols": 512, "ffn_rows": 256, "attn": 1024, "attn_chunk": 1024}

NT_DIMS = (((1,), (1,)), ((), ()))
TN_DIMS = (((0,), (0,)), ((), ()))


def _cparams(*sem):
    return pltpu.CompilerParams(dimension_semantics=sem if sem else None, vmem_limit_bytes=VMEM_LIMIT)


def _tile(n, pref, unit=LANE):
    if n <= pref:
        return n
    t = (pref // unit) * unit
    while t >= unit:
        if n % t == 0:
            return t
        t -= unit
    return n


def _roundup(n, m):
    return (n + m - 1) // m * m


def _rsq(x):
    return lax.rsqrt(jnp.mean(x * x, axis=-1, keepdims=True) + RMS_EPS)


def _colsum(x):
    return jnp.sum(x, axis=0, keepdims=True)


def _sigmoid(x):
    return 1.0 / (1.0 + jnp.exp(-x))


def _matmul(a, b, *, ta=False, tb=False, out_dtype, tm, tn, tk, name, exchange=None, b_shard_of=None,
            out_shard_of=None):
    m_dim, k_dim = (a.shape[1], a.shape[0]) if ta else a.shape
    if b_shard_of is not None:
        n_dim, tn = b.shape[0] * b.shape[2], b.shape[2]
    else:
        n_dim = b.shape[0] if tb else b.shape[1]
    if out_shard_of is not None:
        tn = n_dim // N_DEV
    tm, tn, tk = _tile(m_dim, tm), _tile(n_dim, tn), _tile(k_dim, tk)
    gi, gj, nk = m_dim // tm, n_dim // tn, k_dim // tk
    dims = (((0 if ta else 1,), (1 if tb else 0,)), ((), ()))
    chunks = list(exchange or [])
    nx = len(chunks)

    def body(*refs):
        a_ref, b_ref = refs[:2]
        xin, o_ref, xout = refs[2:2 + nx], refs[2 + nx], refs[3 + nx:3 + 2 * nx]
        scratch = refs[3 + 2 * nx:]
        sems = scratch[1:] if nk > 1 else scratch
        i, j, k = pl.program_id(0), pl.program_id(1), pl.program_id(2)
        if nx:
            @pl.when((i == 0) & (j == 0) & (k == 0))
            def _():
                _exchange_start(xin, xout, *sems)

        part = lax.dot_general(a_ref[...], b_ref[...], dims, preferred_element_type=F32)
        if nk == 1:
            o_ref[...] = part.astype(o_ref.dtype)
        else:
            acc_ref = scratch[0]

            @pl.when(k == 0)
            def _():
                acc_ref[...] = part

            @pl.when(k > 0)
            def _():
                acc_ref[...] += part

            @pl.when(k == nk - 1)
            def _():
                o_ref[...] = acc_ref[...].astype(o_ref.dtype)

        if nx:
            @pl.when((i == gi - 1) & (j == gj - 1) & (k == nk - 1))
            def _():
                _exchange_finish(xin, xout, *sems)

    a_spec = pl.BlockSpec((tk, tm), lambda i, j, k: (k, i)) if ta else pl.BlockSpec((tm, tk), lambda i, j, k: (i, k))
    b_spec = pl.BlockSpec((tn, tk), lambda i, j, k: (j, k)) if tb else pl.BlockSpec((tk, tn), lambda i, j, k: (k, j))
    if b_shard_of is not None:
        b_spec = pl.BlockSpec((None, tk, tn), lambda i, j, k: (b_shard_of(j), k, 0))
    o_spec, o_shape = pl.BlockSpec((tm, tn), lambda i, j, k: (i, j)), (m_dim, n_dim)
    if out_shard_of is not None:
        o_spec, o_shape = pl.BlockSpec((None, tm, tn), lambda i, j, k: (out_shard_of(j), i, 0)), (N_DEV, m_dim, tn)
    hbm = pl.BlockSpec(memory_space=pl.ANY)
    out = pl.pallas_call(
        body,
        name=name,
        grid=(gi, gj, nk),
        in_specs=[a_spec, b_spec] + [hbm] * nx,
        out_specs=[o_spec] + [hbm] * nx,
        out_shape=[jax.ShapeDtypeStruct(o_shape, out_dtype)] + [jax.ShapeDtypeStruct(c.shape, c.dtype) for c in chunks],
        scratch_shapes=([pltpu.VMEM((tm, tn), F32)] if nk > 1 else []) + (_comm_scratch(nx) if nx else []),
        compiler_params=_cparams(*(("arbitrary",) * 3 if nx else ("parallel", "parallel", "arbitrary"))),
    )(a, b, *chunks)
    return out if nx else out[0]


def _matmul_pair_shards(a, shards, *, out_dtype, tm, tn, name):
    m_dim = a.shape[0]
    n_sh, n_dim, c = shards.shape
    half = n_sh // 2
    tm, tn = _tile(m_dim, tm), _tile(n_dim, tn)

    def body(a_ref, b0_ref, b1_ref, o_ref, acc_ref):
        k = pl.program_id(2)
        part = (lax.dot_general(a_ref[:, :c], b0_ref[...], NT_DIMS, preferred_element_type=F32)
                + lax.dot_general(a_ref[:, c:], b1_ref[...], NT_DIMS, preferred_element_type=F32))

        @pl.when(k == 0)
        def _():
            acc_ref[...] = part

        @pl.when(k > 0)
        def _():
            acc_ref[...] += part

        @pl.when(k == half - 1)
        def _():
            o_ref[...] = acc_ref[...].astype(o_ref.dtype)

    return pl.pallas_call(
        body, name=name, grid=(m_dim // tm, n_dim // tn, half),
        in_specs=[pl.BlockSpec((tm, 2 * c), lambda i, j, k: (i, k)),
                  pl.BlockSpec((None, tn, c), lambda i, j, k: (k, j, 0)),
                  pl.BlockSpec((None, tn, c), lambda i, j, k: (k + half, j, 0))],
        out_specs=pl.BlockSpec((tm, tn), lambda i, j, k: (i, j)),
        out_shape=jax.ShapeDtypeStruct((m_dim, n_dim), out_dtype),
        scratch_shapes=[pltpu.VMEM((tm, tn), F32)],
        compiler_params=_cparams("parallel", "parallel", "arbitrary"),
    )(a, shards, shards)


def _shift_down(x, halo, n):
    r = pltpu.roll(x, n, 0)
    hr = pltpu.roll(halo, n, 0)
    row = lax.broadcasted_iota(jnp.int32, halo.shape, 0)
    top = jnp.where(row < n, hr, r[:SUB])
    return jnp.concatenate([top, r[SUB:]], axis=0)


def _shift_up(x, halo, n):
    ts = x.shape[0]
    r = pltpu.roll(x, ts - n, 0)
    hr = pltpu.roll(halo, SUB - n, 0)
    row = lax.broadcasted_iota(jnp.int32, halo.shape, 0)
    bot = jnp.where(row >= SUB - n, hr, r[ts - SUB:])
    return jnp.concatenate([r[:ts - SUB], bot], axis=0)


def _conv3(x, halo, w_ref, b_ref):
    return _shift_down(x, halo, 2) * w_ref[0:1, :] + _shift_down(x, halo, 1) * w_ref[1:2, :] + x * w_ref[2:3, :] + b_ref[...]


def _prev_halo(ts, col):
    return lambda j, i: (jnp.maximum(i * (ts // SUB) - 1, 0), col(j))


def _next_halo(ts, n_rows, col):
    return lambda j, i: (jnp.minimum((i + 1) * (ts // SUB), n_rows // SUB - 1), col(j))


def _modnorm_fwd(x, g, sc, sh, ts):
    s_len, d = x.shape

    def body(x_ref, g_ref, sc_ref, sh_ref, h_ref):
        xv = x_ref[...]
        h_ref[...] = ((xv * _rsq(xv) * g_ref[...]) * (1.0 + sc_ref[...]) + sh_ref[...]).astype(BF16)

    vec = pl.BlockSpec((1, d), lambda i: (0, 0))
    return pl.pallas_call(
        body, name="modnorm_fwd", grid=(s_len // ts,),
        in_specs=[pl.BlockSpec((ts, d), lambda i: (i, 0)), vec, vec, vec],
        out_specs=pl.BlockSpec((ts, d), lambda i: (i, 0)),
        out_shape=jax.ShapeDtypeStruct((s_len, d), BF16),
        compiler_params=_cparams("parallel"),
    )(x, g, sc, sh)


def _rope_tables(pos_col, inv_freq_row, ts):
    s_len = pos_col.shape[0]
    half = QK_ROPE // 2

    def body(p_ref, f_ref, c_ref, a_ref, b_ref):
        ang = p_ref[...] * f_ref[...]
        lane = lax.broadcasted_iota(jnp.int32, ang.shape, 1)
        cos, sin = jnp.cos(ang), jnp.sin(ang)
        c_ref[...] = jnp.where(lane < 2 * half, cos, 0.0)
        a_ref[...] = jnp.where(lane < half, -sin, 0.0)
        b_ref[...] = jnp.where((lane >= half) & (lane < 2 * half), sin, 0.0)

    out = jax.ShapeDtypeStruct((s_len, LANE), F32)
    blk = pl.BlockSpec((ts, LANE), lambda i: (i, 0))
    return pl.pallas_call(
        body, name="rope_tables", grid=(s_len // ts,),
        in_specs=[pl.BlockSpec((ts, 1), lambda i: (i, 0)), pl.BlockSpec((1, LANE), lambda i: (0, 0))],
        out_specs=[blk, blk, blk], out_shape=[out, out, out],
        compiler_params=_cparams("parallel"),
    )(pos_col, inv_freq_row)


def _rope(seg, c, a, b):
    return seg * c + pltpu.roll(seg, LANE - QK_ROPE // 2, 1) * a + pltpu.roll(seg, QK_ROPE // 2, 1) * b


def _rope_t(seg, c, a, b):
    return seg * c - pltpu.roll(seg, LANE - QK_ROPE // 2, 1) * a - pltpu.roll(seg, QK_ROPE // 2, 1) * b


def _qkv_fwd(proj, lay, wuq, wk, wv, g_q, g_kv, ctab, atab, btab, n_heads, ts, scale):
    s_len = proj.shape[0]
    ql_w, kl_w = wuq.shape[0], wk.shape[0]

    def body(ql_ref, kl_ref, kr_ref, wuq_ref, wk_ref, wv_ref, gq_ref, gkv_ref, c_ref, a_ref, b_ref,
             q_out, k_out, v_out, qn_out, kvn_out):
        c, a, b = c_ref[...], a_ref[...], b_ref[...]
        ql = ql_ref[...].astype(F32)
        qn = (ql * _rsq(ql) * gq_ref[...]).astype(BF16)
        qn_out[...] = qn
        q = jnp.dot(qn, wuq_ref[...], preferred_element_type=F32)
        kl = kl_ref[...].astype(F32)
        kvn = (kl * _rsq(kl) * gkv_ref[...]).astype(BF16)
        kvn_out[...] = kvn
        kn = jnp.dot(kvn, wk_ref[...], preferred_element_type=F32)
        v_out[...] = jnp.dot(kvn, wv_ref[...], preferred_element_type=F32).astype(BF16)
        kr = _rope(kr_ref[...].astype(F32), c, a, b).astype(BF16)
        for h in range(n_heads):
            o = h * HEAD_W
            q_out[:, o:o + QK_NOPE] = (q[:, o:o + QK_NOPE] * scale).astype(BF16)
            q_out[:, o + QK_NOPE:o + HEAD_W] = (_rope(q[:, o + QK_NOPE:o + HEAD_W], c, a, b) * scale).astype(BF16)
            k_out[:, o:o + QK_NOPE] = kn[:, h * QK_NOPE:(h + 1) * QK_NOPE].astype(BF16)
            k_out[:, o + QK_NOPE:o + HEAD_W] = kr

    def full(arr):
        return pl.BlockSpec(arr.shape, lambda i: (0, 0))

    tab = pl.BlockSpec((ts, LANE), lambda i: (i, 0))
    hw, hv = n_heads * HEAD_W, n_heads * V_DIM
    return pl.pallas_call(
        body, name="qkv_fwd", grid=(s_len // ts,),
        in_specs=[pl.BlockSpec((ts, ql_w), lambda i: (i, lay["ql"] // ql_w)),
                  pl.BlockSpec((ts, kl_w), lambda i: (i, lay["kv"] // kl_w)),
                  pl.BlockSpec((ts, LANE), lambda i: (i, lay["kr"] // LANE)),
                  full(wuq), full(wk), full(wv), full(g_q), full(g_kv), tab, tab, tab],
        out_specs=[pl.BlockSpec((ts, hw), lambda i: (i, 0)), pl.BlockSpec((ts, hw), lambda i: (i, 0)),
                   pl.BlockSpec((ts, hv), lambda i: (i, 0)), pl.BlockSpec((ts, ql_w), lambda i: (i, 0)),
                   pl.BlockSpec((ts, kl_w), lambda i: (i, 0))],
        out_shape=[jax.ShapeDtypeStruct((s_len, hw), BF16), jax.ShapeDtypeStruct((s_len, hw), BF16),
                   jax.ShapeDtypeStruct((s_len, hv), BF16), jax.ShapeDtypeStruct((s_len, ql_w), BF16),
                   jax.ShapeDtypeStruct((s_len, kl_w), BF16)],
        compiler_params=_cparams("parallel"),
    )(proj, proj, proj, wuq, wk, wv, g_q, g_kv, ctab, atab, btab)


def _flash_fwd(q, k, v, n_heads, t, gather):
    ng = len(gather)
    ck = _tile(t, PREF["attn_chunk"])
    s_len = q.shape[0]
    nb = s_len // t
    pairs = [(i, j) for i in range(nb) for j in range(i + 1)]
    itab = jnp.asarray(np.array([p[0] for p in pairs], np.int32))
    jtab = jnp.asarray(np.array([p[1] for p in pairs], np.int32))

    n_steps = len(pairs)

    def body(it_ref, jt_ref, q_ref, k_ref, v_ref, *rest):
        gin, (o_ref, lse_ref), gout = rest[:ng], rest[ng:ng + 2], rest[ng + 2:2 * ng + 2]
        m_sc, l_sc, acc_sc = rest[2 * ng + 2:2 * ng + 5]
        sems = rest[2 * ng + 5:]
        head, step_id = pl.program_id(0), pl.program_id(1)
        i, j = it_ref[step_id], jt_ref[step_id]

        @pl.when((head == 0) & (step_id == 0))
        def _():
            _gather_start(gin, gout, *sems)

        @pl.when((head == n_heads // 2) & (step_id == 0))
        def _():
            _gather_forward(gin, gout, *sems)

        @pl.when(j == 0)
        def _():
            m_sc[...] = jnp.full(m_sc.shape, NEG, F32)
            l_sc[...] = jnp.zeros(l_sc.shape, F32)
            acc_sc[...] = jnp.zeros(acc_sc.shape, F32)

        def step(diag):
            for c in range(t // ck):
                q0 = c * ck if diag else 0
                qs, ks = slice(q0, t), slice(c * ck, (c + 1) * ck)
                s_t = lax.dot_general(k_ref[ks, :], q_ref[qs, :], NT_DIMS, preferred_element_type=F32)
                if diag:
                    krow = lax.broadcasted_iota(jnp.int32, s_t.shape, 0)
                    qcol = lax.broadcasted_iota(jnp.int32, s_t.shape, 1)
                    s_t = jnp.where(krow <= qcol, s_t, NEG)
                m_prev = m_sc[:, qs]
                m_new = jnp.maximum(m_prev, jnp.max(s_t, axis=0, keepdims=True))
                alpha = jnp.exp2(m_prev - m_new)
                p_t = jnp.exp2(s_t - m_new)
                l_sc[:, qs] = alpha * l_sc[:, qs] + jnp.sum(p_t, axis=0, keepdims=True)
                acc_sc[:, qs] = acc_sc[:, qs] * alpha + lax.dot_general(v_ref[ks, :], p_t.astype(BF16), TN_DIMS,
                                                                        preferred_element_type=F32)
                m_sc[:, qs] = m_new

        @pl.when(j < i)
        def _():
            step(False)

        @pl.when(j == i)
        def _():
            step(True)
            l = l_sc[...]
            o_ref[...] = jnp.transpose(acc_sc[...] / l).astype(BF16)
            lse_ref[0] = m_sc[...] + jnp.log(l) * LOG2E

        @pl.when((head == n_heads - 1) & (step_id == n_steps - 1))
        def _():
            _gather_finish(gin, gout, *sems)

    hbm = pl.BlockSpec(memory_space=pl.ANY)
    grid_spec = pltpu.PrefetchScalarGridSpec(
        num_scalar_prefetch=2, grid=(n_heads, n_steps),
        in_specs=[pl.BlockSpec((t, HEAD_W), lambda h, s, it, jt: (it[s], h)),
                  pl.BlockSpec((t, HEAD_W), lambda h, s, it, jt: (jt[s], h)),
                  pl.BlockSpec((t, V_DIM), lambda h, s, it, jt: (jt[s], h))] + [hbm] * ng,
        out_specs=[pl.BlockSpec((t, V_DIM), lambda h, s, it, jt: (it[s], h)),
                   pl.BlockSpec((1, 1, t), lambda h, s, it, jt: (h, 0, it[s]))] + [hbm] * ng,
        scratch_shapes=[pltpu.VMEM((1, t), F32), pltpu.VMEM((1, t), F32), pltpu.VMEM((V_DIM, t), F32)]
        + _comm_scratch(ng),
    )
    return pl.pallas_call(
        body, name="flash_fwd", grid_spec=grid_spec,
        out_shape=[jax.ShapeDtypeStruct((s_len, n_heads * V_DIM), BF16),
                   jax.ShapeDtypeStruct((n_heads, 1, s_len), F32)] + _gathered_shapes(gather),
        compiler_params=_cparams("arbitrary", "arbitrary"),
    )(itab, jtab, q, k, v, *gather)


def _gconv_fwd(proj, lay, w, b, ts, tc):
    s_len = proj.shape[0]
    cw = w.shape[1]
    nj = cw // tc

    def body(gb_ref, gc_ref, ci_ref, gch_ref, cih_ref, w_ref, b_ref, o_ref):
        i = pl.program_id(1)
        p = gc_ref[...].astype(F32) * ci_ref[...].astype(F32)
        ph = jnp.where(i > 0, gch_ref[...].astype(F32) * cih_ref[...].astype(F32), 0.0)
        o_ref[...] = (gb_ref[...].astype(F32) * _conv3(p, ph, w_ref, b_ref)).astype(BF16)

    def blk(off):
        return pl.BlockSpec((ts, tc), lambda j, i: (i, off // tc + j))

    def halo(off):
        return pl.BlockSpec((SUB, tc), _prev_halo(ts, lambda j: off // tc + j))

    return pl.pallas_call(
        body, name="gconv_fwd", grid=(nj, s_len // ts),
        in_specs=[blk(lay["gb"]), blk(lay["gc"]), blk(lay["ci"]), halo(lay["gc"]), halo(lay["ci"]),
                  pl.BlockSpec((3, tc), lambda j, i: (0, j)), pl.BlockSpec((1, tc), lambda j, i: (0, j))],
        out_specs=pl.BlockSpec((ts, tc), lambda j, i: (i, j)),
        out_shape=jax.ShapeDtypeStruct((s_len, cw), BF16),
        compiler_params=_cparams("parallel", "parallel"),
    )(proj, proj, proj, proj, proj, w, b)


def _post_mix_fwd(mix, x, gt, g_post, g_pre, sc, sh, ts):
    s_len, d = x.shape

    def body(mix_ref, x_ref, gt_ref, gp_ref, g2_ref, sc_ref, sh_ref, x1_ref, h2_ref):
        mv = mix_ref[...]
        x1 = x_ref[...] + gt_ref[...] * (mv * _rsq(mv) * gp_ref[...])
        x1_ref[...] = x1
        h2_ref[...] = ((x1 * _rsq(x1) * g2_ref[...]) * (1.0 + sc_ref[...]) + sh_ref[...]).astype(BF16)

    vec = pl.BlockSpec((1, d), lambda i: (0, 0))
    row = pl.BlockSpec((ts, d), lambda i: (i, 0))
    return pl.pallas_call(
        body, name="post_mix_fwd", grid=(s_len // ts,),
        in_specs=[row, row, vec, vec, vec, vec, vec], out_specs=[row, row],
        out_shape=[jax.ShapeDtypeStruct((s_len, d), F32), jax.ShapeDtypeStruct((s_len, d), BF16)],
        compiler_params=_cparams("parallel"),
    )(mix, x, gt, g_post, g_pre, sc, sh)


def _act_fwd(up, w, b, ts, tc):
    s_len, f2 = up.shape
    nh = f2 // 2 // tc

    def body(up_ref, uph_ref, w_ref, b_ref, u_ref, o_ref):
        i = pl.program_id(1)
        halo = jnp.where(i > 0, uph_ref[...].astype(F32), 0.0)
        u = _conv3(up_ref[...].astype(F32), halo, w_ref, b_ref)
        u_ref[...] = u.astype(BF16)
        ua, ug = u[:, :tc], u[:, tc:]
        o_ref[...] = (ug * _sigmoid(ug) * ua).astype(BF16)

    pair = pl.BlockSpec((ts, 2 * tc), lambda j, i: (i, j))
    return pl.pallas_call(
        body, name="act_fwd", grid=(nh, s_len // ts),
        in_specs=[pair, pl.BlockSpec((SUB, 2 * tc), _prev_halo(ts, lambda j: j)),
                  pl.BlockSpec((3, 2 * tc), lambda j, i: (0, j)), pl.BlockSpec((1, 2 * tc), lambda j, i: (0, j))],
        out_specs=[pair, pl.BlockSpec((ts, tc), lambda j, i: (i, j))],
        out_shape=[jax.ShapeDtypeStruct((s_len, f2), BF16), jax.ShapeDtypeStruct((s_len, f2 // 2), BF16)],
        compiler_params=_cparams("parallel", "parallel"),
    )(up, up, w, b)


def _final_bwd(y, x1, tgt, gt, g_post, ts):
    s_len, d = y.shape

    def body(y_ref, x1_ref, t_ref, gt_ref, g_ref, dy_ref, dx2_ref, loss_ref, dgt_ref, dg_ref):
        i = pl.program_id(0)
        yv = y_ref[...]
        r = _rsq(yv)
        yh = yv * r
        n = yh * g_ref[...]
        e = x1_ref[...] + gt_ref[...] * n - t_ref[...]
        loss = 0.5 * jnp.sum(jnp.mean(e * e, axis=-1, keepdims=True), axis=0, keepdims=True)
        dx2 = e * (1.0 / d)
        dx2_ref[...] = dx2
        dn = dx2 * gt_ref[...]
        dyh = dn * g_ref[...]
        dy_ref[...] = (r * (dyh - yh * jnp.mean(dyh * yh, axis=-1, keepdims=True))).astype(BF16)

        @pl.when(i == 0)
        def _():
            loss_ref[...] = jnp.zeros(loss_ref.shape, F32)
            dgt_ref[...] = jnp.zeros(dgt_ref.shape, F32)
            dg_ref[...] = jnp.zeros(dg_ref.shape, F32)

        loss_ref[...] += jnp.broadcast_to(loss, loss_ref.shape)
        dgt_ref[...] += _colsum(dx2 * n)
        dg_ref[...] += _colsum(dn * yh)

    vec = pl.BlockSpec((1, d), lambda i: (0, 0))
    row = pl.BlockSpec((ts, d), lambda i: (i, 0))
    vshape = jax.ShapeDtypeStruct((1, d), F32)
    return pl.pallas_call(
        body, name="final_bwd", grid=(s_len // ts,),
        in_specs=[row, row, row, vec, vec],
        out_specs=[row, row, pl.BlockSpec((1, LANE), lambda i: (0, 0)), vec, vec],
        out_shape=[jax.ShapeDtypeStruct((s_len, d), BF16), jax.ShapeDtypeStruct((s_len, d), F32),
                   jax.ShapeDtypeStruct((1, LANE), F32), vshape, vshape],
        compiler_params=_cparams("arbitrary"),
    )(y, x1, tgt, gt, g_post)


def _ffn_act_bwd(d_act, u, up, w, ts, tc):
    s_len, f2 = up.shape
    nh = f2 // 2 // tc
    n_i = s_len // ts

    def gate(dv, uv):
        ua, ug = uv[:, :tc], uv[:, tc:]
        sg = _sigmoid(ug)
        return jnp.concatenate([dv * (ug * sg), dv * ua * (sg * (1.0 + ug * (1.0 - sg)))], axis=1)

    def body(d_ref, dh_ref, u_ref, uh_ref, x_ref, w_ref, dx_ref, dw_ref, db_ref):
        i = pl.program_id(1)
        du = gate(d_ref[...].astype(F32), u_ref[...].astype(F32))
        duh = jnp.where(i < n_i - 1, gate(dh_ref[...].astype(F32), uh_ref[...].astype(F32)), 0.0)
        du1, du2 = _shift_up(du, duh, 1), _shift_up(du, duh, 2)
        dx_ref[...] = (du * w_ref[2:3, :] + du1 * w_ref[1:2, :] + du2 * w_ref[0:1, :]).astype(BF16)

        @pl.when(i == 0)
        def _():
            dw_ref[...] = jnp.zeros(dw_ref.shape, F32)
            db_ref[...] = jnp.zeros(db_ref.shape, F32)

        xv = x_ref[...].astype(F32)
        dw_ref[0:1, :] += _colsum(du2 * xv)
        dw_ref[1:2, :] += _colsum(du1 * xv)
        dw_ref[2:3, :] += _colsum(du * xv)
        db_ref[...] += _colsum(du)

    pair = pl.BlockSpec((ts, 2 * tc), lambda j, i: (i, j))
    pair_halo = pl.BlockSpec((SUB, 2 * tc), _next_halo(ts, s_len, lambda j: j))
    return pl.pallas_call(
        body, name="ffn_act_bwd", grid=(nh, n_i),
        in_specs=[pl.BlockSpec((ts, tc), lambda j, i: (i, j)), pl.BlockSpec((SUB, tc), _next_halo(ts, s_len, lambda j: j)),
                  pair, pair_halo, pair, pl.BlockSpec((3, 2 * tc), lambda j, i: (0, j))],
        out_specs=[pair, pl.BlockSpec((3, 2 * tc), lambda j, i: (0, j)), pl.BlockSpec((1, 2 * tc), lambda j, i: (0, j))],
        out_shape=[jax.ShapeDtypeStruct((s_len, f2), BF16), jax.ShapeDtypeStruct((3, f2), F32),
                   jax.ShapeDtypeStruct((1, f2), F32)],
        compiler_params=_cparams("parallel", "arbitrary"),
    )(d_act, d_act, u, u, up, w)


def _mid_bwd(dh2, x1, dx2, mix, g_pre, sc, gt_m, g_post, ts):
    s_len, d = x1.shape

    def body(dh_ref, x1_ref, dx2_ref, mix_ref, g_ref, sc_ref, gt_ref, gp_ref,
             dx1_ref, dmix_ref, dsh_ref, dsc_ref, dg_ref, dgt_ref, dgp_ref):
        i = pl.program_id(0)
        dh = dh_ref[...]
        x1 = x1_ref[...]
        r1 = _rsq(x1)
        xh = x1 * r1
        dxh = dh * (1.0 + sc_ref[...]) * g_ref[...]
        dx1 = dx2_ref[...] + r1 * (dxh - xh * jnp.mean(dxh * xh, axis=-1, keepdims=True))
        dx1_ref[...] = dx1
        mv = mix_ref[...]
        rm = _rsq(mv)
        mh = mv * rm
        dn = dx1 * gt_ref[...]
        dmh = dn * gp_ref[...]
        dmix_ref[...] = (rm * (dmh - mh * jnp.mean(dmh * mh, axis=-1, keepdims=True))).astype(BF16)

        @pl.when(i == 0)
        def _():
            for ref in (dsh_ref, dsc_ref, dg_ref, dgt_ref, dgp_ref):
                ref[...] = jnp.zeros(ref.shape, F32)

        dsh_ref[...] += _colsum(dh)
        dsc_ref[...] += _colsum(dh * (xh * g_ref[...]))
        dg_ref[...] += _colsum(dh * (1.0 + sc_ref[...]) * xh)
        dgt_ref[...] += _colsum(dx1 * (mh * gp_ref[...]))
        dgp_ref[...] += _colsum(dn * mh)

    vec = pl.BlockSpec((1, d), lambda i: (0, 0))
    row = pl.BlockSpec((ts, d), lambda i: (i, 0))
    vshape = jax.ShapeDtypeStruct((1, d), F32)
    return pl.pallas_call(
        body, name="mid_bwd", grid=(s_len // ts,),
        in_specs=[row, row, row, row, vec, vec, vec, vec],
        out_specs=[row, row, vec, vec, vec, vec, vec],
        out_shape=[jax.ShapeDtypeStruct((s_len, d), F32), jax.ShapeDtypeStruct((s_len, d), BF16)] + [vshape] * 5,
        compiler_params=_cparams("arbitrary"),
    )(dh2, x1, dx2, mix, g_pre, sc, gt_m, g_post)


def _first_bwd(dh1, x, dx1, g_pre, sc, ts):
    s_len, d = x.shape

    def body(dh_ref, x_ref, dx1_ref, g_ref, sc_ref, dx_ref, dsh_ref, dsc_ref, dg_ref):
        i = pl.program_id(0)
        dh = dh_ref[...]
        xv = x_ref[...]
        r = _rsq(xv)
        xh = xv * r
        dxh = dh * (1.0 + sc_ref[...]) * g_ref[...]
        dx_ref[...] = dx1_ref[...] + r * (dxh - xh * jnp.mean(dxh * xh, axis=-1, keepdims=True))

        @pl.when(i == 0)
        def _():
            for ref in (dsh_ref, dsc_ref, dg_ref):
                ref[...] = jnp.zeros(ref.shape, F32)

        dsh_ref[...] += _colsum(dh)
        dsc_ref[...] += _colsum(dh * (xh * g_ref[...]))
        dg_ref[...] += _colsum(dh * (1.0 + sc_ref[...]) * xh)

    vec = pl.BlockSpec((1, d), lambda i: (0, 0))
    row = pl.BlockSpec((ts, d), lambda i: (i, 0))
    vshape = jax.ShapeDtypeStruct((1, d), F32)
    return pl.pallas_call(
        body, name="first_bwd", grid=(s_len // ts,),
        in_specs=[row, row, row, vec, vec], out_specs=[row, vec, vec, vec],
        out_shape=[jax.ShapeDtypeStruct((s_len, d), F32), vshape, vshape, vshape],
        compiler_params=_cparams("arbitrary"),
    )(dh1, x, dx1, g_pre, sc)


def _gconv_bwd(d_mixcat, proj, lay, w, b, ts, tc):
    s_len = proj.shape[0]
    cw = w.shape[1]
    n_i = s_len // ts
    dc_off = d_mixcat.shape[1] - cw

    def body(dc_ref, dch_ref, gb_ref, gbh_ref, gc_ref, gch_ref, ci_ref, cih_ref, w_ref, b_ref,
             dgb_ref, dgc_ref, dci_ref, dw_ref, db_ref):
        i = pl.program_id(1)
        gc, ci = gc_ref[...].astype(F32), ci_ref[...].astype(F32)
        p = gc * ci
        ph = jnp.where(i > 0, gch_ref[...].astype(F32) * cih_ref[...].astype(F32), 0.0)
        pm1, pm2 = _shift_down(p, ph, 1), _shift_down(p, ph, 2)
        z = pm2 * w_ref[0:1, :] + pm1 * w_ref[1:2, :] + p * w_ref[2:3, :] + b_ref[...]
        dc = dc_ref[...].astype(F32)
        dgb_ref[...] = (dc * z).astype(BF16)
        dz = dc * gb_ref[...].astype(F32)
        dzh = jnp.where(i < n_i - 1, dch_ref[...].astype(F32) * gbh_ref[...].astype(F32), 0.0)
        dz1, dz2 = _shift_up(dz, dzh, 1), _shift_up(dz, dzh, 2)
        dp = dz * w_ref[2:3, :] + dz1 * w_ref[1:2, :] + dz2 * w_ref[0:1, :]
        dgc_ref[...] = (dp * ci).astype(BF16)
        dci_ref[...] = (dp * gc).astype(BF16)

        @pl.when(i == 0)
        def _():
            dw_ref[...] = jnp.zeros(dw_ref.shape, F32)
            db_ref[...] = jnp.zeros(db_ref.shape, F32)

        dw_ref[0:1, :] += _colsum(dz2 * p)
        dw_ref[1:2, :] += _colsum(dz1 * p)
        dw_ref[2:3, :] += _colsum(dz * p)
        db_ref[...] += _colsum(dz)

    def blk(off):
        return pl.BlockSpec((ts, tc), lambda j, i: (i, off // tc + j))

    def prev(off):
        return pl.BlockSpec((SUB, tc), _prev_halo(ts, lambda j: off // tc + j))

    def nxt(off):
        return pl.BlockSpec((SUB, tc), _next_halo(ts, s_len, lambda j: off // tc + j))

    out_blk = pl.BlockSpec((ts, tc), lambda j, i: (i, j))
    act = jax.ShapeDtypeStruct((s_len, cw), BF16)
    return pl.pallas_call(
        body, name="gconv_bwd", grid=(cw // tc, n_i),
        in_specs=[blk(dc_off), nxt(dc_off), blk(lay["gb"]), nxt(lay["gb"]), blk(lay["gc"]), prev(lay["gc"]),
                  blk(lay["ci"]), prev(lay["ci"]),
                  pl.BlockSpec((3, tc), lambda j, i: (0, j)), pl.BlockSpec((1, tc), lambda j, i: (0, j))],
        out_specs=[out_blk, out_blk, out_blk,
                   pl.BlockSpec((3, tc), lambda j, i: (0, j)), pl.BlockSpec((1, tc), lambda j, i: (0, j))],
        out_shape=[act, act, act, jax.ShapeDtypeStruct((3, cw), F32), jax.ShapeDtypeStruct((1, cw), F32)],
        compiler_params=_cparams("parallel", "arbitrary"),
    )(d_mixcat, d_mixcat, proj, proj, proj, proj, proj, proj, w, b)


def _delta(o, d_mixcat, n_heads, ts):
    s_len = o.shape[0]

    def body(o_ref, do_ref, out_ref):
        for h in range(n_heads):
            sl = slice(h * V_DIM, (h + 1) * V_DIM)
            prod = o_ref[:, sl].astype(F32) * do_ref[:, sl].astype(F32)
            out_ref[h] = jnp.broadcast_to(jnp.sum(prod, axis=1, keepdims=True), (ts, LANE))

    hv = n_heads * V_DIM
    return pl.pallas_call(
        body, name="attn_delta", grid=(s_len // ts,),
        in_specs=[pl.BlockSpec((ts, hv), lambda i: (i, 0)), pl.BlockSpec((ts, hv), lambda i: (i, 0))],
        out_specs=pl.BlockSpec((n_heads, ts, LANE), lambda i: (0, i, 0)),
        out_shape=jax.ShapeDtypeStruct((n_heads, s_len, LANE), F32),
        compiler_params=_cparams("parallel"),
    )(o, d_mixcat)


def _flash_bwd(q, k, v, d_mixcat, lse_row, delta_row, n_heads, t, scale, exchange):
    nx = len(exchange)
    s_len = q.shape[0]
    nb = s_len // t
    pairs = [(j, i) for j in range(nb) for i in range(j, nb)]
    jtab = jnp.asarray(np.array([p[0] for p in pairs], np.int32))
    itab = jnp.asarray(np.array([p[1] for p in pairs], np.int32))
    n_steps = len(pairs)

    def body(jt_ref, it_ref, q_ref, k_ref, v_ref, do_ref, lse_ref, dl_ref, *rest):
        xin, (dq_ref, dk_ref, dv_ref), xout = rest[:nx], rest[nx:nx + 3], rest[nx + 3:2 * nx + 3]
        dq_acc, dk_acc, dv_acc = rest[2 * nx + 3:2 * nx + 6]
        sems = rest[2 * nx + 6:]
        head, step_id = pl.program_id(0), pl.program_id(1)
        j, i = jt_ref[step_id], it_ref[step_id]

        @pl.when((head == 0) & (step_id == 0))
        def _():
            _exchange_start(xin, xout, *sems)

        @pl.when(step_id == 0)
        def _():
            dq_acc[...] = jnp.zeros(dq_acc.shape, F32)

        @pl.when(i == j)
        def _():
            dk_acc[...] = jnp.zeros(dk_acc.shape, F32)
            dv_acc[...] = jnp.zeros(dv_acc.shape, F32)

        def step(diag):
            qv, kv, vv, dov = q_ref[...], k_ref[...], v_ref[...], do_ref[...]
            s_t = lax.dot_general(kv, qv, NT_DIMS, preferred_element_type=F32)
            if diag:
                krow = lax.broadcasted_iota(jnp.int32, s_t.shape, 0)
                qcol = lax.broadcasted_iota(jnp.int32, s_t.shape, 1)
                s_t = jnp.where(krow <= qcol, s_t, NEG)
            p_t = jnp.exp2(s_t - lse_ref[0])
            dv_acc[...] += jnp.dot(p_t.astype(BF16), dov, preferred_element_type=F32)
            dp_t = lax.dot_general(vv, dov, NT_DIMS, preferred_element_type=F32)
            ds_t = (p_t * (dp_t - dl_ref[0])).astype(BF16)
            dk_acc[...] += jnp.dot(ds_t, qv, preferred_element_type=F32)
            rows = pl.ds(pl.multiple_of(i * t, t), t)
            dq_acc[rows, :] += lax.dot_general(ds_t, kv, TN_DIMS, preferred_element_type=F32)

        @pl.when(i > j)
        def _():
            step(False)

        @pl.when(i == j)
        def _():
            step(True)

        @pl.when(i == nb - 1)
        def _():
            dk_ref[...] = (dk_acc[...] * LN2).astype(BF16)
            dv_ref[...] = dv_acc[...].astype(BF16)

        @pl.when(step_id == n_steps - 1)
        def _():
            dq_ref[...] = (dq_acc[...] * scale).astype(BF16)

        @pl.when((head == n_heads - 1) & (step_id == n_steps - 1))
        def _():
            _exchange_finish(xin, xout, *sems)

    hbm = pl.BlockSpec(memory_space=pl.ANY)
    hv = n_heads * V_DIM
    do_off = 0
    grid_spec = pltpu.PrefetchScalarGridSpec(
        num_scalar_prefetch=2, grid=(n_heads, n_steps),
        in_specs=[pl.BlockSpec((t, HEAD_W), lambda h, s, jt, it: (it[s], h)),
                  pl.BlockSpec((t, HEAD_W), lambda h, s, jt, it: (jt[s], h)),
                  pl.BlockSpec((t, V_DIM), lambda h, s, jt, it: (jt[s], h)),
                  pl.BlockSpec((t, V_DIM), lambda h, s, jt, it: (it[s], do_off + h)),
                  pl.BlockSpec((1, 1, t), lambda h, s, jt, it: (h, 0, it[s])),
                  pl.BlockSpec((1, 1, t), lambda h, s, jt, it: (h, 0, it[s]))] + [hbm] * nx,
        out_specs=[pl.BlockSpec((s_len, HEAD_W), lambda h, s, jt, it: (0, h)),
                   pl.BlockSpec((t, HEAD_W), lambda h, s, jt, it: (jt[s], h)),
                   pl.BlockSpec((t, V_DIM), lambda h, s, jt, it: (jt[s], h))] + [hbm] * nx,
        scratch_shapes=[pltpu.VMEM((s_len, HEAD_W), F32), pltpu.VMEM((t, HEAD_W), F32), pltpu.VMEM((t, V_DIM), F32)]
        + _comm_scratch(nx),
    )
    return pl.pallas_call(
        body, name="flash_bwd", grid_spec=grid_spec,
        out_shape=[jax.ShapeDtypeStruct((s_len, n_heads * HEAD_W), BF16),
                   jax.ShapeDtypeStruct((s_len, n_heads * HEAD_W), BF16),
                   jax.ShapeDtypeStruct((s_len, hv), BF16)] + [jax.ShapeDtypeStruct(c.shape, c.dtype) for c in exchange],
        compiler_params=_cparams("arbitrary", "arbitrary"),
    )(jtab, itab, q, k, v, d_mixcat, lse_row, delta_row, *exchange)


def _qkv_bwd(dq, dk, dv, proj, lay, wuq, wk, wv, g_q, g_kv, ctab, atab, btab, n_heads, ts):
    s_len = proj.shape[0]
    ql_w, kl_w = wuq.shape[0], wk.shape[0]
    tail_w = lay["np"] - lay["ql"]
    kv_o, kr_o = lay["kv"] - lay["ql"], lay["kr"] - lay["ql"]

    def body(dq_ref, dk_ref, dv_ref, ql_ref, kl_ref, wuq_ref, wk_ref, wv_ref, gq_ref, gkv_ref, c_ref, a_ref, b_ref,
             dqr_ref, dkn_ref, tail_ref, dgq_ref, dgkv_ref):
        i = pl.program_id(0)
        c, a, b = c_ref[...], a_ref[...], b_ref[...]
        dkr = jnp.zeros((ts, LANE), F32)
        for h in range(n_heads):
            o = h * HEAD_W
            dqr_ref[:, o:o + QK_NOPE] = dq_ref[:, o:o + QK_NOPE]
            dqr_ref[:, o + QK_NOPE:o + HEAD_W] = _rope_t(dq_ref[:, o + QK_NOPE:o + HEAD_W].astype(F32), c, a, b).astype(BF16)
            dkn_ref[:, h * QK_NOPE:(h + 1) * QK_NOPE] = dk_ref[:, o:o + QK_NOPE]
            dkr = dkr + dk_ref[:, o + QK_NOPE:o + HEAD_W].astype(F32)
        tail_ref[...] = jnp.zeros(tail_ref.shape, BF16)
        tail_ref[:, kr_o:kr_o + LANE] = _rope_t(dkr, c, a, b).astype(BF16)

        def rms_bwd(lat_ref, dn, g_ref):
            lat = lat_ref[...].astype(F32)
            r = _rsq(lat)
            xh = lat * r
            dxh = dn * g_ref[...]
            return r * (dxh - xh * jnp.mean(dxh * xh, axis=-1, keepdims=True)), _colsum(dn * xh)

        dqn = lax.dot_general(dqr_ref[...], wuq_ref[...], NT_DIMS, preferred_element_type=F32)
        d_ql, dgq = rms_bwd(ql_ref, dqn, gq_ref)
        tail_ref[:, 0:ql_w] = d_ql.astype(BF16)
        dkvn = (lax.dot_general(dkn_ref[...], wk_ref[...], NT_DIMS, preferred_element_type=F32)
                + lax.dot_general(dv_ref[...], wv_ref[...], NT_DIMS, preferred_element_type=F32))
        d_kl, dgkv = rms_bwd(kl_ref, dkvn, gkv_ref)
        tail_ref[:, kv_o:kv_o + kl_w] = d_kl.astype(BF16)

        @pl.when(i == 0)
        def _():
            dgq_ref[...] = jnp.zeros(dgq_ref.shape, F32)
            dgkv_ref[...] = jnp.zeros(dgkv_ref.shape, F32)

        dgq_ref[...] += dgq
        dgkv_ref[...] += dgkv

    def full(arr):
        return pl.BlockSpec(arr.shape, lambda i: (0, 0))

    def rows(w):
        return pl.BlockSpec((ts, w), lambda i: (i, 0))

    tab = pl.BlockSpec((ts, LANE), lambda i: (i, 0))
    hw, hv, hn = n_heads * HEAD_W, n_heads * V_DIM, n_heads * QK_NOPE
    return pl.pallas_call(
        body, name="qkv_bwd", grid=(s_len // ts,),
        in_specs=[rows(hw), rows(hw), rows(hv),
                  pl.BlockSpec((ts, ql_w), lambda i: (i, lay["ql"] // ql_w)),
                  pl.BlockSpec((ts, kl_w), lambda i: (i, lay["kv"] // kl_w)),
                  full(wuq), full(wk), full(wv), full(g_q), full(g_kv), tab, tab, tab],
        out_specs=[rows(hw), rows(hn), rows(tail_w), full(g_q), full(g_kv)],
        out_shape=[jax.ShapeDtypeStruct((s_len, hw), BF16), jax.ShapeDtypeStruct((s_len, hn), BF16),
                   jax.ShapeDtypeStruct((s_len, tail_w), BF16),
                   jax.ShapeDtypeStruct(g_q.shape, F32), jax.ShapeDtypeStruct(g_kv.shape, F32)],
        compiler_params=_cparams("arbitrary"),
    )(dq, dk, dv, proj, proj, wuq, wk, wv, g_q, g_kv, ctab, atab, btab)


def _adamw(w, g, m, v):
    m = ADAM_B1 * m + (1.0 - ADAM_B1) * g
    v = ADAM_B2 * v + (1.0 - ADAM_B2) * (g * g)
    m_hat = m / (1.0 - ADAM_B1 ** ADAM_STEP)
    v_hat = v / (1.0 - ADAM_B2 ** ADAM_STEP)
    delta = -ADAM_LR * (m_hat / (jnp.sqrt(v_hat) + ADAM_EPS) + ADAM_WD * w)
    return delta, m, v


def _adam_parts(parts, w, m, v, tr, name):
    r, c = w.shape
    tr = _tile(r, tr, SUB)

    def body(p_ref, w_ref, m_ref, v_ref, g_out, d_out, m_out, v_out):
        g = p_ref[0].astype(F32)
        for dev in range(1, N_DEV):
            g = g + p_ref[dev].astype(F32)
        g_out[...] = g
        d_out[...], m_out[...], v_out[...] = _adamw(w_ref[...], g, m_ref[...], v_ref[...])

    blk = pl.BlockSpec((tr, c), lambda i: (i, 0))
    shp = jax.ShapeDtypeStruct((r, c), F32)
    return pl.pallas_call(
        body, name=name, grid=(r // tr,),
        in_specs=[pl.BlockSpec((N_DEV, tr, c), lambda i: (0, i, 0)), blk, blk, blk],
        out_specs=[blk, blk, blk, blk], out_shape=[shp, shp, shp, shp],
        compiler_params=_cparams("parallel"),
    )(parts, w, m, v)


def _adam_ada(cact_t, dmod_sh, w, m, v, tr):
    r, c = w.shape

    def body(ct_ref, dm_ref, w_ref, m_ref, v_ref, g_out, d_out, m_out, v_out):
        g = jnp.dot(ct_ref[...], dm_ref[...], preferred_element_type=F32, precision=lax.Precision.HIGHEST)
        g_out[...] = g
        d_out[...], m_out[...], v_out[...] = _adamw(w_ref[...], g, m_ref[...], v_ref[...])

    blk = pl.BlockSpec((tr, c), lambda i: (i, 0))
    shp = jax.ShapeDtypeStruct((r, c), F32)
    return pl.pallas_call(
        body, name="adam_ada", grid=(r // tr,),
        in_specs=[pl.BlockSpec((tr, N_DEV), lambda i: (i, 0)), pl.BlockSpec((N_DEV, c), lambda i: (0, 0)), blk, blk, blk],
        out_specs=[blk, blk, blk, blk], out_shape=[shp, shp, shp, shp],
        compiler_params=_cparams("parallel"),
    )(cact_t, dmod_sh, w, m, v)


def _adam_small(v_all, w, m, v):
    n = w.shape[1]

    def body(p_ref, w_ref, m_ref, v_ref, g_out, d_out, m_out, v_out):
        g = p_ref[0:1, :]
        for dev in range(1, N_DEV):
            g = g + p_ref[dev:dev + 1, :]
        g_out[...] = g
        d_out[...], m_out[...], v_out[...] = _adamw(w_ref[...], g, m_ref[...], v_ref[...])

    shp = jax.ShapeDtypeStruct((1, n), F32)
    vm = pl.BlockSpec(memory_space=pltpu.VMEM)
    return pl.pallas_call(
        body, name="adam_small", in_specs=[vm, vm, vm, vm], out_specs=[vm, vm, vm, vm],
        out_shape=[shp, shp, shp, shp], compiler_params=_cparams(),
    )(v_all, w, m, v)


def _my_place():
    return lax.axis_index("x"), lax.axis_index("y"), lax.axis_index("c")


def _peer(place, k):
    x, y, c = place
    return (x ^ (k >> 2), y ^ ((k >> 1) & 1), c ^ (k & 1))


def _index(place):
    return 4 * place[0] + 2 * place[1] + place[2]


def _ada_fwd(vec, w_ada, b_ada_rows):
    lv = vec.shape[1]
    d, c = w_ada.shape

    def body(vec_ref, w_ref, b_ref, gath_ref, cact_ref, mod_ref, modsh, send_a, recv_a, send_b, recv_b, local_s):
        me = _my_place()
        my_i = _index(me)

        def gather_copy(k, to, src_row):
            row = gath_ref.at[pl.ds(src_row, 1), :]
            return pltpu.make_async_remote_copy(src_ref=row, dst_ref=row, send_sem=send_a.at[k], recv_sem=recv_a.at[k],
                                                device_id=to, device_id_type=MESH)

        own = pltpu.make_async_copy(vec_ref, gath_ref.at[pl.ds(my_i, 1), :], local_s.at[0])
        own.start()
        own.wait()
        sends = [gather_copy(k, _peer(me, k), my_i) for k in range(1, N_DEV)]
        for cp in sends:
            cp.start()
        for k in range(1, N_DEV):
            gather_copy(k, me, _index(_peer(me, k))).wait_recv()
        for cp in sends:
            cp.wait_send()

        c_all = gath_ref[:, 0:d]
        cact = c_all * _sigmoid(c_all)
        cact_ref[...] = cact
        modsh[...] = jnp.dot(cact, w_ref[...], preferred_element_type=F32, precision=lax.Precision.HIGHEST)

        def mod_copy(k, to, src_row, dst_row):
            return pltpu.make_async_remote_copy(src_ref=modsh.at[pl.ds(src_row, 1), :], dst_ref=mod_ref.at[pl.ds(dst_row, 1), :],
                                                send_sem=send_b.at[k], recv_sem=recv_b.at[k],
                                                device_id=to, device_id_type=MESH)

        own = pltpu.make_async_copy(modsh.at[pl.ds(my_i, 1), :], mod_ref.at[pl.ds(my_i, 1), :], local_s.at[1])
        own.start()
        sends = [mod_copy(k, _peer(me, k), _index(_peer(me, k)), my_i) for k in range(1, N_DEV)]
        for cp in sends:
            cp.start()
        for k in range(1, N_DEV):
            mod_copy(k, me, my_i, _index(_peer(me, k))).wait_recv()
        for cp in sends:
            cp.wait_send()
        own.wait()
        mod_ref[...] = mod_ref[...] + b_ref[...]

    vm = pl.BlockSpec(memory_space=pltpu.VMEM)
    return pl.pallas_call(
        body, name="ada_fwd", in_specs=[vm, vm, vm], out_specs=[vm, vm, vm],
        out_shape=[jax.ShapeDtypeStruct((N_DEV, lv), F32), jax.ShapeDtypeStruct((N_DEV, d), F32),
                   jax.ShapeDtypeStruct((N_DEV, c), F32)],
        scratch_shapes=[pltpu.VMEM((N_DEV, c), F32)] + [pltpu.SemaphoreType.DMA((N_DEV,))] * 4
        + [pltpu.SemaphoreType.DMA((2,))],
        compiler_params=pltpu.CompilerParams(vmem_limit_bytes=VMEM_LIMIT),
    )(vec, w_ada, b_ada_rows)


def _gather_small(vec):
    lv = vec.shape[1]

    def body(vec_ref, gath_ref, send_s, recv_s, local_s):
        me = _my_place()
        my_i = _index(me)

        def copy(k, to, src_row):
            row = gath_ref.at[pl.ds(src_row, 1), :]
            return pltpu.make_async_remote_copy(src_ref=row, dst_ref=row, send_sem=send_s.at[k], recv_sem=recv_s.at[k],
                                                device_id=to, device_id_type=MESH)

        own = pltpu.make_async_copy(vec_ref, gath_ref.at[pl.ds(my_i, 1), :], local_s)
        own.start()
        own.wait()
        sends = [copy(k, _peer(me, k), my_i) for k in range(1, N_DEV)]
        for cp in sends:
            cp.start()
        for k in range(1, N_DEV):
            copy(k, me, _index(_peer(me, k))).wait_recv()
        for cp in sends:
            cp.wait_send()

    vm = pl.BlockSpec(memory_space=pltpu.VMEM)
    return pl.pallas_call(
        body, name="gather_small", in_specs=[vm], out_specs=vm,
        out_shape=jax.ShapeDtypeStruct((N_DEV, lv), F32),
        scratch_shapes=[pltpu.SemaphoreType.DMA((N_DEV,))] * 2 + [pltpu.SemaphoreType.DMA],
        compiler_params=pltpu.CompilerParams(vmem_limit_bytes=VMEM_LIMIT),
    )(vec)


PER = N_DEV - 1


def _comm_scratch(n):
    return [pltpu.SemaphoreType.DMA((n * PER,)), pltpu.SemaphoreType.DMA((n * PER,)), pltpu.SemaphoreType.DMA((n,))]


def _gather_copies(ins, outs, send_s, recv_s, local_s):
    n = len(ins)
    me = _my_place()
    x, y, c = me
    sibling = (x, y, 1 - c)
    chips = [(1 - x, y), (x, 1 - y), (1 - x, 1 - y)]

    def copy(a, k, block, to, src=None):
        slot = outs[a].at[_index(block)]
        return pltpu.make_async_remote_copy(src_ref=slot if src is None else src, dst_ref=slot,
                                            send_sem=send_s.at[a * PER + k], recv_sem=recv_s.at[a * PER + k],
                                            device_id=to, device_id_type=MESH)

    mine = [pltpu.make_async_copy(ins[a], outs[a].at[_index(me)], local_s.at[a]) for a in range(n)]
    first = []
    for a in range(n):
        first.append(copy(a, 0, me, sibling, src=ins[a]))
        first += [copy(a, 1 + j, me, (*chip, c), src=ins[a]) for j, chip in enumerate(chips)]
    landed = [copy(a, 1 + j, (*chip, c), me) for j, chip in enumerate(chips) for a in range(n)]
    passed = [copy(a, 4 + j, (*chip, c), sibling) for j, chip in enumerate(chips) for a in range(n)]
    from_sibling = [copy(a, 0, sibling, me) for a in range(n)]
    from_sibling += [copy(a, 4 + j, (*chip, 1 - c), me) for a in range(n) for j, chip in enumerate(chips)]
    return mine, first, landed, passed, from_sibling


def _gather_start(*refs):
    mine, first, _, _, _ = _gather_copies(*refs)
    for cp in mine + first:
        cp.start()


def _gather_forward(*refs):
    _, _, landed, passed, _ = _gather_copies(*refs)
    for got, fwd in zip(landed, passed):
        got.wait_recv()
        fwd.start()


def _gather_finish(*refs):
    mine, first, _, passed, from_sibling = _gather_copies(*refs)
    for cp in from_sibling:
        cp.wait_recv()
    for cp in first + passed:
        cp.wait_send()
    for cp in mine:
        cp.wait()


def _exchange_copies(ins, outs, send_s, recv_s, local_s):
    n = len(ins)
    me = _my_place()
    my_i = _index(me)

    def copy(a, k, to, src_slot, dst_slot):
        return pltpu.make_async_remote_copy(src_ref=ins[a].at[src_slot], dst_ref=outs[a].at[dst_slot],
                                            send_sem=send_s.at[a * PER + k - 1], recv_sem=recv_s.at[a * PER + k - 1],
                                            device_id=to, device_id_type=MESH)

    mine = [pltpu.make_async_copy(ins[a].at[my_i], outs[a].at[my_i], local_s.at[a]) for a in range(n)]
    sends = [copy(a, k, _peer(me, k), _index(_peer(me, k)), my_i) for k in range(1, N_DEV) for a in range(n)]
    recvs = [copy(a, k, me, my_i, _index(_peer(me, k))) for k in range(1, N_DEV) for a in range(n)]
    return mine, sends, recvs


def _exchange_start(*refs):
    mine, sends, _ = _exchange_copies(*refs)
    for cp in mine + sends:
        cp.start()


def _exchange_finish(*refs):
    mine, sends, recvs = _exchange_copies(*refs)
    for cp in recvs:
        cp.wait_recv()
    for cp in sends:
        cp.wait_send()
    for cp in mine:
        cp.wait()


def _gathered_shapes(shards):
    return [jax.ShapeDtypeStruct((N_DEV,) + s.shape, s.dtype) for s in shards]


def _gather_weights(shards):
    n = len(shards)

    def body(*refs):
        parts = (refs[:n], refs[n:2 * n]) + tuple(refs[2 * n:])
        _gather_start(*parts)
        _gather_forward(*parts)
        _gather_finish(*parts)

    hbm = pl.BlockSpec(memory_space=pl.ANY)
    return pl.pallas_call(
        body, name="gather_weights", in_specs=[hbm] * n, out_specs=[hbm] * n,
        out_shape=_gathered_shapes(shards), scratch_shapes=_comm_scratch(n),
    )(*shards)


def _proj_layout(cw, ql, kl):
    lay = {"gb": 0, "gc": cw, "ci": 2 * cw, "ql": 3 * cw}
    assert lay["ql"] % ql == 0
    lay["kv"] = _roundup(lay["ql"] + ql, kl)
    lay["kr"] = lay["kv"] + kl
    lay["np"] = _roundup(lay["kr"] + LANE, 4 * LANE)
    return lay


def _chunks_cols(g):
    r, c8 = g.shape
    return jnp.transpose(g.reshape(r, N_DEV, c8 // N_DEV), (1, 0, 2))


def _from_col_shards(a):
    n, r, c = a.shape
    return jnp.transpose(a, (1, 0, 2)).reshape(r, n * c)


def kernel(x, c, positions, w_ada, b_ada, g_pre_mix, g_post_mix, w_in, g_q, w_uq, g_kv, w_ukv, conv_w_mix, conv_b_mix, w_o, g_pre_ffn, g_post_ffn, w_up, conv_w_ffn, conv_b_ffn, w_down, loss_target, m_w_ada, m_b_ada, m_g_pre_mix, m_g_post_mix, m_w_in, m_g_q, m_w_uq, m_g_kv, m_w_ukv, m_conv_w_mix, m_conv_b_mix, m_w_o, m_g_pre_ffn, m_g_post_ffn, m_w_up, m_conv_w_ffn, m_conv_b_ffn, m_w_down, v_w_ada, v_b_ada, v_g_pre_mix, v_g_post_mix, v_w_in, v_g_q, v_w_uq, v_g_kv, v_w_ukv, v_conv_w_mix, v_conv_b_mix, v_w_o, v_g_pre_ffn, v_g_post_ffn, v_w_up, v_conv_w_ffn, v_conv_b_ffn, v_w_down):
    s_len, d = x.shape[1], x.shape[2]
    ql, kl = w_uq.shape[1], w_ukv.shape[1]
    n_heads = w_ukv.shape[2] * N_DEV // (QK_NOPE + V_DIM)
    cw = conv_w_mix.shape[2] * N_DEV
    f2 = w_up.shape[2] * N_DEV
    ff = f2 // 2
    in_cols = w_in.shape[2] * N_DEV
    ada_c = w_ada.shape[2]
    cwm_c, cwf_c = conv_w_mix.shape[2], conv_w_ffn.shape[2]
    scale = 1.0 / math.sqrt(QK_NOPE + QK_ROPE)
    lay = _proj_layout(cw, ql, kl)
    n_pad = lay["np"]
    my_i = _index(_my_place())

    ts_row = _tile(s_len, PREF["row"], SUB)
    ts_conv = _tile(s_len, PREF["conv_rows"], SUB)
    tc_conv = _tile(cw, PREF["conv_cols"])
    tc_ffn = cwf_c
    ts_ffn = _tile(s_len, PREF["ffn_rows"], SUB)
    pair_order = [k // 2 + (k % 2) * (N_DEV // 2) for k in range(N_DEV)]
    pair_place = [pair_order.index(k) for k in range(N_DEV)]

    def paired(shards):
        return _from_col_shards(jnp.stack([shards[p] for p in pair_order]))

    def unpaired_chunks(g):
        ch = _chunks_cols(g)
        return jnp.stack([ch[p] for p in pair_place])
    ts_qkv = _tile(s_len, PREF["row"], SUB)
    t_attn = _tile(s_len, PREF["attn"])

    x2d, tgt = x[0], loss_target[0]

    vec = jnp.concatenate([c, conv_w_mix[0].reshape(1, -1), conv_w_ffn[0].reshape(1, -1)], axis=1)
    gath, cact, mod_rows = _ada_fwd(vec, w_ada[0], b_ada.reshape(N_DEV, ada_c))
    cwm_full = _from_col_shards(gath[:, d:d + 3 * cwm_c].reshape(N_DEV, 3, cwm_c))
    cwf_shards = gath[:, d + 3 * cwm_c:].reshape(N_DEV, 3, cwf_c)
    cwf_pair = paired(cwf_shards)
    cbf_pair = paired(jnp.transpose(conv_b_ffn.reshape(1, N_DEV, cwf_c), (1, 0, 2)))
    mod = mod_rows.reshape(1, N_DEV * ada_c)
    sh_m, sc_m, gt_m, sh_f, sc_f, gt_f = [mod[:, k * d:(k + 1) * d] for k in range(6)]

    g_in, g_uq, g_ukv = _gather_weights([w_in[0].astype(BF16), w_uq[0].astype(BF16), w_ukv[0].astype(BF16)])
    win = _from_col_shards(g_in)
    cut = np.cumsum([0, ql, kl, QK_ROPE, cw, cw, cw])
    part = [win[:, cut[k]:cut[k + 1]] for k in range(6)]

    def zcols(n):
        return jnp.zeros((d, n), BF16)

    win_p = jnp.concatenate([part[3], part[4], part[5], part[0], zcols(lay["kv"] - lay["ql"] - ql), part[1],
                             part[2], zcols(n_pad - lay["kr"] - QK_ROPE)], axis=1)
    wuq_p = jnp.pad(_from_col_shards(g_uq).reshape(ql, n_heads, QK_NOPE + QK_ROPE),
                    ((0, 0), (0, 0), (0, HEAD_W - QK_NOPE - QK_ROPE))).reshape(ql, n_heads * HEAD_W)
    wukv = _from_col_shards(g_ukv).reshape(kl, n_heads, QK_NOPE + V_DIM)
    wk = wukv[:, :, :QK_NOPE].reshape(kl, n_heads * QK_NOPE)
    wv = wukv[:, :, QK_NOPE:].reshape(kl, n_heads * V_DIM)

    inv_freq = 1.0 / (ROPE_THETA ** (jnp.arange(0, QK_ROPE, 2, dtype=F32) / QK_ROPE))
    inv_row = jnp.tile(inv_freq, LANE // (QK_ROPE // 2)).reshape(1, LANE)
    ctab, atab, btab = _rope_tables(positions.astype(F32).reshape(s_len, 1), inv_row, _tile(s_len, 1024, SUB))

    h1 = _modnorm_fwd(x2d, g_pre_mix, sc_m, sh_m, ts_row)
    proj = _matmul(h1, win_p, out_dtype=BF16, tm=1024, tn=1280, tk=2048, name="mm_proj")
    q, k, v, qn, kvn = _qkv_fwd(proj, lay, wuq_p, wk, wv, g_q, g_kv, ctab, atab, btab, n_heads, ts_qkv, scale * LOG2E)
    attn, lse_row, g_o, g_up, g_down = _flash_fwd(q, k, v, n_heads, t_attn,
                                              [w_o[0].astype(BF16), w_up[0].astype(BF16), w_down[0].astype(BF16)])
    wo, wdown = g_o.reshape(d, d), g_down.reshape(ff, d)

    def pair_shard(j):
        return j // 2 + (j % 2) * (N_DEV // 2)
    conv_out = _gconv_fwd(proj, lay, cwm_full, conv_b_mix, ts_conv, tc_conv)
    mixcat = jnp.concatenate([attn, conv_out], axis=1)
    mix = _matmul(mixcat, wo, out_dtype=F32, tm=512, tn=2048, tk=2048, name="mm_mix")
    x1, h2 = _post_mix_fwd(mix, x2d, gt_m, g_post_mix, g_pre_ffn, sc_f, sh_f, ts_row)
    up = _matmul(h2, g_up, out_dtype=BF16, tm=1024, tn=cwf_c, tk=2048, name="mm_up", b_shard_of=pair_shard)
    u, act = _act_fwd(up, cwf_pair, cbf_pair, ts_ffn, tc_ffn)
    y = _matmul(act, wdown, out_dtype=F32, tm=1024, tn=512, tk=ff, name="mm_down")

    dy, dx2, loss_row, d_gt_f, dg_post_ffn = _final_bwd(y, x1, tgt, gt_f, g_post_ffn, ts_row)
    gw_down = _matmul(act, dy, ta=True, out_dtype=BF16, tm=1408, tn=1024, tk=2048, name="mm_gw_down")
    d_act = _matmul(dy, wdown, tb=True, out_dtype=BF16, tm=1024, tn=1408, tk=2048, name="mm_d_act")
    d_up, dcw_pair, dcb_pair = _ffn_act_bwd(d_act, u, up, cwf_pair, ts_ffn, tc_ffn)
    dcb_ffn = _from_col_shards(unpaired_chunks(dcb_pair))
    gw_up, p_down = _matmul(h2, d_up, ta=True, out_dtype=BF16, tm=1024, tn=cwf_c, tk=2048, name="mm_gw_up",
                            exchange=[gw_down.reshape(N_DEV, ff // N_DEV, d)], out_shard_of=pair_shard)
    dh2 = _matmul_pair_shards(d_up, g_up, out_dtype=F32, tm=512, tn=1024, name="mm_dh2")
    dx1, dmix, d_sh_f, d_sc_f, dg_pre_ffn, d_gt_m, dg_post_mix = _mid_bwd(
        dh2, x1, dx2, mix, g_pre_ffn, sc_f, gt_m, g_post_mix, ts_row)
    gw_o = _matmul(mixcat, dmix, ta=True, out_dtype=BF16, tm=1024, tn=1024, tk=2048, name="mm_gw_o")
    d_mixcat = _matmul(dmix, wo, tb=True, out_dtype=BF16, tm=1024, tn=1024, tk=2048, name="mm_d_mixcat")
    d_gb, d_gc, d_ci, dcw_mix, dcb_mix = _gconv_bwd(d_mixcat, proj, lay, cwm_full, conv_b_mix, ts_conv, tc_conv)
    delta = _delta(attn, d_mixcat, n_heads, _tile(s_len, 512, SUB))
    delta_row = delta[:, :, 0].reshape(n_heads, 1, s_len)
    dq, dk, dv, p_up, p_o = _flash_bwd(q, k, v, d_mixcat, lse_row, delta_row, n_heads, t_attn, scale,
                                       [gw_up, gw_o.reshape(N_DEV, d // N_DEV, d)])
    dq_r, dkn, d_tail, dg_q, dg_kv = _qkv_bwd(dq, dk, dv, proj, lay, wuq_p, wk, wv, g_q, g_kv, ctab, atab, btab,
                                              n_heads, ts_qkv)
    gw_uq_p = _matmul(qn, dq_r, ta=True, out_dtype=BF16, tm=768, tn=1024, tk=2048, name="mm_gw_uq")
    gw_k = _matmul(kvn, dkn, ta=True, out_dtype=BF16, tm=512, tn=1024, tk=2048, name="mm_gw_k")
    gw_v = _matmul(kvn, dv, ta=True, out_dtype=BF16, tm=512, tn=1024, tk=2048, name="mm_gw_v")
    d_proj = jnp.concatenate([d_gb, d_gc, d_ci, d_tail], axis=1)
    gw_in_p = _matmul(h1, d_proj, ta=True, out_dtype=BF16, tm=1024, tn=1280, tk=2048, name="mm_gw_in")

    gw_in = jnp.concatenate([gw_in_p[:, lay["ql"]:lay["ql"] + ql], gw_in_p[:, lay["kv"]:lay["kv"] + kl],
                             gw_in_p[:, lay["kr"]:lay["kr"] + QK_ROPE], gw_in_p[:, :3 * cw]], axis=1)
    gw_uq = gw_uq_p.reshape(ql, n_heads, HEAD_W)[:, :, :QK_NOPE + QK_ROPE].reshape(ql, n_heads * (QK_NOPE + QK_ROPE))
    gw_ukv = jnp.concatenate([gw_k.reshape(kl, n_heads, QK_NOPE), gw_v.reshape(kl, n_heads, V_DIM)],
                             axis=2).reshape(kl, n_heads * (QK_NOPE + V_DIM))
    dh1, p_in, p_uq, p_ukv, p_cwm, p_cwf = _matmul(
        d_proj, win_p, tb=True, out_dtype=F32, tm=512, tn=1024, tk=n_pad, name="mm_dh1",
        exchange=[_chunks_cols(gw_in), _chunks_cols(gw_uq), _chunks_cols(gw_ukv), _chunks_cols(dcw_mix),
                  unpaired_chunks(dcw_pair)])
    grad_x, d_sh_m, d_sc_m, dg_pre_mix = _first_bwd(dh1, x2d, dx1, g_pre_mix, sc_m, ts_row)

    dmod = jnp.concatenate([d_sh_m, d_sc_m, d_gt_m, d_sh_f, d_sc_f, d_gt_f], axis=1)
    small_g = [loss_row, dmod, dg_pre_mix, dg_post_mix, dg_q, dg_kv, dcb_mix, dg_pre_ffn, dg_post_ffn, dcb_ffn]
    v_all = _gather_small(jnp.concatenate(small_g, axis=1))

    def pack(b_ada_, g_pre_mix_, g_post_mix_, g_q_, g_kv_, cb_mix_, g_pre_ffn_, g_post_ffn_, cb_ffn_):
        return jnp.concatenate([jnp.zeros((1, LANE), F32), b_ada_, g_pre_mix_, g_post_mix_, g_q_, g_kv_, cb_mix_,
                                g_pre_ffn_, g_post_ffn_, cb_ffn_], axis=1)

    w_small = pack(b_ada, g_pre_mix, g_post_mix, g_q, g_kv, conv_b_mix, g_pre_ffn, g_post_ffn, conv_b_ffn)
    m_small = pack(m_b_ada, m_g_pre_mix, m_g_post_mix, m_g_q, m_g_kv, m_conv_b_mix, m_g_pre_ffn, m_g_post_ffn,
                   m_conv_b_ffn)
    v_small = pack(v_b_ada, v_g_pre_mix, v_g_post_mix, v_g_q, v_g_kv, v_conv_b_mix, v_g_pre_ffn, v_g_post_ffn,
                   v_conv_b_ffn)
    small_out = _adam_small(v_all, w_small, m_small, v_small)

    sizes = [LANE, 6 * d, d, d, ql, kl, cw, d, d, f2]
    offs = np.cumsum([0] + sizes)

    def unpack(row):
        parts = [row[:, offs[k]:offs[k + 1]] for k in range(len(sizes))]

        named = dict(loss=parts[0][0, 0], b_ada=parts[1], g_pre_mix=parts[2], g_post_mix=parts[3], g_q=parts[4],
                     g_kv=parts[5], conv_b_mix=parts[6], g_pre_ffn=parts[7], g_post_ffn=parts[8], conv_b_ffn=parts[9])
        return named

    small = [unpack(r) for r in small_out]

    dmod_all = v_all[:, offs[1]:offs[2]]
    dmod_sh = lax.dynamic_slice(dmod_all, (0, my_i * ada_c), (N_DEV, ada_c))
    cact_t = jnp.transpose(cact)
    big = dict(
        w_ada=_adam_ada(cact_t, dmod_sh, w_ada[0], m_w_ada[0], v_w_ada[0], _tile(d, 256, SUB)),
        w_in=_adam_parts(p_in, w_in[0], m_w_in[0], v_w_in[0], 256, "adam_w_in"),
        w_uq=_adam_parts(p_uq, w_uq[0], m_w_uq[0], v_w_uq[0], 256, "adam_w_uq"),
        w_ukv=_adam_parts(p_ukv, w_ukv[0], m_w_ukv[0], v_w_ukv[0], 256, "adam_w_ukv"),
        w_o=_adam_parts(p_o, w_o[0], m_w_o[0], v_w_o[0], 128, "adam_w_o"),
        w_up=_adam_parts(p_up, w_up[0], m_w_up[0], v_w_up[0], 256, "adam_w_up"),
        w_down=_adam_parts(p_down, w_down[0], m_w_down[0], v_w_down[0], 176, "adam_w_down"),
        conv_w_mix=_adam_parts(p_cwm, conv_w_mix[0], m_conv_w_mix[0], v_conv_w_mix[0], 8, "adam_cw_mix"),
        conv_w_ffn=_adam_parts(p_cwf, conv_w_ffn[0], m_conv_w_ffn[0], v_conv_w_ffn[0], 8, "adam_cw_ffn"),
    )

    names = ["w_ada", "b_ada", "g_pre_mix", "g_post_mix", "w_in", "g_q", "w_uq", "g_kv", "w_ukv", "conv_w_mix",
             "conv_b_mix", "w_o", "g_pre_ffn", "g_post_ffn", "w_up", "conv_w_ffn", "conv_b_ffn", "w_down"]
    outs = [small[0]["loss"], grad_x[None]]
    for kind in range(4):
        for nm in names:
            outs.append(big[nm][kind][None] if nm in big else small[kind][nm])
    return tuple(outs)
```

```python
import math

import numpy as np
import jax
import jax.numpy as jnp
from jax import lax
from jax.experimental import pallas as pl
from jax.experimental.pallas import tpu as pltpu

F32 = jnp.float32
BF16 = jnp.bfloat16
N_DEV = 8
MESH = pl.DeviceIdType.MESH

QK_NOPE = 128
QK_ROPE = 64
V_DIM = 128
HEAD_W = 256
LANE = 128
SUB = 8
RMS_EPS = 1e-6
ROPE_THETA = 10000.0
ADAM_LR = 0.001
ADAM_B1 = 0.9
ADAM_B2 = 0.999
ADAM_EPS = 1e-08
ADAM_WD = 0.01
ADAM_STEP = 10
NEG = -1e30
LOG2E = 1.4426950408889634
LN2 = 0.6931471805599453
VMEM_LIMIT = 56 * 1024 * 1024

PREF = {"row": 256, "conv_rows": 512, "conv_cols": 512, "ffn_rows": 256, "attn": 1024, "attn_chunk": 1024}

NT_DIMS = (((1,), (1,)), ((), ()))
TN_DIMS = (((0,), (0,)), ((), ()))


def _cparams(*sem):
    return pltpu.CompilerParams(dimension_semantics=sem if sem else None, vmem_limit_bytes=VMEM_LIMIT)


def _tile(n, pref, unit=LANE):
    if n <= pref:
        return n
    t = (pref // unit) * unit
    while t >= unit:
        if n % t == 0:
            return t
        t -= unit
    return n


def _roundup(n, m):
    return (n + m - 1) // m * m


def _rsq(x):
    return lax.rsqrt(jnp.mean(x * x, axis=-1, keepdims=True) + RMS_EPS)


def _colsum(x):
    return jnp.sum(x, axis=0, keepdims=True)


def _sigmoid(x):
    return 1.0 / (1.0 + jnp.exp(-x))


def _matmul(a, b, *, ta=False, tb=False, out_dtype, tm, tn, tk, name, exchange=None, b_shard_of=None,
            out_shard_of=None):
    m_dim, k_dim = (a.shape[1], a.shape[0]) if ta else a.shape
    if b_shard_of is not None:
        n_dim, tn = b.shape[0] * b.shape[2], b.shape[2]
    else:
        n_dim = b.shape[0] if tb else b.shape[1]
    if out_shard_of is not None:
        tn = n_dim // N_DEV
    tm, tn, tk = _tile(m_dim, tm), _tile(n_dim, tn), _tile(k_dim, tk)
    gi, gj, nk = m_dim // tm, n_dim // tn, k_dim // tk
    dims = (((0 if ta else 1,), (1 if tb else 0,)), ((), ()))
    chunks = list(exchange or [])
    nx = len(chunks)

    def body(*refs):
        a_ref, b_ref = refs[:2]
        xin, o_ref, xout = refs[2:2 + nx], refs[2 + nx], refs[3 + nx:3 + 2 * nx]
        scratch = refs[3 + 2 * nx:]
        sems = scratch[1:] if nk > 1 else scratch
        i, j, k = pl.program_id(0), pl.program_id(1), pl.program_id(2)
        if nx:
            @pl.when((i == 0) & (j == 0) & (k == 0))
            def _():
                _exchange_start(xin, xout, *sems)

        part = lax.dot_general(a_ref[...], b_ref[...], dims, preferred_element_type=F32)
        if nk == 1:
            o_ref[...] = part.astype(o_ref.dtype)
        else:
            acc_ref = scratch[0]

            @pl.when(k == 0)
            def _():
                acc_ref[...] = part

            @pl.when(k > 0)
            def _():
                acc_ref[...] += part

            @pl.when(k == nk - 1)
            def _():
                o_ref[...] = acc_ref[...].astype(o_ref.dtype)

        if nx:
            @pl.when((i == gi - 1) & (j == gj - 1) & (k == nk - 1))
            def _():
                _exchange_finish(xin, xout, *sems)

    a_spec = pl.BlockSpec((tk, tm), lambda i, j, k: (k, i)) if ta else pl.BlockSpec((tm, tk), lambda i, j, k: (i, k))
    b_spec = pl.BlockSpec((tn, tk), lambda i, j, k: (j, k)) if tb else pl.BlockSpec((tk, tn), lambda i, j, k: (k, j))
    if b_shard_of is not None:
        b_spec = pl.BlockSpec((None, tk, tn), lambda i, j, k: (b_shard_of(j), k, 0))
    o_spec, o_shape = pl.BlockSpec((tm, tn), lambda i, j, k: (i, j)), (m_dim, n_dim)
    if out_shard_of is not None:
        o_spec, o_shape = pl.BlockSpec((None, tm, tn), lambda i, j, k: (out_shard_of(j), i, 0)), (N_DEV, m_dim, tn)
    hbm = pl.BlockSpec(memory_space=pl.ANY)
    out = pl.pallas_call(
        body,
        name=name,
        grid=(gi, gj, nk),
        in_specs=[a_spec, b_spec] + [hbm] * nx,
        out_specs=[o_spec] + [hbm] * nx,
        out_shape=[jax.ShapeDtypeStruct(o_shape, out_dtype)] + [jax.ShapeDtypeStruct(c.shape, c.dtype) for c in chunks],
        scratch_shapes=([pltpu.VMEM((tm, tn), F32)] if nk > 1 else []) + (_comm_scratch(nx) if nx else []),
        compiler_params=_cparams(*(("arbitrary",) * 3 if nx else ("parallel", "parallel", "arbitrary"))),
    )(a, b, *chunks)
    return out if nx else out[0]


def _matmul_pair_shards(a, shards, *, out_dtype, tm, tn, name):
    m_dim = a.shape[0]
    n_sh, n_dim, c = shards.shape
    half = n_sh // 2
    tm, tn = _tile(m_dim, tm), _tile(n_dim, tn)

    def body(a_ref, b0_ref, b1_ref, o_ref, acc_ref):
        k = pl.program_id(2)
        part = (lax.dot_general(a_ref[:, :c], b0_ref[...], NT_DIMS, preferred_element_type=F32)
                + lax.dot_general(a_ref[:, c:], b1_ref[...], NT_DIMS, preferred_element_type=F32))

        @pl.when(k == 0)
        def _():
            acc_ref[...] = part

        @pl.when(k > 0)
        def _():
            acc_ref[...] += part

        @pl.when(k == half - 1)
        def _():
            o_ref[...] = acc_ref[...].astype(o_ref.dtype)

    return pl.pallas_call(
        body, name=name, grid=(m_dim // tm, n_dim // tn, half),
        in_specs=[pl.BlockSpec((tm, 2 * c), lambda i, j, k: (i, k)),
                  pl.BlockSpec((None, tn, c), lambda i, j, k: (k, j, 0)),
                  pl.BlockSpec((None, tn, c), lambda i, j, k: (k + half, j, 0))],
        out_specs=pl.BlockSpec((tm, tn), lambda i, j, k: (i, j)),
        out_shape=jax.ShapeDtypeStruct((m_dim, n_dim), out_dtype),
        scratch_shapes=[pltpu.VMEM((tm, tn), F32)],
        compiler_params=_cparams("parallel", "parallel", "arbitrary"),
    )(a, shards, shards)


def _shift_down(x, halo, n):
    r = pltpu.roll(x, n, 0)
    hr = pltpu.roll(halo, n, 0)
    row = lax.broadcasted_iota(jnp.int32, halo.shape, 0)
    top = jnp.where(row < n, hr, r[:SUB])
    return jnp.concatenate([top, r[SUB:]], axis=0)


def _shift_up(x, halo, n):
    ts = x.shape[0]
    r = pltpu.roll(x, ts - n, 0)
    hr = pltpu.roll(halo, SUB - n, 0)
    row = lax.broadcasted_iota(jnp.int32, halo.shape, 0)
    bot = jnp.where(row >= SUB - n, hr, r[ts - SUB:])
    return jnp.concatenate([r[:ts - SUB], bot], axis=0)


def _conv3(x, halo, w_ref, b_ref):
    return _shift_down(x, halo, 2) * w_ref[0:1, :] + _shift_down(x, halo, 1) * w_ref[1:2, :] + x * w_ref[2:3, :] + b_ref[...]


def _prev_halo(ts, col):
    return lambda j, i: (jnp.maximum(i * (ts // SUB) - 1, 0), col(j))


def _next_halo(ts, n_rows, col):
    return lambda j, i: (jnp.minimum((i + 1) * (ts // SUB), n_rows // SUB - 1), col(j))


def _modnorm_fwd(x, g, sc, sh, ts):
    s_len, d = x.shape

    def body(x_ref, g_ref, sc_ref, sh_ref, h_ref):
        xv = x_ref[...]
        h_ref[...] = ((xv * _rsq(xv) * g_ref[...]) * (1.0 + sc_ref[...]) + sh_ref[...]).astype(BF16)

    vec = pl.BlockSpec((1, d), lambda i: (0, 0))
    return pl.pallas_call(
        body, name="modnorm_fwd", grid=(s_len // ts,),
        in_specs=[pl.BlockSpec((ts, d), lambda i: (i, 0)), vec, vec, vec],
        out_specs=pl.BlockSpec((ts, d), lambda i: (i, 0)),
        out_shape=jax.ShapeDtypeStruct((s_len, d), BF16),
        compiler_params=_cparams("parallel"),
    )(x, g, sc, sh)


def _rope_tables(pos_col, inv_freq_row, ts):
    s_len = pos_col.shape[0]
    half = QK_ROPE // 2

    def body(p_ref, f_ref, c_ref, a_ref, b_ref):
        ang = p_ref[...] * f_ref[...]
        lane = lax.broadcasted_iota(jnp.int32, ang.shape, 1)
        cos, sin = jnp.cos(ang), jnp.sin(ang)
        c_ref[...] = jnp.where(lane < 2 * half, cos, 0.0)
        a_ref[...] = jnp.where(lane < half, -sin, 0.0)
        b_ref[...] = jnp.where((lane >= half) & (lane < 2 * half), sin, 0.0)

    out = jax.ShapeDtypeStruct((s_len, LANE), F32)
    blk = pl.BlockSpec((ts, LANE), lambda i: (i, 0))
    return pl.pallas_call(
        body, name="rope_tables", grid=(s_len // ts,),
        in_specs=[pl.BlockSpec((ts, 1), lambda i: (i, 0)), pl.BlockSpec((1, LANE), lambda i: (0, 0))],
        out_specs=[blk, blk, blk], out_shape=[out, out, out],
        compiler_params=_cparams("parallel"),
    )(pos_col, inv_freq_row)


def _rope(seg, c, a, b):
    return seg * c + pltpu.roll(seg, LANE - QK_ROPE // 2, 1) * a + pltpu.roll(seg, QK_ROPE // 2, 1) * b


def _rope_t(seg, c, a, b):
    return seg * c - pltpu.roll(seg, LANE - QK_ROPE // 2, 1) * a - pltpu.roll(seg, QK_ROPE // 2, 1) * b


def _qkv_fwd(proj, lay, wuq, wk, wv, g_q, g_kv, ctab, atab, btab, n_heads, ts, scale):
    s_len = proj.shape[0]
    ql_w, kl_w = wuq.shape[0], wk.shape[0]

    def body(ql_ref, kl_ref, kr_ref, wuq_ref, wk_ref, wv_ref, gq_ref, gkv_ref, c_ref, a_ref, b_ref,
             q_out, k_out, v_out, qn_out, kvn_out):
        c, a, b = c_ref[...], a_ref[...], b_ref[...]
        ql = ql_ref[...].astype(F32)
        qn = (ql * _rsq(ql) * gq_ref[...]).astype(BF16)
        qn_out[...] = qn
        q = jnp.dot(qn, wuq_ref[...], preferred_element_type=F32)
        kl = kl_ref[...].astype(F32)
        kvn = (kl * _rsq(kl) * gkv_ref[...]).astype(BF16)
        kvn_out[...] = kvn
        kn = jnp.dot(kvn, wk_ref[...], preferred_element_type=F32)
        v_out[...] = jnp.dot(kvn, wv_ref[...], preferred_element_type=F32).astype(BF16)
        kr = _rope(kr_ref[...].astype(F32), c, a, b).astype(BF16)
        for h in range(n_heads):
            o = h * HEAD_W
            q_out[:, o:o + QK_NOPE] = (q[:, o:o + QK_NOPE] * scale).astype(BF16)
            q_out[:, o + QK_NOPE:o + HEAD_W] = (_rope(q[:, o + QK_NOPE:o + HEAD_W], c, a, b) * scale).astype(BF16)
            k_out[:, o:o + QK_NOPE] = kn[:, h * QK_NOPE:(h + 1) * QK_NOPE].astype(BF16)
            k_out[:, o + QK_NOPE:o + HEAD_W] = kr

    def full(arr):
        return pl.BlockSpec(arr.shape, lambda i: (0, 0))

    tab = pl.BlockSpec((ts, LANE), lambda i: (i, 0))
    hw, hv = n_heads * HEAD_W, n_heads * V_DIM
    return pl.pallas_call(
        body, name="qkv_fwd", grid=(s_len // ts,),
        in_specs=[pl.BlockSpec((ts, ql_w), lambda i: (i, lay["ql"] // ql_w)),
                  pl.BlockSpec((ts, kl_w), lambda i: (i, lay["kv"] // kl_w)),
                  pl.BlockSpec((ts, LANE), lambda i: (i, lay["kr"] // LANE)),
                  full(wuq), full(wk), full(wv), full(g_q), full(g_kv), tab, tab, tab],
        out_specs=[pl.BlockSpec((ts, hw), lambda i: (i, 0)), pl.BlockSpec((ts, hw), lambda i: (i, 0)),
                   pl.BlockSpec((ts, hv), lambda i: (i, 0)), pl.BlockSpec((ts, ql_w), lambda i: (i, 0)),
                   pl.BlockSpec((ts, kl_w), lambda i: (i, 0))],
        out_shape=[jax.ShapeDtypeStruct((s_len, hw), BF16), jax.ShapeDtypeStruct((s_len, hw), BF16),
                   jax.ShapeDtypeStruct((s_len, hv), BF16), jax.ShapeDtypeStruct((s_len, ql_w), BF16),
                   jax.ShapeDtypeStruct((s_len, kl_w), BF16)],
        compiler_params=_cparams("parallel"),
    )(proj, proj, proj, wuq, wk, wv, g_q, g_kv, ctab, atab, btab)


def _flash_fwd(q, k, v, n_heads, t, gather, out_cols):
    ng = len(gather)
    ck = _tile(t, PREF["attn_chunk"])
    s_len = q.shape[0]
    nb = s_len // t
    pairs = [(i, j) for i in range(nb) for j in range(i + 1)]
    itab = jnp.asarray(np.array([p[0] for p in pairs], np.int32))
    jtab = jnp.asarray(np.array([p[1] for p in pairs], np.int32))

    n_steps = len(pairs)

    def body(it_ref, jt_ref, q_ref, k_ref, v_ref, *rest):
        gin, (o_ref, lse_ref), gout = rest[:ng], rest[ng:ng + 2], rest[ng + 2:2 * ng + 2]
        m_sc, l_sc, acc_sc = rest[2 * ng + 2:2 * ng + 5]
        sems = rest[2 * ng + 5:]
        head, step_id = pl.program_id(0), pl.program_id(1)
        i, j = it_ref[step_id], jt_ref[step_id]

        @pl.when((head == 0) & (step_id == 0))
        def _():
            _gather_start(gin, gout, *sems)

        @pl.when((head == n_heads // 2) & (step_id == 0))
        def _():
            _gather_forward(gin, gout, *sems)

        @pl.when(j == 0)
        def _():
            m_sc[...] = jnp.full(m_sc.shape, NEG, F32)
            l_sc[...] = jnp.zeros(l_sc.shape, F32)
            acc_sc[...] = jnp.zeros(acc_sc.shape, F32)

        def step(diag):
            for c in range(t // ck):
                q0 = c * ck if diag else 0
                qs, ks = slice(q0, t), slice(c * ck, (c + 1) * ck)
                s_t = lax.dot_general(k_ref[ks, :], q_ref[qs, :], NT_DIMS, preferred_element_type=F32)
                if diag:
                    krow = lax.broadcasted_iota(jnp.int32, s_t.shape, 0)
                    qcol = lax.broadcasted_iota(jnp.int32, s_t.shape, 1)
                    s_t = jnp.where(krow <= qcol, s_t, NEG)
                m_prev = m_sc[:, qs]
                m_new = jnp.maximum(m_prev, jnp.max(s_t, axis=0, keepdims=True))
                alpha = jnp.exp2(m_prev - m_new)
                p_t = jnp.exp2(s_t - m_new)
                l_sc[:, qs] = alpha * l_sc[:, qs] + jnp.sum(p_t, axis=0, keepdims=True)
                acc_sc[:, qs] = acc_sc[:, qs] * alpha + lax.dot_general(v_ref[ks, :], p_t.astype(BF16), TN_DIMS,
                                                                        preferred_element_type=F32)
                m_sc[:, qs] = m_new

        @pl.when(j < i)
        def _():
            step(False)

        @pl.when(j == i)
        def _():
            step(True)
            l = l_sc[...]
            o_ref[...] = jnp.transpose(acc_sc[...] / l).astype(BF16)
            lse_ref[0] = m_sc[...] + jnp.log(l) * LOG2E

        @pl.when((head == n_heads - 1) & (step_id == n_steps - 1))
        def _():
            _gather_finish(gin, gout, *sems)

    hbm = pl.BlockSpec(memory_space=pl.ANY)
    grid_spec = pltpu.PrefetchScalarGridSpec(
        num_scalar_prefetch=2, grid=(n_heads, n_steps),
        in_specs=[pl.BlockSpec((t, HEAD_W), lambda h, s, it, jt: (it[s], h)),
                  pl.BlockSpec((t, HEAD_W), lambda h, s, it, jt: (jt[s], h)),
                  pl.BlockSpec((t, V_DIM), lambda h, s, it, jt: (jt[s], h))] + [hbm] * ng,
        out_specs=[pl.BlockSpec((t, V_DIM), lambda h, s, it, jt: (it[s], h)),
                   pl.BlockSpec((1, 1, t), lambda h, s, it, jt: (h, 0, it[s]))] + [hbm] * ng,
        scratch_shapes=[pltpu.VMEM((1, t), F32), pltpu.VMEM((1, t), F32), pltpu.VMEM((V_DIM, t), F32)]
        + _comm_scratch(ng),
    )
    return pl.pallas_call(
        body, name="flash_fwd", grid_spec=grid_spec,
        out_shape=[jax.ShapeDtypeStruct((s_len, out_cols), BF16),
                   jax.ShapeDtypeStruct((n_heads, 1, s_len), F32)] + _gathered_shapes(gather),
        compiler_params=_cparams("arbitrary", "arbitrary"),
    )(itab, jtab, q, k, v, *gather)


def _gconv_fwd(proj, lay, w, b, ts, tc, mixcat):
    s_len = proj.shape[0]
    cw = w.shape[1]
    nj = cw // tc
    out_off = mixcat.shape[1] - cw

    def body(gb_ref, gc_ref, ci_ref, gch_ref, cih_ref, w_ref, b_ref, mix_in, o_ref):
        i = pl.program_id(1)
        p = gc_ref[...].astype(F32) * ci_ref[...].astype(F32)
        ph = jnp.where(i > 0, gch_ref[...].astype(F32) * cih_ref[...].astype(F32), 0.0)
        o_ref[...] = (gb_ref[...].astype(F32) * _conv3(p, ph, w_ref, b_ref)).astype(BF16)

    def blk(off):
        return pl.BlockSpec((ts, tc), lambda j, i: (i, off // tc + j))

    def halo(off):
        return pl.BlockSpec((SUB, tc), _prev_halo(ts, lambda j: off // tc + j))

    return pl.pallas_call(
        body, name="gconv_fwd", grid=(nj, s_len // ts),
        in_specs=[blk(lay["gb"]), blk(lay["gc"]), blk(lay["ci"]), halo(lay["gc"]), halo(lay["ci"]),
                  pl.BlockSpec((3, tc), lambda j, i: (0, j)), pl.BlockSpec((1, tc), lambda j, i: (0, j)),
                  pl.BlockSpec(memory_space=pl.ANY)],
        out_specs=pl.BlockSpec((ts, tc), lambda j, i: (i, out_off // tc + j)),
        out_shape=jax.ShapeDtypeStruct(mixcat.shape, BF16),
        input_output_aliases={7: 0},
        compiler_params=_cparams("parallel", "parallel"),
    )(proj, proj, proj, proj, proj, w, b, mixcat)


def _post_mix_fwd(mix, x, gt, g_post, g_pre, sc, sh, ts):
    s_len, d = x.shape

    def body(mix_ref, x_ref, gt_ref, gp_ref, g2_ref, sc_ref, sh_ref, x1_ref, h2_ref):
        mv = mix_ref[...]
        x1 = x_ref[...] + gt_ref[...] * (mv * _rsq(mv) * gp_ref[...])
        x1_ref[...] = x1
        h2_ref[...] = ((x1 * _rsq(x1) * g2_ref[...]) * (1.0 + sc_ref[...]) + sh_ref[...]).astype(BF16)

    vec = pl.BlockSpec((1, d), lambda i: (0, 0))
    row = pl.BlockSpec((ts, d), lambda i: (i, 0))
    return pl.pallas_call(
        body, name="post_mix_fwd", grid=(s_len // ts,),
        in_specs=[row, row, vec, vec, vec, vec, vec], out_specs=[row, row],
        out_shape=[jax.ShapeDtypeStruct((s_len, d), F32), jax.ShapeDtypeStruct((s_len, d), BF16)],
        compiler_params=_cparams("parallel"),
    )(mix, x, gt, g_post, g_pre, sc, sh)


def _act_fwd(up, w, b, ts, tc):
    s_len, f2 = up.shape
    nh = f2 // 2 // tc

    def body(up_ref, uph_ref, w_ref, b_ref, u_ref, o_ref):
        i = pl.program_id(1)
        halo = jnp.where(i > 0, uph_ref[...].astype(F32), 0.0)
        u = _conv3(up_ref[...].astype(F32), halo, w_ref, b_ref)
        u_ref[...] = u.astype(BF16)
        ua, ug = u[:, :tc], u[:, tc:]
        o_ref[...] = (ug * _sigmoid(ug) * ua).astype(BF16)

    pair = pl.BlockSpec((ts, 2 * tc), lambda j, i: (i, j))
    return pl.pallas_call(
        body, name="act_fwd", grid=(nh, s_len // ts),
        in_specs=[pair, pl.BlockSpec((SUB, 2 * tc), _prev_halo(ts, lambda j: j)),
                  pl.BlockSpec((3, 2 * tc), lambda j, i: (0, j)), pl.BlockSpec((1, 2 * tc), lambda j, i: (0, j))],
        out_specs=[pair, pl.BlockSpec((ts, tc), lambda j, i: (i, j))],
        out_shape=[jax.ShapeDtypeStruct((s_len, f2), BF16), jax.ShapeDtypeStruct((s_len, f2 // 2), BF16)],
        compiler_params=_cparams("parallel", "parallel"),
    )(up, up, w, b)


def _final_bwd(y, x1, tgt, gt, g_post, ts):
    s_len, d = y.shape

    def body(y_ref, x1_ref, t_ref, gt_ref, g_ref, dy_ref, dx2_ref, loss_ref, dgt_ref, dg_ref):
        i = pl.program_id(0)
        yv = y_ref[...]
        r = _rsq(yv)
        yh = yv * r
        n = yh * g_ref[...]
        e = x1_ref[...] + gt_ref[...] * n - t_ref[...]
        loss = 0.5 * jnp.sum(jnp.mean(e * e, axis=-1, keepdims=True), axis=0, keepdims=True)
        dx2 = e * (1.0 / d)
        dx2_ref[...] = dx2
        dn = dx2 * gt_ref[...]
        dyh = dn * g_ref[...]
        dy_ref[...] = (r * (dyh - yh * jnp.mean(dyh * yh, axis=-1, keepdims=True))).astype(BF16)

        @pl.when(i == 0)
        def _():
            loss_ref[...] = jnp.zeros(loss_ref.shape, F32)
            dgt_ref[...] = jnp.zeros(dgt_ref.shape, F32)
            dg_ref[...] = jnp.zeros(dg_ref.shape, F32)

        loss_ref[...] += jnp.broadcast_to(loss, loss_ref.shape)
        dgt_ref[...] += _colsum(dx2 * n)
        dg_ref[...] += _colsum(dn * yh)

    vec = pl.BlockSpec((1, d), lambda i: (0, 0))
    row = pl.BlockSpec((ts, d), lambda i: (i, 0))
    vshape = jax.ShapeDtypeStruct((1, d), F32)
    return pl.pallas_call(
        body, name="final_bwd", grid=(s_len // ts,),
        in_specs=[row, row, row, vec, vec],
        out_specs=[row, row, pl.BlockSpec((1, LANE), lambda i: (0, 0)), vec, vec],
        out_shape=[jax.ShapeDtypeStruct((s_len, d), BF16), jax.ShapeDtypeStruct((s_len, d), F32),
                   jax.ShapeDtypeStruct((1, LANE), F32), vshape, vshape],
        compiler_params=_cparams("arbitrary"),
    )(y, x1, tgt, gt, g_post)


def _ffn_act_bwd(d_act, u, up, w, ts, tc):
    s_len, f2 = up.shape
    nh = f2 // 2 // tc
    n_i = s_len // ts

    def gate(dv, uv):
        ua, ug = uv[:, :tc], uv[:, tc:]
        sg = _sigmoid(ug)
        return jnp.concatenate([dv * (ug * sg), dv * ua * (sg * (1.0 + ug * (1.0 - sg)))], axis=1)

    def body(d_ref, dh_ref, u_ref, uh_ref, x_ref, w_ref, dx_ref, dw_ref, db_ref):
        i = pl.program_id(1)
        du = gate(d_ref[...].astype(F32), u_ref[...].astype(F32))
        duh = jnp.where(i < n_i - 1, gate(dh_ref[...].astype(F32), uh_ref[...].astype(F32)), 0.0)
        du1, du2 = _shift_up(du, duh, 1), _shift_up(du, duh, 2)
        dx_ref[...] = (du * w_ref[2:3, :] + du1 * w_ref[1:2, :] + du2 * w_ref[0:1, :]).astype(BF16)

        @pl.when(i == 0)
        def _():
            dw_ref[...] = jnp.zeros(dw_ref.shape, F32)
            db_ref[...] = jnp.zeros(db_ref.shape, F32)

        xv = x_ref[...].astype(F32)
        dw_ref[0:1, :] += _colsum(du2 * xv)
        dw_ref[1:2, :] += _colsum(du1 * xv)
        dw_ref[2:3, :] += _colsum(du * xv)
        db_ref[...] += _colsum(du)

    pair = pl.BlockSpec((ts, 2 * tc), lambda j, i: (i, j))
    pair_halo = pl.BlockSpec((SUB, 2 * tc), _next_halo(ts, s_len, lambda j: j))
    return pl.pallas_call(
        body, name="ffn_act_bwd", grid=(nh, n_i),
        in_specs=[pl.BlockSpec((ts, tc), lambda j, i: (i, j)), pl.BlockSpec((SUB, tc), _next_halo(ts, s_len, lambda j: j)),
                  pair, pair_halo, pair, pl.BlockSpec((3, 2 * tc), lambda j, i: (0, j))],
        out_specs=[pair, pl.BlockSpec((3, 2 * tc), lambda j, i: (0, j)), pl.BlockSpec((1, 2 * tc), lambda j, i: (0, j))],
        out_shape=[jax.ShapeDtypeStruct((s_len, f2), BF16), jax.ShapeDtypeStruct((3, f2), F32),
                   jax.ShapeDtypeStruct((1, f2), F32)],
        compiler_params=_cparams("parallel", "arbitrary"),
    )(d_act, d_act, u, u, up, w)


def _mid_bwd(dh2, x1, dx2, mix, g_pre, sc, gt_m, g_post, ts):
    s_len, d = x1.shape

    def body(dh_ref, x1_ref, dx2_ref, mix_ref, g_ref, sc_ref, gt_ref, gp_ref,
             dx1_ref, dmix_ref, dsh_ref, dsc_ref, dg_ref, dgt_ref, dgp_ref):
        i = pl.program_id(0)
        dh = dh_ref[...]
        x1 = x1_ref[...]
        r1 = _rsq(x1)
        xh = x1 * r1
        dxh = dh * (1.0 + sc_ref[...]) * g_ref[...]
        dx1 = dx2_ref[...] + r1 * (dxh - xh * jnp.mean(dxh * xh, axis=-1, keepdims=True))
        dx1_ref[...] = dx1
        mv = mix_ref[...]
        rm = _rsq(mv)
        mh = mv * rm
        dn = dx1 * gt_ref[...]
        dmh = dn * gp_ref[...]
        dmix_ref[...] = (rm * (dmh - mh * jnp.mean(dmh * mh, axis=-1, keepdims=True))).astype(BF16)

        @pl.when(i == 0)
        def _():
            for ref in (dsh_ref, dsc_ref, dg_ref, dgt_ref, dgp_ref):
                ref[...] = jnp.zeros(ref.shape, F32)

        dsh_ref[...] += _colsum(dh)
        dsc_ref[...] += _colsum(dh * (xh * g_ref[...]))
        dg_ref[...] += _colsum(dh * (1.0 + sc_ref[...]) * xh)
        dgt_ref[...] += _colsum(dx1 * (mh * gp_ref[...]))
        dgp_ref[...] += _colsum(dn * mh)

    vec = pl.BlockSpec((1, d), lambda i: (0, 0))
    row = pl.BlockSpec((ts, d), lambda i: (i, 0))
    vshape = jax.ShapeDtypeStruct((1, d), F32)
    return pl.pallas_call(
        body, name="mid_bwd", grid=(s_len // ts,),
        in_specs=[row, row, row, row, vec, vec, vec, vec],
        out_specs=[row, row, vec, vec, vec, vec, vec],
        out_shape=[jax.ShapeDtypeStruct((s_len, d), F32), jax.ShapeDtypeStruct((s_len, d), BF16)] + [vshape] * 5,
        compiler_params=_cparams("arbitrary"),
    )(dh2, x1, dx2, mix, g_pre, sc, gt_m, g_post)


def _first_bwd(dh1, x, dx1, g_pre, sc, ts):
    s_len, d = x.shape

    def body(dh_ref, x_ref, dx1_ref, g_ref, sc_ref, dx_ref, dsh_ref, dsc_ref, dg_ref):
        i = pl.program_id(0)
        dh = dh_ref[...]
        xv = x_ref[...]
        r = _rsq(xv)
        xh = xv * r
        dxh = dh * (1.0 + sc_ref[...]) * g_ref[...]
        dx_ref[...] = dx1_ref[...] + r * (dxh - xh * jnp.mean(dxh * xh, axis=-1, keepdims=True))

        @pl.when(i == 0)
        def _():
            for ref in (dsh_ref, dsc_ref, dg_ref):
                ref[...] = jnp.zeros(ref.shape, F32)

        dsh_ref[...] += _colsum(dh)
        dsc_ref[...] += _colsum(dh * (xh * g_ref[...]))
        dg_ref[...] += _colsum(dh * (1.0 + sc_ref[...]) * xh)

    vec = pl.BlockSpec((1, d), lambda i: (0, 0))
    row = pl.BlockSpec((ts, d), lambda i: (i, 0))
    vshape = jax.ShapeDtypeStruct((1, d), F32)
    return pl.pallas_call(
        body, name="first_bwd", grid=(s_len // ts,),
        in_specs=[row, row, row, vec, vec], out_specs=[row, vec, vec, vec],
        out_shape=[jax.ShapeDtypeStruct((s_len, d), F32), vshape, vshape, vshape],
        compiler_params=_cparams("arbitrary"),
    )(dh1, x, dx1, g_pre, sc)


def _gconv_bwd(d_mixcat, proj, lay, w, b, ts, tc):
    s_len = proj.shape[0]
    cw = w.shape[1]
    n_i = s_len // ts
    dc_off = d_mixcat.shape[1] - cw

    def body(dc_ref, dch_ref, gb_ref, gbh_ref, gc_ref, gch_ref, ci_ref, cih_ref, w_ref, b_ref,
             dgb_ref, dgc_ref, dci_ref, dw_ref, db_ref):
        i = pl.program_id(1)
        gc, ci = gc_ref[...].astype(F32), ci_ref[...].astype(F32)
        p = gc * ci
        ph = jnp.where(i > 0, gch_ref[...].astype(F32) * cih_ref[...].astype(F32), 0.0)
        pm1, pm2 = _shift_down(p, ph, 1), _shift_down(p, ph, 2)
        z = pm2 * w_ref[0:1, :] + pm1 * w_ref[1:2, :] + p * w_ref[2:3, :] + b_ref[...]
        dc = dc_ref[...].astype(F32)
        dgb_ref[...] = (dc * z).astype(BF16)
        dz = dc * gb_ref[...].astype(F32)
        dzh = jnp.where(i < n_i - 1, dch_ref[...].astype(F32) * gbh_ref[...].astype(F32), 0.0)
        dz1, dz2 = _shift_up(dz, dzh, 1), _shift_up(dz, dzh, 2)
        dp = dz * w_ref[2:3, :] + dz1 * w_ref[1:2, :] + dz2 * w_ref[0:1, :]
        dgc_ref[...] = (dp * ci).astype(BF16)
        dci_ref[...] = (dp * gc).astype(BF16)

        @pl.when(i == 0)
        def _():
            dw_ref[...] = jnp.zeros(dw_ref.shape, F32)
            db_ref[...] = jnp.zeros(db_ref.shape, F32)

        dw_ref[0:1, :] += _colsum(dz2 * p)
        dw_ref[1:2, :] += _colsum(dz1 * p)
        dw_ref[2:3, :] += _colsum(dz * p)
        db_ref[...] += _colsum(dz)

    def blk(off):
        return pl.BlockSpec((ts, tc), lambda j, i: (i, off // tc + j))

    def prev(off):
        return pl.BlockSpec((SUB, tc), _prev_halo(ts, lambda j: off // tc + j))

    def nxt(off):
        return pl.BlockSpec((SUB, tc), _next_halo(ts, s_len, lambda j: off // tc + j))

    out_blk = pl.BlockSpec((ts, tc), lambda j, i: (i, j))
    act = jax.ShapeDtypeStruct((s_len, cw), BF16)
    return pl.pallas_call(
        body, name="gconv_bwd", grid=(cw // tc, n_i),
        in_specs=[blk(dc_off), nxt(dc_off), blk(lay["gb"]), nxt(lay["gb"]), blk(lay["gc"]), prev(lay["gc"]),
                  blk(lay["ci"]), prev(lay["ci"]),
                  pl.BlockSpec((3, tc), lambda j, i: (0, j)), pl.BlockSpec((1, tc), lambda j, i: (0, j))],
        out_specs=[out_blk, out_blk, out_blk,
                   pl.BlockSpec((3, tc), lambda j, i: (0, j)), pl.BlockSpec((1, tc), lambda j, i: (0, j))],
        out_shape=[act, act, act, jax.ShapeDtypeStruct((3, cw), F32), jax.ShapeDtypeStruct((1, cw), F32)],
        compiler_params=_cparams("parallel", "arbitrary"),
    )(d_mixcat, d_mixcat, proj, proj, proj, proj, proj, proj, w, b)


def _delta(o, d_mixcat, n_heads, ts):
    s_len = o.shape[0]

    def body(o_ref, do_ref, out_ref):
        for h in range(n_heads):
            sl = slice(h * V_DIM, (h + 1) * V_DIM)
            prod = o_ref[:, sl].astype(F32) * do_ref[:, sl].astype(F32)
            out_ref[h] = jnp.broadcast_to(jnp.sum(prod, axis=1, keepdims=True), (ts, LANE))

    hv = n_heads * V_DIM
    return pl.pallas_call(
        body, name="attn_delta", grid=(s_len // ts,),
        in_specs=[pl.BlockSpec((ts, hv), lambda i: (i, 0)), pl.BlockSpec((ts, hv), lambda i: (i, 0))],
        out_specs=pl.BlockSpec((n_heads, ts, LANE), lambda i: (0, i, 0)),
        out_shape=jax.ShapeDtypeStruct((n_heads, s_len, LANE), F32),
        compiler_params=_cparams("parallel"),
    )(o, d_mixcat)


def _flash_bwd(q, k, v, d_mixcat, lse_row, delta_row, n_heads, t, scale, exchange):
    nx = len(exchange)
    s_len = q.shape[0]
    nb = s_len // t
    pairs = [(j, i) for j in range(nb) for i in range(j, nb)]
    jtab = jnp.asarray(np.array([p[0] for p in pairs], np.int32))
    itab = jnp.asarray(np.array([p[1] for p in pairs], np.int32))
    n_steps = len(pairs)

    def body(jt_ref, it_ref, q_ref, k_ref, v_ref, do_ref, lse_ref, dl_ref, *rest):
        xin, (dq_ref, dk_ref, dv_ref), xout = rest[:nx], rest[nx:nx + 3], rest[nx + 3:2 * nx + 3]
        dq_acc, dk_acc, dv_acc = rest[2 * nx + 3:2 * nx + 6]
        sems = rest[2 * nx + 6:]
        head, step_id = pl.program_id(0), pl.program_id(1)
        j, i = jt_ref[step_id], it_ref[step_id]

        @pl.when((head == 0) & (step_id == 0))
        def _():
            _exchange_start(xin, xout, *sems)

        @pl.when(step_id == 0)
        def _():
            dq_acc[...] = jnp.zeros(dq_acc.shape, F32)

        @pl.when(i == j)
        def _():
            dk_acc[...] = jnp.zeros(dk_acc.shape, F32)
            dv_acc[...] = jnp.zeros(dv_acc.shape, F32)

        def step(diag):
            qv, kv, vv, dov = q_ref[...], k_ref[...], v_ref[...], do_ref[...]
            s_t = lax.dot_general(kv, qv, NT_DIMS, preferred_element_type=F32)
            if diag:
                krow = lax.broadcasted_iota(jnp.int32, s_t.shape, 0)
                qcol = lax.broadcasted_iota(jnp.int32, s_t.shape, 1)
                s_t = jnp.where(krow <= qcol, s_t, NEG)
            p_t = jnp.exp2(s_t - lse_ref[0])
            dv_acc[...] += jnp.dot(p_t.astype(BF16), dov, preferred_element_type=F32)
            dp_t = lax.dot_general(vv, dov, NT_DIMS, preferred_element_type=F32)
            ds_t = (p_t * (dp_t - dl_ref[0])).astype(BF16)
            dk_acc[...] += jnp.dot(ds_t, qv, preferred_element_type=F32)
            rows = pl.ds(pl.multiple_of(i * t, t), t)
            dq_acc[rows, :] += lax.dot_general(ds_t, kv, TN_DIMS, preferred_element_type=F32)

        @pl.when(i > j)
        def _():
            step(False)

        @pl.when(i == j)
        def _():
            step(True)

        @pl.when(i == nb - 1)
        def _():
            dk_ref[...] = (dk_acc[...] * LN2).astype(BF16)
            dv_ref[...] = dv_acc[...].astype(BF16)

        @pl.when(step_id == n_steps - 1)
        def _():
            dq_ref[...] = (dq_acc[...] * scale).astype(BF16)

        @pl.when((head == n_heads - 1) & (step_id == n_steps - 1))
        def _():
            _exchange_finish(xin, xout, *sems)

    hbm = pl.BlockSpec(memory_space=pl.ANY)
    hv = n_heads * V_DIM
    do_off = 0
    grid_spec = pltpu.PrefetchScalarGridSpec(
        num_scalar_prefetch=2, grid=(n_heads, n_steps),
        in_specs=[pl.BlockSpec((t, HEAD_W), lambda h, s, jt, it: (it[s], h)),
                  pl.BlockSpec((t, HEAD_W), lambda h, s, jt, it: (jt[s], h)),
                  pl.BlockSpec((t, V_DIM), lambda h, s, jt, it: (jt[s], h)),
                  pl.BlockSpec((t, V_DIM), lambda h, s, jt, it: (it[s], do_off + h)),
                  pl.BlockSpec((1, 1, t), lambda h, s, jt, it: (h, 0, it[s])),
                  pl.BlockSpec((1, 1, t), lambda h, s, jt, it: (h, 0, it[s]))] + [hbm] * nx,
        out_specs=[pl.BlockSpec((s_len, HEAD_W), lambda h, s, jt, it: (0, h)),
                   pl.BlockSpec((t, HEAD_W), lambda h, s, jt, it: (jt[s], h)),
                   pl.BlockSpec((t, V_DIM), lambda h, s, jt, it: (jt[s], h))] + [hbm] * nx,
        scratch_shapes=[pltpu.VMEM((s_len, HEAD_W), F32), pltpu.VMEM((t, HEAD_W), F32), pltpu.VMEM((t, V_DIM), F32)]
        + _comm_scratch(nx),
    )
    return pl.pallas_call(
        body, name="flash_bwd", grid_spec=grid_spec,
        out_shape=[jax.ShapeDtypeStruct((s_len, n_heads * HEAD_W), BF16),
                   jax.ShapeDtypeStruct((s_len, n_heads * HEAD_W), BF16),
                   jax.ShapeDtypeStruct((s_len, hv), BF16)] + [jax.ShapeDtypeStruct(c.shape, c.dtype) for c in exchange],
        compiler_params=_cparams("arbitrary", "arbitrary"),
    )(jtab, itab, q, k, v, d_mixcat, lse_row, delta_row, *exchange)


def _qkv_bwd(dq, dk, dv, proj, lay, wuq, wk, wv, g_q, g_kv, ctab, atab, btab, n_heads, ts):
    s_len = proj.shape[0]
    ql_w, kl_w = wuq.shape[0], wk.shape[0]
    tail_w = lay["np"] - lay["ql"]
    kv_o, kr_o = lay["kv"] - lay["ql"], lay["kr"] - lay["ql"]

    def body(dq_ref, dk_ref, dv_ref, ql_ref, kl_ref, wuq_ref, wk_ref, wv_ref, gq_ref, gkv_ref, c_ref, a_ref, b_ref,
             dqr_ref, dkn_ref, tail_ref, dgq_ref, dgkv_ref):
        i = pl.program_id(0)
        c, a, b = c_ref[...], a_ref[...], b_ref[...]
        dkr = jnp.zeros((ts, LANE), F32)
        for h in range(n_heads):
            o = h * HEAD_W
            dqr_ref[:, o:o + QK_NOPE] = dq_ref[:, o:o + QK_NOPE]
            dqr_ref[:, o + QK_NOPE:o + HEAD_W] = _rope_t(dq_ref[:, o + QK_NOPE:o + HEAD_W].astype(F32), c, a, b).astype(BF16)
            dkn_ref[:, h * QK_NOPE:(h + 1) * QK_NOPE] = dk_ref[:, o:o + QK_NOPE]
            dkr = dkr + dk_ref[:, o + QK_NOPE:o + HEAD_W].astype(F32)
        tail_ref[...] = jnp.zeros(tail_ref.shape, BF16)
        tail_ref[:, kr_o:kr_o + LANE] = _rope_t(dkr, c, a, b).astype(BF16)

        def rms_bwd(lat_ref, dn, g_ref):
            lat = lat_ref[...].astype(F32)
            r = _rsq(lat)
            xh = lat * r
            dxh = dn * g_ref[...]
            return r * (dxh - xh * jnp.mean(dxh * xh, axis=-1, keepdims=True)), _colsum(dn * xh)

        dqn = lax.dot_general(dqr_ref[...], wuq_ref[...], NT_DIMS, preferred_element_type=F32)
        d_ql, dgq = rms_bwd(ql_ref, dqn, gq_ref)
        tail_ref[:, 0:ql_w] = d_ql.astype(BF16)
        dkvn = (lax.dot_general(dkn_ref[...], wk_ref[...], NT_DIMS, preferred_element_type=F32)
                + lax.dot_general(dv_ref[...], wv_ref[...], NT_DIMS, preferred_element_type=F32))
        d_kl, dgkv = rms_bwd(kl_ref, dkvn, gkv_ref)
        tail_ref[:, kv_o:kv_o + kl_w] = d_kl.astype(BF16)

        @pl.when(i == 0)
        def _():
            dgq_ref[...] = jnp.zeros(dgq_ref.shape, F32)
            dgkv_ref[...] = jnp.zeros(dgkv_ref.shape, F32)

        dgq_ref[...] += dgq
        dgkv_ref[...] += dgkv

    def full(arr):
        return pl.BlockSpec(arr.shape, lambda i: (0, 0))

    def rows(w):
        return pl.BlockSpec((ts, w), lambda i: (i, 0))

    tab = pl.BlockSpec((ts, LANE), lambda i: (i, 0))
    hw, hv, hn = n_heads * HEAD_W, n_heads * V_DIM, n_heads * QK_NOPE
    return pl.pallas_call(
        body, name="qkv_bwd", grid=(s_len // ts,),
        in_specs=[rows(hw), rows(hw), rows(hv),
                  pl.BlockSpec((ts, ql_w), lambda i: (i, lay["ql"] // ql_w)),
                  pl.BlockSpec((ts, kl_w), lambda i: (i, lay["kv"] // kl_w)),
                  full(wuq), full(wk), full(wv), full(g_q), full(g_kv), tab, tab, tab],
        out_specs=[rows(hw), rows(hn), rows(tail_w), full(g_q), full(g_kv)],
        out_shape=[jax.ShapeDtypeStruct((s_len, hw), BF16), jax.ShapeDtypeStruct((s_len, hn), BF16),
                   jax.ShapeDtypeStruct((s_len, tail_w), BF16),
                   jax.ShapeDtypeStruct(g_q.shape, F32), jax.ShapeDtypeStruct(g_kv.shape, F32)],
        compiler_params=_cparams("arbitrary"),
    )(dq, dk, dv, proj, proj, wuq, wk, wv, g_q, g_kv, ctab, atab, btab)


def _adamw(w, g, m, v):
    m = ADAM_B1 * m + (1.0 - ADAM_B1) * g
    v = ADAM_B2 * v + (1.0 - ADAM_B2) * (g * g)
    m_hat = m / (1.0 - ADAM_B1 ** ADAM_STEP)
    v_hat = v / (1.0 - ADAM_B2 ** ADAM_STEP)
    delta = -ADAM_LR * (m_hat / (jnp.sqrt(v_hat) + ADAM_EPS) + ADAM_WD * w)
    return delta, m, v


def _adam_parts(parts, w, m, v, tr, name):
    r, c = w.shape
    tr = _tile(r, tr, SUB)

    def body(p_ref, w_ref, m_ref, v_ref, g_out, d_out, m_out, v_out):
        g = p_ref[0].astype(F32)
        for dev in range(1, N_DEV):
            g = g + p_ref[dev].astype(F32)
        g_out[...] = g
        d_out[...], m_out[...], v_out[...] = _adamw(w_ref[...], g, m_ref[...], v_ref[...])

    blk = pl.BlockSpec((tr, c), lambda i: (i, 0))
    shp = jax.ShapeDtypeStruct((r, c), F32)
    return pl.pallas_call(
        body, name=name, grid=(r // tr,),
        in_specs=[pl.BlockSpec((N_DEV, tr, c), lambda i: (0, i, 0)), blk, blk, blk],
        out_specs=[blk, blk, blk, blk], out_shape=[shp, shp, shp, shp],
        compiler_params=_cparams("parallel"),
    )(parts, w, m, v)


def _adam_ada(cact_t, dmod_sh, w, m, v, tr):
    r, c = w.shape

    def body(ct_ref, dm_ref, w_ref, m_ref, v_ref, g_out, d_out, m_out, v_out):
        g = jnp.dot(ct_ref[...], dm_ref[...], preferred_element_type=F32, precision=lax.Precision.HIGHEST)
        g_out[...] = g
        d_out[...], m_out[...], v_out[...] = _adamw(w_ref[...], g, m_ref[...], v_ref[...])

    blk = pl.BlockSpec((tr, c), lambda i: (i, 0))
    shp = jax.ShapeDtypeStruct((r, c), F32)
    return pl.pallas_call(
        body, name="adam_ada", grid=(r // tr,),
        in_specs=[pl.BlockSpec((tr, N_DEV), lambda i: (i, 0)), pl.BlockSpec((N_DEV, c), lambda i: (0, 0)), blk, blk, blk],
        out_specs=[blk, blk, blk, blk], out_shape=[shp, shp, shp, shp],
        compiler_params=_cparams("parallel"),
    )(cact_t, dmod_sh, w, m, v)


def _adam_small(v_all, offs, ws, ms, vs):
    n_par = len(ws)

    def body(p_ref, *refs):
        w_refs, m_refs, v_refs = refs[:n_par], refs[n_par:2 * n_par], refs[2 * n_par:3 * n_par]
        sum_ref = refs[3 * n_par]
        outs = refs[3 * n_par + 1:]
        g = p_ref[0:1, :]
        for dev in range(1, N_DEV):
            g = g + p_ref[dev:dev + 1, :]
        sum_ref[...] = g
        for p in range(n_par):
            n = w_refs[p].shape[1]
            gp = sum_ref[:, offs[p]:offs[p] + n]
            outs[p][...] = gp
            (outs[n_par + p][...], outs[2 * n_par + p][...], outs[3 * n_par + p][...]) = _adamw(
                w_refs[p][...], gp, m_refs[p][...], v_refs[p][...])

    vm = pl.BlockSpec(memory_space=pltpu.VMEM)
    shapes = [jax.ShapeDtypeStruct(w.shape, F32) for w in ws]
    out = pl.pallas_call(
        body, name="adam_small", in_specs=[vm] * (1 + 3 * n_par), out_specs=[vm] * (1 + 4 * n_par),
        out_shape=[jax.ShapeDtypeStruct((1, v_all.shape[1]), F32)] + shapes * 4, compiler_params=_cparams(),
    )(v_all, *ws, *ms, *vs)
    return out[0], [out[1 + k * n_par:1 + (k + 1) * n_par] for k in range(4)]


def _my_place():
    return lax.axis_index("x"), lax.axis_index("y"), lax.axis_index("c")


def _peer(place, k):
    x, y, c = place
    return (x ^ (k >> 2), y ^ ((k >> 1) & 1), c ^ (k & 1))


def _index(place):
    return 4 * place[0] + 2 * place[1] + place[2]


def _ada_fwd(vec, w_ada, b_ada_rows):
    lv = vec.shape[1]
    d, c = w_ada.shape

    def body(vec_ref, w_ref, b_ref, gath_ref, cact_ref, mod_ref, modsh, send_a, recv_a, send_b, recv_b, local_s):
        me = _my_place()
        my_i = _index(me)

        def gather_copy(k, to, src_row):
            row = gath_ref.at[pl.ds(src_row, 1), :]
            return pltpu.make_async_remote_copy(src_ref=row, dst_ref=row, send_sem=send_a.at[k], recv_sem=recv_a.at[k],
                                                device_id=to, device_id_type=MESH)

        own = pltpu.make_async_copy(vec_ref, gath_ref.at[pl.ds(my_i, 1), :], local_s.at[0])
        own.start()
        own.wait()
        sends = [gather_copy(k, _peer(me, k), my_i) for k in range(1, N_DEV)]
        for cp in sends:
            cp.start()
        for k in range(1, N_DEV):
            gather_copy(k, me, _index(_peer(me, k))).wait_recv()
        for cp in sends:
            cp.wait_send()

        c_all = gath_ref[:, 0:d]
        cact = c_all * _sigmoid(c_all)
        cact_ref[...] = cact
        modsh[...] = jnp.dot(cact, w_ref[...], preferred_element_type=F32, precision=lax.Precision.HIGHEST)

        def mod_copy(k, to, src_row, dst_row):
            return pltpu.make_async_remote_copy(src_ref=modsh.at[pl.ds(src_row, 1), :], dst_ref=mod_ref.at[pl.ds(dst_row, 1), :],
                                                send_sem=send_b.at[k], recv_sem=recv_b.at[k],
                                                device_id=to, device_id_type=MESH)

        own = pltpu.make_async_copy(modsh.at[pl.ds(my_i, 1), :], mod_ref.at[pl.ds(my_i, 1), :], local_s.at[1])
        own.start()
        sends = [mod_copy(k, _peer(me, k), _index(_peer(me, k)), my_i) for k in range(1, N_DEV)]
        for cp in sends:
            cp.start()
        for k in range(1, N_DEV):
            mod_copy(k, me, my_i, _index(_peer(me, k))).wait_recv()
        for cp in sends:
            cp.wait_send()
        own.wait()
        mod_ref[...] = mod_ref[...] + b_ref[...]

    vm = pl.BlockSpec(memory_space=pltpu.VMEM)
    return pl.pallas_call(
        body, name="ada_fwd", in_specs=[vm, vm, vm], out_specs=[vm, vm, vm],
        out_shape=[jax.ShapeDtypeStruct((N_DEV, lv), F32), jax.ShapeDtypeStruct((N_DEV, d), F32),
                   jax.ShapeDtypeStruct((N_DEV, c), F32)],
        scratch_shapes=[pltpu.VMEM((N_DEV, c), F32)] + [pltpu.SemaphoreType.DMA((N_DEV,))] * 4
        + [pltpu.SemaphoreType.DMA((2,))],
        compiler_params=pltpu.CompilerParams(vmem_limit_bytes=VMEM_LIMIT),
    )(vec, w_ada, b_ada_rows)


def _gather_small(vec):
    lv = vec.shape[1]

    def body(vec_ref, gath_ref, send_s, recv_s, local_s):
        me = _my_place()
        my_i = _index(me)

        def copy(k, to, src_row):
            row = gath_ref.at[pl.ds(src_row, 1), :]
            return pltpu.make_async_remote_copy(src_ref=row, dst_ref=row, send_sem=send_s.at[k], recv_sem=recv_s.at[k],
                                                device_id=to, device_id_type=MESH)

        own = pltpu.make_async_copy(vec_ref, gath_ref.at[pl.ds(my_i, 1), :], local_s)
        own.start()
        own.wait()
        sends = [copy(k, _peer(me, k), my_i) for k in range(1, N_DEV)]
        for cp in sends:
            cp.start()
        for k in range(1, N_DEV):
            copy(k, me, _index(_peer(me, k))).wait_recv()
        for cp in sends:
            cp.wait_send()

    vm = pl.BlockSpec(memory_space=pltpu.VMEM)
    return pl.pallas_call(
        body, name="gather_small", in_specs=[vm], out_specs=vm,
        out_shape=jax.ShapeDtypeStruct((N_DEV, lv), F32),
        scratch_shapes=[pltpu.SemaphoreType.DMA((N_DEV,))] * 2 + [pltpu.SemaphoreType.DMA],
        compiler_params=pltpu.CompilerParams(vmem_limit_bytes=VMEM_LIMIT),
    )(vec)


PER = N_DEV - 1


def _comm_scratch(n):
    return [pltpu.SemaphoreType.DMA((n * PER,)), pltpu.SemaphoreType.DMA((n * PER,)), pltpu.SemaphoreType.DMA((n,))]


def _gather_copies(ins, outs, send_s, recv_s, local_s):
    n = len(ins)
    me = _my_place()
    x, y, c = me
    sibling = (x, y, 1 - c)
    chips = [(1 - x, y), (x, 1 - y), (1 - x, 1 - y)]

    def copy(a, k, block, to, src=None):
        slot = outs[a].at[_index(block)]
        return pltpu.make_async_remote_copy(src_ref=slot if src is None else src, dst_ref=slot,
                                            send_sem=send_s.at[a * PER + k], recv_sem=recv_s.at[a * PER + k],
                                            device_id=to, device_id_type=MESH)

    mine = [pltpu.make_async_copy(ins[a], outs[a].at[_index(me)], local_s.at[a]) for a in range(n)]
    first = []
    for a in range(n):
        first.append(copy(a, 0, me, sibling, src=ins[a]))
        first += [copy(a, 1 + j, me, (*chip, c), src=ins[a]) for j, chip in enumerate(chips)]
    landed = [copy(a, 1 + j, (*chip, c), me) for j, chip in enumerate(chips) for a in range(n)]
    passed = [copy(a, 4 + j, (*chip, c), sibling) for j, chip in enumerate(chips) for a in range(n)]
    from_sibling = [copy(a, 0, sibling, me) for a in range(n)]
    from_sibling += [copy(a, 4 + j, (*chip, 1 - c), me) for a in range(n) for j, chip in enumerate(chips)]
    return mine, first, landed, passed, from_sibling


def _gather_start(*refs):
    mine, first, _, _, _ = _gather_copies(*refs)
    for cp in mine + first:
        cp.start()


def _gather_forward(*refs):
    _, _, landed, passed, _ = _gather_copies(*refs)
    for got, fwd in zip(landed, passed):
        got.wait_recv()
        fwd.start()


def _gather_finish(*refs):
    mine, first, _, passed, from_sibling = _gather_copies(*refs)
    for cp in from_sibling:
        cp.wait_recv()
    for cp in first + passed:
        cp.wait_send()
    for cp in mine:
        cp.wait()


def _exchange_copies(ins, outs, send_s, recv_s, local_s):
    n = len(ins)
    me = _my_place()
    my_i = _index(me)

    def copy(a, k, to, src_slot, dst_slot):
        return pltpu.make_async_remote_copy(src_ref=ins[a].at[src_slot], dst_ref=outs[a].at[dst_slot],
                                            send_sem=send_s.at[a * PER + k - 1], recv_sem=recv_s.at[a * PER + k - 1],
                                            device_id=to, device_id_type=MESH)

    mine = [pltpu.make_async_copy(ins[a].at[my_i], outs[a].at[my_i], local_s.at[a]) for a in range(n)]
    sends = [copy(a, k, _peer(me, k), _index(_peer(me, k)), my_i) for k in range(1, N_DEV) for a in range(n)]
    recvs = [copy(a, k, me, my_i, _index(_peer(me, k))) for k in range(1, N_DEV) for a in range(n)]
    return mine, sends, recvs


def _exchange_start(*refs):
    mine, sends, _ = _exchange_copies(*refs)
    for cp in mine + sends:
        cp.start()


def _exchange_finish(*refs):
    mine, sends, recvs = _exchange_copies(*refs)
    for cp in recvs:
        cp.wait_recv()
    for cp in sends:
        cp.wait_send()
    for cp in mine:
        cp.wait()


def _gathered_shapes(shards):
    return [jax.ShapeDtypeStruct((N_DEV,) + s.shape, s.dtype) for s in shards]


def _gather_weights(shards):
    n = len(shards)

    def body(*refs):
        parts = (refs[:n], refs[n:2 * n]) + tuple(refs[2 * n:])
        _gather_start(*parts)
        _gather_forward(*parts)
        _gather_finish(*parts)

    hbm = pl.BlockSpec(memory_space=pl.ANY)
    return pl.pallas_call(
        body, name="gather_weights", in_specs=[hbm] * n, out_specs=[hbm] * n,
        out_shape=_gathered_shapes(shards), scratch_shapes=_comm_scratch(n),
    )(*shards)


def _proj_layout(cw, ql, kl):
    lay = {"gb": 0, "gc": cw, "ci": 2 * cw, "ql": 3 * cw}
    assert lay["ql"] % ql == 0
    lay["kv"] = _roundup(lay["ql"] + ql, kl)
    lay["kr"] = lay["kv"] + kl
    lay["np"] = _roundup(lay["kr"] + LANE, 4 * LANE)
    return lay


def _chunks_cols(g):
    r, c8 = g.shape
    return jnp.transpose(g.reshape(r, N_DEV, c8 // N_DEV), (1, 0, 2))


def _from_col_shards(a):
    n, r, c = a.shape
    return jnp.transpose(a, (1, 0, 2)).reshape(r, n * c)


def kernel(x, c, positions, w_ada, b_ada, g_pre_mix, g_post_mix, w_in, g_q, w_uq, g_kv, w_ukv, conv_w_mix, conv_b_mix, w_o, g_pre_ffn, g_post_ffn, w_up, conv_w_ffn, conv_b_ffn, w_down, loss_target, m_w_ada, m_b_ada, m_g_pre_mix, m_g_post_mix, m_w_in, m_g_q, m_w_uq, m_g_kv, m_w_ukv, m_conv_w_mix, m_conv_b_mix, m_w_o, m_g_pre_ffn, m_g_post_ffn, m_w_up, m_conv_w_ffn, m_conv_b_ffn, m_w_down, v_w_ada, v_b_ada, v_g_pre_mix, v_g_post_mix, v_w_in, v_g_q, v_w_uq, v_g_kv, v_w_ukv, v_conv_w_mix, v_conv_b_mix, v_w_o, v_g_pre_ffn, v_g_post_ffn, v_w_up, v_conv_w_ffn, v_conv_b_ffn, v_w_down):
    s_len, d = x.shape[1], x.shape[2]
    ql, kl = w_uq.shape[1], w_ukv.shape[1]
    n_heads = w_ukv.shape[2] * N_DEV // (QK_NOPE + V_DIM)
    cw = conv_w_mix.shape[2] * N_DEV
    f2 = w_up.shape[2] * N_DEV
    ff = f2 // 2
    in_cols = w_in.shape[2] * N_DEV
    ada_c = w_ada.shape[2]
    cwm_c, cwf_c = conv_w_mix.shape[2], conv_w_ffn.shape[2]
    scale = 1.0 / math.sqrt(QK_NOPE + QK_ROPE)
    lay = _proj_layout(cw, ql, kl)
    n_pad = lay["np"]
    my_i = _index(_my_place())

    ts_row = _tile(s_len, PREF["row"], SUB)
    ts_conv = _tile(s_len, PREF["conv_rows"], SUB)
    tc_conv = _tile(cw, PREF["conv_cols"])
    tc_ffn = cwf_c
    ts_ffn = _tile(s_len, PREF["ffn_rows"], SUB)
    pair_order = [k // 2 + (k % 2) * (N_DEV // 2) for k in range(N_DEV)]
    pair_place = [pair_order.index(k) for k in range(N_DEV)]

    def paired(shards):
        return _from_col_shards(jnp.stack([shards[p] for p in pair_order]))

    def unpaired_chunks(g):
        ch = _chunks_cols(g)
        return jnp.stack([ch[p] for p in pair_place])
    ts_qkv = _tile(s_len, PREF["row"], SUB)
    t_attn = _tile(s_len, PREF["attn"])

    x2d, tgt = x[0], loss_target[0]

    vec = jnp.concatenate([c, conv_w_mix[0].reshape(1, -1), conv_w_ffn[0].reshape(1, -1)], axis=1)
    gath, cact, mod_rows = _ada_fwd(vec, w_ada[0], b_ada.reshape(N_DEV, ada_c))
    cwm_full = _from_col_shards(gath[:, d:d + 3 * cwm_c].reshape(N_DEV, 3, cwm_c))
    cwf_shards = gath[:, d + 3 * cwm_c:].reshape(N_DEV, 3, cwf_c)
    cwf_pair = paired(cwf_shards)
    cbf_pair = paired(jnp.transpose(conv_b_ffn.reshape(1, N_DEV, cwf_c), (1, 0, 2)))
    mod = mod_rows.reshape(1, N_DEV * ada_c)
    sh_m, sc_m, gt_m, sh_f, sc_f, gt_f = [mod[:, k * d:(k + 1) * d] for k in range(6)]

    g_in, g_uq, g_ukv = _gather_weights([w_in[0].astype(BF16), w_uq[0].astype(BF16), w_ukv[0].astype(BF16)])
    win = _from_col_shards(g_in)
    cut = np.cumsum([0, ql, kl, QK_ROPE, cw, cw, cw])
    part = [win[:, cut[k]:cut[k + 1]] for k in range(6)]

    def zcols(n):
        return jnp.zeros((d, n), BF16)

    win_p = jnp.concatenate([part[3], part[4], part[5], part[0], zcols(lay["kv"] - lay["ql"] - ql), part[1],
                             part[2], zcols(n_pad - lay["kr"] - QK_ROPE)], axis=1)
    wuq_p = jnp.pad(_from_col_shards(g_uq).reshape(ql, n_heads, QK_NOPE + QK_ROPE),
                    ((0, 0), (0, 0), (0, HEAD_W - QK_NOPE - QK_ROPE))).reshape(ql, n_heads * HEAD_W)
    wukv = _from_col_shards(g_ukv).reshape(kl, n_heads, QK_NOPE + V_DIM)
    wk = wukv[:, :, :QK_NOPE].reshape(kl, n_heads * QK_NOPE)
    wv = wukv[:, :, QK_NOPE:].reshape(kl, n_heads * V_DIM)

    inv_freq = 1.0 / (ROPE_THETA ** (jnp.arange(0, QK_ROPE, 2, dtype=F32) / QK_ROPE))
    inv_row = jnp.tile(inv_freq, LANE // (QK_ROPE // 2)).reshape(1, LANE)
    ctab, atab, btab = _rope_tables(positions.astype(F32).reshape(s_len, 1), inv_row, _tile(s_len, 1024, SUB))

    h1 = _modnorm_fwd(x2d, g_pre_mix, sc_m, sh_m, ts_row)
    proj = _matmul(h1, win_p, out_dtype=BF16, tm=1024, tn=1280, tk=2048, name="mm_proj")
    q, k, v, qn, kvn = _qkv_fwd(proj, lay, wuq_p, wk, wv, g_q, g_kv, ctab, atab, btab, n_heads, ts_qkv, scale * LOG2E)
    attn, lse_row, g_o, g_up, g_down = _flash_fwd(q, k, v, n_heads, t_attn,
                                              [w_o[0].astype(BF16), w_up[0].astype(BF16), w_down[0].astype(BF16)], d)
    wo, wdown = g_o.reshape(d, d), g_down.reshape(ff, d)

    def pair_shard(j):
        return j // 2 + (j % 2) * (N_DEV // 2)
    mixcat = _gconv_fwd(proj, lay, cwm_full, conv_b_mix, ts_conv, tc_conv, attn)
    mix = _matmul(mixcat, wo, out_dtype=F32, tm=512, tn=2048, tk=2048, name="mm_mix")
    x1, h2 = _post_mix_fwd(mix, x2d, gt_m, g_post_mix, g_pre_ffn, sc_f, sh_f, ts_row)
    up = _matmul(h2, g_up, out_dtype=BF16, tm=1024, tn=cwf_c, tk=2048, name="mm_up", b_shard_of=pair_shard)
    u, act = _act_fwd(up, cwf_pair, cbf_pair, ts_ffn, tc_ffn)
    y = _matmul(act, wdown, out_dtype=F32, tm=1024, tn=512, tk=ff, name="mm_down")

    dy, dx2, loss_row, d_gt_f, dg_post_ffn = _final_bwd(y, x1, tgt, gt_f, g_post_ffn, ts_row)
    gw_down = _matmul(act, dy, ta=True, out_dtype=BF16, tm=1408, tn=1024, tk=2048, name="mm_gw_down")
    d_act = _matmul(dy, wdown, tb=True, out_dtype=BF16, tm=1024, tn=1408, tk=2048, name="mm_d_act")
    d_up, dcw_pair, dcb_pair = _ffn_act_bwd(d_act, u, up, cwf_pair, ts_ffn, tc_ffn)
    dcb_ffn = _from_col_shards(unpaired_chunks(dcb_pair))
    gw_up, p_down = _matmul(h2, d_up, ta=True, out_dtype=BF16, tm=1024, tn=cwf_c, tk=2048, name="mm_gw_up",
                            exchange=[gw_down.reshape(N_DEV, ff // N_DEV, d)], out_shard_of=pair_shard)
    dh2 = _matmul_pair_shards(d_up, g_up, out_dtype=F32, tm=1024, tn=1024, name="mm_dh2")
    dx1, dmix, d_sh_f, d_sc_f, dg_pre_ffn, d_gt_m, dg_post_mix = _mid_bwd(
        dh2, x1, dx2, mix, g_pre_ffn, sc_f, gt_m, g_post_mix, ts_row)
    gw_o = _matmul(mixcat, dmix, ta=True, out_dtype=BF16, tm=1024, tn=1024, tk=2048, name="mm_gw_o")
    d_mixcat = _matmul(dmix, wo, tb=True, out_dtype=BF16, tm=1024, tn=1024, tk=2048, name="mm_d_mixcat")
    d_gb, d_gc, d_ci, dcw_mix, dcb_mix = _gconv_bwd(d_mixcat, proj, lay, cwm_full, conv_b_mix, ts_conv, tc_conv)
    delta = _delta(mixcat, d_mixcat, n_heads, _tile(s_len, 512, SUB))
    delta_row = delta[:, :, 0].reshape(n_heads, 1, s_len)
    dq, dk, dv, p_up, p_o = _flash_bwd(q, k, v, d_mixcat, lse_row, delta_row, n_heads, t_attn, scale,
                                       [gw_up, gw_o.reshape(N_DEV, d // N_DEV, d)])
    dq_r, dkn, d_tail, dg_q, dg_kv = _qkv_bwd(dq, dk, dv, proj, lay, wuq_p, wk, wv, g_q, g_kv, ctab, atab, btab,
                                              n_heads, ts_qkv)
    gw_uq_p = _matmul(qn, dq_r, ta=True, out_dtype=BF16, tm=768, tn=1024, tk=2048, name="mm_gw_uq")
    gw_k = _matmul(kvn, dkn, ta=True, out_dtype=BF16, tm=512, tn=1024, tk=2048, name="mm_gw_k")
    gw_v = _matmul(kvn, dv, ta=True, out_dtype=BF16, tm=512, tn=1024, tk=2048, name="mm_gw_v")
    d_proj = jnp.concatenate([d_gb, d_gc, d_ci, d_tail], axis=1)
    gw_uq = gw_uq_p.reshape(ql, n_heads, HEAD_W)[:, :, :QK_NOPE + QK_ROPE].reshape(ql, n_heads * (QK_NOPE + QK_ROPE))
    gw_ukv = jnp.concatenate([gw_k.reshape(kl, n_heads, QK_NOPE), gw_v.reshape(kl, n_heads, V_DIM)],
                             axis=2).reshape(kl, n_heads * (QK_NOPE + V_DIM))
    gw_in_p, p_uq, p_ukv, p_cwm, p_cwf = _matmul(
        h1, d_proj, ta=True, out_dtype=BF16, tm=1024, tn=1280, tk=2048, name="mm_gw_in",
        exchange=[_chunks_cols(gw_uq), _chunks_cols(gw_ukv), _chunks_cols(dcw_mix), unpaired_chunks(dcw_pair)])

    gw_in =jnp.concatenate([gw_in_p[:, lay["ql"]:lay["ql"] + ql], gw_in_p[:, lay["kv"]:lay["kv"] + kl],
                             gw_in_p[:, lay["kr"]:lay["kr"] + QK_ROPE], gw_in_p[:, :3 * cw]], axis=1)
    dh1, p_in = _matmul(d_proj, win_p, tb=True, out_dtype=F32, tm=512, tn=1024, tk=n_pad, name="mm_dh1",
                        exchange=[_chunks_cols(gw_in)])
    grad_x, d_sh_m, d_sc_m, dg_pre_mix = _first_bwd(dh1, x2d, dx1, g_pre_mix, sc_m, ts_row)

    dmod = jnp.concatenate([d_sh_m, d_sc_m, d_gt_m, d_sh_f, d_sc_f, d_gt_f], axis=1)
    small_g = [loss_row, dmod, dg_pre_mix, dg_post_mix, dg_q, dg_kv, dcb_mix, dg_pre_ffn, dg_post_ffn, dcb_ffn]
    v_all = _gather_small(jnp.concatenate(small_g, axis=1))

    small_names = ["b_ada", "g_pre_mix", "g_post_mix", "g_q", "g_kv", "conv_b_mix", "g_pre_ffn", "g_post_ffn",
                   "conv_b_ffn"]
    small_w = [b_ada, g_pre_mix, g_post_mix, g_q, g_kv, conv_b_mix, g_pre_ffn, g_post_ffn, conv_b_ffn]
    small_m = [m_b_ada, m_g_pre_mix, m_g_post_mix, m_g_q, m_g_kv, m_conv_b_mix, m_g_pre_ffn, m_g_post_ffn, m_conv_b_ffn]
    small_v = [v_b_ada, v_g_pre_mix, v_g_post_mix, v_g_q, v_g_kv, v_conv_b_mix, v_g_pre_ffn, v_g_post_ffn, v_conv_b_ffn]
    offs = np.cumsum([0] + [g.shape[1] for g in small_g])
    g_sum, small_out = _adam_small(v_all, [int(o) for o in offs[1:-1]], small_w, small_m, small_v)
    small = [dict(zip(small_names, kind)) for kind in small_out]
    loss = g_sum[0, 0]

    dmod_sh = lax.dynamic_slice(v_all, (0, int(offs[1]) + my_i * ada_c), (N_DEV, ada_c))
    cact_t = jnp.transpose(cact)
    big = dict(
        w_ada=_adam_ada(cact_t, dmod_sh, w_ada[0], m_w_ada[0], v_w_ada[0], _tile(d, 256, SUB)),
        w_in=_adam_parts(p_in, w_in[0], m_w_in[0], v_w_in[0], 256, "adam_w_in"),
        w_uq=_adam_parts(p_uq, w_uq[0], m_w_uq[0], v_w_uq[0], 256, "adam_w_uq"),
        w_ukv=_adam_parts(p_ukv, w_ukv[0], m_w_ukv[0], v_w_ukv[0], 256, "adam_w_ukv"),
        w_o=_adam_parts(p_o, w_o[0], m_w_o[0], v_w_o[0], 128, "adam_w_o"),
        w_up=_adam_parts(p_up, w_up[0], m_w_up[0], v_w_up[0], 256, "adam_w_up"),
        w_down=_adam_parts(p_down, w_down[0], m_w_down[0], v_w_down[0], 176, "adam_w_down"),
        conv_w_mix=_adam_parts(p_cwm, conv_w_mix[0], m_conv_w_mix[0], v_conv_w_mix[0], 8, "adam_cw_mix"),
        conv_w_ffn=_adam_parts(p_cwf, conv_w_ffn[0], m_conv_w_ffn[0], v_conv_w_ffn[0], 8, "adam_cw_ffn"),
    )

    names = ["w_ada", "b_ada", "g_pre_mix", "g_post_mix", "w_in", "g_q", "w_uq", "g_kv", "w_ukv", "conv_w_mix",
             "conv_b_mix", "w_o", "g_pre_ffn", "g_post_ffn", "w_up", "conv_w_ffn", "conv_b_ffn", "w_down"]
    outs = [loss, grad_x[None]]
    for kind in range(4):
        for nm in names:
            outs.append(big[nm][kind][None] if nm in big else small[kind][nm])
    return tuple(outs)
```

```python
import math

import numpy as np
import jax
import jax.numpy as jnp
from jax import lax
from jax.experimental import pallas as pl
from jax.experimental.pallas import tpu as pltpu

F32 = jnp.float32
BF16 = jnp.bfloat16
N_DEV = 8
MESH = pl.DeviceIdType.MESH

QK_NOPE = 128
QK_ROPE = 64
V_DIM = 128
HEAD_W = 256
LANE = 128
SUB = 8
HALO = 16
STRIP_ROWS = 64
RMS_EPS = 1e-6
ROPE_THETA = 10000.0
ADAM_LR = 0.001
ADAM_B1 = 0.9
ADAM_B2 = 0.999
ADAM_EPS = 1e-08
ADAM_WD = 0.01
ADAM_STEP = 10
NEG = -1e30
LOG2E = 1.4426950408889634
LN2 = 0.6931471805599453
VMEM_LIMIT = 56 * 1024 * 1024

PREF = {"row": 256, "conv_rows": 512, "conv_cols": 512, "ffn_rows": 256, "attn": 1024, "attn_chunk": 1024}

NT_DIMS = (((1,), (1,)), ((), ()))
TN_DIMS = (((0,), (0,)), ((), ()))


def _cparams(*sem):
    return pltpu.CompilerParams(dimension_semantics=sem if sem else None, vmem_limit_bytes=VMEM_LIMIT)


def _tile(n, pref, unit=LANE):
    if n <= pref:
        return n
    t = (pref // unit) * unit
    while t >= unit:
        if n % t == 0:
            return t
        t -= unit
    return n


def _roundup(n, m):
    return (n + m - 1) // m * m


def _rsq(x):
    return lax.rsqrt(jnp.mean(x * x, axis=-1, keepdims=True) + RMS_EPS)


def _colsum(x):
    return jnp.sum(x, axis=0, keepdims=True)


def _sigmoid(x):
    return 1.0 / (1.0 + jnp.exp(-x))


def _matmul(a, b, *, ta=False, tb=False, out_dtype, tm, tn, tk, name, exchange=None, gather=None, b_shard_of=None,
            out_shard_of=None):
    m_dim, k_dim = (a.shape[1], a.shape[0]) if ta else a.shape
    if b_shard_of is not None:
        n_dim, tn = b.shape[0] * b.shape[2], b.shape[2]
    else:
        n_dim = b.shape[0] if tb else b.shape[1]
    if out_shard_of is not None:
        tn = n_dim // N_DEV
    tm, tn, tk = _tile(m_dim, tm), _tile(n_dim, tn), _tile(k_dim, tk)
    gi, gj, nk = m_dim // tm, n_dim // tn, k_dim // tk
    dims = (((0 if ta else 1,), (1 if tb else 0,)), ((), ()))
    chunks = list(exchange or gather or [])
    nx = len(chunks)
    comm_start, comm_finish = (_gather_start, _gather_finish) if gather else (_exchange_start, _exchange_finish)

    def body(*refs):
        a_ref, b_ref = refs[:2]
        xin, o_ref, xout = refs[2:2 + nx], refs[2 + nx], refs[3 + nx:3 + 2 * nx]
        scratch = refs[3 + 2 * nx:]
        sems = scratch[1:] if nk > 1 else scratch
        i, j, k = pl.program_id(0), pl.program_id(1), pl.program_id(2)
        if nx:
            @pl.when((i == 0) & (j == 0) & (k == 0))
            def _():
                comm_start(xin, xout, *sems)

        if gather:
            @pl.when((i == gi // 2) & (j == 0) & (k == 0))
            def _():
                _gather_forward(xin, xout, *sems)

        part = lax.dot_general(a_ref[...], b_ref[...], dims, preferred_element_type=F32)
        if nk == 1:
            o_ref[...] = part.astype(o_ref.dtype)
        else:
            acc_ref = scratch[0]

            @pl.when(k == 0)
            def _():
                acc_ref[...] = part

            @pl.when(k > 0)
            def _():
                acc_ref[...] += part

            @pl.when(k == nk - 1)
            def _():
                o_ref[...] = acc_ref[...].astype(o_ref.dtype)

        if nx:
            @pl.when((i == gi - 1) & (j == gj - 1) & (k == nk - 1))
            def _():
                comm_finish(xin, xout, *sems)

    a_spec = pl.BlockSpec((tk, tm), lambda i, j, k: (k, i)) if ta else pl.BlockSpec((tm, tk), lambda i, j, k: (i, k))
    b_spec = pl.BlockSpec((tn, tk), lambda i, j, k: (j, k)) if tb else pl.BlockSpec((tk, tn), lambda i, j, k: (k, j))
    if b_shard_of is not None:
        b_spec = pl.BlockSpec((None, tk, tn), lambda i, j, k: (b_shard_of(j), k, 0))
    o_spec, o_shape = pl.BlockSpec((tm, tn), lambda i, j, k: (i, j)), (m_dim, n_dim)
    if out_shard_of is not None:
        o_spec, o_shape = pl.BlockSpec((None, tm, tn), lambda i, j, k: (out_shard_of(j), i, 0)), (N_DEV, m_dim, tn)
    hbm = pl.BlockSpec(memory_space=pl.ANY)
    out = pl.pallas_call(
        body,
        name=name,
        grid=(gi, gj, nk),
        in_specs=[a_spec, b_spec] + [hbm] * nx,
        out_specs=[o_spec] + [hbm] * nx,
        out_shape=[jax.ShapeDtypeStruct(o_shape, out_dtype)]
        + (_gathered_shapes(chunks) if gather else [jax.ShapeDtypeStruct(c.shape, c.dtype) for c in chunks]),
        scratch_shapes=([pltpu.VMEM((tm, tn), F32)] if nk > 1 else []) + (_comm_scratch(nx) if nx else []),
        compiler_params=_cparams(*(("arbitrary",) * 3 if nx else ("parallel", "parallel", "arbitrary"))),
    )(a, b, *chunks)
    return out if nx else out[0]


def _matmul_pair_shards(a, shards, *, out_dtype, tm, tn, name):
    m_dim = a.shape[0]
    n_sh, n_dim, c = shards.shape
    half = n_sh // 2
    tm, tn = _tile(m_dim, tm), _tile(n_dim, tn)

    def body(a_ref, b0_ref, b1_ref, o_ref, acc_ref):
        k = pl.program_id(2)
        part = (lax.dot_general(a_ref[:, :c], b0_ref[...], NT_DIMS, preferred_element_type=F32)
                + lax.dot_general(a_ref[:, c:], b1_ref[...], NT_DIMS, preferred_element_type=F32))

        @pl.when(k == 0)
        def _():
            acc_ref[...] = part

        @pl.when(k > 0)
        def _():
            acc_ref[...] += part

        @pl.when(k == half - 1)
        def _():
            o_ref[...] = acc_ref[...].astype(o_ref.dtype)

    return pl.pallas_call(
        body, name=name, grid=(m_dim // tm, n_dim // tn, half),
        in_specs=[pl.BlockSpec((tm, 2 * c), lambda i, j, k: (i, k)),
                  pl.BlockSpec((None, tn, c), lambda i, j, k: (k, j, 0)),
                  pl.BlockSpec((None, tn, c), lambda i, j, k: (k + half, j, 0))],
        out_specs=pl.BlockSpec((tm, tn), lambda i, j, k: (i, j)),
        out_shape=jax.ShapeDtypeStruct((m_dim, n_dim), out_dtype),
        scratch_shapes=[pltpu.VMEM((tm, tn), F32)],
        compiler_params=_cparams("parallel", "parallel", "arbitrary"),
    )(a, shards, shards)


def _shift_down(x, halo, n):
    r = pltpu.roll(x, n, 0)
    hr = pltpu.roll(halo, n, 0)
    row = lax.broadcasted_iota(jnp.int32, halo.shape, 0)
    top = jnp.where(row < n, hr, r[:SUB])
    return jnp.concatenate([top, r[SUB:]], axis=0)


def _shift_up(x, halo, n):
    ts = x.shape[0]
    r = pltpu.roll(x, ts - n, 0)
    hr = pltpu.roll(halo, SUB - n, 0)
    row = lax.broadcasted_iota(jnp.int32, halo.shape, 0)
    bot = jnp.where(row >= SUB - n, hr, r[ts - SUB:])
    return jnp.concatenate([r[:ts - SUB], bot], axis=0)


def _conv3(x, halo, w_ref, b_ref):
    return _shift_down(x, halo, 2) * w_ref[0:1, :] + _shift_down(x, halo, 1) * w_ref[1:2, :] + x * w_ref[2:3, :] + b_ref[...]


def _prev_halo(ts, col):
    return lambda j, i: (jnp.maximum(i * (ts // SUB) - 1, 0), col(j))


def _next_halo(ts, n_rows, col):
    return lambda j, i: (jnp.minimum((i + 1) * (ts // SUB), n_rows // SUB - 1), col(j))


def _modnorm_fwd(x, g, sc, sh, ts):
    s_len, d = x.shape

    def body(x_ref, g_ref, sc_ref, sh_ref, h_ref):
        xv = x_ref[...]
        h_ref[...] = ((xv * _rsq(xv) * g_ref[...]) * (1.0 + sc_ref[...]) + sh_ref[...]).astype(BF16)

    vec = pl.BlockSpec((1, d), lambda i: (0, 0))
    return pl.pallas_call(
        body, name="modnorm_fwd", grid=(s_len // ts,),
        in_specs=[pl.BlockSpec((ts, d), lambda i: (i, 0)), vec, vec, vec],
        out_specs=pl.BlockSpec((ts, d), lambda i: (i, 0)),
        out_shape=jax.ShapeDtypeStruct((s_len, d), BF16),
        compiler_params=_cparams("parallel"),
    )(x, g, sc, sh)


def _rope_tables(pos_col, inv_freq_row, ts):
    s_len = pos_col.shape[0]
    half = QK_ROPE // 2

    def body(p_ref, f_ref, c_ref, a_ref, b_ref):
        ang = p_ref[...] * f_ref[...]
        lane = lax.broadcasted_iota(jnp.int32, ang.shape, 1)
        cos, sin = jnp.cos(ang), jnp.sin(ang)
        c_ref[...] = jnp.where(lane < 2 * half, cos, 0.0)
        a_ref[...] = jnp.where(lane < half, -sin, 0.0)
        b_ref[...] = jnp.where((lane >= half) & (lane < 2 * half), sin, 0.0)

    out = jax.ShapeDtypeStruct((s_len, LANE), F32)
    blk = pl.BlockSpec((ts, LANE), lambda i: (i, 0))
    return pl.pallas_call(
        body, name="rope_tables", grid=(s_len // ts,),
        in_specs=[pl.BlockSpec((ts, 1), lambda i: (i, 0)), pl.BlockSpec((1, LANE), lambda i: (0, 0))],
        out_specs=[blk, blk, blk], out_shape=[out, out, out],
        compiler_params=_cparams("parallel"),
    )(pos_col, inv_freq_row)


def _rope(seg, c, a, b):
    return seg * c + pltpu.roll(seg, LANE - QK_ROPE // 2, 1) * a + pltpu.roll(seg, QK_ROPE // 2, 1) * b


def _rope_t(seg, c, a, b):
    return seg * c - pltpu.roll(seg, LANE - QK_ROPE // 2, 1) * a - pltpu.roll(seg, QK_ROPE // 2, 1) * b


def _qkv_fwd(proj, lay, wuq, wk, wv, g_q, g_kv, ctab, atab, btab, n_heads, ts, scale):
    s_len = proj.shape[0]
    ql_w, kl_w = wuq.shape[0], wk.shape[0]

    def body(ql_ref, kl_ref, kr_ref, wuq_ref, wk_ref, wv_ref, gq_ref, gkv_ref, c_ref, a_ref, b_ref,
             q_out, k_out, v_out, qn_out, kvn_out):
        c, a, b = c_ref[...], a_ref[...], b_ref[...]
        ql = ql_ref[...].astype(F32)
        qn = (ql * _rsq(ql) * gq_ref[...]).astype(BF16)
        qn_out[...] = qn
        q = jnp.dot(qn, wuq_ref[...], preferred_element_type=F32)
        kl = kl_ref[...].astype(F32)
        kvn = (kl * _rsq(kl) * gkv_ref[...]).astype(BF16)
        kvn_out[...] = kvn
        kn = jnp.dot(kvn, wk_ref[...], preferred_element_type=F32)
        v_out[...] = jnp.dot(kvn, wv_ref[...], preferred_element_type=F32).astype(BF16)
        kr = _rope(kr_ref[...].astype(F32), c, a, b).astype(BF16)
        for h in range(n_heads):
            o = h * HEAD_W
            q_out[:, o:o + QK_NOPE] = (q[:, o:o + QK_NOPE] * scale).astype(BF16)
            q_out[:, o + QK_NOPE:o + HEAD_W] = (_rope(q[:, o + QK_NOPE:o + HEAD_W], c, a, b) * scale).astype(BF16)
            k_out[:, o:o + QK_NOPE] = kn[:, h * QK_NOPE:(h + 1) * QK_NOPE].astype(BF16)
            k_out[:, o + QK_NOPE:o + HEAD_W] = kr

    def full(arr):
        return pl.BlockSpec(arr.shape, lambda i: (0, 0))

    tab = pl.BlockSpec((ts, LANE), lambda i: (i, 0))
    hw, hv = n_heads * HEAD_W, n_heads * V_DIM
    return pl.pallas_call(
        body, name="qkv_fwd", grid=(s_len // ts,),
        in_specs=[pl.BlockSpec((ts, ql_w), lambda i: (i, lay["ql"] // ql_w)),
                  pl.BlockSpec((ts, kl_w), lambda i: (i, lay["kv"] // kl_w)),
                  pl.BlockSpec((ts, LANE), lambda i: (i, lay["kr"] // LANE)),
                  full(wuq), full(wk), full(wv), full(g_q), full(g_kv), tab, tab, tab],
        out_specs=[pl.BlockSpec((ts, hw), lambda i: (i, 0)), pl.BlockSpec((ts, hw), lambda i: (i, 0)),
                   pl.BlockSpec((ts, hv), lambda i: (i, 0)), pl.BlockSpec((ts, ql_w), lambda i: (i, 0)),
                   pl.BlockSpec((ts, kl_w), lambda i: (i, 0))],
        out_shape=[jax.ShapeDtypeStruct((s_len, hw), BF16), jax.ShapeDtypeStruct((s_len, hw), BF16),
                   jax.ShapeDtypeStruct((s_len, hv), BF16), jax.ShapeDtypeStruct((s_len, ql_w), BF16),
                   jax.ShapeDtypeStruct((s_len, kl_w), BF16)],
        compiler_params=_cparams("parallel"),
    )(proj, proj, proj, wuq, wk, wv, g_q, g_kv, ctab, atab, btab)


def _flash_fwd(q, k, v, n_heads, t, gather, out_cols):
    ng = len(gather)
    ck = _tile(t, PREF["attn_chunk"])
    s_len = q.shape[0]
    nb = s_len // t
    pairs = [(i, j) for i in range(nb) for j in range(i + 1)]
    itab = jnp.asarray(np.array([p[0] for p in pairs], np.int32))
    jtab = jnp.asarray(np.array([p[1] for p in pairs], np.int32))

    n_steps = len(pairs)

    def body(it_ref, jt_ref, q_ref, k_ref, v_ref, *rest):
        gin, (o_ref, lse_ref), gout = rest[:ng], rest[ng:ng + 2], rest[ng + 2:2 * ng + 2]
        m_sc, l_sc, acc_sc = rest[2 * ng + 2:2 * ng + 5]
        sems = rest[2 * ng + 5:]
        head, step_id = pl.program_id(0), pl.program_id(1)
        i, j = it_ref[step_id], jt_ref[step_id]

        @pl.when((head == 0) & (step_id == 0))
        def _():
            _gather_start(gin, gout, *sems)

        @pl.when((head == (5 * n_heads) // 8) & (step_id == 0))
        def _():
            _gather_forward(gin, gout, *sems)

        @pl.when(j == 0)
        def _():
            m_sc[...] = jnp.full(m_sc.shape, NEG, F32)
            l_sc[...] = jnp.zeros(l_sc.shape, F32)
            acc_sc[...] = jnp.zeros(acc_sc.shape, F32)

        def step(diag):
            for c in range(t // ck):
                q0 = c * ck if diag else 0
                qs, ks = slice(q0, t), slice(c * ck, (c + 1) * ck)
                s_t = lax.dot_general(k_ref[ks, :], q_ref[qs, :], NT_DIMS, preferred_element_type=F32)
                if diag:
                    krow = lax.broadcasted_iota(jnp.int32, s_t.shape, 0)
                    qcol = lax.broadcasted_iota(jnp.int32, s_t.shape, 1)
                    s_t = jnp.where(krow <= qcol, s_t, NEG)
                m_prev = m_sc[:, qs]
                m_new = jnp.maximum(m_prev, jnp.max(s_t, axis=0, keepdims=True))
                alpha = jnp.exp2(m_prev - m_new)
                p_t = jnp.exp2(s_t - m_new)
                l_sc[:, qs] = alpha * l_sc[:, qs] + jnp.sum(p_t, axis=0, keepdims=True)
                acc_sc[:, qs] = acc_sc[:, qs] * alpha + lax.dot_general(v_ref[ks, :], p_t.astype(BF16), TN_DIMS,
                                                                        preferred_element_type=F32)
                m_sc[:, qs] = m_new

        @pl.when(j < i)
        def _():
            step(False)

        @pl.when(j == i)
        def _():
            step(True)
            l = l_sc[...]
            o_ref[...] = jnp.transpose(acc_sc[...] / l).astype(BF16)
            lse_ref[0] = m_sc[...] + jnp.log(l) * LOG2E

        @pl.when((head == n_heads - 1) & (step_id == n_steps - 1))
        def _():
            _gather_finish(gin, gout, *sems)

    hbm = pl.BlockSpec(memory_space=pl.ANY)
    grid_spec = pltpu.PrefetchScalarGridSpec(
        num_scalar_prefetch=2, grid=(n_heads, n_steps),
        in_specs=[pl.BlockSpec((t, HEAD_W), lambda h, s, it, jt: (it[s], h)),
                  pl.BlockSpec((t, HEAD_W), lambda h, s, it, jt: (jt[s], h)),
                  pl.BlockSpec((t, V_DIM), lambda h, s, it, jt: (jt[s], h))] + [hbm] * ng,
        out_specs=[pl.BlockSpec((t, V_DIM), lambda h, s, it, jt: (it[s], h)),
                   pl.BlockSpec((1, 1, t), lambda h, s, it, jt: (h, 0, it[s]))] + [hbm] * ng,
        scratch_shapes=[pltpu.VMEM((1, t), F32), pltpu.VMEM((1, t), F32), pltpu.VMEM((V_DIM, t), F32)]
        + _comm_scratch(ng),
    )
    return pl.pallas_call(
        body, name="flash_fwd", grid_spec=grid_spec,
        out_shape=[jax.ShapeDtypeStruct((s_len, out_cols), BF16),
                   jax.ShapeDtypeStruct((n_heads, 1, s_len), F32)] + _gathered_shapes(gather),
        compiler_params=_cparams("arbitrary", "arbitrary"),
    )(itab, jtab, q, k, v, *gather)


def _gconv_fwd(proj, lay, w, b, ts, tc, mixcat):
    s_len = proj.shape[0]
    cw = w.shape[1]
    nj = cw // tc
    out_off = mixcat.shape[1] - cw

    def body(gb_ref, gc_ref, ci_ref, gch_ref, cih_ref, w_ref, b_ref, mix_in, o_ref):
        i = pl.program_id(1)
        p = gc_ref[...].astype(F32) * ci_ref[...].astype(F32)
        ph = jnp.where(i > 0, gch_ref[...].astype(F32) * cih_ref[...].astype(F32), 0.0)
        o_ref[...] = (gb_ref[...].astype(F32) * _conv3(p, ph, w_ref, b_ref)).astype(BF16)

    def blk(off):
        return pl.BlockSpec((ts, tc), lambda j, i: (i, off // tc + j))

    def halo(off):
        return pl.BlockSpec((SUB, tc), _prev_halo(ts, lambda j: off // tc + j))

    return pl.pallas_call(
        body, name="gconv_fwd", grid=(nj, s_len // ts),
        in_specs=[blk(lay["gb"]), blk(lay["gc"]), blk(lay["ci"]), halo(lay["gc"]), halo(lay["ci"]),
                  pl.BlockSpec((3, tc), lambda j, i: (0, j)), pl.BlockSpec((1, tc), lambda j, i: (0, j)),
                  pl.BlockSpec(memory_space=pl.ANY)],
        out_specs=pl.BlockSpec((ts, tc), lambda j, i: (i, out_off // tc + j)),
        out_shape=jax.ShapeDtypeStruct(mixcat.shape, BF16),
        input_output_aliases={7: 0},
        compiler_params=_cparams("parallel", "parallel"),
    )(proj, proj, proj, proj, proj, w, b, mixcat)


def _post_mix_fwd(mix, x, gt, g_post, g_pre, sc, sh, ts):
    s_len, d = x.shape

    def body(mix_ref, x_ref, gt_ref, gp_ref, g2_ref, sc_ref, sh_ref, x1_ref, h2_ref):
        mv = mix_ref[...]
        x1 = x_ref[...] + gt_ref[...] * (mv * _rsq(mv) * gp_ref[...])
        x1_ref[...] = x1
        h2_ref[...] = ((x1 * _rsq(x1) * g2_ref[...]) * (1.0 + sc_ref[...]) + sh_ref[...]).astype(BF16)

    vec = pl.BlockSpec((1, d), lambda i: (0, 0))
    row = pl.BlockSpec((ts, d), lambda i: (i, 0))
    return pl.pallas_call(
        body, name="post_mix_fwd", grid=(s_len // ts,),
        in_specs=[row, row, vec, vec, vec, vec, vec], out_specs=[row, row],
        out_shape=[jax.ShapeDtypeStruct((s_len, d), F32), jax.ShapeDtypeStruct((s_len, d), BF16)],
        compiler_params=_cparams("parallel"),
    )(mix, x, gt, g_post, g_pre, sc, sh)


def _act_fwd(up, w, b, ts, tc):
    s_len, f2 = up.shape
    nh = f2 // 2 // tc
    rs_n = _tile(ts, STRIP_ROWS, HALO)

    def body(up_ref, uph_ref, w_ref, b_ref, u_ref, o_ref):
        i = pl.program_id(1)

        def conv_strip(r0, lanes):
            if r0 == 0:
                top = jnp.where(i > 0, uph_ref[:, lanes].astype(F32), 0.0)
                xe = jnp.concatenate([top, up_ref[0:rs_n, lanes].astype(F32)], axis=0)
            else:
                xe = up_ref[r0 - HALO:r0 + rs_n, lanes].astype(F32)
            u = (pltpu.roll(xe, 2, 0)[HALO:] * w_ref[0:1, lanes] + pltpu.roll(xe, 1, 0)[HALO:] * w_ref[1:2, lanes]
                 + xe[HALO:] * w_ref[2:3, lanes] + b_ref[:, lanes])
            u_ref[r0:r0 + rs_n, lanes] = u.astype(BF16)
            return u

        for r0 in range(0, ts, rs_n):
            for c0 in range(0, tc, LANE):
                ua = conv_strip(r0, slice(c0, c0 + LANE))
                ug = conv_strip(r0, slice(tc + c0, tc + c0 + LANE))
                o_ref[r0:r0 + rs_n, c0:c0 + LANE] = (ug * _sigmoid(ug) * ua).astype(BF16)

    pair = pl.BlockSpec((ts, 2 * tc), lambda j, i: (i, j))
    return pl.pallas_call(
        body, name="act_fwd", grid=(nh, s_len // ts),
        in_specs=[pair, pl.BlockSpec((HALO, 2 * tc), lambda j, i: (jnp.maximum(i * (ts // HALO) - 1, 0), j)),
                  pl.BlockSpec((3, 2 * tc), lambda j, i: (0, j)), pl.BlockSpec((1, 2 * tc), lambda j, i: (0, j))],
        out_specs=[pair, pl.BlockSpec((ts, tc), lambda j, i: (i, j))],
        out_shape=[jax.ShapeDtypeStruct((s_len, f2), BF16), jax.ShapeDtypeStruct((s_len, f2 // 2), BF16)],
        compiler_params=_cparams("parallel", "parallel"),
    )(up, up, w, b)


def _final_bwd(y, x1, tgt, gt, g_post, ts):
    s_len, d = y.shape

    def body(y_ref, x1_ref, t_ref, gt_ref, g_ref, dy_ref, dx2_ref, loss_ref, dgt_ref, dg_ref):
        i = pl.program_id(0)
        yv = y_ref[...]
        r = _rsq(yv)
        yh = yv * r
        n = yh * g_ref[...]
        e = x1_ref[...] + gt_ref[...] * n - t_ref[...]
        loss = 0.5 * jnp.sum(jnp.mean(e * e, axis=-1, keepdims=True), axis=0, keepdims=True)
        dx2 = e * (1.0 / d)
        dx2_ref[...] = dx2
        dn = dx2 * gt_ref[...]
        dyh = dn * g_ref[...]
        dy_ref[...] = (r * (dyh - yh * jnp.mean(dyh * yh, axis=-1, keepdims=True))).astype(BF16)

        @pl.when(i == 0)
        def _():
            loss_ref[...] = jnp.zeros(loss_ref.shape, F32)
            dgt_ref[...] = jnp.zeros(dgt_ref.shape, F32)
            dg_ref[...] = jnp.zeros(dg_ref.shape, F32)

        loss_ref[...] += jnp.broadcast_to(loss, loss_ref.shape)
        dgt_ref[...] += _colsum(dx2 * n)
        dg_ref[...] += _colsum(dn * yh)

    vec = pl.BlockSpec((1, d), lambda i: (0, 0))
    row = pl.BlockSpec((ts, d), lambda i: (i, 0))
    vshape = jax.ShapeDtypeStruct((1, d), F32)
    return pl.pallas_call(
        body, name="final_bwd", grid=(s_len // ts,),
        in_specs=[row, row, row, vec, vec],
        out_specs=[row, row, pl.BlockSpec((1, LANE), lambda i: (0, 0)), vec, vec],
        out_shape=[jax.ShapeDtypeStruct((s_len, d), BF16), jax.ShapeDtypeStruct((s_len, d), F32),
                   jax.ShapeDtypeStruct((1, LANE), F32), vshape, vshape],
        compiler_params=_cparams("arbitrary"),
    )(y, x1, tgt, gt, g_post)


def _ffn_act_bwd(d_act, u, up, w, ts, tc):
    s_len, f2 = up.shape
    nh = f2 // 2 // tc
    n_i = s_len // ts
    rs_n = _tile(ts, STRIP_ROWS, HALO)
    ext = rs_n + HALO

    def body(d_ref, dh_ref, u_ref, uh_ref, x_ref, w_ref, dx_ref, dw_ref, db_ref, acc):
        i = pl.program_id(1)
        acc[...] = jnp.zeros(acc.shape, F32)

        def below(ref, halo_ref, r0, lanes):
            if r0 + ext <= ts:
                return ref[r0:r0 + ext, lanes].astype(F32)
            bot = jnp.where(i < n_i - 1, halo_ref[:, lanes].astype(F32), 0.0)
            return jnp.concatenate([ref[r0:ts, lanes].astype(F32), bot], axis=0)

        def fold8(v):
            return jnp.sum(v.reshape(v.shape[0] // SUB, SUB, v.shape[1]), axis=0)

        def conv_bwd_strip(du, r0, lanes):
            du1, du2 = pltpu.roll(du, ext - 1, 0)[:rs_n], pltpu.roll(du, ext - 2, 0)[:rs_n]
            du0 = du[:rs_n]
            dx_ref[r0:r0 + rs_n, lanes] = (du0 * w_ref[2:3, lanes] + du1 * w_ref[1:2, lanes]
                                           + du2 * w_ref[0:1, lanes]).astype(BF16)
            xv = x_ref[r0:r0 + rs_n, lanes].astype(F32)
            acc[0, :, lanes] += fold8(du2 * xv)
            acc[1, :, lanes] += fold8(du1 * xv)
            acc[2, :, lanes] += fold8(du0 * xv)
            acc[3, :, lanes] += fold8(du0)

        for r0 in range(0, ts, rs_n):
            for c0 in range(0, tc, LANE):
                la, lg = slice(c0, c0 + LANE), slice(tc + c0, tc + c0 + LANE)
                dv = below(d_ref, dh_ref, r0, la)
                ua, ug = below(u_ref, uh_ref, r0, la), below(u_ref, uh_ref, r0, lg)
                sg = _sigmoid(ug)
                conv_bwd_strip(dv * (ug * sg), r0, la)
                conv_bwd_strip(dv * ua * (sg * (1.0 + ug * (1.0 - sg))), r0, lg)

        @pl.when(i == 0)
        def _():
            dw_ref[...] = jnp.zeros(dw_ref.shape, F32)
            db_ref[...] = jnp.zeros(db_ref.shape, F32)

        dw_ref[...] += jnp.concatenate([_colsum(acc[k]) for k in range(3)], axis=0)
        db_ref[...] += _colsum(acc[3])

    pair = pl.BlockSpec((ts, 2 * tc), lambda j, i: (i, j))

    def nxt(j, i):
        return (jnp.minimum((i + 1) * (ts // HALO), s_len // HALO - 1), j)

    return pl.pallas_call(
        body, name="ffn_act_bwd", grid=(nh, n_i),
        in_specs=[pl.BlockSpec((ts, tc), lambda j, i: (i, j)), pl.BlockSpec((HALO, tc), nxt),
                  pair, pl.BlockSpec((HALO, 2 * tc), nxt), pair, pl.BlockSpec((3, 2 * tc), lambda j, i: (0, j))],
        out_specs=[pair, pl.BlockSpec((3, 2 * tc), lambda j, i: (0, j)), pl.BlockSpec((1, 2 * tc), lambda j, i: (0, j))],
        out_shape=[jax.ShapeDtypeStruct((s_len, f2), BF16), jax.ShapeDtypeStruct((3, f2), F32),
                   jax.ShapeDtypeStruct((1, f2), F32)],
        scratch_shapes=[pltpu.VMEM((4, SUB, 2 * tc), F32)],
        compiler_params=_cparams("parallel", "arbitrary"),
    )(d_act, d_act, u, u, up, w)


def _mid_bwd(dh2, x1, dx2, mix, g_pre, sc, gt_m, g_post, ts):
    s_len, d = x1.shape

    def body(dh_ref, x1_ref, dx2_ref, mix_ref, g_ref, sc_ref, gt_ref, gp_ref,
             dx1_ref, dmix_ref, dsh_ref, dsc_ref, dg_ref, dgt_ref, dgp_ref):
        i = pl.program_id(0)
        dh = dh_ref[...]
        x1 = x1_ref[...]
        r1 = _rsq(x1)
        xh = x1 * r1
        dxh = dh * (1.0 + sc_ref[...]) * g_ref[...]
        dx1 = dx2_ref[...] + r1 * (dxh - xh * jnp.mean(dxh * xh, axis=-1, keepdims=True))
        dx1_ref[...] = dx1
        mv = mix_ref[...]
        rm = _rsq(mv)
        mh = mv * rm
        dn = dx1 * gt_ref[...]
        dmh = dn * gp_ref[...]
        dmix_ref[...] = (rm * (dmh - mh * jnp.mean(dmh * mh, axis=-1, keepdims=True))).astype(BF16)

        @pl.when(i == 0)
        def _():
            for ref in (dsh_ref, dsc_ref, dg_ref, dgt_ref, dgp_ref):
                ref[...] = jnp.zeros(ref.shape, F32)

        dsh_ref[...] += _colsum(dh)
        dsc_ref[...] += _colsum(dh * (xh * g_ref[...]))
        dg_ref[...] += _colsum(dh * (1.0 + sc_ref[...]) * xh)
        dgt_ref[...] += _colsum(dx1 * (mh * gp_ref[...]))
        dgp_ref[...] += _colsum(dn * mh)

    vec = pl.BlockSpec((1, d), lambda i: (0, 0))
    row = pl.BlockSpec((ts, d), lambda i: (i, 0))
    vshape = jax.ShapeDtypeStruct((1, d), F32)
    return pl.pallas_call(
        body, name="mid_bwd", grid=(s_len // ts,),
        in_specs=[row, row, row, row, vec, vec, vec, vec],
        out_specs=[row, row, vec, vec, vec, vec, vec],
        out_shape=[jax.ShapeDtypeStruct((s_len, d), F32), jax.ShapeDtypeStruct((s_len, d), BF16)] + [vshape] * 5,
        compiler_params=_cparams("arbitrary"),
    )(dh2, x1, dx2, mix, g_pre, sc, gt_m, g_post)


def _first_bwd(dh1, x, dx1, g_pre, sc, ts):
    s_len, d = x.shape

    def body(dh_ref, x_ref, dx1_ref, g_ref, sc_ref, dx_ref, dsh_ref, dsc_ref, dg_ref):
        i = pl.program_id(0)
        dh = dh_ref[...]
        xv = x_ref[...]
        r = _rsq(xv)
        xh = xv * r
        dxh = dh * (1.0 + sc_ref[...]) * g_ref[...]
        dx_ref[...] = dx1_ref[...] + r * (dxh - xh * jnp.mean(dxh * xh, axis=-1, keepdims=True))

        @pl.when(i == 0)
        def _():
            for ref in (dsh_ref, dsc_ref, dg_ref):
                ref[...] = jnp.zeros(ref.shape, F32)

        dsh_ref[...] += _colsum(dh)
        dsc_ref[...] += _colsum(dh * (xh * g_ref[...]))
        dg_ref[...] += _colsum(dh * (1.0 + sc_ref[...]) * xh)

    vec = pl.BlockSpec((1, d), lambda i: (0, 0))
    row = pl.BlockSpec((ts, d), lambda i: (i, 0))
    vshape = jax.ShapeDtypeStruct((1, d), F32)
    return pl.pallas_call(
        body, name="first_bwd", grid=(s_len // ts,),
        in_specs=[row, row, row, vec, vec], out_specs=[row, vec, vec, vec],
        out_shape=[jax.ShapeDtypeStruct((s_len, d), F32), vshape, vshape, vshape],
        compiler_params=_cparams("arbitrary"),
    )(dh1, x, dx1, g_pre, sc)


def _gconv_bwd(d_mixcat, proj, lay, w, b, ts, tc):
    s_len = proj.shape[0]
    cw = w.shape[1]
    n_i = s_len // ts
    dc_off = d_mixcat.shape[1] - cw

    def body(dc_ref, dch_ref, gb_ref, gbh_ref, gc_ref, gch_ref, ci_ref, cih_ref, w_ref, b_ref,
             dgb_ref, dgc_ref, dci_ref, dw_ref, db_ref):
        i = pl.program_id(1)
        gc, ci = gc_ref[...].astype(F32), ci_ref[...].astype(F32)
        p = gc * ci
        ph = jnp.where(i > 0, gch_ref[...].astype(F32) * cih_ref[...].astype(F32), 0.0)
        pm1, pm2 = _shift_down(p, ph, 1), _shift_down(p, ph, 2)
        z = pm2 * w_ref[0:1, :] + pm1 * w_ref[1:2, :] + p * w_ref[2:3, :] + b_ref[...]
        dc = dc_ref[...].astype(F32)
        dgb_ref[...] = (dc * z).astype(BF16)
        dz = dc * gb_ref[...].astype(F32)
        dzh = jnp.where(i < n_i - 1, dch_ref[...].astype(F32) * gbh_ref[...].astype(F32), 0.0)
        dz1, dz2 = _shift_up(dz, dzh, 1), _shift_up(dz, dzh, 2)
        dp = dz * w_ref[2:3, :] + dz1 * w_ref[1:2, :] + dz2 * w_ref[0:1, :]
        dgc_ref[...] = (dp * ci).astype(BF16)
        dci_ref[...] = (dp * gc).astype(BF16)

        @pl.when(i == 0)
        def _():
            dw_ref[...] = jnp.zeros(dw_ref.shape, F32)
            db_ref[...] = jnp.zeros(db_ref.shape, F32)

        dw_ref[0:1, :] += _colsum(dz2 * p)
        dw_ref[1:2, :] += _colsum(dz1 * p)
        dw_ref[2:3, :] += _colsum(dz * p)
        db_ref[...] += _colsum(dz)

    def blk(off):
        return pl.BlockSpec((ts, tc), lambda j, i: (i, off // tc + j))

    def prev(off):
        return pl.BlockSpec((SUB, tc), _prev_halo(ts, lambda j: off // tc + j))

    def nxt(off):
        return pl.BlockSpec((SUB, tc), _next_halo(ts, s_len, lambda j: off // tc + j))

    out_blk = pl.BlockSpec((ts, tc), lambda j, i: (i, j))
    act = jax.ShapeDtypeStruct((s_len, cw), BF16)
    return pl.pallas_call(
        body, name="gconv_bwd", grid=(cw // tc, n_i),
        in_specs=[blk(dc_off), nxt(dc_off), blk(lay["gb"]), nxt(lay["gb"]), blk(lay["gc"]), prev(lay["gc"]),
                  blk(lay["ci"]), prev(lay["ci"]),
                  pl.BlockSpec((3, tc), lambda j, i: (0, j)), pl.BlockSpec((1, tc), lambda j, i: (0, j))],
        out_specs=[out_blk, out_blk, out_blk,
                   pl.BlockSpec((3, tc), lambda j, i: (0, j)), pl.BlockSpec((1, tc), lambda j, i: (0, j))],
        out_shape=[act, act, act, jax.ShapeDtypeStruct((3, cw), F32), jax.ShapeDtypeStruct((1, cw), F32)],
        compiler_params=_cparams("parallel", "arbitrary"),
    )(d_mixcat, d_mixcat, proj, proj, proj, proj, proj, proj, w, b)


def _delta(o, d_mixcat, n_heads, ts):
    s_len = o.shape[0]

    def body(o_ref, do_ref, out_ref):
        for h in range(n_heads):
            sl = slice(h * V_DIM, (h + 1) * V_DIM)
            prod = o_ref[:, sl].astype(F32) * do_ref[:, sl].astype(F32)
            out_ref[h] = jnp.broadcast_to(jnp.sum(prod, axis=1, keepdims=True), (ts, LANE))

    hv = n_heads * V_DIM
    return pl.pallas_call(
        body, name="attn_delta", grid=(s_len // ts,),
        in_specs=[pl.BlockSpec((ts, hv), lambda i: (i, 0)), pl.BlockSpec((ts, hv), lambda i: (i, 0))],
        out_specs=pl.BlockSpec((n_heads, ts, LANE), lambda i: (0, i, 0)),
        out_shape=jax.ShapeDtypeStruct((n_heads, s_len, LANE), F32),
        compiler_params=_cparams("parallel"),
    )(o, d_mixcat)


def _flash_bwd(q, k, v, d_mixcat, lse_row, delta_row, n_heads, t, scale, exchange):
    nx = len(exchange)
    s_len = q.shape[0]
    nb = s_len // t
    pairs = [(j, i) for j in range(nb) for i in range(j, nb)]
    jtab = jnp.asarray(np.array([p[0] for p in pairs], np.int32))
    itab = jnp.asarray(np.array([p[1] for p in pairs], np.int32))
    n_steps = len(pairs)

    def body(jt_ref, it_ref, q_ref, k_ref, v_ref, do_ref, lse_ref, dl_ref, *rest):
        xin, (dq_ref, dk_ref, dv_ref), xout = rest[:nx], rest[nx:nx + 3], rest[nx + 3:2 * nx + 3]
        dq_acc, dk_acc, dv_acc = rest[2 * nx + 3:2 * nx + 6]
        sems = rest[2 * nx + 6:]
        head, step_id = pl.program_id(0), pl.program_id(1)
        j, i = jt_ref[step_id], it_ref[step_id]

        @pl.when((head == 0) & (step_id == 0))
        def _():
            _exchange_start(xin, xout, *sems)

        @pl.when(step_id == 0)
        def _():
            dq_acc[...] = jnp.zeros(dq_acc.shape, F32)

        @pl.when(i == j)
        def _():
            dk_acc[...] = jnp.zeros(dk_acc.shape, F32)
            dv_acc[...] = jnp.zeros(dv_acc.shape, F32)

        def step(diag):
            qv, kv, vv, dov = q_ref[...], k_ref[...], v_ref[...], do_ref[...]
            s_t = lax.dot_general(kv, qv, NT_DIMS, preferred_element_type=F32)
            if diag:
                krow = lax.broadcasted_iota(jnp.int32, s_t.shape, 0)
                qcol = lax.broadcasted_iota(jnp.int32, s_t.shape, 1)
                s_t = jnp.where(krow <= qcol, s_t, NEG)
            p_t = jnp.exp2(s_t - lse_ref[0])
            dv_acc[...] += jnp.dot(p_t.astype(BF16), dov, preferred_element_type=F32)
            dp_t = lax.dot_general(vv, dov, NT_DIMS, preferred_element_type=F32)
            ds_t = (p_t * (dp_t - dl_ref[0])).astype(BF16)
            dk_acc[...] += jnp.dot(ds_t, qv, preferred_element_type=F32)
            rows = pl.ds(pl.multiple_of(i * t, t), t)
            dq_acc[rows, :] += lax.dot_general(ds_t, kv, TN_DIMS, preferred_element_type=F32)

        @pl.when(i > j)
        def _():
            step(False)

        @pl.when(i == j)
        def _():
            step(True)

        @pl.when(i == nb - 1)
        def _():
            dk_ref[...] = (dk_acc[...] * LN2).astype(BF16)
            dv_ref[...] = dv_acc[...].astype(BF16)

        @pl.when(step_id == n_steps - 1)
        def _():
            dq_ref[...] = (dq_acc[...] * scale).astype(BF16)

        @pl.when((head == n_heads - 1) & (step_id == n_steps - 1))
        def _():
            _exchange_finish(xin, xout, *sems)

    hbm = pl.BlockSpec(memory_space=pl.ANY)
    hv = n_heads * V_DIM
    do_off = 0
    grid_spec = pltpu.PrefetchScalarGridSpec(
        num_scalar_prefetch=2, grid=(n_heads, n_steps),
        in_specs=[pl.BlockSpec((t, HEAD_W), lambda h, s, jt, it: (it[s], h)),
                  pl.BlockSpec((t, HEAD_W), lambda h, s, jt, it: (jt[s], h)),
                  pl.BlockSpec((t, V_DIM), lambda h, s, jt, it: (jt[s], h)),
                  pl.BlockSpec((t, V_DIM), lambda h, s, jt, it: (it[s], do_off + h)),
                  pl.BlockSpec((1, 1, t), lambda h, s, jt, it: (h, 0, it[s])),
                  pl.BlockSpec((1, 1, t), lambda h, s, jt, it: (h, 0, it[s]))] + [hbm] * nx,
        out_specs=[pl.BlockSpec((s_len, HEAD_W), lambda h, s, jt, it: (0, h)),
                   pl.BlockSpec((t, HEAD_W), lambda h, s, jt, it: (jt[s], h)),
                   pl.BlockSpec((t, V_DIM), lambda h, s, jt, it: (jt[s], h))] + [hbm] * nx,
        scratch_shapes=[pltpu.VMEM((s_len, HEAD_W), F32), pltpu.VMEM((t, HEAD_W), F32), pltpu.VMEM((t, V_DIM), F32)]
        + _comm_scratch(nx),
    )
    return pl.pallas_call(
        body, name="flash_bwd", grid_spec=grid_spec,
        out_shape=[jax.ShapeDtypeStruct((s_len, n_heads * HEAD_W), BF16),
                   jax.ShapeDtypeStruct((s_len, n_heads * HEAD_W), BF16),
                   jax.ShapeDtypeStruct((s_len, hv), BF16)] + [jax.ShapeDtypeStruct(c.shape, c.dtype) for c in exchange],
        compiler_params=_cparams("arbitrary", "arbitrary"),
    )(jtab, itab, q, k, v, d_mixcat, lse_row, delta_row, *exchange)


def _qkv_bwd(dq, dk, dv, proj, lay, wuq, wk, wv, g_q, g_kv, ctab, atab, btab, n_heads, ts):
    s_len = proj.shape[0]
    ql_w, kl_w = wuq.shape[0], wk.shape[0]
    tail_w = lay["np"] - lay["ql"]
    kv_o, kr_o = lay["kv"] - lay["ql"], lay["kr"] - lay["ql"]

    def body(dq_ref, dk_ref, dv_ref, ql_ref, kl_ref, wuq_ref, wk_ref, wv_ref, gq_ref, gkv_ref, c_ref, a_ref, b_ref,
             dqr_ref, dkn_ref, tail_ref, dgq_ref, dgkv_ref):
        i = pl.program_id(0)
        c, a, b = c_ref[...], a_ref[...], b_ref[...]
        dkr = jnp.zeros((ts, LANE), F32)
        for h in range(n_heads):
            o = h * HEAD_W
            dqr_ref[:, o:o + QK_NOPE] = dq_ref[:, o:o + QK_NOPE]
            dqr_ref[:, o + QK_NOPE:o + HEAD_W] = _rope_t(dq_ref[:, o + QK_NOPE:o + HEAD_W].astype(F32), c, a, b).astype(BF16)
            dkn_ref[:, h * QK_NOPE:(h + 1) * QK_NOPE] = dk_ref[:, o:o + QK_NOPE]
            dkr = dkr + dk_ref[:, o + QK_NOPE:o + HEAD_W].astype(F32)
        tail_ref[...] = jnp.zeros(tail_ref.shape, BF16)
        tail_ref[:, kr_o:kr_o + LANE] = _rope_t(dkr, c, a, b).astype(BF16)

        def rms_bwd(lat_ref, dn, g_ref):
            lat = lat_ref[...].astype(F32)
            r = _rsq(lat)
            xh = lat * r
            dxh = dn * g_ref[...]
            return r * (dxh - xh * jnp.mean(dxh * xh, axis=-1, keepdims=True)), _colsum(dn * xh)

        dqn = lax.dot_general(dqr_ref[...], wuq_ref[...], NT_DIMS, preferred_element_type=F32)
        d_ql, dgq = rms_bwd(ql_ref, dqn, gq_ref)
        tail_ref[:, 0:ql_w] = d_ql.astype(BF16)
        dkvn = (lax.dot_general(dkn_ref[...], wk_ref[...], NT_DIMS, preferred_element_type=F32)
                + lax.dot_general(dv_ref[...], wv_ref[...], NT_DIMS, preferred_element_type=F32))
        d_kl, dgkv = rms_bwd(kl_ref, dkvn, gkv_ref)
        tail_ref[:, kv_o:kv_o + kl_w] = d_kl.astype(BF16)

        @pl.when(i == 0)
        def _():
            dgq_ref[...] = jnp.zeros(dgq_ref.shape, F32)
            dgkv_ref[...] = jnp.zeros(dgkv_ref.shape, F32)

        dgq_ref[...] += dgq
        dgkv_ref[...] += dgkv

    def full(arr):
        return pl.BlockSpec(arr.shape, lambda i: (0, 0))

    def rows(w):
        return pl.BlockSpec((ts, w), lambda i: (i, 0))

    tab = pl.BlockSpec((ts, LANE), lambda i: (i, 0))
    hw, hv, hn = n_heads * HEAD_W, n_heads * V_DIM, n_heads * QK_NOPE
    return pl.pallas_call(
        body, name="qkv_bwd", grid=(s_len // ts,),
        in_specs=[rows(hw), rows(hw), rows(hv),
                  pl.BlockSpec((ts, ql_w), lambda i: (i, lay["ql"] // ql_w)),
                  pl.BlockSpec((ts, kl_w), lambda i: (i, lay["kv"] // kl_w)),
                  full(wuq), full(wk), full(wv), full(g_q), full(g_kv), tab, tab, tab],
        out_specs=[rows(hw), rows(hn), rows(tail_w), full(g_q), full(g_kv)],
        out_shape=[jax.ShapeDtypeStruct((s_len, hw), BF16), jax.ShapeDtypeStruct((s_len, hn), BF16),
                   jax.ShapeDtypeStruct((s_len, tail_w), BF16),
                   jax.ShapeDtypeStruct(g_q.shape, F32), jax.ShapeDtypeStruct(g_kv.shape, F32)],
        compiler_params=_cparams("arbitrary"),
    )(dq, dk, dv, proj, proj, wuq, wk, wv, g_q, g_kv, ctab, atab, btab)


def _adamw(w, g, m, v):
    m = ADAM_B1 * m + (1.0 - ADAM_B1) * g
    v = ADAM_B2 * v + (1.0 - ADAM_B2) * (g * g)
    m_hat = m / (1.0 - ADAM_B1 ** ADAM_STEP)
    v_hat = v / (1.0 - ADAM_B2 ** ADAM_STEP)
    delta = -ADAM_LR * (m_hat / (jnp.sqrt(v_hat) + ADAM_EPS) + ADAM_WD * w)
    return delta, m, v


def _adam_parts(parts, w, m, v, tr, name):
    r, c = w.shape
    tr = _tile(r, tr, SUB)

    def body(p_ref, w_ref, m_ref, v_ref, g_out, d_out, m_out, v_out):
        g = p_ref[0].astype(F32)
        for dev in range(1, N_DEV):
            g = g + p_ref[dev].astype(F32)
        g_out[...] = g
        d_out[...], m_out[...], v_out[...] = _adamw(w_ref[...], g, m_ref[...], v_ref[...])

    blk = pl.BlockSpec((tr, c), lambda i: (i, 0))
    shp = jax.ShapeDtypeStruct((r, c), F32)
    return pl.pallas_call(
        body, name=name, grid=(r // tr,),
        in_specs=[pl.BlockSpec((N_DEV, tr, c), lambda i: (0, i, 0)), blk, blk, blk],
        out_specs=[blk, blk, blk, blk], out_shape=[shp, shp, shp, shp],
        compiler_params=_cparams("parallel"),
    )(parts, w, m, v)


def _adam_ada(cact_t, dmod_sh, w, m, v, tr):
    r, c = w.shape

    def body(ct_ref, dm_ref, w_ref, m_ref, v_ref, g_out, d_out, m_out, v_out):
        g = jnp.dot(ct_ref[...], dm_ref[...], preferred_element_type=F32, precision=lax.Precision.HIGHEST)
        g_out[...] = g
        d_out[...], m_out[...], v_out[...] = _adamw(w_ref[...], g, m_ref[...], v_ref[...])

    blk = pl.BlockSpec((tr, c), lambda i: (i, 0))
    shp = jax.ShapeDtypeStruct((r, c), F32)
    return pl.pallas_call(
        body, name="adam_ada", grid=(r // tr,),
        in_specs=[pl.BlockSpec((tr, N_DEV), lambda i: (i, 0)), pl.BlockSpec((N_DEV, c), lambda i: (0, 0)), blk, blk, blk],
        out_specs=[blk, blk, blk, blk], out_shape=[shp, shp, shp, shp],
        compiler_params=_cparams("parallel"),
    )(cact_t, dmod_sh, w, m, v)


def _adam_small(v_all, offs, ws, ms, vs):
    n_par = len(ws)

    def body(p_ref, *refs):
        w_refs, m_refs, v_refs = refs[:n_par], refs[n_par:2 * n_par], refs[2 * n_par:3 * n_par]
        sum_ref = refs[3 * n_par]
        outs = refs[3 * n_par + 1:]
        g = p_ref[0:1, :]
        for dev in range(1, N_DEV):
            g = g + p_ref[dev:dev + 1, :]
        sum_ref[...] = g
        for p in range(n_par):
            n = w_refs[p].shape[1]
            gp = sum_ref[:, offs[p]:offs[p] + n]
            outs[p][...] = gp
            (outs[n_par + p][...], outs[2 * n_par + p][...], outs[3 * n_par + p][...]) = _adamw(
                w_refs[p][...], gp, m_refs[p][...], v_refs[p][...])

    vm = pl.BlockSpec(memory_space=pltpu.VMEM)
    shapes = [jax.ShapeDtypeStruct(w.shape, F32) for w in ws]
    out = pl.pallas_call(
        body, name="adam_small", in_specs=[vm] * (1 + 3 * n_par), out_specs=[vm] * (1 + 4 * n_par),
        out_shape=[jax.ShapeDtypeStruct((1, v_all.shape[1]), F32)] + shapes * 4, compiler_params=_cparams(),
    )(v_all, *ws, *ms, *vs)
    return out[0], [out[1 + k * n_par:1 + (k + 1) * n_par] for k in range(4)]


def _my_place():
    return lax.axis_index("x"), lax.axis_index("y"), lax.axis_index("c")


def _peer(place, k):
    x, y, c = place
    return (x ^ (k >> 2), y ^ ((k >> 1) & 1), c ^ (k & 1))


def _index(place):
    return 4 * place[0] + 2 * place[1] + place[2]


def _ada_fwd(vec, w_ada, b_ada_rows):
    lv = vec.shape[1]
    d, c = w_ada.shape

    def body(vec_ref, w_ref, b_ref, gath_ref, cact_ref, mod_ref, modsh, send_a, recv_a, send_b, recv_b, local_s):
        me = _my_place()
        my_i = _index(me)

        def gather_copy(k, to, src_row):
            row = gath_ref.at[pl.ds(src_row, 1), :]
            return pltpu.make_async_remote_copy(src_ref=row, dst_ref=row, send_sem=send_a.at[k], recv_sem=recv_a.at[k],
                                                device_id=to, device_id_type=MESH)

        own = pltpu.make_async_copy(vec_ref, gath_ref.at[pl.ds(my_i, 1), :], local_s.at[0])
        own.start()
        own.wait()
        sends = [gather_copy(k, _peer(me, k), my_i) for k in range(1, N_DEV)]
        for cp in sends:
            cp.start()
        for k in range(1, N_DEV):
            gather_copy(k, me, _index(_peer(me, k))).wait_recv()
        for cp in sends:
            cp.wait_send()

        c_all = gath_ref[:, 0:d]
        cact = c_all * _sigmoid(c_all)
        cact_ref[...] = cact
        modsh[...] = jnp.dot(cact, w_ref[...], preferred_element_type=F32, precision=lax.Precision.HIGHEST)

        def mod_copy(k, to, src_row, dst_row):
            return pltpu.make_async_remote_copy(src_ref=modsh.at[pl.ds(src_row, 1), :], dst_ref=mod_ref.at[pl.ds(dst_row, 1), :],
                                                send_sem=send_b.at[k], recv_sem=recv_b.at[k],
                                                device_id=to, device_id_type=MESH)

        own = pltpu.make_async_copy(modsh.at[pl.ds(my_i, 1), :], mod_ref.at[pl.ds(my_i, 1), :], local_s.at[1])
        own.start()
        sends = [mod_copy(k, _peer(me, k), _index(_peer(me, k)), my_i) for k in range(1, N_DEV)]
        for cp in sends:
            cp.start()
        for k in range(1, N_DEV):
            mod_copy(k, me, my_i, _index(_peer(me, k))).wait_recv()
        for cp in sends:
            cp.wait_send()
        own.wait()
        mod_ref[...] = mod_ref[...] + b_ref[...]

    vm = pl.BlockSpec(memory_space=pltpu.VMEM)
    return pl.pallas_call(
        body, name="ada_fwd", in_specs=[vm, vm, vm], out_specs=[vm, vm, vm],
        out_shape=[jax.ShapeDtypeStruct((N_DEV, lv), F32), jax.ShapeDtypeStruct((N_DEV, d), F32),
                   jax.ShapeDtypeStruct((N_DEV, c), F32)],
        scratch_shapes=[pltpu.VMEM((N_DEV, c), F32)] + [pltpu.SemaphoreType.DMA((N_DEV,))] * 4
        + [pltpu.SemaphoreType.DMA((2,))],
        compiler_params=pltpu.CompilerParams(vmem_limit_bytes=VMEM_LIMIT),
    )(vec, w_ada, b_ada_rows)


def _gather_small(vec):
    lv = vec.shape[1]

    def body(vec_ref, gath_ref, send_s, recv_s, local_s):
        me = _my_place()
        my_i = _index(me)

        def copy(k, to, src_row):
            row = gath_ref.at[pl.ds(src_row, 1), :]
            return pltpu.make_async_remote_copy(src_ref=row, dst_ref=row, send_sem=send_s.at[k], recv_sem=recv_s.at[k],
                                                device_id=to, device_id_type=MESH)

        own = pltpu.make_async_copy(vec_ref, gath_ref.at[pl.ds(my_i, 1), :], local_s)
        own.start()
        own.wait()
        sends = [copy(k, _peer(me, k), my_i) for k in range(1, N_DEV)]
        for cp in sends:
            cp.start()
        for k in range(1, N_DEV):
            copy(k, me, _index(_peer(me, k))).wait_recv()
        for cp in sends:
            cp.wait_send()

    vm = pl.BlockSpec(memory_space=pltpu.VMEM)
    return pl.pallas_call(
        body, name="gather_small", in_specs=[vm], out_specs=vm,
        out_shape=jax.ShapeDtypeStruct((N_DEV, lv), F32),
        scratch_shapes=[pltpu.SemaphoreType.DMA((N_DEV,))] * 2 + [pltpu.SemaphoreType.DMA],
        compiler_params=pltpu.CompilerParams(vmem_limit_bytes=VMEM_LIMIT),
    )(vec)


PER = N_DEV - 1


def _comm_scratch(n):
    return [pltpu.SemaphoreType.DMA((n * PER,)), pltpu.SemaphoreType.DMA((n * PER,)), pltpu.SemaphoreType.DMA((n,))]


def _gather_copies(ins, outs, send_s, recv_s, local_s):
    n = len(ins)
    me = _my_place()
    x, y, c = me
    sibling = (x, y, 1 - c)
    chips = [(1 - x, y), (x, 1 - y), (1 - x, 1 - y)]

    def copy(a, k, block, to, src=None):
        slot = outs[a].at[_index(block)]
        return pltpu.make_async_remote_copy(src_ref=slot if src is None else src, dst_ref=slot,
                                            send_sem=send_s.at[a * PER + k], recv_sem=recv_s.at[a * PER + k],
                                            device_id=to, device_id_type=MESH)

    mine = [pltpu.make_async_copy(ins[a], outs[a].at[_index(me)], local_s.at[a]) for a in range(n)]
    first = []
    for a in range(n):
        first.append(copy(a, 0, me, sibling, src=ins[a]))
        first += [copy(a, 1 + j, me, (*chip, c), src=ins[a]) for j, chip in enumerate(chips)]
    landed = [copy(a, 1 + j, (*chip, c), me) for j, chip in enumerate(chips) for a in range(n)]
    passed = [copy(a, 4 + j, (*chip, c), sibling) for j, chip in enumerate(chips) for a in range(n)]
    from_sibling = [copy(a, 0, sibling, me) for a in range(n)]
    from_sibling += [copy(a, 4 + j, (*chip, 1 - c), me) for a in range(n) for j, chip in enumerate(chips)]
    return mine, first, landed, passed, from_sibling


def _gather_start(*refs):
    mine, first, _, _, _ = _gather_copies(*refs)
    for cp in mine + first:
        cp.start()


def _gather_forward(*refs):
    _, _, landed, passed, _ = _gather_copies(*refs)
    for got, fwd in zip(landed, passed):
        got.wait_recv()
        fwd.start()


def _gather_finish(*refs):
    mine, first, _, passed, from_sibling = _gather_copies(*refs)
    for cp in from_sibling:
        cp.wait_recv()
    for cp in first + passed:
        cp.wait_send()
    for cp in mine:
        cp.wait()


def _exchange_copies(ins, outs, send_s, recv_s, local_s):
    n = len(ins)
    me = _my_place()
    my_i = _index(me)

    def copy(a, k, to, src_slot, dst_slot):
        return pltpu.make_async_remote_copy(src_ref=ins[a].at[src_slot], dst_ref=outs[a].at[dst_slot],
                                            send_sem=send_s.at[a * PER + k - 1], recv_sem=recv_s.at[a * PER + k - 1],
                                            device_id=to, device_id_type=MESH)

    mine = [pltpu.make_async_copy(ins[a].at[my_i], outs[a].at[my_i], local_s.at[a]) for a in range(n)]
    sends = [copy(a, k, _peer(me, k), _index(_peer(me, k)), my_i) for k in range(1, N_DEV) for a in range(n)]
    recvs = [copy(a, k, me, my_i, _index(_peer(me, k))) for k in range(1, N_DEV) for a in range(n)]
    return mine, sends, recvs


def _exchange_start(*refs):
    mine, sends, _ = _exchange_copies(*refs)
    for cp in mine + sends:
        cp.start()


def _exchange_finish(*refs):
    mine, sends, recvs = _exchange_copies(*refs)
    for cp in recvs:
        cp.wait_recv()
    for cp in sends:
        cp.wait_send()
    for cp in mine:
        cp.wait()


def _gathered_shapes(shards):
    return [jax.ShapeDtypeStruct((N_DEV,) + s.shape, s.dtype) for s in shards]


def _gather_weights(shards):
    n = len(shards)

    def body(*refs):
        parts = (refs[:n], refs[n:2 * n]) + tuple(refs[2 * n:])
        _gather_start(*parts)
        _gather_forward(*parts)
        _gather_finish(*parts)

    hbm = pl.BlockSpec(memory_space=pl.ANY)
    return pl.pallas_call(
        body, name="gather_weights", in_specs=[hbm] * n, out_specs=[hbm] * n,
        out_shape=_gathered_shapes(shards), scratch_shapes=_comm_scratch(n),
    )(*shards)


def _proj_layout(cw, ql, kl):
    lay = {"gb": 0, "gc": cw, "ci": 2 * cw, "ql": 3 * cw}
    assert lay["ql"] % ql == 0
    lay["kv"] = _roundup(lay["ql"] + ql, kl)
    lay["kr"] = lay["kv"] + kl
    lay["np"] = _roundup(lay["kr"] + LANE, 4 * LANE)
    return lay


def _chunks_cols(g):
    r, c8 = g.shape
    return jnp.transpose(g.reshape(r, N_DEV, c8 // N_DEV), (1, 0, 2))


def _from_col_shards(a):
    n, r, c = a.shape
    return jnp.transpose(a, (1, 0, 2)).reshape(r, n * c)


def kernel(x, c, positions, w_ada, b_ada, g_pre_mix, g_post_mix, w_in, g_q, w_uq, g_kv, w_ukv, conv_w_mix, conv_b_mix, w_o, g_pre_ffn, g_post_ffn, w_up, conv_w_ffn, conv_b_ffn, w_down, loss_target, m_w_ada, m_b_ada, m_g_pre_mix, m_g_post_mix, m_w_in, m_g_q, m_w_uq, m_g_kv, m_w_ukv, m_conv_w_mix, m_conv_b_mix, m_w_o, m_g_pre_ffn, m_g_post_ffn, m_w_up, m_conv_w_ffn, m_conv_b_ffn, m_w_down, v_w_ada, v_b_ada, v_g_pre_mix, v_g_post_mix, v_w_in, v_g_q, v_w_uq, v_g_kv, v_w_ukv, v_conv_w_mix, v_conv_b_mix, v_w_o, v_g_pre_ffn, v_g_post_ffn, v_w_up, v_conv_w_ffn, v_conv_b_ffn, v_w_down):
    s_len, d = x.shape[1], x.shape[2]
    ql, kl = w_uq.shape[1], w_ukv.shape[1]
    n_heads = w_ukv.shape[2] * N_DEV // (QK_NOPE + V_DIM)
    cw = conv_w_mix.shape[2] * N_DEV
    f2 = w_up.shape[2] * N_DEV
    ff = f2 // 2
    in_cols = w_in.shape[2] * N_DEV
    ada_c = w_ada.shape[2]
    cwm_c, cwf_c = conv_w_mix.shape[2], conv_w_ffn.shape[2]
    scale = 1.0 / math.sqrt(QK_NOPE + QK_ROPE)
    lay = _proj_layout(cw, ql, kl)
    n_pad = lay["np"]
    my_i = _index(_my_place())

    ts_row = _tile(s_len, PREF["row"], SUB)
    ts_conv = _tile(s_len, PREF["conv_rows"], SUB)
    tc_conv = _tile(cw, PREF["conv_cols"])
    tc_ffn = cwf_c
    ts_ffn = _tile(s_len, PREF["ffn_rows"], SUB)
    pair_order = [k // 2 + (k % 2) * (N_DEV // 2) for k in range(N_DEV)]
    pair_place = [pair_order.index(k) for k in range(N_DEV)]

    def paired(shards):
        return _from_col_shards(jnp.stack([shards[p] for p in pair_order]))

    def unpaired_chunks(g):
        ch = _chunks_cols(g)
        return jnp.stack([ch[p] for p in pair_place])
    ts_qkv = _tile(s_len, PREF["row"], SUB)
    t_attn = _tile(s_len, PREF["attn"])

    x2d, tgt = x[0], loss_target[0]

    vec = jnp.concatenate([c, conv_w_mix[0].reshape(1, -1), conv_w_ffn[0].reshape(1, -1)], axis=1)
    gath, cact, mod_rows = _ada_fwd(vec, w_ada[0], b_ada.reshape(N_DEV, ada_c))
    cwm_full = _from_col_shards(gath[:, d:d + 3 * cwm_c].reshape(N_DEV, 3, cwm_c))
    cwf_shards = gath[:, d + 3 * cwm_c:].reshape(N_DEV, 3, cwf_c)
    cwf_pair = paired(cwf_shards)
    cbf_pair = paired(jnp.transpose(conv_b_ffn.reshape(1, N_DEV, cwf_c), (1, 0, 2)))
    mod = mod_rows.reshape(1, N_DEV * ada_c)
    sh_m, sc_m, gt_m, sh_f, sc_f, gt_f = [mod[:, k * d:(k + 1) * d] for k in range(6)]

    g_in, g_uq, g_ukv = _gather_weights([w_in[0].astype(BF16), w_uq[0].astype(BF16), w_ukv[0].astype(BF16)])
    win = _from_col_shards(g_in)
    cut = np.cumsum([0, ql, kl, QK_ROPE, cw, cw, cw])
    part = [win[:, cut[k]:cut[k + 1]] for k in range(6)]

    def zcols(n):
        return jnp.zeros((d, n), BF16)

    win_p = jnp.concatenate([part[3], part[4], part[5], part[0], zcols(lay["kv"] - lay["ql"] - ql), part[1],
                             part[2], zcols(n_pad - lay["kr"] - QK_ROPE)], axis=1)
    wuq_p = jnp.pad(_from_col_shards(g_uq).reshape(ql, n_heads, QK_NOPE + QK_ROPE),
                    ((0, 0), (0, 0), (0, HEAD_W - QK_NOPE - QK_ROPE))).reshape(ql, n_heads * HEAD_W)
    wukv = _from_col_shards(g_ukv).reshape(kl, n_heads, QK_NOPE + V_DIM)
    wk = wukv[:, :, :QK_NOPE].reshape(kl, n_heads * QK_NOPE)
    wv = wukv[:, :, QK_NOPE:].reshape(kl, n_heads * V_DIM)

    inv_freq = 1.0 / (ROPE_THETA ** (jnp.arange(0, QK_ROPE, 2, dtype=F32) / QK_ROPE))
    inv_row = jnp.tile(inv_freq, LANE // (QK_ROPE // 2)).reshape(1, LANE)
    ctab, atab, btab = _rope_tables(positions.astype(F32).reshape(s_len, 1), inv_row, _tile(s_len, 1024, SUB))

    h1 = _modnorm_fwd(x2d, g_pre_mix, sc_m, sh_m, ts_row)
    proj = _matmul(h1, win_p, out_dtype=BF16, tm=1024, tn=1280, tk=2048, name="mm_proj")
    q, k, v, qn, kvn = _qkv_fwd(proj, lay, wuq_p, wk, wv, g_q, g_kv, ctab, atab, btab, n_heads, ts_qkv, scale * LOG2E)
    attn, lse_row, g_o, g_up = _flash_fwd(q, k, v, n_heads, t_attn, [w_o[0].astype(BF16), w_up[0].astype(BF16)], d)
    wo = g_o.reshape(d, d)

    def pair_shard(j):
        return j // 2 + (j % 2) * (N_DEV // 2)
    mixcat = _gconv_fwd(proj, lay, cwm_full, conv_b_mix, ts_conv, tc_conv, attn)
    mix = _matmul(mixcat, wo, out_dtype=F32, tm=512, tn=2048, tk=2048, name="mm_mix")
    x1, h2 = _post_mix_fwd(mix, x2d, gt_m, g_post_mix, g_pre_ffn, sc_f, sh_f, ts_row)
    up, g_down = _matmul(h2, g_up, out_dtype=BF16, tm=1024, tn=cwf_c, tk=2048, name="mm_up", b_shard_of=pair_shard,
                         gather=[w_down[0].astype(BF16)])
    wdown = g_down.reshape(ff, d)
    u, act = _act_fwd(up, cwf_pair, cbf_pair, ts_ffn, tc_ffn)
    y = _matmul(act, wdown, out_dtype=F32, tm=1024, tn=512, tk=ff, name="mm_down")

    dy, dx2, loss_row, d_gt_f, dg_post_ffn = _final_bwd(y, x1, tgt, gt_f, g_post_ffn, ts_row)
    gw_down = _matmul(act, dy, ta=True, out_dtype=BF16, tm=1408, tn=1024, tk=2048, name="mm_gw_down")
    d_act = _matmul(dy, wdown, tb=True, out_dtype=BF16, tm=1024, tn=1408, tk=2048, name="mm_d_act")
    d_up, dcw_pair, dcb_pair = _ffn_act_bwd(d_act, u, up, cwf_pair, ts_ffn, tc_ffn)
    dcb_ffn = _from_col_shards(unpaired_chunks(dcb_pair))
    gw_up, p_down = _matmul(h2, d_up, ta=True, out_dtype=BF16, tm=1024, tn=cwf_c, tk=2048, name="mm_gw_up",
                            exchange=[gw_down.reshape(N_DEV, ff // N_DEV, d)], out_shard_of=pair_shard)
    dh2 = _matmul_pair_shards(d_up, g_up, out_dtype=F32, tm=1024, tn=1024, name="mm_dh2")
    dx1, dmix, d_sh_f, d_sc_f, dg_pre_ffn, d_gt_m, dg_post_mix = _mid_bwd(
        dh2, x1, dx2, mix, g_pre_ffn, sc_f, gt_m, g_post_mix, ts_row)
    gw_o = _matmul(mixcat, dmix, ta=True, out_dtype=BF16, tm=1024, tn=1024, tk=2048, name="mm_gw_o")
    d_mixcat = _matmul(dmix, wo, tb=True, out_dtype=BF16, tm=1024, tn=1024, tk=2048, name="mm_d_mixcat")
    d_gb, d_gc, d_ci, dcw_mix, dcb_mix = _gconv_bwd(d_mixcat, proj, lay, cwm_full, conv_b_mix, ts_conv, tc_conv)
    delta = _delta(mixcat, d_mixcat, n_heads, _tile(s_len, 512, SUB))
    delta_row = delta[:, :, 0].reshape(n_heads, 1, s_len)
    dq, dk, dv, p_up, p_o = _flash_bwd(q, k, v, d_mixcat, lse_row, delta_row, n_heads, t_attn, scale,
                                       [gw_up, gw_o.reshape(N_DEV, d // N_DEV, d)])
    dq_r, dkn, d_tail, dg_q, dg_kv = _qkv_bwd(dq, dk, dv, proj, lay, wuq_p, wk, wv, g_q, g_kv, ctab, atab, btab,
                                              n_heads, ts_qkv)
    gw_uq_p = _matmul(qn, dq_r, ta=True, out_dtype=BF16, tm=768, tn=1024, tk=2048, name="mm_gw_uq")
    gw_k = _matmul(kvn, dkn, ta=True, out_dtype=BF16, tm=512, tn=1024, tk=2048, name="mm_gw_k")
    gw_v = _matmul(kvn, dv, ta=True, out_dtype=BF16, tm=512, tn=1024, tk=2048, name="mm_gw_v")
    d_proj = jnp.concatenate([d_gb, d_gc, d_ci, d_tail], axis=1)
    gw_uq = gw_uq_p.reshape(ql, n_heads, HEAD_W)[:, :, :QK_NOPE + QK_ROPE].reshape(ql, n_heads * (QK_NOPE + QK_ROPE))
    gw_ukv = jnp.concatenate([gw_k.reshape(kl, n_heads, QK_NOPE), gw_v.reshape(kl, n_heads, V_DIM)],
                             axis=2).reshape(kl, n_heads * (QK_NOPE + V_DIM))
    gw_in_p, p_uq, p_ukv, p_cwm, p_cwf = _matmul(
        h1, d_proj, ta=True, out_dtype=BF16, tm=1024, tn=1280, tk=2048, name="mm_gw_in",
        exchange=[_chunks_cols(gw_uq), _chunks_cols(gw_ukv), _chunks_cols(dcw_mix), unpaired_chunks(dcw_pair)])

    gw_in = jnp.concatenate([gw_in_p[:, lay["ql"]:lay["ql"] + ql], gw_in_p[:, lay["kv"]:lay["kv"] + kl],
                             gw_in_p[:, lay["kr"]:lay["kr"] + QK_ROPE], gw_in_p[:, :3 * cw]], axis=1)
    dh1, p_in = _matmul(d_proj, win_p, tb=True, out_dtype=F32, tm=512, tn=1024, tk=n_pad, name="mm_dh1",
                        exchange=[_chunks_cols(gw_in)])
    grad_x, d_sh_m, d_sc_m, dg_pre_mix = _first_bwd(dh1, x2d, dx1, g_pre_mix, sc_m, ts_row)

    dmod = jnp.concatenate([d_sh_m, d_sc_m, d_gt_m, d_sh_f, d_sc_f, d_gt_f], axis=1)
    small_g = [loss_row, dmod, dg_pre_mix, dg_post_mix, dg_q, dg_kv, dcb_mix, dg_pre_ffn, dg_post_ffn, dcb_ffn]
    v_all = _gather_small(jnp.concatenate(small_g, axis=1))

    small_names = ["b_ada", "g_pre_mix", "g_post_mix", "g_q", "g_kv", "conv_b_mix", "g_pre_ffn", "g_post_ffn",
                   "conv_b_ffn"]
    small_w = [b_ada, g_pre_mix, g_post_mix, g_q, g_kv, conv_b_mix, g_pre_ffn, g_post_ffn, conv_b_ffn]
    small_m = [m_b_ada, m_g_pre_mix, m_g_post_mix, m_g_q, m_g_kv, m_conv_b_mix, m_g_pre_ffn, m_g_post_ffn, m_conv_b_ffn]
    small_v = [v_b_ada, v_g_pre_mix, v_g_post_mix, v_g_q, v_g_kv, v_conv_b_mix, v_g_pre_ffn, v_g_post_ffn, v_conv_b_ffn]
    offs = np.cumsum([0] + [g.shape[1] for g in small_g])
    g_sum, small_out = _adam_small(v_all, [int(o) for o in offs[1:-1]], small_w, small_m, small_v)
    small = [dict(zip(small_names, kind)) for kind in small_out]
    loss = g_sum[0, 0]

    dmod_sh = lax.dynamic_slice(v_all, (0, int(offs[1]) + my_i * ada_c), (N_DEV, ada_c))
    cact_t = jnp.transpose(cact)
    big = dict(
        w_ada=_adam_ada(cact_t, dmod_sh, w_ada[0], m_w_ada[0], v_w_ada[0], _tile(d, 256, SUB)),
        w_in=_adam_parts(p_in, w_in[0], m_w_in[0], v_w_in[0], 256, "adam_w_in"),
        w_uq=_adam_parts(p_uq, w_uq[0], m_w_uq[0], v_w_uq[0], 256, "adam_w_uq"),
        w_ukv=_adam_parts(p_ukv, w_ukv[0], m_w_ukv[0], v_w_ukv[0], 256, "adam_w_ukv"),
        w_o=_adam_parts(p_o, w_o[0], m_w_o[0], v_w_o[0], 128, "adam_w_o"),
        w_up=_adam_parts(p_up, w_up[0], m_w_up[0], v_w_up[0], 256, "adam_w_up"),
        w_down=_adam_parts(p_down, w_down[0], m_w_down[0], v_w_down[0], 176, "adam_w_down"),
        conv_w_mix=_adam_parts(p_cwm, conv_w_mix[0], m_conv_w_mix[0], v_conv_w_mix[0], 8, "adam_cw_mix"),
        conv_w_ffn=_adam_parts(p_cwf, conv_w_ffn[0], m_conv_w_ffn[0], v_conv_w_ffn[0], 8, "adam_cw_ffn"),
    )

    names = ["w_ada", "b_ada", "g_pre_mix", "g_post_mix", "w_in", "g_q", "w_uq", "g_kv", "w_ukv", "conv_w_mix",
             "conv_b_mix", "w_o", "g_pre_ffn", "g_post_ffn", "w_up", "conv_w_ffn", "conv_b_ffn", "w_down"]
    outs = [loss, grad_x[None]]
    for kind in range(4):
        for nm in names:
            outs.append(big[nm][kind][None] if nm in big else small[kind][nm])
    return tuple(outs)
```

```python
import math

import numpy as np
import jax
import jax.numpy as jnp
from jax import lax
from jax.experimental import pallas as pl
from jax.experimental.pallas import tpu as pltpu

F32 = jnp.float32
BF16 = jnp.bfloat16
N_DEV = 8
MESH = pl.DeviceIdType.MESH

QK_NOPE = 128
QK_ROPE = 64
V_DIM = 128
HEAD_W = 256
LANE = 128
SUB = 8
HALO = 16
STRIP_ROWS = 64
RMS_EPS = 1e-6
ROPE_THETA = 10000.0
ADAM_LR = 0.001
ADAM_B1 = 0.9
ADAM_B2 = 0.999
ADAM_EPS = 1e-08
ADAM_WD = 0.01
ADAM_STEP = 10
NEG = -1e30
LOG2E = 1.4426950408889634
LN2 = 0.6931471805599453
VMEM_LIMIT = 56 * 1024 * 1024

PREF = {"row": 256, "conv_rows": 512, "conv_cols": 512, "ffn_rows": 256, "attn": 1024, "attn_chunk": 1024, "attn_heads": 2}

NT_DIMS = (((1,), (1,)), ((), ()))
TN_DIMS = (((0,), (0,)), ((), ()))


def _cparams(*sem):
    return pltpu.CompilerParams(dimension_semantics=sem if sem else None, vmem_limit_bytes=VMEM_LIMIT)


def _tile(n, pref, unit=LANE):
    if n <= pref:
        return n
    t = (pref // unit) * unit
    while t >= unit:
        if n % t == 0:
            return t
        t -= unit
    return n


def _roundup(n, m):
    return (n + m - 1) // m * m


def _rsq(x):
    return lax.rsqrt(jnp.mean(x * x, axis=-1, keepdims=True) + RMS_EPS)


def _colsum(x):
    return jnp.sum(x, axis=0, keepdims=True)


def _sigmoid(x):
    return 1.0 / (1.0 + jnp.exp(-x))


def _matmul(a, b, *, ta=False, tb=False, out_dtype, tm, tn, tk, name, exchange=None, gather=None, b_shard_of=None,
            out_shard_of=None):
    m_dim, k_dim = (a.shape[1], a.shape[0]) if ta else a.shape
    if b_shard_of is not None:
        n_dim, tn = b.shape[0] * b.shape[2], b.shape[2]
    else:
        n_dim = b.shape[0] if tb else b.shape[1]
    if out_shard_of is not None:
        tn = n_dim // N_DEV
    tm, tn, tk = _tile(m_dim, tm), _tile(n_dim, tn), _tile(k_dim, tk)
    gi, gj, nk = m_dim // tm, n_dim // tn, k_dim // tk
    dims = (((0 if ta else 1,), (1 if tb else 0,)), ((), ()))
    chunks = list(exchange or gather or [])
    nx = len(chunks)
    comm_start, comm_finish = (_gather_start, _gather_finish) if gather else (_exchange_start, _exchange_finish)

    def body(*refs):
        a_ref, b_ref = refs[:2]
        xin, o_ref, xout = refs[2:2 + nx], refs[2 + nx], refs[3 + nx:3 + 2 * nx]
        scratch = refs[3 + 2 * nx:]
        sems = scratch[1:] if nk > 1 else scratch
        i, j, k = pl.program_id(0), pl.program_id(1), pl.program_id(2)
        if nx:
            @pl.when((i == 0) & (j == 0) & (k == 0))
            def _():
                comm_start(xin, xout, *sems)

        if gather:
            @pl.when((i == gi // 2) & (j == 0) & (k == 0))
            def _():
                _gather_forward(xin, xout, *sems)

        part = lax.dot_general(a_ref[...], b_ref[...], dims, preferred_element_type=F32)
        if nk == 1:
            o_ref[...] = part.astype(o_ref.dtype)
        else:
            acc_ref = scratch[0]

            @pl.when(k == 0)
            def _():
                acc_ref[...] = part

            @pl.when(k > 0)
            def _():
                acc_ref[...] += part

            @pl.when(k == nk - 1)
            def _():
                o_ref[...] = acc_ref[...].astype(o_ref.dtype)

        if nx:
            @pl.when((i == gi - 1) & (j == gj - 1) & (k == nk - 1))
            def _():
                comm_finish(xin, xout, *sems)

    a_spec = pl.BlockSpec((tk, tm), lambda i, j, k: (k, i)) if ta else pl.BlockSpec((tm, tk), lambda i, j, k: (i, k))
    b_spec = pl.BlockSpec((tn, tk), lambda i, j, k: (j, k)) if tb else pl.BlockSpec((tk, tn), lambda i, j, k: (k, j))
    if b_shard_of is not None:
        b_spec = pl.BlockSpec((None, tk, tn), lambda i, j, k: (b_shard_of(j), k, 0))
    o_spec, o_shape = pl.BlockSpec((tm, tn), lambda i, j, k: (i, j)), (m_dim, n_dim)
    if out_shard_of is not None:
        o_spec, o_shape = pl.BlockSpec((None, tm, tn), lambda i, j, k: (out_shard_of(j), i, 0)), (N_DEV, m_dim, tn)
    hbm = pl.BlockSpec(memory_space=pl.ANY)
    out = pl.pallas_call(
        body,
        name=name,
        grid=(gi, gj, nk),
        in_specs=[a_spec, b_spec] + [hbm] * nx,
        out_specs=[o_spec] + [hbm] * nx,
        out_shape=[jax.ShapeDtypeStruct(o_shape, out_dtype)]
        + (_gathered_shapes(chunks) if gather else [jax.ShapeDtypeStruct(c.shape, c.dtype) for c in chunks]),
        scratch_shapes=([pltpu.VMEM((tm, tn), F32)] if nk > 1 else []) + (_comm_scratch(nx) if nx else []),
        compiler_params=_cparams(*(("arbitrary",) * 3 if nx else ("parallel", "parallel", "arbitrary"))),
    )(a, b, *chunks)
    return out if nx else out[0]


def _matmul_pair_shards(a, shards, *, out_dtype, tm, tn, name):
    m_dim = a.shape[0]
    n_sh, n_dim, c = shards.shape
    half = n_sh // 2
    tm, tn = _tile(m_dim, tm), _tile(n_dim, tn)

    def body(a_ref, b0_ref, b1_ref, o_ref, acc_ref):
        k = pl.program_id(2)
        part = (lax.dot_general(a_ref[:, :c], b0_ref[...], NT_DIMS, preferred_element_type=F32)
                + lax.dot_general(a_ref[:, c:], b1_ref[...], NT_DIMS, preferred_element_type=F32))

        @pl.when(k == 0)
        def _():
            acc_ref[...] = part

        @pl.when(k > 0)
        def _():
            acc_ref[...] += part

        @pl.when(k == half - 1)
        def _():
            o_ref[...] = acc_ref[...].astype(o_ref.dtype)

    return pl.pallas_call(
        body, name=name, grid=(m_dim // tm, n_dim // tn, half),
        in_specs=[pl.BlockSpec((tm, 2 * c), lambda i, j, k: (i, k)),
                  pl.BlockSpec((None, tn, c), lambda i, j, k: (k, j, 0)),
                  pl.BlockSpec((None, tn, c), lambda i, j, k: (k + half, j, 0))],
        out_specs=pl.BlockSpec((tm, tn), lambda i, j, k: (i, j)),
        out_shape=jax.ShapeDtypeStruct((m_dim, n_dim), out_dtype),
        scratch_shapes=[pltpu.VMEM((tm, tn), F32)],
        compiler_params=_cparams("parallel", "parallel", "arbitrary"),
    )(a, shards, shards)


def _shift_down(x, halo, n):
    r = pltpu.roll(x, n, 0)
    hr = pltpu.roll(halo, n, 0)
    row = lax.broadcasted_iota(jnp.int32, halo.shape, 0)
    top = jnp.where(row < n, hr, r[:SUB])
    return jnp.concatenate([top, r[SUB:]], axis=0)


def _shift_up(x, halo, n):
    ts = x.shape[0]
    r = pltpu.roll(x, ts - n, 0)
    hr = pltpu.roll(halo, SUB - n, 0)
    row = lax.broadcasted_iota(jnp.int32, halo.shape, 0)
    bot = jnp.where(row >= SUB - n, hr, r[ts - SUB:])
    return jnp.concatenate([r[:ts - SUB], bot], axis=0)


def _conv3(x, halo, w_ref, b_ref):
    return _shift_down(x, halo, 2) * w_ref[0:1, :] + _shift_down(x, halo, 1) * w_ref[1:2, :] + x * w_ref[2:3, :] + b_ref[...]


def _prev_halo(ts, col):
    return lambda j, i: (jnp.maximum(i * (ts // SUB) - 1, 0), col(j))


def _next_halo(ts, n_rows, col):
    return lambda j, i: (jnp.minimum((i + 1) * (ts // SUB), n_rows // SUB - 1), col(j))


def _modnorm_fwd(x, g, sc, sh, ts):
    s_len, d = x.shape

    def body(x_ref, g_ref, sc_ref, sh_ref, h_ref):
        xv = x_ref[...]
        h_ref[...] = ((xv * _rsq(xv) * g_ref[...]) * (1.0 + sc_ref[...]) + sh_ref[...]).astype(BF16)

    vec = pl.BlockSpec((1, d), lambda i: (0, 0))
    return pl.pallas_call(
        body, name="modnorm_fwd", grid=(s_len // ts,),
        in_specs=[pl.BlockSpec((ts, d), lambda i: (i, 0)), vec, vec, vec],
        out_specs=pl.BlockSpec((ts, d), lambda i: (i, 0)),
        out_shape=jax.ShapeDtypeStruct((s_len, d), BF16),
        compiler_params=_cparams("parallel"),
    )(x, g, sc, sh)


def _rope_tables(pos_col, inv_freq_row, ts):
    s_len = pos_col.shape[0]
    half = QK_ROPE // 2

    def body(p_ref, f_ref, c_ref, a_ref, b_ref):
        ang = p_ref[...] * f_ref[...]
        lane = lax.broadcasted_iota(jnp.int32, ang.shape, 1)
        cos, sin = jnp.cos(ang), jnp.sin(ang)
        c_ref[...] = jnp.where(lane < 2 * half, cos, 0.0)
        a_ref[...] = jnp.where(lane < half, -sin, 0.0)
        b_ref[...] = jnp.where((lane >= half) & (lane < 2 * half), sin, 0.0)

    out = jax.ShapeDtypeStruct((s_len, LANE), F32)
    blk = pl.BlockSpec((ts, LANE), lambda i: (i, 0))
    return pl.pallas_call(
        body, name="rope_tables", grid=(s_len // ts,),
        in_specs=[pl.BlockSpec((ts, 1), lambda i: (i, 0)), pl.BlockSpec((1, LANE), lambda i: (0, 0))],
        out_specs=[blk, blk, blk], out_shape=[out, out, out],
        compiler_params=_cparams("parallel"),
    )(pos_col, inv_freq_row)


def _rope(seg, c, a, b):
    return seg * c + pltpu.roll(seg, LANE - QK_ROPE // 2, 1) * a + pltpu.roll(seg, QK_ROPE // 2, 1) * b


def _rope_t(seg, c, a, b):
    return seg * c - pltpu.roll(seg, LANE - QK_ROPE // 2, 1) * a - pltpu.roll(seg, QK_ROPE // 2, 1) * b


def _qkv_fwd(proj, lay, wuq, wk, wv, g_q, g_kv, ctab, atab, btab, n_heads, ts, scale):
    s_len = proj.shape[0]
    ql_w, kl_w = wuq.shape[0], wk.shape[0]

    def body(ql_ref, kl_ref, kr_ref, wuq_ref, wk_ref, wv_ref, gq_ref, gkv_ref, c_ref, a_ref, b_ref,
             q_out, k_out, v_out, qn_out, kvn_out):
        c, a, b = c_ref[...], a_ref[...], b_ref[...]
        ql = ql_ref[...].astype(F32)
        qn = (ql * _rsq(ql) * gq_ref[...]).astype(BF16)
        qn_out[...] = qn
        q = jnp.dot(qn, wuq_ref[...], preferred_element_type=F32)
        kl = kl_ref[...].astype(F32)
        kvn = (kl * _rsq(kl) * gkv_ref[...]).astype(BF16)
        kvn_out[...] = kvn
        kn = jnp.dot(kvn, wk_ref[...], preferred_element_type=F32)
        v_out[...] = jnp.dot(kvn, wv_ref[...], preferred_element_type=F32).astype(BF16)
        kr = _rope(kr_ref[...].astype(F32), c, a, b).astype(BF16)
        for h in range(n_heads):
            o = h * HEAD_W
            q_out[:, o:o + QK_NOPE] = (q[:, o:o + QK_NOPE] * scale).astype(BF16)
            q_out[:, o + QK_NOPE:o + HEAD_W] = (_rope(q[:, o + QK_NOPE:o + HEAD_W], c, a, b) * scale).astype(BF16)
            k_out[:, o:o + QK_NOPE] = kn[:, h * QK_NOPE:(h + 1) * QK_NOPE].astype(BF16)
            k_out[:, o + QK_NOPE:o + HEAD_W] = kr

    def full(arr):
        return pl.BlockSpec(arr.shape, lambda i: (0, 0))

    tab = pl.BlockSpec((ts, LANE), lambda i: (i, 0))
    hw, hv = n_heads * HEAD_W, n_heads * V_DIM
    return pl.pallas_call(
        body, name="qkv_fwd", grid=(s_len // ts,),
        in_specs=[pl.BlockSpec((ts, ql_w), lambda i: (i, lay["ql"] // ql_w)),
                  pl.BlockSpec((ts, kl_w), lambda i: (i, lay["kv"] // kl_w)),
                  pl.BlockSpec((ts, LANE), lambda i: (i, lay["kr"] // LANE)),
                  full(wuq), full(wk), full(wv), full(g_q), full(g_kv), tab, tab, tab],
        out_specs=[pl.BlockSpec((ts, hw), lambda i: (i, 0)), pl.BlockSpec((ts, hw), lambda i: (i, 0)),
                   pl.BlockSpec((ts, hv), lambda i: (i, 0)), pl.BlockSpec((ts, ql_w), lambda i: (i, 0)),
                   pl.BlockSpec((ts, kl_w), lambda i: (i, 0))],
        out_shape=[jax.ShapeDtypeStruct((s_len, hw), BF16), jax.ShapeDtypeStruct((s_len, hw), BF16),
                   jax.ShapeDtypeStruct((s_len, hv), BF16), jax.ShapeDtypeStruct((s_len, ql_w), BF16),
                   jax.ShapeDtypeStruct((s_len, kl_w), BF16)],
        compiler_params=_cparams("parallel"),
    )(proj, proj, proj, wuq, wk, wv, g_q, g_kv, ctab, atab, btab)


def _flash_fwd(q, k, v, n_heads, t, gather, out_cols):
    ng = len(gather)
    ck = _tile(t, PREF["attn_chunk"])
    hp = PREF["attn_heads"] if n_heads % PREF["attn_heads"] == 0 else 1
    n_groups = n_heads // hp
    s_len = q.shape[0]
    nb = s_len // t
    pairs = [(i, j) for i in range(nb) for j in range(i + 1)]
    itab = jnp.asarray(np.array([p[0] for p in pairs], np.int32))
    jtab = jnp.asarray(np.array([p[1] for p in pairs], np.int32))

    n_steps = len(pairs)

    def body(it_ref, jt_ref, q_ref, k_ref, v_ref, *rest):
        gin, (o_ref, lse_ref), gout = rest[:ng], rest[ng:ng + 2], rest[ng + 2:2 * ng + 2]
        m_sc, l_sc, acc_sc = rest[2 * ng + 2:2 * ng + 5]
        sems = rest[2 * ng + 5:]
        group, step_id = pl.program_id(0), pl.program_id(1)
        i, j = it_ref[step_id], jt_ref[step_id]

        @pl.when((group == 0) & (step_id == 0))
        def _():
            _gather_start(gin, gout, *sems)

        @pl.when((group == (3 * n_groups) // 4) & (step_id == 0))
        def _():
            _gather_forward(gin, gout, *sems)

        @pl.when(j == 0)
        def _():
            m_sc[...] = jnp.full(m_sc.shape, NEG, F32)
            l_sc[...] = jnp.zeros(l_sc.shape, F32)
            acc_sc[...] = jnp.zeros(acc_sc.shape, F32)

        def step(diag):
            for h in range(hp):
                qk, vc = slice(h * HEAD_W, (h + 1) * HEAD_W), slice(h * V_DIM, (h + 1) * V_DIM)
                for c in range(t // ck):
                    q0 = c * ck if diag else 0
                    qs, ks = slice(q0, t), slice(c * ck, (c + 1) * ck)
                    s_t = lax.dot_general(k_ref[ks, qk], q_ref[qs, qk], NT_DIMS, preferred_element_type=F32)
                    if diag:
                        krow = lax.broadcasted_iota(jnp.int32, s_t.shape, 0)
                        qcol = lax.broadcasted_iota(jnp.int32, s_t.shape, 1)
                        s_t = jnp.where(krow <= qcol, s_t, NEG)
                    m_prev = m_sc[h, :, qs]
                    m_new = jnp.maximum(m_prev, jnp.max(s_t, axis=0, keepdims=True))
                    alpha = jnp.exp2(m_prev - m_new)
                    p_t = jnp.exp2(s_t - m_new)
                    l_sc[h, :, qs] = alpha * l_sc[h, :, qs] + jnp.sum(p_t, axis=0, keepdims=True)
                    acc_sc[h, :, qs] = acc_sc[h, :, qs] * alpha + lax.dot_general(
                        v_ref[ks, vc], p_t.astype(BF16), TN_DIMS, preferred_element_type=F32)
                    m_sc[h, :, qs] = m_new

        @pl.when(j < i)
        def _():
            step(False)

        @pl.when(j == i)
        def _():
            step(True)
            for h in range(hp):
                l = l_sc[h]
                o_ref[:, h * V_DIM:(h + 1) * V_DIM] = jnp.transpose(acc_sc[h] / l).astype(BF16)
                lse_ref[h] = m_sc[h] + jnp.log(l) * LOG2E

        @pl.when((group == n_groups - 1) & (step_id == n_steps - 1))
        def _():
            _gather_finish(gin, gout, *sems)

    hbm = pl.BlockSpec(memory_space=pl.ANY)
    grid_spec = pltpu.PrefetchScalarGridSpec(
        num_scalar_prefetch=2, grid=(n_groups, n_steps),
        in_specs=[pl.BlockSpec((t, hp * HEAD_W), lambda g, s, it, jt: (it[s], g)),
                  pl.BlockSpec((t, hp * HEAD_W), lambda g, s, it, jt: (jt[s], g)),
                  pl.BlockSpec((t, hp * V_DIM), lambda g, s, it, jt: (jt[s], g))] + [hbm] * ng,
        out_specs=[pl.BlockSpec((t, hp * V_DIM), lambda g, s, it, jt: (it[s], g)),
                   pl.BlockSpec((hp, 1, t), lambda g, s, it, jt: (g, 0, it[s]))] + [hbm] * ng,
        scratch_shapes=[pltpu.VMEM((hp, 1, t), F32), pltpu.VMEM((hp, 1, t), F32), pltpu.VMEM((hp, V_DIM, t), F32)]
        + _comm_scratch(ng),
    )
    return pl.pallas_call(
        body, name="flash_fwd", grid_spec=grid_spec,
        out_shape=[jax.ShapeDtypeStruct((s_len, out_cols), BF16),
                   jax.ShapeDtypeStruct((n_heads, 1, s_len), F32)] + _gathered_shapes(gather),
        compiler_params=_cparams("arbitrary", "arbitrary"),
    )(itab, jtab, q, k, v, *gather)


def _gconv_fwd(proj, lay, w, b, ts, tc, mixcat):
    s_len = proj.shape[0]
    cw = w.shape[1]
    nj = cw // tc
    out_off = mixcat.shape[1] - cw

    def body(gb_ref, gc_ref, ci_ref, gch_ref, cih_ref, w_ref, b_ref, mix_in, o_ref):
        i = pl.program_id(1)
        p = gc_ref[...].astype(F32) * ci_ref[...].astype(F32)
        ph = jnp.where(i > 0, gch_ref[...].astype(F32) * cih_ref[...].astype(F32), 0.0)
        o_ref[...] = (gb_ref[...].astype(F32) * _conv3(p, ph, w_ref, b_ref)).astype(BF16)

    def blk(off):
        return pl.BlockSpec((ts, tc), lambda j, i: (i, off // tc + j))

    def halo(off):
        return pl.BlockSpec((SUB, tc), _prev_halo(ts, lambda j: off // tc + j))

    return pl.pallas_call(
        body, name="gconv_fwd", grid=(nj, s_len // ts),
        in_specs=[blk(lay["gb"]), blk(lay["gc"]), blk(lay["ci"]), halo(lay["gc"]), halo(lay["ci"]),
                  pl.BlockSpec((3, tc), lambda j, i: (0, j)), pl.BlockSpec((1, tc), lambda j, i: (0, j)),
                  pl.BlockSpec(memory_space=pl.ANY)],
        out_specs=pl.BlockSpec((ts, tc), lambda j, i: (i, out_off // tc + j)),
        out_shape=jax.ShapeDtypeStruct(mixcat.shape, BF16),
        input_output_aliases={7: 0},
        compiler_params=_cparams("parallel", "parallel"),
    )(proj, proj, proj, proj, proj, w, b, mixcat)


def _post_mix_fwd(mix, x, gt, g_post, g_pre, sc, sh, ts):
    s_len, d = x.shape

    def body(mix_ref, x_ref, gt_ref, gp_ref, g2_ref, sc_ref, sh_ref, x1_ref, h2_ref):
        mv = mix_ref[...].astype(F32)
        x1 = x_ref[...] + gt_ref[...] * (mv * _rsq(mv) * gp_ref[...])
        x1_ref[...] = x1
        h2_ref[...] = ((x1 * _rsq(x1) * g2_ref[...]) * (1.0 + sc_ref[...]) + sh_ref[...]).astype(BF16)

    vec = pl.BlockSpec((1, d), lambda i: (0, 0))
    row = pl.BlockSpec((ts, d), lambda i: (i, 0))
    return pl.pallas_call(
        body, name="post_mix_fwd", grid=(s_len // ts,),
        in_specs=[row, row, vec, vec, vec, vec, vec], out_specs=[row, row],
        out_shape=[jax.ShapeDtypeStruct((s_len, d), F32), jax.ShapeDtypeStruct((s_len, d), BF16)],
        compiler_params=_cparams("parallel"),
    )(mix, x, gt, g_post, g_pre, sc, sh)


def _act_fwd(up, w, b, ts, tc):
    s_len, f2 = up.shape
    nh = f2 // 2 // tc
    rs_n = _tile(ts, STRIP_ROWS, HALO)

    def body(up_ref, uph_ref, w_ref, b_ref, u_ref, o_ref):
        i = pl.program_id(1)

        def conv_strip(r0, lanes):
            if r0 == 0:
                top = jnp.where(i > 0, uph_ref[:, lanes].astype(F32), 0.0)
                xe = jnp.concatenate([top, up_ref[0:rs_n, lanes].astype(F32)], axis=0)
            else:
                xe = up_ref[r0 - HALO:r0 + rs_n, lanes].astype(F32)
            u = (pltpu.roll(xe, 2, 0)[HALO:] * w_ref[0:1, lanes] + pltpu.roll(xe, 1, 0)[HALO:] * w_ref[1:2, lanes]
                 + xe[HALO:] * w_ref[2:3, lanes] + b_ref[:, lanes])
            u_ref[r0:r0 + rs_n, lanes] = u.astype(BF16)
            return u

        for r0 in range(0, ts, rs_n):
            for c0 in range(0, tc, LANE):
                ua = conv_strip(r0, slice(c0, c0 + LANE))
                ug = conv_strip(r0, slice(tc + c0, tc + c0 + LANE))
                o_ref[r0:r0 + rs_n, c0:c0 + LANE] = (ug * _sigmoid(ug) * ua).astype(BF16)

    pair = pl.BlockSpec((ts, 2 * tc), lambda j, i: (i, j))
    return pl.pallas_call(
        body, name="act_fwd", grid=(nh, s_len // ts),
        in_specs=[pair, pl.BlockSpec((HALO, 2 * tc), lambda j, i: (jnp.maximum(i * (ts // HALO) - 1, 0), j)),
                  pl.BlockSpec((3, 2 * tc), lambda j, i: (0, j)), pl.BlockSpec((1, 2 * tc), lambda j, i: (0, j))],
        out_specs=[pair, pl.BlockSpec((ts, tc), lambda j, i: (i, j))],
        out_shape=[jax.ShapeDtypeStruct((s_len, f2), BF16), jax.ShapeDtypeStruct((s_len, f2 // 2), BF16)],
        compiler_params=_cparams("parallel", "parallel"),
    )(up, up, w, b)


def _final_bwd(y, x1, tgt, gt, g_post, ts):
    s_len, d = y.shape

    def body(y_ref, x1_ref, t_ref, gt_ref, g_ref, dy_ref, dx2_ref, loss_ref, dgt_ref, dg_ref):
        i = pl.program_id(0)
        yv = y_ref[...].astype(F32)
        r = _rsq(yv)
        yh = yv * r
        n = yh * g_ref[...]
        e = x1_ref[...] + gt_ref[...] * n - t_ref[...]
        loss = 0.5 * jnp.sum(jnp.mean(e * e, axis=-1, keepdims=True), axis=0, keepdims=True)
        dx2 = e * (1.0 / d)
        dx2_ref[...] = dx2
        dn = dx2 * gt_ref[...]
        dyh = dn * g_ref[...]
        dy_ref[...] = (r * (dyh - yh * jnp.mean(dyh * yh, axis=-1, keepdims=True))).astype(BF16)

        @pl.when(i == 0)
        def _():
            loss_ref[...] = jnp.zeros(loss_ref.shape, F32)
            dgt_ref[...] = jnp.zeros(dgt_ref.shape, F32)
            dg_ref[...] = jnp.zeros(dg_ref.shape, F32)

        loss_ref[...] += jnp.broadcast_to(loss, loss_ref.shape)
        dgt_ref[...] += _colsum(dx2 * n)
        dg_ref[...] += _colsum(dn * yh)

    vec = pl.BlockSpec((1, d), lambda i: (0, 0))
    row = pl.BlockSpec((ts, d), lambda i: (i, 0))
    vshape = jax.ShapeDtypeStruct((1, d), F32)
    return pl.pallas_call(
        body, name="final_bwd", grid=(s_len // ts,),
        in_specs=[row, row, row, vec, vec],
        out_specs=[row, row, pl.BlockSpec((1, LANE), lambda i: (0, 0)), vec, vec],
        out_shape=[jax.ShapeDtypeStruct((s_len, d), BF16), jax.ShapeDtypeStruct((s_len, d), F32),
                   jax.ShapeDtypeStruct((1, LANE), F32), vshape, vshape],
        compiler_params=_cparams("arbitrary"),
    )(y, x1, tgt, gt, g_post)


def _ffn_act_bwd(d_act, u, up, w, ts, tc):
    s_len, f2 = up.shape
    nh = f2 // 2 // tc
    n_i = s_len // ts
    rs_n = _tile(ts, STRIP_ROWS, HALO)
    ext = rs_n + HALO

    def body(d_ref, dh_ref, u_ref, uh_ref, x_ref, w_ref, dx_ref, dw_ref, db_ref, acc):
        i = pl.program_id(1)
        acc[...] = jnp.zeros(acc.shape, F32)

        def below(ref, halo_ref, r0, lanes):
            if r0 + ext <= ts:
                return ref[r0:r0 + ext, lanes].astype(F32)
            bot = jnp.where(i < n_i - 1, halo_ref[:, lanes].astype(F32), 0.0)
            return jnp.concatenate([ref[r0:ts, lanes].astype(F32), bot], axis=0)

        def fold8(v):
            return jnp.sum(v.reshape(v.shape[0] // SUB, SUB, v.shape[1]), axis=0)

        def conv_bwd_strip(du, r0, lanes):
            du1, du2 = pltpu.roll(du, ext - 1, 0)[:rs_n], pltpu.roll(du, ext - 2, 0)[:rs_n]
            du0 = du[:rs_n]
            dx_ref[r0:r0 + rs_n, lanes] = (du0 * w_ref[2:3, lanes] + du1 * w_ref[1:2, lanes]
                                           + du2 * w_ref[0:1, lanes]).astype(BF16)
            xv = x_ref[r0:r0 + rs_n, lanes].astype(F32)
            acc[0, :, lanes] += fold8(du2 * xv)
            acc[1, :, lanes] += fold8(du1 * xv)
            acc[2, :, lanes] += fold8(du0 * xv)
            acc[3, :, lanes] += fold8(du0)

        for r0 in range(0, ts, rs_n):
            for c0 in range(0, tc, LANE):
                la, lg = slice(c0, c0 + LANE), slice(tc + c0, tc + c0 + LANE)
                dv = below(d_ref, dh_ref, r0, la)
                ua, ug = below(u_ref, uh_ref, r0, la), below(u_ref, uh_ref, r0, lg)
                sg = _sigmoid(ug)
                conv_bwd_strip(dv * (ug * sg), r0, la)
                conv_bwd_strip(dv * ua * (sg * (1.0 + ug * (1.0 - sg))), r0, lg)

        @pl.when(i == 0)
        def _():
            dw_ref[...] = jnp.zeros(dw_ref.shape, F32)
            db_ref[...] = jnp.zeros(db_ref.shape, F32)

        dw_ref[...] += jnp.concatenate([_colsum(acc[k]) for k in range(3)], axis=0)
        db_ref[...] += _colsum(acc[3])

    pair = pl.BlockSpec((ts, 2 * tc), lambda j, i: (i, j))

    def nxt(j, i):
        return (jnp.minimum((i + 1) * (ts // HALO), s_len // HALO - 1), j)

    return pl.pallas_call(
        body, name="ffn_act_bwd", grid=(nh, n_i),
        in_specs=[pl.BlockSpec((ts, tc), lambda j, i: (i, j)), pl.BlockSpec((HALO, tc), nxt),
                  pair, pl.BlockSpec((HALO, 2 * tc), nxt), pair, pl.BlockSpec((3, 2 * tc), lambda j, i: (0, j))],
        out_specs=[pair, pl.BlockSpec((3, 2 * tc), lambda j, i: (0, j)), pl.BlockSpec((1, 2 * tc), lambda j, i: (0, j))],
        out_shape=[jax.ShapeDtypeStruct((s_len, f2), BF16), jax.ShapeDtypeStruct((3, f2), F32),
                   jax.ShapeDtypeStruct((1, f2), F32)],
        scratch_shapes=[pltpu.VMEM((4, SUB, 2 * tc), F32)],
        compiler_params=_cparams("parallel", "arbitrary"),
    )(d_act, d_act, u, u, up, w)


def _mid_bwd(dh2, x1, dx2, mix, g_pre, sc, gt_m, g_post, ts):
    s_len, d = x1.shape

    def body(dh_ref, x1_ref, dx2_ref, mix_ref, g_ref, sc_ref, gt_ref, gp_ref,
             dx1_ref, dmix_ref, dsh_ref, dsc_ref, dg_ref, dgt_ref, dgp_ref):
        i = pl.program_id(0)
        dh = dh_ref[...].astype(F32)
        x1 = x1_ref[...]
        r1 = _rsq(x1)
        xh = x1 * r1
        dxh = dh * (1.0 + sc_ref[...]) * g_ref[...]
        dx1 = dx2_ref[...] + r1 * (dxh - xh * jnp.mean(dxh * xh, axis=-1, keepdims=True))
        dx1_ref[...] = dx1
        mv = mix_ref[...].astype(F32)
        rm = _rsq(mv)
        mh = mv * rm
        dn = dx1 * gt_ref[...]
        dmh = dn * gp_ref[...]
        dmix_ref[...] = (rm * (dmh - mh * jnp.mean(dmh * mh, axis=-1, keepdims=True))).astype(BF16)

        @pl.when(i == 0)
        def _():
            for ref in (dsh_ref, dsc_ref, dg_ref, dgt_ref, dgp_ref):
                ref[...] = jnp.zeros(ref.shape, F32)

        dsh_ref[...] += _colsum(dh)
        dsc_ref[...] += _colsum(dh * (xh * g_ref[...]))
        dg_ref[...] += _colsum(dh * (1.0 + sc_ref[...]) * xh)
        dgt_ref[...] += _colsum(dx1 * (mh * gp_ref[...]))
        dgp_ref[...] += _colsum(dn * mh)

    vec = pl.BlockSpec((1, d), lambda i: (0, 0))
    row = pl.BlockSpec((ts, d), lambda i: (i, 0))
    vshape = jax.ShapeDtypeStruct((1, d), F32)
    return pl.pallas_call(
        body, name="mid_bwd", grid=(s_len // ts,),
        in_specs=[row, row, row, row, vec, vec, vec, vec],
        out_specs=[row, row, vec, vec, vec, vec, vec],
        out_shape=[jax.ShapeDtypeStruct((s_len, d), F32), jax.ShapeDtypeStruct((s_len, d), BF16)] + [vshape] * 5,
        compiler_params=_cparams("arbitrary"),
    )(dh2, x1, dx2, mix, g_pre, sc, gt_m, g_post)


def _first_bwd(dh1, x, dx1, g_pre, sc, ts):
    s_len, d = x.shape

    def body(dh_ref, x_ref, dx1_ref, g_ref, sc_ref, dx_ref, dsh_ref, dsc_ref, dg_ref):
        i = pl.program_id(0)
        dh = dh_ref[...].astype(F32)
        xv = x_ref[...]
        r = _rsq(xv)
        xh = xv * r
        dxh = dh * (1.0 + sc_ref[...]) * g_ref[...]
        dx_ref[...] = dx1_ref[...] + r * (dxh - xh * jnp.mean(dxh * xh, axis=-1, keepdims=True))

        @pl.when(i == 0)
        def _():
            for ref in (dsh_ref, dsc_ref, dg_ref):
                ref[...] = jnp.zeros(ref.shape, F32)

        dsh_ref[...] += _colsum(dh)
        dsc_ref[...] += _colsum(dh * (xh * g_ref[...]))
        dg_ref[...] += _colsum(dh * (1.0 + sc_ref[...]) * xh)

    vec = pl.BlockSpec((1, d), lambda i: (0, 0))
    row = pl.BlockSpec((ts, d), lambda i: (i, 0))
    vshape = jax.ShapeDtypeStruct((1, d), F32)
    return pl.pallas_call(
        body, name="first_bwd", grid=(s_len // ts,),
        in_specs=[row, row, row, vec, vec], out_specs=[row, vec, vec, vec],
        out_shape=[jax.ShapeDtypeStruct((s_len, d), F32), vshape, vshape, vshape],
        compiler_params=_cparams("arbitrary"),
    )(dh1, x, dx1, g_pre, sc)


def _gconv_bwd(d_mixcat, proj, lay, w, b, ts, tc):
    s_len = proj.shape[0]
    cw = w.shape[1]
    n_i = s_len // ts
    dc_off = d_mixcat.shape[1] - cw

    def body(dc_ref, dch_ref, gb_ref, gbh_ref, gc_ref, gch_ref, ci_ref, cih_ref, w_ref, b_ref,
             dgb_ref, dgc_ref, dci_ref, dw_ref, db_ref):
        i = pl.program_id(1)
        gc, ci = gc_ref[...].astype(F32), ci_ref[...].astype(F32)
        p = gc * ci
        ph = jnp.where(i > 0, gch_ref[...].astype(F32) * cih_ref[...].astype(F32), 0.0)
        pm1, pm2 = _shift_down(p, ph, 1), _shift_down(p, ph, 2)
        z = pm2 * w_ref[0:1, :] + pm1 * w_ref[1:2, :] + p * w_ref[2:3, :] + b_ref[...]
        dc = dc_ref[...].astype(F32)
        dgb_ref[...] = (dc * z).astype(BF16)
        dz = dc * gb_ref[...].astype(F32)
        dzh = jnp.where(i < n_i - 1, dch_ref[...].astype(F32) * gbh_ref[...].astype(F32), 0.0)
        dz1, dz2 = _shift_up(dz, dzh, 1), _shift_up(dz, dzh, 2)
        dp = dz * w_ref[2:3, :] + dz1 * w_ref[1:2, :] + dz2 * w_ref[0:1, :]
        dgc_ref[...] = (dp * ci).astype(BF16)
        dci_ref[...] = (dp * gc).astype(BF16)

        @pl.when(i == 0)
        def _():
            dw_ref[...] = jnp.zeros(dw_ref.shape, F32)
            db_ref[...] = jnp.zeros(db_ref.shape, F32)

        dw_ref[0:1, :] += _colsum(dz2 * p)
        dw_ref[1:2, :] += _colsum(dz1 * p)
        dw_ref[2:3, :] += _colsum(dz * p)
        db_ref[...] += _colsum(dz)

    def blk(off):
        return pl.BlockSpec((ts, tc), lambda j, i: (i, off // tc + j))

    def prev(off):
        return pl.BlockSpec((SUB, tc), _prev_halo(ts, lambda j: off // tc + j))

    def nxt(off):
        return pl.BlockSpec((SUB, tc), _next_halo(ts, s_len, lambda j: off // tc + j))

    out_blk = pl.BlockSpec((ts, tc), lambda j, i: (i, j))
    act = jax.ShapeDtypeStruct((s_len, cw), BF16)
    return pl.pallas_call(
        body, name="gconv_bwd", grid=(cw // tc, n_i),
        in_specs=[blk(dc_off), nxt(dc_off), blk(lay["gb"]), nxt(lay["gb"]), blk(lay["gc"]), prev(lay["gc"]),
                  blk(lay["ci"]), prev(lay["ci"]),
                  pl.BlockSpec((3, tc), lambda j, i: (0, j)), pl.BlockSpec((1, tc), lambda j, i: (0, j))],
        out_specs=[out_blk, out_blk, out_blk,
                   pl.BlockSpec((3, tc), lambda j, i: (0, j)), pl.BlockSpec((1, tc), lambda j, i: (0, j))],
        out_shape=[act, act, act, jax.ShapeDtypeStruct((3, cw), F32), jax.ShapeDtypeStruct((1, cw), F32)],
        compiler_params=_cparams("parallel", "arbitrary"),
    )(d_mixcat, d_mixcat, proj, proj, proj, proj, proj, proj, w, b)


def _delta(o, d_mixcat, n_heads, ts):
    s_len = o.shape[0]

    def body(o_ref, do_ref, out_ref):
        for h in range(n_heads):
            sl = slice(h * V_DIM, (h + 1) * V_DIM)
            prod = o_ref[:, sl].astype(F32) * do_ref[:, sl].astype(F32)
            out_ref[h] = jnp.broadcast_to(jnp.sum(prod, axis=1, keepdims=True), (ts, LANE))

    hv = n_heads * V_DIM
    return pl.pallas_call(
        body, name="attn_delta", grid=(s_len // ts,),
        in_specs=[pl.BlockSpec((ts, hv), lambda i: (i, 0)), pl.BlockSpec((ts, hv), lambda i: (i, 0))],
        out_specs=pl.BlockSpec((n_heads, ts, LANE), lambda i: (0, i, 0)),
        out_shape=jax.ShapeDtypeStruct((n_heads, s_len, LANE), F32),
        compiler_params=_cparams("parallel"),
    )(o, d_mixcat)


def _flash_bwd(q, k, v, d_mixcat, lse_row, delta_row, n_heads, t, scale, exchange):
    nx = len(exchange)
    s_len = q.shape[0]
    nb = s_len // t
    pairs = [(j, i) for j in range(nb) for i in range(j, nb)]
    jtab = jnp.asarray(np.array([p[0] for p in pairs], np.int32))
    itab = jnp.asarray(np.array([p[1] for p in pairs], np.int32))
    n_steps = len(pairs)

    def body(jt_ref, it_ref, q_ref, k_ref, v_ref, do_ref, lse_ref, dl_ref, *rest):
        xin, (dq_ref, dk_ref, dv_ref), xout = rest[:nx], rest[nx:nx + 3], rest[nx + 3:2 * nx + 3]
        dq_acc, dk_acc, dv_acc = rest[2 * nx + 3:2 * nx + 6]
        sems = rest[2 * nx + 6:]
        head, step_id = pl.program_id(0), pl.program_id(1)
        j, i = jt_ref[step_id], it_ref[step_id]

        @pl.when((head == 0) & (step_id == 0))
        def _():
            _exchange_start(xin, xout, *sems)

        @pl.when(step_id == 0)
        def _():
            dq_acc[...] = jnp.zeros(dq_acc.shape, F32)

        @pl.when(i == j)
        def _():
            dk_acc[...] = jnp.zeros(dk_acc.shape, F32)
            dv_acc[...] = jnp.zeros(dv_acc.shape, F32)

        def step(diag):
            qv, kv, vv, dov = q_ref[...], k_ref[...], v_ref[...], do_ref[...]
            s_t = lax.dot_general(kv, qv, NT_DIMS, preferred_element_type=F32)
            if diag:
                krow = lax.broadcasted_iota(jnp.int32, s_t.shape, 0)
                qcol = lax.broadcasted_iota(jnp.int32, s_t.shape, 1)
                s_t = jnp.where(krow <= qcol, s_t, NEG)
            p_t = jnp.exp2(s_t - lse_ref[0])
            dv_acc[...] += jnp.dot(p_t.astype(BF16), dov, preferred_element_type=F32)
            dp_t = lax.dot_general(vv, dov, NT_DIMS, preferred_element_type=F32)
            ds_t = (p_t * (dp_t - dl_ref[0])).astype(BF16)
            dk_acc[...] += jnp.dot(ds_t, qv, preferred_element_type=F32)
            rows = pl.ds(pl.multiple_of(i * t, t), t)
            dq_acc[rows, :] += lax.dot_general(ds_t, kv, TN_DIMS, preferred_element_type=F32)

        @pl.when(i > j)
        def _():
            step(False)

        @pl.when(i == j)
        def _():
            step(True)

        @pl.when(i == nb - 1)
        def _():
            dk_ref[...] = (dk_acc[...] * LN2).astype(BF16)
            dv_ref[...] = dv_acc[...].astype(BF16)

        @pl.when(step_id == n_steps - 1)
        def _():
            dq_ref[...] = (dq_acc[...] * scale).astype(BF16)

        @pl.when((head == n_heads - 1) & (step_id == n_steps - 1))
        def _():
            _exchange_finish(xin, xout, *sems)

    hbm = pl.BlockSpec(memory_space=pl.ANY)
    hv = n_heads * V_DIM
    do_off = 0
    grid_spec = pltpu.PrefetchScalarGridSpec(
        num_scalar_prefetch=2, grid=(n_heads, n_steps),
        in_specs=[pl.BlockSpec((t, HEAD_W), lambda h, s, jt, it: (it[s], h)),
                  pl.BlockSpec((t, HEAD_W), lambda h, s, jt, it: (jt[s], h)),
                  pl.BlockSpec((t, V_DIM), lambda h, s, jt, it: (jt[s], h)),
                  pl.BlockSpec((t, V_DIM), lambda h, s, jt, it: (it[s], do_off + h)),
                  pl.BlockSpec((1, 1, t), lambda h, s, jt, it: (h, 0, it[s])),
                  pl.BlockSpec((1, 1, t), lambda h, s, jt, it: (h, 0, it[s]))] + [hbm] * nx,
        out_specs=[pl.BlockSpec((s_len, HEAD_W), lambda h, s, jt, it: (0, h)),
                   pl.BlockSpec((t, HEAD_W), lambda h, s, jt, it: (jt[s], h)),
                   pl.BlockSpec((t, V_DIM), lambda h, s, jt, it: (jt[s], h))] + [hbm] * nx,
        scratch_shapes=[pltpu.VMEM((s_len, HEAD_W), F32), pltpu.VMEM((t, HEAD_W), F32), pltpu.VMEM((t, V_DIM), F32)]
        + _comm_scratch(nx),
    )
    return pl.pallas_call(
        body, name="flash_bwd", grid_spec=grid_spec,
        out_shape=[jax.ShapeDtypeStruct((s_len, n_heads * HEAD_W), BF16),
                   jax.ShapeDtypeStruct((s_len, n_heads * HEAD_W), BF16),
                   jax.ShapeDtypeStruct((s_len, hv), BF16)] + [jax.ShapeDtypeStruct(c.shape, c.dtype) for c in exchange],
        compiler_params=_cparams("arbitrary", "arbitrary"),
    )(jtab, itab, q, k, v, d_mixcat, lse_row, delta_row, *exchange)


def _qkv_bwd(dq, dk, dv, proj, lay, wuq, wk, wv, g_q, g_kv, ctab, atab, btab, n_heads, ts):
    s_len = proj.shape[0]
    ql_w, kl_w = wuq.shape[0], wk.shape[0]
    tail_w = lay["np"] - lay["ql"]
    kv_o, kr_o = lay["kv"] - lay["ql"], lay["kr"] - lay["ql"]

    def body(dq_ref, dk_ref, dv_ref, ql_ref, kl_ref, wuq_ref, wk_ref, wv_ref, gq_ref, gkv_ref, c_ref, a_ref, b_ref,
             dqr_ref, dkn_ref, tail_ref, dgq_ref, dgkv_ref):
        i = pl.program_id(0)
        c, a, b = c_ref[...], a_ref[...], b_ref[...]
        dkr = jnp.zeros((ts, LANE), F32)
        for h in range(n_heads):
            o = h * HEAD_W
            dqr_ref[:, o:o + QK_NOPE] = dq_ref[:, o:o + QK_NOPE]
            dqr_ref[:, o + QK_NOPE:o + HEAD_W] = _rope_t(dq_ref[:, o + QK_NOPE:o + HEAD_W].astype(F32), c, a, b).astype(BF16)
            dkn_ref[:, h * QK_NOPE:(h + 1) * QK_NOPE] = dk_ref[:, o:o + QK_NOPE]
            dkr = dkr + dk_ref[:, o + QK_NOPE:o + HEAD_W].astype(F32)
        tail_ref[...] = jnp.zeros(tail_ref.shape, BF16)
        tail_ref[:, kr_o:kr_o + LANE] = _rope_t(dkr, c, a, b).astype(BF16)

        def rms_bwd(lat_ref, dn, g_ref):
            lat = lat_ref[...].astype(F32)
            r = _rsq(lat)
            xh = lat * r
            dxh = dn * g_ref[...]
            return r * (dxh - xh * jnp.mean(dxh * xh, axis=-1, keepdims=True)), _colsum(dn * xh)

        dqn = lax.dot_general(dqr_ref[...], wuq_ref[...], NT_DIMS, preferred_element_type=F32)
        d_ql, dgq = rms_bwd(ql_ref, dqn, gq_ref)
        tail_ref[:, 0:ql_w] = d_ql.astype(BF16)
        dkvn = (lax.dot_general(dkn_ref[...], wk_ref[...], NT_DIMS, preferred_element_type=F32)
                + lax.dot_general(dv_ref[...], wv_ref[...], NT_DIMS, preferred_element_type=F32))
        d_kl, dgkv = rms_bwd(kl_ref, dkvn, gkv_ref)
        tail_ref[:, kv_o:kv_o + kl_w] = d_kl.astype(BF16)

        @pl.when(i == 0)
        def _():
            dgq_ref[...] = jnp.zeros(dgq_ref.shape, F32)
            dgkv_ref[...] = jnp.zeros(dgkv_ref.shape, F32)

        dgq_ref[...] += dgq
        dgkv_ref[...] += dgkv

    def full(arr):
        return pl.BlockSpec(arr.shape, lambda i: (0, 0))

    def rows(w):
        return pl.BlockSpec((ts, w), lambda i: (i, 0))

    tab = pl.BlockSpec((ts, LANE), lambda i: (i, 0))
    hw, hv, hn = n_heads * HEAD_W, n_heads * V_DIM, n_heads * QK_NOPE
    return pl.pallas_call(
        body, name="qkv_bwd", grid=(s_len // ts,),
        in_specs=[rows(hw), rows(hw), rows(hv),
                  pl.BlockSpec((ts, ql_w), lambda i: (i, lay["ql"] // ql_w)),
                  pl.BlockSpec((ts, kl_w), lambda i: (i, lay["kv"] // kl_w)),
                  full(wuq), full(wk), full(wv), full(g_q), full(g_kv), tab, tab, tab],
        out_specs=[rows(hw), rows(hn), rows(tail_w), full(g_q), full(g_kv)],
        out_shape=[jax.ShapeDtypeStruct((s_len, hw), BF16), jax.ShapeDtypeStruct((s_len, hn), BF16),
                   jax.ShapeDtypeStruct((s_len, tail_w), BF16),
                   jax.ShapeDtypeStruct(g_q.shape, F32), jax.ShapeDtypeStruct(g_kv.shape, F32)],
        compiler_params=_cparams("arbitrary"),
    )(dq, dk, dv, proj, proj, wuq, wk, wv, g_q, g_kv, ctab, atab, btab)


def _adamw(w, g, m, v):
    m = ADAM_B1 * m + (1.0 - ADAM_B1) * g
    v = ADAM_B2 * v + (1.0 - ADAM_B2) * (g * g)
    m_hat = m / (1.0 - ADAM_B1 ** ADAM_STEP)
    v_hat = v / (1.0 - ADAM_B2 ** ADAM_STEP)
    delta = -ADAM_LR * (m_hat / (jnp.sqrt(v_hat) + ADAM_EPS) + ADAM_WD * w)
    return delta, m, v


def _adam_parts(parts, w, m, v, tr, name):
    r, c = w.shape
    tr = _tile(r, tr, SUB)

    def body(p_ref, w_ref, m_ref, v_ref, g_out, d_out, m_out, v_out):
        g = p_ref[0].astype(F32)
        for dev in range(1, N_DEV):
            g = g + p_ref[dev].astype(F32)
        g_out[...] = g
        d_out[...], m_out[...], v_out[...] = _adamw(w_ref[...], g, m_ref[...], v_ref[...])

    blk = pl.BlockSpec((tr, c), lambda i: (i, 0))
    shp = jax.ShapeDtypeStruct((r, c), F32)
    return pl.pallas_call(
        body, name=name, grid=(r // tr,),
        in_specs=[pl.BlockSpec((N_DEV, tr, c), lambda i: (0, i, 0)), blk, blk, blk],
        out_specs=[blk, blk, blk, blk], out_shape=[shp, shp, shp, shp],
        compiler_params=_cparams("parallel"),
    )(parts, w, m, v)


def _adam_ada(cact_t, dmod_sh, w, m, v, tr):
    r, c = w.shape

    def body(ct_ref, dm_ref, w_ref, m_ref, v_ref, g_out, d_out, m_out, v_out):
        g = jnp.dot(ct_ref[...], dm_ref[...], preferred_element_type=F32, precision=lax.Precision.HIGHEST)
        g_out[...] = g
        d_out[...], m_out[...], v_out[...] = _adamw(w_ref[...], g, m_ref[...], v_ref[...])

    blk = pl.BlockSpec((tr, c), lambda i: (i, 0))
    shp = jax.ShapeDtypeStruct((r, c), F32)
    return pl.pallas_call(
        body, name="adam_ada", grid=(r // tr,),
        in_specs=[pl.BlockSpec((tr, N_DEV), lambda i: (i, 0)), pl.BlockSpec((N_DEV, c), lambda i: (0, 0)), blk, blk, blk],
        out_specs=[blk, blk, blk, blk], out_shape=[shp, shp, shp, shp],
        compiler_params=_cparams("parallel"),
    )(cact_t, dmod_sh, w, m, v)


def _adam_small(v_all, offs, ws, ms, vs):
    n_par = len(ws)

    def body(p_ref, *refs):
        w_refs, m_refs, v_refs = refs[:n_par], refs[n_par:2 * n_par], refs[2 * n_par:3 * n_par]
        sum_ref = refs[3 * n_par]
        outs = refs[3 * n_par + 1:]
        g = p_ref[0:1, :]
        for dev in range(1, N_DEV):
            g = g + p_ref[dev:dev + 1, :]
        sum_ref[...] = g
        for p in range(n_par):
            n = w_refs[p].shape[1]
            gp = sum_ref[:, offs[p]:offs[p] + n]
            outs[p][...] = gp
            (outs[n_par + p][...], outs[2 * n_par + p][...], outs[3 * n_par + p][...]) = _adamw(
                w_refs[p][...], gp, m_refs[p][...], v_refs[p][...])

    vm = pl.BlockSpec(memory_space=pltpu.VMEM)
    shapes = [jax.ShapeDtypeStruct(w.shape, F32) for w in ws]
    out = pl.pallas_call(
        body, name="adam_small", in_specs=[vm] * (1 + 3 * n_par), out_specs=[vm] * (1 + 4 * n_par),
        out_shape=[jax.ShapeDtypeStruct((1, v_all.shape[1]), F32)] + shapes * 4, compiler_params=_cparams(),
    )(v_all, *ws, *ms, *vs)
    return out[0], [out[1 + k * n_par:1 + (k + 1) * n_par] for k in range(4)]


def _my_place():
    return lax.axis_index("x"), lax.axis_index("y"), lax.axis_index("c")


def _peer(place, k):
    x, y, c = place
    return (x ^ (k >> 2), y ^ ((k >> 1) & 1), c ^ (k & 1))


def _index(place):
    return 4 * place[0] + 2 * place[1] + place[2]


def _ada_fwd(vec, w_ada, b_ada_rows):
    lv = vec.shape[1]
    d, c = w_ada.shape

    def body(vec_ref, w_ref, b_ref, gath_ref, cact_ref, mod_ref, modsh, send_a, recv_a, send_b, recv_b, local_s):
        me = _my_place()
        my_i = _index(me)

        def gather_copy(k, to, src_row):
            row = gath_ref.at[pl.ds(src_row, 1), :]
            return pltpu.make_async_remote_copy(src_ref=row, dst_ref=row, send_sem=send_a.at[k], recv_sem=recv_a.at[k],
                                                device_id=to, device_id_type=MESH)

        own = pltpu.make_async_copy(vec_ref, gath_ref.at[pl.ds(my_i, 1), :], local_s.at[0])
        own.start()
        own.wait()
        sends = [gather_copy(k, _peer(me, k), my_i) for k in range(1, N_DEV)]
        for cp in sends:
            cp.start()
        for k in range(1, N_DEV):
            gather_copy(k, me, _index(_peer(me, k))).wait_recv()
        for cp in sends:
            cp.wait_send()

        c_all = gath_ref[:, 0:d]
        cact = c_all * _sigmoid(c_all)
        cact_ref[...] = cact
        modsh[...] = jnp.dot(cact, w_ref[...], preferred_element_type=F32, precision=lax.Precision.HIGHEST)

        def mod_copy(k, to, src_row, dst_row):
            return pltpu.make_async_remote_copy(src_ref=modsh.at[pl.ds(src_row, 1), :], dst_ref=mod_ref.at[pl.ds(dst_row, 1), :],
                                                send_sem=send_b.at[k], recv_sem=recv_b.at[k],
                                                device_id=to, device_id_type=MESH)

        own = pltpu.make_async_copy(modsh.at[pl.ds(my_i, 1), :], mod_ref.at[pl.ds(my_i, 1), :], local_s.at[1])
        own.start()
        sends = [mod_copy(k, _peer(me, k), _index(_peer(me, k)), my_i) for k in range(1, N_DEV)]
        for cp in sends:
            cp.start()
        for k in range(1, N_DEV):
            mod_copy(k, me, my_i, _index(_peer(me, k))).wait_recv()
        for cp in sends:
            cp.wait_send()
        own.wait()
        mod_ref[...] = mod_ref[...] + b_ref[...]

    vm = pl.BlockSpec(memory_space=pltpu.VMEM)
    return pl.pallas_call(
        body, name="ada_fwd", in_specs=[vm, vm, vm], out_specs=[vm, vm, vm],
        out_shape=[jax.ShapeDtypeStruct((N_DEV, lv), F32), jax.ShapeDtypeStruct((N_DEV, d), F32),
                   jax.ShapeDtypeStruct((N_DEV, c), F32)],
        scratch_shapes=[pltpu.VMEM((N_DEV, c), F32)] + [pltpu.SemaphoreType.DMA((N_DEV,))] * 4
        + [pltpu.SemaphoreType.DMA((2,))],
        compiler_params=pltpu.CompilerParams(vmem_limit_bytes=VMEM_LIMIT),
    )(vec, w_ada, b_ada_rows)


def _gather_small(vec):
    lv = vec.shape[1]

    def body(vec_ref, gath_ref, send_s, recv_s, local_s):
        me = _my_place()
        my_i = _index(me)

        def copy(k, to, src_row):
            row = gath_ref.at[pl.ds(src_row, 1), :]
            return pltpu.make_async_remote_copy(src_ref=row, dst_ref=row, send_sem=send_s.at[k], recv_sem=recv_s.at[k],
                                                device_id=to, device_id_type=MESH)

        own = pltpu.make_async_copy(vec_ref, gath_ref.at[pl.ds(my_i, 1), :], local_s)
        own.start()
        own.wait()
        sends = [copy(k, _peer(me, k), my_i) for k in range(1, N_DEV)]
        for cp in sends:
            cp.start()
        for k in range(1, N_DEV):
            copy(k, me, _index(_peer(me, k))).wait_recv()
        for cp in sends:
            cp.wait_send()

    vm = pl.BlockSpec(memory_space=pltpu.VMEM)
    return pl.pallas_call(
        body, name="gather_small", in_specs=[vm], out_specs=vm,
        out_shape=jax.ShapeDtypeStruct((N_DEV, lv), F32),
        scratch_shapes=[pltpu.SemaphoreType.DMA((N_DEV,))] * 2 + [pltpu.SemaphoreType.DMA],
        compiler_params=pltpu.CompilerParams(vmem_limit_bytes=VMEM_LIMIT),
    )(vec)


PER = N_DEV - 1


def _comm_scratch(n):
    return [pltpu.SemaphoreType.DMA((n * PER,)), pltpu.SemaphoreType.DMA((n * PER,)), pltpu.SemaphoreType.DMA((n,))]


def _gather_copies(ins, outs, send_s, recv_s, local_s):
    n = len(ins)
    me = _my_place()
    x, y, c = me
    sibling = (x, y, 1 - c)
    chips = [(1 - x, y), (x, 1 - y), (1 - x, 1 - y)]

    def copy(a, k, block, to, src=None):
        slot = outs[a].at[_index(block)]
        return pltpu.make_async_remote_copy(src_ref=slot if src is None else src, dst_ref=slot,
                                            send_sem=send_s.at[a * PER + k], recv_sem=recv_s.at[a * PER + k],
                                            device_id=to, device_id_type=MESH)

    mine = [pltpu.make_async_copy(ins[a], outs[a].at[_index(me)], local_s.at[a]) for a in range(n)]
    first = []
    for a in range(n):
        first.append(copy(a, 0, me, sibling, src=ins[a]))
        first += [copy(a, 1 + j, me, (*chip, c), src=ins[a]) for j, chip in enumerate(chips)]
    landed = [copy(a, 1 + j, (*chip, c), me) for j, chip in enumerate(chips) for a in range(n)]
    passed = [copy(a, 4 + j, (*chip, c), sibling) for j, chip in enumerate(chips) for a in range(n)]
    from_sibling = [copy(a, 0, sibling, me) for a in range(n)]
    from_sibling += [copy(a, 4 + j, (*chip, 1 - c), me) for a in range(n) for j, chip in enumerate(chips)]
    return mine, first, landed, passed, from_sibling


def _gather_start(*refs):
    mine, first, _, _, _ = _gather_copies(*refs)
    for cp in mine + first:
        cp.start()


def _gather_forward(*refs):
    _, _, landed, passed, _ = _gather_copies(*refs)
    for got, fwd in zip(landed, passed):
        got.wait_recv()
        fwd.start()


def _gather_finish(*refs):
    mine, first, _, passed, from_sibling = _gather_copies(*refs)
    for cp in from_sibling:
        cp.wait_recv()
    for cp in first + passed:
        cp.wait_send()
    for cp in mine:
        cp.wait()


def _exchange_copies(ins, outs, send_s, recv_s, local_s):
    n = len(ins)
    me = _my_place()
    my_i = _index(me)

    def copy(a, k, to, src_slot, dst_slot):
        return pltpu.make_async_remote_copy(src_ref=ins[a].at[src_slot], dst_ref=outs[a].at[dst_slot],
                                            send_sem=send_s.at[a * PER + k - 1], recv_sem=recv_s.at[a * PER + k - 1],
                                            device_id=to, device_id_type=MESH)

    mine = [pltpu.make_async_copy(ins[a].at[my_i], outs[a].at[my_i], local_s.at[a]) for a in range(n)]
    sends = [copy(a, k, _peer(me, k), _index(_peer(me, k)), my_i) for k in range(1, N_DEV) for a in range(n)]
    recvs = [copy(a, k, me, my_i, _index(_peer(me, k))) for k in range(1, N_DEV) for a in range(n)]
    return mine, sends, recvs


def _exchange_start(*refs):
    mine, sends, _ = _exchange_copies(*refs)
    for cp in mine + sends:
        cp.start()


def _exchange_finish(*refs):
    mine, sends, recvs = _exchange_copies(*refs)
    for cp in recvs:
        cp.wait_recv()
    for cp in sends:
        cp.wait_send()
    for cp in mine:
        cp.wait()


def _gathered_shapes(shards):
    return [jax.ShapeDtypeStruct((N_DEV,) + s.shape, s.dtype) for s in shards]


def _gather_weights(shards):
    n = len(shards)

    def body(*refs):
        parts = (refs[:n], refs[n:2 * n]) + tuple(refs[2 * n:])
        _gather_start(*parts)
        _gather_forward(*parts)
        _gather_finish(*parts)

    hbm = pl.BlockSpec(memory_space=pl.ANY)
    return pl.pallas_call(
        body, name="gather_weights", in_specs=[hbm] * n, out_specs=[hbm] * n,
        out_shape=_gathered_shapes(shards), scratch_shapes=_comm_scratch(n),
    )(*shards)


def _proj_layout(cw, ql, kl):
    lay = {"gb": 0, "gc": cw, "ci": 2 * cw, "ql": 3 * cw}
    assert lay["ql"] % ql == 0
    lay["kv"] = _roundup(lay["ql"] + ql, kl)
    lay["kr"] = lay["kv"] + kl
    lay["np"] = _roundup(lay["kr"] + LANE, 4 * LANE)
    return lay


def _chunks_cols(g):
    r, c8 = g.shape
    return jnp.transpose(g.reshape(r, N_DEV, c8 // N_DEV), (1, 0, 2))


def _from_col_shards(a):
    n, r, c = a.shape
    return jnp.transpose(a, (1, 0, 2)).reshape(r, n * c)


def kernel(x, c, positions, w_ada, b_ada, g_pre_mix, g_post_mix, w_in, g_q, w_uq, g_kv, w_ukv, conv_w_mix, conv_b_mix, w_o, g_pre_ffn, g_post_ffn, w_up, conv_w_ffn, conv_b_ffn, w_down, loss_target, m_w_ada, m_b_ada, m_g_pre_mix, m_g_post_mix, m_w_in, m_g_q, m_w_uq, m_g_kv, m_w_ukv, m_conv_w_mix, m_conv_b_mix, m_w_o, m_g_pre_ffn, m_g_post_ffn, m_w_up, m_conv_w_ffn, m_conv_b_ffn, m_w_down, v_w_ada, v_b_ada, v_g_pre_mix, v_g_post_mix, v_w_in, v_g_q, v_w_uq, v_g_kv, v_w_ukv, v_conv_w_mix, v_conv_b_mix, v_w_o, v_g_pre_ffn, v_g_post_ffn, v_w_up, v_conv_w_ffn, v_conv_b_ffn, v_w_down):
    s_len, d = x.shape[1], x.shape[2]
    ql, kl = w_uq.shape[1], w_ukv.shape[1]
    n_heads = w_ukv.shape[2] * N_DEV // (QK_NOPE + V_DIM)
    cw = conv_w_mix.shape[2] * N_DEV
    f2 = w_up.shape[2] * N_DEV
    ff = f2 // 2
    in_cols = w_in.shape[2] * N_DEV
    ada_c = w_ada.shape[2]
    cwm_c, cwf_c = conv_w_mix.shape[2], conv_w_ffn.shape[2]
    scale = 1.0 / math.sqrt(QK_NOPE + QK_ROPE)
    lay = _proj_layout(cw, ql, kl)
    n_pad = lay["np"]
    my_i = _index(_my_place())

    ts_row = _tile(s_len, PREF["row"], SUB)
    ts_conv = _tile(s_len, PREF["conv_rows"], SUB)
    tc_conv = _tile(cw, PREF["conv_cols"])
    tc_ffn = cwf_c
    ts_ffn = _tile(s_len, PREF["ffn_rows"], SUB)
    pair_order = [k // 2 + (k % 2) * (N_DEV // 2) for k in range(N_DEV)]
    pair_place = [pair_order.index(k) for k in range(N_DEV)]

    def paired(shards):
        return _from_col_shards(jnp.stack([shards[p] for p in pair_order]))

    def unpaired_chunks(g):
        ch = _chunks_cols(g)
        return jnp.stack([ch[p] for p in pair_place])
    ts_qkv = _tile(s_len, PREF["row"], SUB)
    t_attn = _tile(s_len, PREF["attn"])

    x2d, tgt = x[0], loss_target[0]

    vec = jnp.concatenate([c, conv_w_mix[0].reshape(1, -1), conv_w_ffn[0].reshape(1, -1)], axis=1)
    gath, cact, mod_rows = _ada_fwd(vec, w_ada[0], b_ada.reshape(N_DEV, ada_c))
    cwm_full = _from_col_shards(gath[:, d:d + 3 * cwm_c].reshape(N_DEV, 3, cwm_c))
    cwf_shards = gath[:, d + 3 * cwm_c:].reshape(N_DEV, 3, cwf_c)
    cwf_pair = paired(cwf_shards)
    cbf_pair = paired(jnp.transpose(conv_b_ffn.reshape(1, N_DEV, cwf_c), (1, 0, 2)))
    mod = mod_rows.reshape(1, N_DEV * ada_c)
    sh_m, sc_m, gt_m, sh_f, sc_f, gt_f = [mod[:, k * d:(k + 1) * d] for k in range(6)]

    (g_in,) = _gather_weights([w_in[0].astype(BF16)])
    win = _from_col_shards(g_in)
    cut = np.cumsum([0, ql, kl, QK_ROPE, cw, cw, cw])
    part = [win[:, cut[k]:cut[k + 1]] for k in range(6)]

    def zcols(n):
        return jnp.zeros((d, n), BF16)

    win_p = jnp.concatenate([part[3], part[4], part[5], part[0], zcols(lay["kv"] - lay["ql"] - ql), part[1],
                             part[2], zcols(n_pad - lay["kr"] - QK_ROPE)], axis=1)
    inv_freq = 1.0 / (ROPE_THETA ** (jnp.arange(0, QK_ROPE, 2, dtype=F32) / QK_ROPE))
    inv_row = jnp.tile(inv_freq, LANE // (QK_ROPE // 2)).reshape(1, LANE)
    ctab, atab, btab = _rope_tables(positions.astype(F32).reshape(s_len, 1), inv_row, _tile(s_len, 1024, SUB))

    h1 = _modnorm_fwd(x2d, g_pre_mix, sc_m, sh_m, ts_row)
    proj, g_uq, g_ukv = _matmul(h1, win_p, out_dtype=BF16, tm=1024, tn=1280, tk=2048, name="mm_proj",
                                gather=[w_uq[0].astype(BF16), w_ukv[0].astype(BF16)])
    wuq_p = jnp.pad(_from_col_shards(g_uq).reshape(ql, n_heads, QK_NOPE + QK_ROPE),
                    ((0, 0), (0, 0), (0, HEAD_W - QK_NOPE - QK_ROPE))).reshape(ql, n_heads * HEAD_W)
    wukv = _from_col_shards(g_ukv).reshape(kl, n_heads, QK_NOPE + V_DIM)
    wk = wukv[:, :, :QK_NOPE].reshape(kl, n_heads * QK_NOPE)
    wv = wukv[:, :, QK_NOPE:].reshape(kl, n_heads * V_DIM)
    q, k, v, qn, kvn = _qkv_fwd(proj, lay, wuq_p, wk, wv, g_q, g_kv, ctab, atab, btab, n_heads, ts_qkv, scale * LOG2E)
    attn, lse_row, g_o, g_up = _flash_fwd(q, k, v, n_heads, t_attn, [w_o[0].astype(BF16), w_up[0].astype(BF16)], d)
    wo = g_o.reshape(d, d)

    def pair_shard(j):
        return j // 2 + (j % 2) * (N_DEV // 2)
    mixcat = _gconv_fwd(proj, lay, cwm_full, conv_b_mix, ts_conv, tc_conv, attn)
    mix = _matmul(mixcat, wo, out_dtype=BF16, tm=512, tn=2048, tk=2048, name="mm_mix")
    x1, h2 = _post_mix_fwd(mix, x2d, gt_m, g_post_mix, g_pre_ffn, sc_f, sh_f, ts_row)
    up, g_down = _matmul(h2, g_up, out_dtype=BF16, tm=1024, tn=cwf_c, tk=2048, name="mm_up", b_shard_of=pair_shard,
                         gather=[w_down[0].astype(BF16)])
    wdown = g_down.reshape(ff, d)
    u, act = _act_fwd(up, cwf_pair, cbf_pair, ts_ffn, tc_ffn)
    y = _matmul(act, wdown, out_dtype=BF16, tm=1024, tn=512, tk=ff, name="mm_down")

    dy, dx2, loss_row, d_gt_f, dg_post_ffn = _final_bwd(y, x1, tgt, gt_f, g_post_ffn, ts_row)
    gw_down = _matmul(act, dy, ta=True, out_dtype=BF16, tm=1408, tn=1024, tk=2048, name="mm_gw_down")
    d_act = _matmul(dy, wdown, tb=True, out_dtype=BF16, tm=1024, tn=1408, tk=2048, name="mm_d_act")
    d_up, dcw_pair, dcb_pair = _ffn_act_bwd(d_act, u, up, cwf_pair, ts_ffn, tc_ffn)
    dcb_ffn = _from_col_shards(unpaired_chunks(dcb_pair))
    gw_up, p_down = _matmul(h2, d_up, ta=True, out_dtype=BF16, tm=1024, tn=cwf_c, tk=2048, name="mm_gw_up",
                            exchange=[gw_down.reshape(N_DEV, ff // N_DEV, d)], out_shard_of=pair_shard)
    dh2 = _matmul_pair_shards(d_up, g_up, out_dtype=BF16, tm=1024, tn=1024, name="mm_dh2")
    dx1, dmix, d_sh_f, d_sc_f, dg_pre_ffn, d_gt_m, dg_post_mix = _mid_bwd(
        dh2, x1, dx2, mix, g_pre_ffn, sc_f, gt_m, g_post_mix, ts_row)
    gw_o = _matmul(mixcat, dmix, ta=True, out_dtype=BF16, tm=1024, tn=1024, tk=2048, name="mm_gw_o")
    d_mixcat = _matmul(dmix, wo, tb=True, out_dtype=BF16, tm=1024, tn=1024, tk=2048, name="mm_d_mixcat")
    d_gb, d_gc, d_ci, dcw_mix, dcb_mix = _gconv_bwd(d_mixcat, proj, lay, cwm_full, conv_b_mix, ts_conv, tc_conv)
    delta = _delta(mixcat, d_mixcat, n_heads, _tile(s_len, 512, SUB))
    delta_row = delta[:, :, 0].reshape(n_heads, 1, s_len)
    dq, dk, dv, p_up, p_o = _flash_bwd(q, k, v, d_mixcat, lse_row, delta_row, n_heads, t_attn, scale,
                                       [gw_up, gw_o.reshape(N_DEV, d // N_DEV, d)])
    dq_r, dkn, d_tail, dg_q, dg_kv = _qkv_bwd(dq, dk, dv, proj, lay, wuq_p, wk, wv, g_q, g_kv, ctab, atab, btab,
                                              n_heads, ts_qkv)
    gw_uq_p = _matmul(qn, dq_r, ta=True, out_dtype=BF16, tm=768, tn=1024, tk=2048, name="mm_gw_uq")
    gw_k = _matmul(kvn, dkn, ta=True, out_dtype=BF16, tm=512, tn=1024, tk=2048, name="mm_gw_k")
    gw_v = _matmul(kvn, dv, ta=True, out_dtype=BF16, tm=512, tn=1024, tk=2048, name="mm_gw_v")
    d_proj = jnp.concatenate([d_gb, d_gc, d_ci, d_tail], axis=1)
    gw_uq = gw_uq_p.reshape(ql, n_heads, HEAD_W)[:, :, :QK_NOPE + QK_ROPE].reshape(ql, n_heads * (QK_NOPE + QK_ROPE))
    gw_ukv = jnp.concatenate([gw_k.reshape(kl, n_heads, QK_NOPE), gw_v.reshape(kl, n_heads, V_DIM)],
                             axis=2).reshape(kl, n_heads * (QK_NOPE + V_DIM))
    gw_in_p, p_uq, p_ukv, p_cwm, p_cwf = _matmul(
        h1, d_proj, ta=True, out_dtype=BF16, tm=1024, tn=1280, tk=2048, name="mm_gw_in",
        exchange=[_chunks_cols(gw_uq), _chunks_cols(gw_ukv), _chunks_cols(dcw_mix), unpaired_chunks(dcw_pair)])

    gw_in = jnp.concatenate([gw_in_p[:, lay["ql"]:lay["ql"] + ql], gw_in_p[:, lay["kv"]:lay["kv"] + kl],
                             gw_in_p[:, lay["kr"]:lay["kr"] + QK_ROPE], gw_in_p[:, :3 * cw]], axis=1)
    dh1, p_in = _matmul(d_proj, win_p, tb=True, out_dtype=BF16, tm=512, tn=1024, tk=n_pad, name="mm_dh1",
                        exchange=[_chunks_cols(gw_in)])
    grad_x, d_sh_m, d_sc_m, dg_pre_mix = _first_bwd(dh1, x2d, dx1, g_pre_mix, sc_m, ts_row)

    dmod = jnp.concatenate([d_sh_m, d_sc_m, d_gt_m, d_sh_f, d_sc_f, d_gt_f], axis=1)
    small_g = [loss_row, dmod, dg_pre_mix, dg_post_mix, dg_q, dg_kv, dcb_mix, dg_pre_ffn, dg_post_ffn, dcb_ffn]
    v_all = _gather_small(jnp.concatenate(small_g, axis=1))

    small_names = ["b_ada", "g_pre_mix", "g_post_mix", "g_q", "g_kv", "conv_b_mix", "g_pre_ffn", "g_post_ffn",
                   "conv_b_ffn"]
    small_w = [b_ada, g_pre_mix, g_post_mix, g_q, g_kv, conv_b_mix, g_pre_ffn, g_post_ffn, conv_b_ffn]
    small_m = [m_b_ada, m_g_pre_mix, m_g_post_mix, m_g_q, m_g_kv, m_conv_b_mix, m_g_pre_ffn, m_g_post_ffn, m_conv_b_ffn]
    small_v = [v_b_ada, v_g_pre_mix, v_g_post_mix, v_g_q, v_g_kv, v_conv_b_mix, v_g_pre_ffn, v_g_post_ffn, v_conv_b_ffn]
    offs = np.cumsum([0] + [g.shape[1] for g in small_g])
    g_sum, small_out = _adam_small(v_all, [int(o) for o in offs[1:-1]], small_w, small_m, small_v)
    small = [dict(zip(small_names, kind)) for kind in small_out]
    loss = g_sum[0, 0]

    dmod_sh = lax.dynamic_slice(v_all, (0, int(offs[1]) + my_i * ada_c), (N_DEV, ada_c))
    cact_t = jnp.transpose(cact)
    big = dict(
        w_ada=_adam_ada(cact_t, dmod_sh, w_ada[0], m_w_ada[0], v_w_ada[0], _tile(d, 256, SUB)),
        w_in=_adam_parts(p_in, w_in[0], m_w_in[0], v_w_in[0], 256, "adam_w_in"),
        w_uq=_adam_parts(p_uq, w_uq[0], m_w_uq[0], v_w_uq[0], 256, "adam_w_uq"),
        w_ukv=_adam_parts(p_ukv, w_ukv[0], m_w_ukv[0], v_w_ukv[0], 256, "adam_w_ukv"),
        w_o=_adam_parts(p_o, w_o[0], m_w_o[0], v_w_o[0], 128, "adam_w_o"),
        w_up=_adam_parts(p_up, w_up[0], m_w_up[0], v_w_up[0], 256, "adam_w_up"),
        w_down=_adam_parts(p_down, w_down[0], m_w_down[0], v_w_down[0], 176, "adam_w_down"),
        conv_w_mix=_adam_parts(p_cwm, conv_w_mix[0], m_conv_w_mix[0], v_conv_w_mix[0], 8, "adam_cw_mix"),
        conv_w_ffn=_adam_parts(p_cwf, conv_w_ffn[0], m_conv_w_ffn[0], v_conv_w_ffn[0], 8, "adam_cw_ffn"),
    )

    names = ["w_ada", "b_ada", "g_pre_mix", "g_post_mix", "w_in", "g_q", "w_uq", "g_kv", "w_ukv", "conv_w_mix",
             "conv_b_mix", "w_o", "g_pre_ffn", "g_post_ffn", "w_up", "conv_w_ffn", "conv_b_ffn", "w_down"]
    outs = [loss, grad_x[None]]
    for kind in range(4):
        for nm in names:
            outs.append(big[nm][kind][None] if nm in big else small[kind][nm])
    return tuple(outs)
```

```python
import math

import numpy as np
import jax
import jax.numpy as jnp
from jax import lax
from jax.experimental import pallas as pl
from jax.experimental.pallas import tpu as pltpu

F32 = jnp.float32
BF16 = jnp.bfloat16
N_DEV = 8
MESH = pl.DeviceIdType.MESH

QK_NOPE = 128
QK_ROPE = 64
V_DIM = 128
HEAD_W = 256
LANE = 128
SUB = 8
HALO = 16
STRIP_ROWS = 64
RMS_EPS = 1e-6
ROPE_THETA = 10000.0
ADAM_LR = 0.001
ADAM_B1 = 0.9
ADAM_B2 = 0.999
ADAM_EPS = 1e-08
ADAM_WD = 0.01
ADAM_STEP = 10
NEG = -1e30
LOG2E = 1.4426950408889634
LN2 = 0.6931471805599453
VMEM_LIMIT = 56 * 1024 * 1024

PREF = {"row": 256, "conv_rows": 512, "conv_cols": 512, "ffn_rows": 256, "attn": 1024, "attn_chunk": 1024, "attn_heads": 2}

NT_DIMS = (((1,), (1,)), ((), ()))
TN_DIMS = (((0,), (0,)), ((), ()))


def _cparams(*sem):
    return pltpu.CompilerParams(dimension_semantics=sem if sem else None, vmem_limit_bytes=VMEM_LIMIT)


def _tile(n, pref, unit=LANE):
    if n <= pref:
        return n
    t = (pref // unit) * unit
    while t >= unit:
        if n % t == 0:
            return t
        t -= unit
    return n


def _roundup(n, m):
    return (n + m - 1) // m * m


def _rsq(x):
    return lax.rsqrt(jnp.mean(x * x, axis=-1, keepdims=True) + RMS_EPS)


def _colsum(x):
    return jnp.sum(x, axis=0, keepdims=True)


def _sigmoid(x):
    return 1.0 / (1.0 + jnp.exp(-x))


def _matmul(a, b, *, ta=False, tb=False, out_dtype, tm, tn, tk, name, exchange=None, gather=None, b_shard_of=None,
            out_shard_of=None):
    m_dim, k_dim = (a.shape[1], a.shape[0]) if ta else a.shape
    if b_shard_of is not None:
        n_dim, tn = b.shape[0] * b.shape[2], b.shape[2]
    else:
        n_dim = b.shape[0] if tb else b.shape[1]
    if out_shard_of is not None:
        tn = n_dim // N_DEV
    tm, tn, tk = _tile(m_dim, tm), _tile(n_dim, tn), _tile(k_dim, tk)
    gi, gj, nk = m_dim // tm, n_dim // tn, k_dim // tk
    dims = (((0 if ta else 1,), (1 if tb else 0,)), ((), ()))
    chunks = list(exchange or gather or [])
    nx = len(chunks)
    comm_start, comm_finish = (_gather_start, _gather_finish) if gather else (_exchange_start, _exchange_finish)

    def body(*refs):
        a_ref, b_ref = refs[:2]
        xin, o_ref, xout = refs[2:2 + nx], refs[2 + nx], refs[3 + nx:3 + 2 * nx]
        scratch = refs[3 + 2 * nx:]
        sems = scratch[1:] if nk > 1 else scratch
        i, j, k = pl.program_id(0), pl.program_id(1), pl.program_id(2)
        if nx:
            @pl.when((i == 0) & (j == 0) & (k == 0))
            def _():
                comm_start(xin, xout, *sems)

        if gather:
            @pl.when((i == gi // 2) & (j == 0) & (k == 0))
            def _():
                _gather_forward(xin, xout, *sems)

        part = lax.dot_general(a_ref[...], b_ref[...], dims, preferred_element_type=F32)
        if nk == 1:
            o_ref[...] = part.astype(o_ref.dtype)
        else:
            acc_ref = scratch[0]

            @pl.when(k == 0)
            def _():
                acc_ref[...] = part

            @pl.when(k > 0)
            def _():
                acc_ref[...] += part

            @pl.when(k == nk - 1)
            def _():
                o_ref[...] = acc_ref[...].astype(o_ref.dtype)

        if nx:
            @pl.when((i == gi - 1) & (j == gj - 1) & (k == nk - 1))
            def _():
                comm_finish(xin, xout, *sems)

    a_spec = pl.BlockSpec((tk, tm), lambda i, j, k: (k, i)) if ta else pl.BlockSpec((tm, tk), lambda i, j, k: (i, k))
    b_spec = pl.BlockSpec((tn, tk), lambda i, j, k: (j, k)) if tb else pl.BlockSpec((tk, tn), lambda i, j, k: (k, j))
    if b_shard_of is not None:
        b_spec = pl.BlockSpec((None, tk, tn), lambda i, j, k: (b_shard_of(j), k, 0))
    o_spec, o_shape = pl.BlockSpec((tm, tn), lambda i, j, k: (i, j)), (m_dim, n_dim)
    if out_shard_of is not None:
        o_spec, o_shape = pl.BlockSpec((None, tm, tn), lambda i, j, k: (out_shard_of(j), i, 0)), (N_DEV, m_dim, tn)
    hbm = pl.BlockSpec(memory_space=pl.ANY)
    out = pl.pallas_call(
        body,
        name=name,
        grid=(gi, gj, nk),
        in_specs=[a_spec, b_spec] + [hbm] * nx,
        out_specs=[o_spec] + [hbm] * nx,
        out_shape=[jax.ShapeDtypeStruct(o_shape, out_dtype)]
        + (_gathered_shapes(chunks) if gather else [jax.ShapeDtypeStruct(c.shape, c.dtype) for c in chunks]),
        scratch_shapes=([pltpu.VMEM((tm, tn), F32)] if nk > 1 else []) + (_comm_scratch(nx) if nx else []),
        compiler_params=_cparams(*(("arbitrary",) * 3 if nx else ("parallel", "parallel", "arbitrary"))),
    )(a, b, *chunks)
    return out if nx else out[0]


def _matmul_pair_shards(a, shards, *, out_dtype, tm, tn, name):
    m_dim = a.shape[0]
    n_sh, n_dim, c = shards.shape
    half = n_sh // 2
    tm, tn = _tile(m_dim, tm), _tile(n_dim, tn)

    def body(a_ref, b0_ref, b1_ref, o_ref, acc_ref):
        k = pl.program_id(2)
        part = (lax.dot_general(a_ref[:, :c], b0_ref[...], NT_DIMS, preferred_element_type=F32)
                + lax.dot_general(a_ref[:, c:], b1_ref[...], NT_DIMS, preferred_element_type=F32))

        @pl.when(k == 0)
        def _():
            acc_ref[...] = part

        @pl.when(k > 0)
        def _():
            acc_ref[...] += part

        @pl.when(k == half - 1)
        def _():
            o_ref[...] = acc_ref[...].astype(o_ref.dtype)

    return pl.pallas_call(
        body, name=name, grid=(m_dim // tm, n_dim // tn, half),
        in_specs=[pl.BlockSpec((tm, 2 * c), lambda i, j, k: (i, k)),
                  pl.BlockSpec((None, tn, c), lambda i, j, k: (k, j, 0)),
                  pl.BlockSpec((None, tn, c), lambda i, j, k: (k + half, j, 0))],
        out_specs=pl.BlockSpec((tm, tn), lambda i, j, k: (i, j)),
        out_shape=jax.ShapeDtypeStruct((m_dim, n_dim), out_dtype),
        scratch_shapes=[pltpu.VMEM((tm, tn), F32)],
        compiler_params=_cparams("parallel", "parallel", "arbitrary"),
    )(a, shards, shards)


def _shift_down(x, halo, n):
    r = pltpu.roll(x, n, 0)
    hr = pltpu.roll(halo, n, 0)
    row = lax.broadcasted_iota(jnp.int32, halo.shape, 0)
    top = jnp.where(row < n, hr, r[:SUB])
    return jnp.concatenate([top, r[SUB:]], axis=0)


def _shift_up(x, halo, n):
    ts = x.shape[0]
    r = pltpu.roll(x, ts - n, 0)
    hr = pltpu.roll(halo, SUB - n, 0)
    row = lax.broadcasted_iota(jnp.int32, halo.shape, 0)
    bot = jnp.where(row >= SUB - n, hr, r[ts - SUB:])
    return jnp.concatenate([r[:ts - SUB], bot], axis=0)


def _conv3(x, halo, w_ref, b_ref):
    return _shift_down(x, halo, 2) * w_ref[0:1, :] + _shift_down(x, halo, 1) * w_ref[1:2, :] + x * w_ref[2:3, :] + b_ref[...]


def _prev_halo(ts, col):
    return lambda j, i: (jnp.maximum(i * (ts // SUB) - 1, 0), col(j))


def _next_halo(ts, n_rows, col):
    return lambda j, i: (jnp.minimum((i + 1) * (ts // SUB), n_rows // SUB - 1), col(j))


def _modnorm_fwd(x, g, sc, sh, ts, gather):
    s_len, d = x.shape
    n_i = s_len // ts
    ng = len(gather)

    def body(x_ref, g_ref, sc_ref, sh_ref, *rest):
        gin, h_ref, gout, sems = rest[:ng], rest[ng], rest[ng + 1:2 * ng + 1], rest[2 * ng + 1:]
        i = pl.program_id(0)

        @pl.when(i == 0)
        def _():
            _gather_start(gin, gout, *sems)

        xv = x_ref[...]
        h_ref[...] = ((xv * _rsq(xv) * g_ref[...]) * (1.0 + sc_ref[...]) + sh_ref[...]).astype(BF16)

        @pl.when(i == n_i - 1)
        def _():
            _gather_forward(gin, gout, *sems)
            _gather_finish(gin, gout, *sems)

    vec = pl.BlockSpec((1, d), lambda i: (0, 0))
    hbm = pl.BlockSpec(memory_space=pl.ANY)
    return pl.pallas_call(
        body, name="modnorm_fwd", grid=(n_i,),
        in_specs=[pl.BlockSpec((ts, d), lambda i: (i, 0)), vec, vec, vec] + [hbm] * ng,
        out_specs=[pl.BlockSpec((ts, d), lambda i: (i, 0))] + [hbm] * ng,
        out_shape=[jax.ShapeDtypeStruct((s_len, d), BF16)] + _gathered_shapes(gather),
        scratch_shapes=_comm_scratch(ng),
        compiler_params=_cparams("arbitrary"),
    )(x, g, sc, sh, *gather)


def _rope_tables(pos_col, inv_freq_row, ts):
    s_len = pos_col.shape[0]
    half = QK_ROPE // 2

    def body(p_ref, f_ref, c_ref, a_ref, b_ref):
        ang = p_ref[...] * f_ref[...]
        lane = lax.broadcasted_iota(jnp.int32, ang.shape, 1)
        cos, sin = jnp.cos(ang), jnp.sin(ang)
        c_ref[...] = jnp.where(lane < 2 * half, cos, 0.0)
        a_ref[...] = jnp.where(lane < half, -sin, 0.0)
        b_ref[...] = jnp.where((lane >= half) & (lane < 2 * half), sin, 0.0)

    out = jax.ShapeDtypeStruct((s_len, LANE), F32)
    blk = pl.BlockSpec((ts, LANE), lambda i: (i, 0))
    return pl.pallas_call(
        body, name="rope_tables", grid=(s_len // ts,),
        in_specs=[pl.BlockSpec((ts, 1), lambda i: (i, 0)), pl.BlockSpec((1, LANE), lambda i: (0, 0))],
        out_specs=[blk, blk, blk], out_shape=[out, out, out],
        compiler_params=_cparams("parallel"),
    )(pos_col, inv_freq_row)


def _rope(seg, c, a, b):
    return seg * c + pltpu.roll(seg, LANE - QK_ROPE // 2, 1) * a + pltpu.roll(seg, QK_ROPE // 2, 1) * b


def _rope_t(seg, c, a, b):
    return seg * c - pltpu.roll(seg, LANE - QK_ROPE // 2, 1) * a - pltpu.roll(seg, QK_ROPE // 2, 1) * b


def _qkv_fwd(proj, lay, wuq, wk, wv, g_q, g_kv, ctab, atab, btab, n_heads, ts, scale):
    s_len = proj.shape[0]
    ql_w, kl_w = wuq.shape[0], wk.shape[0]

    def body(ql_ref, kl_ref, kr_ref, wuq_ref, wk_ref, wv_ref, gq_ref, gkv_ref, c_ref, a_ref, b_ref,
             q_out, k_out, v_out, qn_out, kvn_out):
        c, a, b = c_ref[...], a_ref[...], b_ref[...]
        ql = ql_ref[...].astype(F32)
        qn = (ql * _rsq(ql) * gq_ref[...]).astype(BF16)
        qn_out[...] = qn
        q = jnp.dot(qn, wuq_ref[...], preferred_element_type=F32)
        kl = kl_ref[...].astype(F32)
        kvn = (kl * _rsq(kl) * gkv_ref[...]).astype(BF16)
        kvn_out[...] = kvn
        kn = jnp.dot(kvn, wk_ref[...], preferred_element_type=F32)
        v_out[...] = jnp.dot(kvn, wv_ref[...], preferred_element_type=F32).astype(BF16)
        kr = _rope(kr_ref[...].astype(F32), c, a, b).astype(BF16)
        for h in range(n_heads):
            o = h * HEAD_W
            q_out[:, o:o + QK_NOPE] = (q[:, o:o + QK_NOPE] * scale).astype(BF16)
            q_out[:, o + QK_NOPE:o + HEAD_W] = (_rope(q[:, o + QK_NOPE:o + HEAD_W], c, a, b) * scale).astype(BF16)
            k_out[:, o:o + QK_NOPE] = kn[:, h * QK_NOPE:(h + 1) * QK_NOPE].astype(BF16)
            k_out[:, o + QK_NOPE:o + HEAD_W] = kr

    def full(arr):
        return pl.BlockSpec(arr.shape, lambda i: (0, 0))

    tab = pl.BlockSpec((ts, LANE), lambda i: (i, 0))
    hw, hv = n_heads * HEAD_W, n_heads * V_DIM
    return pl.pallas_call(
        body, name="qkv_fwd", grid=(s_len // ts,),
        in_specs=[pl.BlockSpec((ts, ql_w), lambda i: (i, lay["ql"] // ql_w)),
                  pl.BlockSpec((ts, kl_w), lambda i: (i, lay["kv"] // kl_w)),
                  pl.BlockSpec((ts, LANE), lambda i: (i, lay["kr"] // LANE)),
                  full(wuq), full(wk), full(wv), full(g_q), full(g_kv), tab, tab, tab],
        out_specs=[pl.BlockSpec((ts, hw), lambda i: (i, 0)), pl.BlockSpec((ts, hw), lambda i: (i, 0)),
                   pl.BlockSpec((ts, hv), lambda i: (i, 0)), pl.BlockSpec((ts, ql_w), lambda i: (i, 0)),
                   pl.BlockSpec((ts, kl_w), lambda i: (i, 0))],
        out_shape=[jax.ShapeDtypeStruct((s_len, hw), BF16), jax.ShapeDtypeStruct((s_len, hw), BF16),
                   jax.ShapeDtypeStruct((s_len, hv), BF16), jax.ShapeDtypeStruct((s_len, ql_w), BF16),
                   jax.ShapeDtypeStruct((s_len, kl_w), BF16)],
        compiler_params=_cparams("parallel"),
    )(proj, proj, proj, wuq, wk, wv, g_q, g_kv, ctab, atab, btab)


def _flash_fwd(q, k, v, n_heads, t, gather, out_cols):
    ng = len(gather)
    ck = _tile(t, PREF["attn_chunk"])
    hp = PREF["attn_heads"] if n_heads % PREF["attn_heads"] == 0 else 1
    n_groups = n_heads // hp
    s_len = q.shape[0]
    nb = s_len // t
    pairs = [(i, j) for i in range(nb) for j in range(i + 1)]
    itab = jnp.asarray(np.array([p[0] for p in pairs], np.int32))
    jtab = jnp.asarray(np.array([p[1] for p in pairs], np.int32))

    n_steps = len(pairs)

    def body(it_ref, jt_ref, q_ref, k_ref, v_ref, *rest):
        gin, (o_ref, lse_ref), gout = rest[:ng], rest[ng:ng + 2], rest[ng + 2:2 * ng + 2]
        m_sc, l_sc, acc_sc = rest[2 * ng + 2:2 * ng + 5]
        sems = rest[2 * ng + 5:]
        group, step_id = pl.program_id(0), pl.program_id(1)
        i, j = it_ref[step_id], jt_ref[step_id]

        @pl.when((group == 0) & (step_id == 0))
        def _():
            _gather_start(gin, gout, *sems)

        @pl.when((group == (3 * n_groups) // 4) & (step_id == 0))
        def _():
            _gather_forward(gin, gout, *sems)

        @pl.when(j == 0)
        def _():
            m_sc[...] = jnp.full(m_sc.shape, NEG, F32)
            l_sc[...] = jnp.zeros(l_sc.shape, F32)
            acc_sc[...] = jnp.zeros(acc_sc.shape, F32)

        def step(diag):
            for h in range(hp):
                qk, vc = slice(h * HEAD_W, (h + 1) * HEAD_W), slice(h * V_DIM, (h + 1) * V_DIM)
                cd = min(ck, t // 2) if diag else ck
                for c in range(t // cd):
                    q0 = c * cd if diag else 0
                    qs, ks = slice(q0, t), slice(c * cd, (c + 1) * cd)
                    s_t = lax.dot_general(k_ref[ks, qk], q_ref[qs, qk], NT_DIMS, preferred_element_type=F32)
                    if diag:
                        krow = lax.broadcasted_iota(jnp.int32, s_t.shape, 0)
                        qcol = lax.broadcasted_iota(jnp.int32, s_t.shape, 1)
                        s_t = jnp.where(krow <= qcol, s_t, NEG)
                    m_prev = m_sc[h, :, qs]
                    m_new = jnp.maximum(m_prev, jnp.max(s_t, axis=0, keepdims=True))
                    alpha = jnp.exp2(m_prev - m_new)
                    p_t = jnp.exp2(s_t - m_new)
                    l_sc[h, :, qs] = alpha * l_sc[h, :, qs] + jnp.sum(p_t, axis=0, keepdims=True)
                    acc_sc[h, :, qs] = acc_sc[h, :, qs] * alpha + lax.dot_general(
                        v_ref[ks, vc], p_t.astype(BF16), TN_DIMS, preferred_element_type=F32)
                    m_sc[h, :, qs] = m_new

        @pl.when(j < i)
        def _():
            step(False)

        @pl.when(j == i)
        def _():
            step(True)
            for h in range(hp):
                l = l_sc[h]
                o_ref[:, h * V_DIM:(h + 1) * V_DIM] = jnp.transpose(acc_sc[h] / l).astype(BF16)
                lse_ref[h] = m_sc[h] + jnp.log(l) * LOG2E

        @pl.when((group == n_groups - 1) & (step_id == n_steps - 1))
        def _():
            _gather_finish(gin, gout, *sems)

    hbm = pl.BlockSpec(memory_space=pl.ANY)
    grid_spec = pltpu.PrefetchScalarGridSpec(
        num_scalar_prefetch=2, grid=(n_groups, n_steps),
        in_specs=[pl.BlockSpec((t, hp * HEAD_W), lambda g, s, it, jt: (it[s], g)),
                  pl.BlockSpec((t, hp * HEAD_W), lambda g, s, it, jt: (jt[s], g)),
                  pl.BlockSpec((t, hp * V_DIM), lambda g, s, it, jt: (jt[s], g))] + [hbm] * ng,
        out_specs=[pl.BlockSpec((t, hp * V_DIM), lambda g, s, it, jt: (it[s], g)),
                   pl.BlockSpec((hp, 1, t), lambda g, s, it, jt: (g, 0, it[s]))] + [hbm] * ng,
        scratch_shapes=[pltpu.VMEM((hp, 1, t), F32), pltpu.VMEM((hp, 1, t), F32), pltpu.VMEM((hp, V_DIM, t), F32)]
        + _comm_scratch(ng),
    )
    return pl.pallas_call(
        body, name="flash_fwd", grid_spec=grid_spec,
        out_shape=[jax.ShapeDtypeStruct((s_len, out_cols), BF16),
                   jax.ShapeDtypeStruct((n_heads, 1, s_len), F32)] + _gathered_shapes(gather),
        compiler_params=_cparams("arbitrary", "arbitrary"),
    )(itab, jtab, q, k, v, *gather)


def _gconv_fwd(proj, lay, w, b, ts, tc, mixcat):
    s_len = proj.shape[0]
    cw = w.shape[1]
    nj = cw // tc
    out_off = mixcat.shape[1] - cw

    def body(gb_ref, gc_ref, ci_ref, gch_ref, cih_ref, w_ref, b_ref, mix_in, o_ref):
        i = pl.program_id(1)
        p = gc_ref[...].astype(F32) * ci_ref[...].astype(F32)
        ph = jnp.where(i > 0, gch_ref[...].astype(F32) * cih_ref[...].astype(F32), 0.0)
        o_ref[...] = (gb_ref[...].astype(F32) * _conv3(p, ph, w_ref, b_ref)).astype(BF16)

    def blk(off):
        return pl.BlockSpec((ts, tc), lambda j, i: (i, off // tc + j))

    def halo(off):
        return pl.BlockSpec((SUB, tc), _prev_halo(ts, lambda j: off // tc + j))

    return pl.pallas_call(
        body, name="gconv_fwd", grid=(nj, s_len // ts),
        in_specs=[blk(lay["gb"]), blk(lay["gc"]), blk(lay["ci"]), halo(lay["gc"]), halo(lay["ci"]),
                  pl.BlockSpec((3, tc), lambda j, i: (0, j)), pl.BlockSpec((1, tc), lambda j, i: (0, j)),
                  pl.BlockSpec(memory_space=pl.ANY)],
        out_specs=pl.BlockSpec((ts, tc), lambda j, i: (i, out_off // tc + j)),
        out_shape=jax.ShapeDtypeStruct(mixcat.shape, BF16),
        input_output_aliases={7: 0},
        compiler_params=_cparams("parallel", "parallel"),
    )(proj, proj, proj, proj, proj, w, b, mixcat)


def _post_mix_fwd(mix, x, gt, g_post, g_pre, sc, sh, ts):
    s_len, d = x.shape

    def body(mix_ref, x_ref, gt_ref, gp_ref, g2_ref, sc_ref, sh_ref, x1_ref, h2_ref):
        mv = mix_ref[...].astype(F32)
        x1 = x_ref[...] + gt_ref[...] * (mv * _rsq(mv) * gp_ref[...])
        x1_ref[...] = x1
        h2_ref[...] = ((x1 * _rsq(x1) * g2_ref[...]) * (1.0 + sc_ref[...]) + sh_ref[...]).astype(BF16)

    vec = pl.BlockSpec((1, d), lambda i: (0, 0))
    row = pl.BlockSpec((ts, d), lambda i: (i, 0))
    return pl.pallas_call(
        body, name="post_mix_fwd", grid=(s_len // ts,),
        in_specs=[row, row, vec, vec, vec, vec, vec], out_specs=[row, row],
        out_shape=[jax.ShapeDtypeStruct((s_len, d), F32), jax.ShapeDtypeStruct((s_len, d), BF16)],
        compiler_params=_cparams("parallel"),
    )(mix, x, gt, g_post, g_pre, sc, sh)


def _act_fwd(up, w, b, ts, tc):
    s_len, f2 = up.shape
    nh = f2 // 2 // tc
    rs_n = _tile(ts, STRIP_ROWS, HALO)

    def body(up_ref, uph_ref, w_ref, b_ref, u_ref, o_ref):
        i = pl.program_id(1)

        def conv_strip(r0, lanes):
            if r0 == 0:
                top = jnp.where(i > 0, uph_ref[:, lanes].astype(F32), 0.0)
                xe = jnp.concatenate([top, up_ref[0:rs_n, lanes].astype(F32)], axis=0)
            else:
                xe = up_ref[r0 - HALO:r0 + rs_n, lanes].astype(F32)
            u = (pltpu.roll(xe, 2, 0)[HALO:] * w_ref[0:1, lanes] + pltpu.roll(xe, 1, 0)[HALO:] * w_ref[1:2, lanes]
                 + xe[HALO:] * w_ref[2:3, lanes] + b_ref[:, lanes])
            u_ref[r0:r0 + rs_n, lanes] = u.astype(BF16)
            return u

        for r0 in range(0, ts, rs_n):
            for c0 in range(0, tc, LANE):
                ua = conv_strip(r0, slice(c0, c0 + LANE))
                ug = conv_strip(r0, slice(tc + c0, tc + c0 + LANE))
                o_ref[r0:r0 + rs_n, c0:c0 + LANE] = (ug * _sigmoid(ug) * ua).astype(BF16)

    pair = pl.BlockSpec((ts, 2 * tc), lambda j, i: (i, j))
    return pl.pallas_call(
        body, name="act_fwd", grid=(nh, s_len // ts),
        in_specs=[pair, pl.BlockSpec((HALO, 2 * tc), lambda j, i: (jnp.maximum(i * (ts // HALO) - 1, 0), j)),
                  pl.BlockSpec((3, 2 * tc), lambda j, i: (0, j)), pl.BlockSpec((1, 2 * tc), lambda j, i: (0, j))],
        out_specs=[pair, pl.BlockSpec((ts, tc), lambda j, i: (i, j))],
        out_shape=[jax.ShapeDtypeStruct((s_len, f2), BF16), jax.ShapeDtypeStruct((s_len, f2 // 2), BF16)],
        compiler_params=_cparams("parallel", "parallel"),
    )(up, up, w, b)


def _final_bwd(y, x1, tgt, gt, g_post, ts):
    s_len, d = y.shape

    n_i = s_len // ts

    def body(y_ref, x1_ref, t_ref, gt_ref, g_ref, dy_ref, dx2_ref, loss_ref, dgt_ref, dg_ref):
        i = pl.program_id(0)
        gtg = gt_ref[...] * g_ref[...]
        yv = y_ref[...].astype(F32)
        r = _rsq(yv)
        yh = yv * r
        e = x1_ref[...] + yh * gtg - t_ref[...]
        loss = 0.5 * jnp.sum(jnp.mean(e * e, axis=-1, keepdims=True), axis=0, keepdims=True)
        dx2 = e * (1.0 / d)
        dx2_ref[...] = dx2
        dyh = dx2 * gtg
        dy_ref[...] = (r * (dyh - yh * jnp.mean(dyh * yh, axis=-1, keepdims=True))).astype(BF16)

        @pl.when(i == 0)
        def _():
            loss_ref[...] = jnp.zeros(loss_ref.shape, F32)
            dgt_ref[...] = jnp.zeros(dgt_ref.shape, F32)

        loss_ref[...] += jnp.broadcast_to(loss, loss_ref.shape)
        dgt_ref[...] += _colsum(dx2 * yh)

        @pl.when(i == n_i - 1)
        def _():
            both = dgt_ref[...]
            dg_ref[...] = both * gt_ref[...]
            dgt_ref[...] = both * g_ref[...]

    vec = pl.BlockSpec((1, d), lambda i: (0, 0))
    row = pl.BlockSpec((ts, d), lambda i: (i, 0))
    vshape = jax.ShapeDtypeStruct((1, d), F32)
    return pl.pallas_call(
        body, name="final_bwd", grid=(s_len // ts,),
        in_specs=[row, row, row, vec, vec],
        out_specs=[row, row, pl.BlockSpec((1, LANE), lambda i: (0, 0)), vec, vec],
        out_shape=[jax.ShapeDtypeStruct((s_len, d), BF16), jax.ShapeDtypeStruct((s_len, d), F32),
                   jax.ShapeDtypeStruct((1, LANE), F32), vshape, vshape],
        compiler_params=_cparams("arbitrary"),
    )(y, x1, tgt, gt, g_post)


def _ffn_act_bwd(d_act, u, up, w, ts, tc):
    s_len, f2 = up.shape
    nh = f2 // 2 // tc
    n_i = s_len // ts
    rs_n = _tile(ts, STRIP_ROWS, HALO)
    ext = rs_n + HALO

    def body(d_ref, dh_ref, u_ref, uh_ref, x_ref, w_ref, dx_ref, dw_ref, db_ref, acc):
        i = pl.program_id(1)
        acc[...] = jnp.zeros(acc.shape, F32)

        def below(ref, halo_ref, r0, lanes):
            if r0 + ext <= ts:
                return ref[r0:r0 + ext, lanes].astype(F32)
            bot = jnp.where(i < n_i - 1, halo_ref[:, lanes].astype(F32), 0.0)
            return jnp.concatenate([ref[r0:ts, lanes].astype(F32), bot], axis=0)

        def fold8(v):
            return jnp.sum(v.reshape(v.shape[0] // SUB, SUB, v.shape[1]), axis=0)

        def conv_bwd_strip(du, r0, lanes):
            du1, du2 = pltpu.roll(du, ext - 1, 0)[:rs_n], pltpu.roll(du, ext - 2, 0)[:rs_n]
            du0 = du[:rs_n]
            dx_ref[r0:r0 + rs_n, lanes] = (du0 * w_ref[2:3, lanes] + du1 * w_ref[1:2, lanes]
                                           + du2 * w_ref[0:1, lanes]).astype(BF16)
            xv = x_ref[r0:r0 + rs_n, lanes].astype(F32)
            acc[0, :, lanes] += fold8(du2 * xv)
            acc[1, :, lanes] += fold8(du1 * xv)
            acc[2, :, lanes] += fold8(du0 * xv)
            acc[3, :, lanes] += fold8(du0)

        for r0 in range(0, ts, rs_n):
            for c0 in range(0, tc, LANE):
                la, lg = slice(c0, c0 + LANE), slice(tc + c0, tc + c0 + LANE)
                dv = below(d_ref, dh_ref, r0, la)
                ua, ug = below(u_ref, uh_ref, r0, la), below(u_ref, uh_ref, r0, lg)
                sg = _sigmoid(ug)
                conv_bwd_strip(dv * (ug * sg), r0, la)
                conv_bwd_strip(dv * ua * (sg * (1.0 + ug * (1.0 - sg))), r0, lg)

        @pl.when(i == 0)
        def _():
            dw_ref[...] = jnp.zeros(dw_ref.shape, F32)
            db_ref[...] = jnp.zeros(db_ref.shape, F32)

        dw_ref[...] += jnp.concatenate([_colsum(acc[k]) for k in range(3)], axis=0)
        db_ref[...] += _colsum(acc[3])

    pair = pl.BlockSpec((ts, 2 * tc), lambda j, i: (i, j))

    def nxt(j, i):
        return (jnp.minimum((i + 1) * (ts // HALO), s_len // HALO - 1), j)

    return pl.pallas_call(
        body, name="ffn_act_bwd", grid=(nh, n_i),
        in_specs=[pl.BlockSpec((ts, tc), lambda j, i: (i, j)), pl.BlockSpec((HALO, tc), nxt),
                  pair, pl.BlockSpec((HALO, 2 * tc), nxt), pair, pl.BlockSpec((3, 2 * tc), lambda j, i: (0, j))],
        out_specs=[pair, pl.BlockSpec((3, 2 * tc), lambda j, i: (0, j)), pl.BlockSpec((1, 2 * tc), lambda j, i: (0, j))],
        out_shape=[jax.ShapeDtypeStruct((s_len, f2), BF16), jax.ShapeDtypeStruct((3, f2), F32),
                   jax.ShapeDtypeStruct((1, f2), F32)],
        scratch_shapes=[pltpu.VMEM((4, SUB, 2 * tc), F32)],
        compiler_params=_cparams("parallel", "arbitrary"),
    )(d_act, d_act, u, u, up, w)


def _mid_bwd(dh2, x1, dx2, mix, g_pre, sc, gt_m, g_post, ts):
    s_len, d = x1.shape
    n_i = s_len // ts

    def body(dh_ref, x1_ref, dx2_ref, mix_ref, g_ref, sc_ref, gt_ref, gp_ref,
             dx1_ref, dmix_ref, dsh_ref, dsc_ref, dg_ref, dgt_ref, dgp_ref):
        i = pl.program_id(0)
        wv = (1.0 + sc_ref[...]) * g_ref[...]
        gtg = gt_ref[...] * gp_ref[...]
        dh = dh_ref[...].astype(F32)
        x1 = x1_ref[...]
        r1 = _rsq(x1)
        xh = x1 * r1
        dhx = dh * xh
        dx1 = dx2_ref[...] + r1 * (dh * wv - xh * jnp.mean(dhx * wv, axis=-1, keepdims=True))
        dx1_ref[...] = dx1
        mv = mix_ref[...].astype(F32)
        rm = _rsq(mv)
        mh = mv * rm
        dxm = dx1 * mh
        dmix_ref[...] = (rm * (dx1 * gtg - mh * jnp.mean(dxm * gtg, axis=-1, keepdims=True))).astype(BF16)

        @pl.when(i == 0)
        def _():
            for ref in (dsh_ref, dsc_ref, dgt_ref):
                ref[...] = jnp.zeros(ref.shape, F32)

        dsh_ref[...] += _colsum(dh)
        dsc_ref[...] += _colsum(dhx)
        dgt_ref[...] += _colsum(dxm)

        @pl.when(i == n_i - 1)
        def _():
            t1, t2 = dsc_ref[...], dgt_ref[...]
            dsc_ref[...] = t1 * g_ref[...]
            dg_ref[...] = t1 * (1.0 + sc_ref[...])
            dgt_ref[...] = t2 * gp_ref[...]
            dgp_ref[...] = t2 * gt_ref[...]

    vec = pl.BlockSpec((1, d), lambda i: (0, 0))
    row = pl.BlockSpec((ts, d), lambda i: (i, 0))
    vshape = jax.ShapeDtypeStruct((1, d), F32)
    return pl.pallas_call(
        body, name="mid_bwd", grid=(s_len // ts,),
        in_specs=[row, row, row, row, vec, vec, vec, vec],
        out_specs=[row, row, vec, vec, vec, vec, vec],
        out_shape=[jax.ShapeDtypeStruct((s_len, d), F32), jax.ShapeDtypeStruct((s_len, d), BF16)] + [vshape] * 5,
        compiler_params=_cparams("arbitrary"),
    )(dh2, x1, dx2, mix, g_pre, sc, gt_m, g_post)


def _first_bwd(dh1, x, dx1, g_pre, sc, ts):
    s_len, d = x.shape
    n_i = s_len // ts

    def body(dh_ref, x_ref, dx1_ref, g_ref, sc_ref, dx_ref, dsh_ref, dsc_ref, dg_ref):
        i = pl.program_id(0)
        wv = (1.0 + sc_ref[...]) * g_ref[...]
        dh = dh_ref[...].astype(F32)
        xv = x_ref[...]
        r = _rsq(xv)
        xh = xv * r
        dhx = dh * xh
        dx_ref[...] = dx1_ref[...] + r * (dh * wv - xh * jnp.mean(dhx * wv, axis=-1, keepdims=True))

        @pl.when(i == 0)
        def _():
            for ref in (dsh_ref, dsc_ref):
                ref[...] = jnp.zeros(ref.shape, F32)

        dsh_ref[...] += _colsum(dh)
        dsc_ref[...] += _colsum(dhx)

        @pl.when(i == n_i - 1)
        def _():
            t1 = dsc_ref[...]
            dsc_ref[...] = t1 * g_ref[...]
            dg_ref[...] = t1 * (1.0 + sc_ref[...])

    vec = pl.BlockSpec((1, d), lambda i: (0, 0))
    row = pl.BlockSpec((ts, d), lambda i: (i, 0))
    vshape = jax.ShapeDtypeStruct((1, d), F32)
    return pl.pallas_call(
        body, name="first_bwd", grid=(s_len // ts,),
        in_specs=[row, row, row, vec, vec], out_specs=[row, vec, vec, vec],
        out_shape=[jax.ShapeDtypeStruct((s_len, d), F32), vshape, vshape, vshape],
        compiler_params=_cparams("arbitrary"),
    )(dh1, x, dx1, g_pre, sc)


def _gconv_bwd(d_mixcat, proj, lay, w, b, ts, tc):
    s_len = proj.shape[0]
    cw = w.shape[1]
    n_i = s_len // ts
    dc_off = d_mixcat.shape[1] - cw

    def body(dc_ref, dch_ref, gb_ref, gbh_ref, gc_ref, gch_ref, ci_ref, cih_ref, w_ref, b_ref,
             dgb_ref, dgc_ref, dci_ref, dw_ref, db_ref):
        i = pl.program_id(1)
        gc, ci = gc_ref[...].astype(F32), ci_ref[...].astype(F32)
        p = gc * ci
        ph = jnp.where(i > 0, gch_ref[...].astype(F32) * cih_ref[...].astype(F32), 0.0)
        pm1, pm2 = _shift_down(p, ph, 1), _shift_down(p, ph, 2)
        z = pm2 * w_ref[0:1, :] + pm1 * w_ref[1:2, :] + p * w_ref[2:3, :] + b_ref[...]
        dc = dc_ref[...].astype(F32)
        dgb_ref[...] = (dc * z).astype(BF16)
        dz = dc * gb_ref[...].astype(F32)
        dzh = jnp.where(i < n_i - 1, dch_ref[...].astype(F32) * gbh_ref[...].astype(F32), 0.0)
        dz1, dz2 = _shift_up(dz, dzh, 1), _shift_up(dz, dzh, 2)
        dp = dz * w_ref[2:3, :] + dz1 * w_ref[1:2, :] + dz2 * w_ref[0:1, :]
        dgc_ref[...] = (dp * ci).astype(BF16)
        dci_ref[...] = (dp * gc).astype(BF16)

        @pl.when(i == 0)
        def _():
            dw_ref[...] = jnp.zeros(dw_ref.shape, F32)
            db_ref[...] = jnp.zeros(db_ref.shape, F32)

        dw_ref[0:1, :] += _colsum(dz2 * p)
        dw_ref[1:2, :] += _colsum(dz1 * p)
        dw_ref[2:3, :] += _colsum(dz * p)
        db_ref[...] += _colsum(dz)

    def blk(off):
        return pl.BlockSpec((ts, tc), lambda j, i: (i, off // tc + j))

    def prev(off):
        return pl.BlockSpec((SUB, tc), _prev_halo(ts, lambda j: off // tc + j))

    def nxt(off):
        return pl.BlockSpec((SUB, tc), _next_halo(ts, s_len, lambda j: off // tc + j))

    out_blk = pl.BlockSpec((ts, tc), lambda j, i: (i, j))
    act = jax.ShapeDtypeStruct((s_len, cw), BF16)
    return pl.pallas_call(
        body, name="gconv_bwd", grid=(cw // tc, n_i),
        in_specs=[blk(dc_off), nxt(dc_off), blk(lay["gb"]), nxt(lay["gb"]), blk(lay["gc"]), prev(lay["gc"]),
                  blk(lay["ci"]), prev(lay["ci"]),
                  pl.BlockSpec((3, tc), lambda j, i: (0, j)), pl.BlockSpec((1, tc), lambda j, i: (0, j))],
        out_specs=[out_blk, out_blk, out_blk,
                   pl.BlockSpec((3, tc), lambda j, i: (0, j)), pl.BlockSpec((1, tc), lambda j, i: (0, j))],
        out_shape=[act, act, act, jax.ShapeDtypeStruct((3, cw), F32), jax.ShapeDtypeStruct((1, cw), F32)],
        compiler_params=_cparams("parallel", "arbitrary"),
    )(d_mixcat, d_mixcat, proj, proj, proj, proj, proj, proj, w, b)


def _delta(o, d_mixcat, n_heads, ts):
    s_len = o.shape[0]

    def body(o_ref, do_ref, out_ref):
        for h in range(n_heads):
            sl = slice(h * V_DIM, (h + 1) * V_DIM)
            prod = o_ref[:, sl].astype(F32) * do_ref[:, sl].astype(F32)
            out_ref[h] = jnp.broadcast_to(jnp.sum(prod, axis=1, keepdims=True), (ts, LANE))

    hv = n_heads * V_DIM
    return pl.pallas_call(
        body, name="attn_delta", grid=(s_len // ts,),
        in_specs=[pl.BlockSpec((ts, hv), lambda i: (i, 0)), pl.BlockSpec((ts, hv), lambda i: (i, 0))],
        out_specs=pl.BlockSpec((n_heads, ts, LANE), lambda i: (0, i, 0)),
        out_shape=jax.ShapeDtypeStruct((n_heads, s_len, LANE), F32),
        compiler_params=_cparams("parallel"),
    )(o, d_mixcat)


def _flash_bwd(q, k, v, d_mixcat, lse_row, delta_row, n_heads, t, scale, exchange):
    nx = len(exchange)
    s_len = q.shape[0]
    nb = s_len // t
    pairs = [(j, i) for j in range(nb) for i in range(j, nb)]
    jtab = jnp.asarray(np.array([p[0] for p in pairs], np.int32))
    itab = jnp.asarray(np.array([p[1] for p in pairs], np.int32))
    n_steps = len(pairs)

    def body(jt_ref, it_ref, q_ref, k_ref, v_ref, do_ref, lse_ref, dl_ref, *rest):
        xin, (dq_ref, dk_ref, dv_ref), xout = rest[:nx], rest[nx:nx + 3], rest[nx + 3:2 * nx + 3]
        dq_acc, dk_acc, dv_acc = rest[2 * nx + 3:2 * nx + 6]
        sems = rest[2 * nx + 6:]
        head, step_id = pl.program_id(0), pl.program_id(1)
        j, i = jt_ref[step_id], it_ref[step_id]

        @pl.when((head == 0) & (step_id == 0))
        def _():
            _exchange_start(xin, xout, *sems)

        @pl.when(step_id == 0)
        def _():
            dq_acc[...] = jnp.zeros(dq_acc.shape, F32)

        @pl.when(i == j)
        def _():
            dk_acc[...] = jnp.zeros(dk_acc.shape, F32)
            dv_acc[...] = jnp.zeros(dv_acc.shape, F32)

        def step(diag):
            half = t // 2
            lo, hi = slice(0, half), slice(half, t)
            blocks = [(lo, lo, True), (lo, hi, False), (hi, hi, True)] if diag else [(slice(0, t), slice(0, t), False)]
            for ks, qs, masked in blocks:
                qv, kv, vv, dov = q_ref[qs, :], k_ref[ks, :], v_ref[ks, :], do_ref[qs, :]
                s_t = lax.dot_general(kv, qv, NT_DIMS, preferred_element_type=F32)
                if masked:
                    krow = lax.broadcasted_iota(jnp.int32, s_t.shape, 0)
                    qcol = lax.broadcasted_iota(jnp.int32, s_t.shape, 1)
                    s_t = jnp.where(krow <= qcol, s_t, NEG)
                p_t = jnp.exp2(s_t - lse_ref[0, :, qs])
                dv_acc[ks, :] += jnp.dot(p_t.astype(BF16), dov, preferred_element_type=F32)
                dp_t = lax.dot_general(vv, dov, NT_DIMS, preferred_element_type=F32)
                ds_t = (p_t * (dp_t - dl_ref[0, :, qs])).astype(BF16)
                dk_acc[ks, :] += jnp.dot(ds_t, qv, preferred_element_type=F32)
                n_q = qs.stop - qs.start
                rows = pl.ds(pl.multiple_of(i * t + qs.start, n_q), n_q)
                dq_acc[rows, :] += lax.dot_general(ds_t, kv, TN_DIMS, preferred_element_type=F32)

        @pl.when(i > j)
        def _():
            step(False)

        @pl.when(i == j)
        def _():
            step(True)

        @pl.when(i == nb - 1)
        def _():
            dk_ref[...] = (dk_acc[...] * LN2).astype(BF16)
            dv_ref[...] = dv_acc[...].astype(BF16)

        @pl.when(step_id == n_steps - 1)
        def _():
            dq_ref[...] = (dq_acc[...] * scale).astype(BF16)

        @pl.when((head == n_heads - 1) & (step_id == n_steps - 1))
        def _():
            _exchange_finish(xin, xout, *sems)

    hbm = pl.BlockSpec(memory_space=pl.ANY)
    hv = n_heads * V_DIM
    do_off = 0
    grid_spec = pltpu.PrefetchScalarGridSpec(
        num_scalar_prefetch=2, grid=(n_heads, n_steps),
        in_specs=[pl.BlockSpec((t, HEAD_W), lambda h, s, jt, it: (it[s], h)),
                  pl.BlockSpec((t, HEAD_W), lambda h, s, jt, it: (jt[s], h)),
                  pl.BlockSpec((t, V_DIM), lambda h, s, jt, it: (jt[s], h)),
                  pl.BlockSpec((t, V_DIM), lambda h, s, jt, it: (it[s], do_off + h)),
                  pl.BlockSpec((1, 1, t), lambda h, s, jt, it: (h, 0, it[s])),
                  pl.BlockSpec((1, 1, t), lambda h, s, jt, it: (h, 0, it[s]))] + [hbm] * nx,
        out_specs=[pl.BlockSpec((s_len, HEAD_W), lambda h, s, jt, it: (0, h)),
                   pl.BlockSpec((t, HEAD_W), lambda h, s, jt, it: (jt[s], h)),
                   pl.BlockSpec((t, V_DIM), lambda h, s, jt, it: (jt[s], h))] + [hbm] * nx,
        scratch_shapes=[pltpu.VMEM((s_len, HEAD_W), F32), pltpu.VMEM((t, HEAD_W), F32), pltpu.VMEM((t, V_DIM), F32)]
        + _comm_scratch(nx),
    )
    return pl.pallas_call(
        body, name="flash_bwd", grid_spec=grid_spec,
        out_shape=[jax.ShapeDtypeStruct((s_len, n_heads * HEAD_W), BF16),
                   jax.ShapeDtypeStruct((s_len, n_heads * HEAD_W), BF16),
                   jax.ShapeDtypeStruct((s_len, hv), BF16)] + [jax.ShapeDtypeStruct(c.shape, c.dtype) for c in exchange],
        compiler_params=_cparams("arbitrary", "arbitrary"),
    )(jtab, itab, q, k, v, d_mixcat, lse_row, delta_row, *exchange)


def _qkv_bwd(dq, dk, dv, proj, lay, wuq, wk, wv, g_q, g_kv, ctab, atab, btab, n_heads, ts):
    s_len = proj.shape[0]
    ql_w, kl_w = wuq.shape[0], wk.shape[0]
    tail_w = lay["np"] - lay["ql"]
    kv_o, kr_o = lay["kv"] - lay["ql"], lay["kr"] - lay["ql"]

    def body(dq_ref, dk_ref, dv_ref, ql_ref, kl_ref, wuq_ref, wk_ref, wv_ref, gq_ref, gkv_ref, c_ref, a_ref, b_ref,
             dqr_ref, dkn_ref, tail_ref, dgq_ref, dgkv_ref):
        i = pl.program_id(0)
        c, a, b = c_ref[...], a_ref[...], b_ref[...]
        dkr = jnp.zeros((ts, LANE), F32)
        for h in range(n_heads):
            o = h * HEAD_W
            dqr_ref[:, o:o + QK_NOPE] = dq_ref[:, o:o + QK_NOPE]
            dqr_ref[:, o + QK_NOPE:o + HEAD_W] = _rope_t(dq_ref[:, o + QK_NOPE:o + HEAD_W].astype(F32), c, a, b).astype(BF16)
            dkn_ref[:, h * QK_NOPE:(h + 1) * QK_NOPE] = dk_ref[:, o:o + QK_NOPE]
            dkr = dkr + dk_ref[:, o + QK_NOPE:o + HEAD_W].astype(F32)
        tail_ref[...] = jnp.zeros(tail_ref.shape, BF16)
        tail_ref[:, kr_o:kr_o + LANE] = _rope_t(dkr, c, a, b).astype(BF16)

        def rms_bwd(lat_ref, dn, g_ref):
            lat = lat_ref[...].astype(F32)
            r = _rsq(lat)
            xh = lat * r
            dxh = dn * g_ref[...]
            return r * (dxh - xh * jnp.mean(dxh * xh, axis=-1, keepdims=True)), _colsum(dn * xh)

        dqn = lax.dot_general(dqr_ref[...], wuq_ref[...], NT_DIMS, preferred_element_type=F32)
        d_ql, dgq = rms_bwd(ql_ref, dqn, gq_ref)
        tail_ref[:, 0:ql_w] = d_ql.astype(BF16)
        dkvn = (lax.dot_general(dkn_ref[...], wk_ref[...], NT_DIMS, preferred_element_type=F32)
                + lax.dot_general(dv_ref[...], wv_ref[...], NT_DIMS, preferred_element_type=F32))
        d_kl, dgkv = rms_bwd(kl_ref, dkvn, gkv_ref)
        tail_ref[:, kv_o:kv_o + kl_w] = d_kl.astype(BF16)

        @pl.when(i == 0)
        def _():
            dgq_ref[...] = jnp.zeros(dgq_ref.shape, F32)
            dgkv_ref[...] = jnp.zeros(dgkv_ref.shape, F32)

        dgq_ref[...] += dgq
        dgkv_ref[...] += dgkv

    def full(arr):
        return pl.BlockSpec(arr.shape, lambda i: (0, 0))

    def rows(w):
        return pl.BlockSpec((ts, w), lambda i: (i, 0))

    tab = pl.BlockSpec((ts, LANE), lambda i: (i, 0))
    hw, hv, hn = n_heads * HEAD_W, n_heads * V_DIM, n_heads * QK_NOPE
    return pl.pallas_call(
        body, name="qkv_bwd", grid=(s_len // ts,),
        in_specs=[rows(hw), rows(hw), rows(hv),
                  pl.BlockSpec((ts, ql_w), lambda i: (i, lay["ql"] // ql_w)),
                  pl.BlockSpec((ts, kl_w), lambda i: (i, lay["kv"] // kl_w)),
                  full(wuq), full(wk), full(wv), full(g_q), full(g_kv), tab, tab, tab],
        out_specs=[rows(hw), rows(hn), rows(tail_w), full(g_q), full(g_kv)],
        out_shape=[jax.ShapeDtypeStruct((s_len, hw), BF16), jax.ShapeDtypeStruct((s_len, hn), BF16),
                   jax.ShapeDtypeStruct((s_len, tail_w), BF16),
                   jax.ShapeDtypeStruct(g_q.shape, F32), jax.ShapeDtypeStruct(g_kv.shape, F32)],
        compiler_params=_cparams("arbitrary"),
    )(dq, dk, dv, proj, proj, wuq, wk, wv, g_q, g_kv, ctab, atab, btab)


def _adamw(w, g, m, v):
    m = ADAM_B1 * m + (1.0 - ADAM_B1) * g
    v = ADAM_B2 * v + (1.0 - ADAM_B2) * (g * g)
    m_hat = m / (1.0 - ADAM_B1 ** ADAM_STEP)
    v_hat = v / (1.0 - ADAM_B2 ** ADAM_STEP)
    delta = -ADAM_LR * (m_hat / (jnp.sqrt(v_hat) + ADAM_EPS) + ADAM_WD * w)
    return delta, m, v


def _adam_parts(parts, w, m, v, tr, name):
    r, c = w.shape
    tr = _tile(r, tr, SUB)

    def body(p_ref, w_ref, m_ref, v_ref, g_out, d_out, m_out, v_out):
        g = p_ref[0].astype(F32)
        for dev in range(1, N_DEV):
            g = g + p_ref[dev].astype(F32)
        g_out[...] = g
        d_out[...], m_out[...], v_out[...] = _adamw(w_ref[...], g, m_ref[...], v_ref[...])

    blk = pl.BlockSpec((tr, c), lambda i: (i, 0))
    shp = jax.ShapeDtypeStruct((r, c), F32)
    return pl.pallas_call(
        body, name=name, grid=(r // tr,),
        in_specs=[pl.BlockSpec((N_DEV, tr, c), lambda i: (0, i, 0)), blk, blk, blk],
        out_specs=[blk, blk, blk, blk], out_shape=[shp, shp, shp, shp],
        compiler_params=_cparams("parallel"),
    )(parts, w, m, v)


def _adam_ada(cact_t, dmod_sh, w, m, v, tr):
    r, c = w.shape

    def body(ct_ref, dm_ref, w_ref, m_ref, v_ref, g_out, d_out, m_out, v_out):
        g = jnp.dot(ct_ref[...], dm_ref[...], preferred_element_type=F32, precision=lax.Precision.HIGHEST)
        g_out[...] = g
        d_out[...], m_out[...], v_out[...] = _adamw(w_ref[...], g, m_ref[...], v_ref[...])

    blk = pl.BlockSpec((tr, c), lambda i: (i, 0))
    shp = jax.ShapeDtypeStruct((r, c), F32)
    return pl.pallas_call(
        body, name="adam_ada", grid=(r // tr,),
        in_specs=[pl.BlockSpec((tr, N_DEV), lambda i: (i, 0)), pl.BlockSpec((N_DEV, c), lambda i: (0, 0)), blk, blk, blk],
        out_specs=[blk, blk, blk, blk], out_shape=[shp, shp, shp, shp],
        compiler_params=_cparams("parallel"),
    )(cact_t, dmod_sh, w, m, v)


def _adam_small(v_all, offs, ws, ms, vs):
    n_par = len(ws)

    def body(p_ref, *refs):
        w_refs, m_refs, v_refs = refs[:n_par], refs[n_par:2 * n_par], refs[2 * n_par:3 * n_par]
        sum_ref = refs[3 * n_par]
        outs = refs[3 * n_par + 1:]
        g = p_ref[0:1, :]
        for dev in range(1, N_DEV):
            g = g + p_ref[dev:dev + 1, :]
        sum_ref[...] = g
        for p in range(n_par):
            n = w_refs[p].shape[1]
            gp = sum_ref[:, offs[p]:offs[p] + n]
            outs[p][...] = gp
            (outs[n_par + p][...], outs[2 * n_par + p][...], outs[3 * n_par + p][...]) = _adamw(
                w_refs[p][...], gp, m_refs[p][...], v_refs[p][...])

    vm = pl.BlockSpec(memory_space=pltpu.VMEM)
    shapes = [jax.ShapeDtypeStruct(w.shape, F32) for w in ws]
    out = pl.pallas_call(
        body, name="adam_small", in_specs=[vm] * (1 + 3 * n_par), out_specs=[vm] * (1 + 4 * n_par),
        out_shape=[jax.ShapeDtypeStruct((1, v_all.shape[1]), F32)] + shapes * 4, compiler_params=_cparams(),
    )(v_all, *ws, *ms, *vs)
    return out[0], [out[1 + k * n_par:1 + (k + 1) * n_par] for k in range(4)]


def _my_place():
    return lax.axis_index("x"), lax.axis_index("y"), lax.axis_index("c")


def _peer(place, k):
    x, y, c = place
    return (x ^ (k >> 2), y ^ ((k >> 1) & 1), c ^ (k & 1))


def _index(place):
    return 4 * place[0] + 2 * place[1] + place[2]


def _ada_fwd(vec, w_ada, b_ada_rows):
    lv = vec.shape[1]
    d, c = w_ada.shape

    def body(vec_ref, w_ref, b_ref, gath_ref, cact_ref, mod_ref, modsh, send_a, recv_a, send_b, recv_b, local_s):
        me = _my_place()
        my_i = _index(me)

        def gather_copy(k, to, src_row):
            row = gath_ref.at[pl.ds(src_row, 1), :]
            return pltpu.make_async_remote_copy(src_ref=row, dst_ref=row, send_sem=send_a.at[k], recv_sem=recv_a.at[k],
                                                device_id=to, device_id_type=MESH)

        own = pltpu.make_async_copy(vec_ref, gath_ref.at[pl.ds(my_i, 1), :], local_s.at[0])
        own.start()
        own.wait()
        sends = [gather_copy(k, _peer(me, k), my_i) for k in range(1, N_DEV)]
        for cp in sends:
            cp.start()
        for k in range(1, N_DEV):
            gather_copy(k, me, _index(_peer(me, k))).wait_recv()
        for cp in sends:
            cp.wait_send()

        c_all = gath_ref[:, 0:d]
        cact = c_all * _sigmoid(c_all)
        cact_ref[...] = cact
        modsh[...] = jnp.dot(cact, w_ref[...], preferred_element_type=F32, precision=lax.Precision.HIGHEST)

        def mod_copy(k, to, src_row, dst_row):
            return pltpu.make_async_remote_copy(src_ref=modsh.at[pl.ds(src_row, 1), :], dst_ref=mod_ref.at[pl.ds(dst_row, 1), :],
                                                send_sem=send_b.at[k], recv_sem=recv_b.at[k],
                                                device_id=to, device_id_type=MESH)

        own = pltpu.make_async_copy(modsh.at[pl.ds(my_i, 1), :], mod_ref.at[pl.ds(my_i, 1), :], local_s.at[1])
        own.start()
        sends = [mod_copy(k, _peer(me, k), _index(_peer(me, k)), my_i) for k in range(1, N_DEV)]
        for cp in sends:
            cp.start()
        for k in range(1, N_DEV):
            mod_copy(k, me, my_i, _index(_peer(me, k))).wait_recv()
        for cp in sends:
            cp.wait_send()
        own.wait()
        mod_ref[...] = mod_ref[...] + b_ref[...]

    vm = pl.BlockSpec(memory_space=pltpu.VMEM)
    return pl.pallas_call(
        body, name="ada_fwd", in_specs=[vm, vm, vm], out_specs=[vm, vm, vm],
        out_shape=[jax.ShapeDtypeStruct((N_DEV, lv), F32), jax.ShapeDtypeStruct((N_DEV, d), F32),
                   jax.ShapeDtypeStruct((N_DEV, c), F32)],
        scratch_shapes=[pltpu.VMEM((N_DEV, c), F32)] + [pltpu.SemaphoreType.DMA((N_DEV,))] * 4
        + [pltpu.SemaphoreType.DMA((2,))],
        compiler_params=pltpu.CompilerParams(vmem_limit_bytes=VMEM_LIMIT),
    )(vec, w_ada, b_ada_rows)


def _gather_small(vec):
    lv = vec.shape[1]

    def body(vec_ref, gath_ref, send_s, recv_s, local_s):
        me = _my_place()
        my_i = _index(me)

        def copy(k, to, src_row):
            row = gath_ref.at[pl.ds(src_row, 1), :]
            return pltpu.make_async_remote_copy(src_ref=row, dst_ref=row, send_sem=send_s.at[k], recv_sem=recv_s.at[k],
                                                device_id=to, device_id_type=MESH)

        own = pltpu.make_async_copy(vec_ref, gath_ref.at[pl.ds(my_i, 1), :], local_s)
        own.start()
        own.wait()
        sends = [copy(k, _peer(me, k), my_i) for k in range(1, N_DEV)]
        for cp in sends:
            cp.start()
        for k in range(1, N_DEV):
            copy(k, me, _index(_peer(me, k))).wait_recv()
        for cp in sends:
            cp.wait_send()

    vm = pl.BlockSpec(memory_space=pltpu.VMEM)
    return pl.pallas_call(
        body, name="gather_small", in_specs=[vm], out_specs=vm,
        out_shape=jax.ShapeDtypeStruct((N_DEV, lv), F32),
        scratch_shapes=[pltpu.SemaphoreType.DMA((N_DEV,))] * 2 + [pltpu.SemaphoreType.DMA],
        compiler_params=pltpu.CompilerParams(vmem_limit_bytes=VMEM_LIMIT),
    )(vec)


PER = N_DEV - 1


def _comm_scratch(n):
    return [pltpu.SemaphoreType.DMA((n * PER,)), pltpu.SemaphoreType.DMA((n * PER,)), pltpu.SemaphoreType.DMA((n,))]


def _gather_copies(ins, outs, send_s, recv_s, local_s):
    n = len(ins)
    me = _my_place()
    x, y, c = me
    sibling = (x, y, 1 - c)
    chips = [(1 - x, y), (x, 1 - y), (1 - x, 1 - y)]

    def copy(a, k, block, to, src=None):
        slot = outs[a].at[_index(block)]
        return pltpu.make_async_remote_copy(src_ref=slot if src is None else src, dst_ref=slot,
                                            send_sem=send_s.at[a * PER + k], recv_sem=recv_s.at[a * PER + k],
                                            device_id=to, device_id_type=MESH)

    mine = [pltpu.make_async_copy(ins[a], outs[a].at[_index(me)], local_s.at[a]) for a in range(n)]
    first = []
    for a in range(n):
        first.append(copy(a, 0, me, sibling, src=ins[a]))
        first += [copy(a, 1 + j, me, (*chip, c), src=ins[a]) for j, chip in enumerate(chips)]
    landed = [copy(a, 1 + j, (*chip, c), me) for j, chip in enumerate(chips) for a in range(n)]
    passed = [copy(a, 4 + j, (*chip, c), sibling) for j, chip in enumerate(chips) for a in range(n)]
    from_sibling = [copy(a, 0, sibling, me) for a in range(n)]
    from_sibling += [copy(a, 4 + j, (*chip, 1 - c), me) for a in range(n) for j, chip in enumerate(chips)]
    return mine, first, landed, passed, from_sibling


def _gather_start(*refs):
    mine, first, _, _, _ = _gather_copies(*refs)
    for cp in mine + first:
        cp.start()


def _gather_forward(*refs):
    _, _, landed, passed, _ = _gather_copies(*refs)
    for got, fwd in zip(landed, passed):
        got.wait_recv()
        fwd.start()


def _gather_finish(*refs):
    mine, first, _, passed, from_sibling = _gather_copies(*refs)
    for cp in from_sibling:
        cp.wait_recv()
    for cp in first + passed:
        cp.wait_send()
    for cp in mine:
        cp.wait()


def _exchange_copies(ins, outs, send_s, recv_s, local_s):
    n = len(ins)
    me = _my_place()
    my_i = _index(me)

    def copy(a, k, to, src_slot, dst_slot):
        return pltpu.make_async_remote_copy(src_ref=ins[a].at[src_slot], dst_ref=outs[a].at[dst_slot],
                                            send_sem=send_s.at[a * PER + k - 1], recv_sem=recv_s.at[a * PER + k - 1],
                                            device_id=to, device_id_type=MESH)

    mine = [pltpu.make_async_copy(ins[a].at[my_i], outs[a].at[my_i], local_s.at[a]) for a in range(n)]
    sends = [copy(a, k, _peer(me, k), _index(_peer(me, k)), my_i) for k in range(1, N_DEV) for a in range(n)]
    recvs = [copy(a, k, me, my_i, _index(_peer(me, k))) for k in range(1, N_DEV) for a in range(n)]
    return mine, sends, recvs


def _exchange_start(*refs):
    mine, sends, _ = _exchange_copies(*refs)
    for cp in mine + sends:
        cp.start()


def _exchange_finish(*refs):
    mine, sends, recvs = _exchange_copies(*refs)
    for cp in recvs:
        cp.wait_recv()
    for cp in sends:
        cp.wait_send()
    for cp in mine:
        cp.wait()


def _gathered_shapes(shards):
    return [jax.ShapeDtypeStruct((N_DEV,) + s.shape, s.dtype) for s in shards]


def _proj_layout(cw, ql, kl):
    lay = {"gb": 0, "gc": cw, "ci": 2 * cw, "ql": 3 * cw}
    assert lay["ql"] % ql == 0
    lay["kv"] = _roundup(lay["ql"] + ql, kl)
    lay["kr"] = lay["kv"] + kl
    lay["np"] = _roundup(lay["kr"] + LANE, 4 * LANE)
    return lay


def _chunks_cols(g):
    r, c8 = g.shape
    return jnp.transpose(g.reshape(r, N_DEV, c8 // N_DEV), (1, 0, 2))


def _from_col_shards(a):
    n, r, c = a.shape
    return jnp.transpose(a, (1, 0, 2)).reshape(r, n * c)


def kernel(x, c, positions, w_ada, b_ada, g_pre_mix, g_post_mix, w_in, g_q, w_uq, g_kv, w_ukv, conv_w_mix, conv_b_mix, w_o, g_pre_ffn, g_post_ffn, w_up, conv_w_ffn, conv_b_ffn, w_down, loss_target, m_w_ada, m_b_ada, m_g_pre_mix, m_g_post_mix, m_w_in, m_g_q, m_w_uq, m_g_kv, m_w_ukv, m_conv_w_mix, m_conv_b_mix, m_w_o, m_g_pre_ffn, m_g_post_ffn, m_w_up, m_conv_w_ffn, m_conv_b_ffn, m_w_down, v_w_ada, v_b_ada, v_g_pre_mix, v_g_post_mix, v_w_in, v_g_q, v_w_uq, v_g_kv, v_w_ukv, v_conv_w_mix, v_conv_b_mix, v_w_o, v_g_pre_ffn, v_g_post_ffn, v_w_up, v_conv_w_ffn, v_conv_b_ffn, v_w_down):
    s_len, d = x.shape[1], x.shape[2]
    ql, kl = w_uq.shape[1], w_ukv.shape[1]
    n_heads = w_ukv.shape[2] * N_DEV // (QK_NOPE + V_DIM)
    cw = conv_w_mix.shape[2] * N_DEV
    f2 = w_up.shape[2] * N_DEV
    ff = f2 // 2
    in_cols = w_in.shape[2] * N_DEV
    ada_c = w_ada.shape[2]
    cwm_c, cwf_c = conv_w_mix.shape[2], conv_w_ffn.shape[2]
    scale = 1.0 / math.sqrt(QK_NOPE + QK_ROPE)
    lay = _proj_layout(cw, ql, kl)
    n_pad = lay["np"]
    my_i = _index(_my_place())

    ts_row = _tile(s_len, PREF["row"], SUB)
    ts_conv = _tile(s_len, PREF["conv_rows"], SUB)
    tc_conv = _tile(cw, PREF["conv_cols"])
    tc_ffn = cwf_c
    ts_ffn = _tile(s_len, PREF["ffn_rows"], SUB)
    pair_order = [k // 2 + (k % 2) * (N_DEV // 2) for k in range(N_DEV)]
    pair_place = [pair_order.index(k) for k in range(N_DEV)]

    def paired(shards):
        return _from_col_shards(jnp.stack([shards[p] for p in pair_order]))

    def unpaired_chunks(g):
        ch = _chunks_cols(g)
        return jnp.stack([ch[p] for p in pair_place])
    ts_qkv = _tile(s_len, PREF["row"], SUB)
    t_attn = _tile(s_len, PREF["attn"])

    x2d, tgt = x[0], loss_target[0]

    vec = jnp.concatenate([c, conv_w_mix[0].reshape(1, -1), conv_w_ffn[0].reshape(1, -1)], axis=1)
    gath, cact, mod_rows = _ada_fwd(vec, w_ada[0], b_ada.reshape(N_DEV, ada_c))
    cwm_full = _from_col_shards(gath[:, d:d + 3 * cwm_c].reshape(N_DEV, 3, cwm_c))
    cwf_shards = gath[:, d + 3 * cwm_c:].reshape(N_DEV, 3, cwf_c)
    cwf_pair = paired(cwf_shards)
    cbf_pair = paired(jnp.transpose(conv_b_ffn.reshape(1, N_DEV, cwf_c), (1, 0, 2)))
    mod = mod_rows.reshape(1, N_DEV * ada_c)
    sh_m, sc_m, gt_m, sh_f, sc_f, gt_f = [mod[:, k * d:(k + 1) * d] for k in range(6)]

    h1, g_in = _modnorm_fwd(x2d, g_pre_mix, sc_m, sh_m, ts_row, [w_in[0].astype(BF16)])
    win = _from_col_shards(g_in)
    cut = np.cumsum([0, ql, kl, QK_ROPE, cw, cw, cw])
    part = [win[:, cut[k]:cut[k + 1]] for k in range(6)]

    def zcols(n):
        return jnp.zeros((d, n), BF16)

    win_p = jnp.concatenate([part[3], part[4], part[5], part[0], zcols(lay["kv"] - lay["ql"] - ql), part[1],
                             part[2], zcols(n_pad - lay["kr"] - QK_ROPE)], axis=1)
    inv_freq =1.0 / (ROPE_THETA ** (jnp.arange(0, QK_ROPE, 2, dtype=F32) / QK_ROPE))
    inv_row = jnp.tile(inv_freq, LANE // (QK_ROPE // 2)).reshape(1, LANE)
    ctab, atab, btab = _rope_tables(positions.astype(F32).reshape(s_len, 1), inv_row, _tile(s_len, 1024, SUB))

    proj, g_uq, g_ukv = _matmul(h1, win_p, out_dtype=BF16, tm=1024, tn=1280, tk=2048, name="mm_proj",
                                gather=[w_uq[0].astype(BF16), w_ukv[0].astype(BF16)])
    wuq_p = jnp.pad(_from_col_shards(g_uq).reshape(ql, n_heads, QK_NOPE + QK_ROPE),
                    ((0, 0), (0, 0), (0, HEAD_W - QK_NOPE - QK_ROPE))).reshape(ql, n_heads * HEAD_W)
    wukv = _from_col_shards(g_ukv).reshape(kl, n_heads, QK_NOPE + V_DIM)
    wk = wukv[:, :, :QK_NOPE].reshape(kl, n_heads * QK_NOPE)
    wv = wukv[:, :, QK_NOPE:].reshape(kl, n_heads * V_DIM)
    q, k, v, qn, kvn = _qkv_fwd(proj, lay, wuq_p, wk, wv, g_q, g_kv, ctab, atab, btab, n_heads, ts_qkv, scale * LOG2E)
    attn, lse_row, g_o, g_up = _flash_fwd(q, k, v, n_heads, t_attn, [w_o[0].astype(BF16), w_up[0].astype(BF16)], d)
    wo = g_o.reshape(d, d)

    def pair_shard(j):
        return j // 2 + (j % 2) * (N_DEV // 2)
    mixcat = _gconv_fwd(proj, lay, cwm_full, conv_b_mix, ts_conv, tc_conv, attn)
    mix = _matmul(mixcat, wo, out_dtype=BF16, tm=512, tn=2048, tk=2048, name="mm_mix")
    x1, h2 = _post_mix_fwd(mix, x2d, gt_m, g_post_mix, g_pre_ffn, sc_f, sh_f, ts_row)
    up, g_down = _matmul(h2, g_up, out_dtype=BF16, tm=1024, tn=cwf_c, tk=2048, name="mm_up", b_shard_of=pair_shard,
                         gather=[w_down[0].astype(BF16)])
    wdown = g_down.reshape(ff, d)
    u, act = _act_fwd(up, cwf_pair, cbf_pair, ts_ffn, tc_ffn)
    y = _matmul(act, wdown, out_dtype=BF16, tm=1024, tn=512, tk=ff, name="mm_down")

    dy, dx2, loss_row, d_gt_f, dg_post_ffn = _final_bwd(y, x1, tgt, gt_f, g_post_ffn, ts_row)
    gw_down = _matmul(act, dy, ta=True, out_dtype=BF16, tm=1408, tn=1024, tk=2048, name="mm_gw_down")
    d_act = _matmul(dy, wdown, tb=True, out_dtype=BF16, tm=1024, tn=1408, tk=2048, name="mm_d_act")
    d_up, dcw_pair, dcb_pair = _ffn_act_bwd(d_act, u, up, cwf_pair, ts_ffn, tc_ffn)
    dcb_ffn = _from_col_shards(unpaired_chunks(dcb_pair))
    gw_up, p_down = _matmul(h2, d_up, ta=True, out_dtype=BF16, tm=1024, tn=cwf_c, tk=2048, name="mm_gw_up",
                            exchange=[gw_down.reshape(N_DEV, ff // N_DEV, d)], out_shard_of=pair_shard)
    dh2 = _matmul_pair_shards(d_up, g_up, out_dtype=BF16, tm=1024, tn=1024, name="mm_dh2")
    dx1, dmix, d_sh_f, d_sc_f, dg_pre_ffn, d_gt_m, dg_post_mix = _mid_bwd(
        dh2, x1, dx2, mix, g_pre_ffn, sc_f, gt_m, g_post_mix, ts_row)
    gw_o = _matmul(mixcat, dmix, ta=True, out_dtype=BF16, tm=1024, tn=1024, tk=2048, name="mm_gw_o")
    d_mixcat = _matmul(dmix, wo, tb=True, out_dtype=BF16, tm=1024, tn=1024, tk=2048, name="mm_d_mixcat")
    d_gb, d_gc, d_ci, dcw_mix, dcb_mix = _gconv_bwd(d_mixcat, proj, lay, cwm_full, conv_b_mix, ts_conv, tc_conv)
    delta = _delta(mixcat, d_mixcat, n_heads, _tile(s_len, 512, SUB))
    delta_row = delta[:, :, 0].reshape(n_heads, 1, s_len)
    dq, dk, dv, p_up, p_o = _flash_bwd(q, k, v, d_mixcat, lse_row, delta_row, n_heads, t_attn, scale,
                                       [gw_up, gw_o.reshape(N_DEV, d // N_DEV, d)])
    dq_r, dkn, d_tail, dg_q, dg_kv = _qkv_bwd(dq, dk, dv, proj, lay, wuq_p, wk, wv, g_q, g_kv, ctab, atab, btab,
                                              n_heads, ts_qkv)
    gw_uq_p = _matmul(qn, dq_r, ta=True, out_dtype=BF16, tm=768, tn=1024, tk=2048, name="mm_gw_uq")
    gw_k = _matmul(kvn, dkn, ta=True, out_dtype=BF16, tm=512, tn=1024, tk=2048, name="mm_gw_k")
    gw_v = _matmul(kvn, dv, ta=True, out_dtype=BF16, tm=512, tn=1024, tk=2048, name="mm_gw_v")
    d_proj = jnp.concatenate([d_gb, d_gc, d_ci, d_tail], axis=1)
    gw_uq = gw_uq_p.reshape(ql, n_heads, HEAD_W)[:, :, :QK_NOPE + QK_ROPE].reshape(ql, n_heads * (QK_NOPE + QK_ROPE))
    gw_ukv = jnp.concatenate([gw_k.reshape(kl, n_heads, QK_NOPE), gw_v.reshape(kl, n_heads, V_DIM)],
                             axis=2).reshape(kl, n_heads * (QK_NOPE + V_DIM))
    gw_in_p, p_uq, p_ukv, p_cwm, p_cwf = _matmul(
        h1, d_proj, ta=True, out_dtype=BF16, tm=1024, tn=1280, tk=2048, name="mm_gw_in",
        exchange=[_chunks_cols(gw_uq), _chunks_cols(gw_ukv), _chunks_cols(dcw_mix), unpaired_chunks(dcw_pair)])

    gw_in = jnp.concatenate([gw_in_p[:, lay["ql"]:lay["ql"] + ql], gw_in_p[:, lay["kv"]:lay["kv"] + kl],
                             gw_in_p[:, lay["kr"]:lay["kr"] + QK_ROPE], gw_in_p[:, :3 * cw]], axis=1)
    dh1, p_in = _matmul(d_proj, win_p, tb=True, out_dtype=BF16, tm=512, tn=1024, tk=n_pad, name="mm_dh1",
                        exchange=[_chunks_cols(gw_in)])
    grad_x, d_sh_m, d_sc_m, dg_pre_mix = _first_bwd(dh1, x2d, dx1, g_pre_mix, sc_m, ts_row)

    dmod = jnp.concatenate([d_sh_m, d_sc_m, d_gt_m, d_sh_f, d_sc_f, d_gt_f], axis=1)
    small_g = [loss_row, dmod, dg_pre_mix, dg_post_mix, dg_q, dg_kv, dcb_mix, dg_pre_ffn, dg_post_ffn, dcb_ffn]
    v_all = _gather_small(jnp.concatenate(small_g, axis=1))

    small_names = ["b_ada", "g_pre_mix", "g_post_mix", "g_q", "g_kv", "conv_b_mix", "g_pre_ffn", "g_post_ffn",
                   "conv_b_ffn"]
    small_w = [b_ada, g_pre_mix, g_post_mix, g_q, g_kv, conv_b_mix, g_pre_ffn, g_post_ffn, conv_b_ffn]
    small_m = [m_b_ada, m_g_pre_mix, m_g_post_mix, m_g_q, m_g_kv, m_conv_b_mix, m_g_pre_ffn, m_g_post_ffn, m_conv_b_ffn]
    small_v = [v_b_ada, v_g_pre_mix, v_g_post_mix, v_g_q, v_g_kv, v_conv_b_mix, v_g_pre_ffn, v_g_post_ffn, v_conv_b_ffn]
    offs = np.cumsum([0] + [g.shape[1] for g in small_g])
    g_sum, small_out = _adam_small(v_all, [int(o) for o in offs[1:-1]], small_w, small_m, small_v)
    small = [dict(zip(small_names, kind)) for kind in small_out]
    loss = g_sum[0, 0]

    dmod_sh = lax.dynamic_slice(v_all, (0, int(offs[1]) + my_i * ada_c), (N_DEV, ada_c))
    cact_t = jnp.transpose(cact)
    big = dict(
        w_ada=_adam_ada(cact_t, dmod_sh, w_ada[0], m_w_ada[0], v_w_ada[0], _tile(d, 256, SUB)),
        w_in=_adam_parts(p_in, w_in[0], m_w_in[0], v_w_in[0], 256, "adam_w_in"),
        w_uq=_adam_parts(p_uq, w_uq[0], m_w_uq[0], v_w_uq[0], 256, "adam_w_uq"),
        w_ukv=_adam_parts(p_ukv, w_ukv[0], m_w_ukv[0], v_w_ukv[0], 256, "adam_w_ukv"),
        w_o=_adam_parts(p_o, w_o[0], m_w_o[0], v_w_o[0], 128, "adam_w_o"),
        w_up=_adam_parts(p_up, w_up[0], m_w_up[0], v_w_up[0], 256, "adam_w_up"),
        w_down=_adam_parts(p_down, w_down[0], m_w_down[0], v_w_down[0], 176, "adam_w_down"),
        conv_w_mix=_adam_parts(p_cwm, conv_w_mix[0], m_conv_w_mix[0], v_conv_w_mix[0], 8, "adam_cw_mix"),
        conv_w_ffn=_adam_parts(p_cwf, conv_w_ffn[0], m_conv_w_ffn[0], v_conv_w_ffn[0], 8, "adam_cw_ffn"),
    )

    names = ["w_ada", "b_ada", "g_pre_mix", "g_post_mix", "w_in", "g_q", "w_uq", "g_kv", "w_ukv", "conv_w_mix",
             "conv_b_mix", "w_o", "g_pre_ffn", "g_post_ffn", "w_up", "conv_w_ffn", "conv_b_ffn", "w_down"]
    outs = [loss, grad_x[None]]
    for kind in range(4):
        for nm in names:
            outs.append(big[nm][kind][None] if nm in big else small[kind][nm])
    return tuple(outs)
```

```python
import math

import numpy as np
import jax
import jax.numpy as jnp
from jax import lax
from jax.experimental import pallas as pl
from jax.experimental.pallas import tpu as pltpu

F32 = jnp.float32
BF16 = jnp.bfloat16
N_DEV = 8
MESH = pl.DeviceIdType.MESH

QK_NOPE = 128
QK_ROPE = 64
V_DIM = 128
HEAD_W = 256
LANE = 128
SUB = 8
HALO = 16
STRIP_ROWS = 64
RMS_EPS = 1e-6
ROPE_THETA = 10000.0
ADAM_LR = 0.001
ADAM_B1 = 0.9
ADAM_B2 = 0.999
ADAM_EPS = 1e-08
ADAM_WD = 0.01
ADAM_STEP = 10
NEG = -1e30
LOG2E = 1.4426950408889634
LN2 = 0.6931471805599453
VMEM_LIMIT = 56 * 1024 * 1024

PREF = {"row": 256, "conv_rows": 512, "conv_cols": 512, "ffn_rows": 256, "attn": 1024, "attn_chunk": 1024, "attn_heads": 2}

NT_DIMS = (((1,), (1,)), ((), ()))
TN_DIMS = (((0,), (0,)), ((), ()))


def _cparams(*sem):
    return pltpu.CompilerParams(dimension_semantics=sem if sem else None, vmem_limit_bytes=VMEM_LIMIT)


def _tile(n, pref, unit=LANE):
    if n <= pref:
        return n
    t = (pref // unit) * unit
    while t >= unit:
        if n % t == 0:
            return t
        t -= unit
    return n


def _roundup(n, m):
    return (n + m - 1) // m * m


def _rsq(x):
    return lax.rsqrt(jnp.mean(x * x, axis=-1, keepdims=True) + RMS_EPS)


def _colsum(x):
    return jnp.sum(x, axis=0, keepdims=True)


def _sigmoid(x):
    return 1.0 / (1.0 + jnp.exp(-x))


def _matmul(a, b, *, ta=False, tb=False, out_dtype, tm, tn, tk, name, exchange=None, gather=None, b_shard_of=None,
            out_shard_of=None):
    m_dim, k_dim = (a.shape[1], a.shape[0]) if ta else a.shape
    if b_shard_of is not None:
        n_dim, tn = b.shape[0] * b.shape[2], b.shape[2]
    else:
        n_dim = b.shape[0] if tb else b.shape[1]
    chunk_w = n_dim // N_DEV
    if out_shard_of is not None:
        tn = _tile(chunk_w, tn)
    tm, tn, tk = _tile(m_dim, tm), _tile(n_dim, tn), _tile(k_dim, tk)
    per_chunk = chunk_w // tn if out_shard_of is not None else 1
    gi, gj, nk = m_dim // tm, n_dim // tn, k_dim // tk
    dims = (((0 if ta else 1,), (1 if tb else 0,)), ((), ()))
    chunks = list(exchange or gather or [])
    nx = len(chunks)
    comm_start, comm_finish = (_gather_start, _gather_finish) if gather else (_exchange_start, _exchange_finish)

    def body(*refs):
        a_ref, b_ref = refs[:2]
        xin, o_ref, xout = refs[2:2 + nx], refs[2 + nx], refs[3 + nx:3 + 2 * nx]
        scratch = refs[3 + 2 * nx:]
        sems = scratch[1:] if nk > 1 else scratch
        i, j, k = pl.program_id(0), pl.program_id(1), pl.program_id(2)
        if nx:
            @pl.when((i == 0) & (j == 0) & (k == 0))
            def _():
                comm_start(xin, xout, *sems)

        if gather:
            @pl.when((i == gi // 2) & (j == 0) & (k == 0))
            def _():
                _gather_forward(xin, xout, *sems)

        part = lax.dot_general(a_ref[...], b_ref[...], dims, preferred_element_type=F32)
        if nk == 1:
            o_ref[...] = part.astype(o_ref.dtype)
        else:
            acc_ref = scratch[0]

            @pl.when(k == 0)
            def _():
                acc_ref[...] = part

            @pl.when(k > 0)
            def _():
                acc_ref[...] += part

            @pl.when(k == nk - 1)
            def _():
                o_ref[...] = acc_ref[...].astype(o_ref.dtype)

        if nx:
            @pl.when((i == gi - 1) & (j == gj - 1) & (k == nk - 1))
            def _():
                comm_finish(xin, xout, *sems)

    a_spec = pl.BlockSpec((tk, tm), lambda i, j, k: (k, i)) if ta else pl.BlockSpec((tm, tk), lambda i, j, k: (i, k))
    b_spec = pl.BlockSpec((tn, tk), lambda i, j, k: (j, k)) if tb else pl.BlockSpec((tk, tn), lambda i, j, k: (k, j))
    if b_shard_of is not None:
        b_spec = pl.BlockSpec((None, tk, tn), lambda i, j, k: (b_shard_of(j), k, 0))
    o_spec, o_shape = pl.BlockSpec((tm, tn), lambda i, j, k: (i, j)), (m_dim, n_dim)
    if out_shard_of is not None:
        o_spec = pl.BlockSpec((None, tm, tn), lambda i, j, k: (out_shard_of(j // per_chunk), i, j % per_chunk))
        o_shape = (N_DEV, m_dim, chunk_w)
    hbm = pl.BlockSpec(memory_space=pl.ANY)
    out = pl.pallas_call(
        body,
        name=name,
        grid=(gi, gj, nk),
        in_specs=[a_spec, b_spec] + [hbm] * nx,
        out_specs=[o_spec] + [hbm] * nx,
        out_shape=[jax.ShapeDtypeStruct(o_shape, out_dtype)]
        + (_gathered_shapes(chunks) if gather else [jax.ShapeDtypeStruct(c.shape, c.dtype) for c in chunks]),
        scratch_shapes=([pltpu.VMEM((tm, tn), F32)] if nk > 1 else []) + (_comm_scratch(nx) if nx else []),
        compiler_params=_cparams(*(("arbitrary",) * 3 if nx else ("parallel", "parallel", "arbitrary"))),
    )(a, b, *chunks)
    return out if nx else out[0]


def _matmul_pair_shards(a, shards, *, out_dtype, tm, tn, name):
    m_dim = a.shape[0]
    n_sh, n_dim, c = shards.shape
    half = n_sh // 2
    tm, tn = _tile(m_dim, tm), _tile(n_dim, tn)

    def body(a_ref, b0_ref, b1_ref, o_ref, acc_ref):
        k = pl.program_id(2)
        part = (lax.dot_general(a_ref[:, :c], b0_ref[...], NT_DIMS, preferred_element_type=F32)
                + lax.dot_general(a_ref[:, c:], b1_ref[...], NT_DIMS, preferred_element_type=F32))

        @pl.when(k == 0)
        def _():
            acc_ref[...] = part

        @pl.when(k > 0)
        def _():
            acc_ref[...] += part

        @pl.when(k == half - 1)
        def _():
            o_ref[...] = acc_ref[...].astype(o_ref.dtype)

    return pl.pallas_call(
        body, name=name, grid=(m_dim // tm, n_dim // tn, half),
        in_specs=[pl.BlockSpec((tm, 2 * c), lambda i, j, k: (i, k)),
                  pl.BlockSpec((None, tn, c), lambda i, j, k: (k, j, 0)),
                  pl.BlockSpec((None, tn, c), lambda i, j, k: (k + half, j, 0))],
        out_specs=pl.BlockSpec((tm, tn), lambda i, j, k: (i, j)),
        out_shape=jax.ShapeDtypeStruct((m_dim, n_dim), out_dtype),
        scratch_shapes=[pltpu.VMEM((tm, tn), F32)],
        compiler_params=_cparams("parallel", "parallel", "arbitrary"),
    )(a, shards, shards)


def _shift_down(x, halo, n):
    r = pltpu.roll(x, n, 0)
    hr = pltpu.roll(halo, n, 0)
    row = lax.broadcasted_iota(jnp.int32, halo.shape, 0)
    top = jnp.where(row < n, hr, r[:SUB])
    return jnp.concatenate([top, r[SUB:]], axis=0)


def _shift_up(x, halo, n):
    ts = x.shape[0]
    r = pltpu.roll(x, ts - n, 0)
    hr = pltpu.roll(halo, SUB - n, 0)
    row = lax.broadcasted_iota(jnp.int32, halo.shape, 0)
    bot = jnp.where(row >= SUB - n, hr, r[ts - SUB:])
    return jnp.concatenate([r[:ts - SUB], bot], axis=0)


def _conv3(x, halo, w_ref, b_ref):
    return _shift_down(x, halo, 2) * w_ref[0:1, :] + _shift_down(x, halo, 1) * w_ref[1:2, :] + x * w_ref[2:3, :] + b_ref[...]


def _prev_halo(ts, col):
    return lambda j, i: (jnp.maximum(i * (ts // SUB) - 1, 0), col(j))


def _next_halo(ts, n_rows, col):
    return lambda j, i: (jnp.minimum((i + 1) * (ts // SUB), n_rows // SUB - 1), col(j))


def _modnorm_fwd(x, g, sc, sh, ts, gather):
    s_len, d = x.shape
    n_i = s_len // ts
    ng = len(gather)

    def body(x_ref, g_ref, sc_ref, sh_ref, *rest):
        gin, h_ref, gout, sems = rest[:ng], rest[ng], rest[ng + 1:2 * ng + 1], rest[2 * ng + 1:]
        i = pl.program_id(0)

        @pl.when(i == 0)
        def _():
            _gather_start(gin, gout, *sems)

        xv = x_ref[...]
        h_ref[...] = ((xv * _rsq(xv) * g_ref[...]) * (1.0 + sc_ref[...]) + sh_ref[...]).astype(BF16)

        @pl.when(i == n_i - 1)
        def _():
            _gather_forward(gin, gout, *sems)
            _gather_finish(gin, gout, *sems)

    vec = pl.BlockSpec((1, d), lambda i: (0, 0))
    hbm = pl.BlockSpec(memory_space=pl.ANY)
    return pl.pallas_call(
        body, name="modnorm_fwd", grid=(n_i,),
        in_specs=[pl.BlockSpec((ts, d), lambda i: (i, 0)), vec, vec, vec] + [hbm] * ng,
        out_specs=[pl.BlockSpec((ts, d), lambda i: (i, 0))] + [hbm] * ng,
        out_shape=[jax.ShapeDtypeStruct((s_len, d), BF16)] + _gathered_shapes(gather),
        scratch_shapes=_comm_scratch(ng),
        compiler_params=_cparams("arbitrary"),
    )(x, g, sc, sh, *gather)


def _rope_tables(pos_col, inv_freq_row, ts):
    s_len = pos_col.shape[0]
    half = QK_ROPE // 2

    def body(p_ref, f_ref, c_ref, a_ref, b_ref):
        ang = p_ref[...] * f_ref[...]
        lane = lax.broadcasted_iota(jnp.int32, ang.shape, 1)
        cos, sin = jnp.cos(ang), jnp.sin(ang)
        c_ref[...] = jnp.where(lane < 2 * half, cos, 0.0)
        a_ref[...] = jnp.where(lane < half, -sin, 0.0)
        b_ref[...] = jnp.where((lane >= half) & (lane < 2 * half), sin, 0.0)

    out = jax.ShapeDtypeStruct((s_len, LANE), F32)
    blk = pl.BlockSpec((ts, LANE), lambda i: (i, 0))
    return pl.pallas_call(
        body, name="rope_tables", grid=(s_len // ts,),
        in_specs=[pl.BlockSpec((ts, 1), lambda i: (i, 0)), pl.BlockSpec((1, LANE), lambda i: (0, 0))],
        out_specs=[blk, blk, blk], out_shape=[out, out, out],
        compiler_params=_cparams("parallel"),
    )(pos_col, inv_freq_row)


def _rope(seg, c, a, b):
    return seg * c + pltpu.roll(seg, LANE - QK_ROPE // 2, 1) * a + pltpu.roll(seg, QK_ROPE // 2, 1) * b


def _rope_t(seg, c, a, b):
    return seg * c - pltpu.roll(seg, LANE - QK_ROPE // 2, 1) * a - pltpu.roll(seg, QK_ROPE // 2, 1) * b


def _qkv_fwd(proj, lay, wuq, wk, wv, g_q, g_kv, ctab, atab, btab, n_heads, ts, scale):
    s_len = proj.shape[0]
    ql_w, kl_w = wuq.shape[0], wk.shape[0]

    def body(ql_ref, kl_ref, kr_ref, wuq_ref, wk_ref, wv_ref, gq_ref, gkv_ref, c_ref, a_ref, b_ref,
             q_out, k_out, v_out, qn_out, kvn_out):
        c, a, b = c_ref[...], a_ref[...], b_ref[...]
        ql = ql_ref[...].astype(F32)
        qn = (ql * _rsq(ql) * gq_ref[...]).astype(BF16)
        qn_out[...] = qn
        q = jnp.dot(qn, wuq_ref[...], preferred_element_type=F32)
        kl = kl_ref[...].astype(F32)
        kvn = (kl * _rsq(kl) * gkv_ref[...]).astype(BF16)
        kvn_out[...] = kvn
        kn = jnp.dot(kvn, wk_ref[...], preferred_element_type=F32)
        v_out[...] = jnp.dot(kvn, wv_ref[...], preferred_element_type=F32).astype(BF16)
        kr = _rope(kr_ref[...].astype(F32), c, a, b).astype(BF16)
        for h in range(n_heads):
            o = h * HEAD_W
            q_out[:, o:o + QK_NOPE] = (q[:, o:o + QK_NOPE] * scale).astype(BF16)
            q_out[:, o + QK_NOPE:o + HEAD_W] = (_rope(q[:, o + QK_NOPE:o + HEAD_W], c, a, b) * scale).astype(BF16)
            k_out[:, o:o + QK_NOPE] = kn[:, h * QK_NOPE:(h + 1) * QK_NOPE].astype(BF16)
            k_out[:, o + QK_NOPE:o + HEAD_W] = kr

    def full(arr):
        return pl.BlockSpec(arr.shape, lambda i: (0, 0))

    tab = pl.BlockSpec((ts, LANE), lambda i: (i, 0))
    hw, hv = n_heads * HEAD_W, n_heads * V_DIM
    return pl.pallas_call(
        body, name="qkv_fwd", grid=(s_len // ts,),
        in_specs=[pl.BlockSpec((ts, ql_w), lambda i: (i, lay["ql"] // ql_w)),
                  pl.BlockSpec((ts, kl_w), lambda i: (i, lay["kv"] // kl_w)),
                  pl.BlockSpec((ts, LANE), lambda i: (i, lay["kr"] // LANE)),
                  full(wuq), full(wk), full(wv), full(g_q), full(g_kv), tab, tab, tab],
        out_specs=[pl.BlockSpec((ts, hw), lambda i: (i, 0)), pl.BlockSpec((ts, hw), lambda i: (i, 0)),
                   pl.BlockSpec((ts, hv), lambda i: (i, 0)), pl.BlockSpec((ts, ql_w), lambda i: (i, 0)),
                   pl.BlockSpec((ts, kl_w), lambda i: (i, 0))],
        out_shape=[jax.ShapeDtypeStruct((s_len, hw), BF16), jax.ShapeDtypeStruct((s_len, hw), BF16),
                   jax.ShapeDtypeStruct((s_len, hv), BF16), jax.ShapeDtypeStruct((s_len, ql_w), BF16),
                   jax.ShapeDtypeStruct((s_len, kl_w), BF16)],
        compiler_params=_cparams("parallel"),
    )(proj, proj, proj, wuq, wk, wv, g_q, g_kv, ctab, atab, btab)


def _flash_fwd(q, k, v, n_heads, t, gather, out_cols):
    ng = len(gather)
    ck = _tile(t, PREF["attn_chunk"])
    hp = PREF["attn_heads"] if n_heads % PREF["attn_heads"] == 0 else 1
    n_groups = n_heads // hp
    s_len = q.shape[0]
    nb = s_len // t
    pairs = [(i, j) for i in range(nb) for j in range(i + 1)]
    itab = jnp.asarray(np.array([p[0] for p in pairs], np.int32))
    jtab = jnp.asarray(np.array([p[1] for p in pairs], np.int32))

    n_steps = len(pairs)

    def body(it_ref, jt_ref, q_ref, k_ref, v_ref, *rest):
        gin, (o_ref, lse_ref), gout = rest[:ng], rest[ng:ng + 2], rest[ng + 2:2 * ng + 2]
        m_sc, l_sc, acc_sc = rest[2 * ng + 2:2 * ng + 5]
        sems = rest[2 * ng + 5:]
        group, step_id = pl.program_id(0), pl.program_id(1)
        i, j = it_ref[step_id], jt_ref[step_id]

        @pl.when((group == 0) & (step_id == 0))
        def _():
            _gather_start(gin, gout, *sems)

        @pl.when((group == (3 * n_groups) // 4) & (step_id == 0))
        def _():
            _gather_forward(gin, gout, *sems)

        @pl.when(j == 0)
        def _():
            m_sc[...] = jnp.full(m_sc.shape, NEG, F32)
            l_sc[...] = jnp.zeros(l_sc.shape, F32)
            acc_sc[...] = jnp.zeros(acc_sc.shape, F32)

        def step(diag):
            for h in range(hp):
                qk, vc = slice(h * HEAD_W, (h + 1) * HEAD_W), slice(h * V_DIM, (h + 1) * V_DIM)
                cd = min(ck, t // 2) if diag else ck
                for c in range(t // cd):
                    q0 = c * cd if diag else 0
                    qs, ks = slice(q0, t), slice(c * cd, (c + 1) * cd)
                    s_t = lax.dot_general(k_ref[ks, qk], q_ref[qs, qk], NT_DIMS, preferred_element_type=F32)
                    if diag:
                        krow = lax.broadcasted_iota(jnp.int32, s_t.shape, 0)
                        qcol = lax.broadcasted_iota(jnp.int32, s_t.shape, 1)
                        s_t = jnp.where(krow <= qcol, s_t, NEG)
                    m_prev = m_sc[h, :, qs]
                    m_new = jnp.maximum(m_prev, jnp.max(s_t, axis=0, keepdims=True))
                    alpha = jnp.exp2(m_prev - m_new)
                    p_t = jnp.exp2(s_t - m_new)
                    l_sc[h, :, qs] = alpha * l_sc[h, :, qs] + jnp.sum(p_t, axis=0, keepdims=True)
                    acc_sc[h, :, qs] = acc_sc[h, :, qs] * alpha + lax.dot_general(
                        v_ref[ks, vc], p_t.astype(BF16), TN_DIMS, preferred_element_type=F32)
                    m_sc[h, :, qs] = m_new

        @pl.when(j < i)
        def _():
            step(False)

        @pl.when(j == i)
        def _():
            step(True)
            for h in range(hp):
                l = l_sc[h]
                o_ref[:, h * V_DIM:(h + 1) * V_DIM] = jnp.transpose(acc_sc[h] / l).astype(BF16)
                lse_ref[h] = m_sc[h] + jnp.log(l) * LOG2E

        @pl.when((group == n_groups - 1) & (step_id == n_steps - 1))
        def _():
            _gather_finish(gin, gout, *sems)

    hbm = pl.BlockSpec(memory_space=pl.ANY)
    grid_spec = pltpu.PrefetchScalarGridSpec(
        num_scalar_prefetch=2, grid=(n_groups, n_steps),
        in_specs=[pl.BlockSpec((t, hp * HEAD_W), lambda g, s, it, jt: (it[s], g)),
                  pl.BlockSpec((t, hp * HEAD_W), lambda g, s, it, jt: (jt[s], g)),
                  pl.BlockSpec((t, hp * V_DIM), lambda g, s, it, jt: (jt[s], g))] + [hbm] * ng,
        out_specs=[pl.BlockSpec((t, hp * V_DIM), lambda g, s, it, jt: (it[s], g)),
                   pl.BlockSpec((hp, 1, t), lambda g, s, it, jt: (g, 0, it[s]))] + [hbm] * ng,
        scratch_shapes=[pltpu.VMEM((hp, 1, t), F32), pltpu.VMEM((hp, 1, t), F32), pltpu.VMEM((hp, V_DIM, t), F32)]
        + _comm_scratch(ng),
    )
    return pl.pallas_call(
        body, name="flash_fwd", grid_spec=grid_spec,
        out_shape=[jax.ShapeDtypeStruct((s_len, out_cols), BF16),
                   jax.ShapeDtypeStruct((n_heads, 1, s_len), F32)] + _gathered_shapes(gather),
        compiler_params=_cparams("arbitrary", "arbitrary"),
    )(itab, jtab, q, k, v, *gather)


def _gconv_fwd(proj, lay, w, b, ts, tc, mixcat):
    s_len = proj.shape[0]
    cw = w.shape[1]
    nj = cw // tc
    out_off = mixcat.shape[1] - cw

    def body(gb_ref, gc_ref, ci_ref, gch_ref, cih_ref, w_ref, b_ref, mix_in, o_ref):
        i = pl.program_id(1)
        p = gc_ref[...].astype(F32) * ci_ref[...].astype(F32)
        ph = jnp.where(i > 0, gch_ref[...].astype(F32) * cih_ref[...].astype(F32), 0.0)
        o_ref[...] = (gb_ref[...].astype(F32) * _conv3(p, ph, w_ref, b_ref)).astype(BF16)

    def blk(off):
        return pl.BlockSpec((ts, tc), lambda j, i: (i, off // tc + j))

    def halo(off):
        return pl.BlockSpec((SUB, tc), _prev_halo(ts, lambda j: off // tc + j))

    return pl.pallas_call(
        body, name="gconv_fwd", grid=(nj, s_len // ts),
        in_specs=[blk(lay["gb"]), blk(lay["gc"]), blk(lay["ci"]), halo(lay["gc"]), halo(lay["ci"]),
                  pl.BlockSpec((3, tc), lambda j, i: (0, j)), pl.BlockSpec((1, tc), lambda j, i: (0, j)),
                  pl.BlockSpec(memory_space=pl.ANY)],
        out_specs=pl.BlockSpec((ts, tc), lambda j, i: (i, out_off // tc + j)),
        out_shape=jax.ShapeDtypeStruct(mixcat.shape, BF16),
        input_output_aliases={7: 0},
        compiler_params=_cparams("parallel", "parallel"),
    )(proj, proj, proj, proj, proj, w, b, mixcat)


def _post_mix_fwd(mix, x, gt, g_post, g_pre, sc, sh, ts):
    s_len, d = x.shape

    def body(mix_ref, x_ref, gt_ref, gp_ref, g2_ref, sc_ref, sh_ref, x1_ref, h2_ref):
        mv = mix_ref[...].astype(F32)
        x1 = x_ref[...] + gt_ref[...] * (mv * _rsq(mv) * gp_ref[...])
        x1_ref[...] = x1
        h2_ref[...] = ((x1 * _rsq(x1) * g2_ref[...]) * (1.0 + sc_ref[...]) + sh_ref[...]).astype(BF16)

    vec = pl.BlockSpec((1, d), lambda i: (0, 0))
    row = pl.BlockSpec((ts, d), lambda i: (i, 0))
    return pl.pallas_call(
        body, name="post_mix_fwd", grid=(s_len // ts,),
        in_specs=[row, row, vec, vec, vec, vec, vec], out_specs=[row, row],
        out_shape=[jax.ShapeDtypeStruct((s_len, d), F32), jax.ShapeDtypeStruct((s_len, d), BF16)],
        compiler_params=_cparams("parallel"),
    )(mix, x, gt, g_post, g_pre, sc, sh)


def _act_fwd(up, w, b, ts, tc):
    s_len, f2 = up.shape
    nh = f2 // 2 // tc
    rs_n = _tile(ts, STRIP_ROWS, HALO)

    def body(up_ref, uph_ref, w_ref, b_ref, u_ref, o_ref):
        i = pl.program_id(1)

        def conv_strip(r0, lanes):
            if r0 == 0:
                top = jnp.where(i > 0, uph_ref[:, lanes].astype(F32), 0.0)
                xe = jnp.concatenate([top, up_ref[0:rs_n, lanes].astype(F32)], axis=0)
            else:
                xe = up_ref[r0 - HALO:r0 + rs_n, lanes].astype(F32)
            u = (pltpu.roll(xe, 2, 0)[HALO:] * w_ref[0:1, lanes] + pltpu.roll(xe, 1, 0)[HALO:] * w_ref[1:2, lanes]
                 + xe[HALO:] * w_ref[2:3, lanes] + b_ref[:, lanes])
            u_ref[r0:r0 + rs_n, lanes] = u.astype(BF16)
            return u

        for r0 in range(0, ts, rs_n):
            for c0 in range(0, tc, LANE):
                ua = conv_strip(r0, slice(c0, c0 + LANE))
                ug = conv_strip(r0, slice(tc + c0, tc + c0 + LANE))
                o_ref[r0:r0 + rs_n, c0:c0 + LANE] = (ug * _sigmoid(ug) * ua).astype(BF16)

    pair = pl.BlockSpec((ts, 2 * tc), lambda j, i: (i, j))
    return pl.pallas_call(
        body, name="act_fwd", grid=(nh, s_len // ts),
        in_specs=[pair, pl.BlockSpec((HALO, 2 * tc), lambda j, i: (jnp.maximum(i * (ts // HALO) - 1, 0), j)),
                  pl.BlockSpec((3, 2 * tc), lambda j, i: (0, j)), pl.BlockSpec((1, 2 * tc), lambda j, i: (0, j))],
        out_specs=[pair, pl.BlockSpec((ts, tc), lambda j, i: (i, j))],
        out_shape=[jax.ShapeDtypeStruct((s_len, f2), BF16), jax.ShapeDtypeStruct((s_len, f2 // 2), BF16)],
        compiler_params=_cparams("parallel", "parallel"),
    )(up, up, w, b)


def _final_bwd(y, x1, tgt, gt, g_post, ts):
    s_len, d = y.shape

    n_i = s_len // ts

    def body(y_ref, x1_ref, t_ref, gt_ref, g_ref, dy_ref, dx2_ref, loss_ref, dgt_ref, dg_ref):
        i = pl.program_id(0)
        gtg = gt_ref[...] * g_ref[...]
        yv = y_ref[...].astype(F32)
        r = _rsq(yv)
        yh = yv * r
        e = x1_ref[...] + yh * gtg - t_ref[...]
        loss = 0.5 * jnp.sum(jnp.mean(e * e, axis=-1, keepdims=True), axis=0, keepdims=True)
        dx2 = e * (1.0 / d)
        dx2_ref[...] = dx2
        dyh = dx2 * gtg
        dy_ref[...] = (r * (dyh - yh * jnp.mean(dyh * yh, axis=-1, keepdims=True))).astype(BF16)

        @pl.when(i == 0)
        def _():
            loss_ref[...] = jnp.zeros(loss_ref.shape, F32)
            dgt_ref[...] = jnp.zeros(dgt_ref.shape, F32)

        loss_ref[...] += jnp.broadcast_to(loss, loss_ref.shape)
        dgt_ref[...] += _colsum(dx2 * yh)

        @pl.when(i == n_i - 1)
        def _():
            both = dgt_ref[...]
            dg_ref[...] = both * gt_ref[...]
            dgt_ref[...] = both * g_ref[...]

    vec = pl.BlockSpec((1, d), lambda i: (0, 0))
    row = pl.BlockSpec((ts, d), lambda i: (i, 0))
    vshape = jax.ShapeDtypeStruct((1, d), F32)
    return pl.pallas_call(
        body, name="final_bwd", grid=(s_len // ts,),
        in_specs=[row, row, row, vec, vec],
        out_specs=[row, row, pl.BlockSpec((1, LANE), lambda i: (0, 0)), vec, vec],
        out_shape=[jax.ShapeDtypeStruct((s_len, d), BF16), jax.ShapeDtypeStruct((s_len, d), F32),
                   jax.ShapeDtypeStruct((1, LANE), F32), vshape, vshape],
        compiler_params=_cparams("arbitrary"),
    )(y, x1, tgt, gt, g_post)


def _ffn_act_bwd(d_act, u, up, w, ts, tc):
    s_len, f2 = up.shape
    nh = f2 // 2 // tc
    n_i = s_len // ts
    rs_n = _tile(ts, STRIP_ROWS, HALO)
    ext = rs_n + HALO

    def body(d_ref, dh_ref, u_ref, uh_ref, x_ref, w_ref, dx_ref, dw_ref, db_ref, acc):
        i = pl.program_id(1)
        acc[...] = jnp.zeros(acc.shape, F32)

        def below(ref, halo_ref, r0, lanes):
            if r0 + ext <= ts:
                return ref[r0:r0 + ext, lanes].astype(F32)
            bot = jnp.where(i < n_i - 1, halo_ref[:, lanes].astype(F32), 0.0)
            return jnp.concatenate([ref[r0:ts, lanes].astype(F32), bot], axis=0)

        def fold8(v):
            return jnp.sum(v.reshape(v.shape[0] // SUB, SUB, v.shape[1]), axis=0)

        def conv_bwd_strip(du, r0, lanes):
            du1, du2 = pltpu.roll(du, ext - 1, 0)[:rs_n], pltpu.roll(du, ext - 2, 0)[:rs_n]
            du0 = du[:rs_n]
            dx_ref[r0:r0 + rs_n, lanes] = (du0 * w_ref[2:3, lanes] + du1 * w_ref[1:2, lanes]
                                           + du2 * w_ref[0:1, lanes]).astype(BF16)
            xv = x_ref[r0:r0 + rs_n, lanes].astype(F32)
            acc[0, :, lanes] += fold8(du2 * xv)
            acc[1, :, lanes] += fold8(du1 * xv)
            acc[2, :, lanes] += fold8(du0 * xv)
            acc[3, :, lanes] += fold8(du0)

        for r0 in range(0, ts, rs_n):
            for c0 in range(0, tc, LANE):
                la, lg = slice(c0, c0 + LANE), slice(tc + c0, tc + c0 + LANE)
                dv = below(d_ref, dh_ref, r0, la)
                ua, ug = below(u_ref, uh_ref, r0, la), below(u_ref, uh_ref, r0, lg)
                sg = _sigmoid(ug)
                conv_bwd_strip(dv * (ug * sg), r0, la)
                conv_bwd_strip(dv * ua * (sg * (1.0 + ug * (1.0 - sg))), r0, lg)

        @pl.when(i == 0)
        def _():
            dw_ref[...] = jnp.zeros(dw_ref.shape, F32)
            db_ref[...] = jnp.zeros(db_ref.shape, F32)

        dw_ref[...] += jnp.concatenate([_colsum(acc[k]) for k in range(3)], axis=0)
        db_ref[...] += _colsum(acc[3])

    pair = pl.BlockSpec((ts, 2 * tc), lambda j, i: (i, j))

    def nxt(j, i):
        return (jnp.minimum((i + 1) * (ts // HALO), s_len // HALO - 1), j)

    return pl.pallas_call(
        body, name="ffn_act_bwd", grid=(nh, n_i),
        in_specs=[pl.BlockSpec((ts, tc), lambda j, i: (i, j)), pl.BlockSpec((HALO, tc), nxt),
                  pair, pl.BlockSpec((HALO, 2 * tc), nxt), pair, pl.BlockSpec((3, 2 * tc), lambda j, i: (0, j))],
        out_specs=[pair, pl.BlockSpec((3, 2 * tc), lambda j, i: (0, j)), pl.BlockSpec((1, 2 * tc), lambda j, i: (0, j))],
        out_shape=[jax.ShapeDtypeStruct((s_len, f2), BF16), jax.ShapeDtypeStruct((3, f2), F32),
                   jax.ShapeDtypeStruct((1, f2), F32)],
        scratch_shapes=[pltpu.VMEM((4, SUB, 2 * tc), F32)],
        compiler_params=_cparams("parallel", "arbitrary"),
    )(d_act, d_act, u, u, up, w)


def _mid_bwd(dh2, x1, dx2, mix, g_pre, sc, gt_m, g_post, ts):
    s_len, d = x1.shape
    n_i = s_len // ts

    def body(dh_ref, x1_ref, dx2_ref, mix_ref, g_ref, sc_ref, gt_ref, gp_ref,
             dx1_ref, dmix_ref, dsh_ref, dsc_ref, dg_ref, dgt_ref, dgp_ref):
        i = pl.program_id(0)
        wv = (1.0 + sc_ref[...]) * g_ref[...]
        gtg = gt_ref[...] * gp_ref[...]
        dh = dh_ref[...].astype(F32)
        x1 = x1_ref[...]
        r1 = _rsq(x1)
        xh = x1 * r1
        dhx = dh * xh
        dx1 = dx2_ref[...] + r1 * (dh * wv - xh * jnp.mean(dhx * wv, axis=-1, keepdims=True))
        dx1_ref[...] = dx1
        mv = mix_ref[...].astype(F32)
        rm = _rsq(mv)
        mh = mv * rm
        dxm = dx1 * mh
        dmix_ref[...] = (rm * (dx1 * gtg - mh * jnp.mean(dxm * gtg, axis=-1, keepdims=True))).astype(BF16)

        @pl.when(i == 0)
        def _():
            for ref in (dsh_ref, dsc_ref, dgt_ref):
                ref[...] = jnp.zeros(ref.shape, F32)

        dsh_ref[...] += _colsum(dh)
        dsc_ref[...] += _colsum(dhx)
        dgt_ref[...] += _colsum(dxm)

        @pl.when(i == n_i - 1)
        def _():
            t1, t2 = dsc_ref[...], dgt_ref[...]
            dsc_ref[...] = t1 * g_ref[...]
            dg_ref[...] = t1 * (1.0 + sc_ref[...])
            dgt_ref[...] = t2 * gp_ref[...]
            dgp_ref[...] = t2 * gt_ref[...]

    vec = pl.BlockSpec((1, d), lambda i: (0, 0))
    row = pl.BlockSpec((ts, d), lambda i: (i, 0))
    vshape = jax.ShapeDtypeStruct((1, d), F32)
    return pl.pallas_call(
        body, name="mid_bwd", grid=(s_len // ts,),
        in_specs=[row, row, row, row, vec, vec, vec, vec],
        out_specs=[row, row, vec, vec, vec, vec, vec],
        out_shape=[jax.ShapeDtypeStruct((s_len, d), F32), jax.ShapeDtypeStruct((s_len, d), BF16)] + [vshape] * 5,
        compiler_params=_cparams("arbitrary"),
    )(dh2, x1, dx2, mix, g_pre, sc, gt_m, g_post)


def _first_bwd(dh1, x, dx1, g_pre, sc, ts):
    s_len, d = x.shape
    n_i = s_len // ts

    def body(dh_ref, x_ref, dx1_ref, g_ref, sc_ref, dx_ref, dsh_ref, dsc_ref, dg_ref):
        i = pl.program_id(0)
        wv = (1.0 + sc_ref[...]) * g_ref[...]
        dh = dh_ref[...].astype(F32)
        xv = x_ref[...]
        r = _rsq(xv)
        xh = xv * r
        dhx = dh * xh
        dx_ref[...] = dx1_ref[...] + r * (dh * wv - xh * jnp.mean(dhx * wv, axis=-1, keepdims=True))

        @pl.when(i == 0)
        def _():
            for ref in (dsh_ref, dsc_ref):
                ref[...] = jnp.zeros(ref.shape, F32)

        dsh_ref[...] += _colsum(dh)
        dsc_ref[...] += _colsum(dhx)

        @pl.when(i == n_i - 1)
        def _():
            t1 = dsc_ref[...]
            dsc_ref[...] = t1 * g_ref[...]
            dg_ref[...] = t1 * (1.0 + sc_ref[...])

    vec = pl.BlockSpec((1, d), lambda i: (0, 0))
    row = pl.BlockSpec((ts, d), lambda i: (i, 0))
    vshape = jax.ShapeDtypeStruct((1, d), F32)
    return pl.pallas_call(
        body, name="first_bwd", grid=(s_len // ts,),
        in_specs=[row, row, row, vec, vec], out_specs=[row, vec, vec, vec],
        out_shape=[jax.ShapeDtypeStruct((s_len, d), F32), vshape, vshape, vshape],
        compiler_params=_cparams("arbitrary"),
    )(dh1, x, dx1, g_pre, sc)


def _gconv_bwd(d_mixcat, proj, lay, w, b, ts, tc):
    s_len = proj.shape[0]
    cw = w.shape[1]
    n_i = s_len // ts
    dc_off = d_mixcat.shape[1] - cw

    def body(dc_ref, dch_ref, gb_ref, gbh_ref, gc_ref, gch_ref, ci_ref, cih_ref, w_ref, b_ref,
             dgb_ref, dgc_ref, dci_ref, dw_ref, db_ref):
        i = pl.program_id(1)
        gc, ci = gc_ref[...].astype(F32), ci_ref[...].astype(F32)
        p = gc * ci
        ph = jnp.where(i > 0, gch_ref[...].astype(F32) * cih_ref[...].astype(F32), 0.0)
        pm1, pm2 = _shift_down(p, ph, 1), _shift_down(p, ph, 2)
        z = pm2 * w_ref[0:1, :] + pm1 * w_ref[1:2, :] + p * w_ref[2:3, :] + b_ref[...]
        dc = dc_ref[...].astype(F32)
        dgb_ref[...] = (dc * z).astype(BF16)
        dz = dc * gb_ref[...].astype(F32)
        dzh = jnp.where(i < n_i - 1, dch_ref[...].astype(F32) * gbh_ref[...].astype(F32), 0.0)
        dz1, dz2 = _shift_up(dz, dzh, 1), _shift_up(dz, dzh, 2)
        dp = dz * w_ref[2:3, :] + dz1 * w_ref[1:2, :] + dz2 * w_ref[0:1, :]
        dgc_ref[...] = (dp * ci).astype(BF16)
        dci_ref[...] = (dp * gc).astype(BF16)

        @pl.when(i == 0)
        def _():
            dw_ref[...] = jnp.zeros(dw_ref.shape, F32)
            db_ref[...] = jnp.zeros(db_ref.shape, F32)

        dw_ref[0:1, :] += _colsum(dz2 * p)
        dw_ref[1:2, :] += _colsum(dz1 * p)
        dw_ref[2:3, :] += _colsum(dz * p)
        db_ref[...] += _colsum(dz)

    def blk(off):
        return pl.BlockSpec((ts, tc), lambda j, i: (i, off // tc + j))

    def prev(off):
        return pl.BlockSpec((SUB, tc), _prev_halo(ts, lambda j: off // tc + j))

    def nxt(off):
        return pl.BlockSpec((SUB, tc), _next_halo(ts, s_len, lambda j: off // tc + j))

    out_blk = pl.BlockSpec((ts, tc), lambda j, i: (i, j))
    act = jax.ShapeDtypeStruct((s_len, cw), BF16)
    return pl.pallas_call(
        body, name="gconv_bwd", grid=(cw // tc, n_i),
        in_specs=[blk(dc_off), nxt(dc_off), blk(lay["gb"]), nxt(lay["gb"]), blk(lay["gc"]), prev(lay["gc"]),
                  blk(lay["ci"]), prev(lay["ci"]),
                  pl.BlockSpec((3, tc), lambda j, i: (0, j)), pl.BlockSpec((1, tc), lambda j, i: (0, j))],
        out_specs=[out_blk, out_blk, out_blk,
                   pl.BlockSpec((3, tc), lambda j, i: (0, j)), pl.BlockSpec((1, tc), lambda j, i: (0, j))],
        out_shape=[act, act, act, jax.ShapeDtypeStruct((3, cw), F32), jax.ShapeDtypeStruct((1, cw), F32)],
        compiler_params=_cparams("parallel", "arbitrary"),
    )(d_mixcat, d_mixcat, proj, proj, proj, proj, proj, proj, w, b)


def _delta(o, d_mixcat, n_heads, ts):
    s_len = o.shape[0]

    def body(o_ref, do_ref, out_ref):
        for h in range(n_heads):
            sl = slice(h * V_DIM, (h + 1) * V_DIM)
            prod = o_ref[:, sl].astype(F32) * do_ref[:, sl].astype(F32)
            out_ref[h] = jnp.broadcast_to(jnp.sum(prod, axis=1, keepdims=True), (ts, LANE))

    hv = n_heads * V_DIM
    return pl.pallas_call(
        body, name="attn_delta", grid=(s_len // ts,),
        in_specs=[pl.BlockSpec((ts, hv), lambda i: (i, 0)), pl.BlockSpec((ts, hv), lambda i: (i, 0))],
        out_specs=pl.BlockSpec((n_heads, ts, LANE), lambda i: (0, i, 0)),
        out_shape=jax.ShapeDtypeStruct((n_heads, s_len, LANE), F32),
        compiler_params=_cparams("parallel"),
    )(o, d_mixcat)


def _flash_bwd(q, k, v, d_mixcat, lse_row, delta_row, n_heads, t, scale, exchange):
    nx = len(exchange)
    s_len = q.shape[0]
    nb = s_len // t
    pairs = [(j, i) for j in range(nb) for i in range(j, nb)]
    jtab = jnp.asarray(np.array([p[0] for p in pairs], np.int32))
    itab = jnp.asarray(np.array([p[1] for p in pairs], np.int32))
    n_steps = len(pairs)

    def body(jt_ref, it_ref, q_ref, k_ref, v_ref, do_ref, lse_ref, dl_ref, *rest):
        xin, (dq_ref, dk_ref, dv_ref), xout = rest[:nx], rest[nx:nx + 3], rest[nx + 3:2 * nx + 3]
        dq_acc, dk_acc, dv_acc = rest[2 * nx + 3:2 * nx + 6]
        sems = rest[2 * nx + 6:]
        head, step_id = pl.program_id(0), pl.program_id(1)
        j, i = jt_ref[step_id], it_ref[step_id]

        @pl.when((head == 0) & (step_id == 0))
        def _():
            _exchange_start(xin, xout, *sems)

        @pl.when(step_id == 0)
        def _():
            dq_acc[...] = jnp.zeros(dq_acc.shape, F32)

        @pl.when(i == j)
        def _():
            dk_acc[...] = jnp.zeros(dk_acc.shape, F32)
            dv_acc[...] = jnp.zeros(dv_acc.shape, F32)

        def step(diag):
            half = t // 2
            lo, hi = slice(0, half), slice(half, t)
            blocks = [(lo, lo, True), (lo, hi, False), (hi, hi, True)] if diag else [(slice(0, t), slice(0, t), False)]
            for ks, qs, masked in blocks:
                qv, kv, vv, dov = q_ref[qs, :], k_ref[ks, :], v_ref[ks, :], do_ref[qs, :]
                s_t = lax.dot_general(kv, qv, NT_DIMS, preferred_element_type=F32)
                if masked:
                    krow = lax.broadcasted_iota(jnp.int32, s_t.shape, 0)
                    qcol = lax.broadcasted_iota(jnp.int32, s_t.shape, 1)
                    s_t = jnp.where(krow <= qcol, s_t, NEG)
                p_t = jnp.exp2(s_t - lse_ref[0, :, qs])
                dv_acc[ks, :] += jnp.dot(p_t.astype(BF16), dov, preferred_element_type=F32)
                dp_t = lax.dot_general(vv, dov, NT_DIMS, preferred_element_type=F32)
                ds_t = (p_t * (dp_t - dl_ref[0, :, qs])).astype(BF16)
                dk_acc[ks, :] += jnp.dot(ds_t, qv, preferred_element_type=F32)
                n_q = qs.stop - qs.start
                rows = pl.ds(pl.multiple_of(i * t + qs.start, n_q), n_q)
                dq_acc[rows, :] += lax.dot_general(ds_t, kv, TN_DIMS, preferred_element_type=F32)

        @pl.when(i > j)
        def _():
            step(False)

        @pl.when(i == j)
        def _():
            step(True)

        @pl.when(i == nb - 1)
        def _():
            dk_ref[...] = (dk_acc[...] * LN2).astype(BF16)
            dv_ref[...] = dv_acc[...].astype(BF16)

        @pl.when(step_id == n_steps - 1)
        def _():
            dq_ref[...] = (dq_acc[...] * scale).astype(BF16)

        @pl.when((head == n_heads - 1) & (step_id == n_steps - 1))
        def _():
            _exchange_finish(xin, xout, *sems)

    hbm = pl.BlockSpec(memory_space=pl.ANY)
    hv = n_heads * V_DIM
    do_off = 0
    grid_spec = pltpu.PrefetchScalarGridSpec(
        num_scalar_prefetch=2, grid=(n_heads, n_steps),
        in_specs=[pl.BlockSpec((t, HEAD_W), lambda h, s, jt, it: (it[s], h)),
                  pl.BlockSpec((t, HEAD_W), lambda h, s, jt, it: (jt[s], h)),
                  pl.BlockSpec((t, V_DIM), lambda h, s, jt, it: (jt[s], h)),
                  pl.BlockSpec((t, V_DIM), lambda h, s, jt, it: (it[s], do_off + h)),
                  pl.BlockSpec((1, 1, t), lambda h, s, jt, it: (h, 0, it[s])),
                  pl.BlockSpec((1, 1, t), lambda h, s, jt, it: (h, 0, it[s]))] + [hbm] * nx,
        out_specs=[pl.BlockSpec((s_len, HEAD_W), lambda h, s, jt, it: (0, h)),
                   pl.BlockSpec((t, HEAD_W), lambda h, s, jt, it: (jt[s], h)),
                   pl.BlockSpec((t, V_DIM), lambda h, s, jt, it: (jt[s], h))] + [hbm] * nx,
        scratch_shapes=[pltpu.VMEM((s_len, HEAD_W), F32), pltpu.VMEM((t, HEAD_W), F32), pltpu.VMEM((t, V_DIM), F32)]
        + _comm_scratch(nx),
    )
    return pl.pallas_call(
        body, name="flash_bwd", grid_spec=grid_spec,
        out_shape=[jax.ShapeDtypeStruct((s_len, n_heads * HEAD_W), BF16),
                   jax.ShapeDtypeStruct((s_len, n_heads * HEAD_W), BF16),
                   jax.ShapeDtypeStruct((s_len, hv), BF16)] + [jax.ShapeDtypeStruct(c.shape, c.dtype) for c in exchange],
        compiler_params=_cparams("arbitrary", "arbitrary"),
    )(jtab, itab, q, k, v, d_mixcat, lse_row, delta_row, *exchange)


def _qkv_bwd(dq, dk, dv, proj, lay, wuq, wk, wv, g_q, g_kv, ctab, atab, btab, n_heads, ts):
    s_len = proj.shape[0]
    ql_w, kl_w = wuq.shape[0], wk.shape[0]
    tail_w = lay["np"] - lay["ql"]
    kv_o, kr_o = lay["kv"] - lay["ql"], lay["kr"] - lay["ql"]

    def body(dq_ref, dk_ref, dv_ref, ql_ref, kl_ref, wuq_ref, wk_ref, wv_ref, gq_ref, gkv_ref, c_ref, a_ref, b_ref,
             dqr_ref, dkn_ref, tail_ref, dgq_ref, dgkv_ref):
        i = pl.program_id(0)
        c, a, b = c_ref[...], a_ref[...], b_ref[...]
        dkr = jnp.zeros((ts, LANE), F32)
        for h in range(n_heads):
            o = h * HEAD_W
            dqr_ref[:, o:o + QK_NOPE] = dq_ref[:, o:o + QK_NOPE]
            dqr_ref[:, o + QK_NOPE:o + HEAD_W] = _rope_t(dq_ref[:, o + QK_NOPE:o + HEAD_W].astype(F32), c, a, b).astype(BF16)
            dkn_ref[:, h * QK_NOPE:(h + 1) * QK_NOPE] = dk_ref[:, o:o + QK_NOPE]
            dkr = dkr + dk_ref[:, o + QK_NOPE:o + HEAD_W].astype(F32)
        tail_ref[...] = jnp.zeros(tail_ref.shape, BF16)
        tail_ref[:, kr_o:kr_o + LANE] = _rope_t(dkr, c, a, b).astype(BF16)

        def rms_bwd(lat_ref, dn, g_ref):
            lat = lat_ref[...].astype(F32)
            r = _rsq(lat)
            xh = lat * r
            dxh = dn * g_ref[...]
            return r * (dxh - xh * jnp.mean(dxh * xh, axis=-1, keepdims=True)), _colsum(dn * xh)

        dqn = lax.dot_general(dqr_ref[...], wuq_ref[...], NT_DIMS, preferred_element_type=F32)
        d_ql, dgq = rms_bwd(ql_ref, dqn, gq_ref)
        tail_ref[:, 0:ql_w] = d_ql.astype(BF16)
        dkvn = (lax.dot_general(dkn_ref[...], wk_ref[...], NT_DIMS, preferred_element_type=F32)
                + lax.dot_general(dv_ref[...], wv_ref[...], NT_DIMS, preferred_element_type=F32))
        d_kl, dgkv = rms_bwd(kl_ref, dkvn, gkv_ref)
        tail_ref[:, kv_o:kv_o + kl_w] = d_kl.astype(BF16)

        @pl.when(i == 0)
        def _():
            dgq_ref[...] = jnp.zeros(dgq_ref.shape, F32)
            dgkv_ref[...] = jnp.zeros(dgkv_ref.shape, F32)

        dgq_ref[...] += dgq
        dgkv_ref[...] += dgkv

    def full(arr):
        return pl.BlockSpec(arr.shape, lambda i: (0, 0))

    def rows(w):
        return pl.BlockSpec((ts, w), lambda i: (i, 0))

    tab = pl.BlockSpec((ts, LANE), lambda i: (i, 0))
    hw, hv, hn = n_heads * HEAD_W, n_heads * V_DIM, n_heads * QK_NOPE
    return pl.pallas_call(
        body, name="qkv_bwd", grid=(s_len // ts,),
        in_specs=[rows(hw), rows(hw), rows(hv),
                  pl.BlockSpec((ts, ql_w), lambda i: (i, lay["ql"] // ql_w)),
                  pl.BlockSpec((ts, kl_w), lambda i: (i, lay["kv"] // kl_w)),
                  full(wuq), full(wk), full(wv), full(g_q), full(g_kv), tab, tab, tab],
        out_specs=[rows(hw), rows(hn), rows(tail_w), full(g_q), full(g_kv)],
        out_shape=[jax.ShapeDtypeStruct((s_len, hw), BF16), jax.ShapeDtypeStruct((s_len, hn), BF16),
                   jax.ShapeDtypeStruct((s_len, tail_w), BF16),
                   jax.ShapeDtypeStruct(g_q.shape, F32), jax.ShapeDtypeStruct(g_kv.shape, F32)],
        compiler_params=_cparams("arbitrary"),
    )(dq, dk, dv, proj, proj, wuq, wk, wv, g_q, g_kv, ctab, atab, btab)


def _adamw(w, g, m, v):
    m = ADAM_B1 * m + (1.0 - ADAM_B1) * g
    v = ADAM_B2 * v + (1.0 - ADAM_B2) * (g * g)
    m_hat = m / (1.0 - ADAM_B1 ** ADAM_STEP)
    v_hat = v / (1.0 - ADAM_B2 ** ADAM_STEP)
    delta = -ADAM_LR * (m_hat / (jnp.sqrt(v_hat) + ADAM_EPS) + ADAM_WD * w)
    return delta, m, v


def _adam_parts(parts, w, m, v, tr, name):
    r, c = w.shape
    tr = _tile(r, tr, SUB)

    def body(p_ref, w_ref, m_ref, v_ref, g_out, d_out, m_out, v_out):
        g = p_ref[0].astype(F32)
        for dev in range(1, N_DEV):
            g = g + p_ref[dev].astype(F32)
        g_out[...] = g
        d_out[...], m_out[...], v_out[...] = _adamw(w_ref[...], g, m_ref[...], v_ref[...])

    blk = pl.BlockSpec((tr, c), lambda i: (i, 0))
    shp = jax.ShapeDtypeStruct((r, c), F32)
    return pl.pallas_call(
        body, name=name, grid=(r // tr,),
        in_specs=[pl.BlockSpec((N_DEV, tr, c), lambda i: (0, i, 0)), blk, blk, blk],
        out_specs=[blk, blk, blk, blk], out_shape=[shp, shp, shp, shp],
        compiler_params=_cparams("parallel"),
    )(parts, w, m, v)


def _adam_ada(cact_t, dmod_sh, w, m, v, tr):
    r, c = w.shape

    def body(ct_ref, dm_ref, w_ref, m_ref, v_ref, g_out, d_out, m_out, v_out):
        g = jnp.dot(ct_ref[...], dm_ref[...], preferred_element_type=F32, precision=lax.Precision.HIGHEST)
        g_out[...] = g
        d_out[...], m_out[...], v_out[...] = _adamw(w_ref[...], g, m_ref[...], v_ref[...])

    blk = pl.BlockSpec((tr, c), lambda i: (i, 0))
    shp = jax.ShapeDtypeStruct((r, c), F32)
    return pl.pallas_call(
        body, name="adam_ada", grid=(r // tr,),
        in_specs=[pl.BlockSpec((tr, N_DEV), lambda i: (i, 0)), pl.BlockSpec((N_DEV, c), lambda i: (0, 0)), blk, blk, blk],
        out_specs=[blk, blk, blk, blk], out_shape=[shp, shp, shp, shp],
        compiler_params=_cparams("parallel"),
    )(cact_t, dmod_sh, w, m, v)


def _adam_small(v_all, offs, ws, ms, vs):
    n_par = len(ws)

    def body(p_ref, *refs):
        w_refs, m_refs, v_refs = refs[:n_par], refs[n_par:2 * n_par], refs[2 * n_par:3 * n_par]
        sum_ref = refs[3 * n_par]
        outs = refs[3 * n_par + 1:]
        g = p_ref[0:1, :]
        for dev in range(1, N_DEV):
            g = g + p_ref[dev:dev + 1, :]
        sum_ref[...] = g
        for p in range(n_par):
            n = w_refs[p].shape[1]
            gp = sum_ref[:, offs[p]:offs[p] + n]
            outs[p][...] = gp
            (outs[n_par + p][...], outs[2 * n_par + p][...], outs[3 * n_par + p][...]) = _adamw(
                w_refs[p][...], gp, m_refs[p][...], v_refs[p][...])

    vm = pl.BlockSpec(memory_space=pltpu.VMEM)
    shapes = [jax.ShapeDtypeStruct(w.shape, F32) for w in ws]
    out = pl.pallas_call(
        body, name="adam_small", in_specs=[vm] * (1 + 3 * n_par), out_specs=[vm] * (1 + 4 * n_par),
        out_shape=[jax.ShapeDtypeStruct((1, v_all.shape[1]), F32)] + shapes * 4, compiler_params=_cparams(),
    )(v_all, *ws, *ms, *vs)
    return out[0], [out[1 + k * n_par:1 + (k + 1) * n_par] for k in range(4)]


def _my_place():
    return lax.axis_index("x"), lax.axis_index("y"), lax.axis_index("c")


def _peer(place, k):
    x, y, c = place
    return (x ^ (k >> 2), y ^ ((k >> 1) & 1), c ^ (k & 1))


def _index(place):
    return 4 * place[0] + 2 * place[1] + place[2]


def _ada_fwd(vec, w_ada, b_ada_rows):
    lv = vec.shape[1]
    d, c = w_ada.shape

    def body(vec_ref, w_ref, b_ref, gath_ref, cact_ref, mod_ref, modsh, send_a, recv_a, send_b, recv_b, local_s):
        me = _my_place()
        my_i = _index(me)

        def gather_copy(k, to, src_row):
            row = gath_ref.at[pl.ds(src_row, 1), :]
            return pltpu.make_async_remote_copy(src_ref=row, dst_ref=row, send_sem=send_a.at[k], recv_sem=recv_a.at[k],
                                                device_id=to, device_id_type=MESH)

        own = pltpu.make_async_copy(vec_ref, gath_ref.at[pl.ds(my_i, 1), :], local_s.at[0])
        own.start()
        own.wait()
        sends = [gather_copy(k, _peer(me, k), my_i) for k in range(1, N_DEV)]
        for cp in sends:
            cp.start()
        for k in range(1, N_DEV):
            gather_copy(k, me, _index(_peer(me, k))).wait_recv()
        for cp in sends:
            cp.wait_send()

        c_all = gath_ref[:, 0:d]
        cact = c_all * _sigmoid(c_all)
        cact_ref[...] = cact
        modsh[...] = jnp.dot(cact, w_ref[...], preferred_element_type=F32, precision=lax.Precision.HIGHEST)

        def mod_copy(k, to, src_row, dst_row):
            return pltpu.make_async_remote_copy(src_ref=modsh.at[pl.ds(src_row, 1), :], dst_ref=mod_ref.at[pl.ds(dst_row, 1), :],
                                                send_sem=send_b.at[k], recv_sem=recv_b.at[k],
                                                device_id=to, device_id_type=MESH)

        own = pltpu.make_async_copy(modsh.at[pl.ds(my_i, 1), :], mod_ref.at[pl.ds(my_i, 1), :], local_s.at[1])
        own.start()
        sends = [mod_copy(k, _peer(me, k), _index(_peer(me, k)), my_i) for k in range(1, N_DEV)]
        for cp in sends:
            cp.start()
        for k in range(1, N_DEV):
            mod_copy(k, me, my_i, _index(_peer(me, k))).wait_recv()
        for cp in sends:
            cp.wait_send()
        own.wait()
        mod_ref[...] = mod_ref[...] + b_ref[...]

    vm = pl.BlockSpec(memory_space=pltpu.VMEM)
    return pl.pallas_call(
        body, name="ada_fwd", in_specs=[vm, vm, vm], out_specs=[vm, vm, vm],
        out_shape=[jax.ShapeDtypeStruct((N_DEV, lv), F32), jax.ShapeDtypeStruct((N_DEV, d), F32),
                   jax.ShapeDtypeStruct((N_DEV, c), F32)],
        scratch_shapes=[pltpu.VMEM((N_DEV, c), F32)] + [pltpu.SemaphoreType.DMA((N_DEV,))] * 4
        + [pltpu.SemaphoreType.DMA((2,))],
        compiler_params=pltpu.CompilerParams(vmem_limit_bytes=VMEM_LIMIT),
    )(vec, w_ada, b_ada_rows)


def _gather_small(vec):
    lv = vec.shape[1]

    def body(vec_ref, gath_ref, send_s, recv_s, local_s):
        me = _my_place()
        my_i = _index(me)

        def copy(k, to, src_row):
            row = gath_ref.at[pl.ds(src_row, 1), :]
            return pltpu.make_async_remote_copy(src_ref=row, dst_ref=row, send_sem=send_s.at[k], recv_sem=recv_s.at[k],
                                                device_id=to, device_id_type=MESH)

        own = pltpu.make_async_copy(vec_ref, gath_ref.at[pl.ds(my_i, 1), :], local_s)
        own.start()
        own.wait()
        sends = [copy(k, _peer(me, k), my_i) for k in range(1, N_DEV)]
        for cp in sends:
            cp.start()
        for k in range(1, N_DEV):
            copy(k, me, _index(_peer(me, k))).wait_recv()
        for cp in sends:
            cp.wait_send()

    vm = pl.BlockSpec(memory_space=pltpu.VMEM)
    return pl.pallas_call(
        body, name="gather_small", in_specs=[vm], out_specs=vm,
        out_shape=jax.ShapeDtypeStruct((N_DEV, lv), F32),
        scratch_shapes=[pltpu.SemaphoreType.DMA((N_DEV,))] * 2 + [pltpu.SemaphoreType.DMA],
        compiler_params=pltpu.CompilerParams(vmem_limit_bytes=VMEM_LIMIT),
    )(vec)


PER = N_DEV - 1


def _comm_scratch(n):
    return [pltpu.SemaphoreType.DMA((n * PER,)), pltpu.SemaphoreType.DMA((n * PER,)), pltpu.SemaphoreType.DMA((n,))]


def _gather_copies(ins, outs, send_s, recv_s, local_s):
    n = len(ins)
    me = _my_place()
    x, y, c = me
    sibling = (x, y, 1 - c)
    chips = [(1 - x, y), (x, 1 - y), (1 - x, 1 - y)]

    def copy(a, k, block, to, src=None):
        slot = outs[a].at[_index(block)]
        return pltpu.make_async_remote_copy(src_ref=slot if src is None else src, dst_ref=slot,
                                            send_sem=send_s.at[a * PER + k], recv_sem=recv_s.at[a * PER + k],
                                            device_id=to, device_id_type=MESH)

    mine = [pltpu.make_async_copy(ins[a], outs[a].at[_index(me)], local_s.at[a]) for a in range(n)]
    first = []
    for a in range(n):
        first.append(copy(a, 0, me, sibling, src=ins[a]))
        first += [copy(a, 1 + j, me, (*chip, c), src=ins[a]) for j, chip in enumerate(chips)]
    landed = [copy(a, 1 + j, (*chip, c), me) for j, chip in enumerate(chips) for a in range(n)]
    passed = [copy(a, 4 + j, (*chip, c), sibling) for j, chip in enumerate(chips) for a in range(n)]
    from_sibling = [copy(a, 0, sibling, me) for a in range(n)]
    from_sibling += [copy(a, 4 + j, (*chip, 1 - c), me) for a in range(n) for j, chip in enumerate(chips)]
    return mine, first, landed, passed, from_sibling


def _gather_start(*refs):
    mine, first, _, _, _ = _gather_copies(*refs)
    for cp in mine + first:
        cp.start()


def _gather_forward(*refs):
    _, _, landed, passed, _ = _gather_copies(*refs)
    for got, fwd in zip(landed, passed):
        got.wait_recv()
        fwd.start()


def _gather_finish(*refs):
    mine, first, _, passed, from_sibling = _gather_copies(*refs)
    for cp in from_sibling:
        cp.wait_recv()
    for cp in first + passed:
        cp.wait_send()
    for cp in mine:
        cp.wait()


def _exchange_copies(ins, outs, send_s, recv_s, local_s):
    n = len(ins)
    me = _my_place()
    my_i = _index(me)

    def copy(a, k, to, src_slot, dst_slot):
        return pltpu.make_async_remote_copy(src_ref=ins[a].at[src_slot], dst_ref=outs[a].at[dst_slot],
                                            send_sem=send_s.at[a * PER + k - 1], recv_sem=recv_s.at[a * PER + k - 1],
                                            device_id=to, device_id_type=MESH)

    mine = [pltpu.make_async_copy(ins[a].at[my_i], outs[a].at[my_i], local_s.at[a]) for a in range(n)]
    sends = [copy(a, k, _peer(me, k), _index(_peer(me, k)), my_i) for k in range(1, N_DEV) for a in range(n)]
    recvs = [copy(a, k, me, my_i, _index(_peer(me, k))) for k in range(1, N_DEV) for a in range(n)]
    return mine, sends, recvs


def _exchange_start(*refs):
    mine, sends, _ = _exchange_copies(*refs)
    for cp in mine + sends:
        cp.start()


def _exchange_finish(*refs):
    mine, sends, recvs = _exchange_copies(*refs)
    for cp in recvs:
        cp.wait_recv()
    for cp in sends:
        cp.wait_send()
    for cp in mine:
        cp.wait()


def _gathered_shapes(shards):
    return [jax.ShapeDtypeStruct((N_DEV,) + s.shape, s.dtype) for s in shards]


def _proj_layout(cw, ql, kl):
    lay = {"gb": 0, "gc": cw, "ci": 2 * cw, "ql": 3 * cw}
    assert lay["ql"] % ql == 0
    lay["kv"] = _roundup(lay["ql"] + ql, kl)
    lay["kr"] = lay["kv"] + kl
    lay["np"] = _roundup(lay["kr"] + LANE, 4 * LANE)
    return lay


def _chunks_cols(g):
    r, c8 = g.shape
    return jnp.transpose(g.reshape(r, N_DEV, c8 // N_DEV), (1, 0, 2))


def _from_col_shards(a):
    n, r, c = a.shape
    return jnp.transpose(a, (1, 0, 2)).reshape(r, n * c)


def kernel(x, c, positions, w_ada, b_ada, g_pre_mix, g_post_mix, w_in, g_q, w_uq, g_kv, w_ukv, conv_w_mix, conv_b_mix, w_o, g_pre_ffn, g_post_ffn, w_up, conv_w_ffn, conv_b_ffn, w_down, loss_target, m_w_ada, m_b_ada, m_g_pre_mix, m_g_post_mix, m_w_in, m_g_q, m_w_uq, m_g_kv, m_w_ukv, m_conv_w_mix, m_conv_b_mix, m_w_o, m_g_pre_ffn, m_g_post_ffn, m_w_up, m_conv_w_ffn, m_conv_b_ffn, m_w_down, v_w_ada, v_b_ada, v_g_pre_mix, v_g_post_mix, v_w_in, v_g_q, v_w_uq, v_g_kv, v_w_ukv, v_conv_w_mix, v_conv_b_mix, v_w_o, v_g_pre_ffn, v_g_post_ffn, v_w_up, v_conv_w_ffn, v_conv_b_ffn, v_w_down):
    s_len, d = x.shape[1], x.shape[2]
    ql, kl = w_uq.shape[1], w_ukv.shape[1]
    n_heads = w_ukv.shape[2] * N_DEV // (QK_NOPE + V_DIM)
    cw = conv_w_mix.shape[2] * N_DEV
    f2 = w_up.shape[2] * N_DEV
    ff = f2 // 2
    in_cols = w_in.shape[2] * N_DEV
    ada_c = w_ada.shape[2]
    cwm_c, cwf_c = conv_w_mix.shape[2], conv_w_ffn.shape[2]
    scale = 1.0 / math.sqrt(QK_NOPE + QK_ROPE)
    lay = _proj_layout(cw, ql, kl)
    n_pad = lay["np"]
    my_i = _index(_my_place())

    ts_row = _tile(s_len, PREF["row"], SUB)
    ts_conv = _tile(s_len, PREF["conv_rows"], SUB)
    tc_conv = _tile(cw, PREF["conv_cols"])
    tc_ffn = cwf_c
    ts_ffn = _tile(s_len, PREF["ffn_rows"], SUB)
    pair_order = [k // 2 + (k % 2) * (N_DEV // 2) for k in range(N_DEV)]
    pair_place = [pair_order.index(k) for k in range(N_DEV)]

    def paired(shards):
        return _from_col_shards(jnp.stack([shards[p] for p in pair_order]))

    def unpaired_chunks(g):
        ch = _chunks_cols(g)
        return jnp.stack([ch[p] for p in pair_place])
    ts_qkv = _tile(s_len, PREF["row"], SUB)
    t_attn = _tile(s_len, PREF["attn"])

    x2d, tgt = x[0], loss_target[0]

    vec = jnp.concatenate([c, conv_w_mix[0].reshape(1, -1), conv_w_ffn[0].reshape(1, -1)], axis=1)
    gath, cact, mod_rows = _ada_fwd(vec, w_ada[0], b_ada.reshape(N_DEV, ada_c))
    cwm_full = _from_col_shards(gath[:, d:d + 3 * cwm_c].reshape(N_DEV, 3, cwm_c))
    cwf_shards = gath[:, d + 3 * cwm_c:].reshape(N_DEV, 3, cwf_c)
    cwf_pair = paired(cwf_shards)
    cbf_pair = paired(jnp.transpose(conv_b_ffn.reshape(1, N_DEV, cwf_c), (1, 0, 2)))
    mod = mod_rows.reshape(1, N_DEV * ada_c)
    sh_m, sc_m, gt_m, sh_f, sc_f, gt_f = [mod[:, k * d:(k + 1) * d] for k in range(6)]

    h1, g_in = _modnorm_fwd(x2d, g_pre_mix, sc_m, sh_m, ts_row, [w_in[0].astype(BF16)])
    win = _from_col_shards(g_in)
    cut = np.cumsum([0, ql, kl, QK_ROPE, cw, cw, cw])
    part = [win[:, cut[k]:cut[k + 1]] for k in range(6)]

    def zcols(n):
        return jnp.zeros((d, n), BF16)

    win_p = jnp.concatenate([part[3], part[4], part[5], part[0], zcols(lay["kv"] - lay["ql"] - ql), part[1],
                             part[2], zcols(n_pad - lay["kr"] - QK_ROPE)], axis=1)
    inv_freq =1.0 / (ROPE_THETA ** (jnp.arange(0, QK_ROPE, 2, dtype=F32) / QK_ROPE))
    inv_row = jnp.tile(inv_freq, LANE // (QK_ROPE // 2)).reshape(1, LANE)
    ctab, atab, btab = _rope_tables(positions.astype(F32).reshape(s_len, 1), inv_row, _tile(s_len, 1024, SUB))

    proj, g_uq, g_ukv = _matmul(h1, win_p, out_dtype=BF16, tm=1024, tn=1280, tk=2048, name="mm_proj",
                                gather=[w_uq[0].astype(BF16), w_ukv[0].astype(BF16)])
    wuq_p = jnp.pad(_from_col_shards(g_uq).reshape(ql, n_heads, QK_NOPE + QK_ROPE),
                    ((0, 0), (0, 0), (0, HEAD_W - QK_NOPE - QK_ROPE))).reshape(ql, n_heads * HEAD_W)
    wukv = _from_col_shards(g_ukv).reshape(kl, n_heads, QK_NOPE + V_DIM)
    wk = wukv[:, :, :QK_NOPE].reshape(kl, n_heads * QK_NOPE)
    wv = wukv[:, :, QK_NOPE:].reshape(kl, n_heads * V_DIM)
    q, k, v, qn, kvn = _qkv_fwd(proj, lay, wuq_p, wk, wv, g_q, g_kv, ctab, atab, btab, n_heads, ts_qkv, scale * LOG2E)
    attn, lse_row, g_o, g_up = _flash_fwd(q, k, v, n_heads, t_attn, [w_o[0].astype(BF16), w_up[0].astype(BF16)], d)
    wo = g_o.reshape(d, d)

    def pair_shard(j):
        return j // 2 + (j % 2) * (N_DEV // 2)
    mixcat = _gconv_fwd(proj, lay, cwm_full, conv_b_mix, ts_conv, tc_conv, attn)
    mix = _matmul(mixcat, wo, out_dtype=BF16, tm=512, tn=2048, tk=2048, name="mm_mix")
    x1, h2 = _post_mix_fwd(mix, x2d, gt_m, g_post_mix, g_pre_ffn, sc_f, sh_f, ts_row)
    up, g_down = _matmul(h2, g_up, out_dtype=BF16, tm=1024, tn=cwf_c, tk=2048, name="mm_up", b_shard_of=pair_shard,
                         gather=[w_down[0].astype(BF16)])
    wdown = g_down.reshape(ff, d)
    u, act = _act_fwd(up, cwf_pair, cbf_pair, ts_ffn, tc_ffn)
    y = _matmul(act, wdown, out_dtype=BF16, tm=1024, tn=512, tk=ff, name="mm_down")

    dy, dx2, loss_row, d_gt_f, dg_post_ffn = _final_bwd(y, x1, tgt, gt_f, g_post_ffn, ts_row)
    gw_down = _matmul(act, dy, ta=True, out_dtype=BF16, tm=512, tn=512, tk=s_len, name="mm_gw_down")
    d_act = _matmul(dy, wdown, tb=True, out_dtype=BF16, tm=1024, tn=1408, tk=2048, name="mm_d_act")
    d_up, dcw_pair, dcb_pair = _ffn_act_bwd(d_act, u, up, cwf_pair, ts_ffn, tc_ffn)
    dcb_ffn = _from_col_shards(unpaired_chunks(dcb_pair))
    gw_up, p_down = _matmul(h2, d_up, ta=True, out_dtype=BF16, tm=512, tn=cwf_c, tk=4096, name="mm_gw_up",
                            exchange=[gw_down.reshape(N_DEV, ff // N_DEV, d)], out_shard_of=pair_shard)
    dh2 = _matmul_pair_shards(d_up, g_up, out_dtype=BF16, tm=1024, tn=1024, name="mm_dh2")
    dx1, dmix, d_sh_f, d_sc_f, dg_pre_ffn, d_gt_m, dg_post_mix = _mid_bwd(
        dh2, x1, dx2, mix, g_pre_ffn, sc_f, gt_m, g_post_mix, ts_row)
    gw_o = _matmul(mixcat, dmix, ta=True, out_dtype=BF16, tm=512, tn=512, tk=s_len, name="mm_gw_o")
    d_mixcat = _matmul(dmix, wo, tb=True, out_dtype=BF16, tm=1024, tn=1024, tk=2048, name="mm_d_mixcat")
    d_gb, d_gc, d_ci, dcw_mix, dcb_mix = _gconv_bwd(d_mixcat, proj, lay, cwm_full, conv_b_mix, ts_conv, tc_conv)
    delta = _delta(mixcat, d_mixcat, n_heads, _tile(s_len, 512, SUB))
    delta_row = delta[:, :, 0].reshape(n_heads, 1, s_len)
    dq, dk, dv, p_up, p_o = _flash_bwd(q, k, v, d_mixcat, lse_row, delta_row, n_heads, t_attn, scale,
                                       [gw_up, gw_o.reshape(N_DEV, d // N_DEV, d)])
    dq_r, dkn, d_tail, dg_q, dg_kv = _qkv_bwd(dq, dk, dv, proj, lay, wuq_p, wk, wv, g_q, g_kv, ctab, atab, btab,
                                              n_heads, ts_qkv)
    gw_uq_p = _matmul(qn, dq_r, ta=True, out_dtype=BF16, tm=768, tn=512, tk=s_len, name="mm_gw_uq")
    gw_k = _matmul(kvn, dkn, ta=True, out_dtype=BF16, tm=512, tn=512, tk=s_len, name="mm_gw_k")
    gw_v = _matmul(kvn, dv, ta=True, out_dtype=BF16, tm=512, tn=512, tk=s_len, name="mm_gw_v")
    d_proj = jnp.concatenate([d_gb, d_gc, d_ci, d_tail], axis=1)
    gw_uq = gw_uq_p.reshape(ql, n_heads, HEAD_W)[:, :, :QK_NOPE + QK_ROPE].reshape(ql, n_heads * (QK_NOPE + QK_ROPE))
    gw_ukv = jnp.concatenate([gw_k.reshape(kl, n_heads, QK_NOPE), gw_v.reshape(kl, n_heads, V_DIM)],
                             axis=2).reshape(kl, n_heads * (QK_NOPE + V_DIM))
    gw_in_p, p_uq, p_ukv, p_cwm, p_cwf = _matmul(
        h1, d_proj, ta=True, out_dtype=BF16, tm=512, tn=512, tk=s_len, name="mm_gw_in",
        exchange=[_chunks_cols(gw_uq), _chunks_cols(gw_ukv), _chunks_cols(dcw_mix), unpaired_chunks(dcw_pair)])

    gw_in = jnp.concatenate([gw_in_p[:, lay["ql"]:lay["ql"] + ql], gw_in_p[:, lay["kv"]:lay["kv"] + kl],
                             gw_in_p[:, lay["kr"]:lay["kr"] + QK_ROPE], gw_in_p[:, :3 * cw]], axis=1)
    dh1, p_in = _matmul(d_proj, win_p, tb=True, out_dtype=BF16, tm=512, tn=1024, tk=n_pad, name="mm_dh1",
                        exchange=[_chunks_cols(gw_in)])
    grad_x, d_sh_m, d_sc_m, dg_pre_mix = _first_bwd(dh1, x2d, dx1, g_pre_mix, sc_m, ts_row)

    dmod = jnp.concatenate([d_sh_m, d_sc_m, d_gt_m, d_sh_f, d_sc_f, d_gt_f], axis=1)
    small_g = [loss_row, dmod, dg_pre_mix, dg_post_mix, dg_q, dg_kv, dcb_mix, dg_pre_ffn, dg_post_ffn, dcb_ffn]
    v_all = _gather_small(jnp.concatenate(small_g, axis=1))

    small_names = ["b_ada", "g_pre_mix", "g_post_mix", "g_q", "g_kv", "conv_b_mix", "g_pre_ffn", "g_post_ffn",
                   "conv_b_ffn"]
    small_w = [b_ada, g_pre_mix, g_post_mix, g_q, g_kv, conv_b_mix, g_pre_ffn, g_post_ffn, conv_b_ffn]
    small_m = [m_b_ada, m_g_pre_mix, m_g_post_mix, m_g_q, m_g_kv, m_conv_b_mix, m_g_pre_ffn, m_g_post_ffn, m_conv_b_ffn]
    small_v = [v_b_ada, v_g_pre_mix, v_g_post_mix, v_g_q, v_g_kv, v_conv_b_mix, v_g_pre_ffn, v_g_post_ffn, v_conv_b_ffn]
    offs = np.cumsum([0] + [g.shape[1] for g in small_g])
    g_sum, small_out = _adam_small(v_all, [int(o) for o in offs[1:-1]], small_w, small_m, small_v)
    small = [dict(zip(small_names, kind)) for kind in small_out]
    loss = g_sum[0, 0]

    dmod_sh = lax.dynamic_slice(v_all, (0, int(offs[1]) + my_i * ada_c), (N_DEV, ada_c))
    cact_t = jnp.transpose(cact)
    big = dict(
        w_ada=_adam_ada(cact_t, dmod_sh, w_ada[0], m_w_ada[0], v_w_ada[0], _tile(d, 256, SUB)),
        w_in=_adam_parts(p_in, w_in[0], m_w_in[0], v_w_in[0], 256, "adam_w_in"),
        w_uq=_adam_parts(p_uq, w_uq[0], m_w_uq[0], v_w_uq[0], 256, "adam_w_uq"),
        w_ukv=_adam_parts(p_ukv, w_ukv[0], m_w_ukv[0], v_w_ukv[0], 256, "adam_w_ukv"),
        w_o=_adam_parts(p_o, w_o[0], m_w_o[0], v_w_o[0], 128, "adam_w_o"),
        w_up=_adam_parts(p_up, w_up[0], m_w_up[0], v_w_up[0], 256, "adam_w_up"),
        w_down=_adam_parts(p_down, w_down[0], m_w_down[0], v_w_down[0], 176, "adam_w_down"),
        conv_w_mix=_adam_parts(p_cwm, conv_w_mix[0], m_conv_w_mix[0], v_conv_w_mix[0], 8, "adam_cw_mix"),
        conv_w_ffn=_adam_parts(p_cwf, conv_w_ffn[0], m_conv_w_ffn[0], v_conv_w_ffn[0], 8, "adam_cw_ffn"),
    )

    names = ["w_ada", "b_ada", "g_pre_mix", "g_post_mix", "w_in", "g_q", "w_uq", "g_kv", "w_ukv", "conv_w_mix",
             "conv_b_mix", "w_o", "g_pre_ffn", "g_post_ffn", "w_up", "conv_w_ffn", "conv_b_ffn", "w_down"]
    outs = [loss, grad_x[None]]
    for kind in range(4):
        for nm in names:
            outs.append(big[nm][kind][None] if nm in big else small[kind][nm])
    return tuple(outs)
```

```python
import math

import numpy as np
import jax
import jax.numpy as jnp
from jax import lax
from jax.experimental import pallas as pl
from jax.experimental.pallas import tpu as pltpu

F32 = jnp.float32
BF16 = jnp.bfloat16
N_DEV = 8
MESH = pl.DeviceIdType.MESH

QK_NOPE = 128
QK_ROPE = 64
V_DIM = 128
HEAD_W = 256
LANE = 128
SUB = 8
HALO = 16
STRIP_ROWS = 64
RMS_EPS = 1e-6
ROPE_THETA = 10000.0
ADAM_LR = 0.001
ADAM_B1 = 0.9
ADAM_B2 = 0.999
ADAM_EPS = 1e-08
ADAM_WD = 0.01
ADAM_STEP = 10
NEG = -1e30
LOG2E = 1.4426950408889634
LN2 = 0.6931471805599453
VMEM_LIMIT = 56 * 1024 * 1024

PREF = {"row": 256, "conv_rows": 512, "conv_cols": 512, "ffn_rows": 512, "attn": 1024, "attn_chunk": 1024, "attn_heads": 2}

NT_DIMS = (((1,), (1,)), ((), ()))
TN_DIMS = (((0,), (0,)), ((), ()))


def _cparams(*sem):
    return pltpu.CompilerParams(dimension_semantics=sem if sem else None, vmem_limit_bytes=VMEM_LIMIT)


def _tile(n, pref, unit=LANE):
    if n <= pref:
        return n
    t = (pref // unit) * unit
    while t >= unit:
        if n % t == 0:
            return t
        t -= unit
    return n


def _roundup(n, m):
    return (n + m - 1) // m * m


def _rsq(x):
    return lax.rsqrt(jnp.mean(x * x, axis=-1, keepdims=True) + RMS_EPS)


def _colsum(x):
    return jnp.sum(x, axis=0, keepdims=True)


def _sigmoid(x):
    return 1.0 / (1.0 + jnp.exp(-x))


def _matmul(a, b, *, ta=False, tb=False, out_dtype, tm, tn, tk, name, exchange=None, gather=None, b_shard_of=None,
            out_shard_of=None):
    m_dim, k_dim = (a.shape[1], a.shape[0]) if ta else a.shape
    if b_shard_of is not None:
        n_dim, tn = b.shape[0] * b.shape[2], b.shape[2]
    else:
        n_dim = b.shape[0] if tb else b.shape[1]
    chunk_w = n_dim // N_DEV
    if out_shard_of is not None:
        tn = _tile(chunk_w, tn)
    tm, tn, tk = _tile(m_dim, tm), _tile(n_dim, tn), _tile(k_dim, tk)
    per_chunk = chunk_w // tn if out_shard_of is not None else 1
    gi, gj, nk = m_dim // tm, n_dim // tn, k_dim // tk
    dims = (((0 if ta else 1,), (1 if tb else 0,)), ((), ()))
    chunks = list(exchange or gather or [])
    nx = len(chunks)
    comm_start, comm_finish = (_gather_start, _gather_finish) if gather else (_exchange_start, _exchange_finish)

    def body(*refs):
        a_ref, b_ref = refs[:2]
        xin, o_ref, xout = refs[2:2 + nx], refs[2 + nx], refs[3 + nx:3 + 2 * nx]
        scratch = refs[3 + 2 * nx:]
        sems = scratch[1:] if nk > 1 else scratch
        i, j, k = pl.program_id(0), pl.program_id(1), pl.program_id(2)
        if nx:
            @pl.when((i == 0) & (j == 0) & (k == 0))
            def _():
                comm_start(xin, xout, *sems)

        if gather:
            @pl.when((i == gi // 2) & (j == 0) & (k == 0))
            def _():
                _gather_forward(xin, xout, *sems)

        part = lax.dot_general(a_ref[...], b_ref[...], dims, preferred_element_type=F32)
        if nk == 1:
            o_ref[...] = part.astype(o_ref.dtype)
        else:
            acc_ref = scratch[0]

            @pl.when(k == 0)
            def _():
                acc_ref[...] = part

            @pl.when(k > 0)
            def _():
                acc_ref[...] += part

            @pl.when(k == nk - 1)
            def _():
                o_ref[...] = acc_ref[...].astype(o_ref.dtype)

        if nx:
            @pl.when((i == gi - 1) & (j == gj - 1) & (k == nk - 1))
            def _():
                comm_finish(xin, xout, *sems)

    a_spec = pl.BlockSpec((tk, tm), lambda i, j, k: (k, i)) if ta else pl.BlockSpec((tm, tk), lambda i, j, k: (i, k))
    b_spec = pl.BlockSpec((tn, tk), lambda i, j, k: (j, k)) if tb else pl.BlockSpec((tk, tn), lambda i, j, k: (k, j))
    if b_shard_of is not None:
        b_spec = pl.BlockSpec((None, tk, tn), lambda i, j, k: (b_shard_of(j), k, 0))
    o_spec, o_shape = pl.BlockSpec((tm, tn), lambda i, j, k: (i, j)), (m_dim, n_dim)
    if out_shard_of is not None:
        o_spec = pl.BlockSpec((None, tm, tn), lambda i, j, k: (out_shard_of(j // per_chunk), i, j % per_chunk))
        o_shape = (N_DEV, m_dim, chunk_w)
    hbm = pl.BlockSpec(memory_space=pl.ANY)
    out = pl.pallas_call(
        body,
        name=name,
        grid=(gi, gj, nk),
        in_specs=[a_spec, b_spec] + [hbm] * nx,
        out_specs=[o_spec] + [hbm] * nx,
        out_shape=[jax.ShapeDtypeStruct(o_shape, out_dtype)]
        + (_gathered_shapes(chunks) if gather else [jax.ShapeDtypeStruct(c.shape, c.dtype) for c in chunks]),
        scratch_shapes=([pltpu.VMEM((tm, tn), F32)] if nk > 1 else []) + (_comm_scratch(nx) if nx else []),
        compiler_params=_cparams(*(("arbitrary",) * 3 if nx else ("parallel", "parallel", "arbitrary"))),
    )(a, b, *chunks)
    return out if nx else out[0]


def _matmul_pair_shards(a, shards, *, out_dtype, tm, tn, name):
    m_dim = a.shape[0]
    n_sh, n_dim, c = shards.shape
    half = n_sh // 2
    tm, tn = _tile(m_dim, tm), _tile(n_dim, tn)

    def body(a_ref, b0_ref, b1_ref, o_ref, acc_ref):
        k = pl.program_id(2)
        part = (lax.dot_general(a_ref[:, :c], b0_ref[...], NT_DIMS, preferred_element_type=F32)
                + lax.dot_general(a_ref[:, c:], b1_ref[...], NT_DIMS, preferred_element_type=F32))

        @pl.when(k == 0)
        def _():
            acc_ref[...] = part

        @pl.when(k > 0)
        def _():
            acc_ref[...] += part

        @pl.when(k == half - 1)
        def _():
            o_ref[...] = acc_ref[...].astype(o_ref.dtype)

    return pl.pallas_call(
        body, name=name, grid=(m_dim // tm, n_dim // tn, half),
        in_specs=[pl.BlockSpec((tm, 2 * c), lambda i, j, k: (i, k)),
                  pl.BlockSpec((None, tn, c), lambda i, j, k: (k, j, 0)),
                  pl.BlockSpec((None, tn, c), lambda i, j, k: (k + half, j, 0))],
        out_specs=pl.BlockSpec((tm, tn), lambda i, j, k: (i, j)),
        out_shape=jax.ShapeDtypeStruct((m_dim, n_dim), out_dtype),
        scratch_shapes=[pltpu.VMEM((tm, tn), F32)],
        compiler_params=_cparams("parallel", "parallel", "arbitrary"),
    )(a, shards, shards)


def _shift_down(x, halo, n):
    r = pltpu.roll(x, n, 0)
    hr = pltpu.roll(halo, n, 0)
    row = lax.broadcasted_iota(jnp.int32, halo.shape, 0)
    top = jnp.where(row < n, hr, r[:SUB])
    return jnp.concatenate([top, r[SUB:]], axis=0)


def _shift_up(x, halo, n):
    ts = x.shape[0]
    r = pltpu.roll(x, ts - n, 0)
    hr = pltpu.roll(halo, SUB - n, 0)
    row = lax.broadcasted_iota(jnp.int32, halo.shape, 0)
    bot = jnp.where(row >= SUB - n, hr, r[ts - SUB:])
    return jnp.concatenate([r[:ts - SUB], bot], axis=0)


def _conv3(x, halo, w_ref, b_ref):
    return _shift_down(x, halo, 2) * w_ref[0:1, :] + _shift_down(x, halo, 1) * w_ref[1:2, :] + x * w_ref[2:3, :] + b_ref[...]


def _prev_halo(ts, col):
    return lambda j, i: (jnp.maximum(i * (ts // SUB) - 1, 0), col(j))


def _next_halo(ts, n_rows, col):
    return lambda j, i: (jnp.minimum((i + 1) * (ts // SUB), n_rows // SUB - 1), col(j))


def _modnorm_fwd(x, g, sc, sh, ts, gather):
    s_len, d = x.shape
    n_i = s_len // ts
    ng = len(gather)

    def body(x_ref, g_ref, sc_ref, sh_ref, *rest):
        gin, h_ref, gout, sems = rest[:ng], rest[ng], rest[ng + 1:2 * ng + 1], rest[2 * ng + 1:]
        i = pl.program_id(0)

        @pl.when(i == 0)
        def _():
            _gather_start(gin, gout, *sems)

        xv = x_ref[...]
        h_ref[...] = ((xv * _rsq(xv) * g_ref[...]) * (1.0 + sc_ref[...]) + sh_ref[...]).astype(BF16)

        @pl.when(i == n_i - 1)
        def _():
            _gather_forward(gin, gout, *sems)
            _gather_finish(gin, gout, *sems)

    vec = pl.BlockSpec((1, d), lambda i: (0, 0))
    hbm = pl.BlockSpec(memory_space=pl.ANY)
    return pl.pallas_call(
        body, name="modnorm_fwd", grid=(n_i,),
        in_specs=[pl.BlockSpec((ts, d), lambda i: (i, 0)), vec, vec, vec] + [hbm] * ng,
        out_specs=[pl.BlockSpec((ts, d), lambda i: (i, 0))] + [hbm] * ng,
        out_shape=[jax.ShapeDtypeStruct((s_len, d), BF16)] + _gathered_shapes(gather),
        scratch_shapes=_comm_scratch(ng),
        compiler_params=_cparams("arbitrary"),
    )(x, g, sc, sh, *gather)


def _rope_tables(pos_col, inv_freq_row, ts):
    s_len = pos_col.shape[0]
    half = QK_ROPE // 2

    def body(p_ref, f_ref, c_ref, a_ref, b_ref):
        ang = p_ref[...] * f_ref[...]
        lane = lax.broadcasted_iota(jnp.int32, ang.shape, 1)
        cos, sin = jnp.cos(ang), jnp.sin(ang)
        c_ref[...] = jnp.where(lane < 2 * half, cos, 0.0)
        a_ref[...] = jnp.where(lane < half, -sin, 0.0)
        b_ref[...] = jnp.where((lane >= half) & (lane < 2 * half), sin, 0.0)

    out = jax.ShapeDtypeStruct((s_len, LANE), F32)
    blk = pl.BlockSpec((ts, LANE), lambda i: (i, 0))
    return pl.pallas_call(
        body, name="rope_tables", grid=(s_len // ts,),
        in_specs=[pl.BlockSpec((ts, 1), lambda i: (i, 0)), pl.BlockSpec((1, LANE), lambda i: (0, 0))],
        out_specs=[blk, blk, blk], out_shape=[out, out, out],
        compiler_params=_cparams("parallel"),
    )(pos_col, inv_freq_row)


def _rope(seg, c, a, b):
    return seg * c + pltpu.roll(seg, LANE - QK_ROPE // 2, 1) * a + pltpu.roll(seg, QK_ROPE // 2, 1) * b


def _rope_t(seg, c, a, b):
    return seg * c - pltpu.roll(seg, LANE - QK_ROPE // 2, 1) * a - pltpu.roll(seg, QK_ROPE // 2, 1) * b


def _qkv_fwd(proj, lay, wuq, wk, wv, g_q, g_kv, ctab, atab, btab, n_heads, ts, scale):
    s_len = proj.shape[0]
    ql_w, kl_w = wuq.shape[0], wk.shape[0]

    def body(ql_ref, kl_ref, kr_ref, wuq_ref, wk_ref, wv_ref, gq_ref, gkv_ref, c_ref, a_ref, b_ref,
             q_out, k_out, v_out, qn_out, kvn_out):
        c, a, b = c_ref[...], a_ref[...], b_ref[...]
        ql = ql_ref[...].astype(F32)
        qn = (ql * _rsq(ql) * gq_ref[...]).astype(BF16)
        qn_out[...] = qn
        q = jnp.dot(qn, wuq_ref[...], preferred_element_type=F32)
        kl = kl_ref[...].astype(F32)
        kvn = (kl * _rsq(kl) * gkv_ref[...]).astype(BF16)
        kvn_out[...] = kvn
        kn = jnp.dot(kvn, wk_ref[...], preferred_element_type=F32)
        v_out[...] = jnp.dot(kvn, wv_ref[...], preferred_element_type=F32).astype(BF16)
        kr = _rope(kr_ref[...].astype(F32), c, a, b).astype(BF16)
        for h in range(n_heads):
            o = h * HEAD_W
            q_out[:, o:o + QK_NOPE] = (q[:, o:o + QK_NOPE] * scale).astype(BF16)
            q_out[:, o + QK_NOPE:o + HEAD_W] = (_rope(q[:, o + QK_NOPE:o + HEAD_W], c, a, b) * scale).astype(BF16)
            k_out[:, o:o + QK_NOPE] = kn[:, h * QK_NOPE:(h + 1) * QK_NOPE].astype(BF16)
            k_out[:, o + QK_NOPE:o + HEAD_W] = kr

    def full(arr):
        return pl.BlockSpec(arr.shape, lambda i: (0, 0))

    tab = pl.BlockSpec((ts, LANE), lambda i: (i, 0))
    hw, hv = n_heads * HEAD_W, n_heads * V_DIM
    return pl.pallas_call(
        body, name="qkv_fwd", grid=(s_len // ts,),
        in_specs=[pl.BlockSpec((ts, ql_w), lambda i: (i, lay["ql"] // ql_w)),
                  pl.BlockSpec((ts, kl_w), lambda i: (i, lay["kv"] // kl_w)),
                  pl.BlockSpec((ts, LANE), lambda i: (i, lay["kr"] // LANE)),
                  full(wuq), full(wk), full(wv), full(g_q), full(g_kv), tab, tab, tab],
        out_specs=[pl.BlockSpec((ts, hw), lambda i: (i, 0)), pl.BlockSpec((ts, hw), lambda i: (i, 0)),
                   pl.BlockSpec((ts, hv), lambda i: (i, 0)), pl.BlockSpec((ts, ql_w), lambda i: (i, 0)),
                   pl.BlockSpec((ts, kl_w), lambda i: (i, 0))],
        out_shape=[jax.ShapeDtypeStruct((s_len, hw), BF16), jax.ShapeDtypeStruct((s_len, hw), BF16),
                   jax.ShapeDtypeStruct((s_len, hv), BF16), jax.ShapeDtypeStruct((s_len, ql_w), BF16),
                   jax.ShapeDtypeStruct((s_len, kl_w), BF16)],
        compiler_params=_cparams("parallel"),
    )(proj, proj, proj, wuq, wk, wv, g_q, g_kv, ctab, atab, btab)


def _flash_fwd(q, k, v, n_heads, t, gather, out_cols):
    ng = len(gather)
    ck = _tile(t, PREF["attn_chunk"])
    hp = PREF["attn_heads"] if n_heads % PREF["attn_heads"] == 0 else 1
    n_groups = n_heads // hp
    s_len = q.shape[0]
    nb = s_len // t
    pairs = [(i, j) for i in range(nb) for j in range(i + 1)]
    itab = jnp.asarray(np.array([p[0] for p in pairs], np.int32))
    jtab = jnp.asarray(np.array([p[1] for p in pairs], np.int32))

    n_steps = len(pairs)

    def body(it_ref, jt_ref, q_ref, k_ref, v_ref, *rest):
        gin, (o_ref, lse_ref), gout = rest[:ng], rest[ng:ng + 2], rest[ng + 2:2 * ng + 2]
        m_sc, l_sc, acc_sc = rest[2 * ng + 2:2 * ng + 5]
        sems = rest[2 * ng + 5:]
        group, step_id = pl.program_id(0), pl.program_id(1)
        i, j = it_ref[step_id], jt_ref[step_id]

        @pl.when((group == 0) & (step_id == 0))
        def _():
            _gather_start(gin, gout, *sems)

        @pl.when((group == (3 * n_groups) // 4) & (step_id == 0))
        def _():
            _gather_forward(gin, gout, *sems)

        @pl.when(j == 0)
        def _():
            m_sc[...] = jnp.full(m_sc.shape, NEG, F32)
            l_sc[...] = jnp.zeros(l_sc.shape, F32)
            acc_sc[...] = jnp.zeros(acc_sc.shape, F32)

        def step(diag):
            for h in range(hp):
                qk, vc = slice(h * HEAD_W, (h + 1) * HEAD_W), slice(h * V_DIM, (h + 1) * V_DIM)
                cd = min(ck, t // 2) if diag else ck
                for c in range(t // cd):
                    q0 = c * cd if diag else 0
                    qs, ks = slice(q0, t), slice(c * cd, (c + 1) * cd)
                    s_t = lax.dot_general(k_ref[ks, qk], q_ref[qs, qk], NT_DIMS, preferred_element_type=F32)
                    if diag:
                        krow = lax.broadcasted_iota(jnp.int32, s_t.shape, 0)
                        qcol = lax.broadcasted_iota(jnp.int32, s_t.shape, 1)
                        s_t = jnp.where(krow <= qcol, s_t, NEG)
                    m_prev = m_sc[h, :, qs]
                    m_new = jnp.maximum(m_prev, jnp.max(s_t, axis=0, keepdims=True))
                    alpha = jnp.exp2(m_prev - m_new)
                    p_t = jnp.exp2(s_t - m_new)
                    l_sc[h, :, qs] = alpha * l_sc[h, :, qs] + jnp.sum(p_t, axis=0, keepdims=True)
                    acc_sc[h, :, qs] = acc_sc[h, :, qs] * alpha + lax.dot_general(
                        v_ref[ks, vc], p_t.astype(BF16), TN_DIMS, preferred_element_type=F32)
                    m_sc[h, :, qs] = m_new

        @pl.when(j < i)
        def _():
            step(False)

        @pl.when(j == i)
        def _():
            step(True)
            for h in range(hp):
                l = l_sc[h]
                o_ref[:, h * V_DIM:(h + 1) * V_DIM] = jnp.transpose(acc_sc[h] / l).astype(BF16)
                lse_ref[h] = m_sc[h] + jnp.log(l) * LOG2E

        @pl.when((group == n_groups - 1) & (step_id == n_steps - 1))
        def _():
            _gather_finish(gin, gout, *sems)

    hbm = pl.BlockSpec(memory_space=pl.ANY)
    grid_spec = pltpu.PrefetchScalarGridSpec(
        num_scalar_prefetch=2, grid=(n_groups, n_steps),
        in_specs=[pl.BlockSpec((t, hp * HEAD_W), lambda g, s, it, jt: (it[s], g)),
                  pl.BlockSpec((t, hp * HEAD_W), lambda g, s, it, jt: (jt[s], g)),
                  pl.BlockSpec((t, hp * V_DIM), lambda g, s, it, jt: (jt[s], g))] + [hbm] * ng,
        out_specs=[pl.BlockSpec((t, hp * V_DIM), lambda g, s, it, jt: (it[s], g)),
                   pl.BlockSpec((hp, 1, t), lambda g, s, it, jt: (g, 0, it[s]))] + [hbm] * ng,
        scratch_shapes=[pltpu.VMEM((hp, 1, t), F32), pltpu.VMEM((hp, 1, t), F32), pltpu.VMEM((hp, V_DIM, t), F32)]
        + _comm_scratch(ng),
    )
    return pl.pallas_call(
        body, name="flash_fwd", grid_spec=grid_spec,
        out_shape=[jax.ShapeDtypeStruct((s_len, out_cols), BF16),
                   jax.ShapeDtypeStruct((n_heads, 1, s_len), F32)] + _gathered_shapes(gather),
        compiler_params=_cparams("arbitrary", "arbitrary"),
    )(itab, jtab, q, k, v, *gather)


def _gconv_fwd(proj, lay, w, b, ts, tc, mixcat):
    s_len = proj.shape[0]
    cw = w.shape[1]
    nj = cw // tc
    out_off = mixcat.shape[1] - cw

    def body(gb_ref, gc_ref, ci_ref, gch_ref, cih_ref, w_ref, b_ref, mix_in, o_ref):
        i = pl.program_id(1)
        p = gc_ref[...].astype(F32) * ci_ref[...].astype(F32)
        ph = jnp.where(i > 0, gch_ref[...].astype(F32) * cih_ref[...].astype(F32), 0.0)
        o_ref[...] = (gb_ref[...].astype(F32) * _conv3(p, ph, w_ref, b_ref)).astype(BF16)

    def blk(off):
        return pl.BlockSpec((ts, tc), lambda j, i: (i, off // tc + j))

    def halo(off):
        return pl.BlockSpec((SUB, tc), _prev_halo(ts, lambda j: off // tc + j))

    return pl.pallas_call(
        body, name="gconv_fwd", grid=(nj, s_len // ts),
        in_specs=[blk(lay["gb"]), blk(lay["gc"]), blk(lay["ci"]), halo(lay["gc"]), halo(lay["ci"]),
                  pl.BlockSpec((3, tc), lambda j, i: (0, j)), pl.BlockSpec((1, tc), lambda j, i: (0, j)),
                  pl.BlockSpec(memory_space=pl.ANY)],
        out_specs=pl.BlockSpec((ts, tc), lambda j, i: (i, out_off // tc + j)),
        out_shape=jax.ShapeDtypeStruct(mixcat.shape, BF16),
        input_output_aliases={7: 0},
        compiler_params=_cparams("parallel", "parallel"),
    )(proj, proj, proj, proj, proj, w, b, mixcat)


def _post_mix_fwd(mix, x, gt, g_post, g_pre, sc, sh, ts):
    s_len, d = x.shape

    def body(mix_ref, x_ref, gt_ref, gp_ref, g2_ref, sc_ref, sh_ref, x1_ref, h2_ref):
        mv = mix_ref[...].astype(F32)
        x1 = x_ref[...] + gt_ref[...] * (mv * _rsq(mv) * gp_ref[...])
        x1_ref[...] = x1
        h2_ref[...] = ((x1 * _rsq(x1) * g2_ref[...]) * (1.0 + sc_ref[...]) + sh_ref[...]).astype(BF16)

    vec = pl.BlockSpec((1, d), lambda i: (0, 0))
    row = pl.BlockSpec((ts, d), lambda i: (i, 0))
    return pl.pallas_call(
        body, name="post_mix_fwd", grid=(s_len // ts,),
        in_specs=[row, row, vec, vec, vec, vec, vec], out_specs=[row, row],
        out_shape=[jax.ShapeDtypeStruct((s_len, d), F32), jax.ShapeDtypeStruct((s_len, d), BF16)],
        compiler_params=_cparams("parallel"),
    )(mix, x, gt, g_post, g_pre, sc, sh)


def _act_fwd(up, w, b, ts, tc):
    s_len, f2 = up.shape
    nh = f2 // 2 // tc
    rs_n = _tile(ts, STRIP_ROWS, HALO)

    def body(up_ref, uph_ref, w_ref, b_ref, u_ref, o_ref):
        i = pl.program_id(1)

        def conv_strip(r0, lanes):
            if r0 == 0:
                top = jnp.where(i > 0, uph_ref[:, lanes].astype(F32), 0.0)
                xe = jnp.concatenate([top, up_ref[0:rs_n, lanes].astype(F32)], axis=0)
            else:
                xe = up_ref[r0 - HALO:r0 + rs_n, lanes].astype(F32)
            u = (pltpu.roll(xe, 2, 0)[HALO:] * w_ref[0:1, lanes] + pltpu.roll(xe, 1, 0)[HALO:] * w_ref[1:2, lanes]
                 + xe[HALO:] * w_ref[2:3, lanes] + b_ref[:, lanes])
            u_ref[r0:r0 + rs_n, lanes] = u.astype(BF16)
            return u

        for r0 in range(0, ts, rs_n):
            for c0 in range(0, tc, LANE):
                ua = conv_strip(r0, slice(c0, c0 + LANE))
                ug = conv_strip(r0, slice(tc + c0, tc + c0 + LANE))
                o_ref[r0:r0 + rs_n, c0:c0 + LANE] = (ug * _sigmoid(ug) * ua).astype(BF16)

    pair = pl.BlockSpec((ts, 2 * tc), lambda j, i: (i, j))
    return pl.pallas_call(
        body, name="act_fwd", grid=(nh, s_len // ts),
        in_specs=[pair, pl.BlockSpec((HALO, 2 * tc), lambda j, i: (jnp.maximum(i * (ts // HALO) - 1, 0), j)),
                  pl.BlockSpec((3, 2 * tc), lambda j, i: (0, j)), pl.BlockSpec((1, 2 * tc), lambda j, i: (0, j))],
        out_specs=[pair, pl.BlockSpec((ts, tc), lambda j, i: (i, j))],
        out_shape=[jax.ShapeDtypeStruct((s_len, f2), BF16), jax.ShapeDtypeStruct((s_len, f2 // 2), BF16)],
        compiler_params=_cparams("parallel", "parallel"),
    )(up, up, w, b)


def _final_bwd(y, x1, tgt, gt, g_post, ts):
    s_len, d = y.shape

    n_i = s_len // ts

    def body(y_ref, x1_ref, t_ref, gt_ref, g_ref, dy_ref, dx2_ref, loss_ref, dgt_ref, dg_ref):
        i = pl.program_id(0)
        gtg = gt_ref[...] * g_ref[...]
        yv = y_ref[...].astype(F32)
        r = _rsq(yv)
        yh = yv * r
        e = x1_ref[...] + yh * gtg - t_ref[...]
        loss = 0.5 * jnp.sum(jnp.mean(e * e, axis=-1, keepdims=True), axis=0, keepdims=True)
        dx2 = e * (1.0 / d)
        dx2_ref[...] = dx2
        dyh = dx2 * gtg
        dy_ref[...] = (r * (dyh - yh * jnp.mean(dyh * yh, axis=-1, keepdims=True))).astype(BF16)

        @pl.when(i == 0)
        def _():
            loss_ref[...] = jnp.zeros(loss_ref.shape, F32)
            dgt_ref[...] = jnp.zeros(dgt_ref.shape, F32)

        loss_ref[...] += jnp.broadcast_to(loss, loss_ref.shape)
        dgt_ref[...] += _colsum(dx2 * yh)

        @pl.when(i == n_i - 1)
        def _():
            both = dgt_ref[...]
            dg_ref[...] = both * gt_ref[...]
            dgt_ref[...] = both * g_ref[...]

    vec = pl.BlockSpec((1, d), lambda i: (0, 0))
    row = pl.BlockSpec((ts, d), lambda i: (i, 0))
    vshape = jax.ShapeDtypeStruct((1, d), F32)
    return pl.pallas_call(
        body, name="final_bwd", grid=(s_len // ts,),
        in_specs=[row, row, row, vec, vec],
        out_specs=[row, row, pl.BlockSpec((1, LANE), lambda i: (0, 0)), vec, vec],
        out_shape=[jax.ShapeDtypeStruct((s_len, d), BF16), jax.ShapeDtypeStruct((s_len, d), F32),
                   jax.ShapeDtypeStruct((1, LANE), F32), vshape, vshape],
        compiler_params=_cparams("arbitrary"),
    )(y, x1, tgt, gt, g_post)


def _ffn_act_bwd(d_act, u, up, w, ts, tc):
    s_len, f2 = up.shape
    nh = f2 // 2 // tc
    n_i = s_len // ts
    rs_n = _tile(ts, STRIP_ROWS, HALO)
    ext = rs_n + HALO

    def body(d_ref, dh_ref, u_ref, uh_ref, x_ref, w_ref, dx_ref, dw_ref, db_ref, acc):
        i = pl.program_id(1)
        acc[...] = jnp.zeros(acc.shape, F32)

        def below(ref, halo_ref, r0, lanes):
            if r0 + ext <= ts:
                return ref[r0:r0 + ext, lanes].astype(F32)
            bot = jnp.where(i < n_i - 1, halo_ref[:, lanes].astype(F32), 0.0)
            return jnp.concatenate([ref[r0:ts, lanes].astype(F32), bot], axis=0)

        def fold8(v):
            return jnp.sum(v.reshape(v.shape[0] // SUB, SUB, v.shape[1]), axis=0)

        def conv_bwd_strip(du, r0, lanes):
            du1, du2 = pltpu.roll(du, ext - 1, 0)[:rs_n], pltpu.roll(du, ext - 2, 0)[:rs_n]
            du0 = du[:rs_n]
            dx_ref[r0:r0 + rs_n, lanes] = (du0 * w_ref[2:3, lanes] + du1 * w_ref[1:2, lanes]
                                           + du2 * w_ref[0:1, lanes]).astype(BF16)
            xv = x_ref[r0:r0 + rs_n, lanes].astype(F32)
            acc[0, :, lanes] += fold8(du2 * xv)
            acc[1, :, lanes] += fold8(du1 * xv)
            acc[2, :, lanes] += fold8(du0 * xv)
            acc[3, :, lanes] += fold8(du0)

        for r0 in range(0, ts, rs_n):
            for c0 in range(0, tc, LANE):
                la, lg = slice(c0, c0 + LANE), slice(tc + c0, tc + c0 + LANE)
                dv = below(d_ref, dh_ref, r0, la)
                ua, ug = below(u_ref, uh_ref, r0, la), below(u_ref, uh_ref, r0, lg)
                sg = _sigmoid(ug)
                conv_bwd_strip(dv * (ug * sg), r0, la)
                conv_bwd_strip(dv * ua * (sg * (1.0 + ug * (1.0 - sg))), r0, lg)

        @pl.when(i == 0)
        def _():
            dw_ref[...] = jnp.zeros(dw_ref.shape, F32)
            db_ref[...] = jnp.zeros(db_ref.shape, F32)

        dw_ref[...] += jnp.concatenate([_colsum(acc[k]) for k in range(3)], axis=0)
        db_ref[...] += _colsum(acc[3])

    pair = pl.BlockSpec((ts, 2 * tc), lambda j, i: (i, j))

    def nxt(j, i):
        return (jnp.minimum((i + 1) * (ts // HALO), s_len // HALO - 1), j)

    return pl.pallas_call(
        body, name="ffn_act_bwd", grid=(nh, n_i),
        in_specs=[pl.BlockSpec((ts, tc), lambda j, i: (i, j)), pl.BlockSpec((HALO, tc), nxt),
                  pair, pl.BlockSpec((HALO, 2 * tc), nxt), pair, pl.BlockSpec((3, 2 * tc), lambda j, i: (0, j))],
        out_specs=[pair, pl.BlockSpec((3, 2 * tc), lambda j, i: (0, j)), pl.BlockSpec((1, 2 * tc), lambda j, i: (0, j))],
        out_shape=[jax.ShapeDtypeStruct((s_len, f2), BF16), jax.ShapeDtypeStruct((3, f2), F32),
                   jax.ShapeDtypeStruct((1, f2), F32)],
        scratch_shapes=[pltpu.VMEM((4, SUB, 2 * tc), F32)],
        compiler_params=_cparams("parallel", "arbitrary"),
    )(d_act, d_act, u, u, up, w)


def _mid_bwd(dh2, x1, dx2, mix, g_pre, sc, gt_m, g_post, ts):
    s_len, d = x1.shape
    n_i = s_len // ts

    def body(dh_ref, x1_ref, dx2_ref, mix_ref, g_ref, sc_ref, gt_ref, gp_ref,
             dx1_ref, dmix_ref, dsh_ref, dsc_ref, dg_ref, dgt_ref, dgp_ref):
        i = pl.program_id(0)
        wv = (1.0 + sc_ref[...]) * g_ref[...]
        gtg = gt_ref[...] * gp_ref[...]
        dh = dh_ref[...].astype(F32)
        x1 = x1_ref[...]
        r1 = _rsq(x1)
        xh = x1 * r1
        dhx = dh * xh
        dx1 = dx2_ref[...] + r1 * (dh * wv - xh * jnp.mean(dhx * wv, axis=-1, keepdims=True))
        dx1_ref[...] = dx1
        mv = mix_ref[...].astype(F32)
        rm = _rsq(mv)
        mh = mv * rm
        dxm = dx1 * mh
        dmix_ref[...] = (rm * (dx1 * gtg - mh * jnp.mean(dxm * gtg, axis=-1, keepdims=True))).astype(BF16)

        @pl.when(i == 0)
        def _():
            for ref in (dsh_ref, dsc_ref, dgt_ref):
                ref[...] = jnp.zeros(ref.shape, F32)

        dsh_ref[...] += _colsum(dh)
        dsc_ref[...] += _colsum(dhx)
        dgt_ref[...] += _colsum(dxm)

        @pl.when(i == n_i - 1)
        def _():
            t1, t2 = dsc_ref[...], dgt_ref[...]
            dsc_ref[...] = t1 * g_ref[...]
            dg_ref[...] = t1 * (1.0 + sc_ref[...])
            dgt_ref[...] = t2 * gp_ref[...]
            dgp_ref[...] = t2 * gt_ref[...]

    vec = pl.BlockSpec((1, d), lambda i: (0, 0))
    row = pl.BlockSpec((ts, d), lambda i: (i, 0))
    vshape = jax.ShapeDtypeStruct((1, d), F32)
    return pl.pallas_call(
        body, name="mid_bwd", grid=(s_len // ts,),
        in_specs=[row, row, row, row, vec, vec, vec, vec],
        out_specs=[row, row, vec, vec, vec, vec, vec],
        out_shape=[jax.ShapeDtypeStruct((s_len, d), F32), jax.ShapeDtypeStruct((s_len, d), BF16)] + [vshape] * 5,
        compiler_params=_cparams("arbitrary"),
    )(dh2, x1, dx2, mix, g_pre, sc, gt_m, g_post)


def _first_bwd(dh1, x, dx1, g_pre, sc, ts):
    s_len, d = x.shape
    n_i = s_len // ts

    def body(dh_ref, x_ref, dx1_ref, g_ref, sc_ref, dx_ref, dsh_ref, dsc_ref, dg_ref):
        i = pl.program_id(0)
        wv = (1.0 + sc_ref[...]) * g_ref[...]
        dh = dh_ref[...].astype(F32)
        xv = x_ref[...]
        r = _rsq(xv)
        xh = xv * r
        dhx = dh * xh
        dx_ref[...] = dx1_ref[...] + r * (dh * wv - xh * jnp.mean(dhx * wv, axis=-1, keepdims=True))

        @pl.when(i == 0)
        def _():
            for ref in (dsh_ref, dsc_ref):
                ref[...] = jnp.zeros(ref.shape, F32)

        dsh_ref[...] += _colsum(dh)
        dsc_ref[...] += _colsum(dhx)

        @pl.when(i == n_i - 1)
        def _():
            t1 = dsc_ref[...]
            dsc_ref[...] = t1 * g_ref[...]
            dg_ref[...] = t1 * (1.0 + sc_ref[...])

    vec = pl.BlockSpec((1, d), lambda i: (0, 0))
    row = pl.BlockSpec((ts, d), lambda i: (i, 0))
    vshape = jax.ShapeDtypeStruct((1, d), F32)
    return pl.pallas_call(
        body, name="first_bwd", grid=(s_len // ts,),
        in_specs=[row, row, row, vec, vec], out_specs=[row, vec, vec, vec],
        out_shape=[jax.ShapeDtypeStruct((s_len, d), F32), vshape, vshape, vshape],
        compiler_params=_cparams("arbitrary"),
    )(dh1, x, dx1, g_pre, sc)


def _gconv_bwd(d_mixcat, proj, lay, w, b, ts, tc):
    s_len = proj.shape[0]
    cw = w.shape[1]
    n_i = s_len // ts
    dc_off = d_mixcat.shape[1] - cw

    def body(dc_ref, dch_ref, gb_ref, gbh_ref, gc_ref, gch_ref, ci_ref, cih_ref, w_ref, b_ref,
             dgb_ref, dgc_ref, dci_ref, dw_ref, db_ref):
        i = pl.program_id(1)
        gc, ci = gc_ref[...].astype(F32), ci_ref[...].astype(F32)
        p = gc * ci
        ph = jnp.where(i > 0, gch_ref[...].astype(F32) * cih_ref[...].astype(F32), 0.0)
        pm1, pm2 = _shift_down(p, ph, 1), _shift_down(p, ph, 2)
        z = pm2 * w_ref[0:1, :] + pm1 * w_ref[1:2, :] + p * w_ref[2:3, :] + b_ref[...]
        dc = dc_ref[...].astype(F32)
        dgb_ref[...] = (dc * z).astype(BF16)
        dz = dc * gb_ref[...].astype(F32)
        dzh = jnp.where(i < n_i - 1, dch_ref[...].astype(F32) * gbh_ref[...].astype(F32), 0.0)
        dz1, dz2 = _shift_up(dz, dzh, 1), _shift_up(dz, dzh, 2)
        dp = dz * w_ref[2:3, :] + dz1 * w_ref[1:2, :] + dz2 * w_ref[0:1, :]
        dgc_ref[...] = (dp * ci).astype(BF16)
        dci_ref[...] = (dp * gc).astype(BF16)

        @pl.when(i == 0)
        def _():
            dw_ref[...] = jnp.zeros(dw_ref.shape, F32)
            db_ref[...] = jnp.zeros(db_ref.shape, F32)

        dw_ref[0:1, :] += _colsum(dz2 * p)
        dw_ref[1:2, :] += _colsum(dz1 * p)
        dw_ref[2:3, :] += _colsum(dz * p)
        db_ref[...] += _colsum(dz)

    def blk(off):
        return pl.BlockSpec((ts, tc), lambda j, i: (i, off // tc + j))

    def prev(off):
        return pl.BlockSpec((SUB, tc), _prev_halo(ts, lambda j: off // tc + j))

    def nxt(off):
        return pl.BlockSpec((SUB, tc), _next_halo(ts, s_len, lambda j: off // tc + j))

    out_blk = pl.BlockSpec((ts, tc), lambda j, i: (i, j))
    act = jax.ShapeDtypeStruct((s_len, cw), BF16)
    return pl.pallas_call(
        body, name="gconv_bwd", grid=(cw // tc, n_i),
        in_specs=[blk(dc_off), nxt(dc_off), blk(lay["gb"]), nxt(lay["gb"]), blk(lay["gc"]), prev(lay["gc"]),
                  blk(lay["ci"]), prev(lay["ci"]),
                  pl.BlockSpec((3, tc), lambda j, i: (0, j)), pl.BlockSpec((1, tc), lambda j, i: (0, j))],
        out_specs=[out_blk, out_blk, out_blk,
                   pl.BlockSpec((3, tc), lambda j, i: (0, j)), pl.BlockSpec((1, tc), lambda j, i: (0, j))],
        out_shape=[act, act, act, jax.ShapeDtypeStruct((3, cw), F32), jax.ShapeDtypeStruct((1, cw), F32)],
        compiler_params=_cparams("parallel", "arbitrary"),
    )(d_mixcat, d_mixcat, proj, proj, proj, proj, proj, proj, w, b)


def _delta(o, d_mixcat, n_heads, ts):
    s_len = o.shape[0]

    def body(o_ref, do_ref, out_ref):
        for h in range(n_heads):
            sl = slice(h * V_DIM, (h + 1) * V_DIM)
            prod = o_ref[:, sl].astype(F32) * do_ref[:, sl].astype(F32)
            out_ref[h] = jnp.broadcast_to(jnp.sum(prod, axis=1, keepdims=True), (ts, LANE))

    hv = n_heads * V_DIM
    return pl.pallas_call(
        body, name="attn_delta", grid=(s_len // ts,),
        in_specs=[pl.BlockSpec((ts, hv), lambda i: (i, 0)), pl.BlockSpec((ts, hv), lambda i: (i, 0))],
        out_specs=pl.BlockSpec((n_heads, ts, LANE), lambda i: (0, i, 0)),
        out_shape=jax.ShapeDtypeStruct((n_heads, s_len, LANE), F32),
        compiler_params=_cparams("parallel"),
    )(o, d_mixcat)


def _flash_bwd(q, k, v, d_mixcat, lse_row, delta_row, n_heads, t, scale, exchange):
    nx = len(exchange)
    s_len = q.shape[0]
    nb = s_len // t
    pairs = [(j, i) for j in range(nb) for i in range(j, nb)]
    jtab = jnp.asarray(np.array([p[0] for p in pairs], np.int32))
    itab = jnp.asarray(np.array([p[1] for p in pairs], np.int32))
    n_steps = len(pairs)

    def body(jt_ref, it_ref, q_ref, k_ref, v_ref, do_ref, lse_ref, dl_ref, *rest):
        xin, (dq_ref, dk_ref, dv_ref), xout = rest[:nx], rest[nx:nx + 3], rest[nx + 3:2 * nx + 3]
        dq_acc, dk_acc, dv_acc = rest[2 * nx + 3:2 * nx + 6]
        sems = rest[2 * nx + 6:]
        head, step_id = pl.program_id(0), pl.program_id(1)
        j, i = jt_ref[step_id], it_ref[step_id]

        @pl.when((head == 0) & (step_id == 0))
        def _():
            _exchange_start(xin, xout, *sems)

        @pl.when(step_id == 0)
        def _():
            dq_acc[...] = jnp.zeros(dq_acc.shape, F32)

        @pl.when(i == j)
        def _():
            dk_acc[...] = jnp.zeros(dk_acc.shape, F32)
            dv_acc[...] = jnp.zeros(dv_acc.shape, F32)

        def step(diag):
            half = t // 2
            lo, hi = slice(0, half), slice(half, t)
            blocks = [(lo, lo, True), (lo, hi, False), (hi, hi, True)] if diag else [(slice(0, t), slice(0, t), False)]
            for ks, qs, masked in blocks:
                qv, kv, vv, dov = q_ref[qs, :], k_ref[ks, :], v_ref[ks, :], do_ref[qs, :]
                s_t = lax.dot_general(kv, qv, NT_DIMS, preferred_element_type=F32)
                if masked:
                    krow = lax.broadcasted_iota(jnp.int32, s_t.shape, 0)
                    qcol = lax.broadcasted_iota(jnp.int32, s_t.shape, 1)
                    s_t = jnp.where(krow <= qcol, s_t, NEG)
                p_t = jnp.exp2(s_t - lse_ref[0, :, qs])
                dv_acc[ks, :] += jnp.dot(p_t.astype(BF16), dov, preferred_element_type=F32)
                dp_t = lax.dot_general(vv, dov, NT_DIMS, preferred_element_type=F32)
                ds_t = (p_t * (dp_t - dl_ref[0, :, qs])).astype(BF16)
                dk_acc[ks, :] += jnp.dot(ds_t, qv, preferred_element_type=F32)
                n_q = qs.stop - qs.start
                rows = pl.ds(pl.multiple_of(i * t + qs.start, n_q), n_q)
                dq_acc[rows, :] += lax.dot_general(ds_t, kv, TN_DIMS, preferred_element_type=F32)

        @pl.when(i > j)
        def _():
            step(False)

        @pl.when(i == j)
        def _():
            step(True)

        @pl.when(i == nb - 1)
        def _():
            dk_ref[...] = (dk_acc[...] * LN2).astype(BF16)
            dv_ref[...] = dv_acc[...].astype(BF16)

        @pl.when(step_id == n_steps - 1)
        def _():
            dq_ref[...] = (dq_acc[...] * scale).astype(BF16)

        @pl.when((head == n_heads - 1) & (step_id == n_steps - 1))
        def _():
            _exchange_finish(xin, xout, *sems)

    hbm = pl.BlockSpec(memory_space=pl.ANY)
    hv = n_heads * V_DIM
    do_off = 0
    grid_spec = pltpu.PrefetchScalarGridSpec(
        num_scalar_prefetch=2, grid=(n_heads, n_steps),
        in_specs=[pl.BlockSpec((t, HEAD_W), lambda h, s, jt, it: (it[s], h)),
                  pl.BlockSpec((t, HEAD_W), lambda h, s, jt, it: (jt[s], h)),
                  pl.BlockSpec((t, V_DIM), lambda h, s, jt, it: (jt[s], h)),
                  pl.BlockSpec((t, V_DIM), lambda h, s, jt, it: (it[s], do_off + h)),
                  pl.BlockSpec((1, 1, t), lambda h, s, jt, it: (h, 0, it[s])),
                  pl.BlockSpec((1, 1, t), lambda h, s, jt, it: (h, 0, it[s]))] + [hbm] * nx,
        out_specs=[pl.BlockSpec((s_len, HEAD_W), lambda h, s, jt, it: (0, h)),
                   pl.BlockSpec((t, HEAD_W), lambda h, s, jt, it: (jt[s], h)),
                   pl.BlockSpec((t, V_DIM), lambda h, s, jt, it: (jt[s], h))] + [hbm] * nx,
        scratch_shapes=[pltpu.VMEM((s_len, HEAD_W), F32), pltpu.VMEM((t, HEAD_W), F32), pltpu.VMEM((t, V_DIM), F32)]
        + _comm_scratch(nx),
    )
    return pl.pallas_call(
        body, name="flash_bwd", grid_spec=grid_spec,
        out_shape=[jax.ShapeDtypeStruct((s_len, n_heads * HEAD_W), BF16),
                   jax.ShapeDtypeStruct((s_len, n_heads * HEAD_W), BF16),
                   jax.ShapeDtypeStruct((s_len, hv), BF16)] + [jax.ShapeDtypeStruct(c.shape, c.dtype) for c in exchange],
        compiler_params=_cparams("arbitrary", "arbitrary"),
    )(jtab, itab, q, k, v, d_mixcat, lse_row, delta_row, *exchange)


def _qkv_bwd(dq, dk, dv, proj, lay, wuq, wk, wv, g_q, g_kv, ctab, atab, btab, n_heads, ts):
    s_len = proj.shape[0]
    ql_w, kl_w = wuq.shape[0], wk.shape[0]
    tail_w = lay["np"] - lay["ql"]
    kv_o, kr_o = lay["kv"] - lay["ql"], lay["kr"] - lay["ql"]

    def body(dq_ref, dk_ref, dv_ref, ql_ref, kl_ref, wuq_ref, wk_ref, wv_ref, gq_ref, gkv_ref, c_ref, a_ref, b_ref,
             dqr_ref, dkn_ref, tail_ref, dgq_ref, dgkv_ref):
        i = pl.program_id(0)
        c, a, b = c_ref[...], a_ref[...], b_ref[...]
        dkr = jnp.zeros((ts, LANE), F32)
        for h in range(n_heads):
            o = h * HEAD_W
            dqr_ref[:, o:o + QK_NOPE] = dq_ref[:, o:o + QK_NOPE]
            dqr_ref[:, o + QK_NOPE:o + HEAD_W] = _rope_t(dq_ref[:, o + QK_NOPE:o + HEAD_W].astype(F32), c, a, b).astype(BF16)
            dkn_ref[:, h * QK_NOPE:(h + 1) * QK_NOPE] = dk_ref[:, o:o + QK_NOPE]
            dkr = dkr + dk_ref[:, o + QK_NOPE:o + HEAD_W].astype(F32)
        tail_ref[...] = jnp.zeros(tail_ref.shape, BF16)
        tail_ref[:, kr_o:kr_o + LANE] = _rope_t(dkr, c, a, b).astype(BF16)

        def rms_bwd(lat_ref, dn, g_ref):
            lat = lat_ref[...].astype(F32)
            r = _rsq(lat)
            xh = lat * r
            dxh = dn * g_ref[...]
            return r * (dxh - xh * jnp.mean(dxh * xh, axis=-1, keepdims=True)), _colsum(dn * xh)

        dqn = lax.dot_general(dqr_ref[...], wuq_ref[...], NT_DIMS, preferred_element_type=F32)
        d_ql, dgq = rms_bwd(ql_ref, dqn, gq_ref)
        tail_ref[:, 0:ql_w] = d_ql.astype(BF16)
        dkvn = (lax.dot_general(dkn_ref[...], wk_ref[...], NT_DIMS, preferred_element_type=F32)
                + lax.dot_general(dv_ref[...], wv_ref[...], NT_DIMS, preferred_element_type=F32))
        d_kl, dgkv = rms_bwd(kl_ref, dkvn, gkv_ref)
        tail_ref[:, kv_o:kv_o + kl_w] = d_kl.astype(BF16)

        @pl.when(i == 0)
        def _():
            dgq_ref[...] = jnp.zeros(dgq_ref.shape, F32)
            dgkv_ref[...] = jnp.zeros(dgkv_ref.shape, F32)

        dgq_ref[...] += dgq
        dgkv_ref[...] += dgkv

    def full(arr):
        return pl.BlockSpec(arr.shape, lambda i: (0, 0))

    def rows(w):
        return pl.BlockSpec((ts, w), lambda i: (i, 0))

    tab = pl.BlockSpec((ts, LANE), lambda i: (i, 0))
    hw, hv, hn = n_heads * HEAD_W, n_heads * V_DIM, n_heads * QK_NOPE
    return pl.pallas_call(
        body, name="qkv_bwd", grid=(s_len // ts,),
        in_specs=[rows(hw), rows(hw), rows(hv),
                  pl.BlockSpec((ts, ql_w), lambda i: (i, lay["ql"] // ql_w)),
                  pl.BlockSpec((ts, kl_w), lambda i: (i, lay["kv"] // kl_w)),
                  full(wuq), full(wk), full(wv), full(g_q), full(g_kv), tab, tab, tab],
        out_specs=[rows(hw), rows(hn), rows(tail_w), full(g_q), full(g_kv)],
        out_shape=[jax.ShapeDtypeStruct((s_len, hw), BF16), jax.ShapeDtypeStruct((s_len, hn), BF16),
                   jax.ShapeDtypeStruct((s_len, tail_w), BF16),
                   jax.ShapeDtypeStruct(g_q.shape, F32), jax.ShapeDtypeStruct(g_kv.shape, F32)],
        compiler_params=_cparams("arbitrary"),
    )(dq, dk, dv, proj, proj, wuq, wk, wv, g_q, g_kv, ctab, atab, btab)


def _adamw(w, g, m, v):
    m = ADAM_B1 * m + (1.0 - ADAM_B1) * g
    v = ADAM_B2 * v + (1.0 - ADAM_B2) * (g * g)
    m_hat = m / (1.0 - ADAM_B1 ** ADAM_STEP)
    v_hat = v / (1.0 - ADAM_B2 ** ADAM_STEP)
    delta = -ADAM_LR * (m_hat / (jnp.sqrt(v_hat) + ADAM_EPS) + ADAM_WD * w)
    return delta, m, v


def _adam_parts(parts, w, m, v, tr, name, small=None):
    r, c = w.shape
    tr = _tile(r, tr, SUB)
    n_i = r // tr
    ns = 0 if small is None else 1

    def body(p_ref, w_ref, m_ref, v_ref, *rest):
        vec_ref = rest[:ns]
        g_out, d_out, m_out, v_out = rest[ns:ns + 4]
        gath_ref, sems = rest[ns + 4:2 * ns + 4], rest[2 * ns + 4:]
        i = pl.program_id(0)

        def small_copies():
            send_s, recv_s, local_s = sems
            me = _my_place()
            my_i = _index(me)

            def copy(k, to, src_row):
                row = gath_ref[0].at[pl.ds(src_row, 1), :]
                return pltpu.make_async_remote_copy(src_ref=row, dst_ref=row, send_sem=send_s.at[k], recv_sem=recv_s.at[k],
                                                    device_id=to, device_id_type=MESH)

            own = pltpu.make_async_copy(vec_ref[0], gath_ref[0].at[pl.ds(my_i, 1), :], local_s)
            sends = [copy(k, _peer(me, k), my_i) for k in range(1, N_DEV)]
            recvs = [copy(k, me, _index(_peer(me, k))) for k in range(1, N_DEV)]
            return own, sends, recvs

        if ns:
            @pl.when(i == 0)
            def _():
                own, sends, _ = small_copies()
                own.start()
                own.wait()
                for cp in sends:
                    cp.start()

        g = p_ref[0].astype(F32)
        for dev in range(1, N_DEV):
            g = g + p_ref[dev].astype(F32)
        g_out[...] = g
        d_out[...], m_out[...], v_out[...] = _adamw(w_ref[...], g, m_ref[...], v_ref[...])

        if ns:
            @pl.when(i == n_i - 1)
            def _():
                _, sends, recvs = small_copies()
                for cp in recvs:
                    cp.wait_recv()
                for cp in sends:
                    cp.wait_send()

    blk = pl.BlockSpec((tr, c), lambda i: (i, 0))
    hbm = pl.BlockSpec(memory_space=pl.ANY)
    shp = jax.ShapeDtypeStruct((r, c), F32)
    out = pl.pallas_call(
        body, name=name, grid=(n_i,),
        in_specs=[pl.BlockSpec((N_DEV, tr, c), lambda i: (0, i, 0)), blk, blk, blk] + [hbm] * ns,
        out_specs=[blk, blk, blk, blk] + [hbm] * ns,
        out_shape=[shp, shp, shp, shp] + ([jax.ShapeDtypeStruct((N_DEV, small.shape[1]), F32)] if ns else []),
        scratch_shapes=([pltpu.SemaphoreType.DMA((N_DEV,))] * 2 + [pltpu.SemaphoreType.DMA]) if ns else [],
        compiler_params=_cparams("arbitrary" if ns else "parallel"),
    )(parts, w, m, v, *([small] if ns else []))
    return (out[:4], out[4]) if ns else out


def _adam_ada(cact_t, dmod_sh, w, m, v, tr):
    r, c = w.shape

    def body(ct_ref, dm_ref, w_ref, m_ref, v_ref, g_out, d_out, m_out, v_out):
        g = jnp.dot(ct_ref[...], dm_ref[...], preferred_element_type=F32, precision=lax.Precision.HIGHEST)
        g_out[...] = g
        d_out[...], m_out[...], v_out[...] = _adamw(w_ref[...], g, m_ref[...], v_ref[...])

    blk = pl.BlockSpec((tr, c), lambda i: (i, 0))
    shp = jax.ShapeDtypeStruct((r, c), F32)
    return pl.pallas_call(
        body, name="adam_ada", grid=(r // tr,),
        in_specs=[pl.BlockSpec((tr, N_DEV), lambda i: (i, 0)), pl.BlockSpec((N_DEV, c), lambda i: (0, 0)), blk, blk, blk],
        out_specs=[blk, blk, blk, blk], out_shape=[shp, shp, shp, shp],
        compiler_params=_cparams("parallel"),
    )(cact_t, dmod_sh, w, m, v)


def _adam_small(v_all, offs, ws, ms, vs):
    n_par = len(ws)

    def body(p_ref, *refs):
        w_refs, m_refs, v_refs = refs[:n_par], refs[n_par:2 * n_par], refs[2 * n_par:3 * n_par]
        sum_ref = refs[3 * n_par]
        outs = refs[3 * n_par + 1:]
        g = p_ref[0:1, :]
        for dev in range(1, N_DEV):
            g = g + p_ref[dev:dev + 1, :]
        sum_ref[...] = g
        for p in range(n_par):
            n = w_refs[p].shape[1]
            gp = sum_ref[:, offs[p]:offs[p] + n]
            outs[p][...] = gp
            (outs[n_par + p][...], outs[2 * n_par + p][...], outs[3 * n_par + p][...]) = _adamw(
                w_refs[p][...], gp, m_refs[p][...], v_refs[p][...])

    vm = pl.BlockSpec(memory_space=pltpu.VMEM)
    shapes = [jax.ShapeDtypeStruct(w.shape, F32) for w in ws]
    out = pl.pallas_call(
        body, name="adam_small", in_specs=[vm] * (1 + 3 * n_par), out_specs=[vm] * (1 + 4 * n_par),
        out_shape=[jax.ShapeDtypeStruct((1, v_all.shape[1]), F32)] + shapes * 4, compiler_params=_cparams(),
    )(v_all, *ws, *ms, *vs)
    return out[0], [out[1 + k * n_par:1 + (k + 1) * n_par] for k in range(4)]


def _my_place():
    return lax.axis_index("x"), lax.axis_index("y"), lax.axis_index("c")


def _peer(place, k):
    x, y, c = place
    return (x ^ (k >> 2), y ^ ((k >> 1) & 1), c ^ (k & 1))


def _index(place):
    return 4 * place[0] + 2 * place[1] + place[2]


def _ada_fwd(vec, w_ada, b_ada_rows):
    lv = vec.shape[1]
    d, c = w_ada.shape

    def body(vec_ref, w_ref, b_ref, gath_ref, cact_ref, mod_ref, modsh, send_a, recv_a, send_b, recv_b, local_s):
        me = _my_place()
        my_i = _index(me)

        def gather_copy(k, to, src_row):
            row = gath_ref.at[pl.ds(src_row, 1), :]
            return pltpu.make_async_remote_copy(src_ref=row, dst_ref=row, send_sem=send_a.at[k], recv_sem=recv_a.at[k],
                                                device_id=to, device_id_type=MESH)

        own = pltpu.make_async_copy(vec_ref, gath_ref.at[pl.ds(my_i, 1), :], local_s.at[0])
        own.start()
        own.wait()
        sends = [gather_copy(k, _peer(me, k), my_i) for k in range(1, N_DEV)]
        for cp in sends:
            cp.start()
        for k in range(1, N_DEV):
            gather_copy(k, me, _index(_peer(me, k))).wait_recv()
        for cp in sends:
            cp.wait_send()

        c_all = gath_ref[:, 0:d]
        cact = c_all * _sigmoid(c_all)
        cact_ref[...] = cact
        modsh[...] = jnp.dot(cact, w_ref[...], preferred_element_type=F32, precision=lax.Precision.HIGHEST)

        def mod_copy(k, to, src_row, dst_row):
            return pltpu.make_async_remote_copy(src_ref=modsh.at[pl.ds(src_row, 1), :], dst_ref=mod_ref.at[pl.ds(dst_row, 1), :],
                                                send_sem=send_b.at[k], recv_sem=recv_b.at[k],
                                                device_id=to, device_id_type=MESH)

        own = pltpu.make_async_copy(modsh.at[pl.ds(my_i, 1), :], mod_ref.at[pl.ds(my_i, 1), :], local_s.at[1])
        own.start()
        sends = [mod_copy(k, _peer(me, k), _index(_peer(me, k)), my_i) for k in range(1, N_DEV)]
        for cp in sends:
            cp.start()
        for k in range(1, N_DEV):
            mod_copy(k, me, my_i, _index(_peer(me, k))).wait_recv()
        for cp in sends:
            cp.wait_send()
        own.wait()
        mod_ref[...] = mod_ref[...] + b_ref[...]

    vm = pl.BlockSpec(memory_space=pltpu.VMEM)
    return pl.pallas_call(
        body, name="ada_fwd", in_specs=[vm, vm, vm], out_specs=[vm, vm, vm],
        out_shape=[jax.ShapeDtypeStruct((N_DEV, lv), F32), jax.ShapeDtypeStruct((N_DEV, d), F32),
                   jax.ShapeDtypeStruct((N_DEV, c), F32)],
        scratch_shapes=[pltpu.VMEM((N_DEV, c), F32)] + [pltpu.SemaphoreType.DMA((N_DEV,))] * 4
        + [pltpu.SemaphoreType.DMA((2,))],
        compiler_params=pltpu.CompilerParams(vmem_limit_bytes=VMEM_LIMIT),
    )(vec, w_ada, b_ada_rows)


PER = N_DEV - 1


def _comm_scratch(n):
    return [pltpu.SemaphoreType.DMA((n * PER,)), pltpu.SemaphoreType.DMA((n * PER,)), pltpu.SemaphoreType.DMA((n,))]


def _gather_copies(ins, outs, send_s, recv_s, local_s):
    n = len(ins)
    me = _my_place()
    x, y, c = me
    sibling = (x, y, 1 - c)
    chips = [(1 - x, y), (x, 1 - y), (1 - x, 1 - y)]

    def copy(a, k, block, to, src=None):
        slot = outs[a].at[_index(block)]
        return pltpu.make_async_remote_copy(src_ref=slot if src is None else src, dst_ref=slot,
                                            send_sem=send_s.at[a * PER + k], recv_sem=recv_s.at[a * PER + k],
                                            device_id=to, device_id_type=MESH)

    mine = [pltpu.make_async_copy(ins[a], outs[a].at[_index(me)], local_s.at[a]) for a in range(n)]
    first = []
    for a in range(n):
        first.append(copy(a, 0, me, sibling, src=ins[a]))
        first += [copy(a, 1 + j, me, (*chip, c), src=ins[a]) for j, chip in enumerate(chips)]
    landed = [copy(a, 1 + j, (*chip, c), me) for j, chip in enumerate(chips) for a in range(n)]
    passed = [copy(a, 4 + j, (*chip, c), sibling) for j, chip in enumerate(chips) for a in range(n)]
    from_sibling = [copy(a, 0, sibling, me) for a in range(n)]
    from_sibling += [copy(a, 4 + j, (*chip, 1 - c), me) for a in range(n) for j, chip in enumerate(chips)]
    return mine, first, landed, passed, from_sibling


def _gather_start(*refs):
    mine, first, _, _, _ = _gather_copies(*refs)
    for cp in mine + first:
        cp.start()


def _gather_forward(*refs):
    _, _, landed, passed, _ = _gather_copies(*refs)
    for got, fwd in zip(landed, passed):
        got.wait_recv()
        fwd.start()


def _gather_finish(*refs):
    mine, first, _, passed, from_sibling = _gather_copies(*refs)
    for cp in from_sibling:
        cp.wait_recv()
    for cp in first + passed:
        cp.wait_send()
    for cp in mine:
        cp.wait()


def _exchange_copies(ins, outs, send_s, recv_s, local_s):
    n = len(ins)
    me = _my_place()
    my_i = _index(me)

    def copy(a, k, to, src_slot, dst_slot):
        return pltpu.make_async_remote_copy(src_ref=ins[a].at[src_slot], dst_ref=outs[a].at[dst_slot],
                                            send_sem=send_s.at[a * PER + k - 1], recv_sem=recv_s.at[a * PER + k - 1],
                                            device_id=to, device_id_type=MESH)

    mine = [pltpu.make_async_copy(ins[a].at[my_i], outs[a].at[my_i], local_s.at[a]) for a in range(n)]
    sends = [copy(a, k, _peer(me, k), _index(_peer(me, k)), my_i) for k in range(1, N_DEV) for a in range(n)]
    recvs = [copy(a, k, me, my_i, _index(_peer(me, k))) for k in range(1, N_DEV) for a in range(n)]
    return mine, sends, recvs


def _exchange_start(*refs):
    mine, sends, _ = _exchange_copies(*refs)
    for cp in mine + sends:
        cp.start()


def _exchange_finish(*refs):
    mine, sends, recvs = _exchange_copies(*refs)
    for cp in recvs:
        cp.wait_recv()
    for cp in sends:
        cp.wait_send()
    for cp in mine:
        cp.wait()


def _gathered_shapes(shards):
    return [jax.ShapeDtypeStruct((N_DEV,) + s.shape, s.dtype) for s in shards]


def _proj_layout(cw, ql, kl):
    lay = {"gb": 0, "gc": cw, "ci": 2 * cw, "ql": 3 * cw}
    assert lay["ql"] % ql == 0
    lay["kv"] = _roundup(lay["ql"] + ql, kl)
    lay["kr"] = lay["kv"] + kl
    lay["np"] = _roundup(lay["kr"] + LANE, 4 * LANE)
    return lay


def _chunks_cols(g):
    r, c8 = g.shape
    return jnp.transpose(g.reshape(r, N_DEV, c8 // N_DEV), (1, 0, 2))


def _from_col_shards(a):
    n, r, c = a.shape
    return jnp.transpose(a, (1, 0, 2)).reshape(r, n * c)


def kernel(x, c, positions, w_ada, b_ada, g_pre_mix, g_post_mix, w_in, g_q, w_uq, g_kv, w_ukv, conv_w_mix, conv_b_mix, w_o, g_pre_ffn, g_post_ffn, w_up, conv_w_ffn, conv_b_ffn, w_down, loss_target, m_w_ada, m_b_ada, m_g_pre_mix, m_g_post_mix, m_w_in, m_g_q, m_w_uq, m_g_kv, m_w_ukv, m_conv_w_mix, m_conv_b_mix, m_w_o, m_g_pre_ffn, m_g_post_ffn, m_w_up, m_conv_w_ffn, m_conv_b_ffn, m_w_down, v_w_ada, v_b_ada, v_g_pre_mix, v_g_post_mix, v_w_in, v_g_q, v_w_uq, v_g_kv, v_w_ukv, v_conv_w_mix, v_conv_b_mix, v_w_o, v_g_pre_ffn, v_g_post_ffn, v_w_up, v_conv_w_ffn, v_conv_b_ffn, v_w_down):
    s_len, d = x.shape[1], x.shape[2]
    ql, kl = w_uq.shape[1], w_ukv.shape[1]
    n_heads = w_ukv.shape[2] * N_DEV // (QK_NOPE + V_DIM)
    cw = conv_w_mix.shape[2] * N_DEV
    f2 = w_up.shape[2] * N_DEV
    ff = f2 // 2
    in_cols = w_in.shape[2] * N_DEV
    ada_c = w_ada.shape[2]
    cwm_c, cwf_c = conv_w_mix.shape[2], conv_w_ffn.shape[2]
    scale = 1.0 / math.sqrt(QK_NOPE + QK_ROPE)
    lay = _proj_layout(cw, ql, kl)
    n_pad = lay["np"]
    my_i = _index(_my_place())

    ts_row = _tile(s_len, PREF["row"], SUB)
    ts_conv = _tile(s_len, PREF["conv_rows"], SUB)
    tc_conv = _tile(cw, PREF["conv_cols"])
    tc_ffn = cwf_c
    ts_ffn = _tile(s_len, PREF["ffn_rows"], SUB)
    pair_order = [k // 2 + (k % 2) * (N_DEV // 2) for k in range(N_DEV)]
    pair_place = [pair_order.index(k) for k in range(N_DEV)]

    def paired(shards):
        return _from_col_shards(jnp.stack([shards[p] for p in pair_order]))

    def unpaired_chunks(g):
        ch = _chunks_cols(g)
        return jnp.stack([ch[p] for p in pair_place])
    ts_qkv = _tile(s_len, PREF["row"], SUB)
    t_attn = _tile(s_len, PREF["attn"])

    x2d, tgt = x[0], loss_target[0]

    vec = jnp.concatenate([c, conv_w_mix[0].reshape(1, -1), conv_w_ffn[0].reshape(1, -1)], axis=1)
    gath, cact, mod_rows = _ada_fwd(vec, w_ada[0], b_ada.reshape(N_DEV, ada_c))
    cwm_full = _from_col_shards(gath[:, d:d + 3 * cwm_c].reshape(N_DEV, 3, cwm_c))
    cwf_shards = gath[:, d + 3 * cwm_c:].reshape(N_DEV, 3, cwf_c)
    cwf_pair = paired(cwf_shards)
    cbf_pair = paired(jnp.transpose(conv_b_ffn.reshape(1, N_DEV, cwf_c), (1, 0, 2)))
    mod = mod_rows.reshape(1, N_DEV * ada_c)
    sh_m, sc_m, gt_m, sh_f, sc_f, gt_f = [mod[:, k * d:(k + 1) * d] for k in range(6)]

    h1, g_in = _modnorm_fwd(x2d, g_pre_mix, sc_m, sh_m, ts_row, [w_in[0].astype(BF16)])
    win = _from_col_shards(g_in)
    cut = np.cumsum([0, ql, kl, QK_ROPE, cw, cw, cw])
    part = [win[:, cut[k]:cut[k + 1]] for k in range(6)]

    def zcols(n):
        return jnp.zeros((d, n), BF16)

    win_p = jnp.concatenate([part[3], part[4], part[5], part[0], zcols(lay["kv"] - lay["ql"] - ql), part[1],
                             part[2], zcols(n_pad - lay["kr"] - QK_ROPE)], axis=1)
    inv_freq =1.0 / (ROPE_THETA ** (jnp.arange(0, QK_ROPE, 2, dtype=F32) / QK_ROPE))
    inv_row = jnp.tile(inv_freq, LANE // (QK_ROPE // 2)).reshape(1, LANE)
    ctab, atab, btab = _rope_tables(positions.astype(F32).reshape(s_len, 1), inv_row, _tile(s_len, 1024, SUB))

    proj, g_uq, g_ukv = _matmul(h1, win_p, out_dtype=BF16, tm=1024, tn=1280, tk=2048, name="mm_proj",
                                gather=[w_uq[0].astype(BF16), w_ukv[0].astype(BF16)])
    wuq_p = jnp.pad(_from_col_shards(g_uq).reshape(ql, n_heads, QK_NOPE + QK_ROPE),
                    ((0, 0), (0, 0), (0, HEAD_W - QK_NOPE - QK_ROPE))).reshape(ql, n_heads * HEAD_W)
    wukv = _from_col_shards(g_ukv).reshape(kl, n_heads, QK_NOPE + V_DIM)
    wk = wukv[:, :, :QK_NOPE].reshape(kl, n_heads * QK_NOPE)
    wv = wukv[:, :, QK_NOPE:].reshape(kl, n_heads * V_DIM)
    q, k, v, qn, kvn = _qkv_fwd(proj, lay, wuq_p, wk, wv, g_q, g_kv, ctab, atab, btab, n_heads, ts_qkv, scale * LOG2E)
    attn, lse_row, g_o, g_up = _flash_fwd(q, k, v, n_heads, t_attn, [w_o[0].astype(BF16), w_up[0].astype(BF16)], d)
    wo = g_o.reshape(d, d)

    def pair_shard(j):
        return j // 2 + (j % 2) * (N_DEV // 2)
    mixcat = _gconv_fwd(proj, lay, cwm_full, conv_b_mix, ts_conv, tc_conv, attn)
    mix = _matmul(mixcat, wo, out_dtype=BF16, tm=512, tn=2048, tk=2048, name="mm_mix")
    x1, h2 = _post_mix_fwd(mix, x2d, gt_m, g_post_mix, g_pre_ffn, sc_f, sh_f, ts_row)
    up, g_down = _matmul(h2, g_up, out_dtype=BF16, tm=1024, tn=cwf_c, tk=2048, name="mm_up", b_shard_of=pair_shard,
                         gather=[w_down[0].astype(BF16)])
    wdown = g_down.reshape(ff, d)
    u, act = _act_fwd(up, cwf_pair, cbf_pair, ts_ffn, tc_ffn)
    y = _matmul(act, wdown, out_dtype=BF16, tm=1024, tn=512, tk=ff, name="mm_down")

    dy, dx2, loss_row, d_gt_f, dg_post_ffn = _final_bwd(y, x1, tgt, gt_f, g_post_ffn, ts_row)
    gw_down = _matmul(act, dy, ta=True, out_dtype=BF16, tm=512, tn=512, tk=s_len, name="mm_gw_down")
    d_act = _matmul(dy, wdown, tb=True, out_dtype=BF16, tm=1024, tn=1408, tk=2048, name="mm_d_act")
    d_up, dcw_pair, dcb_pair = _ffn_act_bwd(d_act, u, up, cwf_pair, ts_ffn, tc_ffn)
    dcb_ffn = _from_col_shards(unpaired_chunks(dcb_pair))
    gw_up, p_down = _matmul(h2, d_up, ta=True, out_dtype=BF16, tm=512, tn=cwf_c, tk=4096, name="mm_gw_up",
                            exchange=[gw_down.reshape(N_DEV, ff // N_DEV, d)], out_shard_of=pair_shard)
    dh2 = _matmul_pair_shards(d_up, g_up, out_dtype=BF16, tm=1024, tn=1024, name="mm_dh2")
    dx1, dmix, d_sh_f, d_sc_f, dg_pre_ffn, d_gt_m, dg_post_mix = _mid_bwd(
        dh2, x1, dx2, mix, g_pre_ffn, sc_f, gt_m, g_post_mix, ts_row)
    gw_o = _matmul(mixcat, dmix, ta=True, out_dtype=BF16, tm=512, tn=512, tk=s_len, name="mm_gw_o")
    d_mixcat = _matmul(dmix, wo, tb=True, out_dtype=BF16, tm=1024, tn=1024, tk=2048, name="mm_d_mixcat")
    d_gb, d_gc, d_ci, dcw_mix, dcb_mix = _gconv_bwd(d_mixcat, proj, lay, cwm_full, conv_b_mix, ts_conv, tc_conv)
    delta = _delta(mixcat, d_mixcat, n_heads, _tile(s_len, 512, SUB))
    delta_row = delta[:, :, 0].reshape(n_heads, 1, s_len)
    dq, dk, dv, p_up, p_o = _flash_bwd(q, k, v, d_mixcat, lse_row, delta_row, n_heads, t_attn, scale,
                                       [gw_up, gw_o.reshape(N_DEV, d // N_DEV, d)])
    dq_r, dkn, d_tail, dg_q, dg_kv = _qkv_bwd(dq, dk, dv, proj, lay, wuq_p, wk, wv, g_q, g_kv, ctab, atab, btab,
                                              n_heads, ts_qkv)
    gw_uq_p = _matmul(qn, dq_r, ta=True, out_dtype=BF16, tm=768, tn=512, tk=s_len, name="mm_gw_uq")
    gw_k = _matmul(kvn, dkn, ta=True, out_dtype=BF16, tm=512, tn=512, tk=s_len, name="mm_gw_k")
    gw_v = _matmul(kvn, dv, ta=True, out_dtype=BF16, tm=512, tn=512, tk=s_len, name="mm_gw_v")
    d_proj = jnp.concatenate([d_gb, d_gc, d_ci, d_tail], axis=1)
    gw_uq = gw_uq_p.reshape(ql, n_heads, HEAD_W)[:, :, :QK_NOPE + QK_ROPE].reshape(ql, n_heads * (QK_NOPE + QK_ROPE))
    gw_ukv = jnp.concatenate([gw_k.reshape(kl, n_heads, QK_NOPE), gw_v.reshape(kl, n_heads, V_DIM)],
                             axis=2).reshape(kl, n_heads * (QK_NOPE + V_DIM))
    gw_in_p, p_uq, p_ukv, p_cwm, p_cwf = _matmul(
        h1, d_proj, ta=True, out_dtype=BF16, tm=512, tn=512, tk=s_len, name="mm_gw_in",
        exchange=[_chunks_cols(gw_uq), _chunks_cols(gw_ukv), _chunks_cols(dcw_mix), unpaired_chunks(dcw_pair)])

    gw_in = jnp.concatenate([gw_in_p[:, lay["ql"]:lay["ql"] + ql], gw_in_p[:, lay["kv"]:lay["kv"] + kl],
                             gw_in_p[:, lay["kr"]:lay["kr"] + QK_ROPE], gw_in_p[:, :3 * cw]], axis=1)
    dh1, p_in = _matmul(d_proj, win_p, tb=True, out_dtype=BF16, tm=512, tn=1024, tk=n_pad, name="mm_dh1",
                        exchange=[_chunks_cols(gw_in)])
    grad_x, d_sh_m, d_sc_m, dg_pre_mix = _first_bwd(dh1, x2d, dx1, g_pre_mix, sc_m, ts_row)

    dmod = jnp.concatenate([d_sh_m, d_sc_m, d_gt_m, d_sh_f, d_sc_f, d_gt_f], axis=1)
    small_g = [loss_row, dmod, dg_pre_mix, dg_post_mix, dg_q, dg_kv, dcb_mix, dg_pre_ffn, dg_post_ffn, dcb_ffn]
    adam_w_up, v_all = _adam_parts(p_up, w_up[0], m_w_up[0], v_w_up[0], 256, "adam_w_up",
                                   small=jnp.concatenate(small_g, axis=1))

    small_names = ["b_ada", "g_pre_mix", "g_post_mix", "g_q", "g_kv", "conv_b_mix", "g_pre_ffn", "g_post_ffn",
                   "conv_b_ffn"]
    small_w = [b_ada, g_pre_mix, g_post_mix, g_q, g_kv, conv_b_mix, g_pre_ffn, g_post_ffn, conv_b_ffn]
    small_m = [m_b_ada, m_g_pre_mix, m_g_post_mix, m_g_q, m_g_kv, m_conv_b_mix, m_g_pre_ffn, m_g_post_ffn, m_conv_b_ffn]
    small_v = [v_b_ada, v_g_pre_mix, v_g_post_mix, v_g_q, v_g_kv, v_conv_b_mix, v_g_pre_ffn, v_g_post_ffn, v_conv_b_ffn]
    offs = np.cumsum([0] + [g.shape[1] for g in small_g])
    g_sum, small_out = _adam_small(v_all, [int(o) for o in offs[1:-1]], small_w, small_m, small_v)
    small = [dict(zip(small_names, kind)) for kind in small_out]
    loss = g_sum[0, 0]

    dmod_sh = lax.dynamic_slice(v_all, (0, int(offs[1]) + my_i * ada_c), (N_DEV, ada_c))
    cact_t = jnp.transpose(cact)
    big = dict(
        w_ada=_adam_ada(cact_t, dmod_sh, w_ada[0], m_w_ada[0], v_w_ada[0], _tile(d, 256, SUB)),
        w_in=_adam_parts(p_in, w_in[0], m_w_in[0], v_w_in[0], 256, "adam_w_in"),
        w_uq=_adam_parts(p_uq, w_uq[0], m_w_uq[0], v_w_uq[0], 256, "adam_w_uq"),
        w_ukv=_adam_parts(p_ukv, w_ukv[0], m_w_ukv[0], v_w_ukv[0], 256, "adam_w_ukv"),
        w_o=_adam_parts(p_o, w_o[0], m_w_o[0], v_w_o[0], 128, "adam_w_o"),
        w_up=adam_w_up,
        w_down=_adam_parts(p_down, w_down[0], m_w_down[0], v_w_down[0], 176, "adam_w_down"),
        conv_w_mix=_adam_parts(p_cwm, conv_w_mix[0], m_conv_w_mix[0], v_conv_w_mix[0], 8, "adam_cw_mix"),
        conv_w_ffn=_adam_parts(p_cwf, conv_w_ffn[0], m_conv_w_ffn[0], v_conv_w_ffn[0], 8, "adam_cw_ffn"),
    )

    names = ["w_ada", "b_ada", "g_pre_mix", "g_post_mix", "w_in", "g_q", "w_uq", "g_kv", "w_ukv", "conv_w_mix",
             "conv_b_mix", "w_o", "g_pre_ffn", "g_post_ffn", "w_up", "conv_w_ffn", "conv_b_ffn", "w_down"]
    outs = [loss, grad_x[None]]
    for kind in range(4):
        for nm in names:
            outs.append(big[nm][kind][None] if nm in big else small[kind][nm])
    return tuple(outs)
```

```python
import math

import numpy as np
import jax
import jax.numpy as jnp
from jax import lax
from jax.experimental import pallas as pl
from jax.experimental.pallas import tpu as pltpu

F32 = jnp.float32
BF16 = jnp.bfloat16
N_DEV = 8
MESH = pl.DeviceIdType.MESH

QK_NOPE = 128
QK_ROPE = 64
V_DIM = 128
HEAD_W = 256
LANE = 128
SUB = 8
HALO = 16
STRIP_ROWS = 64
RMS_EPS = 1e-6
ROPE_THETA = 10000.0
ADAM_LR = 0.001
ADAM_B1 = 0.9
ADAM_B2 = 0.999
ADAM_EPS = 1e-08
ADAM_WD = 0.01
ADAM_STEP = 10
NEG = -1e30
LOG2E = 1.4426950408889634
LN2 = 0.6931471805599453
VMEM_LIMIT = 56 * 1024 * 1024

PREF = {"row": 256, "row_big": 512, "act_rows": 1024, "conv_rows": 512, "conv_cols": 512, "ffn_rows": 512, "attn": 1024, "attn_chunk": 1024, "attn_heads": 2}

NT_DIMS = (((1,), (1,)), ((), ()))
TN_DIMS = (((0,), (0,)), ((), ()))


def _cparams(*sem):
    return pltpu.CompilerParams(dimension_semantics=sem if sem else None, vmem_limit_bytes=VMEM_LIMIT)


def _tile(n, pref, unit=LANE):
    if n <= pref:
        return n
    t = (pref // unit) * unit
    while t >= unit:
        if n % t == 0:
            return t
        t -= unit
    return n


def _roundup(n, m):
    return (n + m - 1) // m * m


def _rsq(x):
    return lax.rsqrt(jnp.mean(x * x, axis=-1, keepdims=True) + RMS_EPS)


def _colsum(x):
    return jnp.sum(x, axis=0, keepdims=True)


def _sigmoid(x):
    return 1.0 / (1.0 + jnp.exp(-x))


def _matmul(a, b, *, ta=False, tb=False, out_dtype, tm, tn, tk, name, exchange=None, gather=None, b_shard_of=None,
            out_shard_of=None):
    m_dim, k_dim = (a.shape[1], a.shape[0]) if ta else a.shape
    if b_shard_of is not None:
        n_dim, tn = b.shape[0] * b.shape[2], b.shape[2]
    else:
        n_dim = b.shape[0] if tb else b.shape[1]
    chunk_w = n_dim // N_DEV
    if out_shard_of is not None:
        tn = _tile(chunk_w, tn)
    tm, tn, tk = _tile(m_dim, tm), _tile(n_dim, tn), _tile(k_dim, tk)
    per_chunk = chunk_w // tn if out_shard_of is not None else 1
    gi, gj, nk = m_dim // tm, n_dim // tn, k_dim // tk
    dims = (((0 if ta else 1,), (1 if tb else 0,)), ((), ()))
    chunks = list(exchange or gather or [])
    nx = len(chunks)
    comm_start, comm_finish = (_gather_start, _gather_finish) if gather else (_exchange_start, _exchange_finish)

    def body(*refs):
        a_ref, b_ref = refs[:2]
        xin, o_ref, xout = refs[2:2 + nx], refs[2 + nx], refs[3 + nx:3 + 2 * nx]
        scratch = refs[3 + 2 * nx:]
        sems = scratch[1:] if nk > 1 else scratch
        i, j, k = pl.program_id(0), pl.program_id(1), pl.program_id(2)
        if nx:
            @pl.when((i == 0) & (j == 0) & (k == 0))
            def _():
                comm_start(xin, xout, *sems)

        if gather:
            @pl.when((i == gi // 2) & (j == 0) & (k == 0))
            def _():
                _gather_forward(xin, xout, *sems)

        part = lax.dot_general(a_ref[...], b_ref[...], dims, preferred_element_type=F32)
        if nk == 1:
            o_ref[...] = part.astype(o_ref.dtype)
        else:
            acc_ref = scratch[0]

            @pl.when(k == 0)
            def _():
                acc_ref[...] = part

            @pl.when(k > 0)
            def _():
                acc_ref[...] += part

            @pl.when(k == nk - 1)
            def _():
                o_ref[...] = acc_ref[...].astype(o_ref.dtype)

        if nx:
            @pl.when((i == gi - 1) & (j == gj - 1) & (k == nk - 1))
            def _():
                comm_finish(xin, xout, *sems)

    a_spec = pl.BlockSpec((tk, tm), lambda i, j, k: (k, i)) if ta else pl.BlockSpec((tm, tk), lambda i, j, k: (i, k))
    b_spec = pl.BlockSpec((tn, tk), lambda i, j, k: (j, k)) if tb else pl.BlockSpec((tk, tn), lambda i, j, k: (k, j))
    if b_shard_of is not None:
        b_spec = pl.BlockSpec((None, tk, tn), lambda i, j, k: (b_shard_of(j), k, 0))
    o_spec, o_shape = pl.BlockSpec((tm, tn), lambda i, j, k: (i, j)), (m_dim, n_dim)
    if out_shard_of is not None:
        o_spec = pl.BlockSpec((None, tm, tn), lambda i, j, k: (out_shard_of(j // per_chunk), i, j % per_chunk))
        o_shape = (N_DEV, m_dim, chunk_w)
    hbm = pl.BlockSpec(memory_space=pl.ANY)
    out = pl.pallas_call(
        body,
        name=name,
        grid=(gi, gj, nk),
        in_specs=[a_spec, b_spec] + [hbm] * nx,
        out_specs=[o_spec] + [hbm] * nx,
        out_shape=[jax.ShapeDtypeStruct(o_shape, out_dtype)]
        + (_gathered_shapes(chunks) if gather else [jax.ShapeDtypeStruct(c.shape, c.dtype) for c in chunks]),
        scratch_shapes=([pltpu.VMEM((tm, tn), F32)] if nk > 1 else []) + (_comm_scratch(nx) if nx else []),
        compiler_params=_cparams(*(("arbitrary",) * 3 if nx else ("parallel", "parallel", "arbitrary"))),
    )(a, b, *chunks)
    return out if nx else out[0]


def _matmul_pair_shards(a, shards, *, out_dtype, tm, tn, name):
    m_dim = a.shape[0]
    n_sh, n_dim, c = shards.shape
    half = n_sh // 2
    tm, tn = _tile(m_dim, tm), _tile(n_dim, tn)

    def body(a_ref, b0_ref, b1_ref, o_ref, acc_ref):
        k = pl.program_id(2)
        part = (lax.dot_general(a_ref[:, :c], b0_ref[...], NT_DIMS, preferred_element_type=F32)
                + lax.dot_general(a_ref[:, c:], b1_ref[...], NT_DIMS, preferred_element_type=F32))

        @pl.when(k == 0)
        def _():
            acc_ref[...] = part

        @pl.when(k > 0)
        def _():
            acc_ref[...] += part

        @pl.when(k == half - 1)
        def _():
            o_ref[...] = acc_ref[...].astype(o_ref.dtype)

    return pl.pallas_call(
        body, name=name, grid=(m_dim // tm, n_dim // tn, half),
        in_specs=[pl.BlockSpec((tm, 2 * c), lambda i, j, k: (i, k)),
                  pl.BlockSpec((None, tn, c), lambda i, j, k: (k, j, 0)),
                  pl.BlockSpec((None, tn, c), lambda i, j, k: (k + half, j, 0))],
        out_specs=pl.BlockSpec((tm, tn), lambda i, j, k: (i, j)),
        out_shape=jax.ShapeDtypeStruct((m_dim, n_dim), out_dtype),
        scratch_shapes=[pltpu.VMEM((tm, tn), F32)],
        compiler_params=_cparams("parallel", "parallel", "arbitrary"),
    )(a, shards, shards)


def _shift_down(x, halo, n):
    r = pltpu.roll(x, n, 0)
    hr = pltpu.roll(halo, n, 0)
    row = lax.broadcasted_iota(jnp.int32, halo.shape, 0)
    top = jnp.where(row < n, hr, r[:SUB])
    return jnp.concatenate([top, r[SUB:]], axis=0)


def _shift_up(x, halo, n):
    ts = x.shape[0]
    r = pltpu.roll(x, ts - n, 0)
    hr = pltpu.roll(halo, SUB - n, 0)
    row = lax.broadcasted_iota(jnp.int32, halo.shape, 0)
    bot = jnp.where(row >= SUB - n, hr, r[ts - SUB:])
    return jnp.concatenate([r[:ts - SUB], bot], axis=0)


def _conv3(x, halo, w_ref, b_ref):
    return _shift_down(x, halo, 2) * w_ref[0:1, :] + _shift_down(x, halo, 1) * w_ref[1:2, :] + x * w_ref[2:3, :] + b_ref[...]


def _prev_halo(ts, col):
    return lambda j, i: (jnp.maximum(i * (ts // SUB) - 1, 0), col(j))


def _next_halo(ts, n_rows, col):
    return lambda j, i: (jnp.minimum((i + 1) * (ts // SUB), n_rows // SUB - 1), col(j))


def _modnorm_fwd(x, g, sc, sh, ts, gather):
    s_len, d = x.shape
    n_i = s_len // ts
    ng = len(gather)

    def body(x_ref, g_ref, sc_ref, sh_ref, *rest):
        gin, h_ref, gout, sems = rest[:ng], rest[ng], rest[ng + 1:2 * ng + 1], rest[2 * ng + 1:]
        i = pl.program_id(0)

        @pl.when(i == 0)
        def _():
            _gather_start(gin, gout, *sems)

        xv = x_ref[...]
        h_ref[...] = ((xv * _rsq(xv) * g_ref[...]) * (1.0 + sc_ref[...]) + sh_ref[...]).astype(BF16)

        @pl.when(i == n_i - 1)
        def _():
            _gather_forward(gin, gout, *sems)
            _gather_finish(gin, gout, *sems)

    vec = pl.BlockSpec((1, d), lambda i: (0, 0))
    hbm = pl.BlockSpec(memory_space=pl.ANY)
    return pl.pallas_call(
        body, name="modnorm_fwd", grid=(n_i,),
        in_specs=[pl.BlockSpec((ts, d), lambda i: (i, 0)), vec, vec, vec] + [hbm] * ng,
        out_specs=[pl.BlockSpec((ts, d), lambda i: (i, 0))] + [hbm] * ng,
        out_shape=[jax.ShapeDtypeStruct((s_len, d), BF16)] + _gathered_shapes(gather),
        scratch_shapes=_comm_scratch(ng),
        compiler_params=_cparams("arbitrary"),
    )(x, g, sc, sh, *gather)


def _rope_tables(pos_col, inv_freq_row, ts):
    s_len = pos_col.shape[0]
    half = QK_ROPE // 2

    def body(p_ref, f_ref, c_ref, a_ref, b_ref):
        ang = p_ref[...] * f_ref[...]
        lane = lax.broadcasted_iota(jnp.int32, ang.shape, 1)
        cos, sin = jnp.cos(ang), jnp.sin(ang)
        c_ref[...] = jnp.where(lane < 2 * half, cos, 0.0)
        a_ref[...] = jnp.where(lane < half, -sin, 0.0)
        b_ref[...] = jnp.where((lane >= half) & (lane < 2 * half), sin, 0.0)

    out = jax.ShapeDtypeStruct((s_len, LANE), F32)
    blk = pl.BlockSpec((ts, LANE), lambda i: (i, 0))
    return pl.pallas_call(
        body, name="rope_tables", grid=(s_len // ts,),
        in_specs=[pl.BlockSpec((ts, 1), lambda i: (i, 0)), pl.BlockSpec((1, LANE), lambda i: (0, 0))],
        out_specs=[blk, blk, blk], out_shape=[out, out, out],
        compiler_params=_cparams("parallel"),
    )(pos_col, inv_freq_row)


def _rope(seg, c, a, b):
    return seg * c + pltpu.roll(seg, LANE - QK_ROPE // 2, 1) * a + pltpu.roll(seg, QK_ROPE // 2, 1) * b


def _rope_t(seg, c, a, b):
    return seg * c - pltpu.roll(seg, LANE - QK_ROPE // 2, 1) * a - pltpu.roll(seg, QK_ROPE // 2, 1) * b


def _qkv_fwd(proj, lay, wuq, wk, wv, g_q, g_kv, ctab, atab, btab, n_heads, ts, scale):
    s_len = proj.shape[0]
    ql_w, kl_w = wuq.shape[0], wk.shape[0]

    def body(ql_ref, kl_ref, kr_ref, wuq_ref, wk_ref, wv_ref, gq_ref, gkv_ref, c_ref, a_ref, b_ref,
             q_out, k_out, v_out, qn_out, kvn_out):
        c, a, b = c_ref[...], a_ref[...], b_ref[...]
        ql = ql_ref[...].astype(F32)
        qn = (ql * _rsq(ql) * gq_ref[...]).astype(BF16)
        qn_out[...] = qn
        q = jnp.dot(qn, wuq_ref[...], preferred_element_type=F32)
        kl = kl_ref[...].astype(F32)
        kvn = (kl * _rsq(kl) * gkv_ref[...]).astype(BF16)
        kvn_out[...] = kvn
        kn = jnp.dot(kvn, wk_ref[...], preferred_element_type=F32)
        v_out[...] = jnp.dot(kvn, wv_ref[...], preferred_element_type=F32).astype(BF16)
        kr = _rope(kr_ref[...].astype(F32), c, a, b).astype(BF16)
        for h in range(n_heads):
            o = h * HEAD_W
            q_out[:, o:o + QK_NOPE] = (q[:, o:o + QK_NOPE] * scale).astype(BF16)
            q_out[:, o + QK_NOPE:o + HEAD_W] = (_rope(q[:, o + QK_NOPE:o + HEAD_W], c, a, b) * scale).astype(BF16)
            k_out[:, o:o + QK_NOPE] = kn[:, h * QK_NOPE:(h + 1) * QK_NOPE].astype(BF16)
            k_out[:, o + QK_NOPE:o + HEAD_W] = kr

    def full(arr):
        return pl.BlockSpec(arr.shape, lambda i: (0, 0))

    tab = pl.BlockSpec((ts, LANE), lambda i: (i, 0))
    hw, hv = n_heads * HEAD_W, n_heads * V_DIM
    return pl.pallas_call(
        body, name="qkv_fwd", grid=(s_len // ts,),
        in_specs=[pl.BlockSpec((ts, ql_w), lambda i: (i, lay["ql"] // ql_w)),
                  pl.BlockSpec((ts, kl_w), lambda i: (i, lay["kv"] // kl_w)),
                  pl.BlockSpec((ts, LANE), lambda i: (i, lay["kr"] // LANE)),
                  full(wuq), full(wk), full(wv), full(g_q), full(g_kv), tab, tab, tab],
        out_specs=[pl.BlockSpec((ts, hw), lambda i: (i, 0)), pl.BlockSpec((ts, hw), lambda i: (i, 0)),
                   pl.BlockSpec((ts, hv), lambda i: (i, 0)), pl.BlockSpec((ts, ql_w), lambda i: (i, 0)),
                   pl.BlockSpec((ts, kl_w), lambda i: (i, 0))],
        out_shape=[jax.ShapeDtypeStruct((s_len, hw), BF16), jax.ShapeDtypeStruct((s_len, hw), BF16),
                   jax.ShapeDtypeStruct((s_len, hv), BF16), jax.ShapeDtypeStruct((s_len, ql_w), BF16),
                   jax.ShapeDtypeStruct((s_len, kl_w), BF16)],
        compiler_params=_cparams("parallel"),
    )(proj, proj, proj, wuq, wk, wv, g_q, g_kv, ctab, atab, btab)


def _flash_fwd(q, k, v, n_heads, t, gather, out_cols):
    ng = len(gather)
    ck = _tile(t, PREF["attn_chunk"])
    hp = PREF["attn_heads"] if n_heads % PREF["attn_heads"] == 0 else 1
    n_groups = n_heads // hp
    s_len = q.shape[0]
    nb = s_len // t
    pairs = [(i, j) for i in range(nb) for j in range(i + 1)]
    itab = jnp.asarray(np.array([p[0] for p in pairs], np.int32))
    jtab = jnp.asarray(np.array([p[1] for p in pairs], np.int32))

    n_steps = len(pairs)

    def body(it_ref, jt_ref, q_ref, k_ref, v_ref, *rest):
        gin, (o_ref, lse_ref), gout = rest[:ng], rest[ng:ng + 2], rest[ng + 2:2 * ng + 2]
        m_sc, l_sc, acc_sc = rest[2 * ng + 2:2 * ng + 5]
        sems = rest[2 * ng + 5:]
        group, step_id = pl.program_id(0), pl.program_id(1)
        i, j = it_ref[step_id], jt_ref[step_id]

        @pl.when((group == 0) & (step_id == 0))
        def _():
            _gather_start(gin, gout, *sems)

        @pl.when((group == (3 * n_groups) // 4) & (step_id == 0))
        def _():
            _gather_forward(gin, gout, *sems)

        @pl.when(j == 0)
        def _():
            m_sc[...] = jnp.full(m_sc.shape, NEG, F32)
            l_sc[...] = jnp.zeros(l_sc.shape, F32)
            acc_sc[...] = jnp.zeros(acc_sc.shape, F32)

        def step(diag):
            for h in range(hp):
                qk, vc = slice(h * HEAD_W, (h + 1) * HEAD_W), slice(h * V_DIM, (h + 1) * V_DIM)
                cd = min(ck, t // 2) if diag else ck
                for c in range(t // cd):
                    q0 = c * cd if diag else 0
                    qs, ks = slice(q0, t), slice(c * cd, (c + 1) * cd)
                    s_t = lax.dot_general(k_ref[ks, qk], q_ref[qs, qk], NT_DIMS, preferred_element_type=F32)
                    if diag:
                        krow = lax.broadcasted_iota(jnp.int32, s_t.shape, 0)
                        qcol = lax.broadcasted_iota(jnp.int32, s_t.shape, 1)
                        s_t = jnp.where(krow <= qcol, s_t, NEG)
                    m_prev = m_sc[h, :, qs]
                    m_new = jnp.maximum(m_prev, jnp.max(s_t, axis=0, keepdims=True))
                    alpha = jnp.exp2(m_prev - m_new)
                    p_t = jnp.exp2(s_t - m_new)
                    l_sc[h, :, qs] = alpha * l_sc[h, :, qs] + jnp.sum(p_t, axis=0, keepdims=True)
                    acc_sc[h, :, qs] = acc_sc[h, :, qs] * alpha + lax.dot_general(
                        v_ref[ks, vc], p_t.astype(BF16), TN_DIMS, preferred_element_type=F32)
                    m_sc[h, :, qs] = m_new

        @pl.when(j < i)
        def _():
            step(False)

        @pl.when(j == i)
        def _():
            step(True)
            for h in range(hp):
                l = l_sc[h]
                o_ref[:, h * V_DIM:(h + 1) * V_DIM] = jnp.transpose(acc_sc[h] / l).astype(BF16)
                lse_ref[h] = m_sc[h] + jnp.log(l) * LOG2E

        @pl.when((group == n_groups - 1) & (step_id == n_steps - 1))
        def _():
            _gather_finish(gin, gout, *sems)

    hbm = pl.BlockSpec(memory_space=pl.ANY)
    grid_spec = pltpu.PrefetchScalarGridSpec(
        num_scalar_prefetch=2, grid=(n_groups, n_steps),
        in_specs=[pl.BlockSpec((t, hp * HEAD_W), lambda g, s, it, jt: (it[s], g)),
                  pl.BlockSpec((t, hp * HEAD_W), lambda g, s, it, jt: (jt[s], g)),
                  pl.BlockSpec((t, hp * V_DIM), lambda g, s, it, jt: (jt[s], g))] + [hbm] * ng,
        out_specs=[pl.BlockSpec((t, hp * V_DIM), lambda g, s, it, jt: (it[s], g)),
                   pl.BlockSpec((hp, 1, t), lambda g, s, it, jt: (g, 0, it[s]))] + [hbm] * ng,
        scratch_shapes=[pltpu.VMEM((hp, 1, t), F32), pltpu.VMEM((hp, 1, t), F32), pltpu.VMEM((hp, V_DIM, t), F32)]
        + _comm_scratch(ng),
    )
    return pl.pallas_call(
        body, name="flash_fwd", grid_spec=grid_spec,
        out_shape=[jax.ShapeDtypeStruct((s_len, out_cols), BF16),
                   jax.ShapeDtypeStruct((n_heads, 1, s_len), F32)] + _gathered_shapes(gather),
        compiler_params=_cparams("arbitrary", "arbitrary"),
    )(itab, jtab, q, k, v, *gather)


def _gconv_fwd(proj, lay, w, b, ts, tc, mixcat):
    s_len = proj.shape[0]
    cw = w.shape[1]
    nj = cw // tc
    out_off = mixcat.shape[1] - cw

    def body(gb_ref, gc_ref, ci_ref, gch_ref, cih_ref, w_ref, b_ref, mix_in, o_ref):
        i = pl.program_id(1)
        p = gc_ref[...].astype(F32) * ci_ref[...].astype(F32)
        ph = jnp.where(i > 0, gch_ref[...].astype(F32) * cih_ref[...].astype(F32), 0.0)
        o_ref[...] = (gb_ref[...].astype(F32) * _conv3(p, ph, w_ref, b_ref)).astype(BF16)

    def blk(off):
        return pl.BlockSpec((ts, tc), lambda j, i: (i, off // tc + j))

    def halo(off):
        return pl.BlockSpec((SUB, tc), _prev_halo(ts, lambda j: off // tc + j))

    return pl.pallas_call(
        body, name="gconv_fwd", grid=(nj, s_len // ts),
        in_specs=[blk(lay["gb"]), blk(lay["gc"]), blk(lay["ci"]), halo(lay["gc"]), halo(lay["ci"]),
                  pl.BlockSpec((3, tc), lambda j, i: (0, j)), pl.BlockSpec((1, tc), lambda j, i: (0, j)),
                  pl.BlockSpec(memory_space=pl.ANY)],
        out_specs=pl.BlockSpec((ts, tc), lambda j, i: (i, out_off // tc + j)),
        out_shape=jax.ShapeDtypeStruct(mixcat.shape, BF16),
        input_output_aliases={7: 0},
        compiler_params=_cparams("parallel", "parallel"),
    )(proj, proj, proj, proj, proj, w, b, mixcat)


def _post_mix_fwd(mix, x, gt, g_post, g_pre, sc, sh, ts):
    s_len, d = x.shape

    def body(mix_ref, x_ref, gt_ref, gp_ref, g2_ref, sc_ref, sh_ref, x1_ref, h2_ref):
        mv = mix_ref[...].astype(F32)
        x1 = x_ref[...] + gt_ref[...] * (mv * _rsq(mv) * gp_ref[...])
        x1_ref[...] = x1
        h2_ref[...] = ((x1 * _rsq(x1) * g2_ref[...]) * (1.0 + sc_ref[...]) + sh_ref[...]).astype(BF16)

    vec = pl.BlockSpec((1, d), lambda i: (0, 0))
    row = pl.BlockSpec((ts, d), lambda i: (i, 0))
    return pl.pallas_call(
        body, name="post_mix_fwd", grid=(s_len // ts,),
        in_specs=[row, row, vec, vec, vec, vec, vec], out_specs=[row, row],
        out_shape=[jax.ShapeDtypeStruct((s_len, d), F32), jax.ShapeDtypeStruct((s_len, d), BF16)],
        compiler_params=_cparams("parallel"),
    )(mix, x, gt, g_post, g_pre, sc, sh)


def _act_fwd(up, w, b, ts, tc):
    s_len, f2 = up.shape
    nh = f2 // 2 // tc
    rs_n = _tile(ts, STRIP_ROWS, HALO)

    def body(up_ref, uph_ref, w_ref, b_ref, u_ref, o_ref):
        i = pl.program_id(1)

        def conv_strip(r0, lanes):
            if r0 == 0:
                top = jnp.where(i > 0, uph_ref[:, lanes].astype(F32), 0.0)
                xe = jnp.concatenate([top, up_ref[0:rs_n, lanes].astype(F32)], axis=0)
            else:
                xe = up_ref[r0 - HALO:r0 + rs_n, lanes].astype(F32)
            u = (pltpu.roll(xe, 2, 0)[HALO:] * w_ref[0:1, lanes] + pltpu.roll(xe, 1, 0)[HALO:] * w_ref[1:2, lanes]
                 + xe[HALO:] * w_ref[2:3, lanes] + b_ref[:, lanes])
            u_ref[r0:r0 + rs_n, lanes] = u.astype(BF16)
            return u

        for r0 in range(0, ts, rs_n):
            for c0 in range(0, tc, LANE):
                ua = conv_strip(r0, slice(c0, c0 + LANE))
                ug = conv_strip(r0, slice(tc + c0, tc + c0 + LANE))
                o_ref[r0:r0 + rs_n, c0:c0 + LANE] = (ug * _sigmoid(ug) * ua).astype(BF16)

    pair = pl.BlockSpec((ts, 2 * tc), lambda j, i: (i, j))
    return pl.pallas_call(
        body, name="act_fwd", grid=(nh, s_len // ts),
        in_specs=[pair, pl.BlockSpec((HALO, 2 * tc), lambda j, i: (jnp.maximum(i * (ts // HALO) - 1, 0), j)),
                  pl.BlockSpec((3, 2 * tc), lambda j, i: (0, j)), pl.BlockSpec((1, 2 * tc), lambda j, i: (0, j))],
        out_specs=[pair, pl.BlockSpec((ts, tc), lambda j, i: (i, j))],
        out_shape=[jax.ShapeDtypeStruct((s_len, f2), BF16), jax.ShapeDtypeStruct((s_len, f2 // 2), BF16)],
        compiler_params=_cparams("parallel", "parallel"),
    )(up, up, w, b)


def _final_bwd(y, x1, tgt, gt, g_post, ts):
    s_len, d = y.shape

    n_i = s_len // ts

    def body(y_ref, x1_ref, t_ref, gt_ref, g_ref, dy_ref, dx2_ref, loss_ref, dgt_ref, dg_ref):
        i = pl.program_id(0)
        gtg = gt_ref[...] * g_ref[...]
        yv = y_ref[...].astype(F32)
        r = _rsq(yv)
        yh = yv * r
        e = x1_ref[...] + yh * gtg - t_ref[...]
        loss = 0.5 * jnp.sum(jnp.mean(e * e, axis=-1, keepdims=True), axis=0, keepdims=True)
        dx2 = e * (1.0 / d)
        dx2_ref[...] = dx2
        dyh = dx2 * gtg
        dy_ref[...] = (r * (dyh - yh * jnp.mean(dyh * yh, axis=-1, keepdims=True))).astype(BF16)

        @pl.when(i == 0)
        def _():
            loss_ref[...] = jnp.zeros(loss_ref.shape, F32)
            dgt_ref[...] = jnp.zeros(dgt_ref.shape, F32)

        loss_ref[...] += jnp.broadcast_to(loss, loss_ref.shape)
        dgt_ref[...] += _colsum(dx2 * yh)

        @pl.when(i == n_i - 1)
        def _():
            both = dgt_ref[...]
            dg_ref[...] = both * gt_ref[...]
            dgt_ref[...] = both * g_ref[...]

    vec = pl.BlockSpec((1, d), lambda i: (0, 0))
    row = pl.BlockSpec((ts, d), lambda i: (i, 0))
    vshape = jax.ShapeDtypeStruct((1, d), F32)
    return pl.pallas_call(
        body, name="final_bwd", grid=(s_len // ts,),
        in_specs=[row, row, row, vec, vec],
        out_specs=[row, row, pl.BlockSpec((1, LANE), lambda i: (0, 0)), vec, vec],
        out_shape=[jax.ShapeDtypeStruct((s_len, d), BF16), jax.ShapeDtypeStruct((s_len, d), F32),
                   jax.ShapeDtypeStruct((1, LANE), F32), vshape, vshape],
        compiler_params=_cparams("arbitrary"),
    )(y, x1, tgt, gt, g_post)


def _ffn_act_bwd(d_act, u, up, w, ts, tc):
    s_len, f2 = up.shape
    nh = f2 // 2 // tc
    n_i = s_len // ts
    rs_n = _tile(ts, STRIP_ROWS, HALO)
    ext = rs_n + HALO

    def body(d_ref, dh_ref, u_ref, uh_ref, x_ref, w_ref, dx_ref, dw_ref, db_ref, acc):
        i = pl.program_id(1)
        acc[...] = jnp.zeros(acc.shape, F32)

        def below(ref, halo_ref, r0, lanes):
            if r0 + ext <= ts:
                return ref[r0:r0 + ext, lanes].astype(F32)
            bot = jnp.where(i < n_i - 1, halo_ref[:, lanes].astype(F32), 0.0)
            return jnp.concatenate([ref[r0:ts, lanes].astype(F32), bot], axis=0)

        def fold8(v):
            return jnp.sum(v.reshape(v.shape[0] // SUB, SUB, v.shape[1]), axis=0)

        def conv_bwd_strip(du, r0, lanes):
            du1, du2 = pltpu.roll(du, ext - 1, 0)[:rs_n], pltpu.roll(du, ext - 2, 0)[:rs_n]
            du0 = du[:rs_n]
            dx_ref[r0:r0 + rs_n, lanes] = (du0 * w_ref[2:3, lanes] + du1 * w_ref[1:2, lanes]
                                           + du2 * w_ref[0:1, lanes]).astype(BF16)
            xv = x_ref[r0:r0 + rs_n, lanes].astype(F32)
            acc[0, :, lanes] += fold8(du2 * xv)
            acc[1, :, lanes] += fold8(du1 * xv)
            acc[2, :, lanes] += fold8(du0 * xv)
            acc[3, :, lanes] += fold8(du0)

        for r0 in range(0, ts, rs_n):
            for c0 in range(0, tc, LANE):
                la, lg = slice(c0, c0 + LANE), slice(tc + c0, tc + c0 + LANE)
                dv = below(d_ref, dh_ref, r0, la)
                ua, ug = below(u_ref, uh_ref, r0, la), below(u_ref, uh_ref, r0, lg)
                sg = _sigmoid(ug)
                conv_bwd_strip(dv * (ug * sg), r0, la)
                conv_bwd_strip(dv * ua * (sg * (1.0 + ug * (1.0 - sg))), r0, lg)

        @pl.when(i == 0)
        def _():
            dw_ref[...] = jnp.zeros(dw_ref.shape, F32)
            db_ref[...] = jnp.zeros(db_ref.shape, F32)

        dw_ref[...] += jnp.concatenate([_colsum(acc[k]) for k in range(3)], axis=0)
        db_ref[...] += _colsum(acc[3])

    pair = pl.BlockSpec((ts, 2 * tc), lambda j, i: (i, j))

    def nxt(j, i):
        return (jnp.minimum((i + 1) * (ts // HALO), s_len // HALO - 1), j)

    return pl.pallas_call(
        body, name="ffn_act_bwd", grid=(nh, n_i),
        in_specs=[pl.BlockSpec((ts, tc), lambda j, i: (i, j)), pl.BlockSpec((HALO, tc), nxt),
                  pair, pl.BlockSpec((HALO, 2 * tc), nxt), pair, pl.BlockSpec((3, 2 * tc), lambda j, i: (0, j))],
        out_specs=[pair, pl.BlockSpec((3, 2 * tc), lambda j, i: (0, j)), pl.BlockSpec((1, 2 * tc), lambda j, i: (0, j))],
        out_shape=[jax.ShapeDtypeStruct((s_len, f2), BF16), jax.ShapeDtypeStruct((3, f2), F32),
                   jax.ShapeDtypeStruct((1, f2), F32)],
        scratch_shapes=[pltpu.VMEM((4, SUB, 2 * tc), F32)],
        compiler_params=_cparams("parallel", "arbitrary"),
    )(d_act, d_act, u, u, up, w)


def _mid_bwd(dh2, x1, dx2, mix, g_pre, sc, gt_m, g_post, ts):
    s_len, d = x1.shape
    n_i = s_len // ts

    def body(dh_ref, x1_ref, dx2_ref, mix_ref, g_ref, sc_ref, gt_ref, gp_ref,
             dx1_ref, dmix_ref, dsh_ref, dsc_ref, dg_ref, dgt_ref, dgp_ref):
        i = pl.program_id(0)
        wv = (1.0 + sc_ref[...]) * g_ref[...]
        gtg = gt_ref[...] * gp_ref[...]
        dh = dh_ref[...].astype(F32)
        x1 = x1_ref[...]
        r1 = _rsq(x1)
        xh = x1 * r1
        dhx = dh * xh
        dx1 = dx2_ref[...] + r1 * (dh * wv - xh * jnp.mean(dhx * wv, axis=-1, keepdims=True))
        dx1_ref[...] = dx1
        mv = mix_ref[...].astype(F32)
        rm = _rsq(mv)
        mh = mv * rm
        dxm = dx1 * mh
        dmix_ref[...] = (rm * (dx1 * gtg - mh * jnp.mean(dxm * gtg, axis=-1, keepdims=True))).astype(BF16)

        @pl.when(i == 0)
        def _():
            for ref in (dsh_ref, dsc_ref, dgt_ref):
                ref[...] = jnp.zeros(ref.shape, F32)

        dsh_ref[...] += _colsum(dh)
        dsc_ref[...] += _colsum(dhx)
        dgt_ref[...] += _colsum(dxm)

        @pl.when(i == n_i - 1)
        def _():
            t1, t2 = dsc_ref[...], dgt_ref[...]
            dsc_ref[...] = t1 * g_ref[...]
            dg_ref[...] = t1 * (1.0 + sc_ref[...])
            dgt_ref[...] = t2 * gp_ref[...]
            dgp_ref[...] = t2 * gt_ref[...]

    vec = pl.BlockSpec((1, d), lambda i: (0, 0))
    row = pl.BlockSpec((ts, d), lambda i: (i, 0))
    vshape = jax.ShapeDtypeStruct((1, d), F32)
    return pl.pallas_call(
        body, name="mid_bwd", grid=(s_len // ts,),
        in_specs=[row, row, row, row, vec, vec, vec, vec],
        out_specs=[row, row, vec, vec, vec, vec, vec],
        out_shape=[jax.ShapeDtypeStruct((s_len, d), F32), jax.ShapeDtypeStruct((s_len, d), BF16)] + [vshape] * 5,
        compiler_params=_cparams("arbitrary"),
    )(dh2, x1, dx2, mix, g_pre, sc, gt_m, g_post)


def _first_bwd(dh1, x, dx1, g_pre, sc, ts):
    s_len, d = x.shape
    n_i = s_len // ts

    def body(dh_ref, x_ref, dx1_ref, g_ref, sc_ref, dx_ref, dsh_ref, dsc_ref, dg_ref):
        i = pl.program_id(0)
        wv = (1.0 + sc_ref[...]) * g_ref[...]
        dh = dh_ref[...].astype(F32)
        xv = x_ref[...]
        r = _rsq(xv)
        xh = xv * r
        dhx = dh * xh
        dx_ref[...] = dx1_ref[...] + r * (dh * wv - xh * jnp.mean(dhx * wv, axis=-1, keepdims=True))

        @pl.when(i == 0)
        def _():
            for ref in (dsh_ref, dsc_ref):
                ref[...] = jnp.zeros(ref.shape, F32)

        dsh_ref[...] += _colsum(dh)
        dsc_ref[...] += _colsum(dhx)

        @pl.when(i == n_i - 1)
        def _():
            t1 = dsc_ref[...]
            dsc_ref[...] = t1 * g_ref[...]
            dg_ref[...] = t1 * (1.0 + sc_ref[...])

    vec = pl.BlockSpec((1, d), lambda i: (0, 0))
    row = pl.BlockSpec((ts, d), lambda i: (i, 0))
    vshape = jax.ShapeDtypeStruct((1, d), F32)
    return pl.pallas_call(
        body, name="first_bwd", grid=(s_len // ts,),
        in_specs=[row, row, row, vec, vec], out_specs=[row, vec, vec, vec],
        out_shape=[jax.ShapeDtypeStruct((s_len, d), F32), vshape, vshape, vshape],
        compiler_params=_cparams("arbitrary"),
    )(dh1, x, dx1, g_pre, sc)


def _gconv_bwd(d_mixcat, proj, lay, w, b, ts, tc):
    s_len = proj.shape[0]
    cw = w.shape[1]
    n_i = s_len // ts
    dc_off = d_mixcat.shape[1] - cw

    def body(dc_ref, dch_ref, gb_ref, gbh_ref, gc_ref, gch_ref, ci_ref, cih_ref, w_ref, b_ref,
             dgb_ref, dgc_ref, dci_ref, dw_ref, db_ref):
        i = pl.program_id(1)
        gc, ci = gc_ref[...].astype(F32), ci_ref[...].astype(F32)
        p = gc * ci
        ph = jnp.where(i > 0, gch_ref[...].astype(F32) * cih_ref[...].astype(F32), 0.0)
        pm1, pm2 = _shift_down(p, ph, 1), _shift_down(p, ph, 2)
        z = pm2 * w_ref[0:1, :] + pm1 * w_ref[1:2, :] + p * w_ref[2:3, :] + b_ref[...]
        dc = dc_ref[...].astype(F32)
        dgb_ref[...] = (dc * z).astype(BF16)
        dz = dc * gb_ref[...].astype(F32)
        dzh = jnp.where(i < n_i - 1, dch_ref[...].astype(F32) * gbh_ref[...].astype(F32), 0.0)
        dz1, dz2 = _shift_up(dz, dzh, 1), _shift_up(dz, dzh, 2)
        dp = dz * w_ref[2:3, :] + dz1 * w_ref[1:2, :] + dz2 * w_ref[0:1, :]
        dgc_ref[...] = (dp * ci).astype(BF16)
        dci_ref[...] = (dp * gc).astype(BF16)

        @pl.when(i == 0)
        def _():
            dw_ref[...] = jnp.zeros(dw_ref.shape, F32)
            db_ref[...] = jnp.zeros(db_ref.shape, F32)

        dw_ref[0:1, :] += _colsum(dz2 * p)
        dw_ref[1:2, :] += _colsum(dz1 * p)
        dw_ref[2:3, :] += _colsum(dz * p)
        db_ref[...] += _colsum(dz)

    def blk(off):
        return pl.BlockSpec((ts, tc), lambda j, i: (i, off // tc + j))

    def prev(off):
        return pl.BlockSpec((SUB, tc), _prev_halo(ts, lambda j: off // tc + j))

    def nxt(off):
        return pl.BlockSpec((SUB, tc), _next_halo(ts, s_len, lambda j: off // tc + j))

    out_blk = pl.BlockSpec((ts, tc), lambda j, i: (i, j))
    act = jax.ShapeDtypeStruct((s_len, cw), BF16)
    return pl.pallas_call(
        body, name="gconv_bwd", grid=(cw // tc, n_i),
        in_specs=[blk(dc_off), nxt(dc_off), blk(lay["gb"]), nxt(lay["gb"]), blk(lay["gc"]), prev(lay["gc"]),
                  blk(lay["ci"]), prev(lay["ci"]),
                  pl.BlockSpec((3, tc), lambda j, i: (0, j)), pl.BlockSpec((1, tc), lambda j, i: (0, j))],
        out_specs=[out_blk, out_blk, out_blk,
                   pl.BlockSpec((3, tc), lambda j, i: (0, j)), pl.BlockSpec((1, tc), lambda j, i: (0, j))],
        out_shape=[act, act, act, jax.ShapeDtypeStruct((3, cw), F32), jax.ShapeDtypeStruct((1, cw), F32)],
        compiler_params=_cparams("parallel", "arbitrary"),
    )(d_mixcat, d_mixcat, proj, proj, proj, proj, proj, proj, w, b)


def _delta(o, d_mixcat, n_heads, ts):
    s_len = o.shape[0]

    def body(o_ref, do_ref, out_ref):
        for h in range(n_heads):
            sl = slice(h * V_DIM, (h + 1) * V_DIM)
            prod = o_ref[:, sl].astype(F32) * do_ref[:, sl].astype(F32)
            out_ref[h] = jnp.broadcast_to(jnp.sum(prod, axis=1, keepdims=True), (ts, LANE))

    hv = n_heads * V_DIM
    return pl.pallas_call(
        body, name="attn_delta", grid=(s_len // ts,),
        in_specs=[pl.BlockSpec((ts, hv), lambda i: (i, 0)), pl.BlockSpec((ts, hv), lambda i: (i, 0))],
        out_specs=pl.BlockSpec((n_heads, ts, LANE), lambda i: (0, i, 0)),
        out_shape=jax.ShapeDtypeStruct((n_heads, s_len, LANE), F32),
        compiler_params=_cparams("parallel"),
    )(o, d_mixcat)


def _flash_bwd(q, k, v, d_mixcat, lse_row, delta_row, n_heads, t, scale, exchange):
    nx = len(exchange)
    s_len = q.shape[0]
    nb = s_len // t
    pairs = [(j, i) for j in range(nb) for i in range(j, nb)]
    jtab = jnp.asarray(np.array([p[0] for p in pairs], np.int32))
    itab = jnp.asarray(np.array([p[1] for p in pairs], np.int32))
    n_steps = len(pairs)

    def body(jt_ref, it_ref, q_ref, k_ref, v_ref, do_ref, lse_ref, dl_ref, *rest):
        xin, (dq_ref, dk_ref, dv_ref), xout = rest[:nx], rest[nx:nx + 3], rest[nx + 3:2 * nx + 3]
        dq_acc, dk_acc, dv_acc = rest[2 * nx + 3:2 * nx + 6]
        sems = rest[2 * nx + 6:]
        head, step_id = pl.program_id(0), pl.program_id(1)
        j, i = jt_ref[step_id], it_ref[step_id]

        @pl.when((head == 0) & (step_id == 0))
        def _():
            _exchange_start(xin, xout, *sems)

        @pl.when(step_id == 0)
        def _():
            dq_acc[...] = jnp.zeros(dq_acc.shape, F32)

        @pl.when(i == j)
        def _():
            dk_acc[...] = jnp.zeros(dk_acc.shape, F32)
            dv_acc[...] = jnp.zeros(dv_acc.shape, F32)

        def step(diag):
            half = t // 2
            lo, hi = slice(0, half), slice(half, t)
            blocks = [(lo, lo, True), (lo, hi, False), (hi, hi, True)] if diag else [(slice(0, t), slice(0, t), False)]
            for ks, qs, masked in blocks:
                qv, kv, vv, dov = q_ref[qs, :], k_ref[ks, :], v_ref[ks, :], do_ref[qs, :]
                s_t = lax.dot_general(kv, qv, NT_DIMS, preferred_element_type=F32)
                if masked:
                    krow = lax.broadcasted_iota(jnp.int32, s_t.shape, 0)
                    qcol = lax.broadcasted_iota(jnp.int32, s_t.shape, 1)
                    s_t = jnp.where(krow <= qcol, s_t, NEG)
                p_t = jnp.exp2(s_t - lse_ref[0, :, qs])
                dv_acc[ks, :] += jnp.dot(p_t.astype(BF16), dov, preferred_element_type=F32)
                dp_t = lax.dot_general(vv, dov, NT_DIMS, preferred_element_type=F32)
                ds_t = (p_t * (dp_t - dl_ref[0, :, qs])).astype(BF16)
                dk_acc[ks, :] += jnp.dot(ds_t, qv, preferred_element_type=F32)
                n_q = qs.stop - qs.start
                rows = pl.ds(pl.multiple_of(i * t + qs.start, n_q), n_q)
                dq_acc[rows, :] += lax.dot_general(ds_t, kv, TN_DIMS, preferred_element_type=F32)

        @pl.when(i > j)
        def _():
            step(False)

        @pl.when(i == j)
        def _():
            step(True)

        @pl.when(i == nb - 1)
        def _():
            dk_ref[...] = (dk_acc[...] * LN2).astype(BF16)
            dv_ref[...] = dv_acc[...].astype(BF16)

        @pl.when(step_id == n_steps - 1)
        def _():
            dq_ref[...] = (dq_acc[...] * scale).astype(BF16)

        @pl.when((head == n_heads - 1) & (step_id == n_steps - 1))
        def _():
            _exchange_finish(xin, xout, *sems)

    hbm = pl.BlockSpec(memory_space=pl.ANY)
    hv = n_heads * V_DIM
    do_off = 0
    grid_spec = pltpu.PrefetchScalarGridSpec(
        num_scalar_prefetch=2, grid=(n_heads, n_steps),
        in_specs=[pl.BlockSpec((t, HEAD_W), lambda h, s, jt, it: (it[s], h)),
                  pl.BlockSpec((t, HEAD_W), lambda h, s, jt, it: (jt[s], h)),
                  pl.BlockSpec((t, V_DIM), lambda h, s, jt, it: (jt[s], h)),
                  pl.BlockSpec((t, V_DIM), lambda h, s, jt, it: (it[s], do_off + h)),
                  pl.BlockSpec((1, 1, t), lambda h, s, jt, it: (h, 0, it[s])),
                  pl.BlockSpec((1, 1, t), lambda h, s, jt, it: (h, 0, it[s]))] + [hbm] * nx,
        out_specs=[pl.BlockSpec((s_len, HEAD_W), lambda h, s, jt, it: (0, h)),
                   pl.BlockSpec((t, HEAD_W), lambda h, s, jt, it: (jt[s], h)),
                   pl.BlockSpec((t, V_DIM), lambda h, s, jt, it: (jt[s], h))] + [hbm] * nx,
        scratch_shapes=[pltpu.VMEM((s_len, HEAD_W), F32), pltpu.VMEM((t, HEAD_W), F32), pltpu.VMEM((t, V_DIM), F32)]
        + _comm_scratch(nx),
    )
    return pl.pallas_call(
        body, name="flash_bwd", grid_spec=grid_spec,
        out_shape=[jax.ShapeDtypeStruct((s_len, n_heads * HEAD_W), BF16),
                   jax.ShapeDtypeStruct((s_len, n_heads * HEAD_W), BF16),
                   jax.ShapeDtypeStruct((s_len, hv), BF16)] + [jax.ShapeDtypeStruct(c.shape, c.dtype) for c in exchange],
        compiler_params=_cparams("arbitrary", "arbitrary"),
    )(jtab, itab, q, k, v, d_mixcat, lse_row, delta_row, *exchange)


def _qkv_bwd(dq, dk, dv, proj, lay, wuq, wk, wv, g_q, g_kv, ctab, atab, btab, n_heads, ts):
    s_len = proj.shape[0]
    ql_w, kl_w = wuq.shape[0], wk.shape[0]
    tail_w = lay["np"] - lay["ql"]
    kv_o, kr_o = lay["kv"] - lay["ql"], lay["kr"] - lay["ql"]

    def body(dq_ref, dk_ref, dv_ref, ql_ref, kl_ref, wuq_ref, wk_ref, wv_ref, gq_ref, gkv_ref, c_ref, a_ref, b_ref,
             dqr_ref, dkn_ref, tail_ref, dgq_ref, dgkv_ref):
        i = pl.program_id(0)
        c, a, b = c_ref[...], a_ref[...], b_ref[...]
        dkr = jnp.zeros((ts, LANE), F32)
        for h in range(n_heads):
            o = h * HEAD_W
            dqr_ref[:, o:o + QK_NOPE] = dq_ref[:, o:o + QK_NOPE]
            dqr_ref[:, o + QK_NOPE:o + HEAD_W] = _rope_t(dq_ref[:, o + QK_NOPE:o + HEAD_W].astype(F32), c, a, b).astype(BF16)
            dkn_ref[:, h * QK_NOPE:(h + 1) * QK_NOPE] = dk_ref[:, o:o + QK_NOPE]
            dkr = dkr + dk_ref[:, o + QK_NOPE:o + HEAD_W].astype(F32)
        tail_ref[...] = jnp.zeros(tail_ref.shape, BF16)
        tail_ref[:, kr_o:kr_o + LANE] = _rope_t(dkr, c, a, b).astype(BF16)

        def rms_bwd(lat_ref, dn, g_ref):
            lat = lat_ref[...].astype(F32)
            r = _rsq(lat)
            xh = lat * r
            dxh = dn * g_ref[...]
            return r * (dxh - xh * jnp.mean(dxh * xh, axis=-1, keepdims=True)), _colsum(dn * xh)

        dqn = lax.dot_general(dqr_ref[...], wuq_ref[...], NT_DIMS, preferred_element_type=F32)
        d_ql, dgq = rms_bwd(ql_ref, dqn, gq_ref)
        tail_ref[:, 0:ql_w] = d_ql.astype(BF16)
        dkvn = (lax.dot_general(dkn_ref[...], wk_ref[...], NT_DIMS, preferred_element_type=F32)
                + lax.dot_general(dv_ref[...], wv_ref[...], NT_DIMS, preferred_element_type=F32))
        d_kl, dgkv = rms_bwd(kl_ref, dkvn, gkv_ref)
        tail_ref[:, kv_o:kv_o + kl_w] = d_kl.astype(BF16)

        @pl.when(i == 0)
        def _():
            dgq_ref[...] = jnp.zeros(dgq_ref.shape, F32)
            dgkv_ref[...] = jnp.zeros(dgkv_ref.shape, F32)

        dgq_ref[...] += dgq
        dgkv_ref[...] += dgkv

    def full(arr):
        return pl.BlockSpec(arr.shape, lambda i: (0, 0))

    def rows(w):
        return pl.BlockSpec((ts, w), lambda i: (i, 0))

    tab = pl.BlockSpec((ts, LANE), lambda i: (i, 0))
    hw, hv, hn = n_heads * HEAD_W, n_heads * V_DIM, n_heads * QK_NOPE
    return pl.pallas_call(
        body, name="qkv_bwd", grid=(s_len // ts,),
        in_specs=[rows(hw), rows(hw), rows(hv),
                  pl.BlockSpec((ts, ql_w), lambda i: (i, lay["ql"] // ql_w)),
                  pl.BlockSpec((ts, kl_w), lambda i: (i, lay["kv"] // kl_w)),
                  full(wuq), full(wk), full(wv), full(g_q), full(g_kv), tab, tab, tab],
        out_specs=[rows(hw), rows(hn), rows(tail_w), full(g_q), full(g_kv)],
        out_shape=[jax.ShapeDtypeStruct((s_len, hw), BF16), jax.ShapeDtypeStruct((s_len, hn), BF16),
                   jax.ShapeDtypeStruct((s_len, tail_w), BF16),
                   jax.ShapeDtypeStruct(g_q.shape, F32), jax.ShapeDtypeStruct(g_kv.shape, F32)],
        compiler_params=_cparams("arbitrary"),
    )(dq, dk, dv, proj, proj, wuq, wk, wv, g_q, g_kv, ctab, atab, btab)


def _adamw(w, g, m, v):
    m = ADAM_B1 * m + (1.0 - ADAM_B1) * g
    v = ADAM_B2 * v + (1.0 - ADAM_B2) * (g * g)
    m_hat = m / (1.0 - ADAM_B1 ** ADAM_STEP)
    v_hat = v / (1.0 - ADAM_B2 ** ADAM_STEP)
    delta = -ADAM_LR * (m_hat / (jnp.sqrt(v_hat) + ADAM_EPS) + ADAM_WD * w)
    return delta, m, v


def _adam_parts(parts, w, m, v, tr, name, small=None):
    r, c = w.shape
    tr = _tile(r, tr, SUB)
    n_i = r // tr
    ns = 0 if small is None else 1

    def body(p_ref, w_ref, m_ref, v_ref, *rest):
        vec_ref = rest[:ns]
        g_out, d_out, m_out, v_out = rest[ns:ns + 4]
        gath_ref, sems = rest[ns + 4:2 * ns + 4], rest[2 * ns + 4:]
        i = pl.program_id(0)

        def small_copies():
            send_s, recv_s, local_s = sems
            me = _my_place()
            my_i = _index(me)

            def copy(k, to, src_row):
                row = gath_ref[0].at[pl.ds(src_row, 1), :]
                return pltpu.make_async_remote_copy(src_ref=row, dst_ref=row, send_sem=send_s.at[k], recv_sem=recv_s.at[k],
                                                    device_id=to, device_id_type=MESH)

            own = pltpu.make_async_copy(vec_ref[0], gath_ref[0].at[pl.ds(my_i, 1), :], local_s)
            sends = [copy(k, _peer(me, k), my_i) for k in range(1, N_DEV)]
            recvs = [copy(k, me, _index(_peer(me, k))) for k in range(1, N_DEV)]
            return own, sends, recvs

        if ns:
            @pl.when(i == 0)
            def _():
                own, sends, _ = small_copies()
                own.start()
                own.wait()
                for cp in sends:
                    cp.start()

        g = p_ref[0].astype(F32)
        for dev in range(1, N_DEV):
            g = g + p_ref[dev].astype(F32)
        g_out[...] = g
        d_out[...], m_out[...], v_out[...] = _adamw(w_ref[...], g, m_ref[...], v_ref[...])

        if ns:
            @pl.when(i == n_i - 1)
            def _():
                _, sends, recvs = small_copies()
                for cp in recvs:
                    cp.wait_recv()
                for cp in sends:
                    cp.wait_send()

    blk = pl.BlockSpec((tr, c), lambda i: (i, 0))
    hbm = pl.BlockSpec(memory_space=pl.ANY)
    shp = jax.ShapeDtypeStruct((r, c), F32)
    out = pl.pallas_call(
        body, name=name, grid=(n_i,),
        in_specs=[pl.BlockSpec((N_DEV, tr, c), lambda i: (0, i, 0)), blk, blk, blk] + [hbm] * ns,
        out_specs=[blk, blk, blk, blk] + [hbm] * ns,
        out_shape=[shp, shp, shp, shp] + ([jax.ShapeDtypeStruct((N_DEV, small.shape[1]), F32)] if ns else []),
        scratch_shapes=([pltpu.SemaphoreType.DMA((N_DEV,))] * 2 + [pltpu.SemaphoreType.DMA]) if ns else [],
        compiler_params=_cparams("arbitrary" if ns else "parallel"),
    )(parts, w, m, v, *([small] if ns else []))
    return (out[:4], out[4]) if ns else out


def _adam_ada(cact_t, dmod_sh, w, m, v, tr):
    r, c = w.shape

    def body(ct_ref, dm_ref, w_ref, m_ref, v_ref, g_out, d_out, m_out, v_out):
        g = jnp.dot(ct_ref[...], dm_ref[...], preferred_element_type=F32, precision=lax.Precision.HIGHEST)
        g_out[...] = g
        d_out[...], m_out[...], v_out[...] = _adamw(w_ref[...], g, m_ref[...], v_ref[...])

    blk = pl.BlockSpec((tr, c), lambda i: (i, 0))
    shp = jax.ShapeDtypeStruct((r, c), F32)
    return pl.pallas_call(
        body, name="adam_ada", grid=(r // tr,),
        in_specs=[pl.BlockSpec((tr, N_DEV), lambda i: (i, 0)), pl.BlockSpec((N_DEV, c), lambda i: (0, 0)), blk, blk, blk],
        out_specs=[blk, blk, blk, blk], out_shape=[shp, shp, shp, shp],
        compiler_params=_cparams("parallel"),
    )(cact_t, dmod_sh, w, m, v)


def _adam_small(v_all, offs, ws, ms, vs):
    n_par = len(ws)

    def body(p_ref, *refs):
        w_refs, m_refs, v_refs = refs[:n_par], refs[n_par:2 * n_par], refs[2 * n_par:3 * n_par]
        sum_ref = refs[3 * n_par]
        outs = refs[3 * n_par + 1:]
        g = p_ref[0:1, :]
        for dev in range(1, N_DEV):
            g = g + p_ref[dev:dev + 1, :]
        sum_ref[...] = g
        for p in range(n_par):
            n = w_refs[p].shape[1]
            gp = sum_ref[:, offs[p]:offs[p] + n]
            outs[p][...] = gp
            (outs[n_par + p][...], outs[2 * n_par + p][...], outs[3 * n_par + p][...]) = _adamw(
                w_refs[p][...], gp, m_refs[p][...], v_refs[p][...])

    vm = pl.BlockSpec(memory_space=pltpu.VMEM)
    shapes = [jax.ShapeDtypeStruct(w.shape, F32) for w in ws]
    out = pl.pallas_call(
        body, name="adam_small", in_specs=[vm] * (1 + 3 * n_par), out_specs=[vm] * (1 + 4 * n_par),
        out_shape=[jax.ShapeDtypeStruct((1, v_all.shape[1]), F32)] + shapes * 4, compiler_params=_cparams(),
    )(v_all, *ws, *ms, *vs)
    return out[0], [out[1 + k * n_par:1 + (k + 1) * n_par] for k in range(4)]


def _my_place():
    return lax.axis_index("x"), lax.axis_index("y"), lax.axis_index("c")


def _peer(place, k):
    x, y, c = place
    return (x ^ (k >> 2), y ^ ((k >> 1) & 1), c ^ (k & 1))


def _index(place):
    return 4 * place[0] + 2 * place[1] + place[2]


def _ada_fwd(vec, w_ada, b_ada_rows):
    lv = vec.shape[1]
    d, c = w_ada.shape

    def body(vec_ref, w_ref, b_ref, gath_ref, cact_ref, mod_ref, modsh, send_a, recv_a, send_b, recv_b, local_s):
        me = _my_place()
        my_i = _index(me)

        def gather_copy(k, to, src_row):
            row = gath_ref.at[pl.ds(src_row, 1), :]
            return pltpu.make_async_remote_copy(src_ref=row, dst_ref=row, send_sem=send_a.at[k], recv_sem=recv_a.at[k],
                                                device_id=to, device_id_type=MESH)

        own = pltpu.make_async_copy(vec_ref, gath_ref.at[pl.ds(my_i, 1), :], local_s.at[0])
        own.start()
        own.wait()
        sends = [gather_copy(k, _peer(me, k), my_i) for k in range(1, N_DEV)]
        for cp in sends:
            cp.start()
        for k in range(1, N_DEV):
            gather_copy(k, me, _index(_peer(me, k))).wait_recv()
        for cp in sends:
            cp.wait_send()

        c_all = gath_ref[:, 0:d]
        cact = c_all * _sigmoid(c_all)
        cact_ref[...] = cact
        modsh[...] = jnp.dot(cact, w_ref[...], preferred_element_type=F32, precision=lax.Precision.HIGHEST)

        def mod_copy(k, to, src_row, dst_row):
            return pltpu.make_async_remote_copy(src_ref=modsh.at[pl.ds(src_row, 1), :], dst_ref=mod_ref.at[pl.ds(dst_row, 1), :],
                                                send_sem=send_b.at[k], recv_sem=recv_b.at[k],
                                                device_id=to, device_id_type=MESH)

        own = pltpu.make_async_copy(modsh.at[pl.ds(my_i, 1), :], mod_ref.at[pl.ds(my_i, 1), :], local_s.at[1])
        own.start()
        sends = [mod_copy(k, _peer(me, k), _index(_peer(me, k)), my_i) for k in range(1, N_DEV)]
        for cp in sends:
            cp.start()
        for k in range(1, N_DEV):
            mod_copy(k, me, my_i, _index(_peer(me, k))).wait_recv()
        for cp in sends:
            cp.wait_send()
        own.wait()
        mod_ref[...] = mod_ref[...] + b_ref[...]

    vm = pl.BlockSpec(memory_space=pltpu.VMEM)
    return pl.pallas_call(
        body, name="ada_fwd", in_specs=[vm, vm, vm], out_specs=[vm, vm, vm],
        out_shape=[jax.ShapeDtypeStruct((N_DEV, lv), F32), jax.ShapeDtypeStruct((N_DEV, d), F32),
                   jax.ShapeDtypeStruct((N_DEV, c), F32)],
        scratch_shapes=[pltpu.VMEM((N_DEV, c), F32)] + [pltpu.SemaphoreType.DMA((N_DEV,))] * 4
        + [pltpu.SemaphoreType.DMA((2,))],
        compiler_params=pltpu.CompilerParams(vmem_limit_bytes=VMEM_LIMIT),
    )(vec, w_ada, b_ada_rows)


PER = N_DEV - 1


def _comm_scratch(n):
    return [pltpu.SemaphoreType.DMA((n * PER,)), pltpu.SemaphoreType.DMA((n * PER,)), pltpu.SemaphoreType.DMA((n,))]


def _gather_copies(ins, outs, send_s, recv_s, local_s):
    n = len(ins)
    me = _my_place()
    x, y, c = me
    sibling = (x, y, 1 - c)
    chips = [(1 - x, y), (x, 1 - y), (1 - x, 1 - y)]

    def copy(a, k, block, to, src=None):
        slot = outs[a].at[_index(block)]
        return pltpu.make_async_remote_copy(src_ref=slot if src is None else src, dst_ref=slot,
                                            send_sem=send_s.at[a * PER + k], recv_sem=recv_s.at[a * PER + k],
                                            device_id=to, device_id_type=MESH)

    mine = [pltpu.make_async_copy(ins[a], outs[a].at[_index(me)], local_s.at[a]) for a in range(n)]
    first = []
    for a in range(n):
        first.append(copy(a, 0, me, sibling, src=ins[a]))
        first += [copy(a, 1 + j, me, (*chip, c), src=ins[a]) for j, chip in enumerate(chips)]
    landed = [copy(a, 1 + j, (*chip, c), me) for j, chip in enumerate(chips) for a in range(n)]
    passed = [copy(a, 4 + j, (*chip, c), sibling) for j, chip in enumerate(chips) for a in range(n)]
    from_sibling = [copy(a, 0, sibling, me) for a in range(n)]
    from_sibling += [copy(a, 4 + j, (*chip, 1 - c), me) for a in range(n) for j, chip in enumerate(chips)]
    return mine, first, landed, passed, from_sibling


def _gather_start(*refs):
    mine, first, _, _, _ = _gather_copies(*refs)
    for cp in mine + first:
        cp.start()


def _gather_forward(*refs):
    _, _, landed, passed, _ = _gather_copies(*refs)
    for got, fwd in zip(landed, passed):
        got.wait_recv()
        fwd.start()


def _gather_finish(*refs):
    mine, first, _, passed, from_sibling = _gather_copies(*refs)
    for cp in from_sibling:
        cp.wait_recv()
    for cp in first + passed:
        cp.wait_send()
    for cp in mine:
        cp.wait()


def _exchange_copies(ins, outs, send_s, recv_s, local_s):
    n = len(ins)
    me = _my_place()
    my_i = _index(me)

    def copy(a, k, to, src_slot, dst_slot):
        return pltpu.make_async_remote_copy(src_ref=ins[a].at[src_slot], dst_ref=outs[a].at[dst_slot],
                                            send_sem=send_s.at[a * PER + k - 1], recv_sem=recv_s.at[a * PER + k - 1],
                                            device_id=to, device_id_type=MESH)

    mine = [pltpu.make_async_copy(ins[a].at[my_i], outs[a].at[my_i], local_s.at[a]) for a in range(n)]
    sends = [copy(a, k, _peer(me, k), _index(_peer(me, k)), my_i) for k in range(1, N_DEV) for a in range(n)]
    recvs = [copy(a, k, me, my_i, _index(_peer(me, k))) for k in range(1, N_DEV) for a in range(n)]
    return mine, sends, recvs


def _exchange_start(*refs):
    mine, sends, _ = _exchange_copies(*refs)
    for cp in mine + sends:
        cp.start()


def _exchange_finish(*refs):
    mine, sends, recvs = _exchange_copies(*refs)
    for cp in recvs:
        cp.wait_recv()
    for cp in sends:
        cp.wait_send()
    for cp in mine:
        cp.wait()


def _gathered_shapes(shards):
    return [jax.ShapeDtypeStruct((N_DEV,) + s.shape, s.dtype) for s in shards]


def _proj_layout(cw, ql, kl):
    lay = {"gb": 0, "gc": cw, "ci": 2 * cw, "ql": 3 * cw}
    assert lay["ql"] % ql == 0
    lay["kv"] = _roundup(lay["ql"] + ql, kl)
    lay["kr"] = lay["kv"] + kl
    lay["np"] = _roundup(lay["kr"] + LANE, 4 * LANE)
    return lay


def _chunks_cols(g):
    r, c8 = g.shape
    return jnp.transpose(g.reshape(r, N_DEV, c8 // N_DEV), (1, 0, 2))


def _from_col_shards(a):
    n, r, c = a.shape
    return jnp.transpose(a, (1, 0, 2)).reshape(r, n * c)


def kernel(x, c, positions, w_ada, b_ada, g_pre_mix, g_post_mix, w_in, g_q, w_uq, g_kv, w_ukv, conv_w_mix, conv_b_mix, w_o, g_pre_ffn, g_post_ffn, w_up, conv_w_ffn, conv_b_ffn, w_down, loss_target, m_w_ada, m_b_ada, m_g_pre_mix, m_g_post_mix, m_w_in, m_g_q, m_w_uq, m_g_kv, m_w_ukv, m_conv_w_mix, m_conv_b_mix, m_w_o, m_g_pre_ffn, m_g_post_ffn, m_w_up, m_conv_w_ffn, m_conv_b_ffn, m_w_down, v_w_ada, v_b_ada, v_g_pre_mix, v_g_post_mix, v_w_in, v_g_q, v_w_uq, v_g_kv, v_w_ukv, v_conv_w_mix, v_conv_b_mix, v_w_o, v_g_pre_ffn, v_g_post_ffn, v_w_up, v_conv_w_ffn, v_conv_b_ffn, v_w_down):
    s_len, d = x.shape[1], x.shape[2]
    ql, kl = w_uq.shape[1], w_ukv.shape[1]
    n_heads = w_ukv.shape[2] * N_DEV // (QK_NOPE + V_DIM)
    cw = conv_w_mix.shape[2] * N_DEV
    f2 = w_up.shape[2] * N_DEV
    ff = f2 // 2
    in_cols = w_in.shape[2] * N_DEV
    ada_c = w_ada.shape[2]
    cwm_c, cwf_c = conv_w_mix.shape[2], conv_w_ffn.shape[2]
    scale = 1.0 / math.sqrt(QK_NOPE + QK_ROPE)
    lay = _proj_layout(cw, ql, kl)
    n_pad = lay["np"]
    my_i = _index(_my_place())

    ts_row = _tile(s_len, PREF["row"], SUB)
    ts_conv = _tile(s_len, PREF["conv_rows"], SUB)
    tc_conv = _tile(cw, PREF["conv_cols"])
    tc_ffn = cwf_c
    ts_ffn = _tile(s_len, PREF["ffn_rows"], SUB)
    ts_act = _tile(s_len, PREF["act_rows"], SUB)
    ts_big = _tile(s_len, PREF["row_big"], SUB)
    pair_order = [k // 2 + (k % 2) * (N_DEV // 2) for k in range(N_DEV)]
    pair_place = [pair_order.index(k) for k in range(N_DEV)]

    def paired(shards):
        return _from_col_shards(jnp.stack([shards[p] for p in pair_order]))

    def unpaired_chunks(g):
        ch = _chunks_cols(g)
        return jnp.stack([ch[p] for p in pair_place])
    ts_qkv = _tile(s_len, PREF["row"], SUB)
    t_attn = _tile(s_len, PREF["attn"])

    x2d, tgt = x[0], loss_target[0]

    vec = jnp.concatenate([c, conv_w_mix[0].reshape(1, -1), conv_w_ffn[0].reshape(1, -1)], axis=1)
    gath, cact, mod_rows = _ada_fwd(vec, w_ada[0], b_ada.reshape(N_DEV, ada_c))
    cwm_full = _from_col_shards(gath[:, d:d + 3 * cwm_c].reshape(N_DEV, 3, cwm_c))
    cwf_shards = gath[:, d + 3 * cwm_c:].reshape(N_DEV, 3, cwf_c)
    cwf_pair = paired(cwf_shards)
    cbf_pair = paired(jnp.transpose(conv_b_ffn.reshape(1, N_DEV, cwf_c), (1, 0, 2)))
    mod = mod_rows.reshape(1, N_DEV * ada_c)
    sh_m, sc_m, gt_m, sh_f, sc_f, gt_f = [mod[:, k * d:(k + 1) * d] for k in range(6)]

    h1, g_in = _modnorm_fwd(x2d, g_pre_mix, sc_m, sh_m, ts_row, [w_in[0].astype(BF16)])
    win = _from_col_shards(g_in)
    cut = np.cumsum([0, ql, kl, QK_ROPE, cw, cw, cw])
    part = [win[:, cut[k]:cut[k + 1]] for k in range(6)]

    def zcols(n):
        return jnp.zeros((d, n), BF16)

    win_p = jnp.concatenate([part[3], part[4], part[5], part[0], zcols(lay["kv"] - lay["ql"] - ql), part[1],
                             part[2], zcols(n_pad - lay["kr"] - QK_ROPE)], axis=1)
    inv_freq =1.0 / (ROPE_THETA ** (jnp.arange(0, QK_ROPE, 2, dtype=F32) / QK_ROPE))
    inv_row = jnp.tile(inv_freq, LANE // (QK_ROPE // 2)).reshape(1, LANE)
    ctab, atab, btab = _rope_tables(positions.astype(F32).reshape(s_len, 1), inv_row, _tile(s_len, 1024, SUB))

    proj, g_uq, g_ukv = _matmul(h1, win_p, out_dtype=BF16, tm=2048, tn=1280, tk=2048, name="mm_proj",
                                gather=[w_uq[0].astype(BF16), w_ukv[0].astype(BF16)])
    wuq_p = jnp.pad(_from_col_shards(g_uq).reshape(ql, n_heads, QK_NOPE + QK_ROPE),
                    ((0, 0), (0, 0), (0, HEAD_W - QK_NOPE - QK_ROPE))).reshape(ql, n_heads * HEAD_W)
    wukv = _from_col_shards(g_ukv).reshape(kl, n_heads, QK_NOPE + V_DIM)
    wk = wukv[:, :, :QK_NOPE].reshape(kl, n_heads * QK_NOPE)
    wv = wukv[:, :, QK_NOPE:].reshape(kl, n_heads * V_DIM)
    q, k, v, qn, kvn = _qkv_fwd(proj, lay, wuq_p, wk, wv, g_q, g_kv, ctab, atab, btab, n_heads, ts_qkv, scale * LOG2E)
    attn, lse_row, g_o, g_up = _flash_fwd(q, k, v, n_heads, t_attn, [w_o[0].astype(BF16), w_up[0].astype(BF16)], d)
    wo = g_o.reshape(d, d)

    def pair_shard(j):
        return j // 2 + (j % 2) * (N_DEV // 2)
    mixcat = _gconv_fwd(proj, lay, cwm_full, conv_b_mix, ts_conv, tc_conv, attn)
    mix = _matmul(mixcat, wo, out_dtype=BF16, tm=512, tn=2048, tk=2048, name="mm_mix")
    x1, h2 = _post_mix_fwd(mix, x2d, gt_m, g_post_mix, g_pre_ffn, sc_f, sh_f, ts_big)
    up, g_down = _matmul(h2, g_up, out_dtype=BF16, tm=2048, tn=cwf_c, tk=2048, name="mm_up", b_shard_of=pair_shard,
                         gather=[w_down[0].astype(BF16)])
    wdown = g_down.reshape(ff, d)
    u, act = _act_fwd(up, cwf_pair, cbf_pair, ts_act, tc_ffn)
    y = _matmul(act, wdown, out_dtype=BF16, tm=1024, tn=512, tk=ff, name="mm_down")

    dy, dx2, loss_row, d_gt_f, dg_post_ffn = _final_bwd(y, x1, tgt, gt_f, g_post_ffn, ts_big)
    gw_down = _matmul(act, dy, ta=True, out_dtype=BF16, tm=512, tn=512, tk=s_len, name="mm_gw_down")
    d_act = _matmul(dy, wdown, tb=True, out_dtype=BF16, tm=2048, tn=1408, tk=2048, name="mm_d_act")
    d_up, dcw_pair, dcb_pair = _ffn_act_bwd(d_act, u, up, cwf_pair, ts_ffn, tc_ffn)
    dcb_ffn = _from_col_shards(unpaired_chunks(dcb_pair))
    gw_up, p_down = _matmul(h2, d_up, ta=True, out_dtype=BF16, tm=512, tn=cwf_c, tk=4096, name="mm_gw_up",
                            exchange=[gw_down.reshape(N_DEV, ff // N_DEV, d)], out_shard_of=pair_shard)
    dh2 = _matmul_pair_shards(d_up, g_up, out_dtype=BF16, tm=1024, tn=1024, name="mm_dh2")
    dx1, dmix, d_sh_f, d_sc_f, dg_pre_ffn, d_gt_m, dg_post_mix = _mid_bwd(
        dh2, x1, dx2, mix, g_pre_ffn, sc_f, gt_m, g_post_mix, ts_row)
    gw_o = _matmul(mixcat, dmix, ta=True, out_dtype=BF16, tm=512, tn=512, tk=s_len, name="mm_gw_o")
    d_mixcat = _matmul(dmix, wo, tb=True, out_dtype=BF16, tm=1024, tn=1024, tk=2048, name="mm_d_mixcat")
    d_gb, d_gc, d_ci, dcw_mix, dcb_mix = _gconv_bwd(d_mixcat, proj, lay, cwm_full, conv_b_mix, ts_conv, tc_conv)
    delta = _delta(mixcat, d_mixcat, n_heads, _tile(s_len, 512, SUB))
    delta_row = delta[:, :, 0].reshape(n_heads, 1, s_len)
    dq, dk, dv, p_up, p_o = _flash_bwd(q, k, v, d_mixcat, lse_row, delta_row, n_heads, t_attn, scale,
                                       [gw_up, gw_o.reshape(N_DEV, d // N_DEV, d)])
    dq_r, dkn, d_tail, dg_q, dg_kv = _qkv_bwd(dq, dk, dv, proj, lay, wuq_p, wk, wv, g_q, g_kv, ctab, atab, btab,
                                              n_heads, ts_qkv)
    gw_uq_p = _matmul(qn, dq_r, ta=True, out_dtype=BF16, tm=768, tn=512, tk=s_len, name="mm_gw_uq")
    gw_k = _matmul(kvn, dkn, ta=True, out_dtype=BF16, tm=512, tn=512, tk=s_len, name="mm_gw_k")
    gw_v = _matmul(kvn, dv, ta=True, out_dtype=BF16, tm=512, tn=512, tk=s_len, name="mm_gw_v")
    d_proj = jnp.concatenate([d_gb, d_gc, d_ci, d_tail], axis=1)
    gw_uq = gw_uq_p.reshape(ql, n_heads, HEAD_W)[:, :, :QK_NOPE + QK_ROPE].reshape(ql, n_heads * (QK_NOPE + QK_ROPE))
    gw_ukv = jnp.concatenate([gw_k.reshape(kl, n_heads, QK_NOPE), gw_v.reshape(kl, n_heads, V_DIM)],
                             axis=2).reshape(kl, n_heads * (QK_NOPE + V_DIM))
    gw_in_p, p_uq, p_ukv, p_cwm, p_cwf = _matmul(
        h1, d_proj, ta=True, out_dtype=BF16, tm=512, tn=512, tk=s_len, name="mm_gw_in",
        exchange=[_chunks_cols(gw_uq), _chunks_cols(gw_ukv), _chunks_cols(dcw_mix), unpaired_chunks(dcw_pair)])

    gw_in = jnp.concatenate([gw_in_p[:, lay["ql"]:lay["ql"] + ql], gw_in_p[:, lay["kv"]:lay["kv"] + kl],
                             gw_in_p[:, lay["kr"]:lay["kr"] + QK_ROPE], gw_in_p[:, :3 * cw]], axis=1)
    dh1, p_in = _matmul(d_proj, win_p, tb=True, out_dtype=BF16, tm=512, tn=1024, tk=n_pad, name="mm_dh1",
                        exchange=[_chunks_cols(gw_in)])
    grad_x, d_sh_m, d_sc_m, dg_pre_mix = _first_bwd(dh1, x2d, dx1, g_pre_mix, sc_m, ts_big)

    dmod = jnp.concatenate([d_sh_m, d_sc_m, d_gt_m, d_sh_f, d_sc_f, d_gt_f], axis=1)
    small_g = [loss_row, dmod, dg_pre_mix, dg_post_mix, dg_q, dg_kv, dcb_mix, dg_pre_ffn, dg_post_ffn, dcb_ffn]
    adam_w_up, v_all = _adam_parts(p_up, w_up[0], m_w_up[0], v_w_up[0], 256, "adam_w_up",
                                   small=jnp.concatenate(small_g, axis=1))

    small_names = ["b_ada", "g_pre_mix", "g_post_mix", "g_q", "g_kv", "conv_b_mix", "g_pre_ffn", "g_post_ffn",
                   "conv_b_ffn"]
    small_w = [b_ada, g_pre_mix, g_post_mix, g_q, g_kv, conv_b_mix, g_pre_ffn, g_post_ffn, conv_b_ffn]
    small_m = [m_b_ada, m_g_pre_mix, m_g_post_mix, m_g_q, m_g_kv, m_conv_b_mix, m_g_pre_ffn, m_g_post_ffn, m_conv_b_ffn]
    small_v = [v_b_ada, v_g_pre_mix, v_g_post_mix, v_g_q, v_g_kv, v_conv_b_mix, v_g_pre_ffn, v_g_post_ffn, v_conv_b_ffn]
    offs = np.cumsum([0] + [g.shape[1] for g in small_g])
    g_sum, small_out = _adam_small(v_all, [int(o) for o in offs[1:-1]], small_w, small_m, small_v)
    small = [dict(zip(small_names, kind)) for kind in small_out]
    loss = g_sum[0, 0]

    dmod_sh = lax.dynamic_slice(v_all, (0, int(offs[1]) + my_i * ada_c), (N_DEV, ada_c))
    cact_t = jnp.transpose(cact)
    big = dict(
        w_ada=_adam_ada(cact_t, dmod_sh, w_ada[0], m_w_ada[0], v_w_ada[0], _tile(d, 256, SUB)),
        w_in=_adam_parts(p_in, w_in[0], m_w_in[0], v_w_in[0], 256, "adam_w_in"),
        w_uq=_adam_parts(p_uq, w_uq[0], m_w_uq[0], v_w_uq[0], 256, "adam_w_uq"),
        w_ukv=_adam_parts(p_ukv, w_ukv[0], m_w_ukv[0], v_w_ukv[0], 256, "adam_w_ukv"),
        w_o=_adam_parts(p_o, w_o[0], m_w_o[0], v_w_o[0], 128, "adam_w_o"),
        w_up=adam_w_up,
        w_down=_adam_parts(p_down, w_down[0], m_w_down[0], v_w_down[0], 176, "adam_w_down"),
        conv_w_mix=_adam_parts(p_cwm, conv_w_mix[0], m_conv_w_mix[0], v_conv_w_mix[0], 8, "adam_cw_mix"),
        conv_w_ffn=_adam_parts(p_cwf, conv_w_ffn[0], m_conv_w_ffn[0], v_conv_w_ffn[0], 8, "adam_cw_ffn"),
    )

    names = ["w_ada", "b_ada", "g_pre_mix", "g_post_mix", "w_in", "g_q", "w_uq", "g_kv", "w_ukv", "conv_w_mix",
             "conv_b_mix", "w_o", "g_pre_ffn", "g_post_ffn", "w_up", "conv_w_ffn", "conv_b_ffn", "w_down"]
    outs = [loss, grad_x[None]]
    for kind in range(4):
        for nm in names:
            outs.append(big[nm][kind][None] if nm in big else small[kind][nm])
    return tuple(outs)
```

```python
import math

import numpy as np
import jax
import jax.numpy as jnp
from jax import lax
from jax.experimental import pallas as pl
from jax.experimental.pallas import tpu as pltpu

F32 = jnp.float32
BF16 = jnp.bfloat16
N_DEV = 8
MESH = pl.DeviceIdType.MESH

QK_NOPE = 128
QK_ROPE = 64
V_DIM = 128
HEAD_W = 256
LANE = 128
SUB = 8
HALO = 16
STRIP_ROWS = 64
RMS_EPS = 1e-6
ROPE_THETA = 10000.0
ADAM_LR = 0.001
ADAM_B1 = 0.9
ADAM_B2 = 0.999
ADAM_EPS = 1e-08
ADAM_WD = 0.01
ADAM_STEP = 10
NEG = -1e30
LOG2E = 1.4426950408889634
LN2 = 0.6931471805599453
VMEM_LIMIT = 56 * 1024 * 1024

PREF = {"row": 256, "row_big": 512, "act_rows": 1024, "conv_rows": 512, "conv_cols": 512, "ffn_rows": 1024, "attn": 1024, "attn_chunk": 1024, "attn_heads": 4}

NT_DIMS = (((1,), (1,)), ((), ()))
TN_DIMS = (((0,), (0,)), ((), ()))


def _cparams(*sem):
    return pltpu.CompilerParams(dimension_semantics=sem if sem else None, vmem_limit_bytes=VMEM_LIMIT)


def _tile(n, pref, unit=LANE):
    if n <= pref:
        return n
    t = (pref // unit) * unit
    while t >= unit:
        if n % t == 0:
            return t
        t -= unit
    return n


def _roundup(n, m):
    return (n + m - 1) // m * m


def _rsq(x):
    return lax.rsqrt(jnp.mean(x * x, axis=-1, keepdims=True) + RMS_EPS)


def _colsum(x):
    return jnp.sum(x, axis=0, keepdims=True)


def _sigmoid(x):
    return 1.0 / (1.0 + jnp.exp(-x))


def _matmul(a, b, *, ta=False, tb=False, out_dtype, tm, tn, tk, name, exchange=None, gather=None, b_shard_of=None,
            out_shard_of=None):
    m_dim, k_dim = (a.shape[1], a.shape[0]) if ta else a.shape
    if b_shard_of is not None:
        n_dim, tn = b.shape[0] * b.shape[2], b.shape[2]
    else:
        n_dim = b.shape[0] if tb else b.shape[1]
    chunk_w = n_dim // N_DEV
    if out_shard_of is not None:
        tn = _tile(chunk_w, tn)
    tm, tn, tk = _tile(m_dim, tm), _tile(n_dim, tn), _tile(k_dim, tk)
    per_chunk = chunk_w // tn if out_shard_of is not None else 1
    gi, gj, nk = m_dim // tm, n_dim // tn, k_dim // tk
    dims = (((0 if ta else 1,), (1 if tb else 0,)), ((), ()))
    chunks = list(exchange or gather or [])
    nx = len(chunks)
    comm_start, comm_finish = (_gather_start, _gather_finish) if gather else (_exchange_start, _exchange_finish)

    def body(*refs):
        a_ref, b_ref = refs[:2]
        xin, o_ref, xout = refs[2:2 + nx], refs[2 + nx], refs[3 + nx:3 + 2 * nx]
        scratch = refs[3 + 2 * nx:]
        sems = scratch[1:] if nk > 1 else scratch
        i, j, k = pl.program_id(0), pl.program_id(1), pl.program_id(2)
        if nx:
            @pl.when((i == 0) & (j == 0) & (k == 0))
            def _():
                comm_start(xin, xout, *sems)

        if gather:
            @pl.when((i == gi // 2) & (j == 0) & (k == 0))
            def _():
                _gather_forward(xin, xout, *sems)

        part = lax.dot_general(a_ref[...], b_ref[...], dims, preferred_element_type=F32)
        if nk == 1:
            o_ref[...] = part.astype(o_ref.dtype)
        else:
            acc_ref = scratch[0]

            @pl.when(k == 0)
            def _():
                acc_ref[...] = part

            @pl.when(k > 0)
            def _():
                acc_ref[...] += part

            @pl.when(k == nk - 1)
            def _():
                o_ref[...] = acc_ref[...].astype(o_ref.dtype)

        if nx:
            @pl.when((i == gi - 1) & (j == gj - 1) & (k == nk - 1))
            def _():
                comm_finish(xin, xout, *sems)

    a_spec = pl.BlockSpec((tk, tm), lambda i, j, k: (k, i)) if ta else pl.BlockSpec((tm, tk), lambda i, j, k: (i, k))
    b_spec = pl.BlockSpec((tn, tk), lambda i, j, k: (j, k)) if tb else pl.BlockSpec((tk, tn), lambda i, j, k: (k, j))
    if b_shard_of is not None:
        b_spec = pl.BlockSpec((None, tk, tn), lambda i, j, k: (b_shard_of(j), k, 0))
    o_spec, o_shape = pl.BlockSpec((tm, tn), lambda i, j, k: (i, j)), (m_dim, n_dim)
    if out_shard_of is not None:
        o_spec = pl.BlockSpec((None, tm, tn), lambda i, j, k: (out_shard_of(j // per_chunk), i, j % per_chunk))
        o_shape = (N_DEV, m_dim, chunk_w)
    hbm = pl.BlockSpec(memory_space=pl.ANY)
    out = pl.pallas_call(
        body,
        name=name,
        grid=(gi, gj, nk),
        in_specs=[a_spec, b_spec] + [hbm] * nx,
        out_specs=[o_spec] + [hbm] * nx,
        out_shape=[jax.ShapeDtypeStruct(o_shape, out_dtype)]
        + (_gathered_shapes(chunks) if gather else [jax.ShapeDtypeStruct(c.shape, c.dtype) for c in chunks]),
        scratch_shapes=([pltpu.VMEM((tm, tn), F32)] if nk > 1 else []) + (_comm_scratch(nx) if nx else []),
        compiler_params=_cparams(*(("arbitrary",) * 3 if nx else ("parallel", "parallel", "arbitrary"))),
    )(a, b, *chunks)
    return out if nx else out[0]


def _matmul_pair_shards(a, shards, *, out_dtype, tm, tn, name):
    m_dim = a.shape[0]
    n_sh, n_dim, c = shards.shape
    half = n_sh // 2
    tm, tn = _tile(m_dim, tm), _tile(n_dim, tn)

    def body(a_ref, b0_ref, b1_ref, o_ref, acc_ref):
        k = pl.program_id(2)
        part = (lax.dot_general(a_ref[:, :c], b0_ref[...], NT_DIMS, preferred_element_type=F32)
                + lax.dot_general(a_ref[:, c:], b1_ref[...], NT_DIMS, preferred_element_type=F32))

        @pl.when(k == 0)
        def _():
            acc_ref[...] = part

        @pl.when(k > 0)
        def _():
            acc_ref[...] += part

        @pl.when(k == half - 1)
        def _():
            o_ref[...] = acc_ref[...].astype(o_ref.dtype)

    return pl.pallas_call(
        body, name=name, grid=(m_dim // tm, n_dim // tn, half),
        in_specs=[pl.BlockSpec((tm, 2 * c), lambda i, j, k: (i, k)),
                  pl.BlockSpec((None, tn, c), lambda i, j, k: (k, j, 0)),
                  pl.BlockSpec((None, tn, c), lambda i, j, k: (k + half, j, 0))],
        out_specs=pl.BlockSpec((tm, tn), lambda i, j, k: (i, j)),
        out_shape=jax.ShapeDtypeStruct((m_dim, n_dim), out_dtype),
        scratch_shapes=[pltpu.VMEM((tm, tn), F32)],
        compiler_params=_cparams("parallel", "parallel", "arbitrary"),
    )(a, shards, shards)


def _shift_down(x, halo, n):
    r = pltpu.roll(x, n, 0)
    hr = pltpu.roll(halo, n, 0)
    row = lax.broadcasted_iota(jnp.int32, halo.shape, 0)
    top = jnp.where(row < n, hr, r[:SUB])
    return jnp.concatenate([top, r[SUB:]], axis=0)


def _shift_up(x, halo, n):
    ts = x.shape[0]
    r = pltpu.roll(x, ts - n, 0)
    hr = pltpu.roll(halo, SUB - n, 0)
    row = lax.broadcasted_iota(jnp.int32, halo.shape, 0)
    bot = jnp.where(row >= SUB - n, hr, r[ts - SUB:])
    return jnp.concatenate([r[:ts - SUB], bot], axis=0)


def _conv3(x, halo, w_ref, b_ref):
    return _shift_down(x, halo, 2) * w_ref[0:1, :] + _shift_down(x, halo, 1) * w_ref[1:2, :] + x * w_ref[2:3, :] + b_ref[...]


def _prev_halo(ts, col):
    return lambda j, i: (jnp.maximum(i * (ts // SUB) - 1, 0), col(j))


def _next_halo(ts, n_rows, col):
    return lambda j, i: (jnp.minimum((i + 1) * (ts // SUB), n_rows // SUB - 1), col(j))


def _modnorm_fwd(x, g, sc, sh, ts, gather):
    s_len, d = x.shape
    n_i = s_len // ts
    ng = len(gather)

    def body(x_ref, g_ref, sc_ref, sh_ref, *rest):
        gin, h_ref, gout, sems = rest[:ng], rest[ng], rest[ng + 1:2 * ng + 1], rest[2 * ng + 1:]
        i = pl.program_id(0)

        @pl.when(i == 0)
        def _():
            _gather_start(gin, gout, *sems)

        xv = x_ref[...]
        h_ref[...] = ((xv * _rsq(xv) * g_ref[...]) * (1.0 + sc_ref[...]) + sh_ref[...]).astype(BF16)

        @pl.when(i == n_i - 1)
        def _():
            _gather_forward(gin, gout, *sems)
            _gather_finish(gin, gout, *sems)

    vec = pl.BlockSpec((1, d), lambda i: (0, 0))
    hbm = pl.BlockSpec(memory_space=pl.ANY)
    return pl.pallas_call(
        body, name="modnorm_fwd", grid=(n_i,),
        in_specs=[pl.BlockSpec((ts, d), lambda i: (i, 0)), vec, vec, vec] + [hbm] * ng,
        out_specs=[pl.BlockSpec((ts, d), lambda i: (i, 0))] + [hbm] * ng,
        out_shape=[jax.ShapeDtypeStruct((s_len, d), BF16)] + _gathered_shapes(gather),
        scratch_shapes=_comm_scratch(ng),
        compiler_params=_cparams("arbitrary"),
    )(x, g, sc, sh, *gather)


def _rope_tables(pos_col, inv_freq_row, ts):
    s_len = pos_col.shape[0]
    half = QK_ROPE // 2

    def body(p_ref, f_ref, c_ref, a_ref, b_ref):
        ang = p_ref[...] * f_ref[...]
        lane = lax.broadcasted_iota(jnp.int32, ang.shape, 1)
        cos, sin = jnp.cos(ang), jnp.sin(ang)
        c_ref[...] = jnp.where(lane < 2 * half, cos, 0.0)
        a_ref[...] = jnp.where(lane < half, -sin, 0.0)
        b_ref[...] = jnp.where((lane >= half) & (lane < 2 * half), sin, 0.0)

    out = jax.ShapeDtypeStruct((s_len, LANE), F32)
    blk = pl.BlockSpec((ts, LANE), lambda i: (i, 0))
    return pl.pallas_call(
        body, name="rope_tables", grid=(s_len // ts,),
        in_specs=[pl.BlockSpec((ts, 1), lambda i: (i, 0)), pl.BlockSpec((1, LANE), lambda i: (0, 0))],
        out_specs=[blk, blk, blk], out_shape=[out, out, out],
        compiler_params=_cparams("parallel"),
    )(pos_col, inv_freq_row)


def _rope(seg, c, a, b):
    return seg * c + pltpu.roll(seg, LANE - QK_ROPE // 2, 1) * a + pltpu.roll(seg, QK_ROPE // 2, 1) * b


def _rope_t(seg, c, a, b):
    return seg * c - pltpu.roll(seg, LANE - QK_ROPE // 2, 1) * a - pltpu.roll(seg, QK_ROPE // 2, 1) * b


def _qkv_fwd(proj, lay, wuq, wk, wv, g_q, g_kv, ctab, atab, btab, n_heads, ts, scale):
    s_len = proj.shape[0]
    ql_w, kl_w = wuq.shape[0], wk.shape[0]

    def body(ql_ref, kl_ref, kr_ref, wuq_ref, wk_ref, wv_ref, gq_ref, gkv_ref, c_ref, a_ref, b_ref,
             q_out, k_out, v_out, qn_out, kvn_out):
        c, a, b = c_ref[...], a_ref[...], b_ref[...]
        ql = ql_ref[...].astype(F32)
        qn = (ql * _rsq(ql) * gq_ref[...]).astype(BF16)
        qn_out[...] = qn
        q = jnp.dot(qn, wuq_ref[...], preferred_element_type=F32)
        kl = kl_ref[...].astype(F32)
        kvn = (kl * _rsq(kl) * gkv_ref[...]).astype(BF16)
        kvn_out[...] = kvn
        kn = jnp.dot(kvn, wk_ref[...], preferred_element_type=F32)
        v_out[...] = jnp.dot(kvn, wv_ref[...], preferred_element_type=F32).astype(BF16)
        kr = _rope(kr_ref[...].astype(F32), c, a, b).astype(BF16)
        for h in range(n_heads):
            o = h * HEAD_W
            q_out[:, o:o + QK_NOPE] = (q[:, o:o + QK_NOPE] * scale).astype(BF16)
            q_out[:, o + QK_NOPE:o + HEAD_W] = (_rope(q[:, o + QK_NOPE:o + HEAD_W], c, a, b) * scale).astype(BF16)
            k_out[:, o:o + QK_NOPE] = kn[:, h * QK_NOPE:(h + 1) * QK_NOPE].astype(BF16)
            k_out[:, o + QK_NOPE:o + HEAD_W] = kr

    def full(arr):
        return pl.BlockSpec(arr.shape, lambda i: (0, 0))

    tab = pl.BlockSpec((ts, LANE), lambda i: (i, 0))
    hw, hv = n_heads * HEAD_W, n_heads * V_DIM
    return pl.pallas_call(
        body, name="qkv_fwd", grid=(s_len // ts,),
        in_specs=[pl.BlockSpec((ts, ql_w), lambda i: (i, lay["ql"] // ql_w)),
                  pl.BlockSpec((ts, kl_w), lambda i: (i, lay["kv"] // kl_w)),
                  pl.BlockSpec((ts, LANE), lambda i: (i, lay["kr"] // LANE)),
                  full(wuq), full(wk), full(wv), full(g_q), full(g_kv), tab, tab, tab],
        out_specs=[pl.BlockSpec((ts, hw), lambda i: (i, 0)), pl.BlockSpec((ts, hw), lambda i: (i, 0)),
                   pl.BlockSpec((ts, hv), lambda i: (i, 0)), pl.BlockSpec((ts, ql_w), lambda i: (i, 0)),
                   pl.BlockSpec((ts, kl_w), lambda i: (i, 0))],
        out_shape=[jax.ShapeDtypeStruct((s_len, hw), BF16), jax.ShapeDtypeStruct((s_len, hw), BF16),
                   jax.ShapeDtypeStruct((s_len, hv), BF16), jax.ShapeDtypeStruct((s_len, ql_w), BF16),
                   jax.ShapeDtypeStruct((s_len, kl_w), BF16)],
        compiler_params=_cparams("parallel"),
    )(proj, proj, proj, wuq, wk, wv, g_q, g_kv, ctab, atab, btab)


def _flash_fwd(q, k, v, n_heads, t, gather, out_cols):
    ng = len(gather)
    ck = _tile(t, PREF["attn_chunk"])
    hp = PREF["attn_heads"] if n_heads % PREF["attn_heads"] == 0 else 1
    n_groups = n_heads // hp
    s_len = q.shape[0]
    nb = s_len // t
    pairs = [(i, j) for i in range(nb) for j in range(i + 1)]
    itab = jnp.asarray(np.array([p[0] for p in pairs], np.int32))
    jtab = jnp.asarray(np.array([p[1] for p in pairs], np.int32))

    n_steps = len(pairs)
    fwd_at = max(1, (7 * n_groups * n_steps) // 10)

    def body(it_ref, jt_ref, q_ref, k_ref, v_ref, *rest):
        gin, (o_ref, lse_ref), gout = rest[:ng], rest[ng:ng + 2], rest[ng + 2:2 * ng + 2]
        m_sc, l_sc, acc_sc = rest[2 * ng + 2:2 * ng + 5]
        sems = rest[2 * ng + 5:]
        group, step_id = pl.program_id(0), pl.program_id(1)
        i, j = it_ref[step_id], jt_ref[step_id]

        @pl.when((group == 0) & (step_id == 0))
        def _():
            _gather_start(gin, gout, *sems)

        @pl.when((group == fwd_at // n_steps) & (step_id == fwd_at % n_steps))
        def _():
            _gather_forward(gin, gout, *sems)

        @pl.when(j == 0)
        def _():
            m_sc[...] = jnp.full(m_sc.shape, NEG, F32)
            l_sc[...] = jnp.zeros(l_sc.shape, F32)
            acc_sc[...] = jnp.zeros(acc_sc.shape, F32)

        def step(diag):
            for h in range(hp):
                qk, vc = slice(h * HEAD_W, (h + 1) * HEAD_W), slice(h * V_DIM, (h + 1) * V_DIM)
                cd = min(ck, t // 2) if diag else ck
                for c in range(t // cd):
                    q0 = c * cd if diag else 0
                    qs, ks = slice(q0, t), slice(c * cd, (c + 1) * cd)
                    s_t = lax.dot_general(k_ref[ks, qk], q_ref[qs, qk], NT_DIMS, preferred_element_type=F32)
                    if diag:
                        krow = lax.broadcasted_iota(jnp.int32, s_t.shape, 0)
                        qcol = lax.broadcasted_iota(jnp.int32, s_t.shape, 1)
                        s_t = jnp.where(krow <= qcol, s_t, NEG)
                    m_prev = m_sc[h, :, qs]
                    m_new = jnp.maximum(m_prev, jnp.max(s_t, axis=0, keepdims=True))
                    alpha = jnp.exp2(m_prev - m_new)
                    p_t = jnp.exp2(s_t - m_new)
                    l_sc[h, :, qs] = alpha * l_sc[h, :, qs] + jnp.sum(p_t, axis=0, keepdims=True)
                    acc_sc[h, :, qs] = acc_sc[h, :, qs] * alpha + lax.dot_general(
                        v_ref[ks, vc], p_t.astype(BF16), TN_DIMS, preferred_element_type=F32)
                    m_sc[h, :, qs] = m_new

        @pl.when(j < i)
        def _():
            step(False)

        @pl.when(j == i)
        def _():
            step(True)
            for h in range(hp):
                l = l_sc[h]
                o_ref[:, h * V_DIM:(h + 1) * V_DIM] = jnp.transpose(acc_sc[h] / l).astype(BF16)
                lse_ref[h] = m_sc[h] + jnp.log(l) * LOG2E

        @pl.when((group == n_groups - 1) & (step_id == n_steps - 1))
        def _():
            _gather_finish(gin, gout, *sems)

    hbm = pl.BlockSpec(memory_space=pl.ANY)
    grid_spec = pltpu.PrefetchScalarGridSpec(
        num_scalar_prefetch=2, grid=(n_groups, n_steps),
        in_specs=[pl.BlockSpec((t, hp * HEAD_W), lambda g, s, it, jt: (it[s], g)),
                  pl.BlockSpec((t, hp * HEAD_W), lambda g, s, it, jt: (jt[s], g)),
                  pl.BlockSpec((t, hp * V_DIM), lambda g, s, it, jt: (jt[s], g))] + [hbm] * ng,
        out_specs=[pl.BlockSpec((t, hp * V_DIM), lambda g, s, it, jt: (it[s], g)),
                   pl.BlockSpec((hp, 1, t), lambda g, s, it, jt: (g, 0, it[s]))] + [hbm] * ng,
        scratch_shapes=[pltpu.VMEM((hp, 1, t), F32), pltpu.VMEM((hp, 1, t), F32), pltpu.VMEM((hp, V_DIM, t), F32)]
        + _comm_scratch(ng),
    )
    return pl.pallas_call(
        body, name="flash_fwd", grid_spec=grid_spec,
        out_shape=[jax.ShapeDtypeStruct((s_len, out_cols), BF16),
                   jax.ShapeDtypeStruct((n_heads, 1, s_len), F32)] + _gathered_shapes(gather),
        compiler_params=_cparams("arbitrary", "arbitrary"),
    )(itab, jtab, q, k, v, *gather)


def _gconv_fwd(proj, lay, w, b, ts, tc, mixcat):
    s_len = proj.shape[0]
    cw = w.shape[1]
    nj = cw // tc
    out_off = mixcat.shape[1] - cw

    def body(gb_ref, gc_ref, ci_ref, gch_ref, cih_ref, w_ref, b_ref, mix_in, o_ref):
        i = pl.program_id(1)
        p = gc_ref[...].astype(F32) * ci_ref[...].astype(F32)
        ph = jnp.where(i > 0, gch_ref[...].astype(F32) * cih_ref[...].astype(F32), 0.0)
        o_ref[...] = (gb_ref[...].astype(F32) * _conv3(p, ph, w_ref, b_ref)).astype(BF16)

    def blk(off):
        return pl.BlockSpec((ts, tc), lambda j, i: (i, off // tc + j))

    def halo(off):
        return pl.BlockSpec((SUB, tc), _prev_halo(ts, lambda j: off // tc + j))

    return pl.pallas_call(
        body, name="gconv_fwd", grid=(nj, s_len // ts),
        in_specs=[blk(lay["gb"]), blk(lay["gc"]), blk(lay["ci"]), halo(lay["gc"]), halo(lay["ci"]),
                  pl.BlockSpec((3, tc), lambda j, i: (0, j)), pl.BlockSpec((1, tc), lambda j, i: (0, j)),
                  pl.BlockSpec(memory_space=pl.ANY)],
        out_specs=pl.BlockSpec((ts, tc), lambda j, i: (i, out_off // tc + j)),
        out_shape=jax.ShapeDtypeStruct(mixcat.shape, BF16),
        input_output_aliases={7: 0},
        compiler_params=_cparams("parallel", "parallel"),
    )(proj, proj, proj, proj, proj, w, b, mixcat)


def _post_mix_fwd(mix, x, gt, g_post, g_pre, sc, sh, ts):
    s_len, d = x.shape

    def body(mix_ref, x_ref, gt_ref, gp_ref, g2_ref, sc_ref, sh_ref, x1_ref, h2_ref):
        mv = mix_ref[...].astype(F32)
        x1 = x_ref[...] + gt_ref[...] * (mv * _rsq(mv) * gp_ref[...])
        x1_ref[...] = x1
        h2_ref[...] = ((x1 * _rsq(x1) * g2_ref[...]) * (1.0 + sc_ref[...]) + sh_ref[...]).astype(BF16)

    vec = pl.BlockSpec((1, d), lambda i: (0, 0))
    row = pl.BlockSpec((ts, d), lambda i: (i, 0))
    return pl.pallas_call(
        body, name="post_mix_fwd", grid=(s_len // ts,),
        in_specs=[row, row, vec, vec, vec, vec, vec], out_specs=[row, row],
        out_shape=[jax.ShapeDtypeStruct((s_len, d), F32), jax.ShapeDtypeStruct((s_len, d), BF16)],
        compiler_params=_cparams("parallel"),
    )(mix, x, gt, g_post, g_pre, sc, sh)


def _act_fwd(up, w, b, ts, tc):
    s_len, f2 = up.shape
    nh = f2 // 2 // tc
    rs_n = _tile(ts, STRIP_ROWS, HALO)

    def body(up_ref, uph_ref, w_ref, b_ref, u_ref, o_ref):
        i = pl.program_id(1)

        def conv_strip(r0, lanes):
            if r0 == 0:
                top = jnp.where(i > 0, uph_ref[:, lanes].astype(F32), 0.0)
                xe = jnp.concatenate([top, up_ref[0:rs_n, lanes].astype(F32)], axis=0)
            else:
                xe = up_ref[r0 - HALO:r0 + rs_n, lanes].astype(F32)
            u = (pltpu.roll(xe, 2, 0)[HALO:] * w_ref[0:1, lanes] + pltpu.roll(xe, 1, 0)[HALO:] * w_ref[1:2, lanes]
                 + xe[HALO:] * w_ref[2:3, lanes] + b_ref[:, lanes])
            u_ref[r0:r0 + rs_n, lanes] = u.astype(BF16)
            return u

        for r0 in range(0, ts, rs_n):
            for c0 in range(0, tc, LANE):
                ua = conv_strip(r0, slice(c0, c0 + LANE))
                ug = conv_strip(r0, slice(tc + c0, tc + c0 + LANE))
                o_ref[r0:r0 + rs_n, c0:c0 + LANE] = (ug * _sigmoid(ug) * ua).astype(BF16)

    pair = pl.BlockSpec((ts, 2 * tc), lambda j, i: (i, j))
    return pl.pallas_call(
        body, name="act_fwd", grid=(nh, s_len // ts),
        in_specs=[pair, pl.BlockSpec((HALO, 2 * tc), lambda j, i: (jnp.maximum(i * (ts // HALO) - 1, 0), j)),
                  pl.BlockSpec((3, 2 * tc), lambda j, i: (0, j)), pl.BlockSpec((1, 2 * tc), lambda j, i: (0, j))],
        out_specs=[pair, pl.BlockSpec((ts, tc), lambda j, i: (i, j))],
        out_shape=[jax.ShapeDtypeStruct((s_len, f2), BF16), jax.ShapeDtypeStruct((s_len, f2 // 2), BF16)],
        compiler_params=_cparams("parallel", "parallel"),
    )(up, up, w, b)


def _final_bwd(y, x1, tgt, gt, g_post, ts):
    s_len, d = y.shape

    n_i = s_len // ts

    def body(y_ref, x1_ref, t_ref, gt_ref, g_ref, dy_ref, dx2_ref, loss_ref, dgt_ref, dg_ref):
        i = pl.program_id(0)
        gtg = gt_ref[...] * g_ref[...]
        yv = y_ref[...].astype(F32)
        r = _rsq(yv)
        yh = yv * r
        e = x1_ref[...] + yh * gtg - t_ref[...]
        loss = 0.5 * jnp.sum(jnp.mean(e * e, axis=-1, keepdims=True), axis=0, keepdims=True)
        dx2 = e * (1.0 / d)
        dx2_ref[...] = dx2
        dyh = dx2 * gtg
        dy_ref[...] = (r * (dyh - yh * jnp.mean(dyh * yh, axis=-1, keepdims=True))).astype(BF16)

        @pl.when(i == 0)
        def _():
            loss_ref[...] = jnp.zeros(loss_ref.shape, F32)
            dgt_ref[...] = jnp.zeros(dgt_ref.shape, F32)

        loss_ref[...] += jnp.broadcast_to(loss, loss_ref.shape)
        dgt_ref[...] += _colsum(dx2 * yh)

        @pl.when(i == n_i - 1)
        def _():
            both = dgt_ref[...]
            dg_ref[...] = both * gt_ref[...]
            dgt_ref[...] = both * g_ref[...]

    vec = pl.BlockSpec((1, d), lambda i: (0, 0))
    row = pl.BlockSpec((ts, d), lambda i: (i, 0))
    vshape = jax.ShapeDtypeStruct((1, d), F32)
    return pl.pallas_call(
        body, name="final_bwd", grid=(s_len // ts,),
        in_specs=[row, row, row, vec, vec],
        out_specs=[row, row, pl.BlockSpec((1, LANE), lambda i: (0, 0)), vec, vec],
        out_shape=[jax.ShapeDtypeStruct((s_len, d), BF16), jax.ShapeDtypeStruct((s_len, d), F32),
                   jax.ShapeDtypeStruct((1, LANE), F32), vshape, vshape],
        compiler_params=_cparams("arbitrary"),
    )(y, x1, tgt, gt, g_post)


def _ffn_act_bwd(d_act, u, up, w, ts, tc):
    s_len, f2 = up.shape
    nh = f2 // 2 // tc
    n_i = s_len // ts
    rs_n = _tile(ts, STRIP_ROWS, HALO)
    ext = rs_n + HALO

    def body(d_ref, dh_ref, u_ref, uh_ref, x_ref, w_ref, dx_ref, dw_ref, db_ref, acc):
        i = pl.program_id(1)
        acc[...] = jnp.zeros(acc.shape, F32)

        def below(ref, halo_ref, r0, lanes):
            if r0 + ext <= ts:
                return ref[r0:r0 + ext, lanes].astype(F32)
            bot = jnp.where(i < n_i - 1, halo_ref[:, lanes].astype(F32), 0.0)
            return jnp.concatenate([ref[r0:ts, lanes].astype(F32), bot], axis=0)

        def fold8(v):
            return jnp.sum(v.reshape(v.shape[0] // SUB, SUB, v.shape[1]), axis=0)

        def conv_bwd_strip(du, r0, lanes):
            du1, du2 = pltpu.roll(du, ext - 1, 0)[:rs_n], pltpu.roll(du, ext - 2, 0)[:rs_n]
            du0 = du[:rs_n]
            dx_ref[r0:r0 + rs_n, lanes] = (du0 * w_ref[2:3, lanes] + du1 * w_ref[1:2, lanes]
                                           + du2 * w_ref[0:1, lanes]).astype(BF16)
            xv = x_ref[r0:r0 + rs_n, lanes].astype(F32)
            acc[0, :, lanes] += fold8(du2 * xv)
            acc[1, :, lanes] += fold8(du1 * xv)
            acc[2, :, lanes] += fold8(du0 * xv)
            acc[3, :, lanes] += fold8(du0)

        for r0 in range(0, ts, rs_n):
            for c0 in range(0, tc, LANE):
                la, lg = slice(c0, c0 + LANE), slice(tc + c0, tc + c0 + LANE)
                dv = below(d_ref, dh_ref, r0, la)
                ua, ug = below(u_ref, uh_ref, r0, la), below(u_ref, uh_ref, r0, lg)
                sg = _sigmoid(ug)
                conv_bwd_strip(dv * (ug * sg), r0, la)
                conv_bwd_strip(dv * ua * (sg * (1.0 + ug * (1.0 - sg))), r0, lg)

        @pl.when(i == 0)
        def _():
            dw_ref[...] = jnp.zeros(dw_ref.shape, F32)
            db_ref[...] = jnp.zeros(db_ref.shape, F32)

        dw_ref[...] += jnp.concatenate([_colsum(acc[k]) for k in range(3)], axis=0)
        db_ref[...] += _colsum(acc[3])

    pair = pl.BlockSpec((ts, 2 * tc), lambda j, i: (i, j))

    def nxt(j, i):
        return (jnp.minimum((i + 1) * (ts // HALO), s_len // HALO - 1), j)

    return pl.pallas_call(
        body, name="ffn_act_bwd", grid=(nh, n_i),
        in_specs=[pl.BlockSpec((ts, tc), lambda j, i: (i, j)), pl.BlockSpec((HALO, tc), nxt),
                  pair, pl.BlockSpec((HALO, 2 * tc), nxt), pair, pl.BlockSpec((3, 2 * tc), lambda j, i: (0, j))],
        out_specs=[pair, pl.BlockSpec((3, 2 * tc), lambda j, i: (0, j)), pl.BlockSpec((1, 2 * tc), lambda j, i: (0, j))],
        out_shape=[jax.ShapeDtypeStruct((s_len, f2), BF16), jax.ShapeDtypeStruct((3, f2), F32),
                   jax.ShapeDtypeStruct((1, f2), F32)],
        scratch_shapes=[pltpu.VMEM((4, SUB, 2 * tc), F32)],
        compiler_params=_cparams("parallel", "arbitrary"),
    )(d_act, d_act, u, u, up, w)


def _mid_bwd(dh2, x1, dx2, mix, g_pre, sc, gt_m, g_post, ts):
    s_len, d = x1.shape
    n_i = s_len // ts

    def body(dh_ref, x1_ref, dx2_ref, mix_ref, g_ref, sc_ref, gt_ref, gp_ref,
             dx1_ref, dmix_ref, dsh_ref, dsc_ref, dg_ref, dgt_ref, dgp_ref):
        i = pl.program_id(0)
        wv = (1.0 + sc_ref[...]) * g_ref[...]
        gtg = gt_ref[...] * gp_ref[...]
        dh = dh_ref[...].astype(F32)
        x1 = x1_ref[...]
        r1 = _rsq(x1)
        xh = x1 * r1
        dhx = dh * xh
        dx1 = dx2_ref[...] + r1 * (dh * wv - xh * jnp.mean(dhx * wv, axis=-1, keepdims=True))
        dx1_ref[...] = dx1
        mv = mix_ref[...].astype(F32)
        rm = _rsq(mv)
        mh = mv * rm
        dxm = dx1 * mh
        dmix_ref[...] = (rm * (dx1 * gtg - mh * jnp.mean(dxm * gtg, axis=-1, keepdims=True))).astype(BF16)

        @pl.when(i == 0)
        def _():
            for ref in (dsh_ref, dsc_ref, dgt_ref):
                ref[...] = jnp.zeros(ref.shape, F32)

        dsh_ref[...] += _colsum(dh)
        dsc_ref[...] += _colsum(dhx)
        dgt_ref[...] += _colsum(dxm)

        @pl.when(i == n_i - 1)
        def _():
            t1, t2 = dsc_ref[...], dgt_ref[...]
            dsc_ref[...] = t1 * g_ref[...]
            dg_ref[...] = t1 * (1.0 + sc_ref[...])
            dgt_ref[...] = t2 * gp_ref[...]
            dgp_ref[...] = t2 * gt_ref[...]

    vec = pl.BlockSpec((1, d), lambda i: (0, 0))
    row = pl.BlockSpec((ts, d), lambda i: (i, 0))
    vshape = jax.ShapeDtypeStruct((1, d), F32)
    return pl.pallas_call(
        body, name="mid_bwd", grid=(s_len // ts,),
        in_specs=[row, row, row, row, vec, vec, vec, vec],
        out_specs=[row, row, vec, vec, vec, vec, vec],
        out_shape=[jax.ShapeDtypeStruct((s_len, d), F32), jax.ShapeDtypeStruct((s_len, d), BF16)] + [vshape] * 5,
        compiler_params=_cparams("arbitrary"),
    )(dh2, x1, dx2, mix, g_pre, sc, gt_m, g_post)


def _first_bwd(dh1, x, dx1, g_pre, sc, ts):
    s_len, d = x.shape
    n_i = s_len // ts

    def body(dh_ref, x_ref, dx1_ref, g_ref, sc_ref, dx_ref, dsh_ref, dsc_ref, dg_ref):
        i = pl.program_id(0)
        wv = (1.0 + sc_ref[...]) * g_ref[...]
        dh = dh_ref[...].astype(F32)
        xv = x_ref[...]
        r = _rsq(xv)
        xh = xv * r
        dhx = dh * xh
        dx_ref[...] = dx1_ref[...] + r * (dh * wv - xh * jnp.mean(dhx * wv, axis=-1, keepdims=True))

        @pl.when(i == 0)
        def _():
            for ref in (dsh_ref, dsc_ref):
                ref[...] = jnp.zeros(ref.shape, F32)

        dsh_ref[...] += _colsum(dh)
        dsc_ref[...] += _colsum(dhx)

        @pl.when(i == n_i - 1)
        def _():
            t1 = dsc_ref[...]
            dsc_ref[...] = t1 * g_ref[...]
            dg_ref[...] = t1 * (1.0 + sc_ref[...])

    vec = pl.BlockSpec((1, d), lambda i: (0, 0))
    row = pl.BlockSpec((ts, d), lambda i: (i, 0))
    vshape = jax.ShapeDtypeStruct((1, d), F32)
    return pl.pallas_call(
        body, name="first_bwd", grid=(s_len // ts,),
        in_specs=[row, row, row, vec, vec], out_specs=[row, vec, vec, vec],
        out_shape=[jax.ShapeDtypeStruct((s_len, d), F32), vshape, vshape, vshape],
        compiler_params=_cparams("arbitrary"),
    )(dh1, x, dx1, g_pre, sc)


def _gconv_bwd(d_mixcat, proj, lay, w, b, ts, tc):
    s_len = proj.shape[0]
    cw = w.shape[1]
    n_i = s_len // ts
    dc_off = d_mixcat.shape[1] - cw

    def body(dc_ref, dch_ref, gb_ref, gbh_ref, gc_ref, gch_ref, ci_ref, cih_ref, w_ref, b_ref,
             dgb_ref, dgc_ref, dci_ref, dw_ref, db_ref):
        i = pl.program_id(1)
        gc, ci = gc_ref[...].astype(F32), ci_ref[...].astype(F32)
        p = gc * ci
        ph = jnp.where(i > 0, gch_ref[...].astype(F32) * cih_ref[...].astype(F32), 0.0)
        pm1, pm2 = _shift_down(p, ph, 1), _shift_down(p, ph, 2)
        z = pm2 * w_ref[0:1, :] + pm1 * w_ref[1:2, :] + p * w_ref[2:3, :] + b_ref[...]
        dc = dc_ref[...].astype(F32)
        dgb_ref[...] = (dc * z).astype(BF16)
        dz = dc * gb_ref[...].astype(F32)
        dzh = jnp.where(i < n_i - 1, dch_ref[...].astype(F32) * gbh_ref[...].astype(F32), 0.0)
        dz1, dz2 = _shift_up(dz, dzh, 1), _shift_up(dz, dzh, 2)
        dp = dz * w_ref[2:3, :] + dz1 * w_ref[1:2, :] + dz2 * w_ref[0:1, :]
        dgc_ref[...] = (dp * ci).astype(BF16)
        dci_ref[...] = (dp * gc).astype(BF16)

        @pl.when(i == 0)
        def _():
            dw_ref[...] = jnp.zeros(dw_ref.shape, F32)
            db_ref[...] = jnp.zeros(db_ref.shape, F32)

        dw_ref[0:1, :] += _colsum(dz2 * p)
        dw_ref[1:2, :] += _colsum(dz1 * p)
        dw_ref[2:3, :] += _colsum(dz * p)
        db_ref[...] += _colsum(dz)

    def blk(off):
        return pl.BlockSpec((ts, tc), lambda j, i: (i, off // tc + j))

    def prev(off):
        return pl.BlockSpec((SUB, tc), _prev_halo(ts, lambda j: off // tc + j))

    def nxt(off):
        return pl.BlockSpec((SUB, tc), _next_halo(ts, s_len, lambda j: off // tc + j))

    out_blk = pl.BlockSpec((ts, tc), lambda j, i: (i, j))
    act = jax.ShapeDtypeStruct((s_len, cw), BF16)
    return pl.pallas_call(
        body, name="gconv_bwd", grid=(cw // tc, n_i),
        in_specs=[blk(dc_off), nxt(dc_off), blk(lay["gb"]), nxt(lay["gb"]), blk(lay["gc"]), prev(lay["gc"]),
                  blk(lay["ci"]), prev(lay["ci"]),
                  pl.BlockSpec((3, tc), lambda j, i: (0, j)), pl.BlockSpec((1, tc), lambda j, i: (0, j))],
        out_specs=[out_blk, out_blk, out_blk,
                   pl.BlockSpec((3, tc), lambda j, i: (0, j)), pl.BlockSpec((1, tc), lambda j, i: (0, j))],
        out_shape=[act, act, act, jax.ShapeDtypeStruct((3, cw), F32), jax.ShapeDtypeStruct((1, cw), F32)],
        compiler_params=_cparams("parallel", "arbitrary"),
    )(d_mixcat, d_mixcat, proj, proj, proj, proj, proj, proj, w, b)


def _delta(o, d_mixcat, n_heads, ts):
    s_len = o.shape[0]

    def body(o_ref, do_ref, out_ref):
        for h in range(n_heads):
            sl = slice(h * V_DIM, (h + 1) * V_DIM)
            prod = o_ref[:, sl].astype(F32) * do_ref[:, sl].astype(F32)
            out_ref[h] = jnp.broadcast_to(jnp.sum(prod, axis=1, keepdims=True), (ts, LANE))

    hv = n_heads * V_DIM
    return pl.pallas_call(
        body, name="attn_delta", grid=(s_len // ts,),
        in_specs=[pl.BlockSpec((ts, hv), lambda i: (i, 0)), pl.BlockSpec((ts, hv), lambda i: (i, 0))],
        out_specs=pl.BlockSpec((n_heads, ts, LANE), lambda i: (0, i, 0)),
        out_shape=jax.ShapeDtypeStruct((n_heads, s_len, LANE), F32),
        compiler_params=_cparams("parallel"),
    )(o, d_mixcat)


def _flash_bwd(q, k, v, d_mixcat, lse_row, delta_row, n_heads, t, scale, exchange):
    nx = len(exchange)
    s_len = q.shape[0]
    nb = s_len // t
    pairs = [(j, i) for j in range(nb) for i in range(j, nb)]
    jtab = jnp.asarray(np.array([p[0] for p in pairs], np.int32))
    itab = jnp.asarray(np.array([p[1] for p in pairs], np.int32))
    n_steps = len(pairs)

    def body(jt_ref, it_ref, q_ref, k_ref, v_ref, do_ref, lse_ref, dl_ref, *rest):
        xin, (dq_ref, dk_ref, dv_ref), xout = rest[:nx], rest[nx:nx + 3], rest[nx + 3:2 * nx + 3]
        dq_acc, dk_acc, dv_acc = rest[2 * nx + 3:2 * nx + 6]
        sems = rest[2 * nx + 6:]
        head, step_id = pl.program_id(0), pl.program_id(1)
        j, i = jt_ref[step_id], it_ref[step_id]

        @pl.when((head == 0) & (step_id == 0))
        def _():
            _exchange_start(xin, xout, *sems)

        @pl.when(step_id == 0)
        def _():
            dq_acc[...] = jnp.zeros(dq_acc.shape, F32)

        @pl.when(i == j)
        def _():
            dk_acc[...] = jnp.zeros(dk_acc.shape, F32)
            dv_acc[...] = jnp.zeros(dv_acc.shape, F32)

        def step(diag):
            half = t // 2
            lo, hi = slice(0, half), slice(half, t)
            blocks = [(lo, lo, True), (lo, hi, False), (hi, hi, True)] if diag else [(slice(0, t), slice(0, t), False)]
            for ks, qs, masked in blocks:
                qv, kv, vv, dov = q_ref[qs, :], k_ref[ks, :], v_ref[ks, :], do_ref[qs, :]
                s_t = lax.dot_general(kv, qv, NT_DIMS, preferred_element_type=F32)
                if masked:
                    krow = lax.broadcasted_iota(jnp.int32, s_t.shape, 0)
                    qcol = lax.broadcasted_iota(jnp.int32, s_t.shape, 1)
                    s_t = jnp.where(krow <= qcol, s_t, NEG)
                p_t = jnp.exp2(s_t - lse_ref[0, :, qs])
                dv_acc[ks, :] += jnp.dot(p_t.astype(BF16), dov, preferred_element_type=F32)
                dp_t = lax.dot_general(vv, dov, NT_DIMS, preferred_element_type=F32)
                ds_t = (p_t * (dp_t - dl_ref[0, :, qs])).astype(BF16)
                dk_acc[ks, :] += jnp.dot(ds_t, qv, preferred_element_type=F32)
                n_q = qs.stop - qs.start
                rows = pl.ds(pl.multiple_of(i * t + qs.start, n_q), n_q)
                dq_acc[rows, :] += lax.dot_general(ds_t, kv, TN_DIMS, preferred_element_type=F32)

        @pl.when(i > j)
        def _():
            step(False)

        @pl.when(i == j)
        def _():
            step(True)

        @pl.when(i == nb - 1)
        def _():
            dk_ref[...] = (dk_acc[...] * LN2).astype(BF16)
            dv_ref[...] = dv_acc[...].astype(BF16)

        @pl.when(step_id == n_steps - 1)
        def _():
            dq_ref[...] = (dq_acc[...] * scale).astype(BF16)

        @pl.when((head == n_heads - 1) & (step_id == n_steps - 1))
        def _():
            _exchange_finish(xin, xout, *sems)

    hbm = pl.BlockSpec(memory_space=pl.ANY)
    hv = n_heads * V_DIM
    do_off = 0
    grid_spec = pltpu.PrefetchScalarGridSpec(
        num_scalar_prefetch=2, grid=(n_heads, n_steps),
        in_specs=[pl.BlockSpec((t, HEAD_W), lambda h, s, jt, it: (it[s], h)),
                  pl.BlockSpec((t, HEAD_W), lambda h, s, jt, it: (jt[s], h)),
                  pl.BlockSpec((t, V_DIM), lambda h, s, jt, it: (jt[s], h)),
                  pl.BlockSpec((t, V_DIM), lambda h, s, jt, it: (it[s], do_off + h)),
                  pl.BlockSpec((1, 1, t), lambda h, s, jt, it: (h, 0, it[s])),
                  pl.BlockSpec((1, 1, t), lambda h, s, jt, it: (h, 0, it[s]))] + [hbm] * nx,
        out_specs=[pl.BlockSpec((s_len, HEAD_W), lambda h, s, jt, it: (0, h)),
                   pl.BlockSpec((t, HEAD_W), lambda h, s, jt, it: (jt[s], h)),
                   pl.BlockSpec((t, V_DIM), lambda h, s, jt, it: (jt[s], h))] + [hbm] * nx,
        scratch_shapes=[pltpu.VMEM((s_len, HEAD_W), F32), pltpu.VMEM((t, HEAD_W), F32), pltpu.VMEM((t, V_DIM), F32)]
        + _comm_scratch(nx),
    )
    return pl.pallas_call(
        body, name="flash_bwd", grid_spec=grid_spec,
        out_shape=[jax.ShapeDtypeStruct((s_len, n_heads * HEAD_W), BF16),
                   jax.ShapeDtypeStruct((s_len, n_heads * HEAD_W), BF16),
                   jax.ShapeDtypeStruct((s_len, hv), BF16)] + [jax.ShapeDtypeStruct(c.shape, c.dtype) for c in exchange],
        compiler_params=_cparams("arbitrary", "arbitrary"),
    )(jtab, itab, q, k, v, d_mixcat, lse_row, delta_row, *exchange)


def _qkv_bwd(dq, dk, dv, proj, lay, wuq, wk, wv, g_q, g_kv, ctab, atab, btab, n_heads, ts):
    s_len = proj.shape[0]
    ql_w, kl_w = wuq.shape[0], wk.shape[0]
    tail_w = lay["np"] - lay["ql"]
    kv_o, kr_o = lay["kv"] - lay["ql"], lay["kr"] - lay["ql"]

    def body(dq_ref, dk_ref, dv_ref, ql_ref, kl_ref, wuq_ref, wk_ref, wv_ref, gq_ref, gkv_ref, c_ref, a_ref, b_ref,
             dqr_ref, dkn_ref, tail_ref, dgq_ref, dgkv_ref):
        i = pl.program_id(0)
        c, a, b = c_ref[...], a_ref[...], b_ref[...]
        dkr = jnp.zeros((ts, LANE), F32)
        for h in range(n_heads):
            o = h * HEAD_W
            dqr_ref[:, o:o + QK_NOPE] = dq_ref[:, o:o + QK_NOPE]
            dqr_ref[:, o + QK_NOPE:o + HEAD_W] = _rope_t(dq_ref[:, o + QK_NOPE:o + HEAD_W].astype(F32), c, a, b).astype(BF16)
            dkn_ref[:, h * QK_NOPE:(h + 1) * QK_NOPE] = dk_ref[:, o:o + QK_NOPE]
            dkr = dkr + dk_ref[:, o + QK_NOPE:o + HEAD_W].astype(F32)
        tail_ref[...] = jnp.zeros(tail_ref.shape, BF16)
        tail_ref[:, kr_o:kr_o + LANE] = _rope_t(dkr, c, a, b).astype(BF16)

        def rms_bwd(lat_ref, dn, g_ref):
            lat = lat_ref[...].astype(F32)
            r = _rsq(lat)
            xh = lat * r
            dxh = dn * g_ref[...]
            return r * (dxh - xh * jnp.mean(dxh * xh, axis=-1, keepdims=True)), _colsum(dn * xh)

        dqn = lax.dot_general(dqr_ref[...], wuq_ref[...], NT_DIMS, preferred_element_type=F32)
        d_ql, dgq = rms_bwd(ql_ref, dqn, gq_ref)
        tail_ref[:, 0:ql_w] = d_ql.astype(BF16)
        dkvn = (lax.dot_general(dkn_ref[...], wk_ref[...], NT_DIMS, preferred_element_type=F32)
                + lax.dot_general(dv_ref[...], wv_ref[...], NT_DIMS, preferred_element_type=F32))
        d_kl, dgkv = rms_bwd(kl_ref, dkvn, gkv_ref)
        tail_ref[:, kv_o:kv_o + kl_w] = d_kl.astype(BF16)

        @pl.when(i == 0)
        def _():
            dgq_ref[...] = jnp.zeros(dgq_ref.shape, F32)
            dgkv_ref[...] = jnp.zeros(dgkv_ref.shape, F32)

        dgq_ref[...] += dgq
        dgkv_ref[...] += dgkv

    def full(arr):
        return pl.BlockSpec(arr.shape, lambda i: (0, 0))

    def rows(w):
        return pl.BlockSpec((ts, w), lambda i: (i, 0))

    tab = pl.BlockSpec((ts, LANE), lambda i: (i, 0))
    hw, hv, hn = n_heads * HEAD_W, n_heads * V_DIM, n_heads * QK_NOPE
    return pl.pallas_call(
        body, name="qkv_bwd", grid=(s_len // ts,),
        in_specs=[rows(hw), rows(hw), rows(hv),
                  pl.BlockSpec((ts, ql_w), lambda i: (i, lay["ql"] // ql_w)),
                  pl.BlockSpec((ts, kl_w), lambda i: (i, lay["kv"] // kl_w)),
                  full(wuq), full(wk), full(wv), full(g_q), full(g_kv), tab, tab, tab],
        out_specs=[rows(hw), rows(hn), rows(tail_w), full(g_q), full(g_kv)],
        out_shape=[jax.ShapeDtypeStruct((s_len, hw), BF16), jax.ShapeDtypeStruct((s_len, hn), BF16),
                   jax.ShapeDtypeStruct((s_len, tail_w), BF16),
                   jax.ShapeDtypeStruct(g_q.shape, F32), jax.ShapeDtypeStruct(g_kv.shape, F32)],
        compiler_params=_cparams("arbitrary"),
    )(dq, dk, dv, proj, proj, wuq, wk, wv, g_q, g_kv, ctab, atab, btab)


def _adamw(w, g, m, v):
    m = ADAM_B1 * m + (1.0 - ADAM_B1) * g
    v = ADAM_B2 * v + (1.0 - ADAM_B2) * (g * g)
    m_hat = m / (1.0 - ADAM_B1 ** ADAM_STEP)
    v_hat = v / (1.0 - ADAM_B2 ** ADAM_STEP)
    delta = -ADAM_LR * (m_hat / (jnp.sqrt(v_hat) + ADAM_EPS) + ADAM_WD * w)
    return delta, m, v


def _adam_parts(parts, w, m, v, tr, name, small=None):
    r, c = w.shape
    tr = _tile(r, tr, SUB)
    n_i = r // tr
    ns = 0 if small is None else 1

    def body(p_ref, w_ref, m_ref, v_ref, *rest):
        vec_ref = rest[:ns]
        g_out, d_out, m_out, v_out = rest[ns:ns + 4]
        gath_ref, sems = rest[ns + 4:2 * ns + 4], rest[2 * ns + 4:]
        i = pl.program_id(0)

        def small_copies():
            send_s, recv_s, local_s = sems
            me = _my_place()
            my_i = _index(me)

            def copy(k, to, src_row):
                row = gath_ref[0].at[pl.ds(src_row, 1), :]
                return pltpu.make_async_remote_copy(src_ref=row, dst_ref=row, send_sem=send_s.at[k], recv_sem=recv_s.at[k],
                                                    device_id=to, device_id_type=MESH)

            own = pltpu.make_async_copy(vec_ref[0], gath_ref[0].at[pl.ds(my_i, 1), :], local_s)
            sends = [copy(k, _peer(me, k), my_i) for k in range(1, N_DEV)]
            recvs = [copy(k, me, _index(_peer(me, k))) for k in range(1, N_DEV)]
            return own, sends, recvs

        if ns:
            @pl.when(i == 0)
            def _():
                own, sends, _ = small_copies()
                own.start()
                own.wait()
                for cp in sends:
                    cp.start()

        g = p_ref[0].astype(F32)
        for dev in range(1, N_DEV):
            g = g + p_ref[dev].astype(F32)
        g_out[...] = g
        d_out[...], m_out[...], v_out[...] = _adamw(w_ref[...], g, m_ref[...], v_ref[...])

        if ns:
            @pl.when(i == n_i - 1)
            def _():
                _, sends, recvs = small_copies()
                for cp in recvs:
                    cp.wait_recv()
                for cp in sends:
                    cp.wait_send()

    blk = pl.BlockSpec((tr, c), lambda i: (i, 0))
    hbm = pl.BlockSpec(memory_space=pl.ANY)
    shp = jax.ShapeDtypeStruct((r, c), F32)
    out = pl.pallas_call(
        body, name=name, grid=(n_i,),
        in_specs=[pl.BlockSpec((N_DEV, tr, c), lambda i: (0, i, 0)), blk, blk, blk] + [hbm] * ns,
        out_specs=[blk, blk, blk, blk] + [hbm] * ns,
        out_shape=[shp, shp, shp, shp] + ([jax.ShapeDtypeStruct((N_DEV, small.shape[1]), F32)] if ns else []),
        scratch_shapes=([pltpu.SemaphoreType.DMA((N_DEV,))] * 2 + [pltpu.SemaphoreType.DMA]) if ns else [],
        compiler_params=_cparams("arbitrary" if ns else "parallel"),
    )(parts, w, m, v, *([small] if ns else []))
    return (out[:4], out[4]) if ns else out


def _adam_ada(cact_t, dmod_sh, w, m, v, tr):
    r, c = w.shape

    def body(ct_ref, dm_ref, w_ref, m_ref, v_ref, g_out, d_out, m_out, v_out):
        g = jnp.dot(ct_ref[...], dm_ref[...], preferred_element_type=F32, precision=lax.Precision.HIGHEST)
        g_out[...] = g
        d_out[...], m_out[...], v_out[...] = _adamw(w_ref[...], g, m_ref[...], v_ref[...])

    blk = pl.BlockSpec((tr, c), lambda i: (i, 0))
    shp = jax.ShapeDtypeStruct((r, c), F32)
    return pl.pallas_call(
        body, name="adam_ada", grid=(r // tr,),
        in_specs=[pl.BlockSpec((tr, N_DEV), lambda i: (i, 0)), pl.BlockSpec((N_DEV, c), lambda i: (0, 0)), blk, blk, blk],
        out_specs=[blk, blk, blk, blk], out_shape=[shp, shp, shp, shp],
        compiler_params=_cparams("parallel"),
    )(cact_t, dmod_sh, w, m, v)


def _adam_small(v_all, offs, ws, ms, vs):
    n_par = len(ws)

    def body(p_ref, *refs):
        w_refs, m_refs, v_refs = refs[:n_par], refs[n_par:2 * n_par], refs[2 * n_par:3 * n_par]
        sum_ref = refs[3 * n_par]
        outs = refs[3 * n_par + 1:]
        g = p_ref[0:1, :]
        for dev in range(1, N_DEV):
            g = g + p_ref[dev:dev + 1, :]
        sum_ref[...] = g
        for p in range(n_par):
            n = w_refs[p].shape[1]
            gp = sum_ref[:, offs[p]:offs[p] + n]
            outs[p][...] = gp
            (outs[n_par + p][...], outs[2 * n_par + p][...], outs[3 * n_par + p][...]) = _adamw(
                w_refs[p][...], gp, m_refs[p][...], v_refs[p][...])

    vm = pl.BlockSpec(memory_space=pltpu.VMEM)
    shapes = [jax.ShapeDtypeStruct(w.shape, F32) for w in ws]
    out = pl.pallas_call(
        body, name="adam_small", in_specs=[vm] * (1 + 3 * n_par), out_specs=[vm] * (1 + 4 * n_par),
        out_shape=[jax.ShapeDtypeStruct((1, v_all.shape[1]), F32)] + shapes * 4, compiler_params=_cparams(),
    )(v_all, *ws, *ms, *vs)
    return out[0], [out[1 + k * n_par:1 + (k + 1) * n_par] for k in range(4)]


def _my_place():
    return lax.axis_index("x"), lax.axis_index("y"), lax.axis_index("c")


def _peer(place, k):
    x, y, c = place
    return (x ^ (k >> 2), y ^ ((k >> 1) & 1), c ^ (k & 1))


def _index(place):
    return 4 * place[0] + 2 * place[1] + place[2]


def _ada_fwd(vec, w_ada, b_ada_rows):
    lv = vec.shape[1]
    d, c = w_ada.shape

    def body(vec_ref, w_ref, b_ref, gath_ref, cact_ref, mod_ref, modsh, send_a, recv_a, send_b, recv_b, local_s):
        me = _my_place()
        my_i = _index(me)

        def gather_copy(k, to, src_row):
            row = gath_ref.at[pl.ds(src_row, 1), :]
            return pltpu.make_async_remote_copy(src_ref=row, dst_ref=row, send_sem=send_a.at[k], recv_sem=recv_a.at[k],
                                                device_id=to, device_id_type=MESH)

        own = pltpu.make_async_copy(vec_ref, gath_ref.at[pl.ds(my_i, 1), :], local_s.at[0])
        own.start()
        own.wait()
        sends = [gather_copy(k, _peer(me, k), my_i) for k in range(1, N_DEV)]
        for cp in sends:
            cp.start()
        for k in range(1, N_DEV):
            gather_copy(k, me, _index(_peer(me, k))).wait_recv()
        for cp in sends:
            cp.wait_send()

        c_all = gath_ref[:, 0:d]
        cact = c_all * _sigmoid(c_all)
        cact_ref[...] = cact
        modsh[...] = jnp.dot(cact, w_ref[...], preferred_element_type=F32, precision=lax.Precision.HIGHEST)

        def mod_copy(k, to, src_row, dst_row):
            return pltpu.make_async_remote_copy(src_ref=modsh.at[pl.ds(src_row, 1), :], dst_ref=mod_ref.at[pl.ds(dst_row, 1), :],
                                                send_sem=send_b.at[k], recv_sem=recv_b.at[k],
                                                device_id=to, device_id_type=MESH)

        own = pltpu.make_async_copy(modsh.at[pl.ds(my_i, 1), :], mod_ref.at[pl.ds(my_i, 1), :], local_s.at[1])
        own.start()
        sends = [mod_copy(k, _peer(me, k), _index(_peer(me, k)), my_i) for k in range(1, N_DEV)]
        for cp in sends:
            cp.start()
        for k in range(1, N_DEV):
            mod_copy(k, me, my_i, _index(_peer(me, k))).wait_recv()
        for cp in sends:
            cp.wait_send()
        own.wait()
        mod_ref[...] = mod_ref[...] + b_ref[...]

    vm = pl.BlockSpec(memory_space=pltpu.VMEM)
    return pl.pallas_call(
        body, name="ada_fwd", in_specs=[vm, vm, vm], out_specs=[vm, vm, vm],
        out_shape=[jax.ShapeDtypeStruct((N_DEV, lv), F32), jax.ShapeDtypeStruct((N_DEV, d), F32),
                   jax.ShapeDtypeStruct((N_DEV, c), F32)],
        scratch_shapes=[pltpu.VMEM((N_DEV, c), F32)] + [pltpu.SemaphoreType.DMA((N_DEV,))] * 4
        + [pltpu.SemaphoreType.DMA((2,))],
        compiler_params=pltpu.CompilerParams(vmem_limit_bytes=VMEM_LIMIT),
    )(vec, w_ada, b_ada_rows)


PER = N_DEV - 1


def _comm_scratch(n):
    return [pltpu.SemaphoreType.DMA((n * PER,)), pltpu.SemaphoreType.DMA((n * PER,)), pltpu.SemaphoreType.DMA((n,))]


def _gather_copies(ins, outs, send_s, recv_s, local_s):
    n = len(ins)
    me = _my_place()
    x, y, c = me
    sibling = (x, y, 1 - c)
    chips = [(1 - x, y), (x, 1 - y), (1 - x, 1 - y)]

    def copy(a, k, block, to, src=None):
        slot = outs[a].at[_index(block)]
        return pltpu.make_async_remote_copy(src_ref=slot if src is None else src, dst_ref=slot,
                                            send_sem=send_s.at[a * PER + k], recv_sem=recv_s.at[a * PER + k],
                                            device_id=to, device_id_type=MESH)

    mine = [pltpu.make_async_copy(ins[a], outs[a].at[_index(me)], local_s.at[a]) for a in range(n)]
    first = []
    for a in range(n):
        first.append(copy(a, 0, me, sibling, src=ins[a]))
        first += [copy(a, 1 + j, me, (*chip, c), src=ins[a]) for j, chip in enumerate(chips)]
    landed = [copy(a, 1 + j, (*chip, c), me) for j, chip in enumerate(chips) for a in range(n)]
    passed = [copy(a, 4 + j, (*chip, c), sibling) for j, chip in enumerate(chips) for a in range(n)]
    from_sibling = [copy(a, 0, sibling, me) for a in range(n)]
    from_sibling += [copy(a, 4 + j, (*chip, 1 - c), me) for a in range(n) for j, chip in enumerate(chips)]
    return mine, first, landed, passed, from_sibling


def _gather_start(*refs):
    mine, first, _, _, _ = _gather_copies(*refs)
    for cp in mine + first:
        cp.start()


def _gather_forward(*refs):
    _, _, landed, passed, _ = _gather_copies(*refs)
    for got, fwd in zip(landed, passed):
        got.wait_recv()
        fwd.start()


def _gather_finish(*refs):
    mine, first, _, passed, from_sibling = _gather_copies(*refs)
    for cp in from_sibling:
        cp.wait_recv()
    for cp in first + passed:
        cp.wait_send()
    for cp in mine:
        cp.wait()


def _exchange_copies(ins, outs, send_s, recv_s, local_s):
    n = len(ins)
    me = _my_place()
    my_i = _index(me)

    def copy(a, k, to, src_slot, dst_slot):
        return pltpu.make_async_remote_copy(src_ref=ins[a].at[src_slot], dst_ref=outs[a].at[dst_slot],
                                            send_sem=send_s.at[a * PER + k - 1], recv_sem=recv_s.at[a * PER + k - 1],
                                            device_id=to, device_id_type=MESH)

    mine = [pltpu.make_async_copy(ins[a].at[my_i], outs[a].at[my_i], local_s.at[a]) for a in range(n)]
    sends = [copy(a, k, _peer(me, k), _index(_peer(me, k)), my_i) for k in range(1, N_DEV) for a in range(n)]
    recvs = [copy(a, k, me, my_i, _index(_peer(me, k))) for k in range(1, N_DEV) for a in range(n)]
    return mine, sends, recvs


def _exchange_start(*refs):
    mine, sends, _ = _exchange_copies(*refs)
    for cp in mine + sends:
        cp.start()


def _exchange_finish(*refs):
    mine, sends, recvs = _exchange_copies(*refs)
    for cp in recvs:
        cp.wait_recv()
    for cp in sends:
        cp.wait_send()
    for cp in mine:
        cp.wait()


def _gathered_shapes(shards):
    return [jax.ShapeDtypeStruct((N_DEV,) + s.shape, s.dtype) for s in shards]


def _proj_layout(cw, ql, kl):
    lay = {"gb": 0, "gc": cw, "ci": 2 * cw, "ql": 3 * cw}
    assert lay["ql"] % ql == 0
    lay["kv"] = _roundup(lay["ql"] + ql, kl)
    lay["kr"] = lay["kv"] + kl
    lay["np"] = _roundup(lay["kr"] + LANE, 4 * LANE)
    return lay


def _chunks_cols(g):
    r, c8 = g.shape
    return jnp.transpose(g.reshape(r, N_DEV, c8 // N_DEV), (1, 0, 2))


def _from_col_shards(a):
    n, r, c = a.shape
    return jnp.transpose(a, (1, 0, 2)).reshape(r, n * c)


def kernel(x, c, positions, w_ada, b_ada, g_pre_mix, g_post_mix, w_in, g_q, w_uq, g_kv, w_ukv, conv_w_mix, conv_b_mix, w_o, g_pre_ffn, g_post_ffn, w_up, conv_w_ffn, conv_b_ffn, w_down, loss_target, m_w_ada, m_b_ada, m_g_pre_mix, m_g_post_mix, m_w_in, m_g_q, m_w_uq, m_g_kv, m_w_ukv, m_conv_w_mix, m_conv_b_mix, m_w_o, m_g_pre_ffn, m_g_post_ffn, m_w_up, m_conv_w_ffn, m_conv_b_ffn, m_w_down, v_w_ada, v_b_ada, v_g_pre_mix, v_g_post_mix, v_w_in, v_g_q, v_w_uq, v_g_kv, v_w_ukv, v_conv_w_mix, v_conv_b_mix, v_w_o, v_g_pre_ffn, v_g_post_ffn, v_w_up, v_conv_w_ffn, v_conv_b_ffn, v_w_down):
    s_len, d = x.shape[1], x.shape[2]
    ql, kl = w_uq.shape[1], w_ukv.shape[1]
    n_heads = w_ukv.shape[2] * N_DEV // (QK_NOPE + V_DIM)
    cw = conv_w_mix.shape[2] * N_DEV
    f2 = w_up.shape[2] * N_DEV
    ff = f2 // 2
    in_cols = w_in.shape[2] * N_DEV
    ada_c = w_ada.shape[2]
    cwm_c, cwf_c = conv_w_mix.shape[2], conv_w_ffn.shape[2]
    scale = 1.0 / math.sqrt(QK_NOPE + QK_ROPE)
    lay = _proj_layout(cw, ql, kl)
    n_pad = lay["np"]
    my_i = _index(_my_place())

    ts_row = _tile(s_len, PREF["row"], SUB)
    ts_conv = _tile(s_len, PREF["conv_rows"], SUB)
    tc_conv = _tile(cw, PREF["conv_cols"])
    tc_ffn = cwf_c
    ts_ffn = _tile(s_len, PREF["ffn_rows"], SUB)
    ts_act = _tile(s_len, PREF["act_rows"], SUB)
    ts_big = _tile(s_len, PREF["row_big"], SUB)
    pair_order = [k // 2 + (k % 2) * (N_DEV // 2) for k in range(N_DEV)]
    pair_place = [pair_order.index(k) for k in range(N_DEV)]

    def paired(shards):
        return _from_col_shards(jnp.stack([shards[p] for p in pair_order]))

    def unpaired_chunks(g):
        ch = _chunks_cols(g)
        return jnp.stack([ch[p] for p in pair_place])
    ts_qkv = _tile(s_len, PREF["row_big"], SUB)
    t_attn = _tile(s_len, PREF["attn"])

    x2d, tgt = x[0], loss_target[0]

    vec = jnp.concatenate([c, conv_w_mix[0].reshape(1, -1), conv_w_ffn[0].reshape(1, -1)], axis=1)
    gath, cact, mod_rows = _ada_fwd(vec, w_ada[0], b_ada.reshape(N_DEV, ada_c))
    cwm_full = _from_col_shards(gath[:, d:d + 3 * cwm_c].reshape(N_DEV, 3, cwm_c))
    cwf_shards = gath[:, d + 3 * cwm_c:].reshape(N_DEV, 3, cwf_c)
    cwf_pair = paired(cwf_shards)
    cbf_pair = paired(jnp.transpose(conv_b_ffn.reshape(1, N_DEV, cwf_c), (1, 0, 2)))
    mod = mod_rows.reshape(1, N_DEV * ada_c)
    sh_m, sc_m, gt_m, sh_f, sc_f, gt_f = [mod[:, k * d:(k + 1) * d] for k in range(6)]

    h1, g_in = _modnorm_fwd(x2d, g_pre_mix, sc_m, sh_m, ts_row, [w_in[0].astype(BF16)])
    win = _from_col_shards(g_in)
    cut = np.cumsum([0, ql, kl, QK_ROPE, cw, cw, cw])
    part = [win[:, cut[k]:cut[k + 1]] for k in range(6)]

    def zcols(n):
        return jnp.zeros((d, n), BF16)

    win_p = jnp.concatenate([part[3], part[4], part[5], part[0], zcols(lay["kv"] - lay["ql"] - ql), part[1],
                             part[2], zcols(n_pad - lay["kr"] - QK_ROPE)], axis=1)
    inv_freq =1.0 / (ROPE_THETA ** (jnp.arange(0, QK_ROPE, 2, dtype=F32) / QK_ROPE))
    inv_row = jnp.tile(inv_freq, LANE // (QK_ROPE // 2)).reshape(1, LANE)
    ctab, atab, btab = _rope_tables(positions.astype(F32).reshape(s_len, 1), inv_row, _tile(s_len, 1024, SUB))

    proj, g_uq, g_ukv = _matmul(h1, win_p, out_dtype=BF16, tm=2048, tn=1280, tk=2048, name="mm_proj",
                                gather=[w_uq[0].astype(BF16), w_ukv[0].astype(BF16)])
    wuq_p = jnp.pad(_from_col_shards(g_uq).reshape(ql, n_heads, QK_NOPE + QK_ROPE),
                    ((0, 0), (0, 0), (0, HEAD_W - QK_NOPE - QK_ROPE))).reshape(ql, n_heads * HEAD_W)
    wukv = _from_col_shards(g_ukv).reshape(kl, n_heads, QK_NOPE + V_DIM)
    wk = wukv[:, :, :QK_NOPE].reshape(kl, n_heads * QK_NOPE)
    wv = wukv[:, :, QK_NOPE:].reshape(kl, n_heads * V_DIM)
    q, k, v, qn, kvn = _qkv_fwd(proj, lay, wuq_p, wk, wv, g_q, g_kv, ctab, atab, btab, n_heads, ts_qkv, scale * LOG2E)
    attn, lse_row, g_o, g_up = _flash_fwd(q, k, v, n_heads, t_attn, [w_o[0].astype(BF16), w_up[0].astype(BF16)], d)
    wo = g_o.reshape(d, d)

    def pair_shard(j):
        return j // 2 + (j % 2) * (N_DEV // 2)
    mixcat = _gconv_fwd(proj, lay, cwm_full, conv_b_mix, ts_conv, tc_conv, attn)
    mix = _matmul(mixcat, wo, out_dtype=BF16, tm=512, tn=2048, tk=2048, name="mm_mix")
    x1, h2 = _post_mix_fwd(mix, x2d, gt_m, g_post_mix, g_pre_ffn, sc_f, sh_f, ts_big)
    up, g_down = _matmul(h2, g_up, out_dtype=BF16, tm=2048, tn=cwf_c, tk=2048, name="mm_up", b_shard_of=pair_shard,
                         gather=[w_down[0].astype(BF16)])
    wdown = g_down.reshape(ff, d)
    u, act = _act_fwd(up, cwf_pair, cbf_pair, ts_act, tc_ffn)
    y = _matmul(act, wdown, out_dtype=BF16, tm=1024, tn=512, tk=ff, name="mm_down")

    dy, dx2, loss_row, d_gt_f, dg_post_ffn = _final_bwd(y, x1, tgt, gt_f, g_post_ffn, ts_big)
    gw_down = _matmul(act, dy, ta=True, out_dtype=BF16, tm=512, tn=512, tk=s_len, name="mm_gw_down")
    d_act = _matmul(dy, wdown, tb=True, out_dtype=BF16, tm=2048, tn=1408, tk=2048, name="mm_d_act")
    d_up, dcw_pair, dcb_pair = _ffn_act_bwd(d_act, u, up, cwf_pair, ts_ffn, tc_ffn)
    dcb_ffn = _from_col_shards(unpaired_chunks(dcb_pair))
    gw_up, p_down = _matmul(h2, d_up, ta=True, out_dtype=BF16, tm=512, tn=cwf_c, tk=4096, name="mm_gw_up",
                            exchange=[gw_down.reshape(N_DEV, ff // N_DEV, d)], out_shard_of=pair_shard)
    dh2 = _matmul_pair_shards(d_up, g_up, out_dtype=BF16, tm=1024, tn=1024, name="mm_dh2")
    dx1, dmix, d_sh_f, d_sc_f, dg_pre_ffn, d_gt_m, dg_post_mix = _mid_bwd(
        dh2, x1, dx2, mix, g_pre_ffn, sc_f, gt_m, g_post_mix, ts_row)
    gw_o = _matmul(mixcat, dmix, ta=True, out_dtype=BF16, tm=512, tn=512, tk=s_len, name="mm_gw_o")
    d_mixcat = _matmul(dmix, wo, tb=True, out_dtype=BF16, tm=1024, tn=1024, tk=2048, name="mm_d_mixcat")
    d_gb, d_gc, d_ci, dcw_mix, dcb_mix = _gconv_bwd(d_mixcat, proj, lay, cwm_full, conv_b_mix, ts_conv, tc_conv)
    delta = _delta(mixcat, d_mixcat, n_heads, _tile(s_len, 512, SUB))
    delta_row = delta[:, :, 0].reshape(n_heads, 1, s_len)
    dq, dk, dv, p_up, p_o = _flash_bwd(q, k, v, d_mixcat, lse_row, delta_row, n_heads, t_attn, scale,
                                       [gw_up, gw_o.reshape(N_DEV, d // N_DEV, d)])
    dq_r, dkn, d_tail, dg_q, dg_kv = _qkv_bwd(dq, dk, dv, proj, lay, wuq_p, wk, wv, g_q, g_kv, ctab, atab, btab,
                                              n_heads, ts_qkv)
    gw_uq_p = _matmul(qn, dq_r, ta=True, out_dtype=BF16, tm=768, tn=512, tk=s_len, name="mm_gw_uq")
    gw_k = _matmul(kvn, dkn, ta=True, out_dtype=BF16, tm=512, tn=512, tk=s_len, name="mm_gw_k")
    gw_v = _matmul(kvn, dv, ta=True, out_dtype=BF16, tm=512, tn=512, tk=s_len, name="mm_gw_v")
    d_proj = jnp.concatenate([d_gb, d_gc, d_ci, d_tail], axis=1)
    gw_uq = gw_uq_p.reshape(ql, n_heads, HEAD_W)[:, :, :QK_NOPE + QK_ROPE].reshape(ql, n_heads * (QK_NOPE + QK_ROPE))
    gw_ukv = jnp.concatenate([gw_k.reshape(kl, n_heads, QK_NOPE), gw_v.reshape(kl, n_heads, V_DIM)],
                             axis=2).reshape(kl, n_heads * (QK_NOPE + V_DIM))
    gw_in_p, p_uq, p_ukv, p_cwm, p_cwf = _matmul(
        h1, d_proj, ta=True, out_dtype=BF16, tm=512, tn=512, tk=s_len, name="mm_gw_in",
        exchange=[_chunks_cols(gw_uq), _chunks_cols(gw_ukv), _chunks_cols(dcw_mix), unpaired_chunks(dcw_pair)])

    gw_in = jnp.concatenate([gw_in_p[:, lay["ql"]:lay["ql"] + ql], gw_in_p[:, lay["kv"]:lay["kv"] + kl],
                             gw_in_p[:, lay["kr"]:lay["kr"] + QK_ROPE], gw_in_p[:, :3 * cw]], axis=1)
    dh1, p_in = _matmul(d_proj, win_p, tb=True, out_dtype=BF16, tm=512, tn=1024, tk=n_pad, name="mm_dh1",
                        exchange=[_chunks_cols(gw_in)])
    grad_x, d_sh_m, d_sc_m, dg_pre_mix = _first_bwd(dh1, x2d, dx1, g_pre_mix, sc_m, ts_big)

    dmod = jnp.concatenate([d_sh_m, d_sc_m, d_gt_m, d_sh_f, d_sc_f, d_gt_f], axis=1)
    small_g = [loss_row, dmod, dg_pre_mix, dg_post_mix, dg_q, dg_kv, dcb_mix, dg_pre_ffn, dg_post_ffn, dcb_ffn]
    adam_w_up, v_all = _adam_parts(p_up, w_up[0], m_w_up[0], v_w_up[0], 256, "adam_w_up",
                                   small=jnp.concatenate(small_g, axis=1))

    small_names = ["b_ada", "g_pre_mix", "g_post_mix", "g_q", "g_kv", "conv_b_mix", "g_pre_ffn", "g_post_ffn",
                   "conv_b_ffn"]
    small_w = [b_ada, g_pre_mix, g_post_mix, g_q, g_kv, conv_b_mix, g_pre_ffn, g_post_ffn, conv_b_ffn]
    small_m = [m_b_ada, m_g_pre_mix, m_g_post_mix, m_g_q, m_g_kv, m_conv_b_mix, m_g_pre_ffn, m_g_post_ffn, m_conv_b_ffn]
    small_v = [v_b_ada, v_g_pre_mix, v_g_post_mix, v_g_q, v_g_kv, v_conv_b_mix, v_g_pre_ffn, v_g_post_ffn, v_conv_b_ffn]
    offs = np.cumsum([0] + [g.shape[1] for g in small_g])
    g_sum, small_out = _adam_small(v_all, [int(o) for o in offs[1:-1]], small_w, small_m, small_v)
    small = [dict(zip(small_names, kind)) for kind in small_out]
    loss = g_sum[0, 0]

    dmod_sh = lax.dynamic_slice(v_all, (0, int(offs[1]) + my_i * ada_c), (N_DEV, ada_c))
    cact_t = jnp.transpose(cact)
    big = dict(
        w_ada=_adam_ada(cact_t, dmod_sh, w_ada[0], m_w_ada[0], v_w_ada[0], _tile(d, 256, SUB)),
        w_in=_adam_parts(p_in, w_in[0], m_w_in[0], v_w_in[0], 256, "adam_w_in"),
        w_uq=_adam_parts(p_uq, w_uq[0], m_w_uq[0], v_w_uq[0], 256, "adam_w_uq"),
        w_ukv=_adam_parts(p_ukv, w_ukv[0], m_w_ukv[0], v_w_ukv[0], 256, "adam_w_ukv"),
        w_o=_adam_parts(p_o, w_o[0], m_w_o[0], v_w_o[0], 128, "adam_w_o"),
        w_up=adam_w_up,
        w_down=_adam_parts(p_down, w_down[0], m_w_down[0], v_w_down[0], 176, "adam_w_down"),
        conv_w_mix=_adam_parts(p_cwm, conv_w_mix[0], m_conv_w_mix[0], v_conv_w_mix[0], 8, "adam_cw_mix"),
        conv_w_ffn=_adam_parts(p_cwf, conv_w_ffn[0], m_conv_w_ffn[0], v_conv_w_ffn[0], 8, "adam_cw_ffn"),
    )

    names = ["w_ada", "b_ada", "g_pre_mix", "g_post_mix", "w_in", "g_q", "w_uq", "g_kv", "w_ukv", "conv_w_mix",
             "conv_b_mix", "w_o", "g_pre_ffn", "g_post_ffn", "w_up", "conv_w_ffn", "conv_b_ffn", "w_down"]
    outs = [loss, grad_x[None]]
    for kind in range(4):
        for nm in names:
            outs.append(big[nm][kind][None] if nm in big else small[kind][nm])
    return tuple(outs)
```

```python
import math

import numpy as np
import jax
import jax.numpy as jnp
from jax import lax
from jax.experimental import pallas as pl
from jax.experimental.pallas import tpu as pltpu

F32 = jnp.float32
BF16 = jnp.bfloat16
N_DEV = 8
MESH = pl.DeviceIdType.MESH

QK_NOPE = 128
QK_ROPE = 64
V_DIM = 128
HEAD_W = 256
LANE = 128
SUB = 8
HALO = 16
STRIP_ROWS = 64
RMS_EPS = 1e-6
ROPE_THETA = 10000.0
ADAM_LR = 0.001
ADAM_B1 = 0.9
ADAM_B2 = 0.999
ADAM_EPS = 1e-08
ADAM_WD = 0.01
ADAM_STEP = 10
NEG = -1e30
LOG2E = 1.4426950408889634
LN2 = 0.6931471805599453
VMEM_LIMIT = 56 * 1024 * 1024

PREF = {"row": 256, "row_big": 512, "act_rows": 1024, "conv_rows": 1024, "conv_cols": 512, "ffn_rows": 1024, "attn": 1024, "attn_chunk": 1024, "attn_heads": 8}

NT_DIMS = (((1,), (1,)), ((), ()))
TN_DIMS = (((0,), (0,)), ((), ()))


def _cparams(*sem):
    return pltpu.CompilerParams(dimension_semantics=sem if sem else None, vmem_limit_bytes=VMEM_LIMIT)


def _tile(n, pref, unit=LANE):
    if n <= pref:
        return n
    t = (pref // unit) * unit
    while t >= unit:
        if n % t == 0:
            return t
        t -= unit
    return n


def _roundup(n, m):
    return (n + m - 1) // m * m


def _rsq(x):
    return lax.rsqrt(jnp.mean(x * x, axis=-1, keepdims=True) + RMS_EPS)


def _colsum(x):
    return jnp.sum(x, axis=0, keepdims=True)


def _sigmoid(x):
    return 1.0 / (1.0 + jnp.exp(-x))


def _matmul(a, b, *, ta=False, tb=False, out_dtype, tm, tn, tk, name, exchange=None, gather=None, b_shard_of=None,
            out_shard_of=None):
    m_dim, k_dim = (a.shape[1], a.shape[0]) if ta else a.shape
    if b_shard_of is not None:
        n_dim, tn = b.shape[0] * b.shape[2], b.shape[2]
    else:
        n_dim = b.shape[0] if tb else b.shape[1]
    chunk_w = n_dim // N_DEV
    if out_shard_of is not None:
        tn = _tile(chunk_w, tn)
    tm, tn, tk = _tile(m_dim, tm), _tile(n_dim, tn), _tile(k_dim, tk)
    per_chunk = chunk_w // tn if out_shard_of is not None else 1
    gi, gj, nk = m_dim // tm, n_dim // tn, k_dim // tk
    dims = (((0 if ta else 1,), (1 if tb else 0,)), ((), ()))
    chunks = list(exchange or gather or [])
    nx = len(chunks)
    comm_start, comm_finish = (_gather_start, _gather_finish) if gather else (_exchange_start, _exchange_finish)

    def body(*refs):
        a_ref, b_ref = refs[:2]
        xin, o_ref, xout = refs[2:2 + nx], refs[2 + nx], refs[3 + nx:3 + 2 * nx]
        scratch = refs[3 + 2 * nx:]
        sems = scratch[1:] if nk > 1 else scratch
        i, j, k = pl.program_id(0), pl.program_id(1), pl.program_id(2)
        if nx:
            @pl.when((i == 0) & (j == 0) & (k == 0))
            def _():
                comm_start(xin, xout, *sems)

        if gather:
            @pl.when((i == gi // 2) & (j == 0) & (k == 0))
            def _():
                _gather_forward(xin, xout, *sems)

        part = lax.dot_general(a_ref[...], b_ref[...], dims, preferred_element_type=F32)
        if nk == 1:
            o_ref[...] = part.astype(o_ref.dtype)
        else:
            acc_ref = scratch[0]

            @pl.when(k == 0)
            def _():
                acc_ref[...] = part

            @pl.when(k > 0)
            def _():
                acc_ref[...] += part

            @pl.when(k == nk - 1)
            def _():
                o_ref[...] = acc_ref[...].astype(o_ref.dtype)

        if nx:
            @pl.when((i == gi - 1) & (j == gj - 1) & (k == nk - 1))
            def _():
                comm_finish(xin, xout, *sems)

    a_spec = pl.BlockSpec((tk, tm), lambda i, j, k: (k, i)) if ta else pl.BlockSpec((tm, tk), lambda i, j, k: (i, k))
    b_spec = pl.BlockSpec((tn, tk), lambda i, j, k: (j, k)) if tb else pl.BlockSpec((tk, tn), lambda i, j, k: (k, j))
    if b_shard_of is not None:
        b_spec = pl.BlockSpec((None, tk, tn), lambda i, j, k: (b_shard_of(j), k, 0))
    o_spec, o_shape = pl.BlockSpec((tm, tn), lambda i, j, k: (i, j)), (m_dim, n_dim)
    if out_shard_of is not None:
        o_spec = pl.BlockSpec((None, tm, tn), lambda i, j, k: (out_shard_of(j // per_chunk), i, j % per_chunk))
        o_shape = (N_DEV, m_dim, chunk_w)
    hbm = pl.BlockSpec(memory_space=pl.ANY)
    out = pl.pallas_call(
        body,
        name=name,
        grid=(gi, gj, nk),
        in_specs=[a_spec, b_spec] + [hbm] * nx,
        out_specs=[o_spec] + [hbm] * nx,
        out_shape=[jax.ShapeDtypeStruct(o_shape, out_dtype)]
        + (_gathered_shapes(chunks) if gather else [jax.ShapeDtypeStruct(c.shape, c.dtype) for c in chunks]),
        scratch_shapes=([pltpu.VMEM((tm, tn), F32)] if nk > 1 else []) + (_comm_scratch(nx) if nx else []),
        compiler_params=_cparams(*(("arbitrary",) * 3 if nx else ("parallel", "parallel", "arbitrary"))),
    )(a, b, *chunks)
    return out if nx else out[0]


def _matmul_pair_shards(a, shards, *, out_dtype, tm, tn, name):
    m_dim = a.shape[0]
    n_sh, n_dim, c = shards.shape
    half = n_sh // 2
    tm, tn = _tile(m_dim, tm), _tile(n_dim, tn)

    def body(a_ref, b0_ref, b1_ref, o_ref, acc_ref):
        k = pl.program_id(2)
        part = (lax.dot_general(a_ref[:, :c], b0_ref[...], NT_DIMS, preferred_element_type=F32)
                + lax.dot_general(a_ref[:, c:], b1_ref[...], NT_DIMS, preferred_element_type=F32))

        @pl.when(k == 0)
        def _():
            acc_ref[...] = part

        @pl.when(k > 0)
        def _():
            acc_ref[...] += part

        @pl.when(k == half - 1)
        def _():
            o_ref[...] = acc_ref[...].astype(o_ref.dtype)

    return pl.pallas_call(
        body, name=name, grid=(m_dim // tm, n_dim // tn, half),
        in_specs=[pl.BlockSpec((tm, 2 * c), lambda i, j, k: (i, k)),
                  pl.BlockSpec((None, tn, c), lambda i, j, k: (k, j, 0)),
                  pl.BlockSpec((None, tn, c), lambda i, j, k: (k + half, j, 0))],
        out_specs=pl.BlockSpec((tm, tn), lambda i, j, k: (i, j)),
        out_shape=jax.ShapeDtypeStruct((m_dim, n_dim), out_dtype),
        scratch_shapes=[pltpu.VMEM((tm, tn), F32)],
        compiler_params=_cparams("parallel", "parallel", "arbitrary"),
    )(a, shards, shards)


def _shift_down(x, halo, n):
    r = pltpu.roll(x, n, 0)
    hr = pltpu.roll(halo, n, 0)
    row = lax.broadcasted_iota(jnp.int32, halo.shape, 0)
    top = jnp.where(row < n, hr, r[:SUB])
    return jnp.concatenate([top, r[SUB:]], axis=0)


def _shift_up(x, halo, n):
    ts = x.shape[0]
    r = pltpu.roll(x, ts - n, 0)
    hr = pltpu.roll(halo, SUB - n, 0)
    row = lax.broadcasted_iota(jnp.int32, halo.shape, 0)
    bot = jnp.where(row >= SUB - n, hr, r[ts - SUB:])
    return jnp.concatenate([r[:ts - SUB], bot], axis=0)


def _conv3(x, halo, w_ref, b_ref):
    return _shift_down(x, halo, 2) * w_ref[0:1, :] + _shift_down(x, halo, 1) * w_ref[1:2, :] + x * w_ref[2:3, :] + b_ref[...]


def _prev_halo(ts, col):
    return lambda j, i: (jnp.maximum(i * (ts // SUB) - 1, 0), col(j))


def _next_halo(ts, n_rows, col):
    return lambda j, i: (jnp.minimum((i + 1) * (ts // SUB), n_rows // SUB - 1), col(j))


def _modnorm_fwd(x, g, sc, sh, ts, gather):
    s_len, d = x.shape
    n_i = s_len // ts
    ng = len(gather)

    def body(x_ref, g_ref, sc_ref, sh_ref, *rest):
        gin, h_ref, gout, sems = rest[:ng], rest[ng], rest[ng + 1:2 * ng + 1], rest[2 * ng + 1:]
        i = pl.program_id(0)

        @pl.when(i == 0)
        def _():
            _gather_start(gin, gout, *sems)

        xv = x_ref[...]
        h_ref[...] = ((xv * _rsq(xv) * g_ref[...]) * (1.0 + sc_ref[...]) + sh_ref[...]).astype(BF16)

        @pl.when(i == n_i - 1)
        def _():
            _gather_forward(gin, gout, *sems)
            _gather_finish(gin, gout, *sems)

    vec = pl.BlockSpec((1, d), lambda i: (0, 0))
    hbm = pl.BlockSpec(memory_space=pl.ANY)
    return pl.pallas_call(
        body, name="modnorm_fwd", grid=(n_i,),
        in_specs=[pl.BlockSpec((ts, d), lambda i: (i, 0)), vec, vec, vec] + [hbm] * ng,
        out_specs=[pl.BlockSpec((ts, d), lambda i: (i, 0))] + [hbm] * ng,
        out_shape=[jax.ShapeDtypeStruct((s_len, d), BF16)] + _gathered_shapes(gather),
        scratch_shapes=_comm_scratch(ng),
        compiler_params=_cparams("arbitrary"),
    )(x, g, sc, sh, *gather)


def _rope_tables(pos_col, inv_freq_row, ts):
    s_len = pos_col.shape[0]
    half = QK_ROPE // 2

    def body(p_ref, f_ref, c_ref, a_ref, b_ref):
        ang = p_ref[...] * f_ref[...]
        lane = lax.broadcasted_iota(jnp.int32, ang.shape, 1)
        cos, sin = jnp.cos(ang), jnp.sin(ang)
        c_ref[...] = jnp.where(lane < 2 * half, cos, 0.0)
        a_ref[...] = jnp.where(lane < half, -sin, 0.0)
        b_ref[...] = jnp.where((lane >= half) & (lane < 2 * half), sin, 0.0)

    out = jax.ShapeDtypeStruct((s_len, LANE), F32)
    blk = pl.BlockSpec((ts, LANE), lambda i: (i, 0))
    return pl.pallas_call(
        body, name="rope_tables", grid=(s_len // ts,),
        in_specs=[pl.BlockSpec((ts, 1), lambda i: (i, 0)), pl.BlockSpec((1, LANE), lambda i: (0, 0))],
        out_specs=[blk, blk, blk], out_shape=[out, out, out],
        compiler_params=_cparams("parallel"),
    )(pos_col, inv_freq_row)


def _rope(seg, c, a, b):
    return seg * c + pltpu.roll(seg, LANE - QK_ROPE // 2, 1) * a + pltpu.roll(seg, QK_ROPE // 2, 1) * b


def _rope_t(seg, c, a, b):
    return seg * c - pltpu.roll(seg, LANE - QK_ROPE // 2, 1) * a - pltpu.roll(seg, QK_ROPE // 2, 1) * b


def _qkv_fwd(proj, lay, wuq, wk, wv, g_q, g_kv, ctab, atab, btab, n_heads, ts, scale):
    s_len = proj.shape[0]
    ql_w, kl_w = wuq.shape[0], wk.shape[0]

    def body(ql_ref, kl_ref, kr_ref, wuq_ref, wk_ref, wv_ref, gq_ref, gkv_ref, c_ref, a_ref, b_ref,
             q_out, k_out, v_out, qn_out, kvn_out):
        c, a, b = c_ref[...], a_ref[...], b_ref[...]
        ql = ql_ref[...].astype(F32)
        qn = (ql * _rsq(ql) * gq_ref[...]).astype(BF16)
        qn_out[...] = qn
        q = jnp.dot(qn, wuq_ref[...], preferred_element_type=F32)
        kl = kl_ref[...].astype(F32)
        kvn = (kl * _rsq(kl) * gkv_ref[...]).astype(BF16)
        kvn_out[...] = kvn
        kn = jnp.dot(kvn, wk_ref[...], preferred_element_type=F32)
        v_out[...] = jnp.dot(kvn, wv_ref[...], preferred_element_type=F32).astype(BF16)
        kr = _rope(kr_ref[...].astype(F32), c, a, b).astype(BF16)
        for h in range(n_heads):
            o = h * HEAD_W
            q_out[:, o:o + QK_NOPE] = (q[:, o:o + QK_NOPE] * scale).astype(BF16)
            q_out[:, o + QK_NOPE:o + HEAD_W] = (_rope(q[:, o + QK_NOPE:o + HEAD_W], c, a, b) * scale).astype(BF16)
            k_out[:, o:o + QK_NOPE] = kn[:, h * QK_NOPE:(h + 1) * QK_NOPE].astype(BF16)
            k_out[:, o + QK_NOPE:o + HEAD_W] = kr

    def full(arr):
        return pl.BlockSpec(arr.shape, lambda i: (0, 0))

    tab = pl.BlockSpec((ts, LANE), lambda i: (i, 0))
    hw, hv = n_heads * HEAD_W, n_heads * V_DIM
    return pl.pallas_call(
        body, name="qkv_fwd", grid=(s_len // ts,),
        in_specs=[pl.BlockSpec((ts, ql_w), lambda i: (i, lay["ql"] // ql_w)),
                  pl.BlockSpec((ts, kl_w), lambda i: (i, lay["kv"] // kl_w)),
                  pl.BlockSpec((ts, LANE), lambda i: (i, lay["kr"] // LANE)),
                  full(wuq), full(wk), full(wv), full(g_q), full(g_kv), tab, tab, tab],
        out_specs=[pl.BlockSpec((ts, hw), lambda i: (i, 0)), pl.BlockSpec((ts, hw), lambda i: (i, 0)),
                   pl.BlockSpec((ts, hv), lambda i: (i, 0)), pl.BlockSpec((ts, ql_w), lambda i: (i, 0)),
                   pl.BlockSpec((ts, kl_w), lambda i: (i, 0))],
        out_shape=[jax.ShapeDtypeStruct((s_len, hw), BF16), jax.ShapeDtypeStruct((s_len, hw), BF16),
                   jax.ShapeDtypeStruct((s_len, hv), BF16), jax.ShapeDtypeStruct((s_len, ql_w), BF16),
                   jax.ShapeDtypeStruct((s_len, kl_w), BF16)],
        compiler_params=_cparams("parallel"),
    )(proj, proj, proj, wuq, wk, wv, g_q, g_kv, ctab, atab, btab)


def _flash_fwd(q, k, v, n_heads, t, gather, out_cols):
    ng = len(gather)
    ck = _tile(t, PREF["attn_chunk"])
    hp = PREF["attn_heads"] if n_heads % PREF["attn_heads"] == 0 else 1
    n_groups = n_heads // hp
    s_len = q.shape[0]
    nb = s_len // t
    pairs = [(i, j) for i in range(nb) for j in range(i + 1)]
    itab = jnp.asarray(np.array([p[0] for p in pairs], np.int32))
    jtab = jnp.asarray(np.array([p[1] for p in pairs], np.int32))

    n_steps = len(pairs)
    fwd_at = max(1, (7 * n_groups * n_steps) // 10)

    def body(it_ref, jt_ref, q_ref, k_ref, v_ref, *rest):
        gin, (o_ref, lse_ref), gout = rest[:ng], rest[ng:ng + 2], rest[ng + 2:2 * ng + 2]
        m_sc, l_sc, acc_sc = rest[2 * ng + 2:2 * ng + 5]
        sems = rest[2 * ng + 5:]
        group, step_id = pl.program_id(0), pl.program_id(1)
        i, j = it_ref[step_id], jt_ref[step_id]

        @pl.when((group == 0) & (step_id == 0))
        def _():
            _gather_start(gin, gout, *sems)

        @pl.when((group == fwd_at // n_steps) & (step_id == fwd_at % n_steps))
        def _():
            _gather_forward(gin, gout, *sems)

        @pl.when(j == 0)
        def _():
            m_sc[...] = jnp.full(m_sc.shape, NEG, F32)
            l_sc[...] = jnp.zeros(l_sc.shape, F32)
            acc_sc[...] = jnp.zeros(acc_sc.shape, F32)

        def step(diag):
            for h in range(hp):
                qk, vc = slice(h * HEAD_W, (h + 1) * HEAD_W), slice(h * V_DIM, (h + 1) * V_DIM)
                cd = min(ck, t // 2) if diag else ck
                for c in range(t // cd):
                    q0 = c * cd if diag else 0
                    qs, ks = slice(q0, t), slice(c * cd, (c + 1) * cd)
                    s_t = lax.dot_general(k_ref[ks, qk], q_ref[qs, qk], NT_DIMS, preferred_element_type=F32)
                    if diag:
                        krow = lax.broadcasted_iota(jnp.int32, s_t.shape, 0)
                        qcol = lax.broadcasted_iota(jnp.int32, s_t.shape, 1)
                        s_t = jnp.where(krow <= qcol, s_t, NEG)
                    m_prev = m_sc[h, :, qs]
                    m_new = jnp.maximum(m_prev, jnp.max(s_t, axis=0, keepdims=True))
                    alpha = jnp.exp2(m_prev - m_new)
                    p_t = jnp.exp2(s_t - m_new)
                    l_sc[h, :, qs] = alpha * l_sc[h, :, qs] + jnp.sum(p_t, axis=0, keepdims=True)
                    acc_sc[h, :, qs] = acc_sc[h, :, qs] * alpha + lax.dot_general(
                        v_ref[ks, vc], p_t.astype(BF16), TN_DIMS, preferred_element_type=F32)
                    m_sc[h, :, qs] = m_new

        @pl.when(j < i)
        def _():
            step(False)

        @pl.when(j == i)
        def _():
            step(True)
            for h in range(hp):
                l = l_sc[h]
                o_ref[:, h * V_DIM:(h + 1) * V_DIM] = jnp.transpose(acc_sc[h] / l).astype(BF16)
                lse_ref[h] = m_sc[h] + jnp.log(l) * LOG2E

        @pl.when((group == n_groups - 1) & (step_id == n_steps - 1))
        def _():
            _gather_finish(gin, gout, *sems)

    hbm = pl.BlockSpec(memory_space=pl.ANY)
    grid_spec = pltpu.PrefetchScalarGridSpec(
        num_scalar_prefetch=2, grid=(n_groups, n_steps),
        in_specs=[pl.BlockSpec((t, hp * HEAD_W), lambda g, s, it, jt: (it[s], g)),
                  pl.BlockSpec((t, hp * HEAD_W), lambda g, s, it, jt: (jt[s], g)),
                  pl.BlockSpec((t, hp * V_DIM), lambda g, s, it, jt: (jt[s], g))] + [hbm] * ng,
        out_specs=[pl.BlockSpec((t, hp * V_DIM), lambda g, s, it, jt: (it[s], g)),
                   pl.BlockSpec((hp, 1, t), lambda g, s, it, jt: (g, 0, it[s]))] + [hbm] * ng,
        scratch_shapes=[pltpu.VMEM((hp, 1, t), F32), pltpu.VMEM((hp, 1, t), F32), pltpu.VMEM((hp, V_DIM, t), F32)]
        + _comm_scratch(ng),
    )
    return pl.pallas_call(
        body, name="flash_fwd", grid_spec=grid_spec,
        out_shape=[jax.ShapeDtypeStruct((s_len, out_cols), BF16),
                   jax.ShapeDtypeStruct((n_heads, 1, s_len), F32)] + _gathered_shapes(gather),
        compiler_params=_cparams("arbitrary", "arbitrary"),
    )(itab, jtab, q, k, v, *gather)


def _gconv_fwd(proj, lay, w, b, ts, tc, mixcat):
    s_len = proj.shape[0]
    cw = w.shape[1]
    nj = cw // tc
    out_off = mixcat.shape[1] - cw

    def body(gb_ref, gc_ref, ci_ref, gch_ref, cih_ref, w_ref, b_ref, mix_in, o_ref):
        i = pl.program_id(1)
        p = gc_ref[...].astype(F32) * ci_ref[...].astype(F32)
        ph = jnp.where(i > 0, gch_ref[...].astype(F32) * cih_ref[...].astype(F32), 0.0)
        o_ref[...] = (gb_ref[...].astype(F32) * _conv3(p, ph, w_ref, b_ref)).astype(BF16)

    def blk(off):
        return pl.BlockSpec((ts, tc), lambda j, i: (i, off // tc + j))

    def halo(off):
        return pl.BlockSpec((SUB, tc), _prev_halo(ts, lambda j: off // tc + j))

    return pl.pallas_call(
        body, name="gconv_fwd", grid=(nj, s_len // ts),
        in_specs=[blk(lay["gb"]), blk(lay["gc"]), blk(lay["ci"]), halo(lay["gc"]), halo(lay["ci"]),
                  pl.BlockSpec((3, tc), lambda j, i: (0, j)), pl.BlockSpec((1, tc), lambda j, i: (0, j)),
                  pl.BlockSpec(memory_space=pl.ANY)],
        out_specs=pl.BlockSpec((ts, tc), lambda j, i: (i, out_off // tc + j)),
        out_shape=jax.ShapeDtypeStruct(mixcat.shape, BF16),
        input_output_aliases={7: 0},
        compiler_params=_cparams("parallel", "parallel"),
    )(proj, proj, proj, proj, proj, w, b, mixcat)


def _post_mix_fwd(mix, x, gt, g_post, g_pre, sc, sh, ts):
    s_len, d = x.shape

    def body(mix_ref, x_ref, gt_ref, gp_ref, g2_ref, sc_ref, sh_ref, x1_ref, h2_ref):
        mv = mix_ref[...].astype(F32)
        x1 = x_ref[...] + gt_ref[...] * (mv * _rsq(mv) * gp_ref[...])
        x1_ref[...] = x1
        h2_ref[...] = ((x1 * _rsq(x1) * g2_ref[...]) * (1.0 + sc_ref[...]) + sh_ref[...]).astype(BF16)

    vec = pl.BlockSpec((1, d), lambda i: (0, 0))
    row = pl.BlockSpec((ts, d), lambda i: (i, 0))
    return pl.pallas_call(
        body, name="post_mix_fwd", grid=(s_len // ts,),
        in_specs=[row, row, vec, vec, vec, vec, vec], out_specs=[row, row],
        out_shape=[jax.ShapeDtypeStruct((s_len, d), F32), jax.ShapeDtypeStruct((s_len, d), BF16)],
        compiler_params=_cparams("parallel"),
    )(mix, x, gt, g_post, g_pre, sc, sh)


def _act_fwd(up, w, b, ts, tc):
    s_len, f2 = up.shape
    nh = f2 // 2 // tc
    rs_n = _tile(ts, STRIP_ROWS, HALO)

    def body(up_ref, uph_ref, w_ref, b_ref, u_ref, o_ref):
        i = pl.program_id(1)

        def conv_strip(r0, lanes):
            if r0 == 0:
                top = jnp.where(i > 0, uph_ref[:, lanes].astype(F32), 0.0)
                xe = jnp.concatenate([top, up_ref[0:rs_n, lanes].astype(F32)], axis=0)
            else:
                xe = up_ref[r0 - HALO:r0 + rs_n, lanes].astype(F32)
            u = (pltpu.roll(xe, 2, 0)[HALO:] * w_ref[0:1, lanes] + pltpu.roll(xe, 1, 0)[HALO:] * w_ref[1:2, lanes]
                 + xe[HALO:] * w_ref[2:3, lanes] + b_ref[:, lanes])
            u_ref[r0:r0 + rs_n, lanes] = u.astype(BF16)
            return u

        for r0 in range(0, ts, rs_n):
            for c0 in range(0, tc, LANE):
                ua = conv_strip(r0, slice(c0, c0 + LANE))
                ug = conv_strip(r0, slice(tc + c0, tc + c0 + LANE))
                o_ref[r0:r0 + rs_n, c0:c0 + LANE] = (ug * _sigmoid(ug) * ua).astype(BF16)

    pair = pl.BlockSpec((ts, 2 * tc), lambda j, i: (i, j))
    return pl.pallas_call(
        body, name="act_fwd", grid=(nh, s_len // ts),
        in_specs=[pair, pl.BlockSpec((HALO, 2 * tc), lambda j, i: (jnp.maximum(i * (ts // HALO) - 1, 0), j)),
                  pl.BlockSpec((3, 2 * tc), lambda j, i: (0, j)), pl.BlockSpec((1, 2 * tc), lambda j, i: (0, j))],
        out_specs=[pair, pl.BlockSpec((ts, tc), lambda j, i: (i, j))],
        out_shape=[jax.ShapeDtypeStruct((s_len, f2), BF16), jax.ShapeDtypeStruct((s_len, f2 // 2), BF16)],
        compiler_params=_cparams("parallel", "parallel"),
    )(up, up, w, b)


def _final_bwd(y, x1, tgt, gt, g_post, ts):
    s_len, d = y.shape

    n_i = s_len // ts

    def body(y_ref, x1_ref, t_ref, gt_ref, g_ref, dy_ref, dx2_ref, loss_ref, dgt_ref, dg_ref):
        i = pl.program_id(0)
        gtg = gt_ref[...] * g_ref[...]
        yv = y_ref[...].astype(F32)
        r = _rsq(yv)
        yh = yv * r
        e = x1_ref[...] + yh * gtg - t_ref[...]
        loss = 0.5 * jnp.sum(jnp.mean(e * e, axis=-1, keepdims=True), axis=0, keepdims=True)
        dx2 = e * (1.0 / d)
        dx2_ref[...] = dx2
        dyh = dx2 * gtg
        dy_ref[...] = (r * (dyh - yh * jnp.mean(dyh * yh, axis=-1, keepdims=True))).astype(BF16)

        @pl.when(i == 0)
        def _():
            loss_ref[...] = jnp.zeros(loss_ref.shape, F32)
            dgt_ref[...] = jnp.zeros(dgt_ref.shape, F32)

        loss_ref[...] += jnp.broadcast_to(loss, loss_ref.shape)
        dgt_ref[...] += _colsum(dx2 * yh)

        @pl.when(i == n_i - 1)
        def _():
            both = dgt_ref[...]
            dg_ref[...] = both * gt_ref[...]
            dgt_ref[...] = both * g_ref[...]

    vec = pl.BlockSpec((1, d), lambda i: (0, 0))
    row = pl.BlockSpec((ts, d), lambda i: (i, 0))
    vshape = jax.ShapeDtypeStruct((1, d), F32)
    return pl.pallas_call(
        body, name="final_bwd", grid=(s_len // ts,),
        in_specs=[row, row, row, vec, vec],
        out_specs=[row, row, pl.BlockSpec((1, LANE), lambda i: (0, 0)), vec, vec],
        out_shape=[jax.ShapeDtypeStruct((s_len, d), BF16), jax.ShapeDtypeStruct((s_len, d), F32),
                   jax.ShapeDtypeStruct((1, LANE), F32), vshape, vshape],
        compiler_params=_cparams("arbitrary"),
    )(y, x1, tgt, gt, g_post)


def _ffn_act_bwd(d_act, u, up, w, ts, tc):
    s_len, f2 = up.shape
    nh = f2 // 2 // tc
    n_i = s_len // ts
    rs_n = _tile(ts, STRIP_ROWS, HALO)
    ext = rs_n + HALO

    def body(d_ref, dh_ref, u_ref, uh_ref, x_ref, w_ref, dx_ref, dw_ref, db_ref, acc):
        i = pl.program_id(1)
        acc[...] = jnp.zeros(acc.shape, F32)

        def below(ref, halo_ref, r0, lanes):
            if r0 + ext <= ts:
                return ref[r0:r0 + ext, lanes].astype(F32)
            bot = jnp.where(i < n_i - 1, halo_ref[:, lanes].astype(F32), 0.0)
            return jnp.concatenate([ref[r0:ts, lanes].astype(F32), bot], axis=0)

        def fold8(v):
            return jnp.sum(v.reshape(v.shape[0] // SUB, SUB, v.shape[1]), axis=0)

        def conv_bwd_strip(du, r0, lanes):
            du1, du2 = pltpu.roll(du, ext - 1, 0)[:rs_n], pltpu.roll(du, ext - 2, 0)[:rs_n]
            du0 = du[:rs_n]
            dx_ref[r0:r0 + rs_n, lanes] = (du0 * w_ref[2:3, lanes] + du1 * w_ref[1:2, lanes]
                                           + du2 * w_ref[0:1, lanes]).astype(BF16)
            xv = x_ref[r0:r0 + rs_n, lanes].astype(F32)
            acc[0, :, lanes] += fold8(du2 * xv)
            acc[1, :, lanes] += fold8(du1 * xv)
            acc[2, :, lanes] += fold8(du0 * xv)
            acc[3, :, lanes] += fold8(du0)

        for r0 in range(0, ts, rs_n):
            for c0 in range(0, tc, LANE):
                la, lg = slice(c0, c0 + LANE), slice(tc + c0, tc + c0 + LANE)
                dv = below(d_ref, dh_ref, r0, la)
                ua, ug = below(u_ref, uh_ref, r0, la), below(u_ref, uh_ref, r0, lg)
                sg = _sigmoid(ug)
                conv_bwd_strip(dv * (ug * sg), r0, la)
                conv_bwd_strip(dv * ua * (sg * (1.0 + ug * (1.0 - sg))), r0, lg)

        @pl.when(i == 0)
        def _():
            dw_ref[...] = jnp.zeros(dw_ref.shape, F32)
            db_ref[...] = jnp.zeros(db_ref.shape, F32)

        dw_ref[...] += jnp.concatenate([_colsum(acc[k]) for k in range(3)], axis=0)
        db_ref[...] += _colsum(acc[3])

    pair = pl.BlockSpec((ts, 2 * tc), lambda j, i: (i, j))

    def nxt(j, i):
        return (jnp.minimum((i + 1) * (ts // HALO), s_len // HALO - 1), j)

    return pl.pallas_call(
        body, name="ffn_act_bwd", grid=(nh, n_i),
        in_specs=[pl.BlockSpec((ts, tc), lambda j, i: (i, j)), pl.BlockSpec((HALO, tc), nxt),
                  pair, pl.BlockSpec((HALO, 2 * tc), nxt), pair, pl.BlockSpec((3, 2 * tc), lambda j, i: (0, j))],
        out_specs=[pair, pl.BlockSpec((3, 2 * tc), lambda j, i: (0, j)), pl.BlockSpec((1, 2 * tc), lambda j, i: (0, j))],
        out_shape=[jax.ShapeDtypeStruct((s_len, f2), BF16), jax.ShapeDtypeStruct((3, f2), F32),
                   jax.ShapeDtypeStruct((1, f2), F32)],
        scratch_shapes=[pltpu.VMEM((4, SUB, 2 * tc), F32)],
        compiler_params=_cparams("parallel", "arbitrary"),
    )(d_act, d_act, u, u, up, w)


def _mid_bwd(dh2, x1, dx2, mix, g_pre, sc, gt_m, g_post, ts):
    s_len, d = x1.shape
    n_i = s_len // ts

    def body(dh_ref, x1_ref, dx2_ref, mix_ref, g_ref, sc_ref, gt_ref, gp_ref,
             dx1_ref, dmix_ref, dsh_ref, dsc_ref, dg_ref, dgt_ref, dgp_ref):
        i = pl.program_id(0)
        wv = (1.0 + sc_ref[...]) * g_ref[...]
        gtg = gt_ref[...] * gp_ref[...]
        dh = dh_ref[...].astype(F32)
        x1 = x1_ref[...]
        r1 = _rsq(x1)
        xh = x1 * r1
        dhx = dh * xh
        dx1 = dx2_ref[...] + r1 * (dh * wv - xh * jnp.mean(dhx * wv, axis=-1, keepdims=True))
        dx1_ref[...] = dx1
        mv = mix_ref[...].astype(F32)
        rm = _rsq(mv)
        mh = mv * rm
        dxm = dx1 * mh
        dmix_ref[...] = (rm * (dx1 * gtg - mh * jnp.mean(dxm * gtg, axis=-1, keepdims=True))).astype(BF16)

        @pl.when(i == 0)
        def _():
            for ref in (dsh_ref, dsc_ref, dgt_ref):
                ref[...] = jnp.zeros(ref.shape, F32)

        dsh_ref[...] += _colsum(dh)
        dsc_ref[...] += _colsum(dhx)
        dgt_ref[...] += _colsum(dxm)

        @pl.when(i == n_i - 1)
        def _():
            t1, t2 = dsc_ref[...], dgt_ref[...]
            dsc_ref[...] = t1 * g_ref[...]
            dg_ref[...] = t1 * (1.0 + sc_ref[...])
            dgt_ref[...] = t2 * gp_ref[...]
            dgp_ref[...] = t2 * gt_ref[...]

    vec = pl.BlockSpec((1, d), lambda i: (0, 0))
    row = pl.BlockSpec((ts, d), lambda i: (i, 0))
    vshape = jax.ShapeDtypeStruct((1, d), F32)
    return pl.pallas_call(
        body, name="mid_bwd", grid=(s_len // ts,),
        in_specs=[row, row, row, row, vec, vec, vec, vec],
        out_specs=[row, row, vec, vec, vec, vec, vec],
        out_shape=[jax.ShapeDtypeStruct((s_len, d), F32), jax.ShapeDtypeStruct((s_len, d), BF16)] + [vshape] * 5,
        compiler_params=_cparams("arbitrary"),
    )(dh2, x1, dx2, mix, g_pre, sc, gt_m, g_post)


def _first_bwd(dh1, x, dx1, g_pre, sc, ts):
    s_len, d = x.shape
    n_i = s_len // ts

    def body(dh_ref, x_ref, dx1_ref, g_ref, sc_ref, dx_ref, dsh_ref, dsc_ref, dg_ref):
        i = pl.program_id(0)
        wv = (1.0 + sc_ref[...]) * g_ref[...]
        dh = dh_ref[...].astype(F32)
        xv = x_ref[...]
        r = _rsq(xv)
        xh = xv * r
        dhx = dh * xh
        dx_ref[...] = dx1_ref[...] + r * (dh * wv - xh * jnp.mean(dhx * wv, axis=-1, keepdims=True))

        @pl.when(i == 0)
        def _():
            for ref in (dsh_ref, dsc_ref):
                ref[...] = jnp.zeros(ref.shape, F32)

        dsh_ref[...] += _colsum(dh)
        dsc_ref[...] += _colsum(dhx)

        @pl.when(i == n_i - 1)
        def _():
            t1 = dsc_ref[...]
            dsc_ref[...] = t1 * g_ref[...]
            dg_ref[...] = t1 * (1.0 + sc_ref[...])

    vec = pl.BlockSpec((1, d), lambda i: (0, 0))
    row = pl.BlockSpec((ts, d), lambda i: (i, 0))
    vshape = jax.ShapeDtypeStruct((1, d), F32)
    return pl.pallas_call(
        body, name="first_bwd", grid=(s_len // ts,),
        in_specs=[row, row, row, vec, vec], out_specs=[row, vec, vec, vec],
        out_shape=[jax.ShapeDtypeStruct((s_len, d), F32), vshape, vshape, vshape],
        compiler_params=_cparams("arbitrary"),
    )(dh1, x, dx1, g_pre, sc)


def _gconv_bwd(d_mixcat, proj, lay, w, b, ts, tc):
    s_len = proj.shape[0]
    cw = w.shape[1]
    n_i = s_len // ts
    dc_off = d_mixcat.shape[1] - cw

    def body(dc_ref, dch_ref, gb_ref, gbh_ref, gc_ref, gch_ref, ci_ref, cih_ref, w_ref, b_ref,
             dgb_ref, dgc_ref, dci_ref, dw_ref, db_ref):
        i = pl.program_id(1)
        gc, ci = gc_ref[...].astype(F32), ci_ref[...].astype(F32)
        p = gc * ci
        ph = jnp.where(i > 0, gch_ref[...].astype(F32) * cih_ref[...].astype(F32), 0.0)
        pm1, pm2 = _shift_down(p, ph, 1), _shift_down(p, ph, 2)
        z = pm2 * w_ref[0:1, :] + pm1 * w_ref[1:2, :] + p * w_ref[2:3, :] + b_ref[...]
        dc = dc_ref[...].astype(F32)
        dgb_ref[...] = (dc * z).astype(BF16)
        dz = dc * gb_ref[...].astype(F32)
        dzh = jnp.where(i < n_i - 1, dch_ref[...].astype(F32) * gbh_ref[...].astype(F32), 0.0)
        dz1, dz2 = _shift_up(dz, dzh, 1), _shift_up(dz, dzh, 2)
        dp = dz * w_ref[2:3, :] + dz1 * w_ref[1:2, :] + dz2 * w_ref[0:1, :]
        dgc_ref[...] = (dp * ci).astype(BF16)
        dci_ref[...] = (dp * gc).astype(BF16)

        @pl.when(i == 0)
        def _():
            dw_ref[...] = jnp.zeros(dw_ref.shape, F32)
            db_ref[...] = jnp.zeros(db_ref.shape, F32)

        dw_ref[0:1, :] += _colsum(dz2 * p)
        dw_ref[1:2, :] += _colsum(dz1 * p)
        dw_ref[2:3, :] += _colsum(dz * p)
        db_ref[...] += _colsum(dz)

    def blk(off):
        return pl.BlockSpec((ts, tc), lambda j, i: (i, off // tc + j))

    def prev(off):
        return pl.BlockSpec((SUB, tc), _prev_halo(ts, lambda j: off // tc + j))

    def nxt(off):
        return pl.BlockSpec((SUB, tc), _next_halo(ts, s_len, lambda j: off // tc + j))

    out_blk = pl.BlockSpec((ts, tc), lambda j, i: (i, j))
    act = jax.ShapeDtypeStruct((s_len, cw), BF16)
    return pl.pallas_call(
        body, name="gconv_bwd", grid=(cw // tc, n_i),
        in_specs=[blk(dc_off), nxt(dc_off), blk(lay["gb"]), nxt(lay["gb"]), blk(lay["gc"]), prev(lay["gc"]),
                  blk(lay["ci"]), prev(lay["ci"]),
                  pl.BlockSpec((3, tc), lambda j, i: (0, j)), pl.BlockSpec((1, tc), lambda j, i: (0, j))],
        out_specs=[out_blk, out_blk, out_blk,
                   pl.BlockSpec((3, tc), lambda j, i: (0, j)), pl.BlockSpec((1, tc), lambda j, i: (0, j))],
        out_shape=[act, act, act, jax.ShapeDtypeStruct((3, cw), F32), jax.ShapeDtypeStruct((1, cw), F32)],
        compiler_params=_cparams("parallel", "arbitrary"),
    )(d_mixcat, d_mixcat, proj, proj, proj, proj, proj, proj, w, b)


def _delta(o, d_mixcat, n_heads, ts):
    s_len = o.shape[0]

    def body(o_ref, do_ref, out_ref):
        for h in range(n_heads):
            sl = slice(h * V_DIM, (h + 1) * V_DIM)
            prod = o_ref[:, sl].astype(F32) * do_ref[:, sl].astype(F32)
            out_ref[h] = jnp.broadcast_to(jnp.sum(prod, axis=1, keepdims=True), (ts, LANE))

    hv = n_heads * V_DIM
    return pl.pallas_call(
        body, name="attn_delta", grid=(s_len // ts,),
        in_specs=[pl.BlockSpec((ts, hv), lambda i: (i, 0)), pl.BlockSpec((ts, hv), lambda i: (i, 0))],
        out_specs=pl.BlockSpec((n_heads, ts, LANE), lambda i: (0, i, 0)),
        out_shape=jax.ShapeDtypeStruct((n_heads, s_len, LANE), F32),
        compiler_params=_cparams("parallel"),
    )(o, d_mixcat)


def _flash_bwd(q, k, v, d_mixcat, lse_row, delta_row, n_heads, t, scale, exchange):
    nx = len(exchange)
    s_len = q.shape[0]
    nb = s_len // t
    pairs = [(j, i) for j in range(nb) for i in range(j, nb)]
    jtab = jnp.asarray(np.array([p[0] for p in pairs], np.int32))
    itab = jnp.asarray(np.array([p[1] for p in pairs], np.int32))
    n_steps = len(pairs)

    def body(jt_ref, it_ref, q_ref, k_ref, v_ref, do_ref, lse_ref, dl_ref, *rest):
        xin, (dq_ref, dk_ref, dv_ref), xout = rest[:nx], rest[nx:nx + 3], rest[nx + 3:2 * nx + 3]
        dq_acc, dk_acc, dv_acc = rest[2 * nx + 3:2 * nx + 6]
        sems = rest[2 * nx + 6:]
        head, step_id = pl.program_id(0), pl.program_id(1)
        j, i = jt_ref[step_id], it_ref[step_id]

        @pl.when((head == 0) & (step_id == 0))
        def _():
            _exchange_start(xin, xout, *sems)

        @pl.when(step_id == 0)
        def _():
            dq_acc[...] = jnp.zeros(dq_acc.shape, F32)

        @pl.when(i == j)
        def _():
            dk_acc[...] = jnp.zeros(dk_acc.shape, F32)
            dv_acc[...] = jnp.zeros(dv_acc.shape, F32)

        def step(diag):
            half = t // 2
            lo, hi = slice(0, half), slice(half, t)
            blocks = [(lo, lo, True), (lo, hi, False), (hi, hi, True)] if diag else [(slice(0, t), slice(0, t), False)]
            for ks, qs, masked in blocks:
                qv, kv, vv, dov = q_ref[qs, :], k_ref[ks, :], v_ref[ks, :], do_ref[qs, :]
                s_t = lax.dot_general(kv, qv, NT_DIMS, preferred_element_type=F32)
                if masked:
                    krow = lax.broadcasted_iota(jnp.int32, s_t.shape, 0)
                    qcol = lax.broadcasted_iota(jnp.int32, s_t.shape, 1)
                    s_t = jnp.where(krow <= qcol, s_t, NEG)
                p_t = jnp.exp2(s_t - lse_ref[0, :, qs])
                dv_acc[ks, :] += jnp.dot(p_t.astype(BF16), dov, preferred_element_type=F32)
                dp_t = lax.dot_general(vv, dov, NT_DIMS, preferred_element_type=F32)
                ds_t = (p_t * (dp_t - dl_ref[0, :, qs])).astype(BF16)
                dk_acc[ks, :] += jnp.dot(ds_t, qv, preferred_element_type=F32)
                n_q = qs.stop - qs.start
                rows = pl.ds(pl.multiple_of(i * t + qs.start, n_q), n_q)
                dq_acc[rows, :] += lax.dot_general(ds_t, kv, TN_DIMS, preferred_element_type=F32)

        @pl.when(i > j)
        def _():
            step(False)

        @pl.when(i == j)
        def _():
            step(True)

        @pl.when(i == nb - 1)
        def _():
            dk_ref[...] = (dk_acc[...] * LN2).astype(BF16)
            dv_ref[...] = dv_acc[...].astype(BF16)

        @pl.when(step_id == n_steps - 1)
        def _():
            dq_ref[...] = (dq_acc[...] * scale).astype(BF16)

        @pl.when((head == n_heads - 1) & (step_id == n_steps - 1))
        def _():
            _exchange_finish(xin, xout, *sems)

    hbm = pl.BlockSpec(memory_space=pl.ANY)
    hv = n_heads * V_DIM
    do_off = 0
    grid_spec = pltpu.PrefetchScalarGridSpec(
        num_scalar_prefetch=2, grid=(n_heads, n_steps),
        in_specs=[pl.BlockSpec((t, HEAD_W), lambda h, s, jt, it: (it[s], h)),
                  pl.BlockSpec((t, HEAD_W), lambda h, s, jt, it: (jt[s], h)),
                  pl.BlockSpec((t, V_DIM), lambda h, s, jt, it: (jt[s], h)),
                  pl.BlockSpec((t, V_DIM), lambda h, s, jt, it: (it[s], do_off + h)),
                  pl.BlockSpec((1, 1, t), lambda h, s, jt, it: (h, 0, it[s])),
                  pl.BlockSpec((1, 1, t), lambda h, s, jt, it: (h, 0, it[s]))] + [hbm] * nx,
        out_specs=[pl.BlockSpec((s_len, HEAD_W), lambda h, s, jt, it: (0, h)),
                   pl.BlockSpec((t, HEAD_W), lambda h, s, jt, it: (jt[s], h)),
                   pl.BlockSpec((t, V_DIM), lambda h, s, jt, it: (jt[s], h))] + [hbm] * nx,
        scratch_shapes=[pltpu.VMEM((s_len, HEAD_W), F32), pltpu.VMEM((t, HEAD_W), F32), pltpu.VMEM((t, V_DIM), F32)]
        + _comm_scratch(nx),
    )
    return pl.pallas_call(
        body, name="flash_bwd", grid_spec=grid_spec,
        out_shape=[jax.ShapeDtypeStruct((s_len, n_heads * HEAD_W), BF16),
                   jax.ShapeDtypeStruct((s_len, n_heads * HEAD_W), BF16),
                   jax.ShapeDtypeStruct((s_len, hv), BF16)] + [jax.ShapeDtypeStruct(c.shape, c.dtype) for c in exchange],
        compiler_params=_cparams("arbitrary", "arbitrary"),
    )(jtab, itab, q, k, v, d_mixcat, lse_row, delta_row, *exchange)


def _qkv_bwd(dq, dk, dv, proj, lay, wuq, wk, wv, g_q, g_kv, ctab, atab, btab, n_heads, ts):
    s_len = proj.shape[0]
    ql_w, kl_w = wuq.shape[0], wk.shape[0]
    tail_w = lay["np"] - lay["ql"]
    kv_o, kr_o = lay["kv"] - lay["ql"], lay["kr"] - lay["ql"]

    def body(dq_ref, dk_ref, dv_ref, ql_ref, kl_ref, wuq_ref, wk_ref, wv_ref, gq_ref, gkv_ref, c_ref, a_ref, b_ref,
             dqr_ref, dkn_ref, tail_ref, dgq_ref, dgkv_ref):
        i = pl.program_id(0)
        c, a, b = c_ref[...], a_ref[...], b_ref[...]
        dkr = jnp.zeros((ts, LANE), F32)
        for h in range(n_heads):
            o = h * HEAD_W
            dqr_ref[:, o:o + QK_NOPE] = dq_ref[:, o:o + QK_NOPE]
            dqr_ref[:, o + QK_NOPE:o + HEAD_W] = _rope_t(dq_ref[:, o + QK_NOPE:o + HEAD_W].astype(F32), c, a, b).astype(BF16)
            dkn_ref[:, h * QK_NOPE:(h + 1) * QK_NOPE] = dk_ref[:, o:o + QK_NOPE]
            dkr = dkr + dk_ref[:, o + QK_NOPE:o + HEAD_W].astype(F32)
        tail_ref[...] = jnp.zeros(tail_ref.shape, BF16)
        tail_ref[:, kr_o:kr_o + LANE] = _rope_t(dkr, c, a, b).astype(BF16)

        def rms_bwd(lat_ref, dn, g_ref):
            lat = lat_ref[...].astype(F32)
            r = _rsq(lat)
            xh = lat * r
            dxh = dn * g_ref[...]
            return r * (dxh - xh * jnp.mean(dxh * xh, axis=-1, keepdims=True)), _colsum(dn * xh)

        dqn = lax.dot_general(dqr_ref[...], wuq_ref[...], NT_DIMS, preferred_element_type=F32)
        d_ql, dgq = rms_bwd(ql_ref, dqn, gq_ref)
        tail_ref[:, 0:ql_w] = d_ql.astype(BF16)
        dkvn = (lax.dot_general(dkn_ref[...], wk_ref[...], NT_DIMS, preferred_element_type=F32)
                + lax.dot_general(dv_ref[...], wv_ref[...], NT_DIMS, preferred_element_type=F32))
        d_kl, dgkv = rms_bwd(kl_ref, dkvn, gkv_ref)
        tail_ref[:, kv_o:kv_o + kl_w] = d_kl.astype(BF16)

        @pl.when(i == 0)
        def _():
            dgq_ref[...] = jnp.zeros(dgq_ref.shape, F32)
            dgkv_ref[...] = jnp.zeros(dgkv_ref.shape, F32)

        dgq_ref[...] += dgq
        dgkv_ref[...] += dgkv

    def full(arr):
        return pl.BlockSpec(arr.shape, lambda i: (0, 0))

    def rows(w):
        return pl.BlockSpec((ts, w), lambda i: (i, 0))

    tab = pl.BlockSpec((ts, LANE), lambda i: (i, 0))
    hw, hv, hn = n_heads * HEAD_W, n_heads * V_DIM, n_heads * QK_NOPE
    return pl.pallas_call(
        body, name="qkv_bwd", grid=(s_len // ts,),
        in_specs=[rows(hw), rows(hw), rows(hv),
                  pl.BlockSpec((ts, ql_w), lambda i: (i, lay["ql"] // ql_w)),
                  pl.BlockSpec((ts, kl_w), lambda i: (i, lay["kv"] // kl_w)),
                  full(wuq), full(wk), full(wv), full(g_q), full(g_kv), tab, tab, tab],
        out_specs=[rows(hw), rows(hn), rows(tail_w), full(g_q), full(g_kv)],
        out_shape=[jax.ShapeDtypeStruct((s_len, hw), BF16), jax.ShapeDtypeStruct((s_len, hn), BF16),
                   jax.ShapeDtypeStruct((s_len, tail_w), BF16),
                   jax.ShapeDtypeStruct(g_q.shape, F32), jax.ShapeDtypeStruct(g_kv.shape, F32)],
        compiler_params=_cparams("arbitrary"),
    )(dq, dk, dv, proj, proj, wuq, wk, wv, g_q, g_kv, ctab, atab, btab)


def _adamw(w, g, m, v):
    m = ADAM_B1 * m + (1.0 - ADAM_B1) * g
    v = ADAM_B2 * v + (1.0 - ADAM_B2) * (g * g)
    m_hat = m / (1.0 - ADAM_B1 ** ADAM_STEP)
    v_hat = v / (1.0 - ADAM_B2 ** ADAM_STEP)
    delta = -ADAM_LR * (m_hat / (jnp.sqrt(v_hat) + ADAM_EPS) + ADAM_WD * w)
    return delta, m, v


def _adam_parts(parts, w, m, v, tr, name, small=None):
    r, c = w.shape
    tr = _tile(r, tr, SUB)
    n_i = r // tr
    ns = 0 if small is None else 1

    def body(p_ref, w_ref, m_ref, v_ref, *rest):
        vec_ref = rest[:ns]
        g_out, d_out, m_out, v_out = rest[ns:ns + 4]
        gath_ref, sems = rest[ns + 4:2 * ns + 4], rest[2 * ns + 4:]
        i = pl.program_id(0)

        def small_copies():
            send_s, recv_s, local_s = sems
            me = _my_place()
            my_i = _index(me)

            def copy(k, to, src_row):
                row = gath_ref[0].at[pl.ds(src_row, 1), :]
                return pltpu.make_async_remote_copy(src_ref=row, dst_ref=row, send_sem=send_s.at[k], recv_sem=recv_s.at[k],
                                                    device_id=to, device_id_type=MESH)

            own = pltpu.make_async_copy(vec_ref[0], gath_ref[0].at[pl.ds(my_i, 1), :], local_s)
            sends = [copy(k, _peer(me, k), my_i) for k in range(1, N_DEV)]
            recvs = [copy(k, me, _index(_peer(me, k))) for k in range(1, N_DEV)]
            return own, sends, recvs

        if ns:
            @pl.when(i == 0)
            def _():
                own, sends, _ = small_copies()
                own.start()
                own.wait()
                for cp in sends:
                    cp.start()

        g = p_ref[0].astype(F32)
        for dev in range(1, N_DEV):
            g = g + p_ref[dev].astype(F32)
        g_out[...] = g
        d_out[...], m_out[...], v_out[...] = _adamw(w_ref[...], g, m_ref[...], v_ref[...])

        if ns:
            @pl.when(i == n_i - 1)
            def _():
                _, sends, recvs = small_copies()
                for cp in recvs:
                    cp.wait_recv()
                for cp in sends:
                    cp.wait_send()

    blk = pl.BlockSpec((tr, c), lambda i: (i, 0))
    hbm = pl.BlockSpec(memory_space=pl.ANY)
    shp = jax.ShapeDtypeStruct((r, c), F32)
    out = pl.pallas_call(
        body, name=name, grid=(n_i,),
        in_specs=[pl.BlockSpec((N_DEV, tr, c), lambda i: (0, i, 0)), blk, blk, blk] + [hbm] * ns,
        out_specs=[blk, blk, blk, blk] + [hbm] * ns,
        out_shape=[shp, shp, shp, shp] + ([jax.ShapeDtypeStruct((N_DEV, small.shape[1]), F32)] if ns else []),
        scratch_shapes=([pltpu.SemaphoreType.DMA((N_DEV,))] * 2 + [pltpu.SemaphoreType.DMA]) if ns else [],
        compiler_params=_cparams("arbitrary" if ns else "parallel"),
    )(parts, w, m, v, *([small] if ns else []))
    return (out[:4], out[4]) if ns else out


def _adam_ada(cact_t, dmod_sh, w, m, v, tr):
    r, c = w.shape

    def body(ct_ref, dm_ref, w_ref, m_ref, v_ref, g_out, d_out, m_out, v_out):
        g = jnp.dot(ct_ref[...], dm_ref[...], preferred_element_type=F32, precision=lax.Precision.HIGHEST)
        g_out[...] = g
        d_out[...], m_out[...], v_out[...] = _adamw(w_ref[...], g, m_ref[...], v_ref[...])

    blk = pl.BlockSpec((tr, c), lambda i: (i, 0))
    shp = jax.ShapeDtypeStruct((r, c), F32)
    return pl.pallas_call(
        body, name="adam_ada", grid=(r // tr,),
        in_specs=[pl.BlockSpec((tr, N_DEV), lambda i: (i, 0)), pl.BlockSpec((N_DEV, c), lambda i: (0, 0)), blk, blk, blk],
        out_specs=[blk, blk, blk, blk], out_shape=[shp, shp, shp, shp],
        compiler_params=_cparams("parallel"),
    )(cact_t, dmod_sh, w, m, v)


def _adam_small(v_all, offs, ws, ms, vs):
    n_par = len(ws)

    def body(p_ref, *refs):
        w_refs, m_refs, v_refs = refs[:n_par], refs[n_par:2 * n_par], refs[2 * n_par:3 * n_par]
        sum_ref = refs[3 * n_par]
        outs = refs[3 * n_par + 1:]
        g = p_ref[0:1, :]
        for dev in range(1, N_DEV):
            g = g + p_ref[dev:dev + 1, :]
        sum_ref[...] = g
        for p in range(n_par):
            n = w_refs[p].shape[1]
            gp = sum_ref[:, offs[p]:offs[p] + n]
            outs[p][...] = gp
            (outs[n_par + p][...], outs[2 * n_par + p][...], outs[3 * n_par + p][...]) = _adamw(
                w_refs[p][...], gp, m_refs[p][...], v_refs[p][...])

    vm = pl.BlockSpec(memory_space=pltpu.VMEM)
    shapes = [jax.ShapeDtypeStruct(w.shape, F32) for w in ws]
    out = pl.pallas_call(
        body, name="adam_small", in_specs=[vm] * (1 + 3 * n_par), out_specs=[vm] * (1 + 4 * n_par),
        out_shape=[jax.ShapeDtypeStruct((1, v_all.shape[1]), F32)] + shapes * 4, compiler_params=_cparams(),
    )(v_all, *ws, *ms, *vs)
    return out[0], [out[1 + k * n_par:1 + (k + 1) * n_par] for k in range(4)]


def _my_place():
    return lax.axis_index("x"), lax.axis_index("y"), lax.axis_index("c")


def _peer(place, k):
    x, y, c = place
    return (x ^ (k >> 2), y ^ ((k >> 1) & 1), c ^ (k & 1))


def _index(place):
    return 4 * place[0] + 2 * place[1] + place[2]


def _ada_fwd(vec, w_ada, b_ada_rows):
    lv = vec.shape[1]
    d, c = w_ada.shape

    def body(vec_ref, w_ref, b_ref, gath_ref, cact_ref, mod_ref, modsh, send_a, recv_a, send_b, recv_b, local_s):
        me = _my_place()
        my_i = _index(me)

        def gather_copy(k, to, src_row):
            row = gath_ref.at[pl.ds(src_row, 1), :]
            return pltpu.make_async_remote_copy(src_ref=row, dst_ref=row, send_sem=send_a.at[k], recv_sem=recv_a.at[k],
                                                device_id=to, device_id_type=MESH)

        own = pltpu.make_async_copy(vec_ref, gath_ref.at[pl.ds(my_i, 1), :], local_s.at[0])
        own.start()
        own.wait()
        sends = [gather_copy(k, _peer(me, k), my_i) for k in range(1, N_DEV)]
        for cp in sends:
            cp.start()
        for k in range(1, N_DEV):
            gather_copy(k, me, _index(_peer(me, k))).wait_recv()
        for cp in sends:
            cp.wait_send()

        c_all = gath_ref[:, 0:d]
        cact = c_all * _sigmoid(c_all)
        cact_ref[...] = cact
        modsh[...] = jnp.dot(cact, w_ref[...], preferred_element_type=F32, precision=lax.Precision.HIGHEST)

        def mod_copy(k, to, src_row, dst_row):
            return pltpu.make_async_remote_copy(src_ref=modsh.at[pl.ds(src_row, 1), :], dst_ref=mod_ref.at[pl.ds(dst_row, 1), :],
                                                send_sem=send_b.at[k], recv_sem=recv_b.at[k],
                                                device_id=to, device_id_type=MESH)

        own = pltpu.make_async_copy(modsh.at[pl.ds(my_i, 1), :], mod_ref.at[pl.ds(my_i, 1), :], local_s.at[1])
        own.start()
        sends = [mod_copy(k, _peer(me, k), _index(_peer(me, k)), my_i) for k in range(1, N_DEV)]
        for cp in sends:
            cp.start()
        for k in range(1, N_DEV):
            mod_copy(k, me, my_i, _index(_peer(me, k))).wait_recv()
        for cp in sends:
            cp.wait_send()
        own.wait()
        mod_ref[...] = mod_ref[...] + b_ref[...]

    vm = pl.BlockSpec(memory_space=pltpu.VMEM)
    return pl.pallas_call(
        body, name="ada_fwd", in_specs=[vm, vm, vm], out_specs=[vm, vm, vm],
        out_shape=[jax.ShapeDtypeStruct((N_DEV, lv), F32), jax.ShapeDtypeStruct((N_DEV, d), F32),
                   jax.ShapeDtypeStruct((N_DEV, c), F32)],
        scratch_shapes=[pltpu.VMEM((N_DEV, c), F32)] + [pltpu.SemaphoreType.DMA((N_DEV,))] * 4
        + [pltpu.SemaphoreType.DMA((2,))],
        compiler_params=pltpu.CompilerParams(vmem_limit_bytes=VMEM_LIMIT),
    )(vec, w_ada, b_ada_rows)


PER = N_DEV - 1


def _comm_scratch(n):
    return [pltpu.SemaphoreType.DMA((n * PER,)), pltpu.SemaphoreType.DMA((n * PER,)), pltpu.SemaphoreType.DMA((n,))]


def _gather_copies(ins, outs, send_s, recv_s, local_s):
    n = len(ins)
    me = _my_place()
    x, y, c = me
    sibling = (x, y, 1 - c)
    chips = [(1 - x, y), (x, 1 - y), (1 - x, 1 - y)]

    def copy(a, k, block, to, src=None):
        slot = outs[a].at[_index(block)]
        return pltpu.make_async_remote_copy(src_ref=slot if src is None else src, dst_ref=slot,
                                            send_sem=send_s.at[a * PER + k], recv_sem=recv_s.at[a * PER + k],
                                            device_id=to, device_id_type=MESH)

    mine = [pltpu.make_async_copy(ins[a], outs[a].at[_index(me)], local_s.at[a]) for a in range(n)]
    first = []
    for a in range(n):
        first.append(copy(a, 0, me, sibling, src=ins[a]))
        first += [copy(a, 1 + j, me, (*chip, c), src=ins[a]) for j, chip in enumerate(chips)]
    landed = [copy(a, 1 + j, (*chip, c), me) for j, chip in enumerate(chips) for a in range(n)]
    passed = [copy(a, 4 + j, (*chip, c), sibling) for j, chip in enumerate(chips) for a in range(n)]
    from_sibling = [copy(a, 0, sibling, me) for a in range(n)]
    from_sibling += [copy(a, 4 + j, (*chip, 1 - c), me) for a in range(n) for j, chip in enumerate(chips)]
    return mine, first, landed, passed, from_sibling


def _gather_start(*refs):
    mine, first, _, _, _ = _gather_copies(*refs)
    for cp in mine + first:
        cp.start()


def _gather_forward(*refs):
    _, _, landed, passed, _ = _gather_copies(*refs)
    for got, fwd in zip(landed, passed):
        got.wait_recv()
        fwd.start()


def _gather_finish(*refs):
    mine, first, _, passed, from_sibling = _gather_copies(*refs)
    for cp in from_sibling:
        cp.wait_recv()
    for cp in first + passed:
        cp.wait_send()
    for cp in mine:
        cp.wait()


def _exchange_copies(ins, outs, send_s, recv_s, local_s):
    n = len(ins)
    me = _my_place()
    my_i = _index(me)

    def copy(a, k, to, src_slot, dst_slot):
        return pltpu.make_async_remote_copy(src_ref=ins[a].at[src_slot], dst_ref=outs[a].at[dst_slot],
                                            send_sem=send_s.at[a * PER + k - 1], recv_sem=recv_s.at[a * PER + k - 1],
                                            device_id=to, device_id_type=MESH)

    mine = [pltpu.make_async_copy(ins[a].at[my_i], outs[a].at[my_i], local_s.at[a]) for a in range(n)]
    sends = [copy(a, k, _peer(me, k), _index(_peer(me, k)), my_i) for k in range(1, N_DEV) for a in range(n)]
    recvs = [copy(a, k, me, my_i, _index(_peer(me, k))) for k in range(1, N_DEV) for a in range(n)]
    return mine, sends, recvs


def _exchange_start(*refs):
    mine, sends, _ = _exchange_copies(*refs)
    for cp in mine + sends:
        cp.start()


def _exchange_finish(*refs):
    mine, sends, recvs = _exchange_copies(*refs)
    for cp in recvs:
        cp.wait_recv()
    for cp in sends:
        cp.wait_send()
    for cp in mine:
        cp.wait()


def _gathered_shapes(shards):
    return [jax.ShapeDtypeStruct((N_DEV,) + s.shape, s.dtype) for s in shards]


def _proj_layout(cw, ql, kl):
    lay = {"gb": 0, "gc": cw, "ci": 2 * cw, "ql": 3 * cw}
    assert lay["ql"] % ql == 0
    lay["kv"] = _roundup(lay["ql"] + ql, kl)
    lay["kr"] = lay["kv"] + kl
    lay["np"] = _roundup(lay["kr"] + LANE, 4 * LANE)
    return lay


def _chunks_cols(g):
    r, c8 = g.shape
    return jnp.transpose(g.reshape(r, N_DEV, c8 // N_DEV), (1, 0, 2))


def _from_col_shards(a):
    n, r, c = a.shape
    return jnp.transpose(a, (1, 0, 2)).reshape(r, n * c)


def kernel(x, c, positions, w_ada, b_ada, g_pre_mix, g_post_mix, w_in, g_q, w_uq, g_kv, w_ukv, conv_w_mix, conv_b_mix, w_o, g_pre_ffn, g_post_ffn, w_up, conv_w_ffn, conv_b_ffn, w_down, loss_target, m_w_ada, m_b_ada, m_g_pre_mix, m_g_post_mix, m_w_in, m_g_q, m_w_uq, m_g_kv, m_w_ukv, m_conv_w_mix, m_conv_b_mix, m_w_o, m_g_pre_ffn, m_g_post_ffn, m_w_up, m_conv_w_ffn, m_conv_b_ffn, m_w_down, v_w_ada, v_b_ada, v_g_pre_mix, v_g_post_mix, v_w_in, v_g_q, v_w_uq, v_g_kv, v_w_ukv, v_conv_w_mix, v_conv_b_mix, v_w_o, v_g_pre_ffn, v_g_post_ffn, v_w_up, v_conv_w_ffn, v_conv_b_ffn, v_w_down):
    s_len, d = x.shape[1], x.shape[2]
    ql, kl = w_uq.shape[1], w_ukv.shape[1]
    n_heads = w_ukv.shape[2] * N_DEV // (QK_NOPE + V_DIM)
    cw = conv_w_mix.shape[2] * N_DEV
    f2 = w_up.shape[2] * N_DEV
    ff = f2 // 2
    in_cols = w_in.shape[2] * N_DEV
    ada_c = w_ada.shape[2]
    cwm_c, cwf_c = conv_w_mix.shape[2], conv_w_ffn.shape[2]
    scale = 1.0 / math.sqrt(QK_NOPE + QK_ROPE)
    lay = _proj_layout(cw, ql, kl)
    n_pad = lay["np"]
    my_i = _index(_my_place())

    ts_row = _tile(s_len, PREF["row"], SUB)
    ts_conv = _tile(s_len, PREF["conv_rows"], SUB)
    tc_conv = _tile(cw, PREF["conv_cols"])
    tc_ffn = cwf_c
    ts_ffn = _tile(s_len, PREF["ffn_rows"], SUB)
    ts_act = _tile(s_len, PREF["act_rows"], SUB)
    ts_big = _tile(s_len, PREF["row_big"], SUB)
    pair_order = [k // 2 + (k % 2) * (N_DEV // 2) for k in range(N_DEV)]
    pair_place = [pair_order.index(k) for k in range(N_DEV)]

    def paired(shards):
        return _from_col_shards(jnp.stack([shards[p] for p in pair_order]))

    def unpaired_chunks(g):
        ch = _chunks_cols(g)
        return jnp.stack([ch[p] for p in pair_place])
    ts_qkv = _tile(s_len, PREF["row_big"], SUB)
    t_attn = _tile(s_len, PREF["attn"])

    x2d, tgt = x[0], loss_target[0]

    vec = jnp.concatenate([c, conv_w_mix[0].reshape(1, -1), conv_w_ffn[0].reshape(1, -1)], axis=1)
    gath, cact, mod_rows = _ada_fwd(vec, w_ada[0], b_ada.reshape(N_DEV, ada_c))
    cwm_full = _from_col_shards(gath[:, d:d + 3 * cwm_c].reshape(N_DEV, 3, cwm_c))
    cwf_shards = gath[:, d + 3 * cwm_c:].reshape(N_DEV, 3, cwf_c)
    cwf_pair = paired(cwf_shards)
    cbf_pair = paired(jnp.transpose(conv_b_ffn.reshape(1, N_DEV, cwf_c), (1, 0, 2)))
    mod = mod_rows.reshape(1, N_DEV * ada_c)
    sh_m, sc_m, gt_m, sh_f, sc_f, gt_f = [mod[:, k * d:(k + 1) * d] for k in range(6)]

    h1, g_in = _modnorm_fwd(x2d, g_pre_mix, sc_m, sh_m, ts_row, [w_in[0].astype(BF16)])
    win = _from_col_shards(g_in)
    cut = np.cumsum([0, ql, kl, QK_ROPE, cw, cw, cw])
    part = [win[:, cut[k]:cut[k + 1]] for k in range(6)]

    def zcols(n):
        return jnp.zeros((d, n), BF16)

    win_p = jnp.concatenate([part[3], part[4], part[5], part[0], zcols(lay["kv"] - lay["ql"] - ql), part[1],
                             part[2], zcols(n_pad - lay["kr"] - QK_ROPE)], axis=1)
    inv_freq =1.0 / (ROPE_THETA ** (jnp.arange(0, QK_ROPE, 2, dtype=F32) / QK_ROPE))
    inv_row = jnp.tile(inv_freq, LANE // (QK_ROPE // 2)).reshape(1, LANE)
    ctab, atab, btab = _rope_tables(positions.astype(F32).reshape(s_len, 1), inv_row, _tile(s_len, 1024, SUB))

    proj, g_uq, g_ukv = _matmul(h1, win_p, out_dtype=BF16, tm=2048, tn=1280, tk=2048, name="mm_proj",
                                gather=[w_uq[0].astype(BF16), w_ukv[0].astype(BF16)])
    wuq_p = jnp.pad(_from_col_shards(g_uq).reshape(ql, n_heads, QK_NOPE + QK_ROPE),
                    ((0, 0), (0, 0), (0, HEAD_W - QK_NOPE - QK_ROPE))).reshape(ql, n_heads * HEAD_W)
    wukv = _from_col_shards(g_ukv).reshape(kl, n_heads, QK_NOPE + V_DIM)
    wk = wukv[:, :, :QK_NOPE].reshape(kl, n_heads * QK_NOPE)
    wv = wukv[:, :, QK_NOPE:].reshape(kl, n_heads * V_DIM)
    q, k, v, qn, kvn = _qkv_fwd(proj, lay, wuq_p, wk, wv, g_q, g_kv, ctab, atab, btab, n_heads, ts_qkv, scale * LOG2E)
    attn, lse_row, g_o, g_up = _flash_fwd(q, k, v, n_heads, t_attn, [w_o[0].astype(BF16), w_up[0].astype(BF16)], d)
    wo = g_o.reshape(d, d)

    def pair_shard(j):
        return j // 2 + (j % 2) * (N_DEV // 2)
    mixcat = _gconv_fwd(proj, lay, cwm_full, conv_b_mix, ts_conv, tc_conv, attn)
    mix = _matmul(mixcat, wo, out_dtype=BF16, tm=512, tn=2048, tk=2048, name="mm_mix")
    x1, h2 = _post_mix_fwd(mix, x2d, gt_m, g_post_mix, g_pre_ffn, sc_f, sh_f, ts_big)
    up, g_down = _matmul(h2, g_up, out_dtype=BF16, tm=2048, tn=cwf_c, tk=2048, name="mm_up", b_shard_of=pair_shard,
                         gather=[w_down[0].astype(BF16)])
    wdown = g_down.reshape(ff, d)
    u, act = _act_fwd(up, cwf_pair, cbf_pair, ts_act, tc_ffn)
    y = _matmul(act, wdown, out_dtype=BF16, tm=1024, tn=512, tk=ff, name="mm_down")

    dy, dx2, loss_row, d_gt_f, dg_post_ffn = _final_bwd(y, x1, tgt, gt_f, g_post_ffn, ts_big)
    gw_down = _matmul(act, dy, ta=True, out_dtype=BF16, tm=512, tn=512, tk=s_len, name="mm_gw_down")
    d_act = _matmul(dy, wdown, tb=True, out_dtype=BF16, tm=2048, tn=1408, tk=2048, name="mm_d_act")
    d_up, dcw_pair, dcb_pair = _ffn_act_bwd(d_act, u, up, cwf_pair, ts_ffn, tc_ffn)
    dcb_ffn = _from_col_shards(unpaired_chunks(dcb_pair))
    gw_up, p_down = _matmul(h2, d_up, ta=True, out_dtype=BF16, tm=512, tn=cwf_c, tk=4096, name="mm_gw_up",
                            exchange=[gw_down.reshape(N_DEV, ff // N_DEV, d)], out_shard_of=pair_shard)
    dh2 = _matmul_pair_shards(d_up, g_up, out_dtype=BF16, tm=1024, tn=1024, name="mm_dh2")
    dx1, dmix, d_sh_f, d_sc_f, dg_pre_ffn, d_gt_m, dg_post_mix = _mid_bwd(
        dh2, x1, dx2, mix, g_pre_ffn, sc_f, gt_m, g_post_mix, ts_row)
    gw_o = _matmul(mixcat, dmix, ta=True, out_dtype=BF16, tm=512, tn=512, tk=s_len, name="mm_gw_o")
    d_mixcat = _matmul(dmix, wo, tb=True, out_dtype=BF16, tm=1024, tn=1024, tk=2048, name="mm_d_mixcat")
    d_gb, d_gc, d_ci, dcw_mix, dcb_mix = _gconv_bwd(d_mixcat, proj, lay, cwm_full, conv_b_mix, ts_conv, tc_conv)
    delta = _delta(mixcat, d_mixcat, n_heads, _tile(s_len, 512, SUB))
    delta_row = delta[:, :, 0].reshape(n_heads, 1, s_len)
    dq, dk, dv, p_up, p_o = _flash_bwd(q, k, v, d_mixcat, lse_row, delta_row, n_heads, t_attn, scale,
                                       [gw_up, gw_o.reshape(N_DEV, d // N_DEV, d)])
    dq_r, dkn, d_tail, dg_q, dg_kv = _qkv_bwd(dq, dk, dv, proj, lay, wuq_p, wk, wv, g_q, g_kv, ctab, atab, btab,
                                              n_heads, ts_qkv)
    gw_uq_p = _matmul(qn, dq_r, ta=True, out_dtype=BF16, tm=768, tn=512, tk=s_len, name="mm_gw_uq")
    gw_k = _matmul(kvn, dkn, ta=True, out_dtype=BF16, tm=512, tn=512, tk=s_len, name="mm_gw_k")
    gw_v = _matmul(kvn, dv, ta=True, out_dtype=BF16, tm=512, tn=512, tk=s_len, name="mm_gw_v")
    d_proj = jnp.concatenate([d_gb, d_gc, d_ci, d_tail], axis=1)
    gw_uq = gw_uq_p.reshape(ql, n_heads, HEAD_W)[:, :, :QK_NOPE + QK_ROPE].reshape(ql, n_heads * (QK_NOPE + QK_ROPE))
    gw_ukv = jnp.concatenate([gw_k.reshape(kl, n_heads, QK_NOPE), gw_v.reshape(kl, n_heads, V_DIM)],
                             axis=2).reshape(kl, n_heads * (QK_NOPE + V_DIM))
    gw_in_p, p_uq, p_ukv, p_cwm, p_cwf = _matmul(
        h1, d_proj, ta=True, out_dtype=BF16, tm=512, tn=512, tk=s_len, name="mm_gw_in",
        exchange=[_chunks_cols(gw_uq), _chunks_cols(gw_ukv), _chunks_cols(dcw_mix), unpaired_chunks(dcw_pair)])

    gw_in = jnp.concatenate([gw_in_p[:, lay["ql"]:lay["ql"] + ql], gw_in_p[:, lay["kv"]:lay["kv"] + kl],
                             gw_in_p[:, lay["kr"]:lay["kr"] + QK_ROPE], gw_in_p[:, :3 * cw]], axis=1)
    dh1, p_in = _matmul(d_proj, win_p, tb=True, out_dtype=BF16, tm=512, tn=1024, tk=n_pad, name="mm_dh1",
                        exchange=[_chunks_cols(gw_in)])
    grad_x, d_sh_m, d_sc_m, dg_pre_mix = _first_bwd(dh1, x2d, dx1, g_pre_mix, sc_m, ts_big)

    dmod = jnp.concatenate([d_sh_m, d_sc_m, d_gt_m, d_sh_f, d_sc_f, d_gt_f], axis=1)
    small_g = [loss_row, dmod, dg_pre_mix, dg_post_mix, dg_q, dg_kv, dcb_mix, dg_pre_ffn, dg_post_ffn, dcb_ffn]
    adam_w_up, v_all = _adam_parts(p_up, w_up[0], m_w_up[0], v_w_up[0], 256, "adam_w_up",
                                   small=jnp.concatenate(small_g, axis=1))

    small_names = ["b_ada", "g_pre_mix", "g_post_mix", "g_q", "g_kv", "conv_b_mix", "g_pre_ffn", "g_post_ffn",
                   "conv_b_ffn"]
    small_w = [b_ada, g_pre_mix, g_post_mix, g_q, g_kv, conv_b_mix, g_pre_ffn, g_post_ffn, conv_b_ffn]
    small_m = [m_b_ada, m_g_pre_mix, m_g_post_mix, m_g_q, m_g_kv, m_conv_b_mix, m_g_pre_ffn, m_g_post_ffn, m_conv_b_ffn]
    small_v = [v_b_ada, v_g_pre_mix, v_g_post_mix, v_g_q, v_g_kv, v_conv_b_mix, v_g_pre_ffn, v_g_post_ffn, v_conv_b_ffn]
    offs = np.cumsum([0] + [g.shape[1] for g in small_g])
    g_sum, small_out = _adam_small(v_all, [int(o) for o in offs[1:-1]], small_w, small_m, small_v)
    small = [dict(zip(small_names, kind)) for kind in small_out]
    loss = g_sum[0, 0]

    dmod_sh = lax.dynamic_slice(v_all, (0, int(offs[1]) + my_i * ada_c), (N_DEV, ada_c))
    cact_t = jnp.transpose(cact)
    big = dict(
        w_ada=_adam_ada(cact_t, dmod_sh, w_ada[0], m_w_ada[0], v_w_ada[0], _tile(d, 256, SUB)),
        w_in=_adam_parts(p_in, w_in[0], m_w_in[0], v_w_in[0], 256, "adam_w_in"),
        w_uq=_adam_parts(p_uq, w_uq[0], m_w_uq[0], v_w_uq[0], 256, "adam_w_uq"),
        w_ukv=_adam_parts(p_ukv, w_ukv[0], m_w_ukv[0], v_w_ukv[0], 256, "adam_w_ukv"),
        w_o=_adam_parts(p_o, w_o[0], m_w_o[0], v_w_o[0], 128, "adam_w_o"),
        w_up=adam_w_up,
        w_down=_adam_parts(p_down, w_down[0], m_w_down[0], v_w_down[0], 176, "adam_w_down"),
        conv_w_mix=_adam_parts(p_cwm, conv_w_mix[0], m_conv_w_mix[0], v_conv_w_mix[0], 8, "adam_cw_mix"),
        conv_w_ffn=_adam_parts(p_cwf, conv_w_ffn[0], m_conv_w_ffn[0], v_conv_w_ffn[0], 8, "adam_cw_ffn"),
    )

    names = ["w_ada", "b_ada", "g_pre_mix", "g_post_mix", "w_in", "g_q", "w_uq", "g_kv", "w_ukv", "conv_w_mix",
             "conv_b_mix", "w_o", "g_pre_ffn", "g_post_ffn", "w_up", "conv_w_ffn", "conv_b_ffn", "w_down"]
    outs = [loss, grad_x[None]]
    for kind in range(4):
        for nm in names:
            outs.append(big[nm][kind][None] if nm in big else small[kind][nm])
    return tuple(outs)
```

```python
import math

import numpy as np
import jax
import jax.numpy as jnp
from jax import lax
from jax.experimental import pallas as pl
from jax.experimental.pallas import tpu as pltpu

F32 = jnp.float32
BF16 = jnp.bfloat16
N_DEV = 8
MESH = pl.DeviceIdType.MESH

QK_NOPE = 128
QK_ROPE = 64
V_DIM = 128
HEAD_W = 256
LANE = 128
SUB = 8
HALO = 16
STRIP_ROWS = 64
RMS_EPS = 1e-6
ROPE_THETA = 10000.0
ADAM_LR = 0.001
ADAM_B1 = 0.9
ADAM_B2 = 0.999
ADAM_EPS = 1e-08
ADAM_WD = 0.01
ADAM_STEP = 10
NEG = -1e30
LOG2E = 1.4426950408889634
LN2 = 0.6931471805599453
VMEM_LIMIT = 56 * 1024 * 1024

PREF = {"row": 256, "row_big": 512, "act_rows": 1024, "conv_rows": 1024, "conv_cols": 512, "ffn_rows": 1024, "attn": 1024, "attn_chunk": 1024, "attn_heads": 8}

NT_DIMS = (((1,), (1,)), ((), ()))
TN_DIMS = (((0,), (0,)), ((), ()))


def _cparams(*sem):
    return pltpu.CompilerParams(dimension_semantics=sem if sem else None, vmem_limit_bytes=VMEM_LIMIT)


def _tile(n, pref, unit=LANE):
    if n <= pref:
        return n
    t = (pref // unit) * unit
    while t >= unit:
        if n % t == 0:
            return t
        t -= unit
    return n


def _roundup(n, m):
    return (n + m - 1) // m * m


def _rsq(x):
    return lax.rsqrt(jnp.mean(x * x, axis=-1, keepdims=True) + RMS_EPS)


def _colsum(x):
    return jnp.sum(x, axis=0, keepdims=True)


def _sigmoid(x):
    return 1.0 / (1.0 + jnp.exp(-x))


def _matmul(a, b, *, ta=False, tb=False, out_dtype, tm, tn, tk, name, exchange=None, gather=None, b_shard_of=None,
            out_shard_of=None):
    m_dim, k_dim = (a.shape[1], a.shape[0]) if ta else a.shape
    if b_shard_of is not None:
        n_dim, tn = b.shape[0] * b.shape[2], b.shape[2]
    else:
        n_dim = b.shape[0] if tb else b.shape[1]
    chunk_w = n_dim // N_DEV
    if out_shard_of is not None:
        tn = _tile(chunk_w, tn)
    tm, tn, tk = _tile(m_dim, tm), _tile(n_dim, tn), _tile(k_dim, tk)
    per_chunk = chunk_w // tn if out_shard_of is not None else 1
    gi, gj, nk = m_dim // tm, n_dim // tn, k_dim // tk
    dims = (((0 if ta else 1,), (1 if tb else 0,)), ((), ()))
    chunks = list(exchange or gather or [])
    nx = len(chunks)
    comm_start, comm_finish = (_gather_start, _gather_finish) if gather else (_exchange_start, _exchange_finish)

    def body(*refs):
        a_ref, b_ref = refs[:2]
        xin, o_ref, xout = refs[2:2 + nx], refs[2 + nx], refs[3 + nx:3 + 2 * nx]
        scratch = refs[3 + 2 * nx:]
        sems = scratch[1:] if nk > 1 else scratch
        i, j, k = pl.program_id(0), pl.program_id(1), pl.program_id(2)
        if nx:
            @pl.when((i == 0) & (j == 0) & (k == 0))
            def _():
                comm_start(xin, xout, *sems)

        if gather:
            @pl.when((i == gi // 2) & (j == 0) & (k == 0))
            def _():
                _gather_forward(xin, xout, *sems)

        part = lax.dot_general(a_ref[...], b_ref[...], dims, preferred_element_type=F32)
        if nk == 1:
            o_ref[...] = part.astype(o_ref.dtype)
        else:
            acc_ref = scratch[0]

            @pl.when(k == 0)
            def _():
                acc_ref[...] = part

            @pl.when(k > 0)
            def _():
                acc_ref[...] += part

            @pl.when(k == nk - 1)
            def _():
                o_ref[...] = acc_ref[...].astype(o_ref.dtype)

        if nx:
            @pl.when((i == gi - 1) & (j == gj - 1) & (k == nk - 1))
            def _():
                comm_finish(xin, xout, *sems)

    a_spec = pl.BlockSpec((tk, tm), lambda i, j, k: (k, i)) if ta else pl.BlockSpec((tm, tk), lambda i, j, k: (i, k))
    b_spec = pl.BlockSpec((tn, tk), lambda i, j, k: (j, k)) if tb else pl.BlockSpec((tk, tn), lambda i, j, k: (k, j))
    if b_shard_of is not None:
        b_spec = pl.BlockSpec((None, tk, tn), lambda i, j, k: (b_shard_of(j), k, 0))
    o_spec, o_shape = pl.BlockSpec((tm, tn), lambda i, j, k: (i, j)), (m_dim, n_dim)
    if out_shard_of is not None:
        o_spec = pl.BlockSpec((None, tm, tn), lambda i, j, k: (out_shard_of(j // per_chunk), i, j % per_chunk))
        o_shape = (N_DEV, m_dim, chunk_w)
    hbm = pl.BlockSpec(memory_space=pl.ANY)
    out = pl.pallas_call(
        body,
        name=name,
        grid=(gi, gj, nk),
        in_specs=[a_spec, b_spec] + [hbm] * nx,
        out_specs=[o_spec] + [hbm] * nx,
        out_shape=[jax.ShapeDtypeStruct(o_shape, out_dtype)]
        + (_gathered_shapes(chunks) if gather else [jax.ShapeDtypeStruct(c.shape, c.dtype) for c in chunks]),
        scratch_shapes=([pltpu.VMEM((tm, tn), F32)] if nk > 1 else []) + (_comm_scratch(nx) if nx else []),
        compiler_params=_cparams(*(("arbitrary",) * 3 if nx else ("parallel", "parallel", "arbitrary"))),
    )(a, b, *chunks)
    return out if nx else out[0]


def _matmul_pair_shards(a, shards, *, out_dtype, tm, tn, name):
    m_dim = a.shape[0]
    n_sh, n_dim, c = shards.shape
    half = n_sh // 2
    tm, tn = _tile(m_dim, tm), _tile(n_dim, tn)

    def body(a_ref, b0_ref, b1_ref, o_ref, acc_ref):
        k = pl.program_id(2)
        part = (lax.dot_general(a_ref[:, :c], b0_ref[...], NT_DIMS, preferred_element_type=F32)
                + lax.dot_general(a_ref[:, c:], b1_ref[...], NT_DIMS, preferred_element_type=F32))

        @pl.when(k == 0)
        def _():
            acc_ref[...] = part

        @pl.when(k > 0)
        def _():
            acc_ref[...] += part

        @pl.when(k == half - 1)
        def _():
            o_ref[...] = acc_ref[...].astype(o_ref.dtype)

    return pl.pallas_call(
        body, name=name, grid=(m_dim // tm, n_dim // tn, half),
        in_specs=[pl.BlockSpec((tm, 2 * c), lambda i, j, k: (i, k)),
                  pl.BlockSpec((None, tn, c), lambda i, j, k: (k, j, 0)),
                  pl.BlockSpec((None, tn, c), lambda i, j, k: (k + half, j, 0))],
        out_specs=pl.BlockSpec((tm, tn), lambda i, j, k: (i, j)),
        out_shape=jax.ShapeDtypeStruct((m_dim, n_dim), out_dtype),
        scratch_shapes=[pltpu.VMEM((tm, tn), F32)],
        compiler_params=_cparams("parallel", "parallel", "arbitrary"),
    )(a, shards, shards)


def _shift_down(x, halo, n):
    r = pltpu.roll(x, n, 0)
    hr = pltpu.roll(halo, n, 0)
    row = lax.broadcasted_iota(jnp.int32, halo.shape, 0)
    top = jnp.where(row < n, hr, r[:SUB])
    return jnp.concatenate([top, r[SUB:]], axis=0)


def _shift_up(x, halo, n):
    ts = x.shape[0]
    r = pltpu.roll(x, ts - n, 0)
    hr = pltpu.roll(halo, SUB - n, 0)
    row = lax.broadcasted_iota(jnp.int32, halo.shape, 0)
    bot = jnp.where(row >= SUB - n, hr, r[ts - SUB:])
    return jnp.concatenate([r[:ts - SUB], bot], axis=0)


def _conv3(x, halo, w_ref, b_ref):
    return _shift_down(x, halo, 2) * w_ref[0:1, :] + _shift_down(x, halo, 1) * w_ref[1:2, :] + x * w_ref[2:3, :] + b_ref[...]


def _prev_halo(ts, col):
    return lambda j, i: (jnp.maximum(i * (ts // SUB) - 1, 0), col(j))


def _next_halo(ts, n_rows, col):
    return lambda j, i: (jnp.minimum((i + 1) * (ts // SUB), n_rows // SUB - 1), col(j))


def _modnorm_fwd(x, g, sc, sh, ts, gather):
    s_len, d = x.shape
    n_i = s_len // ts
    ng = len(gather)

    def body(x_ref, g_ref, sc_ref, sh_ref, *rest):
        gin, h_ref, gout, sems = rest[:ng], rest[ng], rest[ng + 1:2 * ng + 1], rest[2 * ng + 1:]
        i = pl.program_id(0)

        @pl.when(i == 0)
        def _():
            _gather_start(gin, gout, *sems)

        xv = x_ref[...]
        h_ref[...] = ((xv * _rsq(xv) * g_ref[...]) * (1.0 + sc_ref[...]) + sh_ref[...]).astype(BF16)

        @pl.when(i == n_i - 1)
        def _():
            _gather_forward(gin, gout, *sems)
            _gather_finish(gin, gout, *sems)

    vec = pl.BlockSpec((1, d), lambda i: (0, 0))
    hbm = pl.BlockSpec(memory_space=pl.ANY)
    return pl.pallas_call(
        body, name="modnorm_fwd", grid=(n_i,),
        in_specs=[pl.BlockSpec((ts, d), lambda i: (i, 0)), vec, vec, vec] + [hbm] * ng,
        out_specs=[pl.BlockSpec((ts, d), lambda i: (i, 0))] + [hbm] * ng,
        out_shape=[jax.ShapeDtypeStruct((s_len, d), BF16)] + _gathered_shapes(gather),
        scratch_shapes=_comm_scratch(ng),
        compiler_params=_cparams("arbitrary"),
    )(x, g, sc, sh, *gather)


def _rope_tables(pos_col, inv_freq_row, ts):
    s_len = pos_col.shape[0]
    half = QK_ROPE // 2

    def body(p_ref, f_ref, c_ref, a_ref, b_ref):
        ang = p_ref[...] * f_ref[...]
        lane = lax.broadcasted_iota(jnp.int32, ang.shape, 1)
        cos, sin = jnp.cos(ang), jnp.sin(ang)
        c_ref[...] = jnp.where(lane < 2 * half, cos, 0.0)
        a_ref[...] = jnp.where(lane < half, -sin, 0.0)
        b_ref[...] = jnp.where((lane >= half) & (lane < 2 * half), sin, 0.0)

    out = jax.ShapeDtypeStruct((s_len, LANE), F32)
    blk = pl.BlockSpec((ts, LANE), lambda i: (i, 0))
    return pl.pallas_call(
        body, name="rope_tables", grid=(s_len // ts,),
        in_specs=[pl.BlockSpec((ts, 1), lambda i: (i, 0)), pl.BlockSpec((1, LANE), lambda i: (0, 0))],
        out_specs=[blk, blk, blk], out_shape=[out, out, out],
        compiler_params=_cparams("parallel"),
    )(pos_col, inv_freq_row)


def _rope(seg, c, a, b):
    return seg * c + pltpu.roll(seg, LANE - QK_ROPE // 2, 1) * a + pltpu.roll(seg, QK_ROPE // 2, 1) * b


def _rope_t(seg, c, a, b):
    return seg * c - pltpu.roll(seg, LANE - QK_ROPE // 2, 1) * a - pltpu.roll(seg, QK_ROPE // 2, 1) * b


def _qkv_fwd(proj, lay, wuq, wk, wv, g_q, g_kv, ctab, atab, btab, n_heads, ts, scale):
    s_len = proj.shape[0]
    ql_w, kl_w = wuq.shape[0], wk.shape[0]

    def body(ql_ref, kl_ref, kr_ref, wuq_ref, wk_ref, wv_ref, gq_ref, gkv_ref, c_ref, a_ref, b_ref,
             q_out, k_out, v_out, qn_out, kvn_out):
        c, a, b = c_ref[...], a_ref[...], b_ref[...]
        ql = ql_ref[...].astype(F32)
        qn = (ql * _rsq(ql) * gq_ref[...]).astype(BF16)
        qn_out[...] = qn
        q = jnp.dot(qn, wuq_ref[...], preferred_element_type=F32)
        kl = kl_ref[...].astype(F32)
        kvn = (kl * _rsq(kl) * gkv_ref[...]).astype(BF16)
        kvn_out[...] = kvn
        kn = jnp.dot(kvn, wk_ref[...], preferred_element_type=F32)
        v_out[...] = jnp.dot(kvn, wv_ref[...], preferred_element_type=F32).astype(BF16)
        kr = _rope(kr_ref[...].astype(F32), c, a, b).astype(BF16)
        for h in range(n_heads):
            o = h * HEAD_W
            q_out[:, o:o + QK_NOPE] = (q[:, o:o + QK_NOPE] * scale).astype(BF16)
            q_out[:, o + QK_NOPE:o + HEAD_W] = (_rope(q[:, o + QK_NOPE:o + HEAD_W], c, a, b) * scale).astype(BF16)
            k_out[:, o:o + QK_NOPE] = kn[:, h * QK_NOPE:(h + 1) * QK_NOPE].astype(BF16)
            k_out[:, o + QK_NOPE:o + HEAD_W] = kr

    def full(arr):
        return pl.BlockSpec(arr.shape, lambda i: (0, 0))

    tab = pl.BlockSpec((ts, LANE), lambda i: (i, 0))
    hw, hv = n_heads * HEAD_W, n_heads * V_DIM
    return pl.pallas_call(
        body, name="qkv_fwd", grid=(s_len // ts,),
        in_specs=[pl.BlockSpec((ts, ql_w), lambda i: (i, lay["ql"] // ql_w)),
                  pl.BlockSpec((ts, kl_w), lambda i: (i, lay["kv"] // kl_w)),
                  pl.BlockSpec((ts, LANE), lambda i: (i, lay["kr"] // LANE)),
                  full(wuq), full(wk), full(wv), full(g_q), full(g_kv), tab, tab, tab],
        out_specs=[pl.BlockSpec((ts, hw), lambda i: (i, 0)), pl.BlockSpec((ts, hw), lambda i: (i, 0)),
                   pl.BlockSpec((ts, hv), lambda i: (i, 0)), pl.BlockSpec((ts, ql_w), lambda i: (i, 0)),
                   pl.BlockSpec((ts, kl_w), lambda i: (i, 0))],
        out_shape=[jax.ShapeDtypeStruct((s_len, hw), BF16), jax.ShapeDtypeStruct((s_len, hw), BF16),
                   jax.ShapeDtypeStruct((s_len, hv), BF16), jax.ShapeDtypeStruct((s_len, ql_w), BF16),
                   jax.ShapeDtypeStruct((s_len, kl_w), BF16)],
        compiler_params=_cparams("parallel"),
    )(proj, proj, proj, wuq, wk, wv, g_q, g_kv, ctab, atab, btab)


def _flash_fwd(q, k, v, n_heads, t, gather, out_cols):
    ng = len(gather)
    ck = _tile(t, PREF["attn_chunk"])
    hp = PREF["attn_heads"] if n_heads % PREF["attn_heads"] == 0 else 1
    n_groups = n_heads // hp
    s_len = q.shape[0]
    nb = s_len // t
    pairs = [(i, j) for i in range(nb) for j in range(i + 1)]
    itab = jnp.asarray(np.array([p[0] for p in pairs], np.int32))
    jtab = jnp.asarray(np.array([p[1] for p in pairs], np.int32))

    n_steps = len(pairs)
    fwd_at = max(1, (7 * n_groups * n_steps) // 10)

    def body(it_ref, jt_ref, q_ref, k_ref, v_ref, *rest):
        gin, (o_ref, lse_ref), gout = rest[:ng], rest[ng:ng + 2], rest[ng + 2:2 * ng + 2]
        m_sc, l_sc, acc_sc = rest[2 * ng + 2:2 * ng + 5]
        sems = rest[2 * ng + 5:]
        group, step_id = pl.program_id(0), pl.program_id(1)
        i, j = it_ref[step_id], jt_ref[step_id]

        @pl.when((group == 0) & (step_id == 0))
        def _():
            _gather_start(gin, gout, *sems)

        @pl.when((group == fwd_at // n_steps) & (step_id == fwd_at % n_steps))
        def _():
            _gather_forward(gin, gout, *sems)

        @pl.when(j == 0)
        def _():
            m_sc[...] = jnp.full(m_sc.shape, NEG, F32)
            l_sc[...] = jnp.zeros(l_sc.shape, F32)
            acc_sc[...] = jnp.zeros(acc_sc.shape, F32)

        def step(diag):
            for h in range(hp):
                qk, vc = slice(h * HEAD_W, (h + 1) * HEAD_W), slice(h * V_DIM, (h + 1) * V_DIM)
                cd = min(ck, t // 2) if diag else ck
                for c in range(t // cd):
                    q0 = c * cd if diag else 0
                    qs, ks = slice(q0, t), slice(c * cd, (c + 1) * cd)
                    s_t = lax.dot_general(k_ref[ks, qk], q_ref[qs, qk], NT_DIMS, preferred_element_type=F32)
                    if diag:
                        krow = lax.broadcasted_iota(jnp.int32, s_t.shape, 0)
                        qcol = lax.broadcasted_iota(jnp.int32, s_t.shape, 1)
                        s_t = jnp.where(krow <= qcol, s_t, NEG)
                    m_prev = m_sc[h, :, qs]
                    m_new = jnp.maximum(m_prev, jnp.max(s_t, axis=0, keepdims=True))
                    alpha = jnp.exp2(m_prev - m_new)
                    p_t = jnp.exp2(s_t - m_new)
                    l_sc[h, :, qs] = alpha * l_sc[h, :, qs] + jnp.sum(p_t, axis=0, keepdims=True)
                    acc_sc[h, :, qs] = acc_sc[h, :, qs] * alpha + lax.dot_general(
                        v_ref[ks, vc], p_t.astype(BF16), TN_DIMS, preferred_element_type=F32)
                    m_sc[h, :, qs] = m_new

        @pl.when(j < i)
        def _():
            step(False)

        @pl.when(j == i)
        def _():
            step(True)
            for h in range(hp):
                l = l_sc[h]
                o_ref[:, h * V_DIM:(h + 1) * V_DIM] = jnp.transpose(acc_sc[h] / l).astype(BF16)
                lse_ref[h] = m_sc[h] + jnp.log(l) * LOG2E

        @pl.when((group == n_groups - 1) & (step_id == n_steps - 1))
        def _():
            _gather_finish(gin, gout, *sems)

    hbm = pl.BlockSpec(memory_space=pl.ANY)
    grid_spec = pltpu.PrefetchScalarGridSpec(
        num_scalar_prefetch=2, grid=(n_groups, n_steps),
        in_specs=[pl.BlockSpec((t, hp * HEAD_W), lambda g, s, it, jt: (it[s], g)),
                  pl.BlockSpec((t, hp * HEAD_W), lambda g, s, it, jt: (jt[s], g)),
                  pl.BlockSpec((t, hp * V_DIM), lambda g, s, it, jt: (jt[s], g))] + [hbm] * ng,
        out_specs=[pl.BlockSpec((t, hp * V_DIM), lambda g, s, it, jt: (it[s], g)),
                   pl.BlockSpec((hp, 1, t), lambda g, s, it, jt: (g, 0, it[s]))] + [hbm] * ng,
        scratch_shapes=[pltpu.VMEM((hp, 1, t), F32), pltpu.VMEM((hp, 1, t), F32), pltpu.VMEM((hp, V_DIM, t), F32)]
        + _comm_scratch(ng),
    )
    return pl.pallas_call(
        body, name="flash_fwd", grid_spec=grid_spec,
        out_shape=[jax.ShapeDtypeStruct((s_len, out_cols), BF16),
                   jax.ShapeDtypeStruct((n_heads, 1, s_len), F32)] + _gathered_shapes(gather),
        compiler_params=_cparams("arbitrary", "arbitrary"),
    )(itab, jtab, q, k, v, *gather)


def _gconv_fwd(proj, lay, w, b, ts, tc, mixcat):
    s_len = proj.shape[0]
    cw = w.shape[1]
    nj = cw // tc
    out_off = mixcat.shape[1] - cw

    def body(gb_ref, gc_ref, ci_ref, gch_ref, cih_ref, w_ref, b_ref, mix_in, o_ref):
        i = pl.program_id(1)
        p = gc_ref[...].astype(F32) * ci_ref[...].astype(F32)
        ph = jnp.where(i > 0, gch_ref[...].astype(F32) * cih_ref[...].astype(F32), 0.0)
        o_ref[...] = (gb_ref[...].astype(F32) * _conv3(p, ph, w_ref, b_ref)).astype(BF16)

    def blk(off):
        return pl.BlockSpec((ts, tc), lambda j, i: (i, off // tc + j))

    def halo(off):
        return pl.BlockSpec((SUB, tc), _prev_halo(ts, lambda j: off // tc + j))

    return pl.pallas_call(
        body, name="gconv_fwd", grid=(nj, s_len // ts),
        in_specs=[blk(lay["gb"]), blk(lay["gc"]), blk(lay["ci"]), halo(lay["gc"]), halo(lay["ci"]),
                  pl.BlockSpec((3, tc), lambda j, i: (0, j)), pl.BlockSpec((1, tc), lambda j, i: (0, j)),
                  pl.BlockSpec(memory_space=pl.ANY)],
        out_specs=pl.BlockSpec((ts, tc), lambda j, i: (i, out_off // tc + j)),
        out_shape=jax.ShapeDtypeStruct(mixcat.shape, BF16),
        input_output_aliases={7: 0},
        compiler_params=_cparams("parallel", "parallel"),
    )(proj, proj, proj, proj, proj, w, b, mixcat)


def _post_mix_fwd(mix, x, gt, g_post, g_pre, sc, sh, ts):
    s_len, d = x.shape

    def body(mix_ref, x_ref, gt_ref, gp_ref, g2_ref, sc_ref, sh_ref, x1_ref, h2_ref):
        mv = mix_ref[...].astype(F32)
        x1 = x_ref[...] + gt_ref[...] * (mv * _rsq(mv) * gp_ref[...])
        x1_ref[...] = x1
        h2_ref[...] = ((x1 * _rsq(x1) * g2_ref[...]) * (1.0 + sc_ref[...]) + sh_ref[...]).astype(BF16)

    vec = pl.BlockSpec((1, d), lambda i: (0, 0))
    row = pl.BlockSpec((ts, d), lambda i: (i, 0))
    return pl.pallas_call(
        body, name="post_mix_fwd", grid=(s_len // ts,),
        in_specs=[row, row, vec, vec, vec, vec, vec], out_specs=[row, row],
        out_shape=[jax.ShapeDtypeStruct((s_len, d), F32), jax.ShapeDtypeStruct((s_len, d), BF16)],
        compiler_params=_cparams("parallel"),
    )(mix, x, gt, g_post, g_pre, sc, sh)


def _act_fwd(up, w, b, ts, tc):
    s_len, f2 = up.shape
    nh = f2 // 2 // tc
    rs_n = _tile(ts, STRIP_ROWS, HALO)

    def body(up_ref, uph_ref, w_ref, b_ref, u_ref, o_ref):
        i = pl.program_id(1)

        def conv_strip(r0, lanes):
            if r0 == 0:
                top = jnp.where(i > 0, uph_ref[:, lanes].astype(F32), 0.0)
                xe = jnp.concatenate([top, up_ref[0:rs_n, lanes].astype(F32)], axis=0)
            else:
                xe = up_ref[r0 - HALO:r0 + rs_n, lanes].astype(F32)
            u = (pltpu.roll(xe, 2, 0)[HALO:] * w_ref[0:1, lanes] + pltpu.roll(xe, 1, 0)[HALO:] * w_ref[1:2, lanes]
                 + xe[HALO:] * w_ref[2:3, lanes] + b_ref[:, lanes])
            u_ref[r0:r0 + rs_n, lanes] = u.astype(BF16)
            return u

        for r0 in range(0, ts, rs_n):
            for c0 in range(0, tc, LANE):
                ua = conv_strip(r0, slice(c0, c0 + LANE))
                ug = conv_strip(r0, slice(tc + c0, tc + c0 + LANE))
                o_ref[r0:r0 + rs_n, c0:c0 + LANE] = (ug * _sigmoid(ug) * ua).astype(BF16)

    pair = pl.BlockSpec((ts, 2 * tc), lambda j, i: (i, j))
    return pl.pallas_call(
        body, name="act_fwd", grid=(nh, s_len // ts),
        in_specs=[pair, pl.BlockSpec((HALO, 2 * tc), lambda j, i: (jnp.maximum(i * (ts // HALO) - 1, 0), j)),
                  pl.BlockSpec((3, 2 * tc), lambda j, i: (0, j)), pl.BlockSpec((1, 2 * tc), lambda j, i: (0, j))],
        out_specs=[pair, pl.BlockSpec((ts, tc), lambda j, i: (i, j))],
        out_shape=[jax.ShapeDtypeStruct((s_len, f2), BF16), jax.ShapeDtypeStruct((s_len, f2 // 2), BF16)],
        compiler_params=_cparams("parallel", "parallel"),
    )(up, up, w, b)


def _final_bwd(y, x1, tgt, gt, g_post, ts):
    s_len, d = y.shape

    n_i = s_len // ts

    def body(y_ref, x1_ref, t_ref, gt_ref, g_ref, dy_ref, dx2_ref, loss_ref, dgt_ref, dg_ref):
        i = pl.program_id(0)
        gtg = gt_ref[...] * g_ref[...]
        yv = y_ref[...].astype(F32)
        r = _rsq(yv)
        yh = yv * r
        e = x1_ref[...] + yh * gtg - t_ref[...]
        loss = 0.5 * jnp.sum(jnp.mean(e * e, axis=-1, keepdims=True), axis=0, keepdims=True)
        dx2 = e * (1.0 / d)
        dx2_ref[...] = dx2
        dyh = dx2 * gtg
        dy_ref[...] = (r * (dyh - yh * jnp.mean(dyh * yh, axis=-1, keepdims=True))).astype(BF16)

        @pl.when(i == 0)
        def _():
            loss_ref[...] = jnp.zeros(loss_ref.shape, F32)
            dgt_ref[...] = jnp.zeros(dgt_ref.shape, F32)

        loss_ref[...] += jnp.broadcast_to(loss, loss_ref.shape)
        dgt_ref[...] += _colsum(dx2 * yh)

        @pl.when(i == n_i - 1)
        def _():
            both = dgt_ref[...]
            dg_ref[...] = both * gt_ref[...]
            dgt_ref[...] = both * g_ref[...]

    vec = pl.BlockSpec((1, d), lambda i: (0, 0))
    row = pl.BlockSpec((ts, d), lambda i: (i, 0))
    vshape = jax.ShapeDtypeStruct((1, d), F32)
    return pl.pallas_call(
        body, name="final_bwd", grid=(s_len // ts,),
        in_specs=[row, row, row, vec, vec],
        out_specs=[row, row, pl.BlockSpec((1, LANE), lambda i: (0, 0)), vec, vec],
        out_shape=[jax.ShapeDtypeStruct((s_len, d), BF16), jax.ShapeDtypeStruct((s_len, d), F32),
                   jax.ShapeDtypeStruct((1, LANE), F32), vshape, vshape],
        compiler_params=_cparams("arbitrary"),
    )(y, x1, tgt, gt, g_post)


def _ffn_act_bwd(d_act, u, up, w, ts, tc):
    s_len, f2 = up.shape
    nh = f2 // 2 // tc
    n_i = s_len // ts
    rs_n = _tile(ts, STRIP_ROWS, HALO)
    ext = rs_n + HALO

    def body(d_ref, dh_ref, u_ref, uh_ref, x_ref, w_ref, dx_ref, dw_ref, db_ref, acc):
        i = pl.program_id(1)
        acc[...] = jnp.zeros(acc.shape, F32)

        def below(ref, halo_ref, r0, lanes):
            if r0 + ext <= ts:
                return ref[r0:r0 + ext, lanes].astype(F32)
            bot = jnp.where(i < n_i - 1, halo_ref[:, lanes].astype(F32), 0.0)
            return jnp.concatenate([ref[r0:ts, lanes].astype(F32), bot], axis=0)

        def fold8(v):
            return jnp.sum(v.reshape(v.shape[0] // SUB, SUB, v.shape[1]), axis=0)

        def conv_bwd_strip(du, r0, lanes):
            du1, du2 = pltpu.roll(du, ext - 1, 0)[:rs_n], pltpu.roll(du, ext - 2, 0)[:rs_n]
            du0 = du[:rs_n]
            dx_ref[r0:r0 + rs_n, lanes] = (du0 * w_ref[2:3, lanes] + du1 * w_ref[1:2, lanes]
                                           + du2 * w_ref[0:1, lanes]).astype(BF16)
            xv = x_ref[r0:r0 + rs_n, lanes].astype(F32)
            acc[0, :, lanes] += fold8(du2 * xv)
            acc[1, :, lanes] += fold8(du1 * xv)
            acc[2, :, lanes] += fold8(du0 * xv)
            acc[3, :, lanes] += fold8(du0)

        for r0 in range(0, ts, rs_n):
            for c0 in range(0, tc, LANE):
                la, lg = slice(c0, c0 + LANE), slice(tc + c0, tc + c0 + LANE)
                dv = below(d_ref, dh_ref, r0, la)
                ua, ug = below(u_ref, uh_ref, r0, la), below(u_ref, uh_ref, r0, lg)
                sg = _sigmoid(ug)
                conv_bwd_strip(dv * (ug * sg), r0, la)
                conv_bwd_strip(dv * ua * (sg * (1.0 + ug * (1.0 - sg))), r0, lg)

        @pl.when(i == 0)
        def _():
            dw_ref[...] = jnp.zeros(dw_ref.shape, F32)
            db_ref[...] = jnp.zeros(db_ref.shape, F32)

        dw_ref[...] += jnp.concatenate([_colsum(acc[k]) for k in range(3)], axis=0)
        db_ref[...] += _colsum(acc[3])

    pair = pl.BlockSpec((ts, 2 * tc), lambda j, i: (i, j))

    def nxt(j, i):
        return (jnp.minimum((i + 1) * (ts // HALO), s_len // HALO - 1), j)

    return pl.pallas_call(
        body, name="ffn_act_bwd", grid=(nh, n_i),
        in_specs=[pl.BlockSpec((ts, tc), lambda j, i: (i, j)), pl.BlockSpec((HALO, tc), nxt),
                  pair, pl.BlockSpec((HALO, 2 * tc), nxt), pair, pl.BlockSpec((3, 2 * tc), lambda j, i: (0, j))],
        out_specs=[pair, pl.BlockSpec((3, 2 * tc), lambda j, i: (0, j)), pl.BlockSpec((1, 2 * tc), lambda j, i: (0, j))],
        out_shape=[jax.ShapeDtypeStruct((s_len, f2), BF16), jax.ShapeDtypeStruct((3, f2), F32),
                   jax.ShapeDtypeStruct((1, f2), F32)],
        scratch_shapes=[pltpu.VMEM((4, SUB, 2 * tc), F32)],
        compiler_params=_cparams("parallel", "arbitrary"),
    )(d_act, d_act, u, u, up, w)


def _mid_bwd(dh2, x1, dx2, mix, g_pre, sc, gt_m, g_post, ts):
    s_len, d = x1.shape
    n_i = s_len // ts

    def body(dh_ref, x1_ref, dx2_ref, mix_ref, g_ref, sc_ref, gt_ref, gp_ref,
             dx1_ref, dmix_ref, dsh_ref, dsc_ref, dg_ref, dgt_ref, dgp_ref):
        i = pl.program_id(0)
        wv = (1.0 + sc_ref[...]) * g_ref[...]
        gtg = gt_ref[...] * gp_ref[...]
        dh = dh_ref[...].astype(F32)
        x1 = x1_ref[...]
        r1 = _rsq(x1)
        xh = x1 * r1
        dhx = dh * xh
        dx1 = dx2_ref[...] + r1 * (dh * wv - xh * jnp.mean(dhx * wv, axis=-1, keepdims=True))
        dx1_ref[...] = dx1
        mv = mix_ref[...].astype(F32)
        rm = _rsq(mv)
        mh = mv * rm
        dxm = dx1 * mh
        dmix_ref[...] = (rm * (dx1 * gtg - mh * jnp.mean(dxm * gtg, axis=-1, keepdims=True))).astype(BF16)

        @pl.when(i == 0)
        def _():
            for ref in (dsh_ref, dsc_ref, dgt_ref):
                ref[...] = jnp.zeros(ref.shape, F32)

        dsh_ref[...] += _colsum(dh)
        dsc_ref[...] += _colsum(dhx)
        dgt_ref[...] += _colsum(dxm)

        @pl.when(i == n_i - 1)
        def _():
            t1, t2 = dsc_ref[...], dgt_ref[...]
            dsc_ref[...] = t1 * g_ref[...]
            dg_ref[...] = t1 * (1.0 + sc_ref[...])
            dgt_ref[...] = t2 * gp_ref[...]
            dgp_ref[...] = t2 * gt_ref[...]

    vec = pl.BlockSpec((1, d), lambda i: (0, 0))
    row = pl.BlockSpec((ts, d), lambda i: (i, 0))
    vshape = jax.ShapeDtypeStruct((1, d), F32)
    return pl.pallas_call(
        body, name="mid_bwd", grid=(s_len // ts,),
        in_specs=[row, row, row, row, vec, vec, vec, vec],
        out_specs=[row, row, vec, vec, vec, vec, vec],
        out_shape=[jax.ShapeDtypeStruct((s_len, d), F32), jax.ShapeDtypeStruct((s_len, d), BF16)] + [vshape] * 5,
        compiler_params=_cparams("arbitrary"),
    )(dh2, x1, dx2, mix, g_pre, sc, gt_m, g_post)


def _first_bwd(dh1, x, dx1, g_pre, sc, ts):
    s_len, d = x.shape
    n_i = s_len // ts

    def body(dh_ref, x_ref, dx1_ref, g_ref, sc_ref, dx_ref, dsh_ref, dsc_ref, dg_ref):
        i = pl.program_id(0)
        wv = (1.0 + sc_ref[...]) * g_ref[...]
        dh = dh_ref[...].astype(F32)
        xv = x_ref[...]
        r = _rsq(xv)
        xh = xv * r
        dhx = dh * xh
        dx_ref[...] = dx1_ref[...] + r * (dh * wv - xh * jnp.mean(dhx * wv, axis=-1, keepdims=True))

        @pl.when(i == 0)
        def _():
            for ref in (dsh_ref, dsc_ref):
                ref[...] = jnp.zeros(ref.shape, F32)

        dsh_ref[...] += _colsum(dh)
        dsc_ref[...] += _colsum(dhx)

        @pl.when(i == n_i - 1)
        def _():
            t1 = dsc_ref[...]
            dsc_ref[...] = t1 * g_ref[...]
            dg_ref[...] = t1 * (1.0 + sc_ref[...])

    vec = pl.BlockSpec((1, d), lambda i: (0, 0))
    row = pl.BlockSpec((ts, d), lambda i: (i, 0))
    vshape = jax.ShapeDtypeStruct((1, d), F32)
    return pl.pallas_call(
        body, name="first_bwd", grid=(s_len // ts,),
        in_specs=[row, row, row, vec, vec], out_specs=[row, vec, vec, vec],
        out_shape=[jax.ShapeDtypeStruct((s_len, d), F32), vshape, vshape, vshape],
        compiler_params=_cparams("arbitrary"),
    )(dh1, x, dx1, g_pre, sc)


def _gconv_bwd(d_mixcat, proj, lay, w, b, ts, tc):
    s_len = proj.shape[0]
    cw = w.shape[1]
    n_i = s_len // ts
    dc_off = d_mixcat.shape[1] - cw

    def body(dc_ref, dch_ref, gb_ref, gbh_ref, gc_ref, gch_ref, ci_ref, cih_ref, w_ref, b_ref,
             dgb_ref, dgc_ref, dci_ref, dw_ref, db_ref):
        i = pl.program_id(1)
        gc, ci = gc_ref[...].astype(F32), ci_ref[...].astype(F32)
        p = gc * ci
        ph = jnp.where(i > 0, gch_ref[...].astype(F32) * cih_ref[...].astype(F32), 0.0)
        pm1, pm2 = _shift_down(p, ph, 1), _shift_down(p, ph, 2)
        z = pm2 * w_ref[0:1, :] + pm1 * w_ref[1:2, :] + p * w_ref[2:3, :] + b_ref[...]
        dc = dc_ref[...].astype(F32)
        dgb_ref[...] = (dc * z).astype(BF16)
        dz = dc * gb_ref[...].astype(F32)
        dzh = jnp.where(i < n_i - 1, dch_ref[...].astype(F32) * gbh_ref[...].astype(F32), 0.0)
        dz1, dz2 = _shift_up(dz, dzh, 1), _shift_up(dz, dzh, 2)
        dp = dz * w_ref[2:3, :] + dz1 * w_ref[1:2, :] + dz2 * w_ref[0:1, :]
        dgc_ref[...] = (dp * ci).astype(BF16)
        dci_ref[...] = (dp * gc).astype(BF16)

        @pl.when(i == 0)
        def _():
            dw_ref[...] = jnp.zeros(dw_ref.shape, F32)
            db_ref[...] = jnp.zeros(db_ref.shape, F32)

        dw_ref[0:1, :] += _colsum(dz2 * p)
        dw_ref[1:2, :] += _colsum(dz1 * p)
        dw_ref[2:3, :] += _colsum(dz * p)
        db_ref[...] += _colsum(dz)

    def blk(off):
        return pl.BlockSpec((ts, tc), lambda j, i: (i, off // tc + j))

    def prev(off):
        return pl.BlockSpec((SUB, tc), _prev_halo(ts, lambda j: off // tc + j))

    def nxt(off):
        return pl.BlockSpec((SUB, tc), _next_halo(ts, s_len, lambda j: off // tc + j))

    out_blk = pl.BlockSpec((ts, tc), lambda j, i: (i, j))
    act = jax.ShapeDtypeStruct((s_len, cw), BF16)
    return pl.pallas_call(
        body, name="gconv_bwd", grid=(cw // tc, n_i),
        in_specs=[blk(dc_off), nxt(dc_off), blk(lay["gb"]), nxt(lay["gb"]), blk(lay["gc"]), prev(lay["gc"]),
                  blk(lay["ci"]), prev(lay["ci"]),
                  pl.BlockSpec((3, tc), lambda j, i: (0, j)), pl.BlockSpec((1, tc), lambda j, i: (0, j))],
        out_specs=[out_blk, out_blk, out_blk,
                   pl.BlockSpec((3, tc), lambda j, i: (0, j)), pl.BlockSpec((1, tc), lambda j, i: (0, j))],
        out_shape=[act, act, act, jax.ShapeDtypeStruct((3, cw), F32), jax.ShapeDtypeStruct((1, cw), F32)],
        compiler_params=_cparams("parallel", "arbitrary"),
    )(d_mixcat, d_mixcat, proj, proj, proj, proj, proj, proj, w, b)


def _delta(o, d_mixcat, n_heads, ts):
    s_len = o.shape[0]

    def body(o_ref, do_ref, out_ref):
        for h in range(n_heads):
            sl = slice(h * V_DIM, (h + 1) * V_DIM)
            prod = o_ref[:, sl].astype(F32) * do_ref[:, sl].astype(F32)
            out_ref[h] = jnp.broadcast_to(jnp.sum(prod, axis=1, keepdims=True), (ts, LANE))

    hv = n_heads * V_DIM
    return pl.pallas_call(
        body, name="attn_delta", grid=(s_len // ts,),
        in_specs=[pl.BlockSpec((ts, hv), lambda i: (i, 0)), pl.BlockSpec((ts, hv), lambda i: (i, 0))],
        out_specs=pl.BlockSpec((n_heads, ts, LANE), lambda i: (0, i, 0)),
        out_shape=jax.ShapeDtypeStruct((n_heads, s_len, LANE), F32),
        compiler_params=_cparams("parallel"),
    )(o, d_mixcat)


def _flash_bwd(q, k, v, d_mixcat, lse_row, delta_row, n_heads, t, scale, exchange):
    nx = len(exchange)
    s_len = q.shape[0]
    nb = s_len // t
    pairs = [(j, i) for j in range(nb) for i in range(j, nb)]
    jtab = jnp.asarray(np.array([p[0] for p in pairs], np.int32))
    itab = jnp.asarray(np.array([p[1] for p in pairs], np.int32))
    n_steps = len(pairs)

    def body(jt_ref, it_ref, q_ref, k_ref, v_ref, do_ref, lse_ref, dl_ref, *rest):
        xin, (dq_ref, dk_ref, dv_ref), xout = rest[:nx], rest[nx:nx + 3], rest[nx + 3:2 * nx + 3]
        dq_acc, dk_acc, dv_acc = rest[2 * nx + 3:2 * nx + 6]
        sems = rest[2 * nx + 6:]
        head, step_id = pl.program_id(0), pl.program_id(1)
        j, i = jt_ref[step_id], it_ref[step_id]

        @pl.when((head == 0) & (step_id == 0))
        def _():
            _exchange_start(xin, xout, *sems)

        @pl.when(step_id == 0)
        def _():
            dq_acc[...] = jnp.zeros(dq_acc.shape, F32)

        @pl.when(i == j)
        def _():
            dk_acc[...] = jnp.zeros(dk_acc.shape, F32)
            dv_acc[...] = jnp.zeros(dv_acc.shape, F32)

        def step(diag):
            half = t // 2
            lo, hi = slice(0, half), slice(half, t)
            blocks = [(lo, lo, True), (lo, hi, False), (hi, hi, True)] if diag else [(slice(0, t), slice(0, t), False)]
            for ks, qs, masked in blocks:
                qv, kv, vv, dov = q_ref[qs, :], k_ref[ks, :], v_ref[ks, :], do_ref[qs, :]
                s_t = lax.dot_general(kv, qv, NT_DIMS, preferred_element_type=F32)
                if masked:
                    krow = lax.broadcasted_iota(jnp.int32, s_t.shape, 0)
                    qcol = lax.broadcasted_iota(jnp.int32, s_t.shape, 1)
                    s_t = jnp.where(krow <= qcol, s_t, NEG)
                p_t = jnp.exp2(s_t - lse_ref[0, :, qs])
                dv_acc[ks, :] += jnp.dot(p_t.astype(BF16), dov, preferred_element_type=F32)
                dp_t = lax.dot_general(vv, dov, NT_DIMS, preferred_element_type=F32)
                ds_t = (p_t * (dp_t - dl_ref[0, :, qs])).astype(BF16)
                dk_acc[ks, :] += jnp.dot(ds_t, qv, preferred_element_type=F32)
                n_q = qs.stop - qs.start
                rows = pl.ds(pl.multiple_of(i * t + qs.start, n_q), n_q)
                dq_acc[rows, :] += lax.dot_general(ds_t, kv, TN_DIMS, preferred_element_type=F32)

        @pl.when(i > j)
        def _():
            step(False)

        @pl.when(i == j)
        def _():
            step(True)

        @pl.when(i == nb - 1)
        def _():
            dk_ref[...] = (dk_acc[...] * LN2).astype(BF16)
            dv_ref[...] = dv_acc[...].astype(BF16)

        @pl.when(step_id == n_steps - 1)
        def _():
            dq_ref[...] = (dq_acc[...] * scale).astype(BF16)

        @pl.when((head == n_heads - 1) & (step_id == n_steps - 1))
        def _():
            _exchange_finish(xin, xout, *sems)

    hbm = pl.BlockSpec(memory_space=pl.ANY)
    hv = n_heads * V_DIM
    do_off = 0
    grid_spec = pltpu.PrefetchScalarGridSpec(
        num_scalar_prefetch=2, grid=(n_heads, n_steps),
        in_specs=[pl.BlockSpec((t, HEAD_W), lambda h, s, jt, it: (it[s], h)),
                  pl.BlockSpec((t, HEAD_W), lambda h, s, jt, it: (jt[s], h)),
                  pl.BlockSpec((t, V_DIM), lambda h, s, jt, it: (jt[s], h)),
                  pl.BlockSpec((t, V_DIM), lambda h, s, jt, it: (it[s], do_off + h)),
                  pl.BlockSpec((1, 1, t), lambda h, s, jt, it: (h, 0, it[s])),
                  pl.BlockSpec((1, 1, t), lambda h, s, jt, it: (h, 0, it[s]))] + [hbm] * nx,
        out_specs=[pl.BlockSpec((s_len, HEAD_W), lambda h, s, jt, it: (0, h)),
                   pl.BlockSpec((t, HEAD_W), lambda h, s, jt, it: (jt[s], h)),
                   pl.BlockSpec((t, V_DIM), lambda h, s, jt, it: (jt[s], h))] + [hbm] * nx,
        scratch_shapes=[pltpu.VMEM((s_len, HEAD_W), F32), pltpu.VMEM((t, HEAD_W), F32), pltpu.VMEM((t, V_DIM), F32)]
        + _comm_scratch(nx),
    )
    return pl.pallas_call(
        body, name="flash_bwd", grid_spec=grid_spec,
        out_shape=[jax.ShapeDtypeStruct((s_len, n_heads * HEAD_W), BF16),
                   jax.ShapeDtypeStruct((s_len, n_heads * HEAD_W), BF16),
                   jax.ShapeDtypeStruct((s_len, hv), BF16)] + [jax.ShapeDtypeStruct(c.shape, c.dtype) for c in exchange],
        compiler_params=_cparams("arbitrary", "arbitrary"),
    )(jtab, itab, q, k, v, d_mixcat, lse_row, delta_row, *exchange)


def _qkv_bwd(dq, dk, dv, proj, lay, wuq, wk, wv, g_q, g_kv, ctab, atab, btab, n_heads, ts):
    s_len = proj.shape[0]
    ql_w, kl_w = wuq.shape[0], wk.shape[0]
    tail_w = lay["np"] - lay["ql"]
    kv_o, kr_o = lay["kv"] - lay["ql"], lay["kr"] - lay["ql"]

    def body(dq_ref, dk_ref, dv_ref, ql_ref, kl_ref, wuq_ref, wk_ref, wv_ref, gq_ref, gkv_ref, c_ref, a_ref, b_ref,
             dqr_ref, dkn_ref, tail_ref, dgq_ref, dgkv_ref):
        i = pl.program_id(0)
        c, a, b = c_ref[...], a_ref[...], b_ref[...]
        dkr = jnp.zeros((ts, LANE), F32)
        for h in range(n_heads):
            o = h * HEAD_W
            dqr_ref[:, o:o + QK_NOPE] = dq_ref[:, o:o + QK_NOPE]
            dqr_ref[:, o + QK_NOPE:o + HEAD_W] = _rope_t(dq_ref[:, o + QK_NOPE:o + HEAD_W].astype(F32), c, a, b).astype(BF16)
            dkn_ref[:, h * QK_NOPE:(h + 1) * QK_NOPE] = dk_ref[:, o:o + QK_NOPE]
            dkr = dkr + dk_ref[:, o + QK_NOPE:o + HEAD_W].astype(F32)
        tail_ref[...] = jnp.zeros(tail_ref.shape, BF16)
        tail_ref[:, kr_o:kr_o + LANE] = _rope_t(dkr, c, a, b).astype(BF16)

        def rms_bwd(lat_ref, dn, g_ref):
            lat = lat_ref[...].astype(F32)
            r = _rsq(lat)
            xh = lat * r
            dxh = dn * g_ref[...]
            return r * (dxh - xh * jnp.mean(dxh * xh, axis=-1, keepdims=True)), _colsum(dn * xh)

        dqn = lax.dot_general(dqr_ref[...], wuq_ref[...], NT_DIMS, preferred_element_type=F32)
        d_ql, dgq = rms_bwd(ql_ref, dqn, gq_ref)
        tail_ref[:, 0:ql_w] = d_ql.astype(BF16)
        dkvn = (lax.dot_general(dkn_ref[...], wk_ref[...], NT_DIMS, preferred_element_type=F32)
                + lax.dot_general(dv_ref[...], wv_ref[...], NT_DIMS, preferred_element_type=F32))
        d_kl, dgkv = rms_bwd(kl_ref, dkvn, gkv_ref)
        tail_ref[:, kv_o:kv_o + kl_w] = d_kl.astype(BF16)

        @pl.when(i == 0)
        def _():
            dgq_ref[...] = jnp.zeros(dgq_ref.shape, F32)
            dgkv_ref[...] = jnp.zeros(dgkv_ref.shape, F32)

        dgq_ref[...] += dgq
        dgkv_ref[...] += dgkv

    def full(arr):
        return pl.BlockSpec(arr.shape, lambda i: (0, 0))

    def rows(w):
        return pl.BlockSpec((ts, w), lambda i: (i, 0))

    tab = pl.BlockSpec((ts, LANE), lambda i: (i, 0))
    hw, hv, hn = n_heads * HEAD_W, n_heads * V_DIM, n_heads * QK_NOPE
    return pl.pallas_call(
        body, name="qkv_bwd", grid=(s_len // ts,),
        in_specs=[rows(hw), rows(hw), rows(hv),
                  pl.BlockSpec((ts, ql_w), lambda i: (i, lay["ql"] // ql_w)),
                  pl.BlockSpec((ts, kl_w), lambda i: (i, lay["kv"] // kl_w)),
                  full(wuq), full(wk), full(wv), full(g_q), full(g_kv), tab, tab, tab],
        out_specs=[rows(hw), rows(hn), rows(tail_w), full(g_q), full(g_kv)],
        out_shape=[jax.ShapeDtypeStruct((s_len, hw), BF16), jax.ShapeDtypeStruct((s_len, hn), BF16),
                   jax.ShapeDtypeStruct((s_len, tail_w), BF16),
                   jax.ShapeDtypeStruct(g_q.shape, F32), jax.ShapeDtypeStruct(g_kv.shape, F32)],
        compiler_params=_cparams("arbitrary"),
    )(dq, dk, dv, proj, proj, wuq, wk, wv, g_q, g_kv, ctab, atab, btab)


def _adamw(w, g, m, v):
    m = ADAM_B1 * m + (1.0 - ADAM_B1) * g
    v = ADAM_B2 * v + (1.0 - ADAM_B2) * (g * g)
    m_hat = m / (1.0 - ADAM_B1 ** ADAM_STEP)
    v_hat = v / (1.0 - ADAM_B2 ** ADAM_STEP)
    delta = -ADAM_LR * (m_hat / (jnp.sqrt(v_hat) + ADAM_EPS) + ADAM_WD * w)
    return delta, m, v


def _adam_parts(parts, w, m, v, tr, name, small=None):
    r, c = w.shape
    tr = _tile(r, tr, SUB)
    n_i = r // tr
    ns = 0 if small is None else 1

    def body(p_ref, w_ref, m_ref, v_ref, *rest):
        vec_ref = rest[:ns]
        g_out, d_out, m_out, v_out = rest[ns:ns + 4]
        gath_ref, sems = rest[ns + 4:2 * ns + 4], rest[2 * ns + 4:]
        i = pl.program_id(0)

        def small_copies():
            send_s, recv_s, local_s = sems
            me = _my_place()
            my_i = _index(me)

            def copy(k, to, src_row):
                row = gath_ref[0].at[pl.ds(src_row, 1), :]
                return pltpu.make_async_remote_copy(src_ref=row, dst_ref=row, send_sem=send_s.at[k], recv_sem=recv_s.at[k],
                                                    device_id=to, device_id_type=MESH)

            own = pltpu.make_async_copy(vec_ref[0], gath_ref[0].at[pl.ds(my_i, 1), :], local_s)
            sends = [copy(k, _peer(me, k), my_i) for k in range(1, N_DEV)]
            recvs = [copy(k, me, _index(_peer(me, k))) for k in range(1, N_DEV)]
            return own, sends, recvs

        if ns:
            @pl.when(i == 0)
            def _():
                own, sends, _ = small_copies()
                own.start()
                own.wait()
                for cp in sends:
                    cp.start()

        g = p_ref[0].astype(F32)
        for dev in range(1, N_DEV):
            g = g + p_ref[dev].astype(F32)
        g_out[...] = g
        d_out[...], m_out[...], v_out[...] = _adamw(w_ref[...], g, m_ref[...], v_ref[...])

        if ns:
            @pl.when(i == n_i - 1)
            def _():
                _, sends, recvs = small_copies()
                for cp in recvs:
                    cp.wait_recv()
                for cp in sends:
                    cp.wait_send()

    blk = pl.BlockSpec((tr, c), lambda i: (i, 0))
    hbm = pl.BlockSpec(memory_space=pl.ANY)
    shp = jax.ShapeDtypeStruct((r, c), F32)
    out = pl.pallas_call(
        body, name=name, grid=(n_i,),
        in_specs=[pl.BlockSpec((N_DEV, tr, c), lambda i: (0, i, 0)), blk, blk, blk] + [hbm] * ns,
        out_specs=[blk, blk, blk, blk] + [hbm] * ns,
        out_shape=[shp, shp, shp, shp] + ([jax.ShapeDtypeStruct((N_DEV, small.shape[1]), F32)] if ns else []),
        scratch_shapes=([pltpu.SemaphoreType.DMA((N_DEV,))] * 2 + [pltpu.SemaphoreType.DMA]) if ns else [],
        compiler_params=_cparams("arbitrary" if ns else "parallel"),
    )(parts, w, m, v, *([small] if ns else []))
    return (out[:4], out[4]) if ns else out


def _adam_ada(cact_t, dmod_sh, w, m, v, tr):
    r, c = w.shape

    def body(ct_ref, dm_ref, w_ref, m_ref, v_ref, g_out, d_out, m_out, v_out):
        g = jnp.dot(ct_ref[...], dm_ref[...], preferred_element_type=F32, precision=lax.Precision.HIGHEST)
        g_out[...] = g
        d_out[...], m_out[...], v_out[...] = _adamw(w_ref[...], g, m_ref[...], v_ref[...])

    blk = pl.BlockSpec((tr, c), lambda i: (i, 0))
    shp = jax.ShapeDtypeStruct((r, c), F32)
    return pl.pallas_call(
        body, name="adam_ada", grid=(r // tr,),
        in_specs=[pl.BlockSpec((tr, N_DEV), lambda i: (i, 0)), pl.BlockSpec((N_DEV, c), lambda i: (0, 0)), blk, blk, blk],
        out_specs=[blk, blk, blk, blk], out_shape=[shp, shp, shp, shp],
        compiler_params=_cparams("parallel"),
    )(cact_t, dmod_sh, w, m, v)


def _adam_small(v_all, offs, ws, ms, vs):
    n_par = len(ws)

    def body(p_ref, *refs):
        w_refs, m_refs, v_refs = refs[:n_par], refs[n_par:2 * n_par], refs[2 * n_par:3 * n_par]
        sum_ref = refs[3 * n_par]
        outs = refs[3 * n_par + 1:]
        g = p_ref[0:1, :]
        for dev in range(1, N_DEV):
            g = g + p_ref[dev:dev + 1, :]
        sum_ref[...] = g
        for p in range(n_par):
            n = w_refs[p].shape[1]
            gp = sum_ref[:, offs[p]:offs[p] + n]
            outs[p][...] = gp
            (outs[n_par + p][...], outs[2 * n_par + p][...], outs[3 * n_par + p][...]) = _adamw(
                w_refs[p][...], gp, m_refs[p][...], v_refs[p][...])

    vm = pl.BlockSpec(memory_space=pltpu.VMEM)
    shapes = [jax.ShapeDtypeStruct(w.shape, F32) for w in ws]
    out = pl.pallas_call(
        body, name="adam_small", in_specs=[vm] * (1 + 3 * n_par), out_specs=[vm] * (1 + 4 * n_par),
        out_shape=[jax.ShapeDtypeStruct((1, v_all.shape[1]), F32)] + shapes * 4, compiler_params=_cparams(),
    )(v_all, *ws, *ms, *vs)
    return out[0], [out[1 + k * n_par:1 + (k + 1) * n_par] for k in range(4)]


def _my_place():
    return lax.axis_index("x"), lax.axis_index("y"), lax.axis_index("c")


def _peer(place, k):
    x, y, c = place
    return (x ^ (k >> 2), y ^ ((k >> 1) & 1), c ^ (k & 1))


def _index(place):
    return 4 * place[0] + 2 * place[1] + place[2]


def _ada_fwd(vec, w_ada, b_ada_rows):
    lv = vec.shape[1]
    d, c = w_ada.shape

    def body(vec_ref, w_ref, b_ref, gath_ref, cact_ref, mod_ref, modsh, send_a, recv_a, send_b, recv_b, local_s):
        me = _my_place()
        my_i = _index(me)

        def gather_copy(k, to, src_row):
            row = gath_ref.at[pl.ds(src_row, 1), :]
            return pltpu.make_async_remote_copy(src_ref=row, dst_ref=row, send_sem=send_a.at[k], recv_sem=recv_a.at[k],
                                                device_id=to, device_id_type=MESH)

        own = pltpu.make_async_copy(vec_ref, gath_ref.at[pl.ds(my_i, 1), :], local_s.at[0])
        own.start()
        own.wait()
        sends = [gather_copy(k, _peer(me, k), my_i) for k in range(1, N_DEV)]
        for cp in sends:
            cp.start()
        for k in range(1, N_DEV):
            gather_copy(k, me, _index(_peer(me, k))).wait_recv()
        for cp in sends:
            cp.wait_send()

        c_all = gath_ref[:, 0:d]
        cact = c_all * _sigmoid(c_all)
        cact_ref[...] = cact
        modsh[...] = jnp.dot(cact, w_ref[...], preferred_element_type=F32, precision=lax.Precision.HIGHEST)

        def mod_copy(k, to, src_row, dst_row):
            return pltpu.make_async_remote_copy(src_ref=modsh.at[pl.ds(src_row, 1), :], dst_ref=mod_ref.at[pl.ds(dst_row, 1), :],
                                                send_sem=send_b.at[k], recv_sem=recv_b.at[k],
                                                device_id=to, device_id_type=MESH)

        own = pltpu.make_async_copy(modsh.at[pl.ds(my_i, 1), :], mod_ref.at[pl.ds(my_i, 1), :], local_s.at[1])
        own.start()
        sends = [mod_copy(k, _peer(me, k), _index(_peer(me, k)), my_i) for k in range(1, N_DEV)]
        for cp in sends:
            cp.start()
        for k in range(1, N_DEV):
            mod_copy(k, me, my_i, _index(_peer(me, k))).wait_recv()
        for cp in sends:
            cp.wait_send()
        own.wait()
        mod_ref[...] = mod_ref[...] + b_ref[...]

    vm = pl.BlockSpec(memory_space=pltpu.VMEM)
    return pl.pallas_call(
        body, name="ada_fwd", in_specs=[vm, vm, vm], out_specs=[vm, vm, vm],
        out_shape=[jax.ShapeDtypeStruct((N_DEV, lv), F32), jax.ShapeDtypeStruct((N_DEV, d), F32),
                   jax.ShapeDtypeStruct((N_DEV, c), F32)],
        scratch_shapes=[pltpu.VMEM((N_DEV, c), F32)] + [pltpu.SemaphoreType.DMA((N_DEV,))] * 4
        + [pltpu.SemaphoreType.DMA((2,))],
        compiler_params=pltpu.CompilerParams(vmem_limit_bytes=VMEM_LIMIT),
    )(vec, w_ada, b_ada_rows)


PER = N_DEV - 1


def _comm_scratch(n):
    return [pltpu.SemaphoreType.DMA((n * PER,)), pltpu.SemaphoreType.DMA((n * PER,)), pltpu.SemaphoreType.DMA((n,))]


def _gather_copies(ins, outs, send_s, recv_s, local_s):
    n = len(ins)
    me = _my_place()
    x, y, c = me
    sibling = (x, y, 1 - c)
    chips = [(1 - x, y), (x, 1 - y), (1 - x, 1 - y)]

    def copy(a, k, block, to, src=None):
        slot = outs[a].at[_index(block)]
        return pltpu.make_async_remote_copy(src_ref=slot if src is None else src, dst_ref=slot,
                                            send_sem=send_s.at[a * PER + k], recv_sem=recv_s.at[a * PER + k],
                                            device_id=to, device_id_type=MESH)

    mine = [pltpu.make_async_copy(ins[a], outs[a].at[_index(me)], local_s.at[a]) for a in range(n)]
    first = []
    for a in range(n):
        first.append(copy(a, 0, me, sibling, src=ins[a]))
        first += [copy(a, 1 + j, me, (*chip, c), src=ins[a]) for j, chip in enumerate(chips)]
    landed = [copy(a, 1 + j, (*chip, c), me) for j, chip in enumerate(chips) for a in range(n)]
    passed = [copy(a, 4 + j, (*chip, c), sibling) for j, chip in enumerate(chips) for a in range(n)]
    from_sibling = [copy(a, 0, sibling, me) for a in range(n)]
    from_sibling += [copy(a, 4 + j, (*chip, 1 - c), me) for a in range(n) for j, chip in enumerate(chips)]
    return mine, first, landed, passed, from_sibling


def _gather_start(*refs):
    mine, first, _, _, _ = _gather_copies(*refs)
    for cp in mine + first:
        cp.start()


def _gather_forward(*refs):
    _, _, landed, passed, _ = _gather_copies(*refs)
    for got, fwd in zip(landed, passed):
        got.wait_recv()
        fwd.start()


def _gather_finish(*refs):
    mine, first, _, passed, from_sibling = _gather_copies(*refs)
    for cp in from_sibling:
        cp.wait_recv()
    for cp in first + passed:
        cp.wait_send()
    for cp in mine:
        cp.wait()


def _exchange_copies(ins, outs, send_s, recv_s, local_s):
    n = len(ins)
    me = _my_place()
    my_i = _index(me)

    def copy(a, k, to, src_slot, dst_slot):
        return pltpu.make_async_remote_copy(src_ref=ins[a].at[src_slot], dst_ref=outs[a].at[dst_slot],
                                            send_sem=send_s.at[a * PER + k - 1], recv_sem=recv_s.at[a * PER + k - 1],
                                            device_id=to, device_id_type=MESH)

    mine = [pltpu.make_async_copy(ins[a].at[my_i], outs[a].at[my_i], local_s.at[a]) for a in range(n)]
    sends = [copy(a, k, _peer(me, k), _index(_peer(me, k)), my_i) for k in range(1, N_DEV) for a in range(n)]
    recvs = [copy(a, k, me, my_i, _index(_peer(me, k))) for k in range(1, N_DEV) for a in range(n)]
    return mine, sends, recvs


def _exchange_start(*refs):
    mine, sends, _ = _exchange_copies(*refs)
    for cp in mine + sends:
        cp.start()


def _exchange_finish(*refs):
    mine, sends, recvs = _exchange_copies(*refs)
    for cp in recvs:
        cp.wait_recv()
    for cp in sends:
        cp.wait_send()
    for cp in mine:
        cp.wait()


def _gathered_shapes(shards):
    return [jax.ShapeDtypeStruct((N_DEV,) + s.shape, s.dtype) for s in shards]


def _proj_layout(cw, ql, kl):
    lay = {"gb": 0, "gc": cw, "ci": 2 * cw, "ql": 3 * cw}
    assert lay["ql"] % ql == 0
    lay["kv"] = _roundup(lay["ql"] + ql, kl)
    lay["kr"] = lay["kv"] + kl
    lay["np"] = _roundup(lay["kr"] + LANE, 4 * LANE)
    return lay


def _chunks_cols(g):
    r, c8 = g.shape
    return jnp.transpose(g.reshape(r, N_DEV, c8 // N_DEV), (1, 0, 2))


def _from_col_shards(a):
    n, r, c = a.shape
    return jnp.transpose(a, (1, 0, 2)).reshape(r, n * c)


def kernel(x, c, positions, w_ada, b_ada, g_pre_mix, g_post_mix, w_in, g_q, w_uq, g_kv, w_ukv, conv_w_mix, conv_b_mix, w_o, g_pre_ffn, g_post_ffn, w_up, conv_w_ffn, conv_b_ffn, w_down, loss_target, m_w_ada, m_b_ada, m_g_pre_mix, m_g_post_mix, m_w_in, m_g_q, m_w_uq, m_g_kv, m_w_ukv, m_conv_w_mix, m_conv_b_mix, m_w_o, m_g_pre_ffn, m_g_post_ffn, m_w_up, m_conv_w_ffn, m_conv_b_ffn, m_w_down, v_w_ada, v_b_ada, v_g_pre_mix, v_g_post_mix, v_w_in, v_g_q, v_w_uq, v_g_kv, v_w_ukv, v_conv_w_mix, v_conv_b_mix, v_w_o, v_g_pre_ffn, v_g_post_ffn, v_w_up, v_conv_w_ffn, v_conv_b_ffn, v_w_down):
    s_len, d = x.shape[1], x.shape[2]
    ql, kl = w_uq.shape[1], w_ukv.shape[1]
    n_heads = w_ukv.shape[2] * N_DEV // (QK_NOPE + V_DIM)
    cw = conv_w_mix.shape[2] * N_DEV
    f2 = w_up.shape[2] * N_DEV
    ff = f2 // 2
    in_cols = w_in.shape[2] * N_DEV
    ada_c = w_ada.shape[2]
    cwm_c, cwf_c = conv_w_mix.shape[2], conv_w_ffn.shape[2]
    scale = 1.0 / math.sqrt(QK_NOPE + QK_ROPE)
    lay = _proj_layout(cw, ql, kl)
    n_pad = lay["np"]
    my_i = _index(_my_place())

    ts_row = _tile(s_len, PREF["row"], SUB)
    ts_conv = _tile(s_len, PREF["conv_rows"], SUB)
    tc_conv = _tile(cw, PREF["conv_cols"])
    tc_ffn = cwf_c
    ts_ffn = _tile(s_len, PREF["ffn_rows"], SUB)
    ts_act = _tile(s_len, PREF["act_rows"], SUB)
    ts_big = _tile(s_len, PREF["row_big"], SUB)
    pair_order = [k // 2 + (k % 2) * (N_DEV // 2) for k in range(N_DEV)]
    pair_place = [pair_order.index(k) for k in range(N_DEV)]

    def paired(shards):
        return _from_col_shards(jnp.stack([shards[p] for p in pair_order]))

    def unpaired_chunks(g):
        ch = _chunks_cols(g)
        return jnp.stack([ch[p] for p in pair_place])
    ts_qkv = _tile(s_len, PREF["row_big"], SUB)
    t_attn = _tile(s_len, PREF["attn"])

    x2d, tgt = x[0], loss_target[0]

    vec = jnp.concatenate([c, conv_w_mix[0].reshape(1, -1), conv_w_ffn[0].reshape(1, -1)], axis=1)
    gath, cact, mod_rows = _ada_fwd(vec, w_ada[0], b_ada.reshape(N_DEV, ada_c))
    cwm_full = _from_col_shards(gath[:, d:d + 3 * cwm_c].reshape(N_DEV, 3, cwm_c))
    cwf_shards = gath[:, d + 3 * cwm_c:].reshape(N_DEV, 3, cwf_c)
    cwf_pair = paired(cwf_shards)
    cbf_pair = paired(jnp.transpose(conv_b_ffn.reshape(1, N_DEV, cwf_c), (1, 0, 2)))
    mod = mod_rows.reshape(1, N_DEV * ada_c)
    sh_m, sc_m, gt_m, sh_f, sc_f, gt_f = [mod[:, k * d:(k + 1) * d] for k in range(6)]

    h1, g_in = _modnorm_fwd(x2d, g_pre_mix, sc_m, sh_m, ts_row, [w_in[0].astype(BF16)])
    win = _from_col_shards(g_in)
    cut = np.cumsum([0, ql, kl, QK_ROPE, cw, cw, cw])
    part = [win[:, cut[k]:cut[k + 1]] for k in range(6)]

    def zcols(n):
        return jnp.zeros((d, n), BF16)

    win_p = jnp.concatenate([part[3], part[4], part[5], part[0], zcols(lay["kv"] - lay["ql"] - ql), part[1],
                             part[2], zcols(n_pad - lay["kr"] - QK_ROPE)], axis=1)
    inv_freq =1.0 / (ROPE_THETA ** (jnp.arange(0, QK_ROPE, 2, dtype=F32) / QK_ROPE))
    inv_row = jnp.tile(inv_freq, LANE // (QK_ROPE // 2)).reshape(1, LANE)
    ctab, atab, btab = _rope_tables(positions.astype(F32).reshape(s_len, 1), inv_row, _tile(s_len, 1024, SUB))

    proj, g_uq, g_ukv = _matmul(h1, win_p, out_dtype=BF16, tm=2048, tn=1280, tk=2048, name="mm_proj",
                                gather=[w_uq[0].astype(BF16), w_ukv[0].astype(BF16)])
    wuq_p = jnp.pad(_from_col_shards(g_uq).reshape(ql, n_heads, QK_NOPE + QK_ROPE),
                    ((0, 0), (0, 0), (0, HEAD_W - QK_NOPE - QK_ROPE))).reshape(ql, n_heads * HEAD_W)
    wukv = _from_col_shards(g_ukv).reshape(kl, n_heads, QK_NOPE + V_DIM)
    wk = wukv[:, :, :QK_NOPE].reshape(kl, n_heads * QK_NOPE)
    wv = wukv[:, :, QK_NOPE:].reshape(kl, n_heads * V_DIM)
    q, k, v, qn, kvn = _qkv_fwd(proj, lay, wuq_p, wk, wv, g_q, g_kv, ctab, atab, btab, n_heads, ts_qkv, scale * LOG2E)
    attn, lse_row, g_o, g_up = _flash_fwd(q, k, v, n_heads, t_attn, [w_o[0].astype(BF16), w_up[0].astype(BF16)], d)
    wo = g_o.reshape(d, d)

    def pair_shard(j):
        return j // 2 + (j % 2) * (N_DEV // 2)
    mixcat = _gconv_fwd(proj, lay, cwm_full, conv_b_mix, ts_conv, tc_conv, attn)
    mix = _matmul(mixcat, wo, out_dtype=BF16, tm=1024, tn=2048, tk=2048, name="mm_mix")
    x1, h2 = _post_mix_fwd(mix, x2d, gt_m, g_post_mix, g_pre_ffn, sc_f, sh_f, ts_big)
    up, g_down = _matmul(h2, g_up, out_dtype=BF16, tm=2048, tn=cwf_c, tk=2048, name="mm_up", b_shard_of=pair_shard,
                         gather=[w_down[0].astype(BF16)])
    wdown = g_down.reshape(ff, d)
    u, act = _act_fwd(up, cwf_pair, cbf_pair, ts_act, tc_ffn)
    y = _matmul(act, wdown, out_dtype=BF16, tm=1024, tn=512, tk=ff, name="mm_down")

    dy, dx2, loss_row, d_gt_f, dg_post_ffn = _final_bwd(y, x1, tgt, gt_f, g_post_ffn, ts_big)
    gw_down = _matmul(act, dy, ta=True, out_dtype=BF16, tm=512, tn=512, tk=s_len, name="mm_gw_down")
    d_act = _matmul(dy, wdown, tb=True, out_dtype=BF16, tm=2048, tn=1408, tk=2048, name="mm_d_act")
    d_up, dcw_pair, dcb_pair = _ffn_act_bwd(d_act, u, up, cwf_pair, ts_ffn, tc_ffn)
    dcb_ffn = _from_col_shards(unpaired_chunks(dcb_pair))
    gw_up, p_down = _matmul(h2, d_up, ta=True, out_dtype=BF16, tm=512, tn=cwf_c, tk=4096, name="mm_gw_up",
                            exchange=[gw_down.reshape(N_DEV, ff // N_DEV, d)], out_shard_of=pair_shard)
    dh2 = _matmul_pair_shards(d_up, g_up, out_dtype=BF16, tm=1024, tn=1024, name="mm_dh2")
    dx1, dmix, d_sh_f, d_sc_f, dg_pre_ffn, d_gt_m, dg_post_mix = _mid_bwd(
        dh2, x1, dx2, mix, g_pre_ffn, sc_f, gt_m, g_post_mix, ts_row)
    gw_o = _matmul(mixcat, dmix, ta=True, out_dtype=BF16, tm=512, tn=512, tk=s_len, name="mm_gw_o")
    d_mixcat = _matmul(dmix, wo, tb=True, out_dtype=BF16, tm=2048, tn=1024, tk=2048, name="mm_d_mixcat")
    d_gb, d_gc, d_ci, dcw_mix, dcb_mix = _gconv_bwd(d_mixcat, proj, lay, cwm_full, conv_b_mix, ts_conv, tc_conv)
    delta = _delta(mixcat, d_mixcat, n_heads, _tile(s_len, 512, SUB))
    delta_row = delta[:, :, 0].reshape(n_heads, 1, s_len)
    dq, dk, dv, p_up, p_o = _flash_bwd(q, k, v, d_mixcat, lse_row, delta_row, n_heads, t_attn, scale,
                                       [gw_up, gw_o.reshape(N_DEV, d // N_DEV, d)])
    dq_r, dkn, d_tail, dg_q, dg_kv = _qkv_bwd(dq, dk, dv, proj, lay, wuq_p, wk, wv, g_q, g_kv, ctab, atab, btab,
                                              n_heads, ts_qkv)
    gw_uq_p = _matmul(qn, dq_r, ta=True, out_dtype=BF16, tm=768, tn=512, tk=s_len, name="mm_gw_uq")
    gw_k = _matmul(kvn, dkn, ta=True, out_dtype=BF16, tm=512, tn=512, tk=s_len, name="mm_gw_k")
    gw_v = _matmul(kvn, dv, ta=True, out_dtype=BF16, tm=512, tn=512, tk=s_len, name="mm_gw_v")
    d_proj = jnp.concatenate([d_gb, d_gc, d_ci, d_tail], axis=1)
    gw_uq = gw_uq_p.reshape(ql, n_heads, HEAD_W)[:, :, :QK_NOPE + QK_ROPE].reshape(ql, n_heads * (QK_NOPE + QK_ROPE))
    gw_ukv = jnp.concatenate([gw_k.reshape(kl, n_heads, QK_NOPE), gw_v.reshape(kl, n_heads, V_DIM)],
                             axis=2).reshape(kl, n_heads * (QK_NOPE + V_DIM))
    gw_in_p, p_uq, p_ukv, p_cwm, p_cwf = _matmul(
        h1, d_proj, ta=True, out_dtype=BF16, tm=512, tn=512, tk=s_len, name="mm_gw_in",
        exchange=[_chunks_cols(gw_uq), _chunks_cols(gw_ukv), _chunks_cols(dcw_mix), unpaired_chunks(dcw_pair)])

    gw_in = jnp.concatenate([gw_in_p[:, lay["ql"]:lay["ql"] + ql], gw_in_p[:, lay["kv"]:lay["kv"] + kl],
                             gw_in_p[:, lay["kr"]:lay["kr"] + QK_ROPE], gw_in_p[:, :3 * cw]], axis=1)
    dh1, p_in = _matmul(d_proj, win_p, tb=True, out_dtype=BF16, tm=1024, tn=1024, tk=n_pad, name="mm_dh1",
                        exchange=[_chunks_cols(gw_in)])
    grad_x, d_sh_m, d_sc_m, dg_pre_mix = _first_bwd(dh1, x2d, dx1, g_pre_mix, sc_m, ts_big)

    dmod = jnp.concatenate([d_sh_m, d_sc_m, d_gt_m, d_sh_f, d_sc_f, d_gt_f], axis=1)
    small_g = [loss_row, dmod, dg_pre_mix, dg_post_mix, dg_q, dg_kv, dcb_mix, dg_pre_ffn, dg_post_ffn, dcb_ffn]
    adam_w_up, v_all = _adam_parts(p_up, w_up[0], m_w_up[0], v_w_up[0], 256, "adam_w_up",
                                   small=jnp.concatenate(small_g, axis=1))

    small_names = ["b_ada", "g_pre_mix", "g_post_mix", "g_q", "g_kv", "conv_b_mix", "g_pre_ffn", "g_post_ffn",
                   "conv_b_ffn"]
    small_w = [b_ada, g_pre_mix, g_post_mix, g_q, g_kv, conv_b_mix, g_pre_ffn, g_post_ffn, conv_b_ffn]
    small_m = [m_b_ada, m_g_pre_mix, m_g_post_mix, m_g_q, m_g_kv, m_conv_b_mix, m_g_pre_ffn, m_g_post_ffn, m_conv_b_ffn]
    small_v = [v_b_ada, v_g_pre_mix, v_g_post_mix, v_g_q, v_g_kv, v_conv_b_mix, v_g_pre_ffn, v_g_post_ffn, v_conv_b_ffn]
    offs = np.cumsum([0] + [g.shape[1] for g in small_g])
    g_sum, small_out = _adam_small(v_all, [int(o) for o in offs[1:-1]], small_w, small_m, small_v)
    small = [dict(zip(small_names, kind)) for kind in small_out]
    loss = g_sum[0, 0]

    dmod_sh = lax.dynamic_slice(v_all, (0, int(offs[1]) + my_i * ada_c), (N_DEV, ada_c))
    cact_t = jnp.transpose(cact)
    big = dict(
        w_ada=_adam_ada(cact_t, dmod_sh, w_ada[0], m_w_ada[0], v_w_ada[0], _tile(d, 256, SUB)),
        w_in=_adam_parts(p_in, w_in[0], m_w_in[0], v_w_in[0], 256, "adam_w_in"),
        w_uq=_adam_parts(p_uq, w_uq[0], m_w_uq[0], v_w_uq[0], 256, "adam_w_uq"),
        w_ukv=_adam_parts(p_ukv, w_ukv[0], m_w_ukv[0], v_w_ukv[0], 256, "adam_w_ukv"),
        w_o=_adam_parts(p_o, w_o[0], m_w_o[0], v_w_o[0], 128, "adam_w_o"),
        w_up=adam_w_up,
        w_down=_adam_parts(p_down, w_down[0], m_w_down[0], v_w_down[0], 176, "adam_w_down"),
        conv_w_mix=_adam_parts(p_cwm, conv_w_mix[0], m_conv_w_mix[0], v_conv_w_mix[0], 8, "adam_cw_mix"),
        conv_w_ffn=_adam_parts(p_cwf, conv_w_ffn[0], m_conv_w_ffn[0], v_conv_w_ffn[0], 8, "adam_cw_ffn"),
    )

    names = ["w_ada", "b_ada", "g_pre_mix", "g_post_mix", "w_in", "g_q", "w_uq", "g_kv", "w_ukv", "conv_w_mix",
             "conv_b_mix", "w_o", "g_pre_ffn", "g_post_ffn", "w_up", "conv_w_ffn", "conv_b_ffn", "w_down"]
    outs = [loss, grad_x[None]]
    for kind in range(4):
        for nm in names:
            outs.append(big[nm][kind][None] if nm in big else small[kind][nm])
    return tuple(outs)
```

```python
import math

import numpy as np
import jax
import jax.numpy as jnp
from jax import lax
from jax.experimental import pallas as pl
from jax.experimental.pallas import tpu as pltpu

F32 = jnp.float32
BF16 = jnp.bfloat16
N_DEV = 8
MESH = pl.DeviceIdType.MESH

QK_NOPE = 128
QK_ROPE = 64
V_DIM = 128
HEAD_W = 256
LANE = 128
SUB = 8
HALO = 16
STRIP_ROWS = 64
RMS_EPS = 1e-6
ROPE_THETA = 10000.0
ADAM_LR = 0.001
ADAM_B1 = 0.9
ADAM_B2 = 0.999
ADAM_EPS = 1e-08
ADAM_WD = 0.01
ADAM_STEP = 10
NEG = -1e30
LOG2E = 1.4426950408889634
LN2 = 0.6931471805599453
VMEM_LIMIT = 56 * 1024 * 1024

PREF = {"row": 256, "row_big": 512, "act_rows": 1024, "conv_rows": 1024, "conv_cols": 512, "ffn_rows": 1024, "attn": 1024, "attn_chunk": 1024, "attn_heads": 8}

NT_DIMS = (((1,), (1,)), ((), ()))
TN_DIMS = (((0,), (0,)), ((), ()))


def _cparams(*sem):
    return pltpu.CompilerParams(dimension_semantics=sem if sem else None, vmem_limit_bytes=VMEM_LIMIT)


def _tile(n, pref, unit=LANE):
    if n <= pref:
        return n
    t = (pref // unit) * unit
    while t >= unit:
        if n % t == 0:
            return t
        t -= unit
    return n


def _roundup(n, m):
    return (n + m - 1) // m * m


def _rsq(x):
    return lax.rsqrt(jnp.mean(x * x, axis=-1, keepdims=True) + RMS_EPS)


def _colsum(x):
    return jnp.sum(x, axis=0, keepdims=True)


def _sigmoid(x):
    return 1.0 / (1.0 + jnp.exp(-x))


def _matmul(a, b, *, ta=False, tb=False, out_dtype, tm, tn, tk, name, exchange=None, gather=None, b_shard_of=None,
            out_shard_of=None):
    m_dim, k_dim = (a.shape[1], a.shape[0]) if ta else a.shape
    if b_shard_of is not None:
        n_dim, tn = b.shape[0] * b.shape[2], b.shape[2]
    else:
        n_dim = b.shape[0] if tb else b.shape[1]
    chunk_w = n_dim // N_DEV
    if out_shard_of is not None:
        tn = _tile(chunk_w, tn)
    tm, tn, tk = _tile(m_dim, tm), _tile(n_dim, tn), _tile(k_dim, tk)
    per_chunk = chunk_w // tn if out_shard_of is not None else 1
    gi, gj, nk = m_dim // tm, n_dim // tn, k_dim // tk
    dims = (((0 if ta else 1,), (1 if tb else 0,)), ((), ()))
    chunks = list(exchange or gather or [])
    nx = len(chunks)
    comm_start, comm_finish = (_gather_start, _gather_finish) if gather else (_exchange_start, _exchange_finish)

    def body(*refs):
        a_ref, b_ref = refs[:2]
        xin, o_ref, xout = refs[2:2 + nx], refs[2 + nx], refs[3 + nx:3 + 2 * nx]
        scratch = refs[3 + 2 * nx:]
        sems = scratch[1:] if nk > 1 else scratch
        i, j, k = pl.program_id(0), pl.program_id(1), pl.program_id(2)
        if nx:
            @pl.when((i == 0) & (j == 0) & (k == 0))
            def _():
                comm_start(xin, xout, *sems)

        if gather:
            @pl.when((i == gi // 2) & (j == 0) & (k == 0))
            def _():
                _gather_forward(xin, xout, *sems)

        part = lax.dot_general(a_ref[...], b_ref[...], dims, preferred_element_type=F32)
        if nk == 1:
            o_ref[...] = part.astype(o_ref.dtype)
        else:
            acc_ref = scratch[0]

            @pl.when(k == 0)
            def _():
                acc_ref[...] = part

            @pl.when(k > 0)
            def _():
                acc_ref[...] += part

            @pl.when(k == nk - 1)
            def _():
                o_ref[...] = acc_ref[...].astype(o_ref.dtype)

        if nx:
            @pl.when((i == gi - 1) & (j == gj - 1) & (k == nk - 1))
            def _():
                comm_finish(xin, xout, *sems)

    a_spec = pl.BlockSpec((tk, tm), lambda i, j, k: (k, i)) if ta else pl.BlockSpec((tm, tk), lambda i, j, k: (i, k))
    b_spec = pl.BlockSpec((tn, tk), lambda i, j, k: (j, k)) if tb else pl.BlockSpec((tk, tn), lambda i, j, k: (k, j))
    if b_shard_of is not None:
        b_spec = pl.BlockSpec((None, tk, tn), lambda i, j, k: (b_shard_of(j), k, 0))
    o_spec, o_shape = pl.BlockSpec((tm, tn), lambda i, j, k: (i, j)), (m_dim, n_dim)
    if out_shard_of is not None:
        o_spec = pl.BlockSpec((None, tm, tn), lambda i, j, k: (out_shard_of(j // per_chunk), i, j % per_chunk))
        o_shape = (N_DEV, m_dim, chunk_w)
    hbm = pl.BlockSpec(memory_space=pl.ANY)
    out = pl.pallas_call(
        body,
        name=name,
        grid=(gi, gj, nk),
        in_specs=[a_spec, b_spec] + [hbm] * nx,
        out_specs=[o_spec] + [hbm] * nx,
        out_shape=[jax.ShapeDtypeStruct(o_shape, out_dtype)]
        + (_gathered_shapes(chunks) if gather else [jax.ShapeDtypeStruct(c.shape, c.dtype) for c in chunks]),
        scratch_shapes=([pltpu.VMEM((tm, tn), F32)] if nk > 1 else []) + (_comm_scratch(nx) if nx else []),
        compiler_params=_cparams(*(("arbitrary",) * 3 if nx else ("parallel", "parallel", "arbitrary"))),
    )(a, b, *chunks)
    return out if nx else out[0]


def _matmul_pair_shards(a, shards, *, out_dtype, tm, tn, name):
    m_dim = a.shape[0]
    n_sh, n_dim, c = shards.shape
    half = n_sh // 2
    tm, tn = _tile(m_dim, tm), _tile(n_dim, tn)

    def body(a_ref, b0_ref, b1_ref, o_ref, acc_ref):
        k = pl.program_id(2)
        part = (lax.dot_general(a_ref[:, :c], b0_ref[...], NT_DIMS, preferred_element_type=F32)
                + lax.dot_general(a_ref[:, c:], b1_ref[...], NT_DIMS, preferred_element_type=F32))

        @pl.when(k == 0)
        def _():
            acc_ref[...] = part

        @pl.when(k > 0)
        def _():
            acc_ref[...] += part

        @pl.when(k == half - 1)
        def _():
            o_ref[...] = acc_ref[...].astype(o_ref.dtype)

    return pl.pallas_call(
        body, name=name, grid=(m_dim // tm, n_dim // tn, half),
        in_specs=[pl.BlockSpec((tm, 2 * c), lambda i, j, k: (i, k)),
                  pl.BlockSpec((None, tn, c), lambda i, j, k: (k, j, 0)),
                  pl.BlockSpec((None, tn, c), lambda i, j, k: (k + half, j, 0))],
        out_specs=pl.BlockSpec((tm, tn), lambda i, j, k: (i, j)),
        out_shape=jax.ShapeDtypeStruct((m_dim, n_dim), out_dtype),
        scratch_shapes=[pltpu.VMEM((tm, tn), F32)],
        compiler_params=_cparams("parallel", "parallel", "arbitrary"),
    )(a, shards, shards)


def _shift_down(x, halo, n):
    r = pltpu.roll(x, n, 0)
    hr = pltpu.roll(halo, n, 0)
    row = lax.broadcasted_iota(jnp.int32, halo.shape, 0)
    top = jnp.where(row < n, hr, r[:SUB])
    return jnp.concatenate([top, r[SUB:]], axis=0)


def _shift_up(x, halo, n):
    ts = x.shape[0]
    r = pltpu.roll(x, ts - n, 0)
    hr = pltpu.roll(halo, SUB - n, 0)
    row = lax.broadcasted_iota(jnp.int32, halo.shape, 0)
    bot = jnp.where(row >= SUB - n, hr, r[ts - SUB:])
    return jnp.concatenate([r[:ts - SUB], bot], axis=0)


def _conv3(x, halo, w_ref, b_ref):
    return _shift_down(x, halo, 2) * w_ref[0:1, :] + _shift_down(x, halo, 1) * w_ref[1:2, :] + x * w_ref[2:3, :] + b_ref[...]


def _prev_halo(ts, col):
    return lambda j, i: (jnp.maximum(i * (ts // SUB) - 1, 0), col(j))


def _next_halo(ts, n_rows, col):
    return lambda j, i: (jnp.minimum((i + 1) * (ts // SUB), n_rows // SUB - 1), col(j))


def _modnorm_fwd(x, g, sc, sh, pos_col, inv_freq_row, ts, gather):
    half = QK_ROPE // 2
    s_len, d = x.shape
    n_i = s_len // ts
    ng = len(gather)

    def body(x_ref, g_ref, sc_ref, sh_ref, p_ref, f_ref, *rest):
        gin, (h_ref, c_ref, a_ref, b_ref) = rest[:ng], rest[ng:ng + 4]
        gout, sems = rest[ng + 4:2 * ng + 4], rest[2 * ng + 4:]
        i = pl.program_id(0)

        @pl.when(i == 0)
        def _():
            _gather_start(gin, gout, *sems)

        xv = x_ref[...]
        h_ref[...] = ((xv * _rsq(xv) * g_ref[...]) * (1.0 + sc_ref[...]) + sh_ref[...]).astype(BF16)
        ang = p_ref[...] * f_ref[...]
        lane = lax.broadcasted_iota(jnp.int32, ang.shape, 1)
        cos, sin = jnp.cos(ang), jnp.sin(ang)
        c_ref[...] = jnp.where(lane < 2 * half, cos, 0.0)
        a_ref[...] = jnp.where(lane < half, -sin, 0.0)
        b_ref[...] = jnp.where((lane >= half) & (lane < 2 * half), sin, 0.0)

        @pl.when(i == n_i - 1)
        def _():
            _gather_forward(gin, gout, *sems)
            _gather_finish(gin, gout, *sems)

    vec = pl.BlockSpec((1, d), lambda i: (0, 0))
    hbm = pl.BlockSpec(memory_space=pl.ANY)
    tab = pl.BlockSpec((ts, LANE), lambda i: (i, 0))
    return pl.pallas_call(
        body, name="modnorm_fwd", grid=(n_i,),
        in_specs=[pl.BlockSpec((ts, d), lambda i: (i, 0)), vec, vec, vec, pl.BlockSpec((ts, 1), lambda i: (i, 0)),
                  pl.BlockSpec((1, LANE), lambda i: (0, 0))] + [hbm] * ng,
        out_specs=[pl.BlockSpec((ts, d), lambda i: (i, 0))] + [tab] * 3 + [hbm] * ng,
        out_shape=[jax.ShapeDtypeStruct((s_len, d), BF16)] + [jax.ShapeDtypeStruct((s_len, LANE), F32)] * 3
        + _gathered_shapes(gather),
        scratch_shapes=_comm_scratch(ng),
        compiler_params=_cparams("arbitrary"),
    )(x, g, sc, sh, pos_col, inv_freq_row, *gather)


def _rope(seg, c, a, b):
    return seg * c + pltpu.roll(seg, LANE - QK_ROPE // 2, 1) * a + pltpu.roll(seg, QK_ROPE // 2, 1) * b


def _rope_t(seg, c, a, b):
    return seg * c - pltpu.roll(seg, LANE - QK_ROPE // 2, 1) * a - pltpu.roll(seg, QK_ROPE // 2, 1) * b


def _qkv_fwd(proj, lay, wuq, wk, wv, g_q, g_kv, ctab, atab, btab, n_heads, ts, scale):
    s_len = proj.shape[0]
    ql_w, kl_w = wuq.shape[0], wk.shape[0]

    def body(ql_ref, kl_ref, kr_ref, wuq_ref, wk_ref, wv_ref, gq_ref, gkv_ref, c_ref, a_ref, b_ref,
             q_out, k_out, v_out, qn_out, kvn_out):
        c, a, b = c_ref[...], a_ref[...], b_ref[...]
        ql = ql_ref[...].astype(F32)
        qn = (ql * _rsq(ql) * gq_ref[...]).astype(BF16)
        qn_out[...] = qn
        q = jnp.dot(qn, wuq_ref[...], preferred_element_type=F32)
        kl = kl_ref[...].astype(F32)
        kvn = (kl * _rsq(kl) * gkv_ref[...]).astype(BF16)
        kvn_out[...] = kvn
        kn = jnp.dot(kvn, wk_ref[...], preferred_element_type=F32)
        v_out[...] = jnp.dot(kvn, wv_ref[...], preferred_element_type=F32).astype(BF16)
        kr = _rope(kr_ref[...].astype(F32), c, a, b).astype(BF16)
        for h in range(n_heads):
            o = h * HEAD_W
            q_out[:, o:o + QK_NOPE] = (q[:, o:o + QK_NOPE] * scale).astype(BF16)
            q_out[:, o + QK_NOPE:o + HEAD_W] = (_rope(q[:, o + QK_NOPE:o + HEAD_W], c, a, b) * scale).astype(BF16)
            k_out[:, o:o + QK_NOPE] = kn[:, h * QK_NOPE:(h + 1) * QK_NOPE].astype(BF16)
            k_out[:, o + QK_NOPE:o + HEAD_W] = kr

    def full(arr):
        return pl.BlockSpec(arr.shape, lambda i: (0, 0))

    tab = pl.BlockSpec((ts, LANE), lambda i: (i, 0))
    hw, hv = n_heads * HEAD_W, n_heads * V_DIM
    return pl.pallas_call(
        body, name="qkv_fwd", grid=(s_len // ts,),
        in_specs=[pl.BlockSpec((ts, ql_w), lambda i: (i, lay["ql"] // ql_w)),
                  pl.BlockSpec((ts, kl_w), lambda i: (i, lay["kv"] // kl_w)),
                  pl.BlockSpec((ts, LANE), lambda i: (i, lay["kr"] // LANE)),
                  full(wuq), full(wk), full(wv), full(g_q), full(g_kv), tab, tab, tab],
        out_specs=[pl.BlockSpec((ts, hw), lambda i: (i, 0)), pl.BlockSpec((ts, hw), lambda i: (i, 0)),
                   pl.BlockSpec((ts, hv), lambda i: (i, 0)), pl.BlockSpec((ts, ql_w), lambda i: (i, 0)),
                   pl.BlockSpec((ts, kl_w), lambda i: (i, 0))],
        out_shape=[jax.ShapeDtypeStruct((s_len, hw), BF16), jax.ShapeDtypeStruct((s_len, hw), BF16),
                   jax.ShapeDtypeStruct((s_len, hv), BF16), jax.ShapeDtypeStruct((s_len, ql_w), BF16),
                   jax.ShapeDtypeStruct((s_len, kl_w), BF16)],
        compiler_params=_cparams("parallel"),
    )(proj, proj, proj, wuq, wk, wv, g_q, g_kv, ctab, atab, btab)


def _flash_fwd(q, k, v, n_heads, t, gather, out_cols):
    ng = len(gather)
    ck = _tile(t, PREF["attn_chunk"])
    hp = PREF["attn_heads"] if n_heads % PREF["attn_heads"] == 0 else 1
    n_groups = n_heads // hp
    s_len = q.shape[0]
    nb = s_len // t
    pairs = [(i, j) for i in range(nb) for j in range(i + 1)]
    itab = jnp.asarray(np.array([p[0] for p in pairs], np.int32))
    jtab = jnp.asarray(np.array([p[1] for p in pairs], np.int32))

    n_steps = len(pairs)
    fwd_at = max(1, (7 * n_groups * n_steps) // 10)

    def body(it_ref, jt_ref, q_ref, k_ref, v_ref, *rest):
        gin, (o_ref, lse_ref), gout = rest[:ng], rest[ng:ng + 2], rest[ng + 2:2 * ng + 2]
        m_sc, l_sc, acc_sc = rest[2 * ng + 2:2 * ng + 5]
        sems = rest[2 * ng + 5:]
        group, step_id = pl.program_id(0), pl.program_id(1)
        i, j = it_ref[step_id], jt_ref[step_id]

        @pl.when((group == 0) & (step_id == 0))
        def _():
            _gather_start(gin, gout, *sems)

        @pl.when((group == fwd_at // n_steps) & (step_id == fwd_at % n_steps))
        def _():
            _gather_forward(gin, gout, *sems)

        @pl.when(j == 0)
        def _():
            m_sc[...] = jnp.full(m_sc.shape, NEG, F32)
            l_sc[...] = jnp.zeros(l_sc.shape, F32)
            acc_sc[...] = jnp.zeros(acc_sc.shape, F32)

        def step(diag):
            for h in range(hp):
                qk, vc = slice(h * HEAD_W, (h + 1) * HEAD_W), slice(h * V_DIM, (h + 1) * V_DIM)
                cd = min(ck, t // 2) if diag else ck
                for c in range(t // cd):
                    q0 = c * cd if diag else 0
                    qs, ks = slice(q0, t), slice(c * cd, (c + 1) * cd)
                    s_t = lax.dot_general(k_ref[ks, qk], q_ref[qs, qk], NT_DIMS, preferred_element_type=F32)
                    if diag:
                        krow = lax.broadcasted_iota(jnp.int32, s_t.shape, 0)
                        qcol = lax.broadcasted_iota(jnp.int32, s_t.shape, 1)
                        s_t = jnp.where(krow <= qcol, s_t, NEG)
                    m_prev = m_sc[h, :, qs]
                    m_new = jnp.maximum(m_prev, jnp.max(s_t, axis=0, keepdims=True))
                    alpha = jnp.exp2(m_prev - m_new)
                    p_t = jnp.exp2(s_t - m_new)
                    l_sc[h, :, qs] = alpha * l_sc[h, :, qs] + jnp.sum(p_t, axis=0, keepdims=True)
                    acc_sc[h, :, qs] = acc_sc[h, :, qs] * alpha + lax.dot_general(
                        v_ref[ks, vc], p_t.astype(BF16), TN_DIMS, preferred_element_type=F32)
                    m_sc[h, :, qs] = m_new

        @pl.when(j < i)
        def _():
            step(False)

        @pl.when(j == i)
        def _():
            step(True)
            for h in range(hp):
                l = l_sc[h]
                o_ref[:, h * V_DIM:(h + 1) * V_DIM] = jnp.transpose(acc_sc[h] / l).astype(BF16)
                lse_ref[h] = m_sc[h] + jnp.log(l) * LOG2E

        @pl.when((group == n_groups - 1) & (step_id == n_steps - 1))
        def _():
            _gather_finish(gin, gout, *sems)

    hbm = pl.BlockSpec(memory_space=pl.ANY)
    grid_spec = pltpu.PrefetchScalarGridSpec(
        num_scalar_prefetch=2, grid=(n_groups, n_steps),
        in_specs=[pl.BlockSpec((t, hp * HEAD_W), lambda g, s, it, jt: (it[s], g)),
                  pl.BlockSpec((t, hp * HEAD_W), lambda g, s, it, jt: (jt[s], g)),
                  pl.BlockSpec((t, hp * V_DIM), lambda g, s, it, jt: (jt[s], g))] + [hbm] * ng,
        out_specs=[pl.BlockSpec((t, hp * V_DIM), lambda g, s, it, jt: (it[s], g)),
                   pl.BlockSpec((hp, 1, t), lambda g, s, it, jt: (g, 0, it[s]))] + [hbm] * ng,
        scratch_shapes=[pltpu.VMEM((hp, 1, t), F32), pltpu.VMEM((hp, 1, t), F32), pltpu.VMEM((hp, V_DIM, t), F32)]
        + _comm_scratch(ng),
    )
    return pl.pallas_call(
        body, name="flash_fwd", grid_spec=grid_spec,
        out_shape=[jax.ShapeDtypeStruct((s_len, out_cols), BF16),
                   jax.ShapeDtypeStruct((n_heads, 1, s_len), F32)] + _gathered_shapes(gather),
        compiler_params=_cparams("arbitrary", "arbitrary"),
    )(itab, jtab, q, k, v, *gather)


def _gconv_fwd(proj, lay, w, b, ts, tc, mixcat):
    s_len = proj.shape[0]
    cw = w.shape[1]
    nj = cw // tc
    out_off = mixcat.shape[1] - cw

    def body(gb_ref, gc_ref, ci_ref, gch_ref, cih_ref, w_ref, b_ref, mix_in, o_ref):
        i = pl.program_id(1)
        p = gc_ref[...].astype(F32) * ci_ref[...].astype(F32)
        ph = jnp.where(i > 0, gch_ref[...].astype(F32) * cih_ref[...].astype(F32), 0.0)
        o_ref[...] = (gb_ref[...].astype(F32) * _conv3(p, ph, w_ref, b_ref)).astype(BF16)

    def blk(off):
        return pl.BlockSpec((ts, tc), lambda j, i: (i, off // tc + j))

    def halo(off):
        return pl.BlockSpec((SUB, tc), _prev_halo(ts, lambda j: off // tc + j))

    return pl.pallas_call(
        body, name="gconv_fwd", grid=(nj, s_len // ts),
        in_specs=[blk(lay["gb"]), blk(lay["gc"]), blk(lay["ci"]), halo(lay["gc"]), halo(lay["ci"]),
                  pl.BlockSpec((3, tc), lambda j, i: (0, j)), pl.BlockSpec((1, tc), lambda j, i: (0, j)),
                  pl.BlockSpec(memory_space=pl.ANY)],
        out_specs=pl.BlockSpec((ts, tc), lambda j, i: (i, out_off // tc + j)),
        out_shape=jax.ShapeDtypeStruct(mixcat.shape, BF16),
        input_output_aliases={7: 0},
        compiler_params=_cparams("parallel", "parallel"),
    )(proj, proj, proj, proj, proj, w, b, mixcat)


def _post_mix_fwd(mix, x, gt, g_post, g_pre, sc, sh, ts):
    s_len, d = x.shape

    def body(mix_ref, x_ref, gt_ref, gp_ref, g2_ref, sc_ref, sh_ref, x1_ref, h2_ref):
        mv = mix_ref[...].astype(F32)
        x1 = x_ref[...] + gt_ref[...] * (mv * _rsq(mv) * gp_ref[...])
        x1_ref[...] = x1
        h2_ref[...] = ((x1 * _rsq(x1) * g2_ref[...]) * (1.0 + sc_ref[...]) + sh_ref[...]).astype(BF16)

    vec = pl.BlockSpec((1, d), lambda i: (0, 0))
    row = pl.BlockSpec((ts, d), lambda i: (i, 0))
    return pl.pallas_call(
        body, name="post_mix_fwd", grid=(s_len // ts,),
        in_specs=[row, row, vec, vec, vec, vec, vec], out_specs=[row, row],
        out_shape=[jax.ShapeDtypeStruct((s_len, d), F32), jax.ShapeDtypeStruct((s_len, d), BF16)],
        compiler_params=_cparams("parallel"),
    )(mix, x, gt, g_post, g_pre, sc, sh)


def _act_fwd(up, w, b, ts, tc):
    s_len, f2 = up.shape
    nh = f2 // 2 // tc
    rs_n = _tile(ts, STRIP_ROWS, HALO)

    def body(up_ref, uph_ref, w_ref, b_ref, u_ref, o_ref):
        i = pl.program_id(1)

        def conv_strip(r0, lanes):
            if r0 == 0:
                top = jnp.where(i > 0, uph_ref[:, lanes].astype(F32), 0.0)
                xe = jnp.concatenate([top, up_ref[0:rs_n, lanes].astype(F32)], axis=0)
            else:
                xe = up_ref[r0 - HALO:r0 + rs_n, lanes].astype(F32)
            u = (pltpu.roll(xe, 2, 0)[HALO:] * w_ref[0:1, lanes] + pltpu.roll(xe, 1, 0)[HALO:] * w_ref[1:2, lanes]
                 + xe[HALO:] * w_ref[2:3, lanes] + b_ref[:, lanes])
            u_ref[r0:r0 + rs_n, lanes] = u.astype(BF16)
            return u

        for r0 in range(0, ts, rs_n):
            for c0 in range(0, tc, LANE):
                ua = conv_strip(r0, slice(c0, c0 + LANE))
                ug = conv_strip(r0, slice(tc + c0, tc + c0 + LANE))
                o_ref[r0:r0 + rs_n, c0:c0 + LANE] = (ug * _sigmoid(ug) * ua).astype(BF16)

    pair = pl.BlockSpec((ts, 2 * tc), lambda j, i: (i, j))
    return pl.pallas_call(
        body, name="act_fwd", grid=(nh, s_len // ts),
        in_specs=[pair, pl.BlockSpec((HALO, 2 * tc), lambda j, i: (jnp.maximum(i * (ts // HALO) - 1, 0), j)),
                  pl.BlockSpec((3, 2 * tc), lambda j, i: (0, j)), pl.BlockSpec((1, 2 * tc), lambda j, i: (0, j))],
        out_specs=[pair, pl.BlockSpec((ts, tc), lambda j, i: (i, j))],
        out_shape=[jax.ShapeDtypeStruct((s_len, f2), BF16), jax.ShapeDtypeStruct((s_len, f2 // 2), BF16)],
        compiler_params=_cparams("parallel", "parallel"),
    )(up, up, w, b)


def _final_bwd(y, x1, tgt, gt, g_post, ts):
    s_len, d = y.shape

    n_i = s_len // ts

    def body(y_ref, x1_ref, t_ref, gt_ref, g_ref, dy_ref, dx2_ref, loss_ref, dgt_ref, dg_ref):
        i = pl.program_id(0)
        gtg = gt_ref[...] * g_ref[...]
        yv = y_ref[...].astype(F32)
        r = _rsq(yv)
        yh = yv * r
        e = x1_ref[...] + yh * gtg - t_ref[...]
        loss = 0.5 * jnp.sum(jnp.mean(e * e, axis=-1, keepdims=True), axis=0, keepdims=True)
        dx2 = e * (1.0 / d)
        dx2_ref[...] = dx2
        dyh = dx2 * gtg
        dy_ref[...] = (r * (dyh - yh * jnp.mean(dyh * yh, axis=-1, keepdims=True))).astype(BF16)

        @pl.when(i == 0)
        def _():
            loss_ref[...] = jnp.zeros(loss_ref.shape, F32)
            dgt_ref[...] = jnp.zeros(dgt_ref.shape, F32)

        loss_ref[...] += jnp.broadcast_to(loss, loss_ref.shape)
        dgt_ref[...] += _colsum(dx2 * yh)

        @pl.when(i == n_i - 1)
        def _():
            both = dgt_ref[...]
            dg_ref[...] = both * gt_ref[...]
            dgt_ref[...] = both * g_ref[...]

    vec = pl.BlockSpec((1, d), lambda i: (0, 0))
    row = pl.BlockSpec((ts, d), lambda i: (i, 0))
    vshape = jax.ShapeDtypeStruct((1, d), F32)
    return pl.pallas_call(
        body, name="final_bwd", grid=(s_len // ts,),
        in_specs=[row, row, row, vec, vec],
        out_specs=[row, row, pl.BlockSpec((1, LANE), lambda i: (0, 0)), vec, vec],
        out_shape=[jax.ShapeDtypeStruct((s_len, d), BF16), jax.ShapeDtypeStruct((s_len, d), F32),
                   jax.ShapeDtypeStruct((1, LANE), F32), vshape, vshape],
        compiler_params=_cparams("arbitrary"),
    )(y, x1, tgt, gt, g_post)


def _ffn_act_bwd(d_act, u, up, w, ts, tc):
    s_len, f2 = up.shape
    nh = f2 // 2 // tc
    n_i = s_len // ts
    rs_n = _tile(ts, STRIP_ROWS, HALO)
    ext = rs_n + HALO

    def body(d_ref, dh_ref, u_ref, uh_ref, x_ref, w_ref, dx_ref, dw_ref, db_ref, acc):
        i = pl.program_id(1)
        acc[...] = jnp.zeros(acc.shape, F32)

        def below(ref, halo_ref, r0, lanes):
            if r0 + ext <= ts:
                return ref[r0:r0 + ext, lanes].astype(F32)
            bot = jnp.where(i < n_i - 1, halo_ref[:, lanes].astype(F32), 0.0)
            return jnp.concatenate([ref[r0:ts, lanes].astype(F32), bot], axis=0)

        def fold8(v):
            return jnp.sum(v.reshape(v.shape[0] // SUB, SUB, v.shape[1]), axis=0)

        def conv_bwd_strip(du, r0, lanes):
            du1, du2 = pltpu.roll(du, ext - 1, 0)[:rs_n], pltpu.roll(du, ext - 2, 0)[:rs_n]
            du0 = du[:rs_n]
            dx_ref[r0:r0 + rs_n, lanes] = (du0 * w_ref[2:3, lanes] + du1 * w_ref[1:2, lanes]
                                           + du2 * w_ref[0:1, lanes]).astype(BF16)
            xv = x_ref[r0:r0 + rs_n, lanes].astype(F32)
            acc[0, :, lanes] += fold8(du2 * xv)
            acc[1, :, lanes] += fold8(du1 * xv)
            acc[2, :, lanes] += fold8(du0 * xv)
            acc[3, :, lanes] += fold8(du0)

        for r0 in range(0, ts, rs_n):
            for c0 in range(0, tc, LANE):
                la, lg = slice(c0, c0 + LANE), slice(tc + c0, tc + c0 + LANE)
                dv = below(d_ref, dh_ref, r0, la)
                ua, ug = below(u_ref, uh_ref, r0, la), below(u_ref, uh_ref, r0, lg)
                sg = _sigmoid(ug)
                conv_bwd_strip(dv * (ug * sg), r0, la)
                conv_bwd_strip(dv * ua * (sg * (1.0 + ug * (1.0 - sg))), r0, lg)

        @pl.when(i == 0)
        def _():
            dw_ref[...] = jnp.zeros(dw_ref.shape, F32)
            db_ref[...] = jnp.zeros(db_ref.shape, F32)

        dw_ref[...] += jnp.concatenate([_colsum(acc[k]) for k in range(3)], axis=0)
        db_ref[...] += _colsum(acc[3])

    pair = pl.BlockSpec((ts, 2 * tc), lambda j, i: (i, j))

    def nxt(j, i):
        return (jnp.minimum((i + 1) * (ts // HALO), s_len // HALO - 1), j)

    return pl.pallas_call(
        body, name="ffn_act_bwd", grid=(nh, n_i),
        in_specs=[pl.BlockSpec((ts, tc), lambda j, i: (i, j)), pl.BlockSpec((HALO, tc), nxt),
                  pair, pl.BlockSpec((HALO, 2 * tc), nxt), pair, pl.BlockSpec((3, 2 * tc), lambda j, i: (0, j))],
        out_specs=[pair, pl.BlockSpec((3, 2 * tc), lambda j, i: (0, j)), pl.BlockSpec((1, 2 * tc), lambda j, i: (0, j))],
        out_shape=[jax.ShapeDtypeStruct((s_len, f2), BF16), jax.ShapeDtypeStruct((3, f2), F32),
                   jax.ShapeDtypeStruct((1, f2), F32)],
        scratch_shapes=[pltpu.VMEM((4, SUB, 2 * tc), F32)],
        compiler_params=_cparams("parallel", "arbitrary"),
    )(d_act, d_act, u, u, up, w)


def _mid_bwd(dh2, x1, dx2, mix, g_pre, sc, gt_m, g_post, ts):
    s_len, d = x1.shape
    n_i = s_len // ts

    def body(dh_ref, x1_ref, dx2_ref, mix_ref, g_ref, sc_ref, gt_ref, gp_ref,
             dx1_ref, dmix_ref, dsh_ref, dsc_ref, dg_ref, dgt_ref, dgp_ref):
        i = pl.program_id(0)
        wv = (1.0 + sc_ref[...]) * g_ref[...]
        gtg = gt_ref[...] * gp_ref[...]
        dh = dh_ref[...].astype(F32)
        x1 = x1_ref[...]
        r1 = _rsq(x1)
        xh = x1 * r1
        dhx = dh * xh
        dx1 = dx2_ref[...] + r1 * (dh * wv - xh * jnp.mean(dhx * wv, axis=-1, keepdims=True))
        dx1_ref[...] = dx1
        mv = mix_ref[...].astype(F32)
        rm = _rsq(mv)
        mh = mv * rm
        dxm = dx1 * mh
        dmix_ref[...] = (rm * (dx1 * gtg - mh * jnp.mean(dxm * gtg, axis=-1, keepdims=True))).astype(BF16)

        @pl.when(i == 0)
        def _():
            for ref in (dsh_ref, dsc_ref, dgt_ref):
                ref[...] = jnp.zeros(ref.shape, F32)

        dsh_ref[...] += _colsum(dh)
        dsc_ref[...] += _colsum(dhx)
        dgt_ref[...] += _colsum(dxm)

        @pl.when(i == n_i - 1)
        def _():
            t1, t2 = dsc_ref[...], dgt_ref[...]
            dsc_ref[...] = t1 * g_ref[...]
            dg_ref[...] = t1 * (1.0 + sc_ref[...])
            dgt_ref[...] = t2 * gp_ref[...]
            dgp_ref[...] = t2 * gt_ref[...]

    vec = pl.BlockSpec((1, d), lambda i: (0, 0))
    row = pl.BlockSpec((ts, d), lambda i: (i, 0))
    vshape = jax.ShapeDtypeStruct((1, d), F32)
    return pl.pallas_call(
        body, name="mid_bwd", grid=(s_len // ts,),
        in_specs=[row, row, row, row, vec, vec, vec, vec],
        out_specs=[row, row, vec, vec, vec, vec, vec],
        out_shape=[jax.ShapeDtypeStruct((s_len, d), F32), jax.ShapeDtypeStruct((s_len, d), BF16)] + [vshape] * 5,
        compiler_params=_cparams("arbitrary"),
    )(dh2, x1, dx2, mix, g_pre, sc, gt_m, g_post)


def _first_bwd(dh1, x, dx1, g_pre, sc, ts):
    s_len, d = x.shape
    n_i = s_len // ts

    def body(dh_ref, x_ref, dx1_ref, g_ref, sc_ref, dx_ref, dsh_ref, dsc_ref, dg_ref):
        i = pl.program_id(0)
        wv = (1.0 + sc_ref[...]) * g_ref[...]
        dh = dh_ref[...].astype(F32)
        xv = x_ref[...]
        r = _rsq(xv)
        xh = xv * r
        dhx = dh * xh
        dx_ref[...] = dx1_ref[...] + r * (dh * wv - xh * jnp.mean(dhx * wv, axis=-1, keepdims=True))

        @pl.when(i == 0)
        def _():
            for ref in (dsh_ref, dsc_ref):
                ref[...] = jnp.zeros(ref.shape, F32)

        dsh_ref[...] += _colsum(dh)
        dsc_ref[...] += _colsum(dhx)

        @pl.when(i == n_i - 1)
        def _():
            t1 = dsc_ref[...]
            dsc_ref[...] = t1 * g_ref[...]
            dg_ref[...] = t1 * (1.0 + sc_ref[...])

    vec = pl.BlockSpec((1, d), lambda i: (0, 0))
    row = pl.BlockSpec((ts, d), lambda i: (i, 0))
    vshape = jax.ShapeDtypeStruct((1, d), F32)
    return pl.pallas_call(
        body, name="first_bwd", grid=(s_len // ts,),
        in_specs=[row, row, row, vec, vec], out_specs=[row, vec, vec, vec],
        out_shape=[jax.ShapeDtypeStruct((s_len, d), F32), vshape, vshape, vshape],
        compiler_params=_cparams("arbitrary"),
    )(dh1, x, dx1, g_pre, sc)


def _gconv_bwd(d_mixcat, proj, lay, w, b, ts, tc):
    s_len = proj.shape[0]
    cw = w.shape[1]
    n_i = s_len // ts
    dc_off = d_mixcat.shape[1] - cw

    def body(dc_ref, dch_ref, gb_ref, gbh_ref, gc_ref, gch_ref, ci_ref, cih_ref, w_ref, b_ref,
             dgb_ref, dgc_ref, dci_ref, dw_ref, db_ref):
        i = pl.program_id(1)
        gc, ci = gc_ref[...].astype(F32), ci_ref[...].astype(F32)
        p = gc * ci
        ph = jnp.where(i > 0, gch_ref[...].astype(F32) * cih_ref[...].astype(F32), 0.0)
        pm1, pm2 = _shift_down(p, ph, 1), _shift_down(p, ph, 2)
        z = pm2 * w_ref[0:1, :] + pm1 * w_ref[1:2, :] + p * w_ref[2:3, :] + b_ref[...]
        dc = dc_ref[...].astype(F32)
        dgb_ref[...] = (dc * z).astype(BF16)
        dz = dc * gb_ref[...].astype(F32)
        dzh = jnp.where(i < n_i - 1, dch_ref[...].astype(F32) * gbh_ref[...].astype(F32), 0.0)
        dz1, dz2 = _shift_up(dz, dzh, 1), _shift_up(dz, dzh, 2)
        dp = dz * w_ref[2:3, :] + dz1 * w_ref[1:2, :] + dz2 * w_ref[0:1, :]
        dgc_ref[...] = (dp * ci).astype(BF16)
        dci_ref[...] = (dp * gc).astype(BF16)

        @pl.when(i == 0)
        def _():
            dw_ref[...] = jnp.zeros(dw_ref.shape, F32)
            db_ref[...] = jnp.zeros(db_ref.shape, F32)

        dw_ref[0:1, :] += _colsum(dz2 * p)
        dw_ref[1:2, :] += _colsum(dz1 * p)
        dw_ref[2:3, :] += _colsum(dz * p)
        db_ref[...] += _colsum(dz)

    def blk(off):
        return pl.BlockSpec((ts, tc), lambda j, i: (i, off // tc + j))

    def prev(off):
        return pl.BlockSpec((SUB, tc), _prev_halo(ts, lambda j: off // tc + j))

    def nxt(off):
        return pl.BlockSpec((SUB, tc), _next_halo(ts, s_len, lambda j: off // tc + j))

    out_blk = pl.BlockSpec((ts, tc), lambda j, i: (i, j))
    act = jax.ShapeDtypeStruct((s_len, cw), BF16)
    return pl.pallas_call(
        body, name="gconv_bwd", grid=(cw // tc, n_i),
        in_specs=[blk(dc_off), nxt(dc_off), blk(lay["gb"]), nxt(lay["gb"]), blk(lay["gc"]), prev(lay["gc"]),
                  blk(lay["ci"]), prev(lay["ci"]),
                  pl.BlockSpec((3, tc), lambda j, i: (0, j)), pl.BlockSpec((1, tc), lambda j, i: (0, j))],
        out_specs=[out_blk, out_blk, out_blk,
                   pl.BlockSpec((3, tc), lambda j, i: (0, j)), pl.BlockSpec((1, tc), lambda j, i: (0, j))],
        out_shape=[act, act, act, jax.ShapeDtypeStruct((3, cw), F32), jax.ShapeDtypeStruct((1, cw), F32)],
        compiler_params=_cparams("parallel", "arbitrary"),
    )(d_mixcat, d_mixcat, proj, proj, proj, proj, proj, proj, w, b)


def _delta(o, d_mixcat, n_heads, ts):
    s_len = o.shape[0]

    def body(o_ref, do_ref, out_ref):
        for h in range(n_heads):
            sl = slice(h * V_DIM, (h + 1) * V_DIM)
            prod = o_ref[:, sl].astype(F32) * do_ref[:, sl].astype(F32)
            out_ref[h] = jnp.broadcast_to(jnp.sum(prod, axis=1, keepdims=True), (ts, LANE))

    hv = n_heads * V_DIM
    return pl.pallas_call(
        body, name="attn_delta", grid=(s_len // ts,),
        in_specs=[pl.BlockSpec((ts, hv), lambda i: (i, 0)), pl.BlockSpec((ts, hv), lambda i: (i, 0))],
        out_specs=pl.BlockSpec((n_heads, ts, LANE), lambda i: (0, i, 0)),
        out_shape=jax.ShapeDtypeStruct((n_heads, s_len, LANE), F32),
        compiler_params=_cparams("parallel"),
    )(o, d_mixcat)


def _flash_bwd(q, k, v, d_mixcat, lse_row, delta_row, n_heads, t, scale, exchange):
    nx = len(exchange)
    s_len = q.shape[0]
    nb = s_len // t
    pairs = [(j, i) for j in range(nb) for i in range(j, nb)]
    jtab = jnp.asarray(np.array([p[0] for p in pairs], np.int32))
    itab = jnp.asarray(np.array([p[1] for p in pairs], np.int32))
    n_steps = len(pairs)

    def body(jt_ref, it_ref, q_ref, k_ref, v_ref, do_ref, lse_ref, dl_ref, *rest):
        xin, (dq_ref, dk_ref, dv_ref), xout = rest[:nx], rest[nx:nx + 3], rest[nx + 3:2 * nx + 3]
        dq_acc, dk_acc, dv_acc = rest[2 * nx + 3:2 * nx + 6]
        sems = rest[2 * nx + 6:]
        head, step_id = pl.program_id(0), pl.program_id(1)
        j, i = jt_ref[step_id], it_ref[step_id]

        @pl.when((head == 0) & (step_id == 0))
        def _():
            _exchange_start(xin, xout, *sems)

        @pl.when(step_id == 0)
        def _():
            dq_acc[...] = jnp.zeros(dq_acc.shape, F32)

        @pl.when(i == j)
        def _():
            dk_acc[...] = jnp.zeros(dk_acc.shape, F32)
            dv_acc[...] = jnp.zeros(dv_acc.shape, F32)

        def step(diag):
            half = t // 2
            lo, hi = slice(0, half), slice(half, t)
            blocks = [(lo, lo, True), (lo, hi, False), (hi, hi, True)] if diag else [(slice(0, t), slice(0, t), False)]
            for ks, qs, masked in blocks:
                qv, kv, vv, dov = q_ref[qs, :], k_ref[ks, :], v_ref[ks, :], do_ref[qs, :]
                s_t = lax.dot_general(kv, qv, NT_DIMS, preferred_element_type=F32)
                if masked:
                    krow = lax.broadcasted_iota(jnp.int32, s_t.shape, 0)
                    qcol = lax.broadcasted_iota(jnp.int32, s_t.shape, 1)
                    s_t = jnp.where(krow <= qcol, s_t, NEG)
                p_t = jnp.exp2(s_t - lse_ref[0, :, qs])
                dv_acc[ks, :] += jnp.dot(p_t.astype(BF16), dov, preferred_element_type=F32)
                dp_t = lax.dot_general(vv, dov, NT_DIMS, preferred_element_type=F32)
                ds_t = (p_t * (dp_t - dl_ref[0, :, qs])).astype(BF16)
                dk_acc[ks, :] += jnp.dot(ds_t, qv, preferred_element_type=F32)
                n_q = qs.stop - qs.start
                rows = pl.ds(pl.multiple_of(i * t + qs.start, n_q), n_q)
                dq_acc[rows, :] += lax.dot_general(ds_t, kv, TN_DIMS, preferred_element_type=F32)

        @pl.when(i > j)
        def _():
            step(False)

        @pl.when(i == j)
        def _():
            step(True)

        @pl.when(i == nb - 1)
        def _():
            dk_ref[...] = (dk_acc[...] * LN2).astype(BF16)
            dv_ref[...] = dv_acc[...].astype(BF16)

        @pl.when(step_id == n_steps - 1)
        def _():
            dq_ref[...] = (dq_acc[...] * scale).astype(BF16)

        @pl.when((head == n_heads - 1) & (step_id == n_steps - 1))
        def _():
            _exchange_finish(xin, xout, *sems)

    hbm = pl.BlockSpec(memory_space=pl.ANY)
    hv = n_heads * V_DIM
    do_off = 0
    grid_spec = pltpu.PrefetchScalarGridSpec(
        num_scalar_prefetch=2, grid=(n_heads, n_steps),
        in_specs=[pl.BlockSpec((t, HEAD_W), lambda h, s, jt, it: (it[s], h)),
                  pl.BlockSpec((t, HEAD_W), lambda h, s, jt, it: (jt[s], h)),
                  pl.BlockSpec((t, V_DIM), lambda h, s, jt, it: (jt[s], h)),
                  pl.BlockSpec((t, V_DIM), lambda h, s, jt, it: (it[s], do_off + h)),
                  pl.BlockSpec((1, 1, t), lambda h, s, jt, it: (h, 0, it[s])),
                  pl.BlockSpec((1, 1, t), lambda h, s, jt, it: (h, 0, it[s]))] + [hbm] * nx,
        out_specs=[pl.BlockSpec((s_len, HEAD_W), lambda h, s, jt, it: (0, h)),
                   pl.BlockSpec((t, HEAD_W), lambda h, s, jt, it: (jt[s], h)),
                   pl.BlockSpec((t, V_DIM), lambda h, s, jt, it: (jt[s], h))] + [hbm] * nx,
        scratch_shapes=[pltpu.VMEM((s_len, HEAD_W), F32), pltpu.VMEM((t, HEAD_W), F32), pltpu.VMEM((t, V_DIM), F32)]
        + _comm_scratch(nx),
    )
    return pl.pallas_call(
        body, name="flash_bwd", grid_spec=grid_spec,
        out_shape=[jax.ShapeDtypeStruct((s_len, n_heads * HEAD_W), BF16),
                   jax.ShapeDtypeStruct((s_len, n_heads * HEAD_W), BF16),
                   jax.ShapeDtypeStruct((s_len, hv), BF16)] + [jax.ShapeDtypeStruct(c.shape, c.dtype) for c in exchange],
        compiler_params=_cparams("arbitrary", "arbitrary"),
    )(jtab, itab, q, k, v, d_mixcat, lse_row, delta_row, *exchange)


def _qkv_bwd(dq, dk, dv, proj, lay, wuq, wk, wv, g_q, g_kv, ctab, atab, btab, n_heads, ts):
    s_len = proj.shape[0]
    ql_w, kl_w = wuq.shape[0], wk.shape[0]
    tail_w = lay["np"] - lay["ql"]
    kv_o, kr_o = lay["kv"] - lay["ql"], lay["kr"] - lay["ql"]

    def body(dq_ref, dk_ref, dv_ref, ql_ref, kl_ref, wuq_ref, wk_ref, wv_ref, gq_ref, gkv_ref, c_ref, a_ref, b_ref,
             dqr_ref, dkn_ref, tail_ref, dgq_ref, dgkv_ref):
        i = pl.program_id(0)
        c, a, b = c_ref[...], a_ref[...], b_ref[...]
        dkr = jnp.zeros((ts, LANE), F32)
        for h in range(n_heads):
            o = h * HEAD_W
            dqr_ref[:, o:o + QK_NOPE] = dq_ref[:, o:o + QK_NOPE]
            dqr_ref[:, o + QK_NOPE:o + HEAD_W] = _rope_t(dq_ref[:, o + QK_NOPE:o + HEAD_W].astype(F32), c, a, b).astype(BF16)
            dkn_ref[:, h * QK_NOPE:(h + 1) * QK_NOPE] = dk_ref[:, o:o + QK_NOPE]
            dkr = dkr + dk_ref[:, o + QK_NOPE:o + HEAD_W].astype(F32)
        tail_ref[...] = jnp.zeros(tail_ref.shape, BF16)
        tail_ref[:, kr_o:kr_o + LANE] = _rope_t(dkr, c, a, b).astype(BF16)

        def rms_bwd(lat_ref, dn, g_ref):
            lat = lat_ref[...].astype(F32)
            r = _rsq(lat)
            xh = lat * r
            dxh = dn * g_ref[...]
            return r * (dxh - xh * jnp.mean(dxh * xh, axis=-1, keepdims=True)), _colsum(dn * xh)

        dqn = lax.dot_general(dqr_ref[...], wuq_ref[...], NT_DIMS, preferred_element_type=F32)
        d_ql, dgq = rms_bwd(ql_ref, dqn, gq_ref)
        tail_ref[:, 0:ql_w] = d_ql.astype(BF16)
        dkvn = (lax.dot_general(dkn_ref[...], wk_ref[...], NT_DIMS, preferred_element_type=F32)
                + lax.dot_general(dv_ref[...], wv_ref[...], NT_DIMS, preferred_element_type=F32))
        d_kl, dgkv = rms_bwd(kl_ref, dkvn, gkv_ref)
        tail_ref[:, kv_o:kv_o + kl_w] = d_kl.astype(BF16)

        @pl.when(i == 0)
        def _():
            dgq_ref[...] = jnp.zeros(dgq_ref.shape, F32)
            dgkv_ref[...] = jnp.zeros(dgkv_ref.shape, F32)

        dgq_ref[...] += dgq
        dgkv_ref[...] += dgkv

    def full(arr):
        return pl.BlockSpec(arr.shape, lambda i: (0, 0))

    def rows(w):
        return pl.BlockSpec((ts, w), lambda i: (i, 0))

    tab = pl.BlockSpec((ts, LANE), lambda i: (i, 0))
    hw, hv, hn = n_heads * HEAD_W, n_heads * V_DIM, n_heads * QK_NOPE
    return pl.pallas_call(
        body, name="qkv_bwd", grid=(s_len // ts,),
        in_specs=[rows(hw), rows(hw), rows(hv),
                  pl.BlockSpec((ts, ql_w), lambda i: (i, lay["ql"] // ql_w)),
                  pl.BlockSpec((ts, kl_w), lambda i: (i, lay["kv"] // kl_w)),
                  full(wuq), full(wk), full(wv), full(g_q), full(g_kv), tab, tab, tab],
        out_specs=[rows(hw), rows(hn), rows(tail_w), full(g_q), full(g_kv)],
        out_shape=[jax.ShapeDtypeStruct((s_len, hw), BF16), jax.ShapeDtypeStruct((s_len, hn), BF16),
                   jax.ShapeDtypeStruct((s_len, tail_w), BF16),
                   jax.ShapeDtypeStruct(g_q.shape, F32), jax.ShapeDtypeStruct(g_kv.shape, F32)],
        compiler_params=_cparams("arbitrary"),
    )(dq, dk, dv, proj, proj, wuq, wk, wv, g_q, g_kv, ctab, atab, btab)


def _adamw(w, g, m, v):
    m = ADAM_B1 * m + (1.0 - ADAM_B1) * g
    v = ADAM_B2 * v + (1.0 - ADAM_B2) * (g * g)
    m_hat = m / (1.0 - ADAM_B1 ** ADAM_STEP)
    v_hat = v / (1.0 - ADAM_B2 ** ADAM_STEP)
    delta = -ADAM_LR * (m_hat / (jnp.sqrt(v_hat) + ADAM_EPS) + ADAM_WD * w)
    return delta, m, v


def _adam_parts(parts, w, m, v, tr, name, small=None):
    r, c = w.shape
    tr = _tile(r, tr, SUB)
    n_i = r // tr
    ns = 0 if small is None else 1

    def body(p_ref, w_ref, m_ref, v_ref, *rest):
        vec_ref = rest[:ns]
        g_out, d_out, m_out, v_out = rest[ns:ns + 4]
        gath_ref, sems = rest[ns + 4:2 * ns + 4], rest[2 * ns + 4:]
        i = pl.program_id(0)

        def small_copies():
            send_s, recv_s, local_s = sems
            me = _my_place()
            my_i = _index(me)

            def copy(k, to, src_row):
                row = gath_ref[0].at[pl.ds(src_row, 1), :]
                return pltpu.make_async_remote_copy(src_ref=row, dst_ref=row, send_sem=send_s.at[k], recv_sem=recv_s.at[k],
                                                    device_id=to, device_id_type=MESH)

            own = pltpu.make_async_copy(vec_ref[0], gath_ref[0].at[pl.ds(my_i, 1), :], local_s)
            sends = [copy(k, _peer(me, k), my_i) for k in range(1, N_DEV)]
            recvs = [copy(k, me, _index(_peer(me, k))) for k in range(1, N_DEV)]
            return own, sends, recvs

        if ns:
            @pl.when(i == 0)
            def _():
                own, sends, _ = small_copies()
                own.start()
                own.wait()
                for cp in sends:
                    cp.start()

        g = p_ref[0].astype(F32)
        for dev in range(1, N_DEV):
            g = g + p_ref[dev].astype(F32)
        g_out[...] = g
        d_out[...], m_out[...], v_out[...] = _adamw(w_ref[...], g, m_ref[...], v_ref[...])

        if ns:
            @pl.when(i == n_i - 1)
            def _():
                _, sends, recvs = small_copies()
                for cp in recvs:
                    cp.wait_recv()
                for cp in sends:
                    cp.wait_send()

    blk = pl.BlockSpec((tr, c), lambda i: (i, 0))
    hbm = pl.BlockSpec(memory_space=pl.ANY)
    shp = jax.ShapeDtypeStruct((r, c), F32)
    out = pl.pallas_call(
        body, name=name, grid=(n_i,),
        in_specs=[pl.BlockSpec((N_DEV, tr, c), lambda i: (0, i, 0)), blk, blk, blk] + [hbm] * ns,
        out_specs=[blk, blk, blk, blk] + [hbm] * ns,
        out_shape=[shp, shp, shp, shp] + ([jax.ShapeDtypeStruct((N_DEV, small.shape[1]), F32)] if ns else []),
        scratch_shapes=([pltpu.SemaphoreType.DMA((N_DEV,))] * 2 + [pltpu.SemaphoreType.DMA]) if ns else [],
        compiler_params=_cparams("arbitrary" if ns else "parallel"),
    )(parts, w, m, v, *([small] if ns else []))
    return (out[:4], out[4]) if ns else out


def _adam_ada(cact_t, dmod_sh, w, m, v, tr):
    r, c = w.shape

    def body(ct_ref, dm_ref, w_ref, m_ref, v_ref, g_out, d_out, m_out, v_out):
        g = jnp.dot(ct_ref[...], dm_ref[...], preferred_element_type=F32, precision=lax.Precision.HIGHEST)
        g_out[...] = g
        d_out[...], m_out[...], v_out[...] = _adamw(w_ref[...], g, m_ref[...], v_ref[...])

    blk = pl.BlockSpec((tr, c), lambda i: (i, 0))
    shp = jax.ShapeDtypeStruct((r, c), F32)
    return pl.pallas_call(
        body, name="adam_ada", grid=(r // tr,),
        in_specs=[pl.BlockSpec((tr, N_DEV), lambda i: (i, 0)), pl.BlockSpec((N_DEV, c), lambda i: (0, 0)), blk, blk, blk],
        out_specs=[blk, blk, blk, blk], out_shape=[shp, shp, shp, shp],
        compiler_params=_cparams("parallel"),
    )(cact_t, dmod_sh, w, m, v)


def _adam_small(v_all, offs, ws, ms, vs):
    n_par = len(ws)

    def body(p_ref, *refs):
        w_refs, m_refs, v_refs = refs[:n_par], refs[n_par:2 * n_par], refs[2 * n_par:3 * n_par]
        sum_ref = refs[3 * n_par]
        outs = refs[3 * n_par + 1:]
        g = p_ref[0:1, :]
        for dev in range(1, N_DEV):
            g = g + p_ref[dev:dev + 1, :]
        sum_ref[...] = g
        for p in range(n_par):
            n = w_refs[p].shape[1]
            gp = sum_ref[:, offs[p]:offs[p] + n]
            outs[p][...] = gp
            (outs[n_par + p][...], outs[2 * n_par + p][...], outs[3 * n_par + p][...]) = _adamw(
                w_refs[p][...], gp, m_refs[p][...], v_refs[p][...])

    vm = pl.BlockSpec(memory_space=pltpu.VMEM)
    shapes = [jax.ShapeDtypeStruct(w.shape, F32) for w in ws]
    out = pl.pallas_call(
        body, name="adam_small", in_specs=[vm] * (1 + 3 * n_par), out_specs=[vm] * (1 + 4 * n_par),
        out_shape=[jax.ShapeDtypeStruct((1, v_all.shape[1]), F32)] + shapes * 4, compiler_params=_cparams(),
    )(v_all, *ws, *ms, *vs)
    return out[0], [out[1 + k * n_par:1 + (k + 1) * n_par] for k in range(4)]


def _my_place():
    return lax.axis_index("x"), lax.axis_index("y"), lax.axis_index("c")


def _peer(place, k):
    x, y, c = place
    return (x ^ (k >> 2), y ^ ((k >> 1) & 1), c ^ (k & 1))


def _index(place):
    return 4 * place[0] + 2 * place[1] + place[2]


def _ada_fwd(vec, w_ada, b_ada_rows):
    lv = vec.shape[1]
    d, c = w_ada.shape

    def body(vec_ref, w_ref, b_ref, gath_ref, cact_ref, mod_ref, modsh, send_a, recv_a, send_b, recv_b, local_s):
        me = _my_place()
        my_i = _index(me)

        def gather_copy(k, to, src_row):
            row = gath_ref.at[pl.ds(src_row, 1), :]
            return pltpu.make_async_remote_copy(src_ref=row, dst_ref=row, send_sem=send_a.at[k], recv_sem=recv_a.at[k],
                                                device_id=to, device_id_type=MESH)

        own = pltpu.make_async_copy(vec_ref, gath_ref.at[pl.ds(my_i, 1), :], local_s.at[0])
        own.start()
        own.wait()
        sends = [gather_copy(k, _peer(me, k), my_i) for k in range(1, N_DEV)]
        for cp in sends:
            cp.start()
        for k in range(1, N_DEV):
            gather_copy(k, me, _index(_peer(me, k))).wait_recv()
        for cp in sends:
            cp.wait_send()

        c_all = gath_ref[:, 0:d]
        cact = c_all * _sigmoid(c_all)
        cact_ref[...] = cact
        modsh[...] = jnp.dot(cact, w_ref[...], preferred_element_type=F32, precision=lax.Precision.HIGHEST)

        def mod_copy(k, to, src_row, dst_row):
            return pltpu.make_async_remote_copy(src_ref=modsh.at[pl.ds(src_row, 1), :], dst_ref=mod_ref.at[pl.ds(dst_row, 1), :],
                                                send_sem=send_b.at[k], recv_sem=recv_b.at[k],
                                                device_id=to, device_id_type=MESH)

        own = pltpu.make_async_copy(modsh.at[pl.ds(my_i, 1), :], mod_ref.at[pl.ds(my_i, 1), :], local_s.at[1])
        own.start()
        sends = [mod_copy(k, _peer(me, k), _index(_peer(me, k)), my_i) for k in range(1, N_DEV)]
        for cp in sends:
            cp.start()
        for k in range(1, N_DEV):
            mod_copy(k, me, my_i, _index(_peer(me, k))).wait_recv()
        for cp in sends:
            cp.wait_send()
        own.wait()
        mod_ref[...] = mod_ref[...] + b_ref[...]

    vm = pl.BlockSpec(memory_space=pltpu.VMEM)
    return pl.pallas_call(
        body, name="ada_fwd", in_specs=[vm, vm, vm], out_specs=[vm, vm, vm],
        out_shape=[jax.ShapeDtypeStruct((N_DEV, lv), F32), jax.ShapeDtypeStruct((N_DEV, d), F32),
                   jax.ShapeDtypeStruct((N_DEV, c), F32)],
        scratch_shapes=[pltpu.VMEM((N_DEV, c), F32)] + [pltpu.SemaphoreType.DMA((N_DEV,))] * 4
        + [pltpu.SemaphoreType.DMA((2,))],
        compiler_params=pltpu.CompilerParams(vmem_limit_bytes=VMEM_LIMIT),
    )(vec, w_ada, b_ada_rows)


PER = N_DEV - 1


def _comm_scratch(n):
    return [pltpu.SemaphoreType.DMA((n * PER,)), pltpu.SemaphoreType.DMA((n * PER,)), pltpu.SemaphoreType.DMA((n,))]


def _gather_copies(ins, outs, send_s, recv_s, local_s):
    n = len(ins)
    me = _my_place()
    x, y, c = me
    sibling = (x, y, 1 - c)
    chips = [(1 - x, y), (x, 1 - y), (1 - x, 1 - y)]

    def copy(a, k, block, to, src=None):
        slot = outs[a].at[_index(block)]
        return pltpu.make_async_remote_copy(src_ref=slot if src is None else src, dst_ref=slot,
                                            send_sem=send_s.at[a * PER + k], recv_sem=recv_s.at[a * PER + k],
                                            device_id=to, device_id_type=MESH)

    mine = [pltpu.make_async_copy(ins[a], outs[a].at[_index(me)], local_s.at[a]) for a in range(n)]
    first = []
    for a in range(n):
        first.append(copy(a, 0, me, sibling, src=ins[a]))
        first += [copy(a, 1 + j, me, (*chip, c), src=ins[a]) for j, chip in enumerate(chips)]
    landed = [copy(a, 1 + j, (*chip, c), me) for j, chip in enumerate(chips) for a in range(n)]
    passed = [copy(a, 4 + j, (*chip, c), sibling) for j, chip in enumerate(chips) for a in range(n)]
    from_sibling = [copy(a, 0, sibling, me) for a in range(n)]
    from_sibling += [copy(a, 4 + j, (*chip, 1 - c), me) for a in range(n) for j, chip in enumerate(chips)]
    return mine, first, landed, passed, from_sibling


def _gather_start(*refs):
    mine, first, _, _, _ = _gather_copies(*refs)
    for cp in mine + first:
        cp.start()


def _gather_forward(*refs):
    _, _, landed, passed, _ = _gather_copies(*refs)
    for got, fwd in zip(landed, passed):
        got.wait_recv()
        fwd.start()


def _gather_finish(*refs):
    mine, first, _, passed, from_sibling = _gather_copies(*refs)
    for cp in from_sibling:
        cp.wait_recv()
    for cp in first + passed:
        cp.wait_send()
    for cp in mine:
        cp.wait()


def _exchange_copies(ins, outs, send_s, recv_s, local_s):
    n = len(ins)
    me = _my_place()
    my_i = _index(me)

    def copy(a, k, to, src_slot, dst_slot):
        return pltpu.make_async_remote_copy(src_ref=ins[a].at[src_slot], dst_ref=outs[a].at[dst_slot],
                                            send_sem=send_s.at[a * PER + k - 1], recv_sem=recv_s.at[a * PER + k - 1],
                                            device_id=to, device_id_type=MESH)

    mine = [pltpu.make_async_copy(ins[a].at[my_i], outs[a].at[my_i], local_s.at[a]) for a in range(n)]
    sends = [copy(a, k, _peer(me, k), _index(_peer(me, k)), my_i) for k in range(1, N_DEV) for a in range(n)]
    recvs = [copy(a, k, me, my_i, _index(_peer(me, k))) for k in range(1, N_DEV) for a in range(n)]
    return mine, sends, recvs


def _exchange_start(*refs):
    mine, sends, _ = _exchange_copies(*refs)
    for cp in mine + sends:
        cp.start()


def _exchange_finish(*refs):
    mine, sends, recvs = _exchange_copies(*refs)
    for cp in recvs:
        cp.wait_recv()
    for cp in sends:
        cp.wait_send()
    for cp in mine:
        cp.wait()


def _gathered_shapes(shards):
    return [jax.ShapeDtypeStruct((N_DEV,) + s.shape, s.dtype) for s in shards]


def _proj_layout(cw, ql, kl):
    lay = {"gb": 0, "gc": cw, "ci": 2 * cw, "ql": 3 * cw}
    assert lay["ql"] % ql == 0
    lay["kv"] = _roundup(lay["ql"] + ql, kl)
    lay["kr"] = lay["kv"] + kl
    lay["np"] = _roundup(lay["kr"] + LANE, 4 * LANE)
    return lay


def _chunks_cols(g):
    r, c8 = g.shape
    return jnp.transpose(g.reshape(r, N_DEV, c8 // N_DEV), (1, 0, 2))


def _from_col_shards(a):
    n, r, c = a.shape
    return jnp.transpose(a, (1, 0, 2)).reshape(r, n * c)


def kernel(x, c, positions, w_ada, b_ada, g_pre_mix, g_post_mix, w_in, g_q, w_uq, g_kv, w_ukv, conv_w_mix, conv_b_mix, w_o, g_pre_ffn, g_post_ffn, w_up, conv_w_ffn, conv_b_ffn, w_down, loss_target, m_w_ada, m_b_ada, m_g_pre_mix, m_g_post_mix, m_w_in, m_g_q, m_w_uq, m_g_kv, m_w_ukv, m_conv_w_mix, m_conv_b_mix, m_w_o, m_g_pre_ffn, m_g_post_ffn, m_w_up, m_conv_w_ffn, m_conv_b_ffn, m_w_down, v_w_ada, v_b_ada, v_g_pre_mix, v_g_post_mix, v_w_in, v_g_q, v_w_uq, v_g_kv, v_w_ukv, v_conv_w_mix, v_conv_b_mix, v_w_o, v_g_pre_ffn, v_g_post_ffn, v_w_up, v_conv_w_ffn, v_conv_b_ffn, v_w_down):
    s_len, d = x.shape[1], x.shape[2]
    ql, kl = w_uq.shape[1], w_ukv.shape[1]
    n_heads = w_ukv.shape[2] * N_DEV // (QK_NOPE + V_DIM)
    cw = conv_w_mix.shape[2] * N_DEV
    f2 = w_up.shape[2] * N_DEV
    ff = f2 // 2
    in_cols = w_in.shape[2] * N_DEV
    ada_c = w_ada.shape[2]
    cwm_c, cwf_c = conv_w_mix.shape[2], conv_w_ffn.shape[2]
    scale = 1.0 / math.sqrt(QK_NOPE + QK_ROPE)
    lay = _proj_layout(cw, ql, kl)
    n_pad = lay["np"]
    my_i = _index(_my_place())

    ts_row = _tile(s_len, PREF["row"], SUB)
    ts_conv = _tile(s_len, PREF["conv_rows"], SUB)
    tc_conv = _tile(cw, PREF["conv_cols"])
    tc_ffn = cwf_c
    ts_ffn = _tile(s_len, PREF["ffn_rows"], SUB)
    ts_act = _tile(s_len, PREF["act_rows"], SUB)
    ts_big = _tile(s_len, PREF["row_big"], SUB)
    pair_order = [k // 2 + (k % 2) * (N_DEV // 2) for k in range(N_DEV)]
    pair_place = [pair_order.index(k) for k in range(N_DEV)]

    def paired(shards):
        return _from_col_shards(jnp.stack([shards[p] for p in pair_order]))

    def unpaired_chunks(g):
        ch = _chunks_cols(g)
        return jnp.stack([ch[p] for p in pair_place])
    ts_qkv = _tile(s_len, PREF["row_big"], SUB)
    t_attn = _tile(s_len, PREF["attn"])

    x2d, tgt = x[0], loss_target[0]

    vec = jnp.concatenate([c, conv_w_mix[0].reshape(1, -1), conv_w_ffn[0].reshape(1, -1)], axis=1)
    gath, cact, mod_rows = _ada_fwd(vec, w_ada[0], b_ada.reshape(N_DEV, ada_c))
    cwm_full = _from_col_shards(gath[:, d:d + 3 * cwm_c].reshape(N_DEV, 3, cwm_c))
    cwf_shards = gath[:, d + 3 * cwm_c:].reshape(N_DEV, 3, cwf_c)
    cwf_pair = paired(cwf_shards)
    cbf_pair = paired(jnp.transpose(conv_b_ffn.reshape(1, N_DEV, cwf_c), (1, 0, 2)))
    mod = mod_rows.reshape(1, N_DEV * ada_c)
    sh_m, sc_m, gt_m, sh_f, sc_f, gt_f = [mod[:, k * d:(k + 1) * d] for k in range(6)]

    inv_freq = 1.0 / (ROPE_THETA ** (jnp.arange(0, QK_ROPE, 2, dtype=F32) / QK_ROPE))
    inv_row = jnp.tile(inv_freq, LANE // (QK_ROPE // 2)).reshape(1, LANE)
    h1, ctab, atab, btab, g_in = _modnorm_fwd(x2d, g_pre_mix, sc_m, sh_m, positions.astype(F32).reshape(s_len, 1),
                                              inv_row, ts_row, [w_in[0].astype(BF16)])
    win = _from_col_shards(g_in)
    cut = np.cumsum([0, ql, kl, QK_ROPE, cw, cw, cw])
    part = [win[:, cut[k]:cut[k + 1]] for k in range(6)]

    def zcols(n):
        return jnp.zeros((d, n), BF16)

    win_p = jnp.concatenate([part[3], part[4], part[5], part[0], zcols(lay["kv"] - lay["ql"] - ql), part[1],
                             part[2], zcols(n_pad - lay["kr"] - QK_ROPE)], axis=1)

    proj, g_uq, g_ukv = _matmul(h1, win_p, out_dtype=BF16, tm=2048, tn=1280, tk=2048, name="mm_proj",
                                gather=[w_uq[0].astype(BF16), w_ukv[0].astype(BF16)])
    wuq_p = jnp.pad(_from_col_shards(g_uq).reshape(ql, n_heads, QK_NOPE + QK_ROPE),
                    ((0, 0), (0, 0), (0, HEAD_W - QK_NOPE - QK_ROPE))).reshape(ql, n_heads * HEAD_W)
    wukv = _from_col_shards(g_ukv).reshape(kl, n_heads, QK_NOPE + V_DIM)
    wk = wukv[:, :, :QK_NOPE].reshape(kl, n_heads * QK_NOPE)
    wv = wukv[:, :, QK_NOPE:].reshape(kl, n_heads * V_DIM)
    q, k, v, qn, kvn = _qkv_fwd(proj, lay, wuq_p, wk, wv, g_q, g_kv, ctab, atab, btab, n_heads, ts_qkv, scale * LOG2E)
    attn, lse_row, g_o, g_up = _flash_fwd(q, k, v, n_heads, t_attn, [w_o[0].astype(BF16), w_up[0].astype(BF16)], d)
    wo = g_o.reshape(d, d)

    def pair_shard(j):
        return j // 2 + (j % 2) * (N_DEV // 2)
    mixcat = _gconv_fwd(proj, lay, cwm_full, conv_b_mix, ts_conv, tc_conv, attn)
    mix = _matmul(mixcat, wo, out_dtype=BF16, tm=512, tn=2048, tk=2048, name="mm_mix")
    x1, h2 = _post_mix_fwd(mix, x2d, gt_m, g_post_mix, g_pre_ffn, sc_f, sh_f, ts_big)
    up, g_down = _matmul(h2, g_up, out_dtype=BF16, tm=2048, tn=cwf_c, tk=2048, name="mm_up", b_shard_of=pair_shard,
                         gather=[w_down[0].astype(BF16)])
    wdown = g_down.reshape(ff, d)
    u, act = _act_fwd(up, cwf_pair, cbf_pair, ts_act, tc_ffn)
    y = _matmul(act, wdown, out_dtype=BF16, tm=1024, tn=512, tk=ff, name="mm_down")

    dy, dx2, loss_row, d_gt_f, dg_post_ffn = _final_bwd(y, x1, tgt, gt_f, g_post_ffn, ts_big)
    gw_down = _matmul(act, dy, ta=True, out_dtype=BF16, tm=512, tn=512, tk=s_len, name="mm_gw_down")
    d_act = _matmul(dy, wdown, tb=True, out_dtype=BF16, tm=2048, tn=1408, tk=2048, name="mm_d_act")
    d_up, dcw_pair, dcb_pair = _ffn_act_bwd(d_act, u, up, cwf_pair, ts_ffn, tc_ffn)
    dcb_ffn = _from_col_shards(unpaired_chunks(dcb_pair))
    gw_up, p_down = _matmul(h2, d_up, ta=True, out_dtype=BF16, tm=512, tn=cwf_c, tk=4096, name="mm_gw_up",
                            exchange=[gw_down.reshape(N_DEV, ff // N_DEV, d)], out_shard_of=pair_shard)
    dh2 = _matmul_pair_shards(d_up, g_up, out_dtype=BF16, tm=1024, tn=1024, name="mm_dh2")
    dx1, dmix, d_sh_f, d_sc_f, dg_pre_ffn, d_gt_m, dg_post_mix = _mid_bwd(
        dh2, x1, dx2, mix, g_pre_ffn, sc_f, gt_m, g_post_mix, ts_row)
    gw_o = _matmul(mixcat, dmix, ta=True, out_dtype=BF16, tm=512, tn=512, tk=s_len, name="mm_gw_o")
    d_mixcat = _matmul(dmix, wo, tb=True, out_dtype=BF16, tm=1024, tn=1024, tk=2048, name="mm_d_mixcat")
    d_gb, d_gc, d_ci, dcw_mix, dcb_mix = _gconv_bwd(d_mixcat, proj, lay, cwm_full, conv_b_mix, ts_conv, tc_conv)
    delta = _delta(mixcat, d_mixcat, n_heads, _tile(s_len, 512, SUB))
    delta_row = delta[:, :, 0].reshape(n_heads, 1, s_len)
    dq, dk, dv, p_up, p_o = _flash_bwd(q, k, v, d_mixcat, lse_row, delta_row, n_heads, t_attn, scale,
                                       [gw_up, gw_o.reshape(N_DEV, d // N_DEV, d)])
    dq_r, dkn, d_tail, dg_q, dg_kv = _qkv_bwd(dq, dk, dv, proj, lay, wuq_p, wk, wv, g_q, g_kv, ctab, atab, btab,
                                              n_heads, ts_qkv)
    gw_uq_p = _matmul(qn, dq_r, ta=True, out_dtype=BF16, tm=768, tn=512, tk=s_len, name="mm_gw_uq")
    gw_k = _matmul(kvn, dkn, ta=True, out_dtype=BF16, tm=512, tn=512, tk=s_len, name="mm_gw_k")
    gw_v = _matmul(kvn, dv, ta=True, out_dtype=BF16, tm=512, tn=512, tk=s_len, name="mm_gw_v")
    d_proj = jnp.concatenate([d_gb, d_gc, d_ci, d_tail], axis=1)
    gw_uq = gw_uq_p.reshape(ql, n_heads, HEAD_W)[:, :, :QK_NOPE + QK_ROPE].reshape(ql, n_heads * (QK_NOPE + QK_ROPE))
    gw_ukv = jnp.concatenate([gw_k.reshape(kl, n_heads, QK_NOPE), gw_v.reshape(kl, n_heads, V_DIM)],
                             axis=2).reshape(kl, n_heads * (QK_NOPE + V_DIM))
    gw_in_p, p_uq, p_ukv, p_cwm, p_cwf = _matmul(
        h1, d_proj, ta=True, out_dtype=BF16, tm=512, tn=512, tk=s_len, name="mm_gw_in",
        exchange=[_chunks_cols(gw_uq), _chunks_cols(gw_ukv), _chunks_cols(dcw_mix), unpaired_chunks(dcw_pair)])

    gw_in = jnp.concatenate([gw_in_p[:, lay["ql"]:lay["ql"] + ql], gw_in_p[:, lay["kv"]:lay["kv"] + kl],
                             gw_in_p[:, lay["kr"]:lay["kr"] + QK_ROPE], gw_in_p[:, :3 * cw]], axis=1)
    dh1, p_in = _matmul(d_proj, win_p, tb=True, out_dtype=BF16, tm=512, tn=1024, tk=n_pad, name="mm_dh1",
                        exchange=[_chunks_cols(gw_in)])
    grad_x, d_sh_m, d_sc_m, dg_pre_mix = _first_bwd(dh1, x2d, dx1, g_pre_mix, sc_m, ts_big)

    dmod = jnp.concatenate([d_sh_m, d_sc_m, d_gt_m, d_sh_f, d_sc_f, d_gt_f], axis=1)
    small_g = [loss_row, dmod, dg_pre_mix, dg_post_mix, dg_q, dg_kv, dcb_mix, dg_pre_ffn, dg_post_ffn, dcb_ffn]
    adam_w_up, v_all = _adam_parts(p_up, w_up[0], m_w_up[0], v_w_up[0], 256, "adam_w_up",
                                   small=jnp.concatenate(small_g, axis=1))

    small_names = ["b_ada", "g_pre_mix", "g_post_mix", "g_q", "g_kv", "conv_b_mix", "g_pre_ffn", "g_post_ffn",
                   "conv_b_ffn"]
    small_w = [b_ada, g_pre_mix, g_post_mix, g_q, g_kv, conv_b_mix, g_pre_ffn, g_post_ffn, conv_b_ffn]
    small_m = [m_b_ada, m_g_pre_mix, m_g_post_mix, m_g_q, m_g_kv, m_conv_b_mix, m_g_pre_ffn, m_g_post_ffn, m_conv_b_ffn]
    small_v = [v_b_ada, v_g_pre_mix, v_g_post_mix, v_g_q, v_g_kv, v_conv_b_mix, v_g_pre_ffn, v_g_post_ffn, v_conv_b_ffn]
    offs = np.cumsum([0] + [g.shape[1] for g in small_g])
    g_sum, small_out = _adam_small(v_all, [int(o) for o in offs[1:-1]], small_w, small_m, small_v)
    small = [dict(zip(small_names, kind)) for kind in small_out]
    loss = g_sum[0, 0]

    dmod_sh = lax.dynamic_slice(v_all, (0, int(offs[1]) + my_i * ada_c), (N_DEV, ada_c))
    cact_t = jnp.transpose(cact)
    big = dict(
        w_ada=_adam_ada(cact_t, dmod_sh, w_ada[0], m_w_ada[0], v_w_ada[0], _tile(d, 256, SUB)),
        w_in=_adam_parts(p_in, w_in[0], m_w_in[0], v_w_in[0], 256, "adam_w_in"),
        w_uq=_adam_parts(p_uq, w_uq[0], m_w_uq[0], v_w_uq[0], 256, "adam_w_uq"),
        w_ukv=_adam_parts(p_ukv, w_ukv[0], m_w_ukv[0], v_w_ukv[0], 256, "adam_w_ukv"),
        w_o=_adam_parts(p_o, w_o[0], m_w_o[0], v_w_o[0], 128, "adam_w_o"),
        w_up=adam_w_up,
        w_down=_adam_parts(p_down, w_down[0], m_w_down[0], v_w_down[0], 176, "adam_w_down"),
        conv_w_mix=_adam_parts(p_cwm, conv_w_mix[0], m_conv_w_mix[0], v_conv_w_mix[0], 8, "adam_cw_mix"),
        conv_w_ffn=_adam_parts(p_cwf, conv_w_ffn[0], m_conv_w_ffn[0], v_conv_w_ffn[0], 8, "adam_cw_ffn"),
    )

    names = ["w_ada", "b_ada", "g_pre_mix", "g_post_mix", "w_in", "g_q", "w_uq", "g_kv", "w_ukv", "conv_w_mix",
             "conv_b_mix", "w_o", "g_pre_ffn", "g_post_ffn", "w_up", "conv_w_ffn", "conv_b_ffn", "w_down"]
    outs = [loss, grad_x[None]]
    for kind in range(4):
        for nm in names:
            outs.append(big[nm][kind][None] if nm in big else small[kind][nm])
    return tuple(outs)
```
